```python
import jax, jax.numpy as jnp
from jax import lax
import numpy as np

D_MODEL = 1024
BATCH = 16
SEQ = 256
DEPTH = 1
DEC_BATCH = 4
DEC_SEQ = 4096
PAST_LEN = 512

GRID_W = 64
RET_HEADS = 4
RET_DK = 128
RET_DV = 256
RET_QK_W = RET_HEADS * RET_DK
RET_V_W = RET_HEADS * RET_DV
RET_CHUNK = 128
ROPE_BASE = 10000.0
POOL_GROUPS = 4
POOL_CH = 128
POOL_W = POOL_GROUPS * POOL_CH
POOL_WINDOWS = (2, 4, 8, 16)
N_EXPERTS = 64
TOP_K = 8
N_EXPERT_GROUPS = 8
TOPK_GROUPS = 4
D_EXPERT = 256
ROUTED_SCALE = 2.5
EPS = 1e-6
IN_SIZES = (RET_QK_W, RET_QK_W, RET_V_W, RET_V_W, POOL_W, D_MODEL, D_MODEL)
IN_W = sum(IN_SIZES)
IN_SPLITS = [int(s) for s in np.cumsum(IN_SIZES)[:-1]]

kernel_name = "hybrid_retention_pool_moe_diffusion_step"


def rmsnorm(x, g):
    xf = x.astype(jnp.float32)
    y = xf * lax.rsqrt(jnp.mean(xf * xf, axis=-1, keepdims=True) + EPS)
    return (y * g.astype(jnp.float32)).astype(x.dtype)


def to_heads(a, dh):
    b, l, _ = a.shape
    return a.reshape(b, l, RET_HEADS, dh).transpose(0, 2, 1, 3)


def rope_1d(x, ang):
    cos = jnp.cos(ang).astype(x.dtype)
    sin = jnp.sin(ang).astype(x.dtype)
    x1, x2 = jnp.split(x, 2, axis=-1)
    return jnp.concatenate([x1 * cos - x2 * sin, x2 * cos + x1 * sin], axis=-1)


def grid_rope(x):
    l = x.shape[2]
    t = jnp.arange(l)
    row = (t // GRID_W).astype(jnp.float32)
    col = (t % GRID_W).astype(jnp.float32)
    m = RET_DK // 4
    inv = ROPE_BASE ** (-jnp.arange(m, dtype=jnp.float32) / m)
    xr, xc = jnp.split(x, 2, axis=-1)
    return jnp.concatenate([rope_1d(xr, row[:, None] * inv), rope_1d(xc, col[:, None] * inv)], axis=-1)


def retention_scan(q, k, v, log_gamma, s0):
    b, h, l, _ = q.shape
    c = RET_CHUNK
    n = l // c

    def chunks(a):
        return jnp.moveaxis(a.reshape(b, h, n, c, a.shape[-1]), 2, 0)

    i = jnp.arange(c, dtype=jnp.float32)
    lg = log_gamma[:, None]
    rel = i[:, None] - i[None, :]
    dmask = jnp.where(rel >= 0, jnp.exp(lg[:, :, None] * jnp.maximum(rel, 0.0)), 0.0)
    q_decay = jnp.exp(lg * (i + 1.0))
    k_decay = jnp.exp(lg * (c - 1.0 - i))
    chunk_decay = jnp.exp(log_gamma * c)

    def step(s, qkv):
        qc, kc, vc = qkv
        scores = jnp.einsum("bhid,bhjd->bhij", qc, kc) * dmask
        o = jnp.einsum("bhij,bhjv->bhiv", scores, vc) + jnp.einsum("bhid,bhdv->bhiv", qc, s) * q_decay[None, :, :, None]
        s_new = s * chunk_decay[None, :, None, None] + jnp.einsum("bhjd,bhjv->bhdv", kc * k_decay[None, :, :, None], vc)
        return s_new, o

    s_fin, o = lax.scan(step, s0, (chunks(q), chunks(k), chunks(v)))
    o = jnp.moveaxis(o, 0, 2).reshape(b, h, l, -1)
    return o, s_fin


def bidir_retention(q, k, v, lg_f, lg_b, s0_f, s0_b):
    o_f, s_f = retention_scan(q, k, v, lg_f, s0_f)
    flip = lambda a: jnp.flip(a, axis=2)
    o_b, s_b = retention_scan(flip(q), flip(k), flip(v), lg_b, s0_b)
    return o_f + flip(o_b), s_f, s_b


def box_mean(u, window, axis):
    l = u.shape[axis]
    cs = jnp.cumsum(u, axis=axis)
    zero = jnp.zeros_like(lax.slice_in_dim(cs, 0, 1, axis=axis))
    cs = jnp.concatenate([zero, cs], axis=axis)
    pos = jnp.arange(l)
    lo = jnp.clip(pos - window // 2, 0, l)
    hi = jnp.clip(pos + window - window // 2, 0, l)
    total = jnp.take(cs, hi, axis=axis) - jnp.take(cs, lo, axis=axis)
    cnt_shape = [1] * u.ndim
    cnt_shape[axis] = l
    return total / (hi - lo).astype(u.dtype).reshape(cnt_shape)


def pool_mixer(u, pool_w, pool_scale, on_grid):
    b, l, _ = u.shape
    uf = u.astype(jnp.float32)
    outs = []
    for g, w in enumerate(POOL_WINDOWS):
        ug = uf[..., g * POOL_CH:(g + 1) * POOL_CH]
        if on_grid:
            rows = l // GRID_W
            ug2 = ug.reshape(b, rows, GRID_W, POOL_CH)
            pooled = box_mean(box_mean(ug2, w, 2), w, 1).reshape(b, l, POOL_CH)
        else:
            pooled = box_mean(ug, w, 1)
        outs.append(pooled - ug)
    d = jnp.stack(outs, axis=2).astype(u.dtype)
    y = jnp.einsum("blgc,gce->blge", d, pool_w).reshape(b, l, POOL_W)
    return y * pool_scale


def moe(x, router_w, router_bias, exp_w_gate, exp_w_up, exp_w_down, sh_w_gate, sh_w_up, sh_w_down):
    b, l, d = x.shape
    t = x.reshape(-1, d)
    n_tok = t.shape[0]
    scores = jax.nn.sigmoid((t @ router_w).astype(jnp.float32))
    sel = scores + router_bias.astype(jnp.float32)
    grp = sel.reshape(n_tok, N_EXPERT_GROUPS, N_EXPERTS // N_EXPERT_GROUPS)
    grp_score = lax.top_k(grp, 2)[0].sum(-1)
    _, gidx = lax.top_k(grp_score, TOPK_GROUPS)
    gmask = jax.nn.one_hot(gidx, N_EXPERT_GROUPS, dtype=jnp.float32).sum(1)
    emask = jnp.repeat(gmask, N_EXPERTS // N_EXPERT_GROUPS, axis=1)
    masked = jnp.where(emask > 0, sel, -jnp.inf)
    _, eidx = lax.top_k(masked, TOP_K)
    w = jnp.take_along_axis(scores, eidx, axis=1)
    w = w / jnp.sum(w, axis=-1, keepdims=True) * ROUTED_SCALE
    combine = jnp.einsum("tk,tke->te", w, jax.nn.one_hot(eidx, N_EXPERTS, dtype=jnp.float32)).astype(x.dtype)

    def expert(acc, p):
        g_e, u_e, d_e, c_e = p
        hid = jax.nn.silu(t @ g_e) * (t @ u_e)
        return acc + c_e[:, None] * (hid @ d_e), None

    shared = (jax.nn.silu(t @ sh_w_gate) * (t @ sh_w_up)) @ sh_w_down
    out, _ = lax.scan(expert, shared, (exp_w_gate, exp_w_up, exp_w_down, combine.T))
    return out.reshape(b, l, d)


def trunk_layer(x, mod, s0_f, s0_b, on_grid, norm1_g, w_in, decay_f, decay_b, w_br_ret, pool_w, pool_scale,
                w_br_pool, w_out, norm2_g, router_w, router_bias, exp_w_gate, exp_w_up, exp_w_down,
                sh_w_gate, sh_w_up, sh_w_down):
    shift1, scale1, gate1, shift2, scale2, gate2 = jnp.split(mod, 6, axis=-1)
    b, l, _ = x.shape
    h = rmsnorm(x, norm1_g) * (1 + scale1) + shift1
    q, k, v, g_sw, u_pool, g_a, g_b = jnp.split(h @ w_in, IN_SPLITS, axis=-1)
    q = to_heads(q, RET_DK)
    k = to_heads(k, RET_DK)
    v = to_heads(v, RET_DV)
    if on_grid:
        q = grid_rope(q)
        k = grid_rope(k)
    lg_f = jnp.log1p(-jnp.exp2(-decay_f.astype(jnp.float32)))
    lg_b = jnp.log1p(-jnp.exp2(-decay_b.astype(jnp.float32)))
    o, s_f, s_b = bidir_retention(q.astype(jnp.float32), (k * RET_DK ** -0.5).astype(jnp.float32),
                                  v.astype(jnp.float32), lg_f, lg_b, s0_f, s0_b)
    o = o * lax.rsqrt(jnp.mean(o * o, axis=-1, keepdims=True) + EPS)
    o = o.transpose(0, 2, 1, 3).reshape(b, l, RET_V_W).astype(x.dtype)
    y_ret = (jax.nn.silu(g_sw) * o) @ w_br_ret
    y_pool = pool_mixer(u_pool, pool_w, pool_scale, on_grid) @ w_br_pool
    merged = jax.nn.sigmoid(g_a) * y_ret + jax.nn.sigmoid(g_b) * y_pool
    x = x + gate1 * (merged @ w_out)
    h2 = rmsnorm(x, norm2_g) * (1 + scale2) + shift2
    x = x + gate2 * moe(h2, router_w, router_bias, exp_w_gate, exp_w_up, exp_w_down, sh_w_gate, sh_w_up, sh_w_down)
    return x, s_f, s_b


def setup_inputs(seed: int = 0) -> dict:
    key = jax.random.key(seed)
    ks = jax.random.split(key, 32)
    f32 = jnp.float32
    nrm = lambda k, shape, s: jax.random.normal(k, shape, f32) * s
    d, e, fx = D_MODEL, N_EXPERTS, D_EXPERT
    st_shape = (DEC_BATCH, DEPTH, RET_HEADS, RET_DK, RET_DV)
    base_decay = 5.0 + jnp.arange(RET_HEADS, dtype=f32)
    return {
        "x_prompt": nrm(ks[0], (BATCH, SEQ, d), 1.0),
        "x_sample": nrm(ks[1], (DEC_BATCH, DEC_SEQ, d), 1.0),
        "state_ret_fwd": nrm(ks[2], st_shape, 0.5),
        "state_ret_bwd": nrm(ks[3], st_shape, 0.5),
        "c": nrm(ks[4], (DEC_BATCH, d), 1.0),
        "c_ctx": nrm(ks[5], (d,), 1.0),
        "ada_w": nrm(ks[6], (DEPTH, d, 6 * d), 0.5 * d ** -0.5),
        "ada_b": nrm(ks[7], (DEPTH, 6 * d), 0.02),
        "norm1_g": 1.0 + nrm(ks[8], (DEPTH, d), 0.05),
        "norm2_g": 1.0 + nrm(ks[9], (DEPTH, d), 0.05),
        "w_in": nrm(ks[10], (DEPTH, d, IN_W), d ** -0.5),
        "ret_decay_fwd": base_decay[None, :] + nrm(ks[11], (DEPTH, RET_HEADS), 0.1),
        "ret_decay_bwd": base_decay[None, :] + nrm(ks[12], (DEPTH, RET_HEADS), 0.1),
        "w_br_ret": nrm(ks[13], (DEPTH, RET_V_W, d), RET_V_W ** -0.5),
        "pool_w": nrm(ks[14], (DEPTH, POOL_GROUPS, POOL_CH, POOL_CH), POOL_CH ** -0.5),
        "pool_scale": 1.0 + nrm(ks[15], (DEPTH, POOL_W), 0.05),
        "w_br_pool": nrm(ks[16], (DEPTH, POOL_W, d), POOL_W ** -0.5),
        "w_out": nrm(ks[17], (DEPTH, d, d), d ** -0.5),
        "router_w": nrm(ks[18], (DEPTH, d, e), d ** -0.5),
        "router_bias": nrm(ks[19], (DEPTH, e), 0.01),
        "exp_w_gate": nrm(ks[20], (DEPTH, e, d, fx), d ** -0.5),
        "exp_w_up": nrm(ks[21], (DEPTH, e, d, fx), d ** -0.5),
        "exp_w_down": nrm(ks[22], (DEPTH, e, fx, d), fx ** -0.5),
        "sh_w_gate": nrm(ks[23], (DEPTH, d, fx), d ** -0.5),
        "sh_w_up": nrm(ks[24], (DEPTH, d, fx), d ** -0.5),
        "sh_w_down": nrm(ks[25], (DEPTH, fx, d), fx ** -0.5),
        "final_norm_g": 1.0 + nrm(ks[26], (d,), 0.05),
    }


def reference(x_prompt, x_sample, state_ret_fwd, state_ret_bwd, c, c_ctx, ada_w, ada_b, norm1_g, norm2_g, w_in,
              ret_decay_fwd, ret_decay_bwd, w_br_ret, pool_w, pool_scale, w_br_pool, w_out, router_w, router_bias,
              exp_w_gate, exp_w_up, exp_w_down, sh_w_gate, sh_w_up, sh_w_down, final_norm_g):
    def layer_args(l):
        return (norm1_g[l], w_in[l], ret_decay_fwd[l], ret_decay_bwd[l], w_br_ret[l], pool_w[l], pool_scale[l],
                w_br_pool[l], w_out[l], norm2_g[l], router_w[l], router_bias[l], exp_w_gate[l], exp_w_up[l],
                exp_w_down[l], sh_w_gate[l], sh_w_up[l], sh_w_down[l])

    xc = x_prompt
    n_req = x_prompt.shape[0]
    states_f, states_b = [], []
    for l in range(DEPTH):
        mod = jax.nn.silu(c_ctx) @ ada_w[l] + ada_b[l]
        zero = jnp.zeros((n_req, RET_HEADS, RET_DK, RET_DV), jnp.float32)
        xc, s_f, s_b = trunk_layer(xc, mod, zero, zero, False, *layer_args(l))
        states_f.append(s_f)
        states_b.append(s_b)
    y_prompt = rmsnorm(xc, final_norm_g)
    new_state_ret_fwd = jnp.stack(states_f, axis=1).astype(x_prompt.dtype)
    new_state_ret_bwd = jnp.stack(states_b, axis=1).astype(x_prompt.dtype)

    xs = x_sample
    for l in range(DEPTH):
        mod = (jax.nn.silu(c) @ ada_w[l] + ada_b[l])[:, None, :]
        xs, _, _ = trunk_layer(xs, mod, state_ret_fwd[:, l].astype(jnp.float32),
                               state_ret_bwd[:, l].astype(jnp.float32), True, *layer_args(l))
    y_sample = rmsnorm(xs, final_norm_g)
    return (y_prompt, y_sample, new_state_ret_fwd, new_state_ret_bwd)
```

```python
import functools

import jax
import jax.numpy as jnp
from jax import lax
from jax.experimental import pallas as pl
from jax.experimental.pallas import tpu as pltpu

D_MODEL = 1024
GRID_W = 64
RET_HEADS = 4
RET_DK = 128
RET_DV = 256
RET_QK_W = RET_HEADS * RET_DK
RET_V_W = RET_HEADS * RET_DV
RET_CHUNK = 128
ROPE_BASE = 10000.0
POOL_GROUPS = 4
POOL_CH = 128
POOL_W = POOL_GROUPS * POOL_CH
POOL_WINDOWS = (2, 4, 8, 16)
N_EXPERTS = 64
TOP_K = 8
N_EXPERT_GROUPS = 8
GROUP_SIZE = N_EXPERTS // N_EXPERT_GROUPS
TOPK_GROUPS = 4
D_EXPERT = 256
ROUTED_SCALE = 2.5
EPS = 1e-6
IN_SIZES = (RET_QK_W, RET_QK_W, RET_V_W, RET_V_W, POOL_W, D_MODEL, D_MODEL)
IN_OFFS = tuple(sum(IN_SIZES[:i]) for i in range(len(IN_SIZES) + 1))
IN_W = IN_OFFS[-1]

LANES = 128
VMEM_LIMIT = 56 << 20

F32 = jnp.float32
BF16 = jnp.bfloat16


def _cparams(*sem):
    return pltpu.CompilerParams(dimension_semantics=sem, vmem_limit_bytes=VMEM_LIMIT)


def _dot(a, b):
    return jnp.dot(a, b, preferred_element_type=F32)


def _silu(x):
    return x * jax.nn.sigmoid(x)


def _rms_mod(x, g, scale, shift):
    y = x * lax.rsqrt(jnp.mean(x * x, axis=-1, keepdims=True) + EPS)
    return (y * g) * (1.0 + scale) + shift


def _ada_kernel(c_ref, w_ref, b_ref, o_ref):
    c = c_ref[...]
    o_ref[...] = jnp.dot(_silu(c), w_ref[...], preferred_element_type=F32,
                         precision=lax.Precision.HIGHEST) + b_ref[...]


def _ada(c_rows, ada_w, ada_b):
    r = c_rows.shape[0]
    n = ada_w.shape[1]
    tn = D_MODEL
    return pl.pallas_call(
        _ada_kernel,
        grid=(n // tn,),
        in_specs=[pl.BlockSpec((r, D_MODEL), lambda j: (0, 0)),
                  pl.BlockSpec((D_MODEL, tn), lambda j: (0, j)),
                  pl.BlockSpec((1, tn), lambda j: (0, j))],
        out_specs=pl.BlockSpec((r, tn), lambda j: (0, j)),
        out_shape=jax.ShapeDtypeStruct((r, n), F32),
        compiler_params=_cparams("parallel"),
        name="ada_mod",
    )(c_rows, ada_w, ada_b.reshape(1, n))


def _inproj_kernel(x_ref, mod_ref, g_ref, w_ref, cos_ref, sin_ref,
                   q_ref, k_ref, v_ref, gsw_ref, up_ref, ga_ref, gb_ref, *, on_grid):
    x = x_ref[...]
    h = _rms_mod(x, g_ref[...], mod_ref[0, 1:2, :], mod_ref[0, 0:1, :]).astype(BF16)

    def seg(i):
        return _dot(h, w_ref[:, IN_OFFS[i]:IN_OFFS[i + 1]])

    q = seg(0)
    k = seg(1)
    if on_grid:
        cos = jnp.concatenate([cos_ref[...]] * RET_HEADS, axis=1)
        sin = jnp.concatenate([sin_ref[...]] * RET_HEADS, axis=1)
        lane = lax.broadcasted_iota(jnp.int32, q.shape, 1)
        first = (lane & 63) < 32

        def rope(a):
            up = pltpu.roll(a, RET_QK_W - 32, axis=1)
            dn = pltpu.roll(a, 32, axis=1)
            return a * cos + jnp.where(first, up, dn) * sin

        q = rope(q)
        k = rope(k)
    q_ref[...] = q.astype(BF16)
    k_ref[...] = (k * (RET_DK ** -0.5)).astype(BF16)
    v_ref[...] = seg(2).astype(BF16)
    gsw_ref[...] = seg(3).astype(BF16)
    up_ref[...] = seg(4).astype(BF16)
    ga_ref[...] = seg(5).astype(BF16)
    gb_ref[...] = seg(6).astype(BF16)


def _inproj(x, mods, norm_g, w_in, cos_t, sin_t, *, tokens_per_mod, seq_len, on_grid, tm):
    t = x.shape[0]
    tiles_per_mod = tokens_per_mod // tm
    tiles_per_seq = seq_len // tm
    widths = IN_SIZES
    out_shape = [jax.ShapeDtypeStruct((t, w), BF16) for w in widths]
    out_specs = [pl.BlockSpec((tm, w), lambda i: (i, 0)) for w in widths]
    return pl.pallas_call(
        functools.partial(_inproj_kernel, on_grid=on_grid),
        grid=(t // tm,),
        in_specs=[pl.BlockSpec((tm, D_MODEL), lambda i: (i, 0)),
                  pl.BlockSpec((1, 6, D_MODEL), lambda i: (i // tiles_per_mod, 0, 0)),
                  pl.BlockSpec((1, D_MODEL), lambda i: (0, 0)),
                  pl.BlockSpec((D_MODEL, IN_W), lambda i: (0, 0)),
                  pl.BlockSpec((tm, RET_DK), lambda i: (i % tiles_per_seq, 0)),
                  pl.BlockSpec((tm, RET_DK), lambda i: (i % tiles_per_seq, 0))],
        out_specs=out_specs,
        out_shape=out_shape,
        compiler_params=_cparams("parallel"),
        name="inproj_grid" if on_grid else "inproj_seq",
    )(x, mods, norm_g, w_in, cos_t, sin_t)


def _rope_tables(seq_len):
    t = jnp.arange(seq_len)
    row = (t // GRID_W).astype(F32)
    col = (t % GRID_W).astype(F32)
    m = RET_DK // 4
    inv = ROPE_BASE ** (-jnp.arange(m, dtype=F32) / m)
    ar = row[:, None] * inv
    ac = col[:, None] * inv
    cos = jnp.concatenate([jnp.cos(ar), jnp.cos(ar), jnp.cos(ac), jnp.cos(ac)], axis=1)
    sin = jnp.concatenate([-jnp.sin(ar), jnp.sin(ar), -jnp.sin(ac), jnp.sin(ac)], axis=1)
    return cos.astype(F32), sin.astype(F32)


def _ret_kernel(dec_ref, q_ref, k_ref, v_ref, g_ref, s0f_ref, s0b_ref,
                z_ref, sf_ref, sb_ref, sbin_ref, *, n_chunks):
    c = RET_CHUNK
    h = pl.program_id(1)

    def log_gamma(d, shape):
        return jnp.log1p(-jnp.exp2(-jnp.full(shape, d, F32)))

    dec_f = dec_ref[0, h]
    dec_b = dec_ref[1, h]
    ii = lax.broadcasted_iota(jnp.int32, (c, c), 0)
    jj = lax.broadcasted_iota(jnp.int32, (c, c), 1)
    rel = (ii - jj).astype(F32)
    dmask = (jnp.where(rel >= 0, jnp.exp(log_gamma(dec_f, (c, c)) * jnp.maximum(rel, 0.0)), 0.0)
             + jnp.where(rel <= 0, jnp.exp(log_gamma(dec_b, (c, c)) * jnp.maximum(-rel, 0.0)), 0.0))
    iv = lax.broadcasted_iota(jnp.int32, (c, RET_DV), 0).astype(F32)
    ik = lax.broadcasted_iota(jnp.int32, (c, RET_DK), 0).astype(F32)
    lgf_v = log_gamma(dec_f, (c, RET_DV))
    lgb_v = log_gamma(dec_b, (c, RET_DV))
    lgf_k = log_gamma(dec_f, (c, RET_DK))
    lgb_k = log_gamma(dec_b, (c, RET_DK))
    qdec_f = jnp.exp(lgf_v * (iv + 1.0))
    qdec_b = jnp.exp(lgb_v * (c - iv))
    kdec_f = jnp.exp(lgf_k * (c - 1.0 - ik))
    kdec_b = jnp.exp(lgb_k * ik)
    cdec_f = jnp.exp(log_gamma(dec_f, (RET_DK, RET_DV)) * c)
    cdec_b = jnp.exp(log_gamma(dec_b, (RET_DK, RET_DV)) * c)

    def kv_update(kc, vc, kdec):
        kd = (kc.astype(F32) * kdec).astype(BF16)
        return lax.dot_general(kd, vc, (((0,), (0,)), ((), ())), preferred_element_type=F32)

    def rows(ci):
        return pl.ds(pl.multiple_of(ci * c, c), c)

    def bwd(t, s):
        ci = n_chunks - 1 - t
        sbin_ref[ci] = s
        return s * cdec_b + kv_update(k_ref[rows(ci), :], v_ref[rows(ci), :], kdec_b)

    sb_ref[...] = lax.fori_loop(0, n_chunks, bwd, s0b_ref[...])

    def fwd(ci, s):
        r = rows(ci)
        qc = q_ref[r, :]
        kc = k_ref[r, :]
        vc = v_ref[r, :]
        sc = lax.dot_general(qc, kc, (((1,), (1,)), ((), ())), preferred_element_type=F32)
        o = _dot((sc * dmask).astype(BF16), vc)
        o = o + _dot(qc, s.astype(BF16)) * qdec_f
        o = o + _dot(qc, sbin_ref[ci].astype(BF16)) * qdec_b
        o = o * lax.rsqrt(jnp.mean(o * o, axis=-1, keepdims=True) + EPS)
        g = g_ref[r, :].astype(F32)
        z_ref[r, :] = (_silu(g) * o).astype(BF16)
        return s * cdec_f + kv_update(kc, vc, kdec_f)

    sf_ref[...] = lax.fori_loop(0, n_chunks, fwd, s0f_ref[...])


def _retention(q, k, v, gsw, dec, s0f, s0b, *, batch, seq_len):
    n_chunks = seq_len // RET_CHUNK
    t = batch * seq_len
    st_spec = pl.BlockSpec((None, None, RET_DK, RET_DV), lambda b, h: (b, h, 0, 0))
    st_shape = jax.ShapeDtypeStruct((batch, RET_HEADS, RET_DK, RET_DV), F32)
    return pl.pallas_call(
        functools.partial(_ret_kernel, n_chunks=n_chunks),
        grid=(batch, RET_HEADS),
        in_specs=[pl.BlockSpec(memory_space=pltpu.SMEM),
                  pl.BlockSpec((seq_len, RET_DK), lambda b, h: (b, h)),
                  pl.BlockSpec((seq_len, RET_DK), lambda b, h: (b, h)),
                  pl.BlockSpec((seq_len, RET_DV), lambda b, h: (b, h)),
                  pl.BlockSpec((seq_len, RET_DV), lambda b, h: (b, h)),
                  st_spec, st_spec],
        out_specs=[pl.BlockSpec((seq_len, RET_DV), lambda b, h: (b, h)), st_spec, st_spec],
        out_shape=[jax.ShapeDtypeStruct((t, RET_V_W), BF16), st_shape, st_shape],
        scratch_shapes=[pltpu.VMEM((n_chunks, RET_DK, RET_DV), F32)],
        compiler_params=_cparams("parallel", "parallel"),
        name=f"retention_l{seq_len}",
    )(dec, q, k, v, gsw, s0f, s0b)


def _pool_kernel(u_ref, w_ref, sc_ref, o_ref, *, seq_len, on_grid):
    tok = lax.broadcasted_iota(jnp.int32, (seq_len, POOL_CH), 0)

    def shift(a, s, stride, pos, width):
        y = pltpu.roll(a, (-s * stride) % seq_len, axis=0)
        ok = (pos < width - s) if s > 0 else (pos >= -s)
        return jnp.where(ok, y, 0.0)

    def box_mean(a, window, stride, pos, width):
        half = window // 2
        fw = a
        bw = shift(a, -1, stride, pos, width)
        m = 1
        while m < half:
            fw = fw + shift(fw, m, stride, pos, width)
            bw = bw + shift(bw, -m, stride, pos, width)
            m *= 2
        cnt = jnp.minimum(pos + half, width) - jnp.maximum(pos - half, 0)
        return (fw + bw) / cnt.astype(F32)

    for g, window in enumerate(POOL_WINDOWS):
        cols = slice(g * POOL_CH, (g + 1) * POOL_CH)
        ug = u_ref[:, cols].astype(F32)
        if on_grid:
            pooled = box_mean(ug, window, 1, tok & (GRID_W - 1), GRID_W)
            pooled = box_mean(pooled, window, GRID_W, tok >> 6, seq_len // GRID_W)
        else:
            pooled = box_mean(ug, window, 1, tok, seq_len)
        d = (pooled - ug).astype(BF16)
        o_ref[:, cols] = (_dot(d, w_ref[g]) * sc_ref[:, cols]).astype(BF16)


def _pool(u, pool_w, pool_scale, *, batch, seq_len, on_grid):
    t = batch * seq_len
    return pl.pallas_call(
        functools.partial(_pool_kernel, seq_len=seq_len, on_grid=on_grid),
        grid=(batch,),
        in_specs=[pl.BlockSpec((seq_len, POOL_W), lambda b: (b, 0)),
                  pl.BlockSpec((POOL_GROUPS, POOL_CH, POOL_CH), lambda b: (0, 0, 0)),
                  pl.BlockSpec((1, POOL_W), lambda b: (0, 0))],
        out_specs=pl.BlockSpec((seq_len, POOL_W), lambda b: (b, 0)),
        out_shape=jax.ShapeDtypeStruct((t, POOL_W), BF16),
        compiler_params=_cparams("parallel"),
        name=f"pool_l{seq_len}",
    )(u, pool_w, pool_scale)


def _merge_kernel(x_ref, z_ref, p_ref, ga_ref, gb_ref, mod_ref, g2_ref, wr_ref, wp_ref, wo_ref,
                  x1_ref, h2_ref):
    y_ret = _dot(z_ref[...], wr_ref[...])
    y_pool = _dot(p_ref[...], wp_ref[...])
    merged = (jax.nn.sigmoid(ga_ref[...].astype(F32)) * y_ret
              + jax.nn.sigmoid(gb_ref[...].astype(F32)) * y_pool)
    x1 = x_ref[...] + mod_ref[0, 2:3, :] * _dot(merged.astype(BF16), wo_ref[...])
    x1_ref[...] = x1
    h2_ref[...] = _rms_mod(x1, g2_ref[...], mod_ref[0, 4:5, :], mod_ref[0, 3:4, :]).astype(BF16)


def _merge(x, z, p, ga, gb, mods, norm2_g, w_br_ret, w_br_pool, w_out, *, tokens_per_mod, tm):
    t = x.shape[0]
    tiles_per_mod = tokens_per_mod // tm
    row = lambda w: pl.BlockSpec((tm, w), lambda i: (i, 0))
    full = lambda a: pl.BlockSpec(a.shape, lambda i: (0,) * a.ndim)
    return pl.pallas_call(
        _merge_kernel,
        grid=(t // tm,),
        in_specs=[row(D_MODEL), row(RET_V_W), row(POOL_W), row(D_MODEL), row(D_MODEL),
                  pl.BlockSpec((1, 6, D_MODEL), lambda i: (i // tiles_per_mod, 0, 0)),
                  full(norm2_g), full(w_br_ret), full(w_br_pool), full(w_out)],
        out_specs=[row(D_MODEL), row(D_MODEL)],
        out_shape=[jax.ShapeDtypeStruct((t, D_MODEL), F32), jax.ShapeDtypeStruct((t, D_MODEL), BF16)],
        compiler_params=_cparams("parallel"),
        name="merge",
    )(x, z, p, ga, gb, mods, norm2_g, w_br_ret, w_br_pool, w_out)


def _route_kernel(h_ref, rw_ref, bias_ref, c_ref):
    e = N_EXPERTS
    tm = h_ref.shape[0]
    neg = -jnp.inf
    logits = lax.dot_general(rw_ref[...], h_ref[...], (((1,), (1,)), ((), ())),
                             preferred_element_type=F32)[:e]
    scores = jax.nn.sigmoid(logits)
    sel = scores + bias_ref[:e, 0:1]
    e_idx = lax.broadcasted_iota(jnp.int32, (e, tm), 0)

    grp = sel.reshape(N_EXPERT_GROUPS, GROUP_SIZE, tm)
    m_idx = lax.broadcasted_iota(jnp.int32, grp.shape, 1)
    m1 = jnp.max(grp, axis=1, keepdims=True)
    first = jnp.min(jnp.where(grp == m1, m_idx, GROUP_SIZE), axis=1, keepdims=True)
    m2 = jnp.max(jnp.where(m_idx == first, neg, grp), axis=1, keepdims=True)
    gscore = (m1 + m2).reshape(N_EXPERT_GROUPS, tm)

    g_idx = lax.broadcasted_iota(jnp.int32, gscore.shape, 0)
    rank = jnp.zeros(gscore.shape, jnp.int32)
    for g in range(N_EXPERT_GROUPS):
        other = gscore[g:g + 1, :]
        beats = jnp.where(other > gscore, 1, jnp.where(other == gscore, (g_idx > g).astype(jnp.int32), 0))
        rank = rank + beats
    gkeep = (rank < TOPK_GROUPS).astype(F32)
    ekeep = jnp.broadcast_to(gkeep.reshape(N_EXPERT_GROUPS, 1, tm), grp.shape).reshape(e, tm)
    masked = jnp.where(ekeep > 0, sel, neg)

    chosen = jnp.zeros((e, tm), F32)
    for _ in range(TOP_K):
        m = jnp.max(masked, axis=0, keepdims=True)
        pick = jnp.min(jnp.where(masked == m, e_idx, e), axis=0, keepdims=True)
        hit = e_idx == pick
        chosen = jnp.where(hit, 1.0, chosen)
        masked = jnp.where(hit, neg, masked)

    w = scores * chosen
    comb = w / jnp.sum(w, axis=0, keepdims=True) * ROUTED_SCALE
    comb = jnp.concatenate([comb, jnp.zeros((LANES - e, tm), F32)], axis=0)
    c_ref[...] = comb.T


def _route(h2, router_wt, bias_col, *, tm):
    t = h2.shape[0]
    return pl.pallas_call(
        _route_kernel,
        grid=(t // tm,),
        in_specs=[pl.BlockSpec((tm, D_MODEL), lambda i: (i, 0)),
                  pl.BlockSpec((LANES, D_MODEL), lambda i: (0, 0)),
                  pl.BlockSpec((LANES, 1), lambda i: (0, 0))],
        out_specs=pl.BlockSpec((tm, LANES), lambda i: (i, 0)),
        out_shape=jax.ShapeDtypeStruct((t, LANES), F32),
        compiler_params=_cparams("parallel"),
        name="route",
    )(h2, router_wt, bias_col)


def _moe_kernel(h_ref, c_ref, x1_ref, mod_ref, fg_ref, wg_ref, wu_ref, wd_ref,
                sg_ref, su_ref, sd_ref, y_ref, acc_ref):
    e = pl.program_id(1)
    h = h_ref[...]

    @pl.when(e == 0)
    def _():
        hid = _silu(_dot(h, sg_ref[...])) * _dot(h, su_ref[...])
        acc_ref[...] = _dot(hid.astype(BF16), sd_ref[...])

    lane = lax.broadcasted_iota(jnp.int32, c_ref.shape, 1)
    ce = jnp.sum(jnp.where(lane == e, c_ref[...], 0.0), axis=1, keepdims=True)
    hid = _silu(_dot(h, wg_ref[...])) * _dot(h, wu_ref[...])
    acc_ref[...] += _dot((hid * ce).astype(BF16), wd_ref[...])

    @pl.when(e == N_EXPERTS - 1)
    def _():
        x2 = x1_ref[...] + mod_ref[0, 5:6, :] * acc_ref[...]
        y_ref[...] = x2 * lax.rsqrt(jnp.mean(x2 * x2, axis=-1, keepdims=True) + EPS) * fg_ref[...]


def _moe(h2, comb, x1, mods, final_g, wg, wu, wd, sg, su, sd, *, tokens_per_mod, tm):
    t = h2.shape[0]
    tiles_per_mod = tokens_per_mod // tm
    row = lambda w: pl.BlockSpec((tm, w), lambda i, e: (i, 0))
    full = lambda a: pl.BlockSpec(a.shape, lambda i, e: (0,) * a.ndim)
    return pl.pallas_call(
        _moe_kernel,
        grid=(t // tm, N_EXPERTS),
        in_specs=[row(D_MODEL), row(LANES), row(D_MODEL),
                  pl.BlockSpec((1, 6, D_MODEL), lambda i, e: (i // tiles_per_mod, 0, 0)),
                  full(final_g),
                  pl.BlockSpec((None, D_MODEL, D_EXPERT), lambda i, e: (e, 0, 0)),
                  pl.BlockSpec((None, D_MODEL, D_EXPERT), lambda i, e: (e, 0, 0)),
                  pl.BlockSpec((None, D_EXPERT, D_MODEL), lambda i, e: (e, 0, 0)),
                  full(sg), full(su), full(sd)],
        out_specs=row(D_MODEL),
        out_shape=jax.ShapeDtypeStruct((t, D_MODEL), F32),
        scratch_shapes=[pltpu.VMEM((tm, D_MODEL), F32)],
        compiler_params=_cparams("parallel", "arbitrary"),
        name="moe_dense",
    )(h2, comb, x1, mods, final_g, wg, wu, wd, sg, su, sd)


def _trunk(x, mods, s0f, s0b, w, *, batch, seq_len, on_grid):
    t = batch * seq_len
    tokens_per_mod = t // mods.shape[0]
    cos_t, sin_t = _rope_tables(seq_len)
    q, k, v, gsw, up, ga, gb = _inproj(x, mods, w["norm1_g"], w["w_in"], cos_t, sin_t,
                                       tokens_per_mod=tokens_per_mod, seq_len=seq_len,
                                       on_grid=on_grid, tm=256)
    z, s_f, s_b = _retention(q, k, v, gsw, w["dec"], s0f, s0b, batch=batch, seq_len=seq_len)
    p = _pool(up, w["pool_w"], w["pool_scale"], batch=batch, seq_len=seq_len, on_grid=on_grid)
    x1, h2 = _merge(x, z, p, ga, gb, mods, w["norm2_g"], w["w_br_ret"], w["w_br_pool"], w["w_out"],
                    tokens_per_mod=tokens_per_mod, tm=256)
    comb = _route(h2, w["router_wt"], w["router_bias"], tm=512)
    y = _moe(h2, comb, x1, mods, w["final_g"], w["exp_w_gate"], w["exp_w_up"], w["exp_w_down"],
             w["sh_w_gate"], w["sh_w_up"], w["sh_w_down"], tokens_per_mod=tokens_per_mod, tm=1024)
    return y, s_f, s_b


def kernel(x_prompt, x_sample, state_ret_fwd, state_ret_bwd, c, c_ctx, ada_w, ada_b, norm1_g, norm2_g, w_in,
           ret_decay_fwd, ret_decay_bwd, w_br_ret, pool_w, pool_scale, w_br_pool, w_out, router_w, router_bias,
           exp_w_gate, exp_w_up, exp_w_down, sh_w_gate, sh_w_up, sh_w_down, final_norm_g):
    n_req, seq, d = x_prompt.shape
    n_dec, dec_seq, _ = x_sample.shape
    depth = ada_w.shape[0]
    assert depth == 1 and d == D_MODEL

    xc = x_prompt.reshape(n_req * seq, d)
    xs = x_sample.reshape(n_dec * dec_seq, d)
    zero_state = jnp.zeros((n_req, RET_HEADS, RET_DK, RET_DV), F32)
    new_f, new_b = [], []
    for l in range(depth):
        c_rows = jnp.concatenate([c_ctx[None, :], c, jnp.zeros((8 - 1 - n_dec, d), F32)], axis=0)
        mods = _ada(c_rows, ada_w[l], ada_b[l]).reshape(8, 6, d)
        pad_rows = LANES - N_EXPERTS
        w = dict(
            norm1_g=norm1_g[l].reshape(1, d), norm2_g=norm2_g[l].reshape(1, d),
            final_g=final_norm_g.reshape(1, d),
            w_in=w_in[l].astype(BF16),
            dec=jnp.stack([ret_decay_fwd[l], ret_decay_bwd[l]]).astype(F32),
            w_br_ret=w_br_ret[l].astype(BF16), pool_w=pool_w[l].astype(BF16),
            pool_scale=pool_scale[l].reshape(1, POOL_W), w_br_pool=w_br_pool[l].astype(BF16),
            w_out=w_out[l].astype(BF16),
            router_wt=jnp.pad(router_w[l].T, ((0, pad_rows), (0, 0))).astype(BF16),
            router_bias=jnp.pad(router_bias[l].astype(F32).reshape(N_EXPERTS, 1), ((0, pad_rows), (0, 0))),
            exp_w_gate=exp_w_gate[l].astype(BF16), exp_w_up=exp_w_up[l].astype(BF16),
            exp_w_down=exp_w_down[l].astype(BF16), sh_w_gate=sh_w_gate[l].astype(BF16),
            sh_w_up=sh_w_up[l].astype(BF16), sh_w_down=sh_w_down[l].astype(BF16),
        )
        xc, s_f, s_b = _trunk(xc, mods[0:1], zero_state, zero_state, w,
                              batch=n_req, seq_len=seq, on_grid=False)
        new_f.append(s_f)
        new_b.append(s_b)
        xs, _, _ = _trunk(xs, mods[1:1 + n_dec], state_ret_fwd[:, l].astype(F32),
                          state_ret_bwd[:, l].astype(F32), w,
                          batch=n_dec, seq_len=dec_seq, on_grid=True)
    y_prompt = xc.reshape(n_req, seq, d)
    y_sample = xs.reshape(n_dec, dec_seq, d)
    return (y_prompt, y_sample, jnp.stack(new_f, axis=1).astype(x_prompt.dtype),
            jnp.stack(new_b, axis=1).astype(x_prompt.dtype))
```

```python
import functools

import jax
import jax.numpy as jnp
from jax import lax
from jax.experimental import pallas as pl
from jax.experimental.pallas import tpu as pltpu
from jax.experimental.pallas import tpu_sc as plsc

D_MODEL = 1024
GRID_W = 64
RET_HEADS = 4
RET_DK = 128
RET_DV = 256
RET_QK_W = RET_HEADS * RET_DK
RET_V_W = RET_HEADS * RET_DV
RET_CHUNK = 128
ROPE_BASE = 10000.0
POOL_GROUPS = 4
POOL_CH = 128
POOL_W = POOL_GROUPS * POOL_CH
POOL_WINDOWS = (2, 4, 8, 16)
N_EXPERTS = 64
TOP_K = 8
N_EXPERT_GROUPS = 8
GROUP_SIZE = N_EXPERTS // N_EXPERT_GROUPS
TOPK_GROUPS = 4
D_EXPERT = 256
ROUTED_SCALE = 2.5
EPS = 1e-6
IN_SIZES = (RET_QK_W, RET_QK_W, RET_V_W, RET_V_W, POOL_W, D_MODEL, D_MODEL)
IN_OFFS = tuple(sum(IN_SIZES[:i]) for i in range(len(IN_SIZES) + 1))
IN_W = IN_OFFS[-1]

LANES = 128
VMEM_LIMIT = 56 << 20
N_PIECES = D_MODEL // 2 // LANES
GROUP_TILE = 256
SC_CHUNK = 128

F32 = jnp.float32
BF16 = jnp.bfloat16
I32 = jnp.int32
U32 = jnp.uint32


def _cparams(*sem):
    return pltpu.CompilerParams(dimension_semantics=sem, vmem_limit_bytes=VMEM_LIMIT)


def _dot(a, b):
    return jnp.dot(a, b, preferred_element_type=F32)


def _silu(x):
    return x * jax.nn.sigmoid(x)


def _rms_mod(x, g, scale, shift):
    y = x * lax.rsqrt(jnp.mean(x * x, axis=-1, keepdims=True) + EPS)
    return (y * g) * (1.0 + scale) + shift


def _ada_kernel(c_ref, w_ref, b_ref, o_ref):
    c = c_ref[...]
    o_ref[...] = jnp.dot(_silu(c), w_ref[...], preferred_element_type=F32,
                         precision=lax.Precision.HIGHEST) + b_ref[...]


def _ada(c_rows, ada_w, ada_b):
    r = c_rows.shape[0]
    n = ada_w.shape[1]
    tn = D_MODEL
    return pl.pallas_call(
        _ada_kernel,
        grid=(n // tn,),
        in_specs=[pl.BlockSpec((r, D_MODEL), lambda j: (0, 0)),
                  pl.BlockSpec((D_MODEL, tn), lambda j: (0, j)),
                  pl.BlockSpec((1, tn), lambda j: (0, j))],
        out_specs=pl.BlockSpec((r, tn), lambda j: (0, j)),
        out_shape=jax.ShapeDtypeStruct((r, n), F32),
        compiler_params=_cparams("parallel"),
        name="ada_mod",
    )(c_rows, ada_w, ada_b.reshape(1, n))


def _inproj_kernel(x_ref, mod_ref, g_ref, w_ref, cos_ref, sin_ref,
                   q_ref, k_ref, v_ref, gsw_ref, up_ref, ga_ref, gb_ref, *, on_grid):
    x = x_ref[...]
    h = _rms_mod(x, g_ref[...], mod_ref[0, 1:2, :], mod_ref[0, 0:1, :]).astype(BF16)

    def seg(i):
        return _dot(h, w_ref[:, IN_OFFS[i]:IN_OFFS[i + 1]])

    q = seg(0)
    k = seg(1)
    if on_grid:
        cos = jnp.concatenate([cos_ref[...]] * RET_HEADS, axis=1)
        sin = jnp.concatenate([sin_ref[...]] * RET_HEADS, axis=1)
        lane = lax.broadcasted_iota(jnp.int32, q.shape, 1)
        first = (lane & 63) < 32

        def rope(a):
            up = pltpu.roll(a, RET_QK_W - 32, axis=1)
            dn = pltpu.roll(a, 32, axis=1)
            return a * cos + jnp.where(first, up, dn) * sin

        q = rope(q)
        k = rope(k)
    q_ref[...] = q.astype(BF16)
    k_ref[...] = (k * (RET_DK ** -0.5)).astype(BF16)
    v_ref[...] = seg(2).astype(BF16)
    gsw_ref[...] = seg(3).astype(BF16)
    up_ref[...] = seg(4).astype(BF16)
    ga_ref[...] = seg(5).astype(BF16)
    gb_ref[...] = seg(6).astype(BF16)


def _inproj(x, mods, norm_g, w_in, cos_t, sin_t, *, tokens_per_mod, seq_len, on_grid, tm):
    t = x.shape[0]
    tiles_per_mod = tokens_per_mod // tm
    tiles_per_seq = seq_len // tm
    widths = IN_SIZES
    out_shape = [jax.ShapeDtypeStruct((t, w), BF16) for w in widths]
    out_specs = [pl.BlockSpec((tm, w), lambda i: (i, 0)) for w in widths]
    return pl.pallas_call(
        functools.partial(_inproj_kernel, on_grid=on_grid),
        grid=(t // tm,),
        in_specs=[pl.BlockSpec((tm, D_MODEL), lambda i: (i, 0)),
                  pl.BlockSpec((1, 6, D_MODEL), lambda i: (i // tiles_per_mod, 0, 0)),
                  pl.BlockSpec((1, D_MODEL), lambda i: (0, 0)),
                  pl.BlockSpec((D_MODEL, IN_W), lambda i: (0, 0)),
                  pl.BlockSpec((tm, RET_DK), lambda i: (i % tiles_per_seq, 0)),
                  pl.BlockSpec((tm, RET_DK), lambda i: (i % tiles_per_seq, 0))],
        out_specs=out_specs,
        out_shape=out_shape,
        compiler_params=_cparams("parallel"),
        name="inproj_grid" if on_grid else "inproj_seq",
    )(x, mods, norm_g, w_in, cos_t, sin_t)


def _rope_tables(seq_len):
    t = jnp.arange(seq_len)
    row = (t // GRID_W).astype(F32)
    col = (t % GRID_W).astype(F32)
    m = RET_DK // 4
    inv = ROPE_BASE ** (-jnp.arange(m, dtype=F32) / m)
    ar = row[:, None] * inv
    ac = col[:, None] * inv
    cos = jnp.concatenate([jnp.cos(ar), jnp.cos(ar), jnp.cos(ac), jnp.cos(ac)], axis=1)
    sin = jnp.concatenate([-jnp.sin(ar), jnp.sin(ar), -jnp.sin(ac), jnp.sin(ac)], axis=1)
    return cos.astype(F32), sin.astype(F32)


def _ret_kernel(dec_ref, q_ref, k_ref, v_ref, g_ref, s0f_ref, s0b_ref,
                z_ref, sf_ref, sb_ref, sbin_ref, *, n_chunks):
    c = RET_CHUNK
    h = pl.program_id(1)

    def log_gamma(d, shape):
        return jnp.log1p(-jnp.exp2(-jnp.full(shape, d, F32)))

    dec_f = dec_ref[0, h]
    dec_b = dec_ref[1, h]
    ii = lax.broadcasted_iota(jnp.int32, (c, c), 0)
    jj = lax.broadcasted_iota(jnp.int32, (c, c), 1)
    rel = (ii - jj).astype(F32)
    dmask = (jnp.where(rel >= 0, jnp.exp(log_gamma(dec_f, (c, c)) * jnp.maximum(rel, 0.0)), 0.0)
             + jnp.where(rel <= 0, jnp.exp(log_gamma(dec_b, (c, c)) * jnp.maximum(-rel, 0.0)), 0.0))
    iv = lax.broadcasted_iota(jnp.int32, (c, RET_DV), 0).astype(F32)
    ik = lax.broadcasted_iota(jnp.int32, (c, RET_DK), 0).astype(F32)
    lgf_v = log_gamma(dec_f, (c, RET_DV))
    lgb_v = log_gamma(dec_b, (c, RET_DV))
    lgf_k = log_gamma(dec_f, (c, RET_DK))
    lgb_k = log_gamma(dec_b, (c, RET_DK))
    qdec_f = jnp.exp(lgf_v * (iv + 1.0))
    qdec_b = jnp.exp(lgb_v * (c - iv))
    kdec_f = jnp.exp(lgf_k * (c - 1.0 - ik))
    kdec_b = jnp.exp(lgb_k * ik)
    cdec_f = jnp.exp(log_gamma(dec_f, (RET_DK, RET_DV)) * c)
    cdec_b = jnp.exp(log_gamma(dec_b, (RET_DK, RET_DV)) * c)

    def kv_update(kc, vc, kdec):
        kd = (kc.astype(F32) * kdec).astype(BF16)
        return lax.dot_general(kd, vc, (((0,), (0,)), ((), ())), preferred_element_type=F32)

    def rows(ci):
        return pl.ds(pl.multiple_of(ci * c, c), c)

    def bwd(t, s):
        ci = n_chunks - 1 - t
        sbin_ref[ci] = s
        return s * cdec_b + kv_update(k_ref[rows(ci), :], v_ref[rows(ci), :], kdec_b)

    sb_ref[...] = lax.fori_loop(0, n_chunks, bwd, s0b_ref[...])

    def fwd(ci, s):
        r = rows(ci)
        qc = q_ref[r, :]
        kc = k_ref[r, :]
        vc = v_ref[r, :]
        sc = lax.dot_general(qc, kc, (((1,), (1,)), ((), ())), preferred_element_type=F32)
        o = _dot((sc * dmask).astype(BF16), vc)
        o = o + _dot(qc, s.astype(BF16)) * qdec_f
        o = o + _dot(qc, sbin_ref[ci].astype(BF16)) * qdec_b
        o = o * lax.rsqrt(jnp.mean(o * o, axis=-1, keepdims=True) + EPS)
        g = g_ref[r, :].astype(F32)
        z_ref[r, :] = (_silu(g) * o).astype(BF16)
        return s * cdec_f + kv_update(kc, vc, kdec_f)

    sf_ref[...] = lax.fori_loop(0, n_chunks, fwd, s0f_ref[...])


def _retention(q, k, v, gsw, dec, s0f, s0b, *, batch, seq_len):
    n_chunks = seq_len // RET_CHUNK
    t = batch * seq_len
    st_spec = pl.BlockSpec((None, None, RET_DK, RET_DV), lambda b, h: (b, h, 0, 0))
    st_shape = jax.ShapeDtypeStruct((batch, RET_HEADS, RET_DK, RET_DV), F32)
    return pl.pallas_call(
        functools.partial(_ret_kernel, n_chunks=n_chunks),
        grid=(batch, RET_HEADS),
        in_specs=[pl.BlockSpec(memory_space=pltpu.SMEM),
                  pl.BlockSpec((seq_len, RET_DK), lambda b, h: (b, h)),
                  pl.BlockSpec((seq_len, RET_DK), lambda b, h: (b, h)),
                  pl.BlockSpec((seq_len, RET_DV), lambda b, h: (b, h)),
                  pl.BlockSpec((seq_len, RET_DV), lambda b, h: (b, h)),
                  st_spec, st_spec],
        out_specs=[pl.BlockSpec((seq_len, RET_DV), lambda b, h: (b, h)), st_spec, st_spec],
        out_shape=[jax.ShapeDtypeStruct((t, RET_V_W), BF16), st_shape, st_shape],
        scratch_shapes=[pltpu.VMEM((n_chunks, RET_DK, RET_DV), F32)],
        compiler_params=_cparams("parallel", "parallel"),
        name=f"retention_l{seq_len}",
    )(dec, q, k, v, gsw, s0f, s0b)


def _pool_kernel(u_ref, w_ref, sc_ref, o_ref, *, seq_len, on_grid):
    tok = lax.broadcasted_iota(jnp.int32, (seq_len, POOL_CH), 0)

    def shift(a, s, stride, pos, width):
        y = pltpu.roll(a, (-s * stride) % seq_len, axis=0)
        ok = (pos < width - s) if s > 0 else (pos >= -s)
        return jnp.where(ok, y, 0.0)

    def box_mean(a, window, stride, pos, width):
        half = window // 2
        fw = a
        bw = shift(a, -1, stride, pos, width)
        m = 1
        while m < half:
            fw = fw + shift(fw, m, stride, pos, width)
            bw = bw + shift(bw, -m, stride, pos, width)
            m *= 2
        cnt = jnp.minimum(pos + half, width) - jnp.maximum(pos - half, 0)
        return (fw + bw) / cnt.astype(F32)

    for g, window in enumerate(POOL_WINDOWS):
        cols = slice(g * POOL_CH, (g + 1) * POOL_CH)
        ug = u_ref[:, cols].astype(F32)
        if on_grid:
            pooled = box_mean(ug, window, 1, tok & (GRID_W - 1), GRID_W)
            pooled = box_mean(pooled, window, GRID_W, tok >> 6, seq_len // GRID_W)
        else:
            pooled = box_mean(ug, window, 1, tok, seq_len)
        d = (pooled - ug).astype(BF16)
        o_ref[:, cols] = (_dot(d, w_ref[g]) * sc_ref[:, cols]).astype(BF16)


def _pool(u, pool_w, pool_scale, *, batch, seq_len, on_grid):
    t = batch * seq_len
    return pl.pallas_call(
        functools.partial(_pool_kernel, seq_len=seq_len, on_grid=on_grid),
        grid=(batch,),
        in_specs=[pl.BlockSpec((seq_len, POOL_W), lambda b: (b, 0)),
                  pl.BlockSpec((POOL_GROUPS, POOL_CH, POOL_CH), lambda b: (0, 0, 0)),
                  pl.BlockSpec((1, POOL_W), lambda b: (0, 0))],
        out_specs=pl.BlockSpec((seq_len, POOL_W), lambda b: (b, 0)),
        out_shape=jax.ShapeDtypeStruct((t, POOL_W), BF16),
        compiler_params=_cparams("parallel"),
        name=f"pool_l{seq_len}",
    )(u, pool_w, pool_scale)


def _pack_rows(x):
    half = D_MODEL // 2
    lo = lax.bitcast_convert_type(x[:, :half].astype(BF16).astype(F32), U32) >> 16
    hi = lax.bitcast_convert_type(x[:, half:].astype(BF16).astype(F32), U32) & jnp.uint32(0xFFFF0000)
    word = lax.bitcast_convert_type(hi | lo, I32)
    return [word[:, c * LANES:(c + 1) * LANES] for c in range(N_PIECES)]


def _unpack_rows(pieces):
    words = [lax.bitcast_convert_type(p, U32) for p in pieces]
    lo = [lax.bitcast_convert_type(w << 16, F32) for w in words]
    hi = [lax.bitcast_convert_type(w & jnp.uint32(0xFFFF0000), F32) for w in words]
    return lo, hi


def _merge_kernel(x_ref, z_ref, p_ref, ga_ref, gb_ref, mod_ref, g2_ref, wr_ref, wp_ref, wo_ref,
                  x1_ref, h2_ref, *piece_refs):
    y_ret = _dot(z_ref[...], wr_ref[...])
    y_pool = _dot(p_ref[...], wp_ref[...])
    merged = (jax.nn.sigmoid(ga_ref[...].astype(F32)) * y_ret
              + jax.nn.sigmoid(gb_ref[...].astype(F32)) * y_pool)
    x1 = x_ref[...] + mod_ref[0, 2:3, :] * _dot(merged.astype(BF16), wo_ref[...])
    x1_ref[...] = x1
    h2 = _rms_mod(x1, g2_ref[...], mod_ref[0, 4:5, :], mod_ref[0, 3:4, :])
    h2_ref[...] = h2.astype(BF16)
    for ref, piece in zip(piece_refs, _pack_rows(h2)):
        ref[...] = piece


def _merge(x, z, p, ga, gb, mods, norm2_g, w_br_ret, w_br_pool, w_out, *, tokens_per_mod, tm):
    t = x.shape[0]
    tiles_per_mod = tokens_per_mod // tm
    row = lambda w: pl.BlockSpec((tm, w), lambda i: (i, 0))
    full = lambda a: pl.BlockSpec(a.shape, lambda i: (0,) * a.ndim)
    outs = pl.pallas_call(
        _merge_kernel,
        grid=(t // tm,),
        in_specs=[row(D_MODEL), row(RET_V_W), row(POOL_W), row(D_MODEL), row(D_MODEL),
                  pl.BlockSpec((1, 6, D_MODEL), lambda i: (i // tiles_per_mod, 0, 0)),
                  full(norm2_g), full(w_br_ret), full(w_br_pool), full(w_out)],
        out_specs=[row(D_MODEL), row(D_MODEL)] + [row(LANES)] * N_PIECES,
        out_shape=[jax.ShapeDtypeStruct((t, D_MODEL), F32), jax.ShapeDtypeStruct((t, D_MODEL), BF16)]
        + [jax.ShapeDtypeStruct((t, LANES), I32)] * N_PIECES,
        compiler_params=_cparams("parallel"),
        name="merge",
    )(x, z, p, ga, gb, mods, norm2_g, w_br_ret, w_br_pool, w_out)
    return outs[0], outs[1], outs[2:]


def _route_kernel(h_ref, rw_ref, bias_ref, idx_ref, rank_ref, wt_ref, cnt_ref, carry_ref):
    e = N_EXPERTS
    tm = h_ref.shape[0]
    neg = -jnp.inf

    @pl.when(pl.program_id(0) == 0)
    def _():
        carry_ref[...] = jnp.zeros(carry_ref.shape, F32)

    logits = lax.dot_general(rw_ref[...], h_ref[...], (((1,), (1,)), ((), ())),
                             preferred_element_type=F32)[:e]
    scores = jax.nn.sigmoid(logits)
    sel = scores + bias_ref[:e, 0:1]
    e_idx = lax.broadcasted_iota(I32, (e, tm), 0)

    grp = sel.reshape(N_EXPERT_GROUPS, GROUP_SIZE, tm)
    m_idx = lax.broadcasted_iota(I32, grp.shape, 1)
    m1 = jnp.max(grp, axis=1, keepdims=True)
    first = jnp.min(jnp.where(grp == m1, m_idx, GROUP_SIZE), axis=1, keepdims=True)
    m2 = jnp.max(jnp.where(m_idx == first, neg, grp), axis=1, keepdims=True)
    gscore = (m1 + m2).reshape(N_EXPERT_GROUPS, tm)

    g_idx = lax.broadcasted_iota(I32, gscore.shape, 0)
    grank = jnp.zeros(gscore.shape, I32)
    for g in range(N_EXPERT_GROUPS):
        other = gscore[g:g + 1, :]
        beats = jnp.where(other > gscore, 1, jnp.where(other == gscore, (g_idx > g).astype(I32), 0))
        grank = grank + beats
    gkeep = (grank < TOPK_GROUPS).astype(F32)
    ekeep = jnp.broadcast_to(gkeep.reshape(N_EXPERT_GROUPS, 1, tm), grp.shape).reshape(e, tm)
    masked = jnp.where(ekeep > 0, sel, neg)

    chosen = jnp.zeros((e, tm), F32)
    picks, hits = [], []
    for _ in range(TOP_K):
        m = jnp.max(masked, axis=0, keepdims=True)
        pick = jnp.min(jnp.where(masked == m, e_idx, e), axis=0, keepdims=True)
        hit = e_idx == pick
        chosen = jnp.where(hit, 1.0, chosen)
        masked = jnp.where(hit, neg, masked)
        picks.append(pick)
        hits.append(hit)

    w = scores * chosen
    comb = w / jnp.sum(w, axis=0, keepdims=True) * ROUTED_SCALE

    t_row = lax.broadcasted_iota(I32, (tm, tm), 0)
    t_col = lax.broadcasted_iota(I32, (tm, tm), 1)
    before = (t_row < t_col).astype(BF16)
    rankmat = _dot(chosen.astype(BF16), before) + carry_ref[:e, 0:1]
    carry_ref[:e, :] = carry_ref[:e, :] + jnp.sum(chosen, axis=1, keepdims=True)
    cnt_ref[...] = carry_ref[...]

    idx_ref[...] = jnp.concatenate(picks, axis=0)
    rank_ref[...] = jnp.concatenate(
        [jnp.sum(jnp.where(h, rankmat, 0.0), axis=0, keepdims=True) for h in hits], axis=0).astype(I32)
    w_rows = [jnp.sum(jnp.where(h, comb, 0.0), axis=0, keepdims=True) for h in hits]
    wt_ref[...] = jnp.concatenate(w_rows + [jnp.zeros((LANES - TOP_K, tm), F32)], axis=0).T


def _route(h2, router_wt, bias_col, *, tm):
    t = h2.shape[0]
    krow = pl.BlockSpec((TOP_K, tm), lambda i: (0, i))
    return pl.pallas_call(
        _route_kernel,
        grid=(t // tm,),
        in_specs=[pl.BlockSpec((tm, D_MODEL), lambda i: (i, 0)),
                  pl.BlockSpec((LANES, D_MODEL), lambda i: (0, 0)),
                  pl.BlockSpec((LANES, 1), lambda i: (0, 0))],
        out_specs=[krow, krow, pl.BlockSpec((tm, LANES), lambda i: (i, 0)),
                   pl.BlockSpec((LANES, LANES), lambda i: (0, 0))],
        out_shape=[jax.ShapeDtypeStruct((TOP_K, t), I32), jax.ShapeDtypeStruct((TOP_K, t), I32),
                   jax.ShapeDtypeStruct((t, LANES), F32), jax.ShapeDtypeStruct((LANES, LANES), F32)],
        scratch_shapes=[pltpu.VMEM((LANES, LANES), F32)],
        compiler_params=_cparams("arbitrary"),
        name="route",
    )(h2, router_wt, bias_col)


def _plan_kernel(idx_ref, rank_ref, cnt_ref, pos_ref, te_ref, nu_ref):
    tf = idx_ref.shape[1]
    nt = te_ref.shape[1]
    cnt = cnt_ref[...].astype(I32)
    padded = (((cnt + (GROUP_TILE - 1)) // GROUP_TILE) * GROUP_TILE).astype(F32)
    e_sub = lax.broadcasted_iota(I32, (LANES, LANES), 0)
    e_lane = lax.broadcasted_iota(I32, (LANES, LANES), 1)
    base = jnp.sum(jnp.where(e_lane < e_sub, padded.T, 0.0), axis=1, keepdims=True)
    end = base + padded[:, 0:1]

    idx = idx_ref[...]
    start = jnp.zeros(idx.shape, F32)
    for e in range(N_EXPERTS):
        start = jnp.where(idx == e, base[e:e + 1, 0:1], start)
    pos = start.astype(I32) + rank_ref[...]
    for j in range(tf // SC_CHUNK):
        pos_ref[j] = pos[:, j * SC_CHUNK:(j + 1) * SC_CHUNK]

    tile_start = (lax.broadcasted_iota(I32, (N_EXPERTS, nt), 1) * GROUP_TILE).astype(F32)
    done = jnp.sum(jnp.where(end[:N_EXPERTS] <= tile_start, 1.0, 0.0), axis=0, keepdims=True)
    te_ref[...] = jnp.minimum(done, N_EXPERTS - 1.0).astype(I32)
    total = jnp.sum(padded[:, 0:1], axis=0, keepdims=True)
    nu_ref[...] = jnp.broadcast_to(total * (1.0 / GROUP_TILE), nu_ref.shape).astype(I32)


def _plan(idx, rank, counts, *, n_tiles, tf):
    t = idx.shape[1]
    nt_pad = -(-n_tiles // LANES) * LANES
    krow = pl.BlockSpec((TOP_K, tf), lambda i: (0, i))
    return pl.pallas_call(
        _plan_kernel,
        grid=(t // tf,),
        in_specs=[krow, krow, pl.BlockSpec((LANES, LANES), lambda i: (0, 0))],
        out_specs=[pl.BlockSpec((tf // SC_CHUNK, TOP_K, SC_CHUNK), lambda i: (i, 0, 0)),
                   pl.BlockSpec((1, nt_pad), lambda i: (0, 0)),
                   pl.BlockSpec((1, LANES), lambda i: (0, 0))],
        out_shape=[jax.ShapeDtypeStruct((t // SC_CHUNK, TOP_K, SC_CHUNK), I32),
                   jax.ShapeDtypeStruct((1, nt_pad), I32), jax.ShapeDtypeStruct((1, LANES), I32)],
        compiler_params=_cparams("arbitrary"),
        name="moe_plan",
    )(idx, rank, counts)


def _sc_mesh_info():
    info = plsc.get_sparse_core_info()
    mesh = plsc.VectorSubcoreMesh(core_axis_name="c", subcore_axis_name="s")
    return mesh, info.num_cores, info.num_cores * info.num_subcores


def _sc_dispatch(pieces, pos, *, n_rows):
    t = pieces[0].shape[0]
    mesh, n_cores, n_workers = _sc_mesh_info()
    per_w = t // SC_CHUNK // n_workers

    @functools.partial(
        pl.kernel, mesh=mesh,
        out_type=[jax.ShapeDtypeStruct((n_rows, LANES), I32)] * N_PIECES,
        scratch_types=[pltpu.VMEM((TOP_K, SC_CHUNK), I32),
                       pltpu.VMEM((N_PIECES, SC_CHUNK, LANES), I32),
                       pltpu.SemaphoreType.DMA((N_PIECES,)),
                       pltpu.SemaphoreType.DMA],
        name="sc_dispatch",
    )
    def run(*refs):
        src = refs[:N_PIECES]
        pos_hbm = refs[N_PIECES]
        dst = refs[N_PIECES + 1:2 * N_PIECES + 1]
        idx_v, rows_v, load_sem, put_sem = refs[2 * N_PIECES + 1:]
        wid = lax.axis_index("s") * n_cores + lax.axis_index("c")

        @pl.loop(0, per_w)
        def _(j):
            ch = wid * per_w + j
            t0 = pl.multiple_of(ch * SC_CHUNK, SC_CHUNK)
            loads = [pltpu.make_async_copy(src[c].at[pl.ds(t0, SC_CHUNK)], rows_v.at[c], load_sem.at[c])
                     for c in range(N_PIECES)]
            for ld in loads:
                ld.start()
            pltpu.sync_copy(pos_hbm.at[ch], idx_v)
            puts = []
            for c in range(N_PIECES):
                loads[c].wait()
                for k in range(TOP_K):
                    puts.append(pltpu.make_async_copy(rows_v.at[c], dst[c].at[idx_v.at[k]], put_sem))
                    puts[-1].start()
            for cp in puts:
                cp.wait()

    return run(*pieces, pos)


def _sc_collect(pieces, pos, *, n_tokens):
    mesh, n_cores, n_workers = _sc_mesh_info()
    per_w = n_tokens // SC_CHUNK // n_workers
    n_buf = 4
    lag = n_buf // 2
    units = [(k, c) for k in range(TOP_K) for c in range(N_PIECES)]

    @functools.partial(
        pl.kernel, mesh=mesh,
        out_type=[jax.ShapeDtypeStruct((TOP_K, n_tokens, LANES), I32)] * N_PIECES,
        scratch_types=[pltpu.VMEM((TOP_K, SC_CHUNK), I32),
                       pltpu.VMEM((n_buf, SC_CHUNK, LANES), I32),
                       pltpu.SemaphoreType.DMA((n_buf,)),
                       pltpu.SemaphoreType.DMA((n_buf,))],
        name="sc_collect",
    )
    def run(*refs):
        src = refs[:N_PIECES]
        pos_hbm = refs[N_PIECES]
        dst = refs[N_PIECES + 1:2 * N_PIECES + 1]
        idx_v, buf, get_sem, put_sem = refs[2 * N_PIECES + 1:]
        wid = lax.axis_index("s") * n_cores + lax.axis_index("c")

        @pl.loop(0, per_w)
        def _(j):
            ch = wid * per_w + j
            t0 = pl.multiple_of(ch * SC_CHUNK, SC_CHUNK)
            pltpu.sync_copy(pos_hbm.at[ch], idx_v)

            def get(u):
                k, c = units[u]
                return pltpu.make_async_copy(src[c].at[idx_v.at[k]], buf.at[u % n_buf], get_sem.at[u % n_buf])

            def put(u):
                k, c = units[u]
                return pltpu.make_async_copy(buf.at[u % n_buf], dst[c].at[k, pl.ds(t0, SC_CHUNK)],
                                             put_sem.at[u % n_buf])

            n = len(units)
            for u in range(n + lag):
                if u < n:
                    if u >= n_buf:
                        put(u - n_buf).wait()
                    get(u).start()
                if 0 <= u - lag < n:
                    get(u - lag).wait()
                    put(u - lag).start()
            for u in range(n - n_buf, n):
                put(u).wait()

    return run(*pieces, pos)


def _experts_kernel(te_ref, nu_ref, *refs):
    x_refs = refs[:N_PIECES]
    wg_ref, wu_ref, wd_ref = refs[N_PIECES:N_PIECES + 3]
    y_refs = refs[N_PIECES + 3:]

    @pl.when(pl.program_id(0) < nu_ref[0])
    def _():
        lo, hi = _unpack_rows([r[...] for r in x_refs])
        x = jnp.concatenate(lo + hi, axis=1).astype(BF16)
        hid = _silu(_dot(x, wg_ref[...])) * _dot(x, wu_ref[...])
        y = _dot(hid.astype(BF16), wd_ref[...])
        for ref, piece in zip(y_refs, _pack_rows(y)):
            ref[...] = piece


def _experts(x_pieces, tile_expert, n_used, wg, wu, wd):
    n_rows = x_pieces[0].shape[0]
    n_tiles = n_rows // GROUP_TILE

    def tile(i, te, nu):
        return jnp.minimum(i, nu[0] - 1)

    row = pl.BlockSpec((GROUP_TILE, LANES), lambda i, te, nu: (tile(i, te, nu), 0))
    wspec = lambda a, b: pl.BlockSpec((None, a, b), lambda i, te, nu: (te[tile(i, te, nu)], 0, 0))
    return pl.pallas_call(
        _experts_kernel,
        grid_spec=pltpu.PrefetchScalarGridSpec(
            num_scalar_prefetch=2,
            grid=(n_tiles,),
            in_specs=[row] * N_PIECES + [wspec(D_MODEL, D_EXPERT), wspec(D_MODEL, D_EXPERT),
                                         wspec(D_EXPERT, D_MODEL)],
            out_specs=[row] * N_PIECES),
        out_shape=[jax.ShapeDtypeStruct((n_rows, LANES), I32)] * N_PIECES,
        compiler_params=_cparams("arbitrary"),
        name="experts",
    )(tile_expert, n_used, *x_pieces, wg, wu, wd)


def _moe_out_kernel(h_ref, wt_ref, x1_ref, mod_ref, fg_ref, sg_ref, su_ref, sd_ref, *refs):
    y_refs = refs[:N_PIECES]
    out_ref = refs[N_PIECES]
    h = h_ref[...]
    hid = _silu(_dot(h, sg_ref[...])) * _dot(h, su_ref[...])
    shared = _dot(hid.astype(BF16), sd_ref[...])
    wt = wt_ref[...]
    lo_acc = [None] * N_PIECES
    hi_acc = [None] * N_PIECES
    for k in range(TOP_K):
        wk = wt[:, k:k + 1]
        lo, hi = _unpack_rows([r[k] for r in y_refs])
        for c in range(N_PIECES):
            lo_acc[c] = wk * lo[c] if k == 0 else lo_acc[c] + wk * lo[c]
            hi_acc[c] = wk * hi[c] if k == 0 else hi_acc[c] + wk * hi[c]
    routed = jnp.concatenate(lo_acc + hi_acc, axis=1)
    x2 = x1_ref[...] + mod_ref[0, 5:6, :] * (shared + routed)
    out_ref[...] = x2 * lax.rsqrt(jnp.mean(x2 * x2, axis=-1, keepdims=True) + EPS) * fg_ref[...]


def _moe_out(h2, wt, x1, mods, final_g, sg, su, sd, y_pieces, *, tokens_per_mod, tm):
    t = h2.shape[0]
    tiles_per_mod = tokens_per_mod // tm
    row = lambda w: pl.BlockSpec((tm, w), lambda i: (i, 0))
    full = lambda a: pl.BlockSpec(a.shape, lambda i: (0,) * a.ndim)
    return pl.pallas_call(
        _moe_out_kernel,
        grid=(t // tm,),
        in_specs=[row(D_MODEL), row(LANES), row(D_MODEL),
                  pl.BlockSpec((1, 6, D_MODEL), lambda i: (i // tiles_per_mod, 0, 0)),
                  full(final_g), full(sg), full(su), full(sd)]
        + [pl.BlockSpec((TOP_K, tm, LANES), lambda i: (0, i, 0))] * N_PIECES,
        out_specs=row(D_MODEL),
        out_shape=jax.ShapeDtypeStruct((t, D_MODEL), F32),
        compiler_params=_cparams("parallel"),
        name="moe_out",
    )(h2, wt, x1, mods, final_g, sg, su, sd, *y_pieces)


def _trunk(x, mods, s0f, s0b, w, *, batch, seq_len, on_grid):
    t = batch * seq_len
    tokens_per_mod = t // mods.shape[0]
    cos_t, sin_t = _rope_tables(seq_len)
    q, k, v, gsw, up, ga, gb = _inproj(x, mods, w["norm1_g"], w["w_in"], cos_t, sin_t,
                                       tokens_per_mod=tokens_per_mod, seq_len=seq_len,
                                       on_grid=on_grid, tm=256)
    z, s_f, s_b = _retention(q, k, v, gsw, w["dec"], s0f, s0b, batch=batch, seq_len=seq_len)
    p = _pool(up, w["pool_w"], w["pool_scale"], batch=batch, seq_len=seq_len, on_grid=on_grid)
    x1, h2, h2_pieces = _merge(x, z, p, ga, gb, mods, w["norm2_g"], w["w_br_ret"], w["w_br_pool"],
                               w["w_out"], tokens_per_mod=tokens_per_mod, tm=256)

    n_rows = t * TOP_K + N_EXPERTS * GROUP_TILE
    idx, rank, wt, counts = _route(h2, w["router_wt"], w["router_bias"], tm=512)
    pos, tile_expert, n_used = _plan(idx, rank, counts, n_tiles=n_rows // GROUP_TILE, tf=512)
    x_sorted = _sc_dispatch(h2_pieces, pos, n_rows=n_rows)
    y_sorted = _experts(x_sorted, tile_expert.reshape(-1), n_used.reshape(-1),
                        w["exp_w_gate"], w["exp_w_up"], w["exp_w_down"])
    y_tok = _sc_collect(y_sorted, pos, n_tokens=t)
    y = _moe_out(h2, wt, x1, mods, w["final_g"], w["sh_w_gate"], w["sh_w_up"], w["sh_w_down"], y_tok,
                 tokens_per_mod=tokens_per_mod, tm=256)
    return y, s_f, s_b


def kernel(x_prompt, x_sample, state_ret_fwd, state_ret_bwd, c, c_ctx, ada_w, ada_b, norm1_g, norm2_g, w_in,
           ret_decay_fwd, ret_decay_bwd, w_br_ret, pool_w, pool_scale, w_br_pool, w_out, router_w, router_bias,
           exp_w_gate, exp_w_up, exp_w_down, sh_w_gate, sh_w_up, sh_w_down, final_norm_g):
    n_req, seq, d = x_prompt.shape
    n_dec, dec_seq, _ = x_sample.shape
    depth = ada_w.shape[0]
    assert depth == 1 and d == D_MODEL

    xc = x_prompt.reshape(n_req * seq, d)
    xs = x_sample.reshape(n_dec * dec_seq, d)
    zero_state = jnp.zeros((n_req, RET_HEADS, RET_DK, RET_DV), F32)
    new_f, new_b = [], []
    for l in range(depth):
        c_rows = jnp.concatenate([c_ctx[None, :], c, jnp.zeros((8 - 1 - n_dec, d), F32)], axis=0)
        mods = _ada(c_rows, ada_w[l], ada_b[l]).reshape(8, 6, d)
        pad_rows = LANES - N_EXPERTS
        w = dict(
            norm1_g=norm1_g[l].reshape(1, d), norm2_g=norm2_g[l].reshape(1, d),
            final_g=final_norm_g.reshape(1, d),
            w_in=w_in[l].astype(BF16),
            dec=jnp.stack([ret_decay_fwd[l], ret_decay_bwd[l]]).astype(F32),
            w_br_ret=w_br_ret[l].astype(BF16), pool_w=pool_w[l].astype(BF16),
            pool_scale=pool_scale[l].reshape(1, POOL_W), w_br_pool=w_br_pool[l].astype(BF16),
            w_out=w_out[l].astype(BF16),
            router_wt=jnp.pad(router_w[l].T, ((0, pad_rows), (0, 0))).astype(BF16),
            router_bias=jnp.pad(router_bias[l].astype(F32).reshape(N_EXPERTS, 1), ((0, pad_rows), (0, 0))),
            exp_w_gate=exp_w_gate[l].astype(BF16), exp_w_up=exp_w_up[l].astype(BF16),
            exp_w_down=exp_w_down[l].astype(BF16), sh_w_gate=sh_w_gate[l].astype(BF16),
            sh_w_up=sh_w_up[l].astype(BF16), sh_w_down=sh_w_down[l].astype(BF16),
        )
        xc, s_f, s_b = _trunk(xc, mods[0:1], zero_state, zero_state, w,
                              batch=n_req, seq_len=seq, on_grid=False)
        new_f.append(s_f)
        new_b.append(s_b)
        xs, _, _ = _trunk(xs, mods[1:1 + n_dec], state_ret_fwd[:, l].astype(F32),
                          state_ret_bwd[:, l].astype(F32), w,
                          batch=n_dec, seq_len=dec_seq, on_grid=True)
    y_prompt = xc.reshape(n_req, seq, d)
    y_sample = xs.reshape(n_dec, dec_seq, d)
    return (y_prompt, y_sample, jnp.stack(new_f, axis=1).astype(x_prompt.dtype),
            jnp.stack(new_b, axis=1).astype(x_prompt.dtype))
```

```python
import functools

import jax
import jax.numpy as jnp
from jax import lax
from jax.experimental import pallas as pl
from jax.experimental.pallas import tpu as pltpu
from jax.experimental.pallas import tpu_sc as plsc

D_MODEL = 1024
GRID_W = 64
RET_HEADS = 4
RET_DK = 128
RET_DV = 256
RET_QK_W = RET_HEADS * RET_DK
RET_V_W = RET_HEADS * RET_DV
RET_CHUNK = 128
ROPE_BASE = 10000.0
POOL_GROUPS = 4
POOL_CH = 128
POOL_W = POOL_GROUPS * POOL_CH
POOL_WINDOWS = (2, 4, 8, 16)
N_EXPERTS = 64
TOP_K = 8
N_EXPERT_GROUPS = 8
GROUP_SIZE = N_EXPERTS // N_EXPERT_GROUPS
TOPK_GROUPS = 4
D_EXPERT = 256
ROUTED_SCALE = 2.5
EPS = 1e-6
IN_SIZES = (RET_QK_W, RET_QK_W, RET_V_W, RET_V_W, POOL_W, D_MODEL, D_MODEL)
IN_OFFS = tuple(sum(IN_SIZES[:i]) for i in range(len(IN_SIZES) + 1))
IN_W = IN_OFFS[-1]

LANES = 128
VMEM_LIMIT = 56 << 20
N_PIECES = D_MODEL // 2 // LANES
MXU_DIM = 256
SC_CHUNK = 128

F32 = jnp.float32
BF16 = jnp.bfloat16
I32 = jnp.int32
U32 = jnp.uint32


def _cparams(*sem):
    return pltpu.CompilerParams(dimension_semantics=sem, vmem_limit_bytes=VMEM_LIMIT)


def _dot(a, b):
    return jnp.dot(a, b, preferred_element_type=F32)


def _silu(x):
    return x * jax.nn.sigmoid(x)


def _rms_mod(x, g, scale, shift):
    y = x * lax.rsqrt(jnp.mean(x * x, axis=-1, keepdims=True) + EPS)
    return (y * g) * (1.0 + scale) + shift


def _ada_kernel(c_ref, w_ref, b_ref, o_ref):
    c = c_ref[...]
    o_ref[...] = jnp.dot(_silu(c), w_ref[...], preferred_element_type=F32,
                         precision=lax.Precision.HIGHEST) + b_ref[...]


def _ada(c_rows, ada_w, ada_b):
    r = c_rows.shape[0]
    n = ada_w.shape[1]
    tn = D_MODEL
    return pl.pallas_call(
        _ada_kernel,
        grid=(n // tn,),
        in_specs=[pl.BlockSpec((r, D_MODEL), lambda j: (0, 0)),
                  pl.BlockSpec((D_MODEL, tn), lambda j: (0, j)),
                  pl.BlockSpec((1, tn), lambda j: (0, j))],
        out_specs=pl.BlockSpec((r, tn), lambda j: (0, j)),
        out_shape=jax.ShapeDtypeStruct((r, n), F32),
        compiler_params=_cparams("parallel"),
        name="ada_mod",
    )(c_rows, ada_w, ada_b.reshape(1, n))


def _inproj_kernel(x_ref, mod_ref, g_ref, w_ref, cos_ref, sin_ref,
                   q_ref, k_ref, v_ref, gsw_ref, up_ref, ga_ref, gb_ref, *, on_grid):
    x = x_ref[...]
    h = _rms_mod(x, g_ref[...], mod_ref[0, 1:2, :], mod_ref[0, 0:1, :]).astype(BF16)

    def seg(i):
        return _dot(h, w_ref[:, IN_OFFS[i]:IN_OFFS[i + 1]])

    q = seg(0)
    k = seg(1)
    if on_grid:
        cos = jnp.concatenate([cos_ref[...]] * RET_HEADS, axis=1)
        sin = jnp.concatenate([sin_ref[...]] * RET_HEADS, axis=1)
        lane = lax.broadcasted_iota(jnp.int32, q.shape, 1)
        first = (lane & 63) < 32

        def rope(a):
            up = pltpu.roll(a, RET_QK_W - 32, axis=1)
            dn = pltpu.roll(a, 32, axis=1)
            return a * cos + jnp.where(first, up, dn) * sin

        q = rope(q)
        k = rope(k)
    q_ref[...] = q.astype(BF16)
    k_ref[...] = (k * (RET_DK ** -0.5)).astype(BF16)
    v_ref[...] = seg(2).astype(BF16)
    gsw_ref[...] = seg(3).astype(BF16)
    up_ref[...] = seg(4).astype(BF16)
    ga_ref[...] = seg(5).astype(BF16)
    gb_ref[...] = seg(6).astype(BF16)


def _inproj(x, mods, norm_g, w_in, cos_t, sin_t, *, tokens_per_mod, seq_len, on_grid, tm):
    t = x.shape[0]
    tiles_per_mod = tokens_per_mod // tm
    tiles_per_seq = seq_len // tm
    widths = IN_SIZES
    out_shape = [jax.ShapeDtypeStruct((t, w), BF16) for w in widths]
    out_specs = [pl.BlockSpec((tm, w), lambda i: (i, 0)) for w in widths]
    return pl.pallas_call(
        functools.partial(_inproj_kernel, on_grid=on_grid),
        grid=(t // tm,),
        in_specs=[pl.BlockSpec((tm, D_MODEL), lambda i: (i, 0)),
                  pl.BlockSpec((1, 6, D_MODEL), lambda i: (i // tiles_per_mod, 0, 0)),
                  pl.BlockSpec((1, D_MODEL), lambda i: (0, 0)),
                  pl.BlockSpec((D_MODEL, IN_W), lambda i: (0, 0)),
                  pl.BlockSpec((tm, RET_DK), lambda i: (i % tiles_per_seq, 0)),
                  pl.BlockSpec((tm, RET_DK), lambda i: (i % tiles_per_seq, 0))],
        out_specs=out_specs,
        out_shape=out_shape,
        compiler_params=_cparams("parallel"),
        name="inproj_grid" if on_grid else "inproj_seq",
    )(x, mods, norm_g, w_in, cos_t, sin_t)


def _rope_tables(seq_len):
    t = jnp.arange(seq_len)
    row = (t // GRID_W).astype(F32)
    col = (t % GRID_W).astype(F32)
    m = RET_DK // 4
    inv = ROPE_BASE ** (-jnp.arange(m, dtype=F32) / m)
    ar = row[:, None] * inv
    ac = col[:, None] * inv
    cos = jnp.concatenate([jnp.cos(ar), jnp.cos(ar), jnp.cos(ac), jnp.cos(ac)], axis=1)
    sin = jnp.concatenate([-jnp.sin(ar), jnp.sin(ar), -jnp.sin(ac), jnp.sin(ac)], axis=1)
    return cos.astype(F32), sin.astype(F32)


def _ret_kernel(dec_ref, q_ref, k_ref, v_ref, g_ref, s0f_ref, s0b_ref,
                z_ref, sf_ref, sb_ref, sbin_ref, *, n_chunks):
    c = RET_CHUNK
    h = pl.program_id(1)

    def log_gamma(d, shape):
        return jnp.log1p(-jnp.exp2(-jnp.full(shape, d, F32)))

    dec_f = dec_ref[0, h]
    dec_b = dec_ref[1, h]
    ii = lax.broadcasted_iota(jnp.int32, (c, c), 0)
    jj = lax.broadcasted_iota(jnp.int32, (c, c), 1)
    rel = (ii - jj).astype(F32)
    dmask = (jnp.where(rel >= 0, jnp.exp(log_gamma(dec_f, (c, c)) * jnp.maximum(rel, 0.0)), 0.0)
             + jnp.where(rel <= 0, jnp.exp(log_gamma(dec_b, (c, c)) * jnp.maximum(-rel, 0.0)), 0.0))
    iv = lax.broadcasted_iota(jnp.int32, (c, RET_DV), 0).astype(F32)
    ik = lax.broadcasted_iota(jnp.int32, (c, RET_DK), 0).astype(F32)
    lgf_v = log_gamma(dec_f, (c, RET_DV))
    lgb_v = log_gamma(dec_b, (c, RET_DV))
    lgf_k = log_gamma(dec_f, (c, RET_DK))
    lgb_k = log_gamma(dec_b, (c, RET_DK))
    qdec_f = jnp.exp(lgf_v * (iv + 1.0))
    qdec_b = jnp.exp(lgb_v * (c - iv))
    kdec_f = jnp.exp(lgf_k * (c - 1.0 - ik))
    kdec_b = jnp.exp(lgb_k * ik)
    cdec_f = jnp.exp(log_gamma(dec_f, (RET_DK, RET_DV)) * c)
    cdec_b = jnp.exp(log_gamma(dec_b, (RET_DK, RET_DV)) * c)

    def kv_update(kc, vc, kdec):
        kd = (kc.astype(F32) * kdec).astype(BF16)
        return lax.dot_general(kd, vc, (((0,), (0,)), ((), ())), preferred_element_type=F32)

    def rows(ci):
        return pl.ds(pl.multiple_of(ci * c, c), c)

    def bwd(t, s):
        ci = n_chunks - 1 - t
        sbin_ref[ci] = s
        return s * cdec_b + kv_update(k_ref[rows(ci), :], v_ref[rows(ci), :], kdec_b)

    sb_ref[...] = lax.fori_loop(0, n_chunks, bwd, s0b_ref[...])

    def fwd(ci, s):
        r = rows(ci)
        qc = q_ref[r, :]
        kc = k_ref[r, :]
        vc = v_ref[r, :]
        sc = lax.dot_general(qc, kc, (((1,), (1,)), ((), ())), preferred_element_type=F32)
        o = _dot((sc * dmask).astype(BF16), vc)
        o = o + _dot(qc, s.astype(BF16)) * qdec_f
        o = o + _dot(qc, sbin_ref[ci].astype(BF16)) * qdec_b
        o = o * lax.rsqrt(jnp.mean(o * o, axis=-1, keepdims=True) + EPS)
        g = g_ref[r, :].astype(F32)
        z_ref[r, :] = (_silu(g) * o).astype(BF16)
        return s * cdec_f + kv_update(kc, vc, kdec_f)

    sf_ref[...] = lax.fori_loop(0, n_chunks, fwd, s0f_ref[...])


def _retention(q, k, v, gsw, dec, s0f, s0b, *, batch, seq_len):
    n_chunks = seq_len // RET_CHUNK
    t = batch * seq_len
    st_spec = pl.BlockSpec((None, None, RET_DK, RET_DV), lambda b, h: (b, h, 0, 0))
    st_shape = jax.ShapeDtypeStruct((batch, RET_HEADS, RET_DK, RET_DV), F32)
    return pl.pallas_call(
        functools.partial(_ret_kernel, n_chunks=n_chunks),
        grid=(batch, RET_HEADS),
        in_specs=[pl.BlockSpec(memory_space=pltpu.SMEM),
                  pl.BlockSpec((seq_len, RET_DK), lambda b, h: (b, h)),
                  pl.BlockSpec((seq_len, RET_DK), lambda b, h: (b, h)),
                  pl.BlockSpec((seq_len, RET_DV), lambda b, h: (b, h)),
                  pl.BlockSpec((seq_len, RET_DV), lambda b, h: (b, h)),
                  st_spec, st_spec],
        out_specs=[pl.BlockSpec((seq_len, RET_DV), lambda b, h: (b, h)), st_spec, st_spec],
        out_shape=[jax.ShapeDtypeStruct((t, RET_V_W), BF16), st_shape, st_shape],
        scratch_shapes=[pltpu.VMEM((n_chunks, RET_DK, RET_DV), F32)],
        compiler_params=_cparams("parallel", "parallel"),
        name=f"retention_l{seq_len}",
    )(dec, q, k, v, gsw, s0f, s0b)


def _pool_kernel(u_ref, w_ref, sc_ref, o_ref, *, seq_len, on_grid):
    tok = lax.broadcasted_iota(jnp.int32, (seq_len, POOL_CH), 0)

    def shift(a, s, stride, pos, width):
        y = pltpu.roll(a, (-s * stride) % seq_len, axis=0)
        ok = (pos < width - s) if s > 0 else (pos >= -s)
        return jnp.where(ok, y, 0.0)

    def box_mean(a, window, stride, pos, width):
        half = window // 2
        fw = a
        bw = shift(a, -1, stride, pos, width)
        m = 1
        while m < half:
            fw = fw + shift(fw, m, stride, pos, width)
            bw = bw + shift(bw, -m, stride, pos, width)
            m *= 2
        cnt = jnp.minimum(pos + half, width) - jnp.maximum(pos - half, 0)
        return (fw + bw) / cnt.astype(F32)

    for g, window in enumerate(POOL_WINDOWS):
        cols = slice(g * POOL_CH, (g + 1) * POOL_CH)
        ug = u_ref[:, cols].astype(F32)
        if on_grid:
            pooled = box_mean(ug, window, 1, tok & (GRID_W - 1), GRID_W)
            pooled = box_mean(pooled, window, GRID_W, tok >> 6, seq_len // GRID_W)
        else:
            pooled = box_mean(ug, window, 1, tok, seq_len)
        d = (pooled - ug).astype(BF16)
        o_ref[:, cols] = (_dot(d, w_ref[g]) * sc_ref[:, cols]).astype(BF16)


def _pool(u, pool_w, pool_scale, *, batch, seq_len, on_grid):
    t = batch * seq_len
    return pl.pallas_call(
        functools.partial(_pool_kernel, seq_len=seq_len, on_grid=on_grid),
        grid=(batch,),
        in_specs=[pl.BlockSpec((seq_len, POOL_W), lambda b: (b, 0)),
                  pl.BlockSpec((POOL_GROUPS, POOL_CH, POOL_CH), lambda b: (0, 0, 0)),
                  pl.BlockSpec((1, POOL_W), lambda b: (0, 0))],
        out_specs=pl.BlockSpec((seq_len, POOL_W), lambda b: (b, 0)),
        out_shape=jax.ShapeDtypeStruct((t, POOL_W), BF16),
        compiler_params=_cparams("parallel"),
        name=f"pool_l{seq_len}",
    )(u, pool_w, pool_scale)


def _pack_rows(x):
    half = D_MODEL // 2
    lo = lax.bitcast_convert_type(x[:, :half].astype(BF16).astype(F32), U32) >> 16
    hi = lax.bitcast_convert_type(x[:, half:].astype(BF16).astype(F32), U32) & jnp.uint32(0xFFFF0000)
    word = lax.bitcast_convert_type(hi | lo, I32)
    return [word[:, c * LANES:(c + 1) * LANES] for c in range(N_PIECES)]


def _unpack_rows(pieces):
    words = [lax.bitcast_convert_type(p, U32) for p in pieces]
    lo = [lax.bitcast_convert_type(w << 16, F32) for w in words]
    hi = [lax.bitcast_convert_type(w & jnp.uint32(0xFFFF0000), F32) for w in words]
    return lo, hi


def _merge_kernel(x_ref, z_ref, p_ref, ga_ref, gb_ref, mod_ref, g2_ref, wr_ref, wp_ref, wo_ref,
                  x1_ref, h2_ref, *piece_refs):
    y_ret = _dot(z_ref[...], wr_ref[...])
    y_pool = _dot(p_ref[...], wp_ref[...])
    merged = (jax.nn.sigmoid(ga_ref[...].astype(F32)) * y_ret
              + jax.nn.sigmoid(gb_ref[...].astype(F32)) * y_pool)
    x1 = x_ref[...] + mod_ref[0, 2:3, :] * _dot(merged.astype(BF16), wo_ref[...])
    x1_ref[...] = x1
    h2 = _rms_mod(x1, g2_ref[...], mod_ref[0, 4:5, :], mod_ref[0, 3:4, :])
    h2_ref[...] = h2.astype(BF16)
    for ref, piece in zip(piece_refs, _pack_rows(h2)):
        ref[...] = piece


def _merge(x, z, p, ga, gb, mods, norm2_g, w_br_ret, w_br_pool, w_out, *, tokens_per_mod, tm):
    t = x.shape[0]
    tiles_per_mod = tokens_per_mod // tm
    row = lambda w: pl.BlockSpec((tm, w), lambda i: (i, 0))
    full = lambda a: pl.BlockSpec(a.shape, lambda i: (0,) * a.ndim)
    outs = pl.pallas_call(
        _merge_kernel,
        grid=(t // tm,),
        in_specs=[row(D_MODEL), row(RET_V_W), row(POOL_W), row(D_MODEL), row(D_MODEL),
                  pl.BlockSpec((1, 6, D_MODEL), lambda i: (i // tiles_per_mod, 0, 0)),
                  full(norm2_g), full(w_br_ret), full(w_br_pool), full(w_out)],
        out_specs=[row(D_MODEL), row(D_MODEL)] + [row(LANES)] * N_PIECES,
        out_shape=[jax.ShapeDtypeStruct((t, D_MODEL), F32), jax.ShapeDtypeStruct((t, D_MODEL), BF16)]
        + [jax.ShapeDtypeStruct((t, LANES), I32)] * N_PIECES,
        compiler_params=_cparams("parallel"),
        name="merge",
    )(x, z, p, ga, gb, mods, norm2_g, w_br_ret, w_br_pool, w_out)
    return outs[0], outs[1], outs[2:]


def _route_kernel(h_ref, rw_ref, bias_ref, idx_ref, rank_ref, wt_ref, cnt_ref, carry_ref):
    e = N_EXPERTS
    tm = h_ref.shape[0]
    neg = -jnp.inf

    @pl.when(pl.program_id(0) == 0)
    def _():
        carry_ref[...] = jnp.zeros(carry_ref.shape, F32)

    logits = lax.dot_general(rw_ref[...], h_ref[...], (((1,), (1,)), ((), ())),
                             preferred_element_type=F32)[:e]
    scores = jax.nn.sigmoid(logits)
    sel = scores + bias_ref[:e, 0:1]
    e_idx = lax.broadcasted_iota(I32, (e, tm), 0)

    grp = sel.reshape(N_EXPERT_GROUPS, GROUP_SIZE, tm)
    m_idx = lax.broadcasted_iota(I32, grp.shape, 1)
    m1 = jnp.max(grp, axis=1, keepdims=True)
    first = jnp.min(jnp.where(grp == m1, m_idx, GROUP_SIZE), axis=1, keepdims=True)
    m2 = jnp.max(jnp.where(m_idx == first, neg, grp), axis=1, keepdims=True)
    gscore = (m1 + m2).reshape(N_EXPERT_GROUPS, tm)

    g_idx = lax.broadcasted_iota(I32, gscore.shape, 0)
    grank = jnp.zeros(gscore.shape, I32)
    for g in range(N_EXPERT_GROUPS):
        other = gscore[g:g + 1, :]
        beats = jnp.where(other > gscore, 1, jnp.where(other == gscore, (g_idx > g).astype(I32), 0))
        grank = grank + beats
    gkeep = (grank < TOPK_GROUPS).astype(F32)
    ekeep = jnp.broadcast_to(gkeep.reshape(N_EXPERT_GROUPS, 1, tm), grp.shape).reshape(e, tm)
    masked = jnp.where(ekeep > 0, sel, neg)

    chosen = jnp.zeros((e, tm), F32)
    picks, hits = [], []
    for _ in range(TOP_K):
        m = jnp.max(masked, axis=0, keepdims=True)
        pick = jnp.min(jnp.where(masked == m, e_idx, e), axis=0, keepdims=True)
        hit = e_idx == pick
        chosen = jnp.where(hit, 1.0, chosen)
        masked = jnp.where(hit, neg, masked)
        picks.append(pick)
        hits.append(hit)

    w = scores * chosen
    comb = w / jnp.sum(w, axis=0, keepdims=True) * ROUTED_SCALE

    t_row = lax.broadcasted_iota(I32, (tm, tm), 0)
    t_col = lax.broadcasted_iota(I32, (tm, tm), 1)
    before = (t_row < t_col).astype(BF16)
    rankmat = _dot(chosen.astype(BF16), before) + carry_ref[:e, 0:1]
    carry_ref[:e, :] = carry_ref[:e, :] + jnp.sum(chosen, axis=1, keepdims=True)
    cnt_ref[...] = carry_ref[...]

    idx_ref[...] = jnp.concatenate(picks, axis=0)
    rank_ref[...] = jnp.concatenate(
        [jnp.sum(jnp.where(h, rankmat, 0.0), axis=0, keepdims=True) for h in hits], axis=0).astype(I32)
    w_rows = [jnp.sum(jnp.where(h, comb, 0.0), axis=0, keepdims=True) for h in hits]
    wt_ref[...] = jnp.concatenate(w_rows + [jnp.zeros((LANES - TOP_K, tm), F32)], axis=0).T


def _route(h2, router_wt, bias_col, *, tm):
    t = h2.shape[0]
    krow = pl.BlockSpec((TOP_K, tm), lambda i: (0, i))
    return pl.pallas_call(
        _route_kernel,
        grid=(t // tm,),
        in_specs=[pl.BlockSpec((tm, D_MODEL), lambda i: (i, 0)),
                  pl.BlockSpec((LANES, D_MODEL), lambda i: (0, 0)),
                  pl.BlockSpec((LANES, 1), lambda i: (0, 0))],
        out_specs=[krow, krow, pl.BlockSpec((tm, LANES), lambda i: (i, 0)),
                   pl.BlockSpec((LANES, LANES), lambda i: (0, 0))],
        out_shape=[jax.ShapeDtypeStruct((TOP_K, t), I32), jax.ShapeDtypeStruct((TOP_K, t), I32),
                   jax.ShapeDtypeStruct((t, LANES), F32), jax.ShapeDtypeStruct((LANES, LANES), F32)],
        scratch_shapes=[pltpu.VMEM((LANES, LANES), F32)],
        compiler_params=_cparams("arbitrary"),
        name="route",
    )(h2, router_wt, bias_col)


def _plan_kernel(idx_ref, rank_ref, cnt_ref, pos_ref, te_ref, nu_ref, *, group_tile):
    tf = idx_ref.shape[1]
    nt = te_ref.shape[1]
    cnt = cnt_ref[...].astype(I32)
    padded = (((cnt + (group_tile - 1)) // group_tile) * group_tile).astype(F32)
    e_sub = lax.broadcasted_iota(I32, (LANES, LANES), 0)
    e_lane = lax.broadcasted_iota(I32, (LANES, LANES), 1)
    base = jnp.sum(jnp.where(e_lane < e_sub, padded.T, 0.0), axis=1, keepdims=True)
    end = base + padded[:, 0:1]

    idx = idx_ref[...]
    start = jnp.zeros(idx.shape, F32)
    for e in range(N_EXPERTS):
        start = jnp.where(idx == e, base[e:e + 1, 0:1], start)
    pos = start.astype(I32) + rank_ref[...]
    for j in range(tf // SC_CHUNK):
        pos_ref[j] = pos[:, j * SC_CHUNK:(j + 1) * SC_CHUNK]

    tile_start = (lax.broadcasted_iota(I32, (N_EXPERTS, nt), 1) * group_tile).astype(F32)
    done = jnp.sum(jnp.where(end[:N_EXPERTS] <= tile_start, 1.0, 0.0), axis=0, keepdims=True)
    te_ref[...] = jnp.minimum(done, N_EXPERTS - 1.0).astype(I32)
    total = jnp.sum(padded[:, 0:1], axis=0, keepdims=True)
    nu_ref[...] = jnp.broadcast_to(total * (1.0 / group_tile), nu_ref.shape).astype(I32)


def _plan(idx, rank, counts, *, n_tiles, tf, group_tile):
    t = idx.shape[1]
    nt_pad = -(-n_tiles // LANES) * LANES
    krow = pl.BlockSpec((TOP_K, tf), lambda i: (0, i))
    return pl.pallas_call(
        functools.partial(_plan_kernel, group_tile=group_tile),
        grid=(t // tf,),
        in_specs=[krow, krow, pl.BlockSpec((LANES, LANES), lambda i: (0, 0))],
        out_specs=[pl.BlockSpec((tf // SC_CHUNK, TOP_K, SC_CHUNK), lambda i: (i, 0, 0)),
                   pl.BlockSpec((1, nt_pad), lambda i: (0, 0)),
                   pl.BlockSpec((1, LANES), lambda i: (0, 0))],
        out_shape=[jax.ShapeDtypeStruct((t // SC_CHUNK, TOP_K, SC_CHUNK), I32),
                   jax.ShapeDtypeStruct((1, nt_pad), I32), jax.ShapeDtypeStruct((1, LANES), I32)],
        compiler_params=_cparams("arbitrary"),
        name="moe_plan",
    )(idx, rank, counts)


def _sc_mesh_info():
    info = plsc.get_sparse_core_info()
    mesh = plsc.VectorSubcoreMesh(core_axis_name="c", subcore_axis_name="s")
    return mesh, info.num_cores, info.num_cores * info.num_subcores


def _sc_dispatch(pieces, pos, *, n_rows):
    t = pieces[0].shape[0]
    mesh, n_cores, n_workers = _sc_mesh_info()
    per_w = t // SC_CHUNK // n_workers

    @functools.partial(
        pl.kernel, mesh=mesh,
        out_type=[jax.ShapeDtypeStruct((n_rows, LANES), I32)] * N_PIECES,
        scratch_types=[pltpu.VMEM((TOP_K, SC_CHUNK), I32),
                       pltpu.VMEM((N_PIECES, SC_CHUNK, LANES), I32),
                       pltpu.SemaphoreType.DMA((N_PIECES,)),
                       pltpu.SemaphoreType.DMA],
        name="sc_dispatch",
    )
    def run(*refs):
        src = refs[:N_PIECES]
        pos_hbm = refs[N_PIECES]
        dst = refs[N_PIECES + 1:2 * N_PIECES + 1]
        idx_v, rows_v, load_sem, put_sem = refs[2 * N_PIECES + 1:]
        wid = lax.axis_index("s") * n_cores + lax.axis_index("c")

        @pl.loop(0, per_w)
        def _(j):
            ch = wid * per_w + j
            t0 = pl.multiple_of(ch * SC_CHUNK, SC_CHUNK)
            loads = [pltpu.make_async_copy(src[c].at[pl.ds(t0, SC_CHUNK)], rows_v.at[c], load_sem.at[c])
                     for c in range(N_PIECES)]
            for ld in loads:
                ld.start()
            pltpu.sync_copy(pos_hbm.at[ch], idx_v)
            puts = []
            for c in range(N_PIECES):
                loads[c].wait()
                for k in range(TOP_K):
                    puts.append(pltpu.make_async_copy(rows_v.at[c], dst[c].at[idx_v.at[k]], put_sem))
                    puts[-1].start()
            for cp in puts:
                cp.wait()

    return run(*pieces, pos)


def _sc_collect(pieces, pos, *, n_tokens):
    mesh, n_cores, n_workers = _sc_mesh_info()
    per_w = n_tokens // SC_CHUNK // n_workers
    n_buf = 4
    lag = n_buf // 2
    units = [(k, c) for k in range(TOP_K) for c in range(N_PIECES)]

    @functools.partial(
        pl.kernel, mesh=mesh,
        out_type=[jax.ShapeDtypeStruct((TOP_K, n_tokens, LANES), I32)] * N_PIECES,
        scratch_types=[pltpu.VMEM((TOP_K, SC_CHUNK), I32),
                       pltpu.VMEM((n_buf, SC_CHUNK, LANES), I32),
                       pltpu.SemaphoreType.DMA((n_buf,)),
                       pltpu.SemaphoreType.DMA((n_buf,))],
        name="sc_collect",
    )
    def run(*refs):
        src = refs[:N_PIECES]
        pos_hbm = refs[N_PIECES]
        dst = refs[N_PIECES + 1:2 * N_PIECES + 1]
        idx_v, buf, get_sem, put_sem = refs[2 * N_PIECES + 1:]
        wid = lax.axis_index("s") * n_cores + lax.axis_index("c")

        @pl.loop(0, per_w)
        def _(j):
            ch = wid * per_w + j
            t0 = pl.multiple_of(ch * SC_CHUNK, SC_CHUNK)
            pltpu.sync_copy(pos_hbm.at[ch], idx_v)

            def get(u):
                k, c = units[u]
                return pltpu.make_async_copy(src[c].at[idx_v.at[k]], buf.at[u % n_buf], get_sem.at[u % n_buf])

            def put(u):
                k, c = units[u]
                return pltpu.make_async_copy(buf.at[u % n_buf], dst[c].at[k, pl.ds(t0, SC_CHUNK)],
                                             put_sem.at[u % n_buf])

            n = len(units)
            for u in range(n + lag):
                if u < n:
                    if u >= n_buf:
                        put(u - n_buf).wait()
                    get(u).start()
                if 0 <= u - lag < n:
                    get(u - lag).wait()
                    put(u - lag).start()
            for u in range(n - n_buf, n):
                put(u).wait()

    return run(*pieces, pos)


def _group_tile(n_tokens):
    per_expert = n_tokens * TOP_K // N_EXPERTS
    return max(MXU_DIM, min(2 * MXU_DIM, per_expert // 4 // MXU_DIM * MXU_DIM))


def _experts_kernel(te_ref, nu_ref, *refs):
    x_refs = refs[:N_PIECES]
    wg_ref, wu_ref, wd_ref = refs[N_PIECES:N_PIECES + 3]
    y_refs = refs[N_PIECES + 3:]

    @pl.when(pl.program_id(0) < nu_ref[0])
    def _():
        lo, hi = _unpack_rows([r[...] for r in x_refs])
        x = jnp.concatenate(lo + hi, axis=1).astype(BF16)
        hid = _silu(_dot(x, wg_ref[...])) * _dot(x, wu_ref[...])
        y = _dot(hid.astype(BF16), wd_ref[...])
        for ref, piece in zip(y_refs, _pack_rows(y)):
            ref[...] = piece


def _experts(x_pieces, tile_expert, n_used, wg, wu, wd, *, group_tile):
    n_rows = x_pieces[0].shape[0]
    n_tiles = n_rows // group_tile

    def tile(i, te, nu):
        return jnp.minimum(i, nu[0] - 1)

    row = pl.BlockSpec((group_tile, LANES), lambda i, te, nu: (tile(i, te, nu), 0))
    wspec = lambda a, b: pl.BlockSpec((None, a, b), lambda i, te, nu: (te[tile(i, te, nu)], 0, 0))
    return pl.pallas_call(
        _experts_kernel,
        grid_spec=pltpu.PrefetchScalarGridSpec(
            num_scalar_prefetch=2,
            grid=(n_tiles,),
            in_specs=[row] * N_PIECES + [wspec(D_MODEL, D_EXPERT), wspec(D_MODEL, D_EXPERT),
                                         wspec(D_EXPERT, D_MODEL)],
            out_specs=[row] * N_PIECES),
        out_shape=[jax.ShapeDtypeStruct((n_rows, LANES), I32)] * N_PIECES,
        compiler_params=_cparams("arbitrary"),
        name="experts",
    )(tile_expert, n_used, *x_pieces, wg, wu, wd)


def _moe_out_kernel(h_ref, wt_ref, x1_ref, mod_ref, fg_ref, sg_ref, su_ref, sd_ref, *refs):
    y_refs = refs[:N_PIECES]
    out_ref = refs[N_PIECES]
    h = h_ref[...]
    hid = _silu(_dot(h, sg_ref[...])) * _dot(h, su_ref[...])
    shared = _dot(hid.astype(BF16), sd_ref[...])
    wt = wt_ref[...]
    lo_acc = [None] * N_PIECES
    hi_acc = [None] * N_PIECES
    for k in range(TOP_K):
        wk = wt[:, k:k + 1]
        lo, hi = _unpack_rows([r[k] for r in y_refs])
        for c in range(N_PIECES):
            lo_acc[c] = wk * lo[c] if k == 0 else lo_acc[c] + wk * lo[c]
            hi_acc[c] = wk * hi[c] if k == 0 else hi_acc[c] + wk * hi[c]
    routed = jnp.concatenate(lo_acc + hi_acc, axis=1)
    x2 = x1_ref[...] + mod_ref[0, 5:6, :] * (shared + routed)
    out_ref[...] = x2 * lax.rsqrt(jnp.mean(x2 * x2, axis=-1, keepdims=True) + EPS) * fg_ref[...]


def _moe_out(h2, wt, x1, mods, final_g, sg, su, sd, y_pieces, *, tokens_per_mod, tm):
    t = h2.shape[0]
    tiles_per_mod = tokens_per_mod // tm
    row = lambda w: pl.BlockSpec((tm, w), lambda i: (i, 0))
    full = lambda a: pl.BlockSpec(a.shape, lambda i: (0,) * a.ndim)
    return pl.pallas_call(
        _moe_out_kernel,
        grid=(t // tm,),
        in_specs=[row(D_MODEL), row(LANES), row(D_MODEL),
                  pl.BlockSpec((1, 6, D_MODEL), lambda i: (i // tiles_per_mod, 0, 0)),
                  full(final_g), full(sg), full(su), full(sd)]
        + [pl.BlockSpec((TOP_K, tm, LANES), lambda i: (0, i, 0))] * N_PIECES,
        out_specs=row(D_MODEL),
        out_shape=jax.ShapeDtypeStruct((t, D_MODEL), F32),
        compiler_params=_cparams("parallel"),
        name="moe_out",
    )(h2, wt, x1, mods, final_g, sg, su, sd, *y_pieces)


def _trunk(x, mods, s0f, s0b, w, *, batch, seq_len, on_grid):
    t = batch * seq_len
    tokens_per_mod = t // mods.shape[0]
    cos_t, sin_t = _rope_tables(seq_len)
    q, k, v, gsw, up, ga, gb = _inproj(x, mods, w["norm1_g"], w["w_in"], cos_t, sin_t,
                                       tokens_per_mod=tokens_per_mod, seq_len=seq_len,
                                       on_grid=on_grid, tm=256)
    z, s_f, s_b = _retention(q, k, v, gsw, w["dec"], s0f, s0b, batch=batch, seq_len=seq_len)
    p = _pool(up, w["pool_w"], w["pool_scale"], batch=batch, seq_len=seq_len, on_grid=on_grid)
    x1, h2, h2_pieces = _merge(x, z, p, ga, gb, mods, w["norm2_g"], w["w_br_ret"], w["w_br_pool"],
                               w["w_out"], tokens_per_mod=tokens_per_mod, tm=256)

    group_tile = _group_tile(t)
    n_rows = t * TOP_K + N_EXPERTS * group_tile
    idx, rank, wt, counts = _route(h2, w["router_wt"], w["router_bias"], tm=512)
    pos, tile_expert, n_used = _plan(idx, rank, counts, n_tiles=n_rows // group_tile, tf=512,
                                     group_tile=group_tile)
    x_sorted = _sc_dispatch(h2_pieces, pos, n_rows=n_rows)
    y_sorted = _experts(x_sorted, tile_expert.reshape(-1), n_used.reshape(-1),
                        w["exp_w_gate"], w["exp_w_up"], w["exp_w_down"], group_tile=group_tile)
    y_tok = _sc_collect(y_sorted, pos, n_tokens=t)
    y = _moe_out(h2, wt, x1, mods, w["final_g"], w["sh_w_gate"], w["sh_w_up"], w["sh_w_down"], y_tok,
                 tokens_per_mod=tokens_per_mod, tm=256)
    return y, s_f, s_b


def kernel(x_prompt, x_sample, state_ret_fwd, state_ret_bwd, c, c_ctx, ada_w, ada_b, norm1_g, norm2_g, w_in,
           ret_decay_fwd, ret_decay_bwd, w_br_ret, pool_w, pool_scale, w_br_pool, w_out, router_w, router_bias,
           exp_w_gate, exp_w_up, exp_w_down, sh_w_gate, sh_w_up, sh_w_down, final_norm_g):
    n_req, seq, d = x_prompt.shape
    n_dec, dec_seq, _ = x_sample.shape
    depth = ada_w.shape[0]
    assert depth == 1 and d == D_MODEL

    xc = x_prompt.reshape(n_req * seq, d)
    xs = x_sample.reshape(n_dec * dec_seq, d)
    zero_state = jnp.zeros((n_req, RET_HEADS, RET_DK, RET_DV), F32)
    new_f, new_b = [], []
    for l in range(depth):
        c_rows = jnp.concatenate([c_ctx[None, :], c, jnp.zeros((8 - 1 - n_dec, d), F32)], axis=0)
        mods = _ada(c_rows, ada_w[l], ada_b[l]).reshape(8, 6, d)
        pad_rows = LANES - N_EXPERTS
        w = dict(
            norm1_g=norm1_g[l].reshape(1, d), norm2_g=norm2_g[l].reshape(1, d),
            final_g=final_norm_g.reshape(1, d),
            w_in=w_in[l].astype(BF16),
            dec=jnp.stack([ret_decay_fwd[l], ret_decay_bwd[l]]).astype(F32),
            w_br_ret=w_br_ret[l].astype(BF16), pool_w=pool_w[l].astype(BF16),
            pool_scale=pool_scale[l].reshape(1, POOL_W), w_br_pool=w_br_pool[l].astype(BF16),
            w_out=w_out[l].astype(BF16),
            router_wt=jnp.pad(router_w[l].T, ((0, pad_rows), (0, 0))).astype(BF16),
            router_bias=jnp.pad(router_bias[l].astype(F32).reshape(N_EXPERTS, 1), ((0, pad_rows), (0, 0))),
            exp_w_gate=exp_w_gate[l].astype(BF16), exp_w_up=exp_w_up[l].astype(BF16),
            exp_w_down=exp_w_down[l].astype(BF16), sh_w_gate=sh_w_gate[l].astype(BF16),
            sh_w_up=sh_w_up[l].astype(BF16), sh_w_down=sh_w_down[l].astype(BF16),
        )
        xc, s_f, s_b = _trunk(xc, mods[0:1], zero_state, zero_state, w,
                              batch=n_req, seq_len=seq, on_grid=False)
        new_f.append(s_f)
        new_b.append(s_b)
        xs, _, _ = _trunk(xs, mods[1:1 + n_dec], state_ret_fwd[:, l].astype(F32),
                          state_ret_bwd[:, l].astype(F32), w,
                          batch=n_dec, seq_len=dec_seq, on_grid=True)
    y_prompt = xc.reshape(n_req, seq, d)
    y_sample = xs.reshape(n_dec, dec_seq, d)
    return (y_prompt, y_sample, jnp.stack(new_f, axis=1).astype(x_prompt.dtype),
            jnp.stack(new_b, axis=1).astype(x_prompt.dtype))
```

```python
import functools

import jax
import jax.numpy as jnp
from jax import lax
from jax.experimental import pallas as pl
from jax.experimental.pallas import tpu as pltpu
from jax.experimental.pallas import tpu_sc as plsc

D_MODEL = 1024
GRID_W = 64
RET_HEADS = 4
RET_DK = 128
RET_DV = 256
RET_QK_W = RET_HEADS * RET_DK
RET_V_W = RET_HEADS * RET_DV
RET_CHUNK = 128
ROPE_BASE = 10000.0
POOL_GROUPS = 4
POOL_CH = 128
POOL_W = POOL_GROUPS * POOL_CH
POOL_WINDOWS = (2, 4, 8, 16)
N_EXPERTS = 64
TOP_K = 8
N_EXPERT_GROUPS = 8
GROUP_SIZE = N_EXPERTS // N_EXPERT_GROUPS
TOPK_GROUPS = 4
D_EXPERT = 256
ROUTED_SCALE = 2.5
EPS = 1e-6
IN_SIZES = (RET_QK_W, RET_QK_W, RET_V_W, RET_V_W, POOL_W, D_MODEL, D_MODEL)
IN_OFFS = tuple(sum(IN_SIZES[:i]) for i in range(len(IN_SIZES) + 1))
IN_W = IN_OFFS[-1]

LANES = 128
VMEM_LIMIT = 56 << 20
N_PIECES = D_MODEL // 2 // LANES
MXU_DIM = 256
SC_CHUNK = 128

F32 = jnp.float32
BF16 = jnp.bfloat16
I32 = jnp.int32
U32 = jnp.uint32


def _cparams(*sem):
    return pltpu.CompilerParams(dimension_semantics=sem, vmem_limit_bytes=VMEM_LIMIT)


def _dot(a, b):
    return jnp.dot(a, b, preferred_element_type=F32)


def _silu(x):
    return x * jax.nn.sigmoid(x)


def _rms_mod(x, g, scale, shift):
    y = x * lax.rsqrt(jnp.mean(x * x, axis=-1, keepdims=True) + EPS)
    return (y * g) * (1.0 + scale) + shift


def _ada_kernel(c_ref, w_ref, b_ref, o_ref):
    c = c_ref[...]
    o_ref[...] = jnp.dot(_silu(c), w_ref[...], preferred_element_type=F32,
                         precision=lax.Precision.HIGHEST) + b_ref[...]


def _ada(c_rows, ada_w, ada_b):
    r = c_rows.shape[0]
    n = ada_w.shape[1]
    tn = D_MODEL
    return pl.pallas_call(
        _ada_kernel,
        grid=(n // tn,),
        in_specs=[pl.BlockSpec((r, D_MODEL), lambda j: (0, 0)),
                  pl.BlockSpec((D_MODEL, tn), lambda j: (0, j)),
                  pl.BlockSpec((1, tn), lambda j: (0, j))],
        out_specs=pl.BlockSpec((r, tn), lambda j: (0, j)),
        out_shape=jax.ShapeDtypeStruct((r, n), F32),
        compiler_params=_cparams("parallel"),
        name="ada_mod",
    )(c_rows, ada_w, ada_b.reshape(1, n))


def _inproj_kernel(x_ref, mod_ref, g_ref, w_ref, cos_ref, sin_ref,
                   q_ref, k_ref, v_ref, gsw_ref, up_ref, ga_ref, gb_ref, *, on_grid):
    x = x_ref[...]
    h = _rms_mod(x, g_ref[...], mod_ref[0, 1:2, :], mod_ref[0, 0:1, :]).astype(BF16)

    def seg(i):
        return _dot(h, w_ref[:, IN_OFFS[i]:IN_OFFS[i + 1]])

    q = seg(0)
    k = seg(1)
    if on_grid:
        cos = jnp.concatenate([cos_ref[...]] * RET_HEADS, axis=1)
        sin = jnp.concatenate([sin_ref[...]] * RET_HEADS, axis=1)
        lane = lax.broadcasted_iota(jnp.int32, q.shape, 1)
        first = (lane & 63) < 32

        def rope(a):
            up = pltpu.roll(a, RET_QK_W - 32, axis=1)
            dn = pltpu.roll(a, 32, axis=1)
            return a * cos + jnp.where(first, up, dn) * sin

        q = rope(q)
        k = rope(k)
    q_ref[...] = q.astype(BF16)
    k_ref[...] = (k * (RET_DK ** -0.5)).astype(BF16)
    v_ref[...] = seg(2).astype(BF16)
    gsw_ref[...] = seg(3).astype(BF16)
    up_ref[...] = seg(4).astype(BF16)
    ga_ref[...] = seg(5).astype(BF16)
    gb_ref[...] = seg(6).astype(BF16)


def _inproj(x, mods, norm_g, w_in, cos_t, sin_t, *, tokens_per_mod, seq_len, on_grid, tm):
    t = x.shape[0]
    tiles_per_mod = tokens_per_mod // tm
    tiles_per_seq = seq_len // tm
    widths = IN_SIZES
    out_shape = [jax.ShapeDtypeStruct((t, w), BF16) for w in widths]
    out_specs = [pl.BlockSpec((tm, w), lambda i: (i, 0)) for w in widths]
    return pl.pallas_call(
        functools.partial(_inproj_kernel, on_grid=on_grid),
        grid=(t // tm,),
        in_specs=[pl.BlockSpec((tm, D_MODEL), lambda i: (i, 0)),
                  pl.BlockSpec((1, 6, D_MODEL), lambda i: (i // tiles_per_mod, 0, 0)),
                  pl.BlockSpec((1, D_MODEL), lambda i: (0, 0)),
                  pl.BlockSpec((D_MODEL, IN_W), lambda i: (0, 0)),
                  pl.BlockSpec((tm, RET_DK), lambda i: (i % tiles_per_seq, 0)),
                  pl.BlockSpec((tm, RET_DK), lambda i: (i % tiles_per_seq, 0))],
        out_specs=out_specs,
        out_shape=out_shape,
        compiler_params=_cparams("parallel"),
        name="inproj_grid" if on_grid else "inproj_seq",
    )(x, mods, norm_g, w_in, cos_t, sin_t)


def _rope_tables(seq_len):
    t = jnp.arange(seq_len)
    row = (t // GRID_W).astype(F32)
    col = (t % GRID_W).astype(F32)
    m = RET_DK // 4
    inv = ROPE_BASE ** (-jnp.arange(m, dtype=F32) / m)
    ar = row[:, None] * inv
    ac = col[:, None] * inv
    cos = jnp.concatenate([jnp.cos(ar), jnp.cos(ar), jnp.cos(ac), jnp.cos(ac)], axis=1)
    sin = jnp.concatenate([-jnp.sin(ar), jnp.sin(ar), -jnp.sin(ac), jnp.sin(ac)], axis=1)
    return cos.astype(F32), sin.astype(F32)


RET_HEADS_PER_STEP = 2


def _ret_kernel(dec_ref, q_ref, k_ref, v_ref, g_ref, s0f_ref, s0b_ref,
                z_ref, sf_ref, sb_ref, oacc_ref, kt_ref, *, n_chunks):
    c = RET_CHUNK
    heads = RET_HEADS_PER_STEP
    half = n_chunks // 2
    ii = lax.broadcasted_iota(I32, (c, c), 0)
    jj = lax.broadcasted_iota(I32, (c, c), 1)
    ik = lax.broadcasted_iota(I32, (c, RET_DK), 0).astype(F32)
    jk = lax.broadcasted_iota(I32, (RET_DK, c), 1).astype(F32)

    def log_gamma(d, shape):
        return jnp.log1p(-jnp.exp2(-jnp.full(shape, d, F32)))

    consts = {}
    for hh in range(heads):
        h = pl.program_id(1) * heads + hh
        dec_f = dec_ref[0, h]
        dec_b = dec_ref[1, h]
        rel = (ii - jj).astype(F32)
        consts[hh, "f"] = (
            jnp.where(rel >= 0, jnp.exp(log_gamma(dec_f, (c, c)) * jnp.maximum(rel, 0.0)), 0.0),
            jnp.exp(log_gamma(dec_f, (c, RET_DK)) * (ik + 1.0)),
            jnp.exp(log_gamma(dec_f, (RET_DK, c)) * (c - 1.0 - jk)),
            jnp.exp(log_gamma(dec_f, (RET_DK, RET_DV)) * c))
        consts[hh, "b"] = (
            jnp.where(rel <= 0, jnp.exp(log_gamma(dec_b, (c, c)) * jnp.maximum(-rel, 0.0)), 0.0),
            jnp.exp(log_gamma(dec_b, (c, RET_DK)) * (c - ik)),
            jnp.exp(log_gamma(dec_b, (RET_DK, c)) * jk),
            jnp.exp(log_gamma(dec_b, (RET_DK, RET_DV)) * c))

    sf_ref[...] = s0f_ref[...]
    sb_ref[...] = s0b_ref[...]

    def transpose_keys(ci, carry):
        r = pl.ds(pl.multiple_of(ci * c, c), c)
        for hh in range(heads):
            kt_ref[hh, ci] = k_ref[r, hh * RET_DK:(hh + 1) * RET_DK].T
        return carry

    lax.fori_loop(0, n_chunks, transpose_keys, 0)

    def scan_chunk(ci, hh, direction, second):
        dmask, qdec, kdec, cdec = consts[hh, direction]
        s_ref = sf_ref if direction == "f" else sb_ref
        r = pl.ds(pl.multiple_of(ci * c, c), c)
        kcols = slice(hh * RET_DK, (hh + 1) * RET_DK)
        vcols = slice(hh * RET_DV, (hh + 1) * RET_DV)
        qc = q_ref[r, kcols]
        kc = k_ref[r, kcols]
        vc = v_ref[r, vcols]
        s = s_ref[hh]
        sc = lax.dot_general(qc, kc, (((1,), (1,)), ((), ())), preferred_element_type=F32)
        lhs = jnp.concatenate([(sc * dmask).astype(BF16), (qc.astype(F32) * qdec).astype(BF16)], axis=1)
        o = _dot(lhs, jnp.concatenate([vc, s.astype(BF16)], axis=0))
        kd_t = (kt_ref[hh, ci].astype(F32) * kdec).astype(BF16)
        s_ref[hh] = s * cdec + _dot(kd_t, vc)
        if not second:
            oacc_ref[hh, r, :] = o
        else:
            o = o + oacc_ref[hh, r, :]
            o = o * lax.rsqrt(jnp.mean(o * o, axis=-1, keepdims=True) + EPS)
            g = g_ref[r, vcols].astype(F32)
            z_ref[r, vcols] = (_silu(g) * o).astype(BF16)

    def body(second):
        def run(t, carry):
            for hh in range(heads):
                scan_chunk(t, hh, "f", second)
                scan_chunk(n_chunks - 1 - t, hh, "b", second)
            return carry
        return run

    lax.fori_loop(0, half, body(False), 0)
    lax.fori_loop(half, n_chunks, body(True), 0)


def _retention(q, k, v, gsw, dec, s0f, s0b, *, batch, seq_len):
    n_chunks = seq_len // RET_CHUNK
    assert n_chunks % 2 == 0
    heads = RET_HEADS_PER_STEP
    t = batch * seq_len
    st_spec = pl.BlockSpec((None, heads, RET_DK, RET_DV), lambda b, h: (b, h, 0, 0))
    st_shape = jax.ShapeDtypeStruct((batch, RET_HEADS, RET_DK, RET_DV), F32)
    kspec = pl.BlockSpec((seq_len, heads * RET_DK), lambda b, h: (b, h))
    vspec = pl.BlockSpec((seq_len, heads * RET_DV), lambda b, h: (b, h))
    return pl.pallas_call(
        functools.partial(_ret_kernel, n_chunks=n_chunks),
        grid=(batch, RET_HEADS // heads),
        in_specs=[pl.BlockSpec(memory_space=pltpu.SMEM), kspec, kspec, vspec, vspec, st_spec, st_spec],
        out_specs=[vspec, st_spec, st_spec],
        out_shape=[jax.ShapeDtypeStruct((t, RET_V_W), BF16), st_shape, st_shape],
        scratch_shapes=[pltpu.VMEM((heads, seq_len, RET_DV), F32),
                        pltpu.VMEM((heads, n_chunks, RET_DK, RET_CHUNK), BF16)],
        compiler_params=_cparams("parallel", "parallel"),
        name=f"retention_l{seq_len}",
    )(dec, q, k, v, gsw, s0f, s0b)


def _pool_kernel(u_ref, w_ref, sc_ref, o_ref, *, seq_len, on_grid):
    tok = lax.broadcasted_iota(jnp.int32, (seq_len, POOL_CH), 0)

    def shift(a, s, stride, pos, width):
        y = pltpu.roll(a, (-s * stride) % seq_len, axis=0)
        ok = (pos < width - s) if s > 0 else (pos >= -s)
        return jnp.where(ok, y, 0.0)

    def box_mean(a, window, stride, pos, width):
        half = window // 2
        fw = a
        bw = shift(a, -1, stride, pos, width)
        m = 1
        while m < half:
            fw = fw + shift(fw, m, stride, pos, width)
            bw = bw + shift(bw, -m, stride, pos, width)
            m *= 2
        cnt = jnp.minimum(pos + half, width) - jnp.maximum(pos - half, 0)
        return (fw + bw) / cnt.astype(F32)

    for g, window in enumerate(POOL_WINDOWS):
        cols = slice(g * POOL_CH, (g + 1) * POOL_CH)
        ug = u_ref[:, cols].astype(F32)
        if on_grid:
            pooled = box_mean(ug, window, 1, tok & (GRID_W - 1), GRID_W)
            pooled = box_mean(pooled, window, GRID_W, tok >> 6, seq_len // GRID_W)
        else:
            pooled = box_mean(ug, window, 1, tok, seq_len)
        d = (pooled - ug).astype(BF16)
        o_ref[:, cols] = (_dot(d, w_ref[g]) * sc_ref[:, cols]).astype(BF16)


def _pool(u, pool_w, pool_scale, *, batch, seq_len, on_grid):
    t = batch * seq_len
    return pl.pallas_call(
        functools.partial(_pool_kernel, seq_len=seq_len, on_grid=on_grid),
        grid=(batch,),
        in_specs=[pl.BlockSpec((seq_len, POOL_W), lambda b: (b, 0)),
                  pl.BlockSpec((POOL_GROUPS, POOL_CH, POOL_CH), lambda b: (0, 0, 0)),
                  pl.BlockSpec((1, POOL_W), lambda b: (0, 0))],
        out_specs=pl.BlockSpec((seq_len, POOL_W), lambda b: (b, 0)),
        out_shape=jax.ShapeDtypeStruct((t, POOL_W), BF16),
        compiler_params=_cparams("parallel"),
        name=f"pool_l{seq_len}",
    )(u, pool_w, pool_scale)


def _pack_rows(x):
    half = D_MODEL // 2
    lo = lax.bitcast_convert_type(x[:, :half].astype(BF16).astype(F32), U32) >> 16
    hi = lax.bitcast_convert_type(x[:, half:].astype(BF16).astype(F32), U32) & jnp.uint32(0xFFFF0000)
    word = lax.bitcast_convert_type(hi | lo, I32)
    return [word[:, c * LANES:(c + 1) * LANES] for c in range(N_PIECES)]


def _unpack_rows(pieces):
    words = [lax.bitcast_convert_type(p, U32) for p in pieces]
    lo = [lax.bitcast_convert_type(w << 16, F32) for w in words]
    hi = [lax.bitcast_convert_type(w & jnp.uint32(0xFFFF0000), F32) for w in words]
    return lo, hi


def _merge_kernel(x_ref, z_ref, p_ref, ga_ref, gb_ref, mod_ref, g2_ref, wr_ref, wp_ref, wo_ref,
                  x1_ref, h2_ref, *piece_refs):
    y_ret = _dot(z_ref[...], wr_ref[...])
    y_pool = _dot(p_ref[...], wp_ref[...])
    merged = (jax.nn.sigmoid(ga_ref[...].astype(F32)) * y_ret
              + jax.nn.sigmoid(gb_ref[...].astype(F32)) * y_pool)
    x1 = x_ref[...] + mod_ref[0, 2:3, :] * _dot(merged.astype(BF16), wo_ref[...])
    x1_ref[...] = x1
    h2 = _rms_mod(x1, g2_ref[...], mod_ref[0, 4:5, :], mod_ref[0, 3:4, :])
    h2_ref[...] = h2.astype(BF16)
    for ref, piece in zip(piece_refs, _pack_rows(h2)):
        ref[...] = piece


def _merge(x, z, p, ga, gb, mods, norm2_g, w_br_ret, w_br_pool, w_out, *, tokens_per_mod, tm):
    t = x.shape[0]
    tiles_per_mod = tokens_per_mod // tm
    row = lambda w: pl.BlockSpec((tm, w), lambda i: (i, 0))
    full = lambda a: pl.BlockSpec(a.shape, lambda i: (0,) * a.ndim)
    outs = pl.pallas_call(
        _merge_kernel,
        grid=(t // tm,),
        in_specs=[row(D_MODEL), row(RET_V_W), row(POOL_W), row(D_MODEL), row(D_MODEL),
                  pl.BlockSpec((1, 6, D_MODEL), lambda i: (i // tiles_per_mod, 0, 0)),
                  full(norm2_g), full(w_br_ret), full(w_br_pool), full(w_out)],
        out_specs=[row(D_MODEL), row(D_MODEL)] + [row(LANES)] * N_PIECES,
        out_shape=[jax.ShapeDtypeStruct((t, D_MODEL), F32), jax.ShapeDtypeStruct((t, D_MODEL), BF16)]
        + [jax.ShapeDtypeStruct((t, LANES), I32)] * N_PIECES,
        compiler_params=_cparams("parallel"),
        name="merge",
    )(x, z, p, ga, gb, mods, norm2_g, w_br_ret, w_br_pool, w_out)
    return outs[0], outs[1], outs[2:]


def _route_kernel(h_ref, rw_ref, bias_ref, idx_ref, rank_ref, wt_ref, cnt_ref, carry_ref):
    e = N_EXPERTS
    tm = h_ref.shape[0]
    neg = -jnp.inf

    @pl.when(pl.program_id(0) == 0)
    def _():
        carry_ref[...] = jnp.zeros(carry_ref.shape, F32)

    logits = lax.dot_general(rw_ref[...], h_ref[...], (((1,), (1,)), ((), ())),
                             preferred_element_type=F32)[:e]
    scores = jax.nn.sigmoid(logits)
    sel = scores + bias_ref[:e, 0:1]
    e_idx = lax.broadcasted_iota(I32, (e, tm), 0)

    grp = sel.reshape(N_EXPERT_GROUPS, GROUP_SIZE, tm)
    m_idx = lax.broadcasted_iota(I32, grp.shape, 1)
    m1 = jnp.max(grp, axis=1, keepdims=True)
    first = jnp.min(jnp.where(grp == m1, m_idx, GROUP_SIZE), axis=1, keepdims=True)
    m2 = jnp.max(jnp.where(m_idx == first, neg, grp), axis=1, keepdims=True)
    gscore = (m1 + m2).reshape(N_EXPERT_GROUPS, tm)

    g_idx = lax.broadcasted_iota(I32, gscore.shape, 0)
    grank = jnp.zeros(gscore.shape, I32)
    for g in range(N_EXPERT_GROUPS):
        other = gscore[g:g + 1, :]
        beats = jnp.where(other > gscore, 1, jnp.where(other == gscore, (g_idx > g).astype(I32), 0))
        grank = grank + beats
    gkeep = (grank < TOPK_GROUPS).astype(F32)
    ekeep = jnp.broadcast_to(gkeep.reshape(N_EXPERT_GROUPS, 1, tm), grp.shape).reshape(e, tm)
    masked = jnp.where(ekeep > 0, sel, neg)

    chosen = jnp.zeros((e, tm), F32)
    picks, hits = [], []
    for _ in range(TOP_K):
        m = jnp.max(masked, axis=0, keepdims=True)
        pick = jnp.min(jnp.where(masked == m, e_idx, e), axis=0, keepdims=True)
        hit = e_idx == pick
        chosen = jnp.where(hit, 1.0, chosen)
        masked = jnp.where(hit, neg, masked)
        picks.append(pick)
        hits.append(hit)

    w = scores * chosen
    comb = w / jnp.sum(w, axis=0, keepdims=True) * ROUTED_SCALE

    t_row = lax.broadcasted_iota(I32, (tm, tm), 0)
    t_col = lax.broadcasted_iota(I32, (tm, tm), 1)
    before = (t_row < t_col).astype(BF16)
    rankmat = _dot(chosen.astype(BF16), before) + carry_ref[:e, 0:1]
    carry_ref[:e, :] = carry_ref[:e, :] + jnp.sum(chosen, axis=1, keepdims=True)
    cnt_ref[...] = carry_ref[...]

    idx_ref[...] = jnp.concatenate(picks, axis=0)
    rank_ref[...] = jnp.concatenate(
        [jnp.sum(jnp.where(h, rankmat, 0.0), axis=0, keepdims=True) for h in hits], axis=0).astype(I32)
    w_rows = [jnp.sum(jnp.where(h, comb, 0.0), axis=0, keepdims=True) for h in hits]
    wt_ref[...] = jnp.concatenate(w_rows + [jnp.zeros((LANES - TOP_K, tm), F32)], axis=0).T


def _route(h2, router_wt, bias_col, *, tm):
    t = h2.shape[0]
    krow = pl.BlockSpec((TOP_K, tm), lambda i: (0, i))
    return pl.pallas_call(
        _route_kernel,
        grid=(t // tm,),
        in_specs=[pl.BlockSpec((tm, D_MODEL), lambda i: (i, 0)),
                  pl.BlockSpec((LANES, D_MODEL), lambda i: (0, 0)),
                  pl.BlockSpec((LANES, 1), lambda i: (0, 0))],
        out_specs=[krow, krow, pl.BlockSpec((tm, LANES), lambda i: (i, 0)),
                   pl.BlockSpec((LANES, LANES), lambda i: (0, 0))],
        out_shape=[jax.ShapeDtypeStruct((TOP_K, t), I32), jax.ShapeDtypeStruct((TOP_K, t), I32),
                   jax.ShapeDtypeStruct((t, LANES), F32), jax.ShapeDtypeStruct((LANES, LANES), F32)],
        scratch_shapes=[pltpu.VMEM((LANES, LANES), F32)],
        compiler_params=_cparams("arbitrary"),
        name="route",
    )(h2, router_wt, bias_col)


def _plan_kernel(idx_ref, rank_ref, cnt_ref, pos_ref, te_ref, nu_ref, *, group_tile):
    tf = idx_ref.shape[1]
    nt = te_ref.shape[1]
    cnt = cnt_ref[...].astype(I32)
    padded = (((cnt + (group_tile - 1)) // group_tile) * group_tile).astype(F32)
    e_sub = lax.broadcasted_iota(I32, (LANES, LANES), 0)
    e_lane = lax.broadcasted_iota(I32, (LANES, LANES), 1)
    base = jnp.sum(jnp.where(e_lane < e_sub, padded.T, 0.0), axis=1, keepdims=True)
    end = base + padded[:, 0:1]

    idx = idx_ref[...]
    start = jnp.zeros(idx.shape, F32)
    for e in range(N_EXPERTS):
        start = jnp.where(idx == e, base[e:e + 1, 0:1], start)
    pos = start.astype(I32) + rank_ref[...]
    for j in range(tf // SC_CHUNK):
        pos_ref[j] = pos[:, j * SC_CHUNK:(j + 1) * SC_CHUNK]

    tile_start = (lax.broadcasted_iota(I32, (N_EXPERTS, nt), 1) * group_tile).astype(F32)
    done = jnp.sum(jnp.where(end[:N_EXPERTS] <= tile_start, 1.0, 0.0), axis=0, keepdims=True)
    te_ref[...] = jnp.minimum(done, N_EXPERTS - 1.0).astype(I32)
    total = jnp.sum(padded[:, 0:1], axis=0, keepdims=True)
    nu_ref[...] = jnp.broadcast_to(total * (1.0 / group_tile), nu_ref.shape).astype(I32)


def _plan(idx, rank, counts, *, n_tiles, tf, group_tile):
    t = idx.shape[1]
    nt_pad = -(-n_tiles // LANES) * LANES
    krow = pl.BlockSpec((TOP_K, tf), lambda i: (0, i))
    return pl.pallas_call(
        functools.partial(_plan_kernel, group_tile=group_tile),
        grid=(t // tf,),
        in_specs=[krow, krow, pl.BlockSpec((LANES, LANES), lambda i: (0, 0))],
        out_specs=[pl.BlockSpec((tf // SC_CHUNK, TOP_K, SC_CHUNK), lambda i: (i, 0, 0)),
                   pl.BlockSpec((1, nt_pad), lambda i: (0, 0)),
                   pl.BlockSpec((1, LANES), lambda i: (0, 0))],
        out_shape=[jax.ShapeDtypeStruct((t // SC_CHUNK, TOP_K, SC_CHUNK), I32),
                   jax.ShapeDtypeStruct((1, nt_pad), I32), jax.ShapeDtypeStruct((1, LANES), I32)],
        compiler_params=_cparams("arbitrary"),
        name="moe_plan",
    )(idx, rank, counts)


def _sc_mesh_info():
    info = plsc.get_sparse_core_info()
    mesh = plsc.VectorSubcoreMesh(core_axis_name="c", subcore_axis_name="s")
    return mesh, info.num_cores, info.num_cores * info.num_subcores


def _sc_dispatch(pieces, pos, *, n_rows):
    t = pieces[0].shape[0]
    mesh, n_cores, n_workers = _sc_mesh_info()
    per_w = t // SC_CHUNK // n_workers

    @functools.partial(
        pl.kernel, mesh=mesh,
        out_type=[jax.ShapeDtypeStruct((n_rows, LANES), I32)] * N_PIECES,
        scratch_types=[pltpu.VMEM((TOP_K, SC_CHUNK), I32),
                       pltpu.VMEM((N_PIECES, SC_CHUNK, LANES), I32),
                       pltpu.SemaphoreType.DMA((N_PIECES,)),
                       pltpu.SemaphoreType.DMA],
        name="sc_dispatch",
    )
    def run(*refs):
        src = refs[:N_PIECES]
        pos_hbm = refs[N_PIECES]
        dst = refs[N_PIECES + 1:2 * N_PIECES + 1]
        idx_v, rows_v, load_sem, put_sem = refs[2 * N_PIECES + 1:]
        wid = lax.axis_index("s") * n_cores + lax.axis_index("c")

        @pl.loop(0, per_w)
        def _(j):
            ch = wid * per_w + j
            t0 = pl.multiple_of(ch * SC_CHUNK, SC_CHUNK)
            loads = [pltpu.make_async_copy(src[c].at[pl.ds(t0, SC_CHUNK)], rows_v.at[c], load_sem.at[c])
                     for c in range(N_PIECES)]
            for ld in loads:
                ld.start()
            pltpu.sync_copy(pos_hbm.at[ch], idx_v)
            puts = []
            for c in range(N_PIECES):
                loads[c].wait()
                for k in range(TOP_K):
                    puts.append(pltpu.make_async_copy(rows_v.at[c], dst[c].at[idx_v.at[k]], put_sem))
                    puts[-1].start()
            for cp in puts:
                cp.wait()

    return run(*pieces, pos)


def _sc_collect(pieces, pos, *, n_tokens):
    mesh, n_cores, n_workers = _sc_mesh_info()
    per_w = n_tokens // SC_CHUNK // n_workers
    n_buf = 4
    lag = n_buf // 2
    units = [(k, c) for k in range(TOP_K) for c in range(N_PIECES)]

    @functools.partial(
        pl.kernel, mesh=mesh,
        out_type=[jax.ShapeDtypeStruct((TOP_K, n_tokens, LANES), I32)] * N_PIECES,
        scratch_types=[pltpu.VMEM((TOP_K, SC_CHUNK), I32),
                       pltpu.VMEM((n_buf, SC_CHUNK, LANES), I32),
                       pltpu.SemaphoreType.DMA((n_buf,)),
                       pltpu.SemaphoreType.DMA((n_buf,))],
        name="sc_collect",
    )
    def run(*refs):
        src = refs[:N_PIECES]
        pos_hbm = refs[N_PIECES]
        dst = refs[N_PIECES + 1:2 * N_PIECES + 1]
        idx_v, buf, get_sem, put_sem = refs[2 * N_PIECES + 1:]
        wid = lax.axis_index("s") * n_cores + lax.axis_index("c")

        @pl.loop(0, per_w)
        def _(j):
            ch = wid * per_w + j
            t0 = pl.multiple_of(ch * SC_CHUNK, SC_CHUNK)
            pltpu.sync_copy(pos_hbm.at[ch], idx_v)

            def get(u):
                k, c = units[u]
                return pltpu.make_async_copy(src[c].at[idx_v.at[k]], buf.at[u % n_buf], get_sem.at[u % n_buf])

            def put(u):
                k, c = units[u]
                return pltpu.make_async_copy(buf.at[u % n_buf], dst[c].at[k, pl.ds(t0, SC_CHUNK)],
                                             put_sem.at[u % n_buf])

            n = len(units)
            for u in range(n + lag):
                if u < n:
                    if u >= n_buf:
                        put(u - n_buf).wait()
                    get(u).start()
                if 0 <= u - lag < n:
                    get(u - lag).wait()
                    put(u - lag).start()
            for u in range(n - n_buf, n):
                put(u).wait()

    return run(*pieces, pos)


def _group_tile(n_tokens):
    per_expert = n_tokens * TOP_K // N_EXPERTS
    return max(MXU_DIM, min(2 * MXU_DIM, per_expert // 4 // MXU_DIM * MXU_DIM))


def _experts_kernel(te_ref, nu_ref, *refs):
    x_refs = refs[:N_PIECES]
    wg_ref, wu_ref, wd_ref = refs[N_PIECES:N_PIECES + 3]
    y_refs = refs[N_PIECES + 3:2 * N_PIECES + 3]
    wg_s, wu_s, wd_s = refs[2 * N_PIECES + 3:]
    i = pl.program_id(0)

    @pl.when((i == 0) | (te_ref[i] != te_ref[jnp.maximum(i - 1, 0)]))
    def _():
        wg_s[...] = wg_ref[...].astype(BF16)
        wu_s[...] = wu_ref[...].astype(BF16)
        wd_s[...] = wd_ref[...].astype(BF16)

    @pl.when(i < nu_ref[0])
    def _():
        lo, hi = _unpack_rows([r[...] for r in x_refs])
        x = jnp.concatenate(lo + hi, axis=1).astype(BF16)
        hid = _silu(_dot(x, wg_s[...])) * _dot(x, wu_s[...])
        y = _dot(hid.astype(BF16), wd_s[...])
        for ref, piece in zip(y_refs, _pack_rows(y)):
            ref[...] = piece


def _experts(x_pieces, tile_expert, n_used, wg, wu, wd, *, group_tile):
    n_rows = x_pieces[0].shape[0]
    n_tiles = n_rows // group_tile

    def tile(i, te, nu):
        return jnp.minimum(i, nu[0] - 1)

    row = pl.BlockSpec((group_tile, LANES), lambda i, te, nu: (tile(i, te, nu), 0))
    wspec = lambda a, b: pl.BlockSpec((None, a, b), lambda i, te, nu: (te[tile(i, te, nu)], 0, 0))
    return pl.pallas_call(
        _experts_kernel,
        grid_spec=pltpu.PrefetchScalarGridSpec(
            num_scalar_prefetch=2,
            grid=(n_tiles,),
            in_specs=[row] * N_PIECES + [wspec(D_MODEL, D_EXPERT), wspec(D_MODEL, D_EXPERT),
                                         wspec(D_EXPERT, D_MODEL)],
            out_specs=[row] * N_PIECES,
            scratch_shapes=[pltpu.VMEM((D_MODEL, D_EXPERT), BF16), pltpu.VMEM((D_MODEL, D_EXPERT), BF16),
                            pltpu.VMEM((D_EXPERT, D_MODEL), BF16)]),
        out_shape=[jax.ShapeDtypeStruct((n_rows, LANES), I32)] * N_PIECES,
        compiler_params=_cparams("arbitrary"),
        name="experts",
    )(tile_expert, n_used, *x_pieces, wg, wu, wd)


def _moe_out_kernel(h_ref, wt_ref, x1_ref, mod_ref, fg_ref, sg_ref, su_ref, sd_ref, *refs):
    y_refs = refs[:N_PIECES]
    out_ref = refs[N_PIECES]
    h = h_ref[...]
    hid = _silu(_dot(h, sg_ref[...])) * _dot(h, su_ref[...])
    shared = _dot(hid.astype(BF16), sd_ref[...])
    wt = wt_ref[...]
    lo_acc = [None] * N_PIECES
    hi_acc = [None] * N_PIECES
    for k in range(TOP_K):
        wk = wt[:, k:k + 1]
        lo, hi = _unpack_rows([r[k] for r in y_refs])
        for c in range(N_PIECES):
            lo_acc[c] = wk * lo[c] if k == 0 else lo_acc[c] + wk * lo[c]
            hi_acc[c] = wk * hi[c] if k == 0 else hi_acc[c] + wk * hi[c]
    routed = jnp.concatenate(lo_acc + hi_acc, axis=1)
    x2 = x1_ref[...] + mod_ref[0, 5:6, :] * (shared + routed)
    out_ref[...] = x2 * lax.rsqrt(jnp.mean(x2 * x2, axis=-1, keepdims=True) + EPS) * fg_ref[...]


def _moe_out(h2, wt, x1, mods, final_g, sg, su, sd, y_pieces, *, tokens_per_mod, tm):
    t = h2.shape[0]
    tiles_per_mod = tokens_per_mod // tm
    row = lambda w: pl.BlockSpec((tm, w), lambda i: (i, 0))
    full = lambda a: pl.BlockSpec(a.shape, lambda i: (0,) * a.ndim)
    return pl.pallas_call(
        _moe_out_kernel,
        grid=(t // tm,),
        in_specs=[row(D_MODEL), row(LANES), row(D_MODEL),
                  pl.BlockSpec((1, 6, D_MODEL), lambda i: (i // tiles_per_mod, 0, 0)),
                  full(final_g), full(sg), full(su), full(sd)]
        + [pl.BlockSpec((TOP_K, tm, LANES), lambda i: (0, i, 0))] * N_PIECES,
        out_specs=row(D_MODEL),
        out_shape=jax.ShapeDtypeStruct((t, D_MODEL), F32),
        compiler_params=_cparams("parallel"),
        name="moe_out",
    )(h2, wt, x1, mods, final_g, sg, su, sd, *y_pieces)


def _trunk(x, mods, s0f, s0b, w, *, batch, seq_len, on_grid):
    t = batch * seq_len
    tokens_per_mod = t // mods.shape[0]
    cos_t, sin_t = _rope_tables(seq_len)
    q, k, v, gsw, up, ga, gb = _inproj(x, mods, w["norm1_g"], w["w_in"], cos_t, sin_t,
                                       tokens_per_mod=tokens_per_mod, seq_len=seq_len,
                                       on_grid=on_grid, tm=256)
    z, s_f, s_b = _retention(q, k, v, gsw, w["dec"], s0f, s0b, batch=batch, seq_len=seq_len)
    p = _pool(up, w["pool_w"], w["pool_scale"], batch=batch, seq_len=seq_len, on_grid=on_grid)
    x1, h2, h2_pieces = _merge(x, z, p, ga, gb, mods, w["norm2_g"], w["w_br_ret"], w["w_br_pool"],
                               w["w_out"], tokens_per_mod=tokens_per_mod, tm=256)

    group_tile = _group_tile(t)
    n_rows = t * TOP_K + N_EXPERTS * group_tile
    idx, rank, wt, counts = _route(h2, w["router_wt"], w["router_bias"], tm=512)
    pos, tile_expert, n_used = _plan(idx, rank, counts, n_tiles=n_rows // group_tile, tf=512,
                                     group_tile=group_tile)
    x_sorted = _sc_dispatch(h2_pieces, pos, n_rows=n_rows)
    y_sorted = _experts(x_sorted, tile_expert.reshape(-1), n_used.reshape(-1),
                        w["exp_w_gate"], w["exp_w_up"], w["exp_w_down"], group_tile=group_tile)
    y_tok = _sc_collect(y_sorted, pos, n_tokens=t)
    y = _moe_out(h2, wt, x1, mods, w["final_g"], w["sh_w_gate"], w["sh_w_up"], w["sh_w_down"], y_tok,
                 tokens_per_mod=tokens_per_mod, tm=256)
    return y, s_f, s_b


def kernel(x_prompt, x_sample, state_ret_fwd, state_ret_bwd, c, c_ctx, ada_w, ada_b, norm1_g, norm2_g, w_in,
           ret_decay_fwd, ret_decay_bwd, w_br_ret, pool_w, pool_scale, w_br_pool, w_out, router_w, router_bias,
           exp_w_gate, exp_w_up, exp_w_down, sh_w_gate, sh_w_up, sh_w_down, final_norm_g):
    n_req, seq, d = x_prompt.shape
    n_dec, dec_seq, _ = x_sample.shape
    depth = ada_w.shape[0]
    assert depth == 1 and d == D_MODEL

    xc = x_prompt.reshape(n_req * seq, d)
    xs = x_sample.reshape(n_dec * dec_seq, d)
    zero_state = jnp.zeros((n_req, RET_HEADS, RET_DK, RET_DV), F32)
    new_f, new_b = [], []
    for l in range(depth):
        c_rows = jnp.concatenate([c_ctx[None, :], c, jnp.zeros((8 - 1 - n_dec, d), F32)], axis=0)
        mods = _ada(c_rows, ada_w[l], ada_b[l]).reshape(8, 6, d)
        pad_rows = LANES - N_EXPERTS
        w = dict(
            norm1_g=norm1_g[l].reshape(1, d), norm2_g=norm2_g[l].reshape(1, d),
            final_g=final_norm_g.reshape(1, d),
            w_in=w_in[l].astype(BF16),
            dec=jnp.stack([ret_decay_fwd[l], ret_decay_bwd[l]]).astype(F32),
            w_br_ret=w_br_ret[l].astype(BF16), pool_w=pool_w[l].astype(BF16),
            pool_scale=pool_scale[l].reshape(1, POOL_W), w_br_pool=w_br_pool[l].astype(BF16),
            w_out=w_out[l].astype(BF16),
            router_wt=jnp.pad(router_w[l].T, ((0, pad_rows), (0, 0))).astype(BF16),
            router_bias=jnp.pad(router_bias[l].astype(F32).reshape(N_EXPERTS, 1), ((0, pad_rows), (0, 0))),
            exp_w_gate=exp_w_gate[l], exp_w_up=exp_w_up[l], exp_w_down=exp_w_down[l],
            sh_w_gate=sh_w_gate[l].astype(BF16),
            sh_w_up=sh_w_up[l].astype(BF16), sh_w_down=sh_w_down[l].astype(BF16),
        )
        xc, s_f, s_b = _trunk(xc, mods[0:1], zero_state, zero_state, w,
                              batch=n_req, seq_len=seq, on_grid=False)
        new_f.append(s_f)
        new_b.append(s_b)
        xs, _, _ = _trunk(xs, mods[1:1 + n_dec], state_ret_fwd[:, l].astype(F32),
                          state_ret_bwd[:, l].astype(F32), w,
                          batch=n_dec, seq_len=dec_seq, on_grid=True)
    y_prompt = xc.reshape(n_req, seq, d)
    y_sample = xs.reshape(n_dec, dec_seq, d)
    return (y_prompt, y_sample, jnp.stack(new_f, axis=1).astype(x_prompt.dtype),
            jnp.stack(new_b, axis=1).astype(x_prompt.dtype))
```

```python
import functools

import numpy as np
import jax
import jax.numpy as jnp
from jax import lax
from jax.experimental import pallas as pl
from jax.experimental.pallas import tpu as pltpu
from jax.experimental.pallas import tpu_sc as plsc

D_MODEL = 1024
GRID_W = 64
RET_HEADS = 4
RET_DK = 128
RET_DV = 256
RET_QK_W = RET_HEADS * RET_DK
RET_V_W = RET_HEADS * RET_DV
RET_CHUNK = 128
ROPE_BASE = 10000.0
POOL_GROUPS = 4
POOL_CH = 128
POOL_W = POOL_GROUPS * POOL_CH
POOL_WINDOWS = (2, 4, 8, 16)
N_EXPERTS = 64
TOP_K = 8
N_EXPERT_GROUPS = 8
GROUP_SIZE = N_EXPERTS // N_EXPERT_GROUPS
TOPK_GROUPS = 4
D_EXPERT = 256
ROUTED_SCALE = 2.5
EPS = 1e-6
IN_SIZES = (RET_QK_W, RET_QK_W, RET_V_W, RET_V_W, POOL_W, D_MODEL, D_MODEL)
IN_OFFS = tuple(sum(IN_SIZES[:i]) for i in range(len(IN_SIZES) + 1))
IN_W = IN_OFFS[-1]

LANES = 128
VMEM_LIMIT = 56 << 20
N_PIECES = D_MODEL // 2 // LANES
MXU_DIM = 256
SC_CHUNK = 128

F32 = jnp.float32
BF16 = jnp.bfloat16
I32 = jnp.int32
U32 = jnp.uint32


def _cparams(*sem):
    return pltpu.CompilerParams(dimension_semantics=sem, vmem_limit_bytes=VMEM_LIMIT)


def _dot(a, b):
    return jnp.dot(a, b, preferred_element_type=F32)


def _silu(x):
    return x * jax.nn.sigmoid(x)


def _rms_mod(x, g, scale, shift):
    y = x * lax.rsqrt(jnp.mean(x * x, axis=-1, keepdims=True) + EPS)
    return (y * g) * (1.0 + scale) + shift


def _ada_kernel(c_ref, w_ref, b_ref, o_ref):
    c = c_ref[...]
    o_ref[...] = jnp.dot(_silu(c), w_ref[...], preferred_element_type=F32,
                         precision=lax.Precision.HIGHEST) + b_ref[...]


def _ada(c_rows, ada_w, ada_b):
    r = c_rows.shape[0]
    n = ada_w.shape[1]
    tn = D_MODEL
    return pl.pallas_call(
        _ada_kernel,
        grid=(n // tn,),
        in_specs=[pl.BlockSpec((r, D_MODEL), lambda j: (0, 0)),
                  pl.BlockSpec((D_MODEL, tn), lambda j: (0, j)),
                  pl.BlockSpec((1, tn), lambda j: (0, j))],
        out_specs=pl.BlockSpec((r, tn), lambda j: (0, j)),
        out_shape=jax.ShapeDtypeStruct((r, n), F32),
        compiler_params=_cparams("parallel"),
        name="ada_mod",
    )(c_rows, ada_w, ada_b.reshape(1, n))


def _inproj_kernel(x_ref, mod_ref, g_ref, w_ref, cos_ref, sin_ref,
                   q_ref, k_ref, v_ref, gsw_ref, up_ref, ga_ref, gb_ref, *, on_grid):
    x = x_ref[...]
    h = _rms_mod(x, g_ref[...], mod_ref[0, 1:2, :], mod_ref[0, 0:1, :]).astype(BF16)

    def seg(i):
        return _dot(h, w_ref[:, IN_OFFS[i]:IN_OFFS[i + 1]])

    q = seg(0)
    k = seg(1)
    if on_grid:
        cos = jnp.concatenate([cos_ref[...]] * RET_HEADS, axis=1)
        sin = jnp.concatenate([sin_ref[...]] * RET_HEADS, axis=1)
        lane = lax.broadcasted_iota(jnp.int32, q.shape, 1)
        first = (lane & 63) < 32

        def rope(a):
            up = pltpu.roll(a, RET_QK_W - 32, axis=1)
            dn = pltpu.roll(a, 32, axis=1)
            return a * cos + jnp.where(first, up, dn) * sin

        q = rope(q)
        k = rope(k)
    q_ref[...] = q.astype(BF16)
    k_ref[...] = (k * (RET_DK ** -0.5)).astype(BF16)
    v_ref[...] = seg(2).astype(BF16)
    gsw_ref[...] = seg(3).astype(BF16)
    up_ref[...] = seg(4).astype(BF16)
    ga_ref[...] = seg(5).astype(BF16)
    gb_ref[...] = seg(6).astype(BF16)


def _inproj(x, mods, norm_g, w_in, cos_t, sin_t, *, tokens_per_mod, seq_len, on_grid, tm):
    t = x.shape[0]
    tiles_per_mod = tokens_per_mod // tm
    tiles_per_seq = seq_len // tm
    widths = IN_SIZES
    out_shape = [jax.ShapeDtypeStruct((t, w), BF16) for w in widths]
    out_specs = [pl.BlockSpec((tm, w), lambda i: (i, 0)) for w in widths]
    return pl.pallas_call(
        functools.partial(_inproj_kernel, on_grid=on_grid),
        grid=(t // tm,),
        in_specs=[pl.BlockSpec((tm, D_MODEL), lambda i: (i, 0)),
                  pl.BlockSpec((1, 6, D_MODEL), lambda i: (i // tiles_per_mod, 0, 0)),
                  pl.BlockSpec((1, D_MODEL), lambda i: (0, 0)),
                  pl.BlockSpec((D_MODEL, IN_W), lambda i: (0, 0)),
                  pl.BlockSpec((tm, RET_DK), lambda i: (i % tiles_per_seq, 0)),
                  pl.BlockSpec((tm, RET_DK), lambda i: (i % tiles_per_seq, 0))],
        out_specs=out_specs,
        out_shape=out_shape,
        compiler_params=_cparams("parallel"),
        name="inproj_grid" if on_grid else "inproj_seq",
    )(x, mods, norm_g, w_in, cos_t, sin_t)


def _rope_tables(seq_len):
    t = np.arange(seq_len)
    row = (t // GRID_W).astype(np.float32)
    col = (t % GRID_W).astype(np.float32)
    m = RET_DK // 4
    inv = (np.float32(ROPE_BASE) ** (-np.arange(m, dtype=np.float32) / np.float32(m))).astype(np.float32)
    ar = row[:, None] * inv
    ac = col[:, None] * inv
    cos = np.concatenate([np.cos(ar), np.cos(ar), np.cos(ac), np.cos(ac)], axis=1)
    sin = np.concatenate([-np.sin(ar), np.sin(ar), -np.sin(ac), np.sin(ac)], axis=1)
    return jnp.asarray(cos, F32), jnp.asarray(sin, F32)


RET_HEADS_PER_STEP = 2


def _ret_kernel(dec_ref, q_ref, k_ref, v_ref, g_ref, s0f_ref, s0b_ref,
                z_ref, sf_ref, sb_ref, oacc_ref, kt_ref, *, n_chunks):
    c = RET_CHUNK
    heads = RET_HEADS_PER_STEP
    half = n_chunks // 2
    ii = lax.broadcasted_iota(I32, (c, c), 0)
    jj = lax.broadcasted_iota(I32, (c, c), 1)
    ik = lax.broadcasted_iota(I32, (c, RET_DK), 0).astype(F32)
    jk = lax.broadcasted_iota(I32, (RET_DK, c), 1).astype(F32)

    def log_gamma(d, shape):
        return jnp.log1p(-jnp.exp2(-jnp.full(shape, d, F32)))

    consts = {}
    for hh in range(heads):
        h = pl.program_id(1) * heads + hh
        dec_f = dec_ref[0, h]
        dec_b = dec_ref[1, h]
        rel = (ii - jj).astype(F32)
        consts[hh, "f"] = (
            jnp.where(rel >= 0, jnp.exp(log_gamma(dec_f, (c, c)) * jnp.maximum(rel, 0.0)), 0.0),
            jnp.exp(log_gamma(dec_f, (c, RET_DK)) * (ik + 1.0)),
            jnp.exp(log_gamma(dec_f, (RET_DK, c)) * (c - 1.0 - jk)),
            jnp.exp(log_gamma(dec_f, (RET_DK, RET_DV)) * c))
        consts[hh, "b"] = (
            jnp.where(rel <= 0, jnp.exp(log_gamma(dec_b, (c, c)) * jnp.maximum(-rel, 0.0)), 0.0),
            jnp.exp(log_gamma(dec_b, (c, RET_DK)) * (c - ik)),
            jnp.exp(log_gamma(dec_b, (RET_DK, c)) * jk),
            jnp.exp(log_gamma(dec_b, (RET_DK, RET_DV)) * c))

    sf_ref[...] = s0f_ref[...]
    sb_ref[...] = s0b_ref[...]

    def transpose_keys(ci, carry):
        r = pl.ds(pl.multiple_of(ci * c, c), c)
        for hh in range(heads):
            kt_ref[hh, ci] = k_ref[r, hh * RET_DK:(hh + 1) * RET_DK].T
        return carry

    lax.fori_loop(0, n_chunks, transpose_keys, 0)

    def scan_chunk(ci, hh, direction, second):
        dmask, qdec, kdec, cdec = consts[hh, direction]
        s_ref = sf_ref if direction == "f" else sb_ref
        r = pl.ds(pl.multiple_of(ci * c, c), c)
        kcols = slice(hh * RET_DK, (hh + 1) * RET_DK)
        vcols = slice(hh * RET_DV, (hh + 1) * RET_DV)
        qc = q_ref[r, kcols]
        kc = k_ref[r, kcols]
        vc = v_ref[r, vcols]
        s = s_ref[hh]
        sc = lax.dot_general(qc, kc, (((1,), (1,)), ((), ())), preferred_element_type=F32)
        lhs = jnp.concatenate([(sc * dmask).astype(BF16), (qc.astype(F32) * qdec).astype(BF16)], axis=1)
        o = _dot(lhs, jnp.concatenate([vc, s.astype(BF16)], axis=0))
        kd_t = (kt_ref[hh, ci].astype(F32) * kdec).astype(BF16)
        s_ref[hh] = s * cdec + _dot(kd_t, vc)
        if not second:
            oacc_ref[hh, r, :] = o
        else:
            o = o + oacc_ref[hh, r, :]
            o = o * lax.rsqrt(jnp.mean(o * o, axis=-1, keepdims=True) + EPS)
            g = g_ref[r, vcols].astype(F32)
            z_ref[r, vcols] = (_silu(g) * o).astype(BF16)

    def body(second):
        def run(t, carry):
            for hh in range(heads):
                scan_chunk(t, hh, "f", second)
                scan_chunk(n_chunks - 1 - t, hh, "b", second)
            return carry
        return run

    lax.fori_loop(0, half, body(False), 0)
    lax.fori_loop(half, n_chunks, body(True), 0)


def _retention(q, k, v, gsw, dec, s0f, s0b, *, batch, seq_len):
    n_chunks = seq_len // RET_CHUNK
    assert n_chunks % 2 == 0
    heads = RET_HEADS_PER_STEP
    t = batch * seq_len
    st_spec = pl.BlockSpec((None, heads, RET_DK, RET_DV), lambda b, h: (b, h, 0, 0))
    st_shape = jax.ShapeDtypeStruct((batch, RET_HEADS, RET_DK, RET_DV), F32)
    kspec = pl.BlockSpec((seq_len, heads * RET_DK), lambda b, h: (b, h))
    vspec = pl.BlockSpec((seq_len, heads * RET_DV), lambda b, h: (b, h))
    return pl.pallas_call(
        functools.partial(_ret_kernel, n_chunks=n_chunks),
        grid=(batch, RET_HEADS // heads),
        in_specs=[pl.BlockSpec(memory_space=pltpu.SMEM), kspec, kspec, vspec, vspec, st_spec, st_spec],
        out_specs=[vspec, st_spec, st_spec],
        out_shape=[jax.ShapeDtypeStruct((t, RET_V_W), BF16), st_shape, st_shape],
        scratch_shapes=[pltpu.VMEM((heads, seq_len, RET_DV), F32),
                        pltpu.VMEM((heads, n_chunks, RET_DK, RET_CHUNK), BF16)],
        compiler_params=_cparams("parallel", "parallel"),
        name=f"retention_l{seq_len}",
    )(dec, q, k, v, gsw, s0f, s0b)


def _pool_kernel(u_ref, w_ref, sc_ref, o_ref, *, seq_len, on_grid):
    tok = lax.broadcasted_iota(jnp.int32, (seq_len, POOL_CH), 0)

    def shift(a, s, stride, pos, width):
        y = pltpu.roll(a, (-s * stride) % seq_len, axis=0)
        ok = (pos < width - s) if s > 0 else (pos >= -s)
        return jnp.where(ok, y, 0.0)

    def box_mean(a, window, stride, pos, width):
        half = window // 2
        fw = a
        bw = shift(a, -1, stride, pos, width)
        m = 1
        while m < half:
            fw = fw + shift(fw, m, stride, pos, width)
            bw = bw + shift(bw, -m, stride, pos, width)
            m *= 2
        cnt = jnp.minimum(pos + half, width) - jnp.maximum(pos - half, 0)
        return (fw + bw) / cnt.astype(F32)

    for g, window in enumerate(POOL_WINDOWS):
        cols = slice(g * POOL_CH, (g + 1) * POOL_CH)
        ug = u_ref[:, cols].astype(F32)
        if on_grid:
            pooled = box_mean(ug, window, 1, tok & (GRID_W - 1), GRID_W)
            pooled = box_mean(pooled, window, GRID_W, tok >> 6, seq_len // GRID_W)
        else:
            pooled = box_mean(ug, window, 1, tok, seq_len)
        d = (pooled - ug).astype(BF16)
        o_ref[:, cols] = (_dot(d, w_ref[g]) * sc_ref[:, cols]).astype(BF16)


def _pool(u, pool_w, pool_scale, *, batch, seq_len, on_grid):
    t = batch * seq_len
    return pl.pallas_call(
        functools.partial(_pool_kernel, seq_len=seq_len, on_grid=on_grid),
        grid=(batch,),
        in_specs=[pl.BlockSpec((seq_len, POOL_W), lambda b: (b, 0)),
                  pl.BlockSpec((POOL_GROUPS, POOL_CH, POOL_CH), lambda b: (0, 0, 0)),
                  pl.BlockSpec((1, POOL_W), lambda b: (0, 0))],
        out_specs=pl.BlockSpec((seq_len, POOL_W), lambda b: (b, 0)),
        out_shape=jax.ShapeDtypeStruct((t, POOL_W), BF16),
        compiler_params=_cparams("parallel"),
        name=f"pool_l{seq_len}",
    )(u, pool_w, pool_scale)


def _pack_rows(x):
    half = D_MODEL // 2
    lo = lax.bitcast_convert_type(x[:, :half].astype(BF16).astype(F32), U32) >> 16
    hi = lax.bitcast_convert_type(x[:, half:].astype(BF16).astype(F32), U32) & jnp.uint32(0xFFFF0000)
    word = lax.bitcast_convert_type(hi | lo, I32)
    return [word[:, c * LANES:(c + 1) * LANES] for c in range(N_PIECES)]


def _unpack_rows(pieces):
    words = [lax.bitcast_convert_type(p, U32) for p in pieces]
    lo = [lax.bitcast_convert_type(w << 16, F32) for w in words]
    hi = [lax.bitcast_convert_type(w & jnp.uint32(0xFFFF0000), F32) for w in words]
    return lo, hi


def _merge_kernel(x_ref, z_ref, p_ref, ga_ref, gb_ref, mod_ref, g2_ref, wr_ref, wp_ref, wo_ref,
                  x1_ref, h2_ref, *piece_refs):
    y_ret = _dot(z_ref[...], wr_ref[...])
    y_pool = _dot(p_ref[...], wp_ref[...])
    merged = (jax.nn.sigmoid(ga_ref[...].astype(F32)) * y_ret
              + jax.nn.sigmoid(gb_ref[...].astype(F32)) * y_pool)
    x1 = x_ref[...] + mod_ref[0, 2:3, :] * _dot(merged.astype(BF16), wo_ref[...])
    x1_ref[...] = x1
    h2 = _rms_mod(x1, g2_ref[...], mod_ref[0, 4:5, :], mod_ref[0, 3:4, :])
    h2_ref[...] = h2.astype(BF16)
    for ref, piece in zip(piece_refs, _pack_rows(h2)):
        ref[...] = piece


def _merge(x, z, p, ga, gb, mods, norm2_g, w_br_ret, w_br_pool, w_out, *, tokens_per_mod, tm):
    t = x.shape[0]
    tiles_per_mod = tokens_per_mod // tm
    row = lambda w: pl.BlockSpec((tm, w), lambda i: (i, 0))
    full = lambda a: pl.BlockSpec(a.shape, lambda i: (0,) * a.ndim)
    outs = pl.pallas_call(
        _merge_kernel,
        grid=(t // tm,),
        in_specs=[row(D_MODEL), row(RET_V_W), row(POOL_W), row(D_MODEL), row(D_MODEL),
                  pl.BlockSpec((1, 6, D_MODEL), lambda i: (i // tiles_per_mod, 0, 0)),
                  full(norm2_g), full(w_br_ret), full(w_br_pool), full(w_out)],
        out_specs=[row(D_MODEL), row(D_MODEL)] + [row(LANES)] * N_PIECES,
        out_shape=[jax.ShapeDtypeStruct((t, D_MODEL), F32), jax.ShapeDtypeStruct((t, D_MODEL), BF16)]
        + [jax.ShapeDtypeStruct((t, LANES), I32)] * N_PIECES,
        compiler_params=_cparams("parallel"),
        name="merge",
    )(x, z, p, ga, gb, mods, norm2_g, w_br_ret, w_br_pool, w_out)
    return outs[0], outs[1], outs[2:]


def _route_kernel(h_ref, rw_ref, bias_ref, idx_ref, rank_ref, wt_ref, cnt_ref, carry_ref):
    e = N_EXPERTS
    tm = h_ref.shape[0]
    neg = -jnp.inf

    @pl.when(pl.program_id(0) == 0)
    def _():
        carry_ref[...] = jnp.zeros(carry_ref.shape, F32)

    logits = lax.dot_general(rw_ref[...], h_ref[...], (((1,), (1,)), ((), ())),
                             preferred_element_type=F32)[:e]
    scores = jax.nn.sigmoid(logits)
    sel = scores + bias_ref[:e, 0:1]
    e_idx = lax.broadcasted_iota(I32, (e, tm), 0)

    grp = sel.reshape(N_EXPERT_GROUPS, GROUP_SIZE, tm)
    m_idx = lax.broadcasted_iota(I32, grp.shape, 1)
    m1 = jnp.max(grp, axis=1, keepdims=True)
    first = jnp.min(jnp.where(grp == m1, m_idx, GROUP_SIZE), axis=1, keepdims=True)
    m2 = jnp.max(jnp.where(m_idx == first, neg, grp), axis=1, keepdims=True)
    gscore = (m1 + m2).reshape(N_EXPERT_GROUPS, tm)

    g_idx = lax.broadcasted_iota(I32, gscore.shape, 0)
    grank = jnp.zeros(gscore.shape, I32)
    for g in range(N_EXPERT_GROUPS):
        other = gscore[g:g + 1, :]
        beats = jnp.where(other > gscore, 1, jnp.where(other == gscore, (g_idx > g).astype(I32), 0))
        grank = grank + beats
    gkeep = (grank < TOPK_GROUPS).astype(F32)
    ekeep = jnp.broadcast_to(gkeep.reshape(N_EXPERT_GROUPS, 1, tm), grp.shape).reshape(e, tm)
    masked = jnp.where(ekeep > 0, sel, neg)

    chosen = jnp.zeros((e, tm), F32)
    picks, hits = [], []
    for _ in range(TOP_K):
        m = jnp.max(masked, axis=0, keepdims=True)
        pick = jnp.min(jnp.where(masked == m, e_idx, e), axis=0, keepdims=True)
        hit = e_idx == pick
        chosen = jnp.where(hit, 1.0, chosen)
        masked = jnp.where(hit, neg, masked)
        picks.append(pick)
        hits.append(hit)

    w = scores * chosen
    comb = w / jnp.sum(w, axis=0, keepdims=True) * ROUTED_SCALE

    t_row = lax.broadcasted_iota(I32, (tm, tm), 0)
    t_col = lax.broadcasted_iota(I32, (tm, tm), 1)
    before = (t_row < t_col).astype(BF16)
    rankmat = _dot(chosen.astype(BF16), before) + carry_ref[:e, 0:1]
    carry_ref[:e, :] = carry_ref[:e, :] + jnp.sum(chosen, axis=1, keepdims=True)
    cnt_ref[...] = carry_ref[...]

    idx_ref[...] = jnp.concatenate(picks, axis=0)
    rank_ref[...] = jnp.concatenate(
        [jnp.sum(jnp.where(h, rankmat, 0.0), axis=0, keepdims=True) for h in hits], axis=0).astype(I32)
    w_rows = [jnp.sum(jnp.where(h, comb, 0.0), axis=0, keepdims=True) for h in hits]
    wt_ref[...] = jnp.concatenate(w_rows + [jnp.zeros((LANES - TOP_K, tm), F32)], axis=0).T


def _route(h2, router_wt, bias_col, *, tm):
    t = h2.shape[0]
    krow = pl.BlockSpec((TOP_K, tm), lambda i: (0, i))
    return pl.pallas_call(
        _route_kernel,
        grid=(t // tm,),
        in_specs=[pl.BlockSpec((tm, D_MODEL), lambda i: (i, 0)),
                  pl.BlockSpec((LANES, D_MODEL), lambda i: (0, 0)),
                  pl.BlockSpec((LANES, 1), lambda i: (0, 0))],
        out_specs=[krow, krow, pl.BlockSpec((tm, LANES), lambda i: (i, 0)),
                   pl.BlockSpec((LANES, LANES), lambda i: (0, 0))],
        out_shape=[jax.ShapeDtypeStruct((TOP_K, t), I32), jax.ShapeDtypeStruct((TOP_K, t), I32),
                   jax.ShapeDtypeStruct((t, LANES), F32), jax.ShapeDtypeStruct((LANES, LANES), F32)],
        scratch_shapes=[pltpu.VMEM((LANES, LANES), F32)],
        compiler_params=_cparams("arbitrary"),
        name="route",
    )(h2, router_wt, bias_col)


def _plan_kernel(idx_ref, rank_ref, cnt_ref, pos_ref, te_ref, nu_ref, *, group_tile, min_tiles):
    tf = idx_ref.shape[1]
    nt = te_ref.shape[1]
    cnt = cnt_ref[...].astype(I32)
    tiles = jnp.maximum((cnt + (group_tile - 1)) // group_tile, min_tiles)
    e_sub = lax.broadcasted_iota(I32, (LANES, LANES), 0)
    padded = jnp.where(e_sub < N_EXPERTS, tiles * group_tile, 0).astype(F32)
    e_lane = lax.broadcasted_iota(I32, (LANES, LANES), 1)
    base = jnp.sum(jnp.where(e_lane < e_sub, padded.T, 0.0), axis=1, keepdims=True)
    end = base + padded[:, 0:1]

    idx = idx_ref[...]
    start = jnp.zeros(idx.shape, F32)
    for e in range(N_EXPERTS):
        start = jnp.where(idx == e, base[e:e + 1, 0:1], start)
    pos = start.astype(I32) + rank_ref[...]
    for j in range(tf // SC_CHUNK):
        pos_ref[j] = pos[:, j * SC_CHUNK:(j + 1) * SC_CHUNK]

    tile_start = (lax.broadcasted_iota(I32, (N_EXPERTS, nt), 1) * group_tile).astype(F32)
    done = jnp.sum(jnp.where(end[:N_EXPERTS] <= tile_start, 1.0, 0.0), axis=0, keepdims=True)
    te_ref[...] = jnp.minimum(done, N_EXPERTS - 1.0).astype(I32)
    total = jnp.sum(padded[:, 0:1], axis=0, keepdims=True)
    nu_ref[...] = jnp.broadcast_to(total * (1.0 / group_tile), nu_ref.shape).astype(I32)


def _plan(idx, rank, counts, *, n_tiles, tf, group_tile, min_tiles):
    t = idx.shape[1]
    nt_pad = -(-n_tiles // LANES) * LANES
    krow = pl.BlockSpec((TOP_K, tf), lambda i: (0, i))
    return pl.pallas_call(
        functools.partial(_plan_kernel, group_tile=group_tile, min_tiles=min_tiles),
        grid=(t // tf,),
        in_specs=[krow, krow, pl.BlockSpec((LANES, LANES), lambda i: (0, 0))],
        out_specs=[pl.BlockSpec((tf // SC_CHUNK, TOP_K, SC_CHUNK), lambda i: (i, 0, 0)),
                   pl.BlockSpec((1, nt_pad), lambda i: (0, 0)),
                   pl.BlockSpec((1, LANES), lambda i: (0, 0))],
        out_shape=[jax.ShapeDtypeStruct((t // SC_CHUNK, TOP_K, SC_CHUNK), I32),
                   jax.ShapeDtypeStruct((1, nt_pad), I32), jax.ShapeDtypeStruct((1, LANES), I32)],
        compiler_params=_cparams("arbitrary"),
        name="moe_plan",
    )(idx, rank, counts)


def _sc_mesh_info():
    info = plsc.get_sparse_core_info()
    mesh = plsc.VectorSubcoreMesh(core_axis_name="c", subcore_axis_name="s")
    return mesh, info.num_cores, info.num_cores * info.num_subcores


def _sc_dispatch(pieces, pos, *, n_rows):
    t = pieces[0].shape[0]
    mesh, n_cores, n_workers = _sc_mesh_info()
    per_w = t // SC_CHUNK // n_workers

    @functools.partial(
        pl.kernel, mesh=mesh,
        out_type=[jax.ShapeDtypeStruct((n_rows, LANES), I32)] * N_PIECES,
        scratch_types=[pltpu.VMEM((TOP_K, SC_CHUNK), I32),
                       pltpu.VMEM((N_PIECES, SC_CHUNK, LANES), I32),
                       pltpu.SemaphoreType.DMA((N_PIECES,)),
                       pltpu.SemaphoreType.DMA],
        name="sc_dispatch",
    )
    def run(*refs):
        src = refs[:N_PIECES]
        pos_hbm = refs[N_PIECES]
        dst = refs[N_PIECES + 1:2 * N_PIECES + 1]
        idx_v, rows_v, load_sem, put_sem = refs[2 * N_PIECES + 1:]
        wid = lax.axis_index("s") * n_cores + lax.axis_index("c")

        @pl.loop(0, per_w)
        def _(j):
            ch = wid * per_w + j
            t0 = pl.multiple_of(ch * SC_CHUNK, SC_CHUNK)
            loads = [pltpu.make_async_copy(src[c].at[pl.ds(t0, SC_CHUNK)], rows_v.at[c], load_sem.at[c])
                     for c in range(N_PIECES)]
            for ld in loads:
                ld.start()
            pltpu.sync_copy(pos_hbm.at[ch], idx_v)
            puts = []
            for c in range(N_PIECES):
                loads[c].wait()
                for k in range(TOP_K):
                    puts.append(pltpu.make_async_copy(rows_v.at[c], dst[c].at[idx_v.at[k]], put_sem))
                    puts[-1].start()
            for cp in puts:
                cp.wait()

    return run(*pieces, pos)


def _sc_collect(pieces, pos, *, n_tokens):
    mesh, n_cores, n_workers = _sc_mesh_info()
    per_w = n_tokens // SC_CHUNK // n_workers
    n_buf = 4
    lag = n_buf // 2
    units = [(k, c) for k in range(TOP_K) for c in range(N_PIECES)]

    @functools.partial(
        pl.kernel, mesh=mesh,
        out_type=[jax.ShapeDtypeStruct((TOP_K, n_tokens, LANES), I32)] * N_PIECES,
        scratch_types=[pltpu.VMEM((TOP_K, SC_CHUNK), I32),
                       pltpu.VMEM((n_buf, SC_CHUNK, LANES), I32),
                       pltpu.SemaphoreType.DMA((n_buf,)),
                       pltpu.SemaphoreType.DMA((n_buf,))],
        name="sc_collect",
    )
    def run(*refs):
        src = refs[:N_PIECES]
        pos_hbm = refs[N_PIECES]
        dst = refs[N_PIECES + 1:2 * N_PIECES + 1]
        idx_v, buf, get_sem, put_sem = refs[2 * N_PIECES + 1:]
        wid = lax.axis_index("s") * n_cores + lax.axis_index("c")

        @pl.loop(0, per_w)
        def _(j):
            ch = wid * per_w + j
            t0 = pl.multiple_of(ch * SC_CHUNK, SC_CHUNK)
            pltpu.sync_copy(pos_hbm.at[ch], idx_v)

            def get(u):
                k, c = units[u]
                return pltpu.make_async_copy(src[c].at[idx_v.at[k]], buf.at[u % n_buf], get_sem.at[u % n_buf])

            def put(u):
                k, c = units[u]
                return pltpu.make_async_copy(buf.at[u % n_buf], dst[c].at[k, pl.ds(t0, SC_CHUNK)],
                                             put_sem.at[u % n_buf])

            n = len(units)
            for u in range(n + lag):
                if u < n:
                    if u >= n_buf:
                        put(u - n_buf).wait()
                    get(u).start()
                if 0 <= u - lag < n:
                    get(u - lag).wait()
                    put(u - lag).start()
            for u in range(n - n_buf, n):
                put(u).wait()

    return run(*pieces, pos)


def _group_tile(n_tokens):
    per_expert = n_tokens * TOP_K // N_EXPERTS
    return max(MXU_DIM, min(2 * MXU_DIM, per_expert // 4 // MXU_DIM * MXU_DIM))


def _experts_kernel(te_ref, nu_ref, *refs, cast_weights):
    x_refs = refs[:N_PIECES]
    w_in = refs[N_PIECES:N_PIECES + 3]
    y_refs = refs[N_PIECES + 3:2 * N_PIECES + 3]
    i = pl.program_id(0)

    if cast_weights:
        w_bf = refs[2 * N_PIECES + 3:]
        last = nu_ref[0] - 1
        cur = te_ref[jnp.minimum(i, last)]
        prev = te_ref[jnp.minimum(jnp.maximum(i - 1, 0), last)]

        @pl.when((i == 0) | (cur != prev))
        def _():
            for dst, src in zip(w_bf, w_in):
                dst[...] = src[...].astype(BF16)
    else:
        w_bf = w_in
    wg_ref, wu_ref, wd_ref = w_bf

    @pl.when(i < nu_ref[0])
    def _():
        for s in range(x_refs[0].shape[0] // MXU_DIM):
            rows = slice(s * MXU_DIM, (s + 1) * MXU_DIM)
            lo, hi = _unpack_rows([r[rows, :] for r in x_refs])
            x = jnp.concatenate(lo + hi, axis=1).astype(BF16)
            hid = _silu(_dot(x, wg_ref[...])) * _dot(x, wu_ref[...])
            y = _dot(hid.astype(BF16), wd_ref[...])
            for ref, piece in zip(y_refs, _pack_rows(y)):
                ref[rows, :] = piece


def _experts(x_pieces, tile_expert, n_used, wg, wu, wd, *, group_tile):
    n_rows = x_pieces[0].shape[0]
    n_tiles = n_rows // group_tile
    cast_weights = wg.dtype != BF16

    def tile(i, te, nu):
        return jnp.minimum(i, nu[0] - 1)

    row = pl.BlockSpec((group_tile, LANES), lambda i, te, nu: (tile(i, te, nu), 0))
    wspec = lambda a: pl.BlockSpec((None,) + a.shape[1:], lambda i, te, nu: (te[tile(i, te, nu)], 0, 0))
    w_specs = [wspec(wg), wspec(wu), wspec(wd)]
    y_shape = [jax.ShapeDtypeStruct((n_rows, LANES), I32)] * N_PIECES
    w_shape = [jax.ShapeDtypeStruct(a.shape, BF16) for a in (wg, wu, wd)]
    outs = pl.pallas_call(
        functools.partial(_experts_kernel, cast_weights=cast_weights),
        grid_spec=pltpu.PrefetchScalarGridSpec(
            num_scalar_prefetch=2,
            grid=(n_tiles,),
            in_specs=[row] * N_PIECES + w_specs,
            out_specs=[row] * N_PIECES + (w_specs if cast_weights else [])),
        out_shape=y_shape + (w_shape if cast_weights else []),
        compiler_params=_cparams("arbitrary"),
        name="experts_cast" if cast_weights else "experts",
    )(tile_expert, n_used, *x_pieces, wg, wu, wd)
    return outs[:N_PIECES], (tuple(outs[N_PIECES:]) if cast_weights else (wg, wu, wd))


def _moe_out_kernel(h_ref, wt_ref, x1_ref, mod_ref, fg_ref, sg_ref, su_ref, sd_ref, *refs):
    y_refs = refs[:N_PIECES]
    out_ref = refs[N_PIECES]
    h = h_ref[...]
    hid = _silu(_dot(h, sg_ref[...])) * _dot(h, su_ref[...])
    shared = _dot(hid.astype(BF16), sd_ref[...])
    wt = wt_ref[...]
    lo_acc = [None] * N_PIECES
    hi_acc = [None] * N_PIECES
    for k in range(TOP_K):
        wk = wt[:, k:k + 1]
        lo, hi = _unpack_rows([r[k] for r in y_refs])
        for c in range(N_PIECES):
            lo_acc[c] = wk * lo[c] if k == 0 else lo_acc[c] + wk * lo[c]
            hi_acc[c] = wk * hi[c] if k == 0 else hi_acc[c] + wk * hi[c]
    routed = jnp.concatenate(lo_acc + hi_acc, axis=1)
    x2 = x1_ref[...] + mod_ref[0, 5:6, :] * (shared + routed)
    out_ref[...] = x2 * lax.rsqrt(jnp.mean(x2 * x2, axis=-1, keepdims=True) + EPS) * fg_ref[...]


def _moe_out(h2, wt, x1, mods, final_g, sg, su, sd, y_pieces, *, tokens_per_mod, tm):
    t = h2.shape[0]
    tiles_per_mod = tokens_per_mod // tm
    row = lambda w: pl.BlockSpec((tm, w), lambda i: (i, 0))
    full = lambda a: pl.BlockSpec(a.shape, lambda i: (0,) * a.ndim)
    return pl.pallas_call(
        _moe_out_kernel,
        grid=(t // tm,),
        in_specs=[row(D_MODEL), row(LANES), row(D_MODEL),
                  pl.BlockSpec((1, 6, D_MODEL), lambda i: (i // tiles_per_mod, 0, 0)),
                  full(final_g), full(sg), full(su), full(sd)]
        + [pl.BlockSpec((TOP_K, tm, LANES), lambda i: (0, i, 0))] * N_PIECES,
        out_specs=row(D_MODEL),
        out_shape=jax.ShapeDtypeStruct((t, D_MODEL), F32),
        compiler_params=_cparams("parallel"),
        name="moe_out",
    )(h2, wt, x1, mods, final_g, sg, su, sd, *y_pieces)


def _trunk(x, mods, s0f, s0b, w, expert_w, *, batch, seq_len, on_grid):
    t = batch * seq_len
    tokens_per_mod = t // mods.shape[0]
    cos_t, sin_t = _rope_tables(seq_len)
    q, k, v, gsw, up, ga, gb = _inproj(x, mods, w["norm1_g"], w["w_in"], cos_t, sin_t,
                                       tokens_per_mod=tokens_per_mod, seq_len=seq_len,
                                       on_grid=on_grid, tm=256)
    z, s_f, s_b = _retention(q, k, v, gsw, w["dec"], s0f, s0b, batch=batch, seq_len=seq_len)
    p = _pool(up, w["pool_w"], w["pool_scale"], batch=batch, seq_len=seq_len, on_grid=on_grid)
    x1, h2, h2_pieces = _merge(x, z, p, ga, gb, mods, w["norm2_g"], w["w_br_ret"], w["w_br_pool"],
                               w["w_out"], tokens_per_mod=tokens_per_mod, tm=512)

    group_tile = _group_tile(t)
    n_rows = t * TOP_K + N_EXPERTS * group_tile
    idx, rank, wt, counts = _route(h2, w["router_wt"], w["router_bias"], tm=512)
    min_tiles = int(expert_w[0].dtype != BF16)
    pos, tile_expert, n_used = _plan(idx, rank, counts, n_tiles=n_rows // group_tile, tf=512,
                                     group_tile=group_tile, min_tiles=min_tiles)
    x_sorted = _sc_dispatch(h2_pieces, pos, n_rows=n_rows)
    y_sorted, expert_w = _experts(x_sorted, tile_expert.reshape(-1), n_used.reshape(-1), *expert_w,
                                  group_tile=group_tile)
    y_tok = _sc_collect(y_sorted, pos, n_tokens=t)
    y = _moe_out(h2, wt, x1, mods, w["final_g"], w["sh_w_gate"], w["sh_w_up"], w["sh_w_down"], y_tok,
                 tokens_per_mod=tokens_per_mod, tm=512)
    return y, s_f, s_b, expert_w


def kernel(x_prompt, x_sample, state_ret_fwd, state_ret_bwd, c, c_ctx, ada_w, ada_b, norm1_g, norm2_g, w_in,
           ret_decay_fwd, ret_decay_bwd, w_br_ret, pool_w, pool_scale, w_br_pool, w_out, router_w, router_bias,
           exp_w_gate, exp_w_up, exp_w_down, sh_w_gate, sh_w_up, sh_w_down, final_norm_g):
    n_req, seq, d = x_prompt.shape
    n_dec, dec_seq, _ = x_sample.shape
    depth = ada_w.shape[0]
    assert depth == 1 and d == D_MODEL

    xc = x_prompt.reshape(n_req * seq, d)
    xs = x_sample.reshape(n_dec * dec_seq, d)
    zero_state = jnp.zeros((n_req, RET_HEADS, RET_DK, RET_DV), F32)
    new_f, new_b = [], []
    for l in range(depth):
        c_rows = jnp.concatenate([c_ctx[None, :], c, jnp.zeros((8 - 1 - n_dec, d), F32)], axis=0)
        mods = _ada(c_rows, ada_w[l], ada_b[l]).reshape(8, 6, d)
        pad_rows = LANES - N_EXPERTS
        w = dict(
            norm1_g=norm1_g[l].reshape(1, d), norm2_g=norm2_g[l].reshape(1, d),
            final_g=final_norm_g.reshape(1, d),
            w_in=w_in[l].astype(BF16),
            dec=jnp.stack([ret_decay_fwd[l], ret_decay_bwd[l]]).astype(F32),
            w_br_ret=w_br_ret[l].astype(BF16), pool_w=pool_w[l].astype(BF16),
            pool_scale=pool_scale[l].reshape(1, POOL_W), w_br_pool=w_br_pool[l].astype(BF16),
            w_out=w_out[l].astype(BF16),
            router_wt=jnp.pad(router_w[l].T, ((0, pad_rows), (0, 0))).astype(BF16),
            router_bias=jnp.pad(router_bias[l].astype(F32).reshape(N_EXPERTS, 1), ((0, pad_rows), (0, 0))),
            sh_w_gate=sh_w_gate[l].astype(BF16),
            sh_w_up=sh_w_up[l].astype(BF16), sh_w_down=sh_w_down[l].astype(BF16),
        )
        expert_w = (exp_w_gate[l], exp_w_up[l], exp_w_down[l])
        xs, _, _, expert_w = _trunk(xs, mods[1:1 + n_dec], state_ret_fwd[:, l].astype(F32),
                                    state_ret_bwd[:, l].astype(F32), w, expert_w,
                                    batch=n_dec, seq_len=dec_seq, on_grid=True)
        xc, s_f, s_b, _ = _trunk(xc, mods[0:1], zero_state, zero_state, w, expert_w,
                                 batch=n_req, seq_len=seq, on_grid=False)
        new_f.append(s_f)
        new_b.append(s_b)
    y_prompt = xc.reshape(n_req, seq, d)
    y_sample = xs.reshape(n_dec, dec_seq, d)
    return (y_prompt, y_sample, jnp.stack(new_f, axis=1).astype(x_prompt.dtype),
            jnp.stack(new_b, axis=1).astype(x_prompt.dtype))
```

```python
import functools

import numpy as np
import jax
import jax.numpy as jnp
from jax import lax
from jax.experimental import pallas as pl
from jax.experimental.pallas import tpu as pltpu
from jax.experimental.pallas import tpu_sc as plsc

D_MODEL = 1024
GRID_W = 64
RET_HEADS = 4
RET_DK = 128
RET_DV = 256
RET_QK_W = RET_HEADS * RET_DK
RET_V_W = RET_HEADS * RET_DV
RET_CHUNK = 128
ROPE_BASE = 10000.0
POOL_GROUPS = 4
POOL_CH = 128
POOL_W = POOL_GROUPS * POOL_CH
POOL_WINDOWS = (2, 4, 8, 16)
N_EXPERTS = 64
TOP_K = 8
N_EXPERT_GROUPS = 8
GROUP_SIZE = N_EXPERTS // N_EXPERT_GROUPS
TOPK_GROUPS = 4
D_EXPERT = 256
ROUTED_SCALE = 2.5
EPS = 1e-6
IN_SIZES = (RET_QK_W, RET_QK_W, RET_V_W, RET_V_W, POOL_W, D_MODEL, D_MODEL)
IN_OFFS = tuple(sum(IN_SIZES[:i]) for i in range(len(IN_SIZES) + 1))
IN_W = IN_OFFS[-1]

LANES = 128
VMEM_LIMIT = 56 << 20
N_PIECES = D_MODEL // 2 // LANES
MXU_DIM = 256
SC_CHUNK = 128

F32 = jnp.float32
BF16 = jnp.bfloat16
I32 = jnp.int32
U32 = jnp.uint32


def _cparams(*sem):
    return pltpu.CompilerParams(dimension_semantics=sem, vmem_limit_bytes=VMEM_LIMIT)


def _dot(a, b):
    return jnp.dot(a, b, preferred_element_type=F32)


def _silu(x):
    return x * jax.nn.sigmoid(x)


def _rms_mod(x, g, scale, shift):
    y = x * lax.rsqrt(jnp.mean(x * x, axis=-1, keepdims=True) + EPS)
    return (y * g) * (1.0 + scale) + shift


def _ada_kernel(c_ref, w_ref, b_ref, o_ref):
    c = c_ref[...]
    o_ref[...] = jnp.dot(_silu(c), w_ref[...], preferred_element_type=F32,
                         precision=lax.Precision.HIGHEST) + b_ref[...]


def _ada(c_rows, ada_w, ada_b):
    r = c_rows.shape[0]
    n = ada_w.shape[1]
    tn = D_MODEL
    return pl.pallas_call(
        _ada_kernel,
        grid=(n // tn,),
        in_specs=[pl.BlockSpec((r, D_MODEL), lambda j: (0, 0)),
                  pl.BlockSpec((D_MODEL, tn), lambda j: (0, j)),
                  pl.BlockSpec((1, tn), lambda j: (0, j))],
        out_specs=pl.BlockSpec((r, tn), lambda j: (0, j)),
        out_shape=jax.ShapeDtypeStruct((r, n), F32),
        compiler_params=_cparams("parallel"),
        name="ada_mod",
    )(c_rows, ada_w, ada_b.reshape(1, n))


def _inproj_kernel(x_ref, mod_ref, g_ref, w_ref, cos_ref, sin_ref,
                   q_ref, k_ref, v_ref, gsw_ref, up_ref, ga_ref, gb_ref, *, on_grid):
    x = x_ref[...]
    h = _rms_mod(x, g_ref[...], mod_ref[0, 1:2, :], mod_ref[0, 0:1, :]).astype(BF16)

    def seg(i):
        return _dot(h, w_ref[:, IN_OFFS[i]:IN_OFFS[i + 1]])

    q = seg(0)
    k = seg(1)
    if on_grid:
        cos = jnp.concatenate([cos_ref[...]] * RET_HEADS, axis=1)
        sin = jnp.concatenate([sin_ref[...]] * RET_HEADS, axis=1)
        lane = lax.broadcasted_iota(jnp.int32, q.shape, 1)
        first = (lane & 63) < 32

        def rope(a):
            up = pltpu.roll(a, RET_QK_W - 32, axis=1)
            dn = pltpu.roll(a, 32, axis=1)
            return a * cos + jnp.where(first, up, dn) * sin

        q = rope(q)
        k = rope(k)
    q_ref[...] = q.astype(BF16)
    k_ref[...] = (k * (RET_DK ** -0.5)).astype(BF16)
    v_ref[...] = seg(2).astype(BF16)
    gsw_ref[...] = seg(3).astype(BF16)
    up_ref[...] = seg(4).astype(BF16)
    ga_ref[...] = seg(5).astype(BF16)
    gb_ref[...] = seg(6).astype(BF16)


def _inproj(x, mods, norm_g, w_in, cos_t, sin_t, *, tokens_per_mod, seq_len, on_grid, tm):
    t = x.shape[0]
    tiles_per_mod = tokens_per_mod // tm
    tiles_per_seq = seq_len // tm
    widths = IN_SIZES
    out_shape = [jax.ShapeDtypeStruct((t, w), BF16) for w in widths]
    out_specs = [pl.BlockSpec((tm, w), lambda i: (i, 0)) for w in widths]
    return pl.pallas_call(
        functools.partial(_inproj_kernel, on_grid=on_grid),
        grid=(t // tm,),
        in_specs=[pl.BlockSpec((tm, D_MODEL), lambda i: (i, 0)),
                  pl.BlockSpec((1, 6, D_MODEL), lambda i: (i // tiles_per_mod, 0, 0)),
                  pl.BlockSpec((1, D_MODEL), lambda i: (0, 0)),
                  pl.BlockSpec((D_MODEL, IN_W), lambda i: (0, 0)),
                  pl.BlockSpec((tm, RET_DK), lambda i: (i % tiles_per_seq, 0)),
                  pl.BlockSpec((tm, RET_DK), lambda i: (i % tiles_per_seq, 0))],
        out_specs=out_specs,
        out_shape=out_shape,
        compiler_params=_cparams("parallel"),
        name="inproj_grid" if on_grid else "inproj_seq",
    )(x, mods, norm_g, w_in, cos_t, sin_t)


def _rope_tables(seq_len):
    t = np.arange(seq_len)
    row = (t // GRID_W).astype(np.float32)
    col = (t % GRID_W).astype(np.float32)
    m = RET_DK // 4
    inv = (np.float32(ROPE_BASE) ** (-np.arange(m, dtype=np.float32) / np.float32(m))).astype(np.float32)
    ar = row[:, None] * inv
    ac = col[:, None] * inv
    cos = np.concatenate([np.cos(ar), np.cos(ar), np.cos(ac), np.cos(ac)], axis=1)
    sin = np.concatenate([-np.sin(ar), np.sin(ar), -np.sin(ac), np.sin(ac)], axis=1)
    return jnp.asarray(cos, F32), jnp.asarray(sin, F32)


RET_HEADS_PER_STEP = 2


def _ret_kernel(dec_ref, q_ref, k_ref, v_ref, g_ref, s0f_ref, s0b_ref,
                z_ref, sf_ref, sb_ref, oacc_ref, kt_ref, *, n_chunks):
    c = RET_CHUNK
    heads = RET_HEADS_PER_STEP
    half = n_chunks // 2
    ii = lax.broadcasted_iota(I32, (c, c), 0)
    jj = lax.broadcasted_iota(I32, (c, c), 1)
    ik = lax.broadcasted_iota(I32, (c, RET_DK), 0).astype(F32)
    jk = lax.broadcasted_iota(I32, (RET_DK, c), 1).astype(F32)

    def log_gamma(d, shape):
        return jnp.log1p(-jnp.exp2(-jnp.full(shape, d, F32)))

    consts = {}
    for hh in range(heads):
        h = pl.program_id(1) * heads + hh
        dec_f = dec_ref[0, h]
        dec_b = dec_ref[1, h]
        rel = (ii - jj).astype(F32)
        consts[hh, "f"] = (
            jnp.where(rel >= 0, jnp.exp(log_gamma(dec_f, (c, c)) * jnp.maximum(rel, 0.0)), 0.0),
            jnp.exp(log_gamma(dec_f, (c, RET_DK)) * (ik + 1.0)),
            jnp.exp(log_gamma(dec_f, (RET_DK, c)) * (c - 1.0 - jk)),
            jnp.exp(log_gamma(dec_f, (RET_DK, RET_DV)) * c))
        consts[hh, "b"] = (
            jnp.where(rel <= 0, jnp.exp(log_gamma(dec_b, (c, c)) * jnp.maximum(-rel, 0.0)), 0.0),
            jnp.exp(log_gamma(dec_b, (c, RET_DK)) * (c - ik)),
            jnp.exp(log_gamma(dec_b, (RET_DK, c)) * jk),
            jnp.exp(log_gamma(dec_b, (RET_DK, RET_DV)) * c))

    sf_ref[...] = s0f_ref[...]
    sb_ref[...] = s0b_ref[...]

    def transpose_keys(ci, carry):
        r = pl.ds(pl.multiple_of(ci * c, c), c)
        for hh in range(heads):
            kt_ref[hh, ci] = k_ref[r, hh * RET_DK:(hh + 1) * RET_DK].T
        return carry

    lax.fori_loop(0, n_chunks, transpose_keys, 0)

    def scan_chunk(ci, hh, direction, second):
        dmask, qdec, kdec, cdec = consts[hh, direction]
        s_ref = sf_ref if direction == "f" else sb_ref
        r = pl.ds(pl.multiple_of(ci * c, c), c)
        kcols = slice(hh * RET_DK, (hh + 1) * RET_DK)
        vcols = slice(hh * RET_DV, (hh + 1) * RET_DV)
        qc = q_ref[r, kcols]
        kc = k_ref[r, kcols]
        vc = v_ref[r, vcols]
        s = s_ref[hh]
        sc = lax.dot_general(qc, kc, (((1,), (1,)), ((), ())), preferred_element_type=F32)
        lhs = jnp.concatenate([(sc * dmask).astype(BF16), (qc.astype(F32) * qdec).astype(BF16)], axis=1)
        o = _dot(lhs, jnp.concatenate([vc, s.astype(BF16)], axis=0))
        kd_t = (kt_ref[hh, ci].astype(F32) * kdec).astype(BF16)
        s_ref[hh] = s * cdec + _dot(kd_t, vc)
        if not second:
            oacc_ref[hh, r, :] = o
        else:
            o = o + oacc_ref[hh, r, :]
            o = o * lax.rsqrt(jnp.mean(o * o, axis=-1, keepdims=True) + EPS)
            g = g_ref[r, vcols].astype(F32)
            z_ref[r, vcols] = (_silu(g) * o).astype(BF16)

    def body(second):
        def run(t, carry):
            for hh in range(heads):
                scan_chunk(t, hh, "f", second)
                scan_chunk(n_chunks - 1 - t, hh, "b", second)
            return carry
        return run

    lax.fori_loop(0, half, body(False), 0)
    lax.fori_loop(half, n_chunks, body(True), 0)


def _retention(q, k, v, gsw, dec, s0f, s0b, *, batch, seq_len):
    n_chunks = seq_len // RET_CHUNK
    assert n_chunks % 2 == 0
    heads = RET_HEADS_PER_STEP
    t = batch * seq_len
    st_spec = pl.BlockSpec((None, heads, RET_DK, RET_DV), lambda b, h: (b, h, 0, 0))
    st_shape = jax.ShapeDtypeStruct((batch, RET_HEADS, RET_DK, RET_DV), F32)
    kspec = pl.BlockSpec((seq_len, heads * RET_DK), lambda b, h: (b, h))
    vspec = pl.BlockSpec((seq_len, heads * RET_DV), lambda b, h: (b, h))
    return pl.pallas_call(
        functools.partial(_ret_kernel, n_chunks=n_chunks),
        grid=(batch, RET_HEADS // heads),
        in_specs=[pl.BlockSpec(memory_space=pltpu.SMEM), kspec, kspec, vspec, vspec, st_spec, st_spec],
        out_specs=[vspec, st_spec, st_spec],
        out_shape=[jax.ShapeDtypeStruct((t, RET_V_W), BF16), st_shape, st_shape],
        scratch_shapes=[pltpu.VMEM((heads, seq_len, RET_DV), F32),
                        pltpu.VMEM((heads, n_chunks, RET_DK, RET_CHUNK), BF16)],
        compiler_params=_cparams("parallel", "parallel"),
        name=f"retention_l{seq_len}",
    )(dec, q, k, v, gsw, s0f, s0b)


def _pool_kernel(u_ref, w_ref, sc_ref, o_ref, *, seq_len, on_grid):
    tok = lax.broadcasted_iota(jnp.int32, (seq_len, POOL_CH), 0)

    def shift(a, s, stride, pos, width):
        y = pltpu.roll(a, (-s * stride) % seq_len, axis=0)
        ok = (pos < width - s) if s > 0 else (pos >= -s)
        return jnp.where(ok, y, 0.0)

    def box_mean(a, window, stride, pos, width):
        half = window // 2
        fw = a
        bw = shift(a, -1, stride, pos, width)
        m = 1
        while m < half:
            fw = fw + shift(fw, m, stride, pos, width)
            bw = bw + shift(bw, -m, stride, pos, width)
            m *= 2
        cnt = jnp.minimum(pos + half, width) - jnp.maximum(pos - half, 0)
        return (fw + bw) / cnt.astype(F32)

    for g, window in enumerate(POOL_WINDOWS):
        cols = slice(g * POOL_CH, (g + 1) * POOL_CH)
        ug = u_ref[:, cols].astype(F32)
        if on_grid:
            pooled = box_mean(ug, window, 1, tok & (GRID_W - 1), GRID_W)
            pooled = box_mean(pooled, window, GRID_W, tok >> 6, seq_len // GRID_W)
        else:
            pooled = box_mean(ug, window, 1, tok, seq_len)
        d = (pooled - ug).astype(BF16)
        o_ref[:, cols] = (_dot(d, w_ref[g]) * sc_ref[:, cols]).astype(BF16)


def _pool(u, pool_w, pool_scale, *, batch, seq_len, on_grid):
    t = batch * seq_len
    return pl.pallas_call(
        functools.partial(_pool_kernel, seq_len=seq_len, on_grid=on_grid),
        grid=(batch,),
        in_specs=[pl.BlockSpec((seq_len, POOL_W), lambda b: (b, 0)),
                  pl.BlockSpec((POOL_GROUPS, POOL_CH, POOL_CH), lambda b: (0, 0, 0)),
                  pl.BlockSpec((1, POOL_W), lambda b: (0, 0))],
        out_specs=pl.BlockSpec((seq_len, POOL_W), lambda b: (b, 0)),
        out_shape=jax.ShapeDtypeStruct((t, POOL_W), BF16),
        compiler_params=_cparams("parallel"),
        name=f"pool_l{seq_len}",
    )(u, pool_w, pool_scale)


def _pack_rows(x):
    half = D_MODEL // 2
    lo = lax.bitcast_convert_type(x[:, :half].astype(BF16).astype(F32), U32) >> 16
    hi = lax.bitcast_convert_type(x[:, half:].astype(BF16).astype(F32), U32) & jnp.uint32(0xFFFF0000)
    word = lax.bitcast_convert_type(hi | lo, I32)
    return [word[:, c * LANES:(c + 1) * LANES] for c in range(N_PIECES)]


def _unpack_rows(pieces):
    words = [lax.bitcast_convert_type(p, U32) for p in pieces]
    lo = [lax.bitcast_convert_type(w << 16, F32) for w in words]
    hi = [lax.bitcast_convert_type(w & jnp.uint32(0xFFFF0000), F32) for w in words]
    return lo, hi


def _merge_kernel(x_ref, z_ref, p_ref, ga_ref, gb_ref, mod_ref, g2_ref, wr_ref, wp_ref, wo_ref,
                  x1_ref, h2_ref, *piece_refs):
    y_ret = _dot(z_ref[...], wr_ref[...])
    y_pool = _dot(p_ref[...], wp_ref[...])
    merged = (jax.nn.sigmoid(ga_ref[...].astype(F32)) * y_ret
              + jax.nn.sigmoid(gb_ref[...].astype(F32)) * y_pool)
    x1 = x_ref[...] + mod_ref[0, 2:3, :] * _dot(merged.astype(BF16), wo_ref[...])
    x1_ref[...] = x1
    h2 = _rms_mod(x1, g2_ref[...], mod_ref[0, 4:5, :], mod_ref[0, 3:4, :])
    h2_ref[...] = h2.astype(BF16)
    for ref, piece in zip(piece_refs, _pack_rows(h2)):
        ref[...] = piece


def _merge(x, z, p, ga, gb, mods, norm2_g, w_br_ret, w_br_pool, w_out, *, tokens_per_mod, tm):
    t = x.shape[0]
    tiles_per_mod = tokens_per_mod // tm
    row = lambda w: pl.BlockSpec((tm, w), lambda i: (i, 0))
    full = lambda a: pl.BlockSpec(a.shape, lambda i: (0,) * a.ndim)
    outs = pl.pallas_call(
        _merge_kernel,
        grid=(t // tm,),
        in_specs=[row(D_MODEL), row(RET_V_W), row(POOL_W), row(D_MODEL), row(D_MODEL),
                  pl.BlockSpec((1, 6, D_MODEL), lambda i: (i // tiles_per_mod, 0, 0)),
                  full(norm2_g), full(w_br_ret), full(w_br_pool), full(w_out)],
        out_specs=[row(D_MODEL), row(D_MODEL)] + [row(LANES)] * N_PIECES,
        out_shape=[jax.ShapeDtypeStruct((t, D_MODEL), F32), jax.ShapeDtypeStruct((t, D_MODEL), BF16)]
        + [jax.ShapeDtypeStruct((t, LANES), I32)] * N_PIECES,
        compiler_params=_cparams("parallel"),
        name="merge",
    )(x, z, p, ga, gb, mods, norm2_g, w_br_ret, w_br_pool, w_out)
    return outs[0], outs[1], outs[2:]


def _route_kernel(h_ref, rw_ref, bias_ref, idx_ref, rank_ref, wt_ref, cnt_ref, carry_ref):
    e = N_EXPERTS
    tm = h_ref.shape[0]
    neg = -jnp.inf

    @pl.when(pl.program_id(0) == 0)
    def _():
        carry_ref[...] = jnp.zeros(carry_ref.shape, F32)

    logits = lax.dot_general(rw_ref[...], h_ref[...], (((1,), (1,)), ((), ())),
                             preferred_element_type=F32)[:e]
    scores = jax.nn.sigmoid(logits)
    sel = scores + bias_ref[:e, 0:1]
    e_idx = lax.broadcasted_iota(I32, (e, tm), 0)

    grp = sel.reshape(N_EXPERT_GROUPS, GROUP_SIZE, tm)
    m_idx = lax.broadcasted_iota(I32, grp.shape, 1)
    m1 = jnp.max(grp, axis=1, keepdims=True)
    first = jnp.min(jnp.where(grp == m1, m_idx, GROUP_SIZE), axis=1, keepdims=True)
    m2 = jnp.max(jnp.where(m_idx == first, neg, grp), axis=1, keepdims=True)
    gscore = (m1 + m2).reshape(N_EXPERT_GROUPS, tm)

    g_idx = lax.broadcasted_iota(I32, gscore.shape, 0)
    grank = jnp.zeros(gscore.shape, I32)
    for g in range(N_EXPERT_GROUPS):
        other = gscore[g:g + 1, :]
        beats = jnp.where(other > gscore, 1, jnp.where(other == gscore, (g_idx > g).astype(I32), 0))
        grank = grank + beats
    gkeep = (grank < TOPK_GROUPS).astype(F32)
    ekeep = jnp.broadcast_to(gkeep.reshape(N_EXPERT_GROUPS, 1, tm), grp.shape).reshape(e, tm)
    masked = jnp.where(ekeep > 0, sel, neg)

    chosen = jnp.zeros((e, tm), F32)
    picks, hits = [], []
    for _ in range(TOP_K):
        m = jnp.max(masked, axis=0, keepdims=True)
        pick = jnp.min(jnp.where(masked == m, e_idx, e), axis=0, keepdims=True)
        hit = e_idx == pick
        chosen = jnp.where(hit, 1.0, chosen)
        masked = jnp.where(hit, neg, masked)
        picks.append(pick)
        hits.append(hit)

    w = scores * chosen
    comb = w / jnp.sum(w, axis=0, keepdims=True) * ROUTED_SCALE

    t_row = lax.broadcasted_iota(I32, (tm, tm), 0)
    t_col = lax.broadcasted_iota(I32, (tm, tm), 1)
    before = (t_row < t_col).astype(BF16)
    rankmat = _dot(chosen.astype(BF16), before) + carry_ref[:e, 0:1]
    carry_ref[:e, :] = carry_ref[:e, :] + jnp.sum(chosen, axis=1, keepdims=True)
    cnt_ref[...] = carry_ref[...]

    idx_ref[...] = jnp.concatenate(picks, axis=0)
    rank_ref[...] = jnp.concatenate(
        [jnp.sum(jnp.where(h, rankmat, 0.0), axis=0, keepdims=True) for h in hits], axis=0).astype(I32)
    w_rows = [jnp.sum(jnp.where(h, comb, 0.0), axis=0, keepdims=True) for h in hits]
    wt_ref[...] = jnp.concatenate(w_rows + [jnp.zeros((LANES - TOP_K, tm), F32)], axis=0).T


def _route(h2, router_wt, bias_col, *, tm):
    t = h2.shape[0]
    krow = pl.BlockSpec((TOP_K, tm), lambda i: (0, i))
    return pl.pallas_call(
        _route_kernel,
        grid=(t // tm,),
        in_specs=[pl.BlockSpec((tm, D_MODEL), lambda i: (i, 0)),
                  pl.BlockSpec((LANES, D_MODEL), lambda i: (0, 0)),
                  pl.BlockSpec((LANES, 1), lambda i: (0, 0))],
        out_specs=[krow, krow, pl.BlockSpec((tm, LANES), lambda i: (i, 0)),
                   pl.BlockSpec((LANES, LANES), lambda i: (0, 0))],
        out_shape=[jax.ShapeDtypeStruct((TOP_K, t), I32), jax.ShapeDtypeStruct((TOP_K, t), I32),
                   jax.ShapeDtypeStruct((t, LANES), F32), jax.ShapeDtypeStruct((LANES, LANES), F32)],
        scratch_shapes=[pltpu.VMEM((LANES, LANES), F32)],
        compiler_params=_cparams("arbitrary"),
        name="route",
    )(h2, router_wt, bias_col)


def _plan_kernel(idx_ref, rank_ref, cnt_ref, pos_ref, te_ref, nu_ref, *, group_tile, min_tiles):
    tf = idx_ref.shape[1]
    nt = te_ref.shape[1]
    cnt = cnt_ref[...].astype(I32)
    tiles = jnp.maximum((cnt + (group_tile - 1)) // group_tile, min_tiles)
    e_sub = lax.broadcasted_iota(I32, (LANES, LANES), 0)
    padded = jnp.where(e_sub < N_EXPERTS, tiles * group_tile, 0).astype(F32)
    e_lane = lax.broadcasted_iota(I32, (LANES, LANES), 1)
    base = jnp.sum(jnp.where(e_lane < e_sub, padded.T, 0.0), axis=1, keepdims=True)
    end = base + padded[:, 0:1]

    idx = idx_ref[...]
    start = jnp.zeros(idx.shape, F32)
    for e in range(N_EXPERTS):
        start = jnp.where(idx == e, base[e:e + 1, 0:1], start)
    pos = start.astype(I32) + rank_ref[...]
    for j in range(tf // SC_CHUNK):
        pos_ref[j] = pos[:, j * SC_CHUNK:(j + 1) * SC_CHUNK]

    tile_start = (lax.broadcasted_iota(I32, (N_EXPERTS, nt), 1) * group_tile).astype(F32)
    done = jnp.sum(jnp.where(end[:N_EXPERTS] <= tile_start, 1.0, 0.0), axis=0, keepdims=True)
    te_ref[...] = jnp.minimum(done, N_EXPERTS - 1.0).astype(I32)
    total = jnp.sum(padded[:, 0:1], axis=0, keepdims=True)
    nu_ref[...] = jnp.broadcast_to(total * (1.0 / group_tile), nu_ref.shape).astype(I32)


def _plan(idx, rank, counts, *, n_tiles, tf, group_tile, min_tiles):
    t = idx.shape[1]
    nt_pad = -(-n_tiles // LANES) * LANES
    krow = pl.BlockSpec((TOP_K, tf), lambda i: (0, i))
    return pl.pallas_call(
        functools.partial(_plan_kernel, group_tile=group_tile, min_tiles=min_tiles),
        grid=(t // tf,),
        in_specs=[krow, krow, pl.BlockSpec((LANES, LANES), lambda i: (0, 0))],
        out_specs=[pl.BlockSpec((tf // SC_CHUNK, TOP_K, SC_CHUNK), lambda i: (i, 0, 0)),
                   pl.BlockSpec((1, nt_pad), lambda i: (0, 0)),
                   pl.BlockSpec((1, LANES), lambda i: (0, 0))],
        out_shape=[jax.ShapeDtypeStruct((t // SC_CHUNK, TOP_K, SC_CHUNK), I32),
                   jax.ShapeDtypeStruct((1, nt_pad), I32), jax.ShapeDtypeStruct((1, LANES), I32)],
        compiler_params=_cparams("arbitrary"),
        name="moe_plan",
    )(idx, rank, counts)


def _sc_mesh_info():
    info = plsc.get_sparse_core_info()
    mesh = plsc.VectorSubcoreMesh(core_axis_name="c", subcore_axis_name="s")
    return mesh, info.num_cores, info.num_cores * info.num_subcores


def _sc_dispatch(pieces, pos, *, n_rows):
    t = pieces[0].shape[0]
    mesh, n_cores, n_workers = _sc_mesh_info()
    per_w = t // SC_CHUNK // n_workers

    @functools.partial(
        pl.kernel, mesh=mesh,
        out_type=[jax.ShapeDtypeStruct((n_rows, LANES), I32)] * N_PIECES,
        scratch_types=[pltpu.VMEM((TOP_K, SC_CHUNK), I32),
                       pltpu.VMEM((N_PIECES, SC_CHUNK, LANES), I32),
                       pltpu.SemaphoreType.DMA((N_PIECES,)),
                       pltpu.SemaphoreType.DMA],
        name="sc_dispatch",
    )
    def run(*refs):
        src = refs[:N_PIECES]
        pos_hbm = refs[N_PIECES]
        dst = refs[N_PIECES + 1:2 * N_PIECES + 1]
        idx_v, rows_v, load_sem, put_sem = refs[2 * N_PIECES + 1:]
        wid = lax.axis_index("s") * n_cores + lax.axis_index("c")

        @pl.loop(0, per_w)
        def _(j):
            ch = wid * per_w + j
            t0 = pl.multiple_of(ch * SC_CHUNK, SC_CHUNK)
            loads = [pltpu.make_async_copy(src[c].at[pl.ds(t0, SC_CHUNK)], rows_v.at[c], load_sem.at[c])
                     for c in range(N_PIECES)]
            for ld in loads:
                ld.start()
            pltpu.sync_copy(pos_hbm.at[ch], idx_v)
            puts = []
            for c in range(N_PIECES):
                loads[c].wait()
                for k in range(TOP_K):
                    puts.append(pltpu.make_async_copy(rows_v.at[c], dst[c].at[idx_v.at[k]], put_sem))
                    puts[-1].start()
            for cp in puts:
                cp.wait()

    return run(*pieces, pos)


def _sc_collect(pieces, pos, *, n_tokens):
    mesh, n_cores, n_workers = _sc_mesh_info()
    per_w = n_tokens // SC_CHUNK // n_workers
    n_buf = 4
    lag = n_buf // 2
    units = [(k, c) for k in range(TOP_K) for c in range(N_PIECES)]

    @functools.partial(
        pl.kernel, mesh=mesh,
        out_type=[jax.ShapeDtypeStruct((TOP_K, n_tokens, LANES), I32)] * N_PIECES,
        scratch_types=[pltpu.VMEM((TOP_K, SC_CHUNK), I32),
                       pltpu.VMEM((n_buf, SC_CHUNK, LANES), I32),
                       pltpu.SemaphoreType.DMA((n_buf,)),
                       pltpu.SemaphoreType.DMA((n_buf,))],
        name="sc_collect",
    )
    def run(*refs):
        src = refs[:N_PIECES]
        pos_hbm = refs[N_PIECES]
        dst = refs[N_PIECES + 1:2 * N_PIECES + 1]
        idx_v, buf, get_sem, put_sem = refs[2 * N_PIECES + 1:]
        wid = lax.axis_index("s") * n_cores + lax.axis_index("c")

        @pl.loop(0, per_w)
        def _(j):
            ch = wid * per_w + j
            t0 = pl.multiple_of(ch * SC_CHUNK, SC_CHUNK)
            pltpu.sync_copy(pos_hbm.at[ch], idx_v)

            def get(u):
                k, c = units[u]
                return pltpu.make_async_copy(src[c].at[idx_v.at[k]], buf.at[u % n_buf], get_sem.at[u % n_buf])

            def put(u):
                k, c = units[u]
                return pltpu.make_async_copy(buf.at[u % n_buf], dst[c].at[k, pl.ds(t0, SC_CHUNK)],
                                             put_sem.at[u % n_buf])

            n = len(units)
            for u in range(n + lag):
                if u < n:
                    if u >= n_buf:
                        put(u - n_buf).wait()
                    get(u).start()
                if 0 <= u - lag < n:
                    get(u - lag).wait()
                    put(u - lag).start()
            for u in range(n - n_buf, n):
                put(u).wait()

    return run(*pieces, pos)


def _group_tile(n_tokens):
    per_expert = n_tokens * TOP_K // N_EXPERTS
    return max(MXU_DIM, min(4 * MXU_DIM, per_expert // MXU_DIM * MXU_DIM))


def _experts_kernel(te_ref, nu_ref, *refs, cast_weights):
    x_refs = refs[:N_PIECES]
    w_in = refs[N_PIECES:N_PIECES + 3]
    y_refs = refs[N_PIECES + 3:2 * N_PIECES + 3]
    i = pl.program_id(0)

    if cast_weights:
        w_bf = refs[2 * N_PIECES + 3:]
        last = nu_ref[0] - 1
        cur = te_ref[jnp.minimum(i, last)]
        prev = te_ref[jnp.minimum(jnp.maximum(i - 1, 0), last)]

        @pl.when((i == 0) | (cur != prev))
        def _():
            for dst, src in zip(w_bf, w_in):
                dst[...] = src[...].astype(BF16)
    else:
        w_bf = w_in
    wg_ref, wu_ref, wd_ref = w_bf

    @pl.when(i < nu_ref[0])
    def _():
        for s in range(x_refs[0].shape[0] // MXU_DIM):
            rows = slice(s * MXU_DIM, (s + 1) * MXU_DIM)
            lo, hi = _unpack_rows([r[rows, :] for r in x_refs])
            x = jnp.concatenate(lo + hi, axis=1).astype(BF16)
            hid = _silu(_dot(x, wg_ref[...])) * _dot(x, wu_ref[...])
            y = _dot(hid.astype(BF16), wd_ref[...])
            for ref, piece in zip(y_refs, _pack_rows(y)):
                ref[rows, :] = piece


def _experts(x_pieces, tile_expert, n_used, wg, wu, wd, *, group_tile):
    n_rows = x_pieces[0].shape[0]
    n_tiles = n_rows // group_tile
    cast_weights = wg.dtype != BF16

    def tile(i, te, nu):
        return jnp.minimum(i, nu[0] - 1)

    row = pl.BlockSpec((group_tile, LANES), lambda i, te, nu: (tile(i, te, nu), 0))
    wspec = lambda a: pl.BlockSpec((None,) + a.shape[1:], lambda i, te, nu: (te[tile(i, te, nu)], 0, 0))
    w_specs = [wspec(wg), wspec(wu), wspec(wd)]
    y_shape = [jax.ShapeDtypeStruct((n_rows, LANES), I32)] * N_PIECES
    w_shape = [jax.ShapeDtypeStruct(a.shape, BF16) for a in (wg, wu, wd)]
    outs = pl.pallas_call(
        functools.partial(_experts_kernel, cast_weights=cast_weights),
        grid_spec=pltpu.PrefetchScalarGridSpec(
            num_scalar_prefetch=2,
            grid=(n_tiles,),
            in_specs=[row] * N_PIECES + w_specs,
            out_specs=[row] * N_PIECES + (w_specs if cast_weights else [])),
        out_shape=y_shape + (w_shape if cast_weights else []),
        compiler_params=_cparams("arbitrary"),
        name="experts_cast" if cast_weights else "experts",
    )(tile_expert, n_used, *x_pieces, wg, wu, wd)
    return outs[:N_PIECES], (tuple(outs[N_PIECES:]) if cast_weights else (wg, wu, wd))


def _moe_out_kernel(h_ref, wt_ref, x1_ref, mod_ref, fg_ref, sg_ref, su_ref, sd_ref, *refs):
    y_refs = refs[:N_PIECES]
    out_ref = refs[N_PIECES]
    h = h_ref[...]
    hid = _silu(_dot(h, sg_ref[...])) * _dot(h, su_ref[...])
    shared = _dot(hid.astype(BF16), sd_ref[...])
    wt = wt_ref[...]
    lo_acc = [None] * N_PIECES
    hi_acc = [None] * N_PIECES
    for k in range(TOP_K):
        wk = wt[:, k:k + 1]
        lo, hi = _unpack_rows([r[k] for r in y_refs])
        for c in range(N_PIECES):
            lo_acc[c] = wk * lo[c] if k == 0 else lo_acc[c] + wk * lo[c]
            hi_acc[c] = wk * hi[c] if k == 0 else hi_acc[c] + wk * hi[c]
    routed = jnp.concatenate(lo_acc + hi_acc, axis=1)
    x2 = x1_ref[...] + mod_ref[0, 5:6, :] * (shared + routed)
    out_ref[...] = x2 * lax.rsqrt(jnp.mean(x2 * x2, axis=-1, keepdims=True) + EPS) * fg_ref[...]


def _moe_out(h2, wt, x1, mods, final_g, sg, su, sd, y_pieces, *, tokens_per_mod, tm):
    t = h2.shape[0]
    tiles_per_mod = tokens_per_mod // tm
    row = lambda w: pl.BlockSpec((tm, w), lambda i: (i, 0))
    full = lambda a: pl.BlockSpec(a.shape, lambda i: (0,) * a.ndim)
    return pl.pallas_call(
        _moe_out_kernel,
        grid=(t // tm,),
        in_specs=[row(D_MODEL), row(LANES), row(D_MODEL),
                  pl.BlockSpec((1, 6, D_MODEL), lambda i: (i // tiles_per_mod, 0, 0)),
                  full(final_g), full(sg), full(su), full(sd)]
        + [pl.BlockSpec((TOP_K, tm, LANES), lambda i: (0, i, 0))] * N_PIECES,
        out_specs=row(D_MODEL),
        out_shape=jax.ShapeDtypeStruct((t, D_MODEL), F32),
        compiler_params=_cparams("parallel"),
        name="moe_out",
    )(h2, wt, x1, mods, final_g, sg, su, sd, *y_pieces)


def _trunk(x, mods, s0f, s0b, w, expert_w, *, batch, seq_len, on_grid):
    t = batch * seq_len
    tokens_per_mod = t // mods.shape[0]
    cos_t, sin_t = _rope_tables(seq_len)
    q, k, v, gsw, up, ga, gb = _inproj(x, mods, w["norm1_g"], w["w_in"], cos_t, sin_t,
                                       tokens_per_mod=tokens_per_mod, seq_len=seq_len,
                                       on_grid=on_grid, tm=256)
    z, s_f, s_b = _retention(q, k, v, gsw, w["dec"], s0f, s0b, batch=batch, seq_len=seq_len)
    p = _pool(up, w["pool_w"], w["pool_scale"], batch=batch, seq_len=seq_len, on_grid=on_grid)
    x1, h2, h2_pieces = _merge(x, z, p, ga, gb, mods, w["norm2_g"], w["w_br_ret"], w["w_br_pool"],
                               w["w_out"], tokens_per_mod=tokens_per_mod, tm=512)

    group_tile = _group_tile(t)
    n_rows = t * TOP_K + N_EXPERTS * group_tile
    idx, rank, wt, counts = _route(h2, w["router_wt"], w["router_bias"], tm=1024)
    min_tiles = int(expert_w[0].dtype != BF16)
    pos, tile_expert, n_used = _plan(idx, rank, counts, n_tiles=n_rows // group_tile, tf=2048,
                                     group_tile=group_tile, min_tiles=min_tiles)
    x_sorted = _sc_dispatch(h2_pieces, pos, n_rows=n_rows)
    y_sorted, expert_w = _experts(x_sorted, tile_expert.reshape(-1), n_used.reshape(-1), *expert_w,
                                  group_tile=group_tile)
    y_tok = _sc_collect(y_sorted, pos, n_tokens=t)
    y = _moe_out(h2, wt, x1, mods, w["final_g"], w["sh_w_gate"], w["sh_w_up"], w["sh_w_down"], y_tok,
                 tokens_per_mod=tokens_per_mod, tm=512)
    return y, s_f, s_b, expert_w


def kernel(x_prompt, x_sample, state_ret_fwd, state_ret_bwd, c, c_ctx, ada_w, ada_b, norm1_g, norm2_g, w_in,
           ret_decay_fwd, ret_decay_bwd, w_br_ret, pool_w, pool_scale, w_br_pool, w_out, router_w, router_bias,
           exp_w_gate, exp_w_up, exp_w_down, sh_w_gate, sh_w_up, sh_w_down, final_norm_g):
    n_req, seq, d = x_prompt.shape
    n_dec, dec_seq, _ = x_sample.shape
    depth = ada_w.shape[0]
    assert depth == 1 and d == D_MODEL

    xc = x_prompt.reshape(n_req * seq, d)
    xs = x_sample.reshape(n_dec * dec_seq, d)
    zero_state = jnp.zeros((n_req, RET_HEADS, RET_DK, RET_DV), F32)
    new_f, new_b = [], []
    for l in range(depth):
        c_rows = jnp.concatenate([c_ctx[None, :], c, jnp.zeros((8 - 1 - n_dec, d), F32)], axis=0)
        mods = _ada(c_rows, ada_w[l], ada_b[l]).reshape(8, 6, d)
        pad_rows = LANES - N_EXPERTS
        w = dict(
            norm1_g=norm1_g[l].reshape(1, d), norm2_g=norm2_g[l].reshape(1, d),
            final_g=final_norm_g.reshape(1, d),
            w_in=w_in[l].astype(BF16),
            dec=jnp.stack([ret_decay_fwd[l], ret_decay_bwd[l]]).astype(F32),
            w_br_ret=w_br_ret[l].astype(BF16), pool_w=pool_w[l].astype(BF16),
            pool_scale=pool_scale[l].reshape(1, POOL_W), w_br_pool=w_br_pool[l].astype(BF16),
            w_out=w_out[l].astype(BF16),
            router_wt=jnp.pad(router_w[l].T, ((0, pad_rows), (0, 0))).astype(BF16),
            router_bias=jnp.pad(router_bias[l].astype(F32).reshape(N_EXPERTS, 1), ((0, pad_rows), (0, 0))),
            sh_w_gate=sh_w_gate[l].astype(BF16),
            sh_w_up=sh_w_up[l].astype(BF16), sh_w_down=sh_w_down[l].astype(BF16),
        )
        expert_w = (exp_w_gate[l], exp_w_up[l], exp_w_down[l])
        xs, _, _, expert_w = _trunk(xs, mods[1:1 + n_dec], state_ret_fwd[:, l].astype(F32),
                                    state_ret_bwd[:, l].astype(F32), w, expert_w,
                                    batch=n_dec, seq_len=dec_seq, on_grid=True)
        xc, s_f, s_b, _ = _trunk(xc, mods[0:1], zero_state, zero_state, w, expert_w,
                                 batch=n_req, seq_len=seq, on_grid=False)
        new_f.append(s_f)
        new_b.append(s_b)
    y_prompt = xc.reshape(n_req, seq, d)
    y_sample = xs.reshape(n_dec, dec_seq, d)
    return (y_prompt, y_sample, jnp.stack(new_f, axis=1).astype(x_prompt.dtype),
            jnp.stack(new_b, axis=1).astype(x_prompt.dtype))
```

```python
import functools

import numpy as np
import jax
import jax.numpy as jnp
from jax import lax
from jax.experimental import pallas as pl
from jax.experimental.pallas import tpu as pltpu
from jax.experimental.pallas import tpu_sc as plsc

D_MODEL = 1024
GRID_W = 64
RET_HEADS = 4
RET_DK = 128
RET_DV = 256
RET_QK_W = RET_HEADS * RET_DK
RET_V_W = RET_HEADS * RET_DV
RET_CHUNK = 128
ROPE_BASE = 10000.0
POOL_GROUPS = 4
POOL_CH = 128
POOL_W = POOL_GROUPS * POOL_CH
POOL_WINDOWS = (2, 4, 8, 16)
N_EXPERTS = 64
TOP_K = 8
N_EXPERT_GROUPS = 8
GROUP_SIZE = N_EXPERTS // N_EXPERT_GROUPS
TOPK_GROUPS = 4
D_EXPERT = 256
ROUTED_SCALE = 2.5
EPS = 1e-6
IN_SIZES = (RET_QK_W, RET_QK_W, RET_V_W, RET_V_W, POOL_W, D_MODEL, D_MODEL)
IN_OFFS = tuple(sum(IN_SIZES[:i]) for i in range(len(IN_SIZES) + 1))
IN_W = IN_OFFS[-1]

LANES = 128
VMEM_LIMIT = 56 << 20
N_PIECES = D_MODEL // 2 // LANES
MXU_DIM = 256
SC_CHUNK = 128

F32 = jnp.float32
BF16 = jnp.bfloat16
I32 = jnp.int32
U32 = jnp.uint32


def _cparams(*sem):
    return pltpu.CompilerParams(dimension_semantics=sem, vmem_limit_bytes=VMEM_LIMIT)


def _dot(a, b):
    return jnp.dot(a, b, preferred_element_type=F32)


def _silu(x):
    return x * jax.nn.sigmoid(x)


def _rms_mod(x, g, scale, shift):
    y = x * lax.rsqrt(jnp.mean(x * x, axis=-1, keepdims=True) + EPS)
    return (y * g) * (1.0 + scale) + shift


def _ada_kernel(c_ref, w_ref, b_ref, o_ref):
    c = c_ref[...]
    o_ref[...] = jnp.dot(_silu(c), w_ref[...], preferred_element_type=F32,
                         precision=lax.Precision.HIGHEST) + b_ref[...]


def _ada(c_rows, ada_w, ada_b):
    r = c_rows.shape[0]
    n = ada_w.shape[1]
    tn = D_MODEL
    return pl.pallas_call(
        _ada_kernel,
        grid=(n // tn,),
        in_specs=[pl.BlockSpec((r, D_MODEL), lambda j: (0, 0)),
                  pl.BlockSpec((D_MODEL, tn), lambda j: (0, j)),
                  pl.BlockSpec((1, tn), lambda j: (0, j))],
        out_specs=pl.BlockSpec((r, tn), lambda j: (0, j)),
        out_shape=jax.ShapeDtypeStruct((r, n), F32),
        compiler_params=_cparams("parallel"),
        name="ada_mod",
    )(c_rows, ada_w, ada_b.reshape(1, n))


def _inproj_kernel(x_ref, mod_ref, g_ref, w_ref, cos_ref, sin_ref,
                   q_ref, k_ref, v_ref, gsw_ref, up_ref, ga_ref, gb_ref, *, on_grid):
    x = x_ref[...]
    h = _rms_mod(x, g_ref[...], mod_ref[0, 1:2, :], mod_ref[0, 0:1, :]).astype(BF16)

    def seg(i):
        return _dot(h, w_ref[:, IN_OFFS[i]:IN_OFFS[i + 1]])

    q = seg(0)
    k = seg(1)
    if on_grid:
        cos = jnp.concatenate([cos_ref[...]] * RET_HEADS, axis=1)
        sin = jnp.concatenate([sin_ref[...]] * RET_HEADS, axis=1)
        lane = lax.broadcasted_iota(jnp.int32, q.shape, 1)
        first = (lane & 63) < 32

        def rope(a):
            up = pltpu.roll(a, RET_QK_W - 32, axis=1)
            dn = pltpu.roll(a, 32, axis=1)
            return a * cos + jnp.where(first, up, dn) * sin

        q = rope(q)
        k = rope(k)
    q_ref[...] = q.astype(BF16)
    k_ref[...] = (k * (RET_DK ** -0.5)).astype(BF16)
    v_ref[...] = seg(2).astype(BF16)
    gsw_ref[...] = seg(3).astype(BF16)
    up_ref[...] = seg(4).astype(BF16)
    ga_ref[...] = seg(5).astype(BF16)
    gb_ref[...] = seg(6).astype(BF16)


def _inproj(x, mods, norm_g, w_in, cos_t, sin_t, *, tokens_per_mod, seq_len, on_grid, tm):
    t = x.shape[0]
    tiles_per_mod = tokens_per_mod // tm
    tiles_per_seq = seq_len // tm
    widths = IN_SIZES
    out_shape = [jax.ShapeDtypeStruct((t, w), BF16) for w in widths]
    out_specs = [pl.BlockSpec((tm, w), lambda i: (i, 0)) for w in widths]
    return pl.pallas_call(
        functools.partial(_inproj_kernel, on_grid=on_grid),
        grid=(t // tm,),
        in_specs=[pl.BlockSpec((tm, D_MODEL), lambda i: (i, 0)),
                  pl.BlockSpec((1, 6, D_MODEL), lambda i: (i // tiles_per_mod, 0, 0)),
                  pl.BlockSpec((1, D_MODEL), lambda i: (0, 0)),
                  pl.BlockSpec((D_MODEL, IN_W), lambda i: (0, 0)),
                  pl.BlockSpec((tm, RET_DK), lambda i: (i % tiles_per_seq, 0)),
                  pl.BlockSpec((tm, RET_DK), lambda i: (i % tiles_per_seq, 0))],
        out_specs=out_specs,
        out_shape=out_shape,
        compiler_params=_cparams("parallel"),
        name="inproj_grid" if on_grid else "inproj_seq",
    )(x, mods, norm_g, w_in, cos_t, sin_t)


def _rope_tables(seq_len):
    t = np.arange(seq_len)
    row = (t // GRID_W).astype(np.float32)
    col = (t % GRID_W).astype(np.float32)
    m = RET_DK // 4
    inv = (np.float32(ROPE_BASE) ** (-np.arange(m, dtype=np.float32) / np.float32(m))).astype(np.float32)
    ar = row[:, None] * inv
    ac = col[:, None] * inv
    cos = np.concatenate([np.cos(ar), np.cos(ar), np.cos(ac), np.cos(ac)], axis=1)
    sin = np.concatenate([-np.sin(ar), np.sin(ar), -np.sin(ac), np.sin(ac)], axis=1)
    return jnp.asarray(cos, F32), jnp.asarray(sin, F32)


RET_HEADS_PER_STEP = 2


def _ret_kernel(dec_ref, q_ref, k_ref, v_ref, g_ref, s0f_ref, s0b_ref,
                z_ref, sf_ref, sb_ref, oacc_ref, kt_ref, *, n_chunks):
    c = RET_CHUNK
    heads = RET_HEADS_PER_STEP
    half = n_chunks // 2
    ii = lax.broadcasted_iota(I32, (c, c), 0)
    jj = lax.broadcasted_iota(I32, (c, c), 1)
    ik = lax.broadcasted_iota(I32, (c, RET_DK), 0).astype(F32)
    jk = lax.broadcasted_iota(I32, (RET_DK, c), 1).astype(F32)

    def log_gamma(d, shape):
        return jnp.log1p(-jnp.exp2(-jnp.full(shape, d, F32)))

    consts = {}
    for hh in range(heads):
        h = pl.program_id(1) * heads + hh
        dec_f = dec_ref[0, h]
        dec_b = dec_ref[1, h]
        rel = (ii - jj).astype(F32)
        consts[hh, "f"] = (
            jnp.where(rel >= 0, jnp.exp(log_gamma(dec_f, (c, c)) * jnp.maximum(rel, 0.0)), 0.0),
            jnp.exp(log_gamma(dec_f, (c, RET_DK)) * (ik + 1.0)),
            jnp.exp(log_gamma(dec_f, (RET_DK, c)) * (c - 1.0 - jk)),
            jnp.exp(log_gamma(dec_f, (RET_DK, RET_DV)) * c))
        consts[hh, "b"] = (
            jnp.where(rel <= 0, jnp.exp(log_gamma(dec_b, (c, c)) * jnp.maximum(-rel, 0.0)), 0.0),
            jnp.exp(log_gamma(dec_b, (c, RET_DK)) * (c - ik)),
            jnp.exp(log_gamma(dec_b, (RET_DK, c)) * jk),
            jnp.exp(log_gamma(dec_b, (RET_DK, RET_DV)) * c))

    sf_ref[...] = s0f_ref[...]
    sb_ref[...] = s0b_ref[...]

    def transpose_keys(ci, carry):
        r = pl.ds(pl.multiple_of(ci * c, c), c)
        for hh in range(heads):
            kt_ref[hh, ci] = k_ref[r, hh * RET_DK:(hh + 1) * RET_DK].T
        return carry

    lax.fori_loop(0, n_chunks, transpose_keys, 0)

    def scan_chunk(ci, hh, direction, second):
        dmask, qdec, kdec, cdec = consts[hh, direction]
        s_ref = sf_ref if direction == "f" else sb_ref
        r = pl.ds(pl.multiple_of(ci * c, c), c)
        kcols = slice(hh * RET_DK, (hh + 1) * RET_DK)
        vcols = slice(hh * RET_DV, (hh + 1) * RET_DV)
        qc = q_ref[r, kcols]
        kc = k_ref[r, kcols]
        vc = v_ref[r, vcols]
        s = s_ref[hh]
        sc = lax.dot_general(qc, kc, (((1,), (1,)), ((), ())), preferred_element_type=F32)
        lhs = jnp.concatenate([(sc * dmask).astype(BF16), (qc.astype(F32) * qdec).astype(BF16)], axis=1)
        o = _dot(lhs, jnp.concatenate([vc, s.astype(BF16)], axis=0))
        kd_t = (kt_ref[hh, ci].astype(F32) * kdec).astype(BF16)
        s_ref[hh] = s * cdec + _dot(kd_t, vc)
        if not second:
            oacc_ref[hh, r, :] = o
        else:
            o = o + oacc_ref[hh, r, :]
            o = o * lax.rsqrt(jnp.mean(o * o, axis=-1, keepdims=True) + EPS)
            g = g_ref[r, vcols].astype(F32)
            z_ref[r, vcols] = (_silu(g) * o).astype(BF16)

    def body(second):
        def run(t, carry):
            for hh in range(heads):
                scan_chunk(t, hh, "f", second)
                scan_chunk(n_chunks - 1 - t, hh, "b", second)
            return carry
        return run

    lax.fori_loop(0, half, body(False), 0)
    lax.fori_loop(half, n_chunks, body(True), 0)


def _retention(q, k, v, gsw, dec, s0f, s0b, *, batch, seq_len):
    n_chunks = seq_len // RET_CHUNK
    assert n_chunks % 2 == 0
    heads = RET_HEADS_PER_STEP
    t = batch * seq_len
    st_spec = pl.BlockSpec((None, heads, RET_DK, RET_DV), lambda b, h: (b, h, 0, 0))
    st_shape = jax.ShapeDtypeStruct((batch, RET_HEADS, RET_DK, RET_DV), F32)
    kspec = pl.BlockSpec((seq_len, heads * RET_DK), lambda b, h: (b, h))
    vspec = pl.BlockSpec((seq_len, heads * RET_DV), lambda b, h: (b, h))
    return pl.pallas_call(
        functools.partial(_ret_kernel, n_chunks=n_chunks),
        grid=(batch, RET_HEADS // heads),
        in_specs=[pl.BlockSpec(memory_space=pltpu.SMEM), kspec, kspec, vspec, vspec, st_spec, st_spec],
        out_specs=[vspec, st_spec, st_spec],
        out_shape=[jax.ShapeDtypeStruct((t, RET_V_W), BF16), st_shape, st_shape],
        scratch_shapes=[pltpu.VMEM((heads, seq_len, RET_DV), F32),
                        pltpu.VMEM((heads, n_chunks, RET_DK, RET_CHUNK), BF16)],
        compiler_params=_cparams("parallel", "parallel"),
        name=f"retention_l{seq_len}",
    )(dec, q, k, v, gsw, s0f, s0b)


def _pool_kernel(u_ref, w_ref, sc_ref, o_ref, *, seq_len, on_grid):
    tok = lax.broadcasted_iota(jnp.int32, (seq_len, POOL_CH), 0)

    def shift(a, s, stride, pos, width):
        y = pltpu.roll(a, (-s * stride) % seq_len, axis=0)
        ok = (pos < width - s) if s > 0 else (pos >= -s)
        return jnp.where(ok, y, 0.0)

    def box_mean(a, window, stride, pos, width):
        half = window // 2
        fw = a
        bw = shift(a, -1, stride, pos, width)
        m = 1
        while m < half:
            fw = fw + shift(fw, m, stride, pos, width)
            bw = bw + shift(bw, -m, stride, pos, width)
            m *= 2
        cnt = jnp.minimum(pos + half, width) - jnp.maximum(pos - half, 0)
        return (fw + bw) / cnt.astype(F32)

    for g, window in enumerate(POOL_WINDOWS):
        cols = slice(g * POOL_CH, (g + 1) * POOL_CH)
        ug = u_ref[:, cols].astype(F32)
        if on_grid:
            pooled = box_mean(ug, window, 1, tok & (GRID_W - 1), GRID_W)
            pooled = box_mean(pooled, window, GRID_W, tok >> 6, seq_len // GRID_W)
        else:
            pooled = box_mean(ug, window, 1, tok, seq_len)
        d = (pooled - ug).astype(BF16)
        o_ref[:, cols] = (_dot(d, w_ref[g]) * sc_ref[:, cols]).astype(BF16)


def _pool(u, pool_w, pool_scale, *, batch, seq_len, on_grid):
    t = batch * seq_len
    return pl.pallas_call(
        functools.partial(_pool_kernel, seq_len=seq_len, on_grid=on_grid),
        grid=(batch,),
        in_specs=[pl.BlockSpec((seq_len, POOL_W), lambda b: (b, 0)),
                  pl.BlockSpec((POOL_GROUPS, POOL_CH, POOL_CH), lambda b: (0, 0, 0)),
                  pl.BlockSpec((1, POOL_W), lambda b: (0, 0))],
        out_specs=pl.BlockSpec((seq_len, POOL_W), lambda b: (b, 0)),
        out_shape=jax.ShapeDtypeStruct((t, POOL_W), BF16),
        compiler_params=_cparams("parallel"),
        name=f"pool_l{seq_len}",
    )(u, pool_w, pool_scale)


def _pack_rows(x):
    half = D_MODEL // 2
    lo = lax.bitcast_convert_type(x[:, :half].astype(BF16).astype(F32), U32) >> 16
    hi = lax.bitcast_convert_type(x[:, half:].astype(BF16).astype(F32), U32) & jnp.uint32(0xFFFF0000)
    word = lax.bitcast_convert_type(hi | lo, I32)
    return [word[:, c * LANES:(c + 1) * LANES] for c in range(N_PIECES)]


def _unpack_rows(pieces):
    words = [lax.bitcast_convert_type(p, U32) for p in pieces]
    lo = [lax.bitcast_convert_type(w << 16, F32) for w in words]
    hi = [lax.bitcast_convert_type(w & jnp.uint32(0xFFFF0000), F32) for w in words]
    return lo, hi


def _merge_kernel(x_ref, z_ref, p_ref, ga_ref, gb_ref, mod_ref, g2_ref, wr_ref, wp_ref, wo_ref,
                  x1_ref, h2_ref, *piece_refs):
    y_ret = _dot(z_ref[...], wr_ref[...])
    y_pool = _dot(p_ref[...], wp_ref[...])
    merged = (jax.nn.sigmoid(ga_ref[...].astype(F32)) * y_ret
              + jax.nn.sigmoid(gb_ref[...].astype(F32)) * y_pool)
    x1 = x_ref[...] + mod_ref[0, 2:3, :] * _dot(merged.astype(BF16), wo_ref[...])
    x1_ref[...] = x1
    h2 = _rms_mod(x1, g2_ref[...], mod_ref[0, 4:5, :], mod_ref[0, 3:4, :])
    h2_ref[...] = h2.astype(BF16)
    for ref, piece in zip(piece_refs, _pack_rows(h2)):
        ref[...] = piece


def _merge(x, z, p, ga, gb, mods, norm2_g, w_br_ret, w_br_pool, w_out, *, tokens_per_mod, tm):
    t = x.shape[0]
    tiles_per_mod = tokens_per_mod // tm
    row = lambda w: pl.BlockSpec((tm, w), lambda i: (i, 0))
    full = lambda a: pl.BlockSpec(a.shape, lambda i: (0,) * a.ndim)
    outs = pl.pallas_call(
        _merge_kernel,
        grid=(t // tm,),
        in_specs=[row(D_MODEL), row(RET_V_W), row(POOL_W), row(D_MODEL), row(D_MODEL),
                  pl.BlockSpec((1, 6, D_MODEL), lambda i: (i // tiles_per_mod, 0, 0)),
                  full(norm2_g), full(w_br_ret), full(w_br_pool), full(w_out)],
        out_specs=[row(D_MODEL), row(D_MODEL)] + [row(LANES)] * N_PIECES,
        out_shape=[jax.ShapeDtypeStruct((t, D_MODEL), F32), jax.ShapeDtypeStruct((t, D_MODEL), BF16)]
        + [jax.ShapeDtypeStruct((t, LANES), I32)] * N_PIECES,
        compiler_params=_cparams("parallel"),
        name="merge",
    )(x, z, p, ga, gb, mods, norm2_g, w_br_ret, w_br_pool, w_out)
    return outs[0], outs[1], outs[2:]


def _route_kernel(h_ref, rw_ref, bias_ref, idx_ref, rank_ref, wt_ref, cnt_ref, carry_ref):
    e = N_EXPERTS
    tm = h_ref.shape[0]
    neg = -jnp.inf

    @pl.when(pl.program_id(0) == 0)
    def _():
        carry_ref[...] = jnp.zeros(carry_ref.shape, F32)

    logits = lax.dot_general(rw_ref[...], h_ref[...], (((1,), (1,)), ((), ())),
                             preferred_element_type=F32)[:e]
    scores = jax.nn.sigmoid(logits)
    sel = scores + bias_ref[:e, 0:1]
    e_idx = lax.broadcasted_iota(I32, (e, tm), 0)

    grp = sel.reshape(N_EXPERT_GROUPS, GROUP_SIZE, tm)
    m_idx = lax.broadcasted_iota(I32, grp.shape, 1)
    m1 = jnp.max(grp, axis=1, keepdims=True)
    first = jnp.min(jnp.where(grp == m1, m_idx, GROUP_SIZE), axis=1, keepdims=True)
    m2 = jnp.max(jnp.where(m_idx == first, neg, grp), axis=1, keepdims=True)
    gscore = (m1 + m2).reshape(N_EXPERT_GROUPS, tm)

    g_idx = lax.broadcasted_iota(I32, gscore.shape, 0)
    grank = jnp.zeros(gscore.shape, I32)
    for g in range(N_EXPERT_GROUPS):
        other = gscore[g:g + 1, :]
        beats = jnp.where(other > gscore, 1, jnp.where(other == gscore, (g_idx > g).astype(I32), 0))
        grank = grank + beats
    gkeep = (grank < TOPK_GROUPS).astype(F32)
    ekeep = jnp.broadcast_to(gkeep.reshape(N_EXPERT_GROUPS, 1, tm), grp.shape).reshape(e, tm)
    masked = jnp.where(ekeep > 0, sel, neg)

    chosen = jnp.zeros((e, tm), F32)
    picks, hits = [], []
    for _ in range(TOP_K):
        m = jnp.max(masked, axis=0, keepdims=True)
        pick = jnp.min(jnp.where(masked == m, e_idx, e), axis=0, keepdims=True)
        hit = e_idx == pick
        chosen = jnp.where(hit, 1.0, chosen)
        masked = jnp.where(hit, neg, masked)
        picks.append(pick)
        hits.append(hit)

    w = scores * chosen
    comb = w / jnp.sum(w, axis=0, keepdims=True) * ROUTED_SCALE

    t_row = lax.broadcasted_iota(I32, (tm, tm), 0)
    t_col = lax.broadcasted_iota(I32, (tm, tm), 1)
    before = (t_row < t_col).astype(BF16)
    rankmat = _dot(chosen.astype(BF16), before) + carry_ref[:e, 0:1]
    carry_ref[:e, :] = carry_ref[:e, :] + jnp.sum(chosen, axis=1, keepdims=True)
    cnt_ref[...] = carry_ref[...]

    idx_ref[...] = jnp.concatenate(picks, axis=0)
    rank_ref[...] = jnp.concatenate(
        [jnp.sum(jnp.where(h, rankmat, 0.0), axis=0, keepdims=True) for h in hits], axis=0).astype(I32)
    w_rows = [jnp.sum(jnp.where(h, comb, 0.0), axis=0, keepdims=True) for h in hits]
    wt_ref[...] = jnp.concatenate(w_rows + [jnp.zeros((LANES - TOP_K, tm), F32)], axis=0).T


def _route(h2, router_wt, bias_col, *, tm):
    t = h2.shape[0]
    krow = pl.BlockSpec((TOP_K, tm), lambda i: (0, i))
    return pl.pallas_call(
        _route_kernel,
        grid=(t // tm,),
        in_specs=[pl.BlockSpec((tm, D_MODEL), lambda i: (i, 0)),
                  pl.BlockSpec((LANES, D_MODEL), lambda i: (0, 0)),
                  pl.BlockSpec((LANES, 1), lambda i: (0, 0))],
        out_specs=[krow, krow, pl.BlockSpec((tm, LANES), lambda i: (i, 0)),
                   pl.BlockSpec((LANES, LANES), lambda i: (0, 0))],
        out_shape=[jax.ShapeDtypeStruct((TOP_K, t), I32), jax.ShapeDtypeStruct((TOP_K, t), I32),
                   jax.ShapeDtypeStruct((t, LANES), F32), jax.ShapeDtypeStruct((LANES, LANES), F32)],
        scratch_shapes=[pltpu.VMEM((LANES, LANES), F32)],
        compiler_params=_cparams("arbitrary"),
        name="route",
    )(h2, router_wt, bias_col)


def _plan_kernel(idx_ref, rank_ref, cnt_ref, pos_ref, te_ref, nu_ref, *, group_tile, min_tiles):
    tf = idx_ref.shape[1]
    nt = te_ref.shape[1]
    cnt = cnt_ref[...].astype(I32)
    tiles = jnp.maximum((cnt + (group_tile - 1)) // group_tile, min_tiles)
    e_sub = lax.broadcasted_iota(I32, (LANES, LANES), 0)
    padded = jnp.where(e_sub < N_EXPERTS, tiles * group_tile, 0).astype(F32)
    e_lane = lax.broadcasted_iota(I32, (LANES, LANES), 1)
    base = jnp.sum(jnp.where(e_lane < e_sub, padded.T, 0.0), axis=1, keepdims=True)
    end = base + padded[:, 0:1]

    idx = idx_ref[...]
    start = jnp.zeros(idx.shape, F32)
    for e in range(N_EXPERTS):
        start = jnp.where(idx == e, base[e:e + 1, 0:1], start)
    pos = start.astype(I32) + rank_ref[...]
    for j in range(tf // SC_CHUNK):
        pos_ref[j] = pos[:, j * SC_CHUNK:(j + 1) * SC_CHUNK]

    tile_start = (lax.broadcasted_iota(I32, (N_EXPERTS, nt), 1) * group_tile).astype(F32)
    done = jnp.sum(jnp.where(end[:N_EXPERTS] <= tile_start, 1.0, 0.0), axis=0, keepdims=True)
    te_ref[...] = jnp.minimum(done, N_EXPERTS - 1.0).astype(I32)
    total = jnp.sum(padded[:, 0:1], axis=0, keepdims=True)
    nu_ref[...] = jnp.broadcast_to(total * (1.0 / group_tile), nu_ref.shape).astype(I32)


def _plan(idx, rank, counts, *, n_tiles, tf, group_tile, min_tiles):
    t = idx.shape[1]
    nt_pad = -(-n_tiles // LANES) * LANES
    krow = pl.BlockSpec((TOP_K, tf), lambda i: (0, i))
    return pl.pallas_call(
        functools.partial(_plan_kernel, group_tile=group_tile, min_tiles=min_tiles),
        grid=(t // tf,),
        in_specs=[krow, krow, pl.BlockSpec((LANES, LANES), lambda i: (0, 0))],
        out_specs=[pl.BlockSpec((tf // SC_CHUNK, TOP_K, SC_CHUNK), lambda i: (i, 0, 0)),
                   pl.BlockSpec((1, nt_pad), lambda i: (0, 0)),
                   pl.BlockSpec((1, LANES), lambda i: (0, 0))],
        out_shape=[jax.ShapeDtypeStruct((t // SC_CHUNK, TOP_K, SC_CHUNK), I32),
                   jax.ShapeDtypeStruct((1, nt_pad), I32), jax.ShapeDtypeStruct((1, LANES), I32)],
        compiler_params=_cparams("arbitrary"),
        name="moe_plan",
    )(idx, rank, counts)


def _sc_mesh_info():
    info = plsc.get_sparse_core_info()
    mesh = plsc.VectorSubcoreMesh(core_axis_name="c", subcore_axis_name="s")
    return mesh, info.num_cores, info.num_cores * info.num_subcores


def _sc_dispatch(pieces, pos, *, n_rows):
    t = pieces[0].shape[0]
    mesh, n_cores, n_workers = _sc_mesh_info()
    per_w = t // SC_CHUNK // n_workers

    @functools.partial(
        pl.kernel, mesh=mesh,
        out_type=[jax.ShapeDtypeStruct((n_rows, LANES), I32)] * N_PIECES,
        scratch_types=[pltpu.VMEM((TOP_K, SC_CHUNK), I32),
                       pltpu.VMEM((N_PIECES, SC_CHUNK, LANES), I32),
                       pltpu.SemaphoreType.DMA((N_PIECES,)),
                       pltpu.SemaphoreType.DMA],
        name="sc_dispatch",
    )
    def run(*refs):
        src = refs[:N_PIECES]
        pos_hbm = refs[N_PIECES]
        dst = refs[N_PIECES + 1:2 * N_PIECES + 1]
        idx_v, rows_v, load_sem, put_sem = refs[2 * N_PIECES + 1:]
        wid = lax.axis_index("s") * n_cores + lax.axis_index("c")

        @pl.loop(0, per_w)
        def _(j):
            ch = wid * per_w + j
            t0 = pl.multiple_of(ch * SC_CHUNK, SC_CHUNK)
            loads = [pltpu.make_async_copy(src[c].at[pl.ds(t0, SC_CHUNK)], rows_v.at[c], load_sem.at[c])
                     for c in range(N_PIECES)]
            for ld in loads:
                ld.start()
            pltpu.sync_copy(pos_hbm.at[ch], idx_v)
            puts = []
            for c in range(N_PIECES):
                loads[c].wait()
                for k in range(TOP_K):
                    puts.append(pltpu.make_async_copy(rows_v.at[c], dst[c].at[idx_v.at[k]], put_sem))
                    puts[-1].start()
            for cp in puts:
                cp.wait()

    return run(*pieces, pos)


def _sc_collect(pieces, pos, *, n_tokens):
    mesh, n_cores, n_workers = _sc_mesh_info()
    per_w = n_tokens // SC_CHUNK // n_workers
    n_buf = 4
    lag = n_buf // 2
    units = [(k, c) for k in range(TOP_K) for c in range(N_PIECES)]

    @functools.partial(
        pl.kernel, mesh=mesh,
        out_type=[jax.ShapeDtypeStruct((TOP_K, n_tokens, LANES), I32)] * N_PIECES,
        scratch_types=[pltpu.VMEM((TOP_K, SC_CHUNK), I32),
                       pltpu.VMEM((n_buf, SC_CHUNK, LANES), I32),
                       pltpu.SemaphoreType.DMA((n_buf,)),
                       pltpu.SemaphoreType.DMA((n_buf,))],
        name="sc_collect",
    )
    def run(*refs):
        src = refs[:N_PIECES]
        pos_hbm = refs[N_PIECES]
        dst = refs[N_PIECES + 1:2 * N_PIECES + 1]
        idx_v, buf, get_sem, put_sem = refs[2 * N_PIECES + 1:]
        wid = lax.axis_index("s") * n_cores + lax.axis_index("c")

        @pl.loop(0, per_w)
        def _(j):
            ch = wid * per_w + j
            t0 = pl.multiple_of(ch * SC_CHUNK, SC_CHUNK)
            pltpu.sync_copy(pos_hbm.at[ch], idx_v)

            def get(u):
                k, c = units[u]
                return pltpu.make_async_copy(src[c].at[idx_v.at[k]], buf.at[u % n_buf], get_sem.at[u % n_buf])

            def put(u):
                k, c = units[u]
                return pltpu.make_async_copy(buf.at[u % n_buf], dst[c].at[k, pl.ds(t0, SC_CHUNK)],
                                             put_sem.at[u % n_buf])

            n = len(units)
            for u in range(n + lag):
                if u < n:
                    if u >= n_buf:
                        put(u - n_buf).wait()
                    get(u).start()
                if 0 <= u - lag < n:
                    get(u - lag).wait()
                    put(u - lag).start()
            for u in range(n - n_buf, n):
                put(u).wait()

    return run(*pieces, pos)


def _group_tile(n_tokens):
    per_expert = n_tokens * TOP_K // N_EXPERTS
    return max(MXU_DIM, min(4 * MXU_DIM, per_expert // MXU_DIM * MXU_DIM))


SECOND_DMA_QUEUE = 1


def _experts_kernel(te_ref, nu_ref, *refs, cast_weights, tile):
    x_hbm = refs[:N_PIECES]
    w_in = refs[N_PIECES:N_PIECES + 3]
    y_refs = refs[N_PIECES + 3:2 * N_PIECES + 3]
    xbuf, xsem = refs[-2:]
    i = pl.program_id(0)
    n_used = nu_ref[0]
    slot = lax.rem(i, 2)

    def x_copies(step, into):
        r = pl.ds(pl.multiple_of(step * tile, tile), tile)
        return [pltpu.make_async_copy(x_hbm[c].at[r], xbuf.at[into, c], xsem.at[into, c])
                for c in range(N_PIECES)]

    @pl.when(i == 0)
    def _():
        for cp in x_copies(0, 0):
            cp.start(priority=SECOND_DMA_QUEUE)

    @pl.when(i + 1 < n_used)
    def _():
        for cp in x_copies(i + 1, 1 - slot):
            cp.start(priority=SECOND_DMA_QUEUE)

    if cast_weights:
        w_bf = refs[2 * N_PIECES + 3:-2]
        last = n_used - 1
        cur = te_ref[jnp.minimum(i, last)]
        prev = te_ref[jnp.minimum(jnp.maximum(i - 1, 0), last)]

        @pl.when((i == 0) | (cur != prev))
        def _():
            for dst, src in zip(w_bf, w_in):
                dst[...] = src[...].astype(BF16)
    else:
        w_bf = w_in
    wg_ref, wu_ref, wd_ref = w_bf

    @pl.when(i < n_used)
    def _():
        for cp in x_copies(i, slot):
            cp.wait()
        for s in range(tile // MXU_DIM):
            rows = slice(s * MXU_DIM, (s + 1) * MXU_DIM)
            lo, hi = _unpack_rows([xbuf[slot, c, rows, :] for c in range(N_PIECES)])
            x = jnp.concatenate(lo + hi, axis=1).astype(BF16)
            hid = _silu(_dot(x, wg_ref[...])) * _dot(x, wu_ref[...])
            y = _dot(hid.astype(BF16), wd_ref[...])
            for ref, piece in zip(y_refs, _pack_rows(y)):
                ref[rows, :] = piece


def _experts(x_pieces, tile_expert, n_used, wg, wu, wd, *, group_tile):
    n_rows = x_pieces[0].shape[0]
    n_tiles = n_rows // group_tile
    cast_weights = wg.dtype != BF16

    def tile(i, te, nu):
        return jnp.minimum(i, nu[0] - 1)

    row = pl.BlockSpec((group_tile, LANES), lambda i, te, nu: (tile(i, te, nu), 0))
    wspec = lambda a: pl.BlockSpec((None,) + a.shape[1:], lambda i, te, nu: (te[tile(i, te, nu)], 0, 0))
    w_specs = [wspec(wg), wspec(wu), wspec(wd)]
    y_shape = [jax.ShapeDtypeStruct((n_rows, LANES), I32)] * N_PIECES
    w_shape = [jax.ShapeDtypeStruct(a.shape, BF16) for a in (wg, wu, wd)]
    outs = pl.pallas_call(
        functools.partial(_experts_kernel, cast_weights=cast_weights, tile=group_tile),
        grid_spec=pltpu.PrefetchScalarGridSpec(
            num_scalar_prefetch=2,
            grid=(n_tiles,),
            in_specs=[pl.BlockSpec(memory_space=pl.ANY)] * N_PIECES + w_specs,
            out_specs=[row] * N_PIECES + (w_specs if cast_weights else []),
            scratch_shapes=[pltpu.VMEM((2, N_PIECES, group_tile, LANES), I32),
                            pltpu.SemaphoreType.DMA((2, N_PIECES))]),
        out_shape=y_shape + (w_shape if cast_weights else []),
        compiler_params=_cparams("arbitrary"),
        name="experts_cast" if cast_weights else "experts",
    )(tile_expert, n_used, *x_pieces, wg, wu, wd)
    return outs[:N_PIECES], (tuple(outs[N_PIECES:]) if cast_weights else (wg, wu, wd))


def _moe_out_kernel(h_ref, wt_ref, x1_ref, mod_ref, fg_ref, sg_ref, su_ref, sd_ref, *refs, tm):
    y_hbm = refs[:N_PIECES]
    out_ref = refs[N_PIECES]
    ybuf, ysem = refs[N_PIECES + 1:]
    i = pl.program_id(0)
    slot = lax.rem(i, 2)

    def y_copies(step, into):
        r = pl.ds(pl.multiple_of(step * tm, tm), tm)
        return [pltpu.make_async_copy(y_hbm[c].at[:, r, :], ybuf.at[into, c], ysem.at[into, c])
                for c in range(N_PIECES)]

    @pl.when(i == 0)
    def _():
        for cp in y_copies(0, 0):
            cp.start(priority=SECOND_DMA_QUEUE)

    @pl.when(i + 1 < pl.num_programs(0))
    def _():
        for cp in y_copies(i + 1, 1 - slot):
            cp.start(priority=SECOND_DMA_QUEUE)

    h = h_ref[...]
    hid = _silu(_dot(h, sg_ref[...])) * _dot(h, su_ref[...])
    shared = _dot(hid.astype(BF16), sd_ref[...])
    wt = wt_ref[...]
    for cp in y_copies(i, slot):
        cp.wait()
    lo_acc = [None] * N_PIECES
    hi_acc = [None] * N_PIECES
    for k in range(TOP_K):
        wk = wt[:, k:k + 1]
        lo, hi = _unpack_rows([ybuf[slot, c, k] for c in range(N_PIECES)])
        for c in range(N_PIECES):
            lo_acc[c] = wk * lo[c] if k == 0 else lo_acc[c] + wk * lo[c]
            hi_acc[c] = wk * hi[c] if k == 0 else hi_acc[c] + wk * hi[c]
    routed = jnp.concatenate(lo_acc + hi_acc, axis=1)
    x2 = x1_ref[...] + mod_ref[0, 5:6, :] * (shared + routed)
    out_ref[...] = x2 * lax.rsqrt(jnp.mean(x2 * x2, axis=-1, keepdims=True) + EPS) * fg_ref[...]


def _moe_out(h2, wt, x1, mods, final_g, sg, su, sd, y_pieces, *, tokens_per_mod, tm):
    t = h2.shape[0]
    tiles_per_mod = tokens_per_mod // tm
    row = lambda w: pl.BlockSpec((tm, w), lambda i: (i, 0))
    full = lambda a: pl.BlockSpec(a.shape, lambda i: (0,) * a.ndim)
    return pl.pallas_call(
        functools.partial(_moe_out_kernel, tm=tm),
        grid=(t // tm,),
        in_specs=[row(D_MODEL), row(LANES), row(D_MODEL),
                  pl.BlockSpec((1, 6, D_MODEL), lambda i: (i // tiles_per_mod, 0, 0)),
                  full(final_g), full(sg), full(su), full(sd)]
        + [pl.BlockSpec(memory_space=pl.ANY)] * N_PIECES,
        out_specs=row(D_MODEL),
        out_shape=jax.ShapeDtypeStruct((t, D_MODEL), F32),
        scratch_shapes=[pltpu.VMEM((2, N_PIECES, TOP_K, tm, LANES), I32),
                        pltpu.SemaphoreType.DMA((2, N_PIECES))],
        compiler_params=_cparams("arbitrary"),
        name="moe_out",
    )(h2, wt, x1, mods, final_g, sg, su, sd, *y_pieces)


def _trunk(x, mods, s0f, s0b, w, expert_w, *, batch, seq_len, on_grid):
    t = batch * seq_len
    tokens_per_mod = t // mods.shape[0]
    cos_t, sin_t = _rope_tables(seq_len)
    q, k, v, gsw, up, ga, gb = _inproj(x, mods, w["norm1_g"], w["w_in"], cos_t, sin_t,
                                       tokens_per_mod=tokens_per_mod, seq_len=seq_len,
                                       on_grid=on_grid, tm=256)
    z, s_f, s_b = _retention(q, k, v, gsw, w["dec"], s0f, s0b, batch=batch, seq_len=seq_len)
    p = _pool(up, w["pool_w"], w["pool_scale"], batch=batch, seq_len=seq_len, on_grid=on_grid)
    x1, h2, h2_pieces = _merge(x, z, p, ga, gb, mods, w["norm2_g"], w["w_br_ret"], w["w_br_pool"],
                               w["w_out"], tokens_per_mod=tokens_per_mod, tm=512)

    group_tile = _group_tile(t)
    n_rows = t * TOP_K + N_EXPERTS * group_tile
    idx, rank, wt, counts = _route(h2, w["router_wt"], w["router_bias"], tm=1024)
    min_tiles = int(expert_w[0].dtype != BF16)
    pos, tile_expert, n_used = _plan(idx, rank, counts, n_tiles=n_rows // group_tile, tf=2048,
                                     group_tile=group_tile, min_tiles=min_tiles)
    x_sorted = _sc_dispatch(h2_pieces, pos, n_rows=n_rows)
    y_sorted, expert_w = _experts(x_sorted, tile_expert.reshape(-1), n_used.reshape(-1), *expert_w,
                                  group_tile=group_tile)
    y_tok = _sc_collect(y_sorted, pos, n_tokens=t)
    y = _moe_out(h2, wt, x1, mods, w["final_g"], w["sh_w_gate"], w["sh_w_up"], w["sh_w_down"], y_tok,
                 tokens_per_mod=tokens_per_mod, tm=512)
    return y, s_f, s_b, expert_w


def kernel(x_prompt, x_sample, state_ret_fwd, state_ret_bwd, c, c_ctx, ada_w, ada_b, norm1_g, norm2_g, w_in,
           ret_decay_fwd, ret_decay_bwd, w_br_ret, pool_w, pool_scale, w_br_pool, w_out, router_w, router_bias,
           exp_w_gate, exp_w_up, exp_w_down, sh_w_gate, sh_w_up, sh_w_down, final_norm_g):
    n_req, seq, d = x_prompt.shape
    n_dec, dec_seq, _ = x_sample.shape
    depth = ada_w.shape[0]
    assert depth == 1 and d == D_MODEL

    xc = x_prompt.reshape(n_req * seq, d)
    xs = x_sample.reshape(n_dec * dec_seq, d)
    zero_state = jnp.zeros((n_req, RET_HEADS, RET_DK, RET_DV), F32)
    new_f, new_b = [], []
    for l in range(depth):
        c_rows = jnp.concatenate([c_ctx[None, :], c, jnp.zeros((8 - 1 - n_dec, d), F32)], axis=0)
        mods = _ada(c_rows, ada_w[l], ada_b[l]).reshape(8, 6, d)
        pad_rows = LANES - N_EXPERTS
        w = dict(
            norm1_g=norm1_g[l].reshape(1, d), norm2_g=norm2_g[l].reshape(1, d),
            final_g=final_norm_g.reshape(1, d),
            w_in=w_in[l].astype(BF16),
            dec=jnp.stack([ret_decay_fwd[l], ret_decay_bwd[l]]).astype(F32),
            w_br_ret=w_br_ret[l].astype(BF16), pool_w=pool_w[l].astype(BF16),
            pool_scale=pool_scale[l].reshape(1, POOL_W), w_br_pool=w_br_pool[l].astype(BF16),
            w_out=w_out[l].astype(BF16),
            router_wt=jnp.pad(router_w[l].T, ((0, pad_rows), (0, 0))).astype(BF16),
            router_bias=jnp.pad(router_bias[l].astype(F32).reshape(N_EXPERTS, 1), ((0, pad_rows), (0, 0))),
            sh_w_gate=sh_w_gate[l].astype(BF16),
            sh_w_up=sh_w_up[l].astype(BF16), sh_w_down=sh_w_down[l].astype(BF16),
        )
        expert_w = (exp_w_gate[l], exp_w_up[l], exp_w_down[l])
        xs, _, _, expert_w = _trunk(xs, mods[1:1 + n_dec], state_ret_fwd[:, l].astype(F32),
                                    state_ret_bwd[:, l].astype(F32), w, expert_w,
                                    batch=n_dec, seq_len=dec_seq, on_grid=True)
        xc, s_f, s_b, _ = _trunk(xc, mods[0:1], zero_state, zero_state, w, expert_w,
                                 batch=n_req, seq_len=seq, on_grid=False)
        new_f.append(s_f)
        new_b.append(s_b)
    y_prompt = xc.reshape(n_req, seq, d)
    y_sample = xs.reshape(n_dec, dec_seq, d)
    return (y_prompt, y_sample, jnp.stack(new_f, axis=1).astype(x_prompt.dtype),
            jnp.stack(new_b, axis=1).astype(x_prompt.dtype))
```

```python
import functools

import numpy as np
import jax
import jax.numpy as jnp
from jax import lax
from jax.experimental import pallas as pl
from jax.experimental.pallas import tpu as pltpu
from jax.experimental.pallas import tpu_sc as plsc

D_MODEL = 1024
GRID_W = 64
RET_HEADS = 4
RET_DK = 128
RET_DV = 256
RET_QK_W = RET_HEADS * RET_DK
RET_V_W = RET_HEADS * RET_DV
RET_CHUNK = 128
ROPE_BASE = 10000.0
POOL_GROUPS = 4
POOL_CH = 128
POOL_W = POOL_GROUPS * POOL_CH
POOL_WINDOWS = (2, 4, 8, 16)
N_EXPERTS = 64
TOP_K = 8
N_EXPERT_GROUPS = 8
GROUP_SIZE = N_EXPERTS // N_EXPERT_GROUPS
TOPK_GROUPS = 4
D_EXPERT = 256
ROUTED_SCALE = 2.5
EPS = 1e-6
IN_SIZES = (RET_QK_W, RET_QK_W, RET_V_W, RET_V_W, POOL_W, D_MODEL, D_MODEL)
IN_OFFS = tuple(sum(IN_SIZES[:i]) for i in range(len(IN_SIZES) + 1))
IN_W = IN_OFFS[-1]

LANES = 128
VMEM_LIMIT = 56 << 20
N_PIECES = D_MODEL // 2 // LANES
MXU_DIM = 256
SC_CHUNK = 128

F32 = jnp.float32
BF16 = jnp.bfloat16
I32 = jnp.int32
U32 = jnp.uint32


def _cparams(*sem):
    return pltpu.CompilerParams(dimension_semantics=sem, vmem_limit_bytes=VMEM_LIMIT)


def _dot(a, b):
    return jnp.dot(a, b, preferred_element_type=F32)


def _silu(x):
    return x * jax.nn.sigmoid(x)


def _rms_mod(x, g, scale, shift):
    y = x * lax.rsqrt(jnp.mean(x * x, axis=-1, keepdims=True) + EPS)
    return (y * g) * (1.0 + scale) + shift


def _ada_kernel(c_ref, w_ref, b_ref, o_ref):
    c = c_ref[...]
    o_ref[...] = jnp.dot(_silu(c), w_ref[...], preferred_element_type=F32,
                         precision=lax.Precision.HIGHEST) + b_ref[...]


def _ada(c_rows, ada_w, ada_b):
    r = c_rows.shape[0]
    n = ada_w.shape[1]
    tn = D_MODEL
    return pl.pallas_call(
        _ada_kernel,
        grid=(n // tn,),
        in_specs=[pl.BlockSpec((r, D_MODEL), lambda j: (0, 0)),
                  pl.BlockSpec((D_MODEL, tn), lambda j: (0, j)),
                  pl.BlockSpec((1, tn), lambda j: (0, j))],
        out_specs=pl.BlockSpec((r, tn), lambda j: (0, j)),
        out_shape=jax.ShapeDtypeStruct((r, n), F32),
        compiler_params=_cparams("parallel"),
        name="ada_mod",
    )(c_rows, ada_w, ada_b.reshape(1, n))


def _inproj_kernel(x_ref, mod_ref, g_ref, w_ref, cos_ref, sin_ref,
                   q_ref, k_ref, v_ref, gsw_ref, up_ref, ga_ref, gb_ref, *, on_grid):
    x = x_ref[...]
    h = _rms_mod(x, g_ref[...], mod_ref[0, 1:2, :], mod_ref[0, 0:1, :]).astype(BF16)

    def seg(i):
        return _dot(h, w_ref[:, IN_OFFS[i]:IN_OFFS[i + 1]])

    q = seg(0)
    k = seg(1)
    if on_grid:
        cos = jnp.concatenate([cos_ref[...]] * RET_HEADS, axis=1)
        sin = jnp.concatenate([sin_ref[...]] * RET_HEADS, axis=1)
        lane = lax.broadcasted_iota(jnp.int32, q.shape, 1)
        first = (lane & 63) < 32

        def rope(a):
            up = pltpu.roll(a, RET_QK_W - 32, axis=1)
            dn = pltpu.roll(a, 32, axis=1)
            return a * cos + jnp.where(first, up, dn) * sin

        q = rope(q)
        k = rope(k)
    q_ref[...] = q.astype(BF16)
    k_ref[...] = (k * (RET_DK ** -0.5)).astype(BF16)
    v_ref[...] = seg(2).astype(BF16)
    gsw_ref[...] = seg(3).astype(BF16)
    up_ref[...] = seg(4).astype(BF16)
    ga_ref[...] = seg(5).astype(BF16)
    gb_ref[...] = seg(6).astype(BF16)


def _inproj(x, mods, norm_g, w_in, cos_t, sin_t, *, tokens_per_mod, seq_len, on_grid, tm):
    t = x.shape[0]
    tiles_per_mod = tokens_per_mod // tm
    tiles_per_seq = seq_len // tm
    widths = IN_SIZES
    out_shape = [jax.ShapeDtypeStruct((t, w), BF16) for w in widths]
    out_specs = [pl.BlockSpec((tm, w), lambda i: (i, 0)) for w in widths]
    return pl.pallas_call(
        functools.partial(_inproj_kernel, on_grid=on_grid),
        grid=(t // tm,),
        in_specs=[pl.BlockSpec((tm, D_MODEL), lambda i: (i, 0)),
                  pl.BlockSpec((1, 6, D_MODEL), lambda i: (i // tiles_per_mod, 0, 0)),
                  pl.BlockSpec((1, D_MODEL), lambda i: (0, 0)),
                  pl.BlockSpec((D_MODEL, IN_W), lambda i: (0, 0)),
                  pl.BlockSpec((tm, RET_DK), lambda i: (i % tiles_per_seq, 0)),
                  pl.BlockSpec((tm, RET_DK), lambda i: (i % tiles_per_seq, 0))],
        out_specs=out_specs,
        out_shape=out_shape,
        compiler_params=_cparams("parallel"),
        name="inproj_grid" if on_grid else "inproj_seq",
    )(x, mods, norm_g, w_in, cos_t, sin_t)


def _rope_tables(seq_len):
    t = np.arange(seq_len)
    row = (t // GRID_W).astype(np.float32)
    col = (t % GRID_W).astype(np.float32)
    m = RET_DK // 4
    inv = (np.float32(ROPE_BASE) ** (-np.arange(m, dtype=np.float32) / np.float32(m))).astype(np.float32)
    ar = row[:, None] * inv
    ac = col[:, None] * inv
    cos = np.concatenate([np.cos(ar), np.cos(ar), np.cos(ac), np.cos(ac)], axis=1)
    sin = np.concatenate([-np.sin(ar), np.sin(ar), -np.sin(ac), np.sin(ac)], axis=1)
    return jnp.asarray(cos, F32), jnp.asarray(sin, F32)


RET_HEADS_PER_STEP = 2


def _ret_kernel(dec_ref, q_ref, k_ref, v_ref, g_ref, s0f_ref, s0b_ref,
                z_ref, sf_ref, sb_ref, oacc_ref, kt_ref, *, n_chunks):
    c = RET_CHUNK
    heads = RET_HEADS_PER_STEP
    half = n_chunks // 2
    ii = lax.broadcasted_iota(I32, (c, c), 0)
    jj = lax.broadcasted_iota(I32, (c, c), 1)
    ik = lax.broadcasted_iota(I32, (c, RET_DK), 0).astype(F32)
    jk = lax.broadcasted_iota(I32, (RET_DK, c), 1).astype(F32)

    def log_gamma(d, shape):
        return jnp.log1p(-jnp.exp2(-jnp.full(shape, d, F32)))

    consts = {}
    for hh in range(heads):
        h = pl.program_id(1) * heads + hh
        dec_f = dec_ref[0, h]
        dec_b = dec_ref[1, h]
        rel = (ii - jj).astype(F32)
        consts[hh, "f"] = (
            jnp.where(rel >= 0, jnp.exp(log_gamma(dec_f, (c, c)) * jnp.maximum(rel, 0.0)), 0.0),
            jnp.exp(log_gamma(dec_f, (c, RET_DK)) * (ik + 1.0)),
            jnp.exp(log_gamma(dec_f, (RET_DK, c)) * (c - 1.0 - jk)),
            jnp.exp(log_gamma(dec_f, (RET_DK, RET_DV)) * c))
        consts[hh, "b"] = (
            jnp.where(rel <= 0, jnp.exp(log_gamma(dec_b, (c, c)) * jnp.maximum(-rel, 0.0)), 0.0),
            jnp.exp(log_gamma(dec_b, (c, RET_DK)) * (c - ik)),
            jnp.exp(log_gamma(dec_b, (RET_DK, c)) * jk),
            jnp.exp(log_gamma(dec_b, (RET_DK, RET_DV)) * c))

    sf_ref[...] = s0f_ref[...]
    sb_ref[...] = s0b_ref[...]

    def transpose_keys(ci, carry):
        r = pl.ds(pl.multiple_of(ci * c, c), c)
        for hh in range(heads):
            kt_ref[hh, ci] = k_ref[r, hh * RET_DK:(hh + 1) * RET_DK].T
        return carry

    lax.fori_loop(0, n_chunks, transpose_keys, 0)

    def scan_chunk(ci, hh, direction, second):
        dmask, qdec, kdec, cdec = consts[hh, direction]
        s_ref = sf_ref if direction == "f" else sb_ref
        r = pl.ds(pl.multiple_of(ci * c, c), c)
        kcols = slice(hh * RET_DK, (hh + 1) * RET_DK)
        vcols = slice(hh * RET_DV, (hh + 1) * RET_DV)
        qc = q_ref[r, kcols]
        kc = k_ref[r, kcols]
        vc = v_ref[r, vcols]
        s = s_ref[hh]
        sc = lax.dot_general(qc, kc, (((1,), (1,)), ((), ())), preferred_element_type=F32)
        lhs = jnp.concatenate([(sc * dmask).astype(BF16), (qc.astype(F32) * qdec).astype(BF16)], axis=1)
        o = _dot(lhs, jnp.concatenate([vc, s.astype(BF16)], axis=0))
        kd_t = (kt_ref[hh, ci].astype(F32) * kdec).astype(BF16)
        s_ref[hh] = s * cdec + _dot(kd_t, vc)
        if not second:
            oacc_ref[hh, r, :] = o
        else:
            o = o + oacc_ref[hh, r, :]
            o = o * lax.rsqrt(jnp.mean(o * o, axis=-1, keepdims=True) + EPS)
            g = g_ref[r, vcols].astype(F32)
            z_ref[r, vcols] = (_silu(g) * o).astype(BF16)

    def body(second):
        def run(t, carry):
            for hh in range(heads):
                scan_chunk(t, hh, "f", second)
                scan_chunk(n_chunks - 1 - t, hh, "b", second)
            return carry
        return run

    lax.fori_loop(0, half, body(False), 0)
    lax.fori_loop(half, n_chunks, body(True), 0)


def _retention(q, k, v, gsw, dec, s0f, s0b, *, batch, seq_len):
    n_chunks = seq_len // RET_CHUNK
    assert n_chunks % 2 == 0
    heads = RET_HEADS_PER_STEP
    t = batch * seq_len
    st_spec = pl.BlockSpec((None, heads, RET_DK, RET_DV), lambda b, h: (b, h, 0, 0))
    st_shape = jax.ShapeDtypeStruct((batch, RET_HEADS, RET_DK, RET_DV), F32)
    kspec = pl.BlockSpec((seq_len, heads * RET_DK), lambda b, h: (b, h))
    vspec = pl.BlockSpec((seq_len, heads * RET_DV), lambda b, h: (b, h))
    return pl.pallas_call(
        functools.partial(_ret_kernel, n_chunks=n_chunks),
        grid=(batch, RET_HEADS // heads),
        in_specs=[pl.BlockSpec(memory_space=pltpu.SMEM), kspec, kspec, vspec, vspec, st_spec, st_spec],
        out_specs=[vspec, st_spec, st_spec],
        out_shape=[jax.ShapeDtypeStruct((t, RET_V_W), BF16), st_shape, st_shape],
        scratch_shapes=[pltpu.VMEM((heads, seq_len, RET_DV), F32),
                        pltpu.VMEM((heads, n_chunks, RET_DK, RET_CHUNK), BF16)],
        compiler_params=_cparams("parallel", "parallel"),
        name=f"retention_l{seq_len}",
    )(dec, q, k, v, gsw, s0f, s0b)


def _pool_kernel(u_ref, w_ref, sc_ref, o_ref, *, seq_len, on_grid):
    tok = lax.broadcasted_iota(jnp.int32, (seq_len, POOL_CH), 0)

    def shift(a, s, stride, pos, width):
        y = pltpu.roll(a, (-s * stride) % seq_len, axis=0)
        ok = (pos < width - s) if s > 0 else (pos >= -s)
        return jnp.where(ok, y, 0.0)

    def box_mean(a, window, stride, pos, width):
        half = window // 2
        fw = a
        bw = shift(a, -1, stride, pos, width)
        m = 1
        while m < half:
            fw = fw + shift(fw, m, stride, pos, width)
            bw = bw + shift(bw, -m, stride, pos, width)
            m *= 2
        cnt = jnp.minimum(pos + half, width) - jnp.maximum(pos - half, 0)
        return (fw + bw) / cnt.astype(F32)

    for g, window in enumerate(POOL_WINDOWS):
        cols = slice(g * POOL_CH, (g + 1) * POOL_CH)
        ug = u_ref[:, cols].astype(F32)
        if on_grid:
            pooled = box_mean(ug, window, 1, tok & (GRID_W - 1), GRID_W)
            pooled = box_mean(pooled, window, GRID_W, tok >> 6, seq_len // GRID_W)
        else:
            pooled = box_mean(ug, window, 1, tok, seq_len)
        d = (pooled - ug).astype(BF16)
        o_ref[:, cols] = (_dot(d, w_ref[g]) * sc_ref[:, cols]).astype(BF16)


def _pool(u, pool_w, pool_scale, *, batch, seq_len, on_grid):
    t = batch * seq_len
    return pl.pallas_call(
        functools.partial(_pool_kernel, seq_len=seq_len, on_grid=on_grid),
        grid=(batch,),
        in_specs=[pl.BlockSpec((seq_len, POOL_W), lambda b: (b, 0)),
                  pl.BlockSpec((POOL_GROUPS, POOL_CH, POOL_CH), lambda b: (0, 0, 0)),
                  pl.BlockSpec((1, POOL_W), lambda b: (0, 0))],
        out_specs=pl.BlockSpec((seq_len, POOL_W), lambda b: (b, 0)),
        out_shape=jax.ShapeDtypeStruct((t, POOL_W), BF16),
        compiler_params=_cparams("parallel"),
        name=f"pool_l{seq_len}",
    )(u, pool_w, pool_scale)


def _pack_rows(x):
    half = D_MODEL // 2
    lo = lax.bitcast_convert_type(x[:, :half].astype(BF16).astype(F32), U32) >> 16
    hi = lax.bitcast_convert_type(x[:, half:].astype(BF16).astype(F32), U32) & jnp.uint32(0xFFFF0000)
    word = lax.bitcast_convert_type(hi | lo, I32)
    return [word[:, c * LANES:(c + 1) * LANES] for c in range(N_PIECES)]


def _unpack_rows(pieces):
    words = [lax.bitcast_convert_type(p, U32) for p in pieces]
    lo = [lax.bitcast_convert_type(w << 16, F32) for w in words]
    hi = [lax.bitcast_convert_type(w & jnp.uint32(0xFFFF0000), F32) for w in words]
    return lo, hi


def _merge_kernel(x_ref, z_ref, p_ref, ga_ref, gb_ref, mod_ref, g2_ref, wr_ref, wp_ref, wo_ref,
                  x1_ref, h2_ref, *piece_refs):
    y_ret = _dot(z_ref[...], wr_ref[...])
    y_pool = _dot(p_ref[...], wp_ref[...])
    merged = (jax.nn.sigmoid(ga_ref[...].astype(F32)) * y_ret
              + jax.nn.sigmoid(gb_ref[...].astype(F32)) * y_pool)
    x1 = x_ref[...] + mod_ref[0, 2:3, :] * _dot(merged.astype(BF16), wo_ref[...])
    x1_ref[...] = x1
    h2 = _rms_mod(x1, g2_ref[...], mod_ref[0, 4:5, :], mod_ref[0, 3:4, :])
    h2_ref[...] = h2.astype(BF16)
    for ref, piece in zip(piece_refs, _pack_rows(h2)):
        ref[...] = piece


def _merge(x, z, p, ga, gb, mods, norm2_g, w_br_ret, w_br_pool, w_out, *, tokens_per_mod, tm):
    t = x.shape[0]
    tiles_per_mod = tokens_per_mod // tm
    row = lambda w: pl.BlockSpec((tm, w), lambda i: (i, 0))
    full = lambda a: pl.BlockSpec(a.shape, lambda i: (0,) * a.ndim)
    outs = pl.pallas_call(
        _merge_kernel,
        grid=(t // tm,),
        in_specs=[row(D_MODEL), row(RET_V_W), row(POOL_W), row(D_MODEL), row(D_MODEL),
                  pl.BlockSpec((1, 6, D_MODEL), lambda i: (i // tiles_per_mod, 0, 0)),
                  full(norm2_g), full(w_br_ret), full(w_br_pool), full(w_out)],
        out_specs=[row(D_MODEL), row(D_MODEL)] + [row(LANES)] * N_PIECES,
        out_shape=[jax.ShapeDtypeStruct((t, D_MODEL), F32), jax.ShapeDtypeStruct((t, D_MODEL), BF16)]
        + [jax.ShapeDtypeStruct((t, LANES), I32)] * N_PIECES,
        compiler_params=_cparams("parallel"),
        name="merge",
    )(x, z, p, ga, gb, mods, norm2_g, w_br_ret, w_br_pool, w_out)
    return outs[0], outs[1], outs[2:]


def _route_kernel(h_ref, rw_ref, bias_ref, idx_ref, rank_ref, wt_ref, cnt_ref, carry_ref):
    e = N_EXPERTS
    tm = h_ref.shape[0]
    neg = -jnp.inf

    @pl.when(pl.program_id(0) == 0)
    def _():
        carry_ref[...] = jnp.zeros(carry_ref.shape, F32)

    logits = lax.dot_general(rw_ref[...], h_ref[...], (((1,), (1,)), ((), ())),
                             preferred_element_type=F32)[:e]
    scores = jax.nn.sigmoid(logits)
    sel = scores + bias_ref[:e, 0:1]
    e_idx = lax.broadcasted_iota(I32, (e, tm), 0)

    grp = sel.reshape(N_EXPERT_GROUPS, GROUP_SIZE, tm)
    m_idx = lax.broadcasted_iota(I32, grp.shape, 1)
    m1 = jnp.max(grp, axis=1, keepdims=True)
    first = jnp.min(jnp.where(grp == m1, m_idx, GROUP_SIZE), axis=1, keepdims=True)
    m2 = jnp.max(jnp.where(m_idx == first, neg, grp), axis=1, keepdims=True)
    gscore = (m1 + m2).reshape(N_EXPERT_GROUPS, tm)

    g_idx = lax.broadcasted_iota(I32, gscore.shape, 0)
    grank = jnp.zeros(gscore.shape, I32)
    for g in range(N_EXPERT_GROUPS):
        other = gscore[g:g + 1, :]
        beats = jnp.where(other > gscore, 1, jnp.where(other == gscore, (g_idx > g).astype(I32), 0))
        grank = grank + beats
    gkeep = (grank < TOPK_GROUPS).astype(F32)
    ekeep = jnp.broadcast_to(gkeep.reshape(N_EXPERT_GROUPS, 1, tm), grp.shape).reshape(e, tm)
    masked = jnp.where(ekeep > 0, sel, neg)

    chosen = jnp.zeros((e, tm), F32)
    picks, hits = [], []
    for _ in range(TOP_K):
        m = jnp.max(masked, axis=0, keepdims=True)
        pick = jnp.min(jnp.where(masked == m, e_idx, e), axis=0, keepdims=True)
        hit = e_idx == pick
        chosen = jnp.where(hit, 1.0, chosen)
        masked = jnp.where(hit, neg, masked)
        picks.append(pick)
        hits.append(hit)

    w = scores * chosen
    comb = w / jnp.sum(w, axis=0, keepdims=True) * ROUTED_SCALE

    t_row = lax.broadcasted_iota(I32, (tm, tm), 0)
    t_col = lax.broadcasted_iota(I32, (tm, tm), 1)
    before = (t_row < t_col).astype(BF16)
    rankmat = _dot(chosen.astype(BF16), before) + carry_ref[:e, 0:1]
    carry_ref[:e, :] = carry_ref[:e, :] + jnp.sum(chosen, axis=1, keepdims=True)
    cnt_ref[...] = carry_ref[...]

    idx_ref[...] = jnp.concatenate(picks, axis=0)
    rank_ref[...] = jnp.concatenate(
        [jnp.sum(jnp.where(h, rankmat, 0.0), axis=0, keepdims=True) for h in hits], axis=0).astype(I32)
    w_rows = [jnp.sum(jnp.where(h, comb, 0.0), axis=0, keepdims=True) for h in hits]
    wt_ref[...] = jnp.concatenate(w_rows + [jnp.zeros((LANES - TOP_K, tm), F32)], axis=0).T


def _route(h2, router_wt, bias_col, *, tm):
    t = h2.shape[0]
    krow = pl.BlockSpec((TOP_K, tm), lambda i: (0, i))
    return pl.pallas_call(
        _route_kernel,
        grid=(t // tm,),
        in_specs=[pl.BlockSpec((tm, D_MODEL), lambda i: (i, 0)),
                  pl.BlockSpec((LANES, D_MODEL), lambda i: (0, 0)),
                  pl.BlockSpec((LANES, 1), lambda i: (0, 0))],
        out_specs=[krow, krow, pl.BlockSpec((tm, LANES), lambda i: (i, 0)),
                   pl.BlockSpec((LANES, LANES), lambda i: (0, 0))],
        out_shape=[jax.ShapeDtypeStruct((TOP_K, t), I32), jax.ShapeDtypeStruct((TOP_K, t), I32),
                   jax.ShapeDtypeStruct((t, LANES), F32), jax.ShapeDtypeStruct((LANES, LANES), F32)],
        scratch_shapes=[pltpu.VMEM((LANES, LANES), F32)],
        compiler_params=_cparams("arbitrary"),
        name="route",
    )(h2, router_wt, bias_col)


def _plan_kernel(idx_ref, rank_ref, cnt_ref, pos_ref, te_ref, nv_ref, nu_ref, *, group_tile, min_tiles):
    tf = idx_ref.shape[1]
    nt = te_ref.shape[1]
    cnt = cnt_ref[...].astype(I32)
    tiles = jnp.maximum((cnt + (group_tile - 1)) // group_tile, min_tiles)
    e_sub = lax.broadcasted_iota(I32, (LANES, LANES), 0)
    padded = jnp.where(e_sub < N_EXPERTS, tiles * group_tile, 0).astype(F32)
    e_lane = lax.broadcasted_iota(I32, (LANES, LANES), 1)
    base = jnp.sum(jnp.where(e_lane < e_sub, padded.T, 0.0), axis=1, keepdims=True)
    end = base + padded[:, 0:1]

    idx = idx_ref[...]
    start = jnp.zeros(idx.shape, F32)
    for e in range(N_EXPERTS):
        start = jnp.where(idx == e, base[e:e + 1, 0:1], start)
    pos = start.astype(I32) + rank_ref[...]
    for j in range(tf // SC_CHUNK):
        pos_ref[j] = pos[:, j * SC_CHUNK:(j + 1) * SC_CHUNK]

    tile_start = (lax.broadcasted_iota(I32, (N_EXPERTS, nt), 1) * group_tile).astype(F32)
    done = jnp.sum(jnp.where(end[:N_EXPERTS] <= tile_start, 1.0, 0.0), axis=0, keepdims=True)
    te_ref[...] = jnp.minimum(done, N_EXPERTS - 1.0).astype(I32)
    in_group = (base[:N_EXPERTS] <= tile_start) & (tile_start < end[:N_EXPERTS])
    real = jnp.clip(base[:N_EXPERTS] + cnt[:N_EXPERTS, 0:1].astype(F32) - tile_start, 0.0, float(group_tile))
    nv_ref[...] = jnp.sum(jnp.where(in_group, real, 0.0), axis=0, keepdims=True).astype(I32)
    total = jnp.sum(padded[:, 0:1], axis=0, keepdims=True)
    nu_ref[...] = jnp.broadcast_to(total * (1.0 / group_tile), nu_ref.shape).astype(I32)


def _plan(idx, rank, counts, *, n_tiles, tf, group_tile, min_tiles):
    t = idx.shape[1]
    nt_pad = -(-n_tiles // LANES) * LANES
    krow = pl.BlockSpec((TOP_K, tf), lambda i: (0, i))
    return pl.pallas_call(
        functools.partial(_plan_kernel, group_tile=group_tile, min_tiles=min_tiles),
        grid=(t // tf,),
        in_specs=[krow, krow, pl.BlockSpec((LANES, LANES), lambda i: (0, 0))],
        out_specs=[pl.BlockSpec((tf // SC_CHUNK, TOP_K, SC_CHUNK), lambda i: (i, 0, 0)),
                   pl.BlockSpec((1, nt_pad), lambda i: (0, 0)),
                   pl.BlockSpec((1, nt_pad), lambda i: (0, 0)),
                   pl.BlockSpec((1, LANES), lambda i: (0, 0))],
        out_shape=[jax.ShapeDtypeStruct((t // SC_CHUNK, TOP_K, SC_CHUNK), I32),
                   jax.ShapeDtypeStruct((1, nt_pad), I32), jax.ShapeDtypeStruct((1, nt_pad), I32),
                   jax.ShapeDtypeStruct((1, LANES), I32)],
        compiler_params=_cparams("arbitrary"),
        name="moe_plan",
    )(idx, rank, counts)


def _sc_mesh_info():
    info = plsc.get_sparse_core_info()
    mesh = plsc.VectorSubcoreMesh(core_axis_name="c", subcore_axis_name="s")
    return mesh, info.num_cores, info.num_cores * info.num_subcores


def _sc_dispatch(pieces, pos, *, n_rows):
    t = pieces[0].shape[0]
    mesh, n_cores, n_workers = _sc_mesh_info()
    per_w = t // SC_CHUNK // n_workers

    @functools.partial(
        pl.kernel, mesh=mesh,
        out_type=[jax.ShapeDtypeStruct((n_rows, LANES), I32)] * N_PIECES,
        scratch_types=[pltpu.VMEM((TOP_K, SC_CHUNK), I32),
                       pltpu.VMEM((N_PIECES, SC_CHUNK, LANES), I32),
                       pltpu.SemaphoreType.DMA((N_PIECES,)),
                       pltpu.SemaphoreType.DMA],
        name="sc_dispatch",
    )
    def run(*refs):
        src = refs[:N_PIECES]
        pos_hbm = refs[N_PIECES]
        dst = refs[N_PIECES + 1:2 * N_PIECES + 1]
        idx_v, rows_v, load_sem, put_sem = refs[2 * N_PIECES + 1:]
        wid = lax.axis_index("s") * n_cores + lax.axis_index("c")

        @pl.loop(0, per_w)
        def _(j):
            ch = wid * per_w + j
            t0 = pl.multiple_of(ch * SC_CHUNK, SC_CHUNK)
            loads = [pltpu.make_async_copy(src[c].at[pl.ds(t0, SC_CHUNK)], rows_v.at[c], load_sem.at[c])
                     for c in range(N_PIECES)]
            for ld in loads:
                ld.start()
            pltpu.sync_copy(pos_hbm.at[ch], idx_v)
            puts = []
            for c in range(N_PIECES):
                loads[c].wait()
                for k in range(TOP_K):
                    puts.append(pltpu.make_async_copy(rows_v.at[c], dst[c].at[idx_v.at[k]], put_sem))
                    puts[-1].start()
            for cp in puts:
                cp.wait()

    return run(*pieces, pos)


def _sc_collect(pieces, pos, *, n_tokens):
    mesh, n_cores, n_workers = _sc_mesh_info()
    per_w = n_tokens // SC_CHUNK // n_workers
    n_buf = 4
    lag = n_buf // 2
    units = [(k, c) for k in range(TOP_K) for c in range(N_PIECES)]

    @functools.partial(
        pl.kernel, mesh=mesh,
        out_type=[jax.ShapeDtypeStruct((TOP_K, n_tokens, LANES), I32)] * N_PIECES,
        scratch_types=[pltpu.VMEM((TOP_K, SC_CHUNK), I32),
                       pltpu.VMEM((n_buf, SC_CHUNK, LANES), I32),
                       pltpu.SemaphoreType.DMA((n_buf,)),
                       pltpu.SemaphoreType.DMA((n_buf,))],
        name="sc_collect",
    )
    def run(*refs):
        src = refs[:N_PIECES]
        pos_hbm = refs[N_PIECES]
        dst = refs[N_PIECES + 1:2 * N_PIECES + 1]
        idx_v, buf, get_sem, put_sem = refs[2 * N_PIECES + 1:]
        wid = lax.axis_index("s") * n_cores + lax.axis_index("c")

        @pl.loop(0, per_w)
        def _(j):
            ch = wid * per_w + j
            t0 = pl.multiple_of(ch * SC_CHUNK, SC_CHUNK)
            pltpu.sync_copy(pos_hbm.at[ch], idx_v)

            def get(u):
                k, c = units[u]
                return pltpu.make_async_copy(src[c].at[idx_v.at[k]], buf.at[u % n_buf], get_sem.at[u % n_buf])

            def put(u):
                k, c = units[u]
                return pltpu.make_async_copy(buf.at[u % n_buf], dst[c].at[k, pl.ds(t0, SC_CHUNK)],
                                             put_sem.at[u % n_buf])

            n = len(units)
            for u in range(n + lag):
                if u < n:
                    if u >= n_buf:
                        put(u - n_buf).wait()
                    get(u).start()
                if 0 <= u - lag < n:
                    get(u - lag).wait()
                    put(u - lag).start()
            for u in range(n - n_buf, n):
                put(u).wait()

    return run(*pieces, pos)


def _group_tile(n_tokens):
    per_expert = n_tokens * TOP_K // N_EXPERTS
    return max(MXU_DIM, min(4 * MXU_DIM, per_expert // MXU_DIM * MXU_DIM))


SECOND_DMA_QUEUE = 1


def _experts_kernel(te_ref, nv_ref, nu_ref, *refs, cast_weights, tile):
    x_hbm = refs[:N_PIECES]
    w_in = refs[N_PIECES:N_PIECES + 3]
    y_refs = refs[N_PIECES + 3:2 * N_PIECES + 3]
    xbuf, xsem = refs[-2:]
    i = pl.program_id(0)
    n_used = nu_ref[0]
    slot = lax.rem(i, 2)

    def x_copies(step, into):
        r = pl.ds(pl.multiple_of(step * tile, tile), tile)
        return [pltpu.make_async_copy(x_hbm[c].at[r], xbuf.at[into, c], xsem.at[into, c])
                for c in range(N_PIECES)]

    @pl.when(i == 0)
    def _():
        for cp in x_copies(0, 0):
            cp.start(priority=SECOND_DMA_QUEUE)

    @pl.when(i + 1 < n_used)
    def _():
        for cp in x_copies(i + 1, 1 - slot):
            cp.start(priority=SECOND_DMA_QUEUE)

    if cast_weights:
        w_bf = refs[2 * N_PIECES + 3:-2]
        last = n_used - 1
        cur = te_ref[jnp.minimum(i, last)]
        prev = te_ref[jnp.minimum(jnp.maximum(i - 1, 0), last)]

        @pl.when((i == 0) | (cur != prev))
        def _():
            for dst, src in zip(w_bf, w_in):
                dst[...] = src[...].astype(BF16)
    else:
        w_bf = w_in
    wg_ref, wu_ref, wd_ref = w_bf

    @pl.when(i < n_used)
    def _():
        for cp in x_copies(i, slot):
            cp.wait()
        for s in range(tile // MXU_DIM):
            @pl.when(s * MXU_DIM < nv_ref[i])
            def _(s=s):
                rows = slice(s * MXU_DIM, (s + 1) * MXU_DIM)
                lo, hi = _unpack_rows([xbuf[slot, c, rows, :] for c in range(N_PIECES)])
                x = jnp.concatenate(lo + hi, axis=1).astype(BF16)
                hid = _silu(_dot(x, wg_ref[...])) * _dot(x, wu_ref[...])
                y = _dot(hid.astype(BF16), wd_ref[...])
                for ref, piece in zip(y_refs, _pack_rows(y)):
                    ref[rows, :] = piece


def _experts(x_pieces, tile_expert, tile_rows, n_used, wg, wu, wd, *, group_tile):
    n_rows = x_pieces[0].shape[0]
    n_tiles = n_rows // group_tile
    cast_weights = wg.dtype != BF16

    def tile(i, nu):
        return jnp.minimum(i, nu[0] - 1)

    row = pl.BlockSpec((group_tile, LANES), lambda i, te, nv, nu: (tile(i, nu), 0))
    wspec = lambda a: pl.BlockSpec((None,) + a.shape[1:], lambda i, te, nv, nu: (te[tile(i, nu)], 0, 0))
    w_specs = [wspec(wg), wspec(wu), wspec(wd)]
    y_shape = [jax.ShapeDtypeStruct((n_rows, LANES), I32)] * N_PIECES
    w_shape = [jax.ShapeDtypeStruct(a.shape, BF16) for a in (wg, wu, wd)]
    outs = pl.pallas_call(
        functools.partial(_experts_kernel, cast_weights=cast_weights, tile=group_tile),
        grid_spec=pltpu.PrefetchScalarGridSpec(
            num_scalar_prefetch=3,
            grid=(n_tiles,),
            in_specs=[pl.BlockSpec(memory_space=pl.ANY)] * N_PIECES + w_specs,
            out_specs=[row] * N_PIECES + (w_specs if cast_weights else []),
            scratch_shapes=[pltpu.VMEM((2, N_PIECES, group_tile, LANES), I32),
                            pltpu.SemaphoreType.DMA((2, N_PIECES))]),
        out_shape=y_shape + (w_shape if cast_weights else []),
        compiler_params=_cparams("arbitrary"),
        name="experts_cast" if cast_weights else "experts",
    )(tile_expert, tile_rows, n_used, *x_pieces, wg, wu, wd)
    return outs[:N_PIECES], (tuple(outs[N_PIECES:]) if cast_weights else (wg, wu, wd))


def _moe_out_kernel(h_ref, wt_ref, x1_ref, mod_ref, fg_ref, sg_ref, su_ref, sd_ref, *refs, tm):
    y_hbm = refs[:N_PIECES]
    out_ref = refs[N_PIECES]
    ybuf, ysem = refs[N_PIECES + 1:]
    i = pl.program_id(0)
    slot = lax.rem(i, 2)

    def y_copies(step, into):
        r = pl.ds(pl.multiple_of(step * tm, tm), tm)
        return [pltpu.make_async_copy(y_hbm[c].at[:, r, :], ybuf.at[into, c], ysem.at[into, c])
                for c in range(N_PIECES)]

    @pl.when(i == 0)
    def _():
        for cp in y_copies(0, 0):
            cp.start(priority=SECOND_DMA_QUEUE)

    @pl.when(i + 1 < pl.num_programs(0))
    def _():
        for cp in y_copies(i + 1, 1 - slot):
            cp.start(priority=SECOND_DMA_QUEUE)

    h = h_ref[...]
    hid = _silu(_dot(h, sg_ref[...])) * _dot(h, su_ref[...])
    shared = _dot(hid.astype(BF16), sd_ref[...])
    wt = wt_ref[...]
    for cp in y_copies(i, slot):
        cp.wait()
    lo_acc = [None] * N_PIECES
    hi_acc = [None] * N_PIECES
    for k in range(TOP_K):
        wk = wt[:, k:k + 1]
        lo, hi = _unpack_rows([ybuf[slot, c, k] for c in range(N_PIECES)])
        for c in range(N_PIECES):
            lo_acc[c] = wk * lo[c] if k == 0 else lo_acc[c] + wk * lo[c]
            hi_acc[c] = wk * hi[c] if k == 0 else hi_acc[c] + wk * hi[c]
    routed = jnp.concatenate(lo_acc + hi_acc, axis=1)
    x2 = x1_ref[...] + mod_ref[0, 5:6, :] * (shared + routed)
    out_ref[...] = x2 * lax.rsqrt(jnp.mean(x2 * x2, axis=-1, keepdims=True) + EPS) * fg_ref[...]


def _moe_out(h2, wt, x1, mods, final_g, sg, su, sd, y_pieces, *, tokens_per_mod, tm):
    t = h2.shape[0]
    tiles_per_mod = tokens_per_mod // tm
    row = lambda w: pl.BlockSpec((tm, w), lambda i: (i, 0))
    full = lambda a: pl.BlockSpec(a.shape, lambda i: (0,) * a.ndim)
    return pl.pallas_call(
        functools.partial(_moe_out_kernel, tm=tm),
        grid=(t // tm,),
        in_specs=[row(D_MODEL), row(LANES), row(D_MODEL),
                  pl.BlockSpec((1, 6, D_MODEL), lambda i: (i // tiles_per_mod, 0, 0)),
                  full(final_g), full(sg), full(su), full(sd)]
        + [pl.BlockSpec(memory_space=pl.ANY)] * N_PIECES,
        out_specs=row(D_MODEL),
        out_shape=jax.ShapeDtypeStruct((t, D_MODEL), F32),
        scratch_shapes=[pltpu.VMEM((2, N_PIECES, TOP_K, tm, LANES), I32),
                        pltpu.SemaphoreType.DMA((2, N_PIECES))],
        compiler_params=_cparams("arbitrary"),
        name="moe_out",
    )(h2, wt, x1, mods, final_g, sg, su, sd, *y_pieces)


def _trunk(x, mods, s0f, s0b, w, expert_w, *, batch, seq_len, on_grid):
    t = batch * seq_len
    tokens_per_mod = t // mods.shape[0]
    cos_t, sin_t = _rope_tables(seq_len)
    q, k, v, gsw, up, ga, gb = _inproj(x, mods, w["norm1_g"], w["w_in"], cos_t, sin_t,
                                       tokens_per_mod=tokens_per_mod, seq_len=seq_len,
                                       on_grid=on_grid, tm=256)
    z, s_f, s_b = _retention(q, k, v, gsw, w["dec"], s0f, s0b, batch=batch, seq_len=seq_len)
    p = _pool(up, w["pool_w"], w["pool_scale"], batch=batch, seq_len=seq_len, on_grid=on_grid)
    x1, h2, h2_pieces = _merge(x, z, p, ga, gb, mods, w["norm2_g"], w["w_br_ret"], w["w_br_pool"],
                               w["w_out"], tokens_per_mod=tokens_per_mod, tm=512)

    group_tile = _group_tile(t)
    n_rows = t * TOP_K + N_EXPERTS * group_tile
    idx, rank, wt, counts = _route(h2, w["router_wt"], w["router_bias"], tm=1024)
    min_tiles = int(expert_w[0].dtype != BF16)
    pos, tile_expert, tile_rows, n_used = _plan(idx, rank, counts, n_tiles=n_rows // group_tile, tf=2048,
                                                group_tile=group_tile, min_tiles=min_tiles)
    x_sorted = _sc_dispatch(h2_pieces, pos, n_rows=n_rows)
    y_sorted, expert_w = _experts(x_sorted, tile_expert.reshape(-1), tile_rows.reshape(-1),
                                  n_used.reshape(-1), *expert_w, group_tile=group_tile)
    y_tok = _sc_collect(y_sorted, pos, n_tokens=t)
    y = _moe_out(h2, wt, x1, mods, w["final_g"], w["sh_w_gate"], w["sh_w_up"], w["sh_w_down"], y_tok,
                 tokens_per_mod=tokens_per_mod, tm=512)
    return y, s_f, s_b, expert_w


def kernel(x_prompt, x_sample, state_ret_fwd, state_ret_bwd, c, c_ctx, ada_w, ada_b, norm1_g, norm2_g, w_in,
           ret_decay_fwd, ret_decay_bwd, w_br_ret, pool_w, pool_scale, w_br_pool, w_out, router_w, router_bias,
           exp_w_gate, exp_w_up, exp_w_down, sh_w_gate, sh_w_up, sh_w_down, final_norm_g):
    n_req, seq, d = x_prompt.shape
    n_dec, dec_seq, _ = x_sample.shape
    depth = ada_w.shape[0]
    assert depth == 1 and d == D_MODEL

    xc = x_prompt.reshape(n_req * seq, d)
    xs = x_sample.reshape(n_dec * dec_seq, d)
    zero_state = jnp.zeros((n_req, RET_HEADS, RET_DK, RET_DV), F32)
    new_f, new_b = [], []
    for l in range(depth):
        c_rows = jnp.concatenate([c_ctx[None, :], c, jnp.zeros((8 - 1 - n_dec, d), F32)], axis=0)
        mods = _ada(c_rows, ada_w[l], ada_b[l]).reshape(8, 6, d)
        pad_rows = LANES - N_EXPERTS
        w = dict(
            norm1_g=norm1_g[l].reshape(1, d), norm2_g=norm2_g[l].reshape(1, d),
            final_g=final_norm_g.reshape(1, d),
            w_in=w_in[l].astype(BF16),
            dec=jnp.stack([ret_decay_fwd[l], ret_decay_bwd[l]]).astype(F32),
            w_br_ret=w_br_ret[l].astype(BF16), pool_w=pool_w[l].astype(BF16),
            pool_scale=pool_scale[l].reshape(1, POOL_W), w_br_pool=w_br_pool[l].astype(BF16),
            w_out=w_out[l].astype(BF16),
            router_wt=jnp.pad(router_w[l].T, ((0, pad_rows), (0, 0))).astype(BF16),
            router_bias=jnp.pad(router_bias[l].astype(F32).reshape(N_EXPERTS, 1), ((0, pad_rows), (0, 0))),
            sh_w_gate=sh_w_gate[l].astype(BF16),
            sh_w_up=sh_w_up[l].astype(BF16), sh_w_down=sh_w_down[l].astype(BF16),
        )
        expert_w = (exp_w_gate[l], exp_w_up[l], exp_w_down[l])
        xs, _, _, expert_w = _trunk(xs, mods[1:1 + n_dec], state_ret_fwd[:, l].astype(F32),
                                    state_ret_bwd[:, l].astype(F32), w, expert_w,
                                    batch=n_dec, seq_len=dec_seq, on_grid=True)
        xc, s_f, s_b, _ = _trunk(xc, mods[0:1], zero_state, zero_state, w, expert_w,
                                 batch=n_req, seq_len=seq, on_grid=False)
        new_f.append(s_f)
        new_b.append(s_b)
    y_prompt = xc.reshape(n_req, seq, d)
    y_sample = xs.reshape(n_dec, dec_seq, d)
    return (y_prompt, y_sample, jnp.stack(new_f, axis=1).astype(x_prompt.dtype),
            jnp.stack(new_b, axis=1).astype(x_prompt.dtype))
```

```python
import functools

import numpy as np
import jax
import jax.numpy as jnp
from jax import lax
from jax.experimental import pallas as pl
from jax.experimental.pallas import tpu as pltpu
from jax.experimental.pallas import tpu_sc as plsc

D_MODEL = 1024
GRID_W = 64
RET_HEADS = 4
RET_DK = 128
RET_DV = 256
RET_QK_W = RET_HEADS * RET_DK
RET_V_W = RET_HEADS * RET_DV
RET_CHUNK = 128
ROPE_BASE = 10000.0
POOL_GROUPS = 4
POOL_CH = 128
POOL_W = POOL_GROUPS * POOL_CH
POOL_WINDOWS = (2, 4, 8, 16)
N_EXPERTS = 64
TOP_K = 8
N_EXPERT_GROUPS = 8
GROUP_SIZE = N_EXPERTS // N_EXPERT_GROUPS
TOPK_GROUPS = 4
D_EXPERT = 256
ROUTED_SCALE = 2.5
EPS = 1e-6
IN_SIZES = (RET_QK_W, RET_QK_W, RET_V_W, RET_V_W, POOL_W, D_MODEL, D_MODEL)
IN_OFFS = tuple(sum(IN_SIZES[:i]) for i in range(len(IN_SIZES) + 1))
IN_W = IN_OFFS[-1]

LANES = 128
VMEM_LIMIT = 56 << 20
N_PIECES = D_MODEL // 2 // LANES
MXU_DIM = 256
SC_CHUNK = 128

F32 = jnp.float32
BF16 = jnp.bfloat16
I32 = jnp.int32
U32 = jnp.uint32


def _cparams(*sem):
    return pltpu.CompilerParams(dimension_semantics=sem, vmem_limit_bytes=VMEM_LIMIT)


def _dot(a, b):
    return jnp.dot(a, b, preferred_element_type=F32)


def _silu(x):
    return x * jax.nn.sigmoid(x)


def _rms_mod(x, g, scale, shift):
    y = x * lax.rsqrt(jnp.mean(x * x, axis=-1, keepdims=True) + EPS)
    return (y * g) * (1.0 + scale) + shift


def _ada_kernel(c_ref, w_ref, b_ref, o_ref):
    c = c_ref[...]
    o_ref[...] = jnp.dot(_silu(c), w_ref[...], preferred_element_type=F32,
                         precision=lax.Precision.HIGHEST) + b_ref[...]


def _ada(c_rows, ada_w, ada_b):
    r = c_rows.shape[0]
    n = ada_w.shape[1]
    tn = D_MODEL
    return pl.pallas_call(
        _ada_kernel,
        grid=(n // tn,),
        in_specs=[pl.BlockSpec((r, D_MODEL), lambda j: (0, 0)),
                  pl.BlockSpec((D_MODEL, tn), lambda j: (0, j)),
                  pl.BlockSpec((1, tn), lambda j: (0, j))],
        out_specs=pl.BlockSpec((r, tn), lambda j: (0, j)),
        out_shape=jax.ShapeDtypeStruct((r, n), F32),
        compiler_params=_cparams("parallel"),
        name="ada_mod",
    )(c_rows, ada_w, ada_b.reshape(1, n))


def _inproj_kernel(x_ref, mod_ref, g_ref, w_ref, cos_ref, sin_ref,
                   q_ref, k_ref, v_ref, gsw_ref, up_ref, ga_ref, gb_ref, *, on_grid):
    x = x_ref[...]
    h = _rms_mod(x, g_ref[...], mod_ref[0, 1:2, :], mod_ref[0, 0:1, :]).astype(BF16)

    def seg(i):
        return _dot(h, w_ref[:, IN_OFFS[i]:IN_OFFS[i + 1]])

    q = seg(0)
    k = seg(1)
    if on_grid:
        cos = jnp.concatenate([cos_ref[...]] * RET_HEADS, axis=1)
        sin = jnp.concatenate([sin_ref[...]] * RET_HEADS, axis=1)
        lane = lax.broadcasted_iota(jnp.int32, q.shape, 1)
        first = (lane & 63) < 32

        def rope(a):
            up = pltpu.roll(a, RET_QK_W - 32, axis=1)
            dn = pltpu.roll(a, 32, axis=1)
            return a * cos + jnp.where(first, up, dn) * sin

        q = rope(q)
        k = rope(k)
    q_ref[...] = q.astype(BF16)
    k_ref[...] = (k * (RET_DK ** -0.5)).astype(BF16)
    v_ref[...] = seg(2).astype(BF16)
    gsw_ref[...] = seg(3).astype(BF16)
    up_ref[...] = seg(4).astype(BF16)
    ga_ref[...] = seg(5).astype(BF16)
    gb_ref[...] = seg(6).astype(BF16)


def _inproj(x, mods, norm_g, w_in, cos_t, sin_t, *, tokens_per_mod, seq_len, on_grid, tm):
    t = x.shape[0]
    tiles_per_mod = tokens_per_mod // tm
    tiles_per_seq = seq_len // tm
    widths = IN_SIZES
    out_shape = [jax.ShapeDtypeStruct((t, w), BF16) for w in widths]
    out_specs = [pl.BlockSpec((tm, w), lambda i: (i, 0)) for w in widths]
    return pl.pallas_call(
        functools.partial(_inproj_kernel, on_grid=on_grid),
        grid=(t // tm,),
        in_specs=[pl.BlockSpec((tm, D_MODEL), lambda i: (i, 0)),
                  pl.BlockSpec((1, 6, D_MODEL), lambda i: (i // tiles_per_mod, 0, 0)),
                  pl.BlockSpec((1, D_MODEL), lambda i: (0, 0)),
                  pl.BlockSpec((D_MODEL, IN_W), lambda i: (0, 0)),
                  pl.BlockSpec((tm, RET_DK), lambda i: (i % tiles_per_seq, 0)),
                  pl.BlockSpec((tm, RET_DK), lambda i: (i % tiles_per_seq, 0))],
        out_specs=out_specs,
        out_shape=out_shape,
        compiler_params=_cparams("parallel"),
        name="inproj_grid" if on_grid else "inproj_seq",
    )(x, mods, norm_g, w_in, cos_t, sin_t)


def _rope_tables(seq_len):
    t = np.arange(seq_len)
    row = (t // GRID_W).astype(np.float32)
    col = (t % GRID_W).astype(np.float32)
    m = RET_DK // 4
    inv = (np.float32(ROPE_BASE) ** (-np.arange(m, dtype=np.float32) / np.float32(m))).astype(np.float32)
    ar = row[:, None] * inv
    ac = col[:, None] * inv
    cos = np.concatenate([np.cos(ar), np.cos(ar), np.cos(ac), np.cos(ac)], axis=1)
    sin = np.concatenate([-np.sin(ar), np.sin(ar), -np.sin(ac), np.sin(ac)], axis=1)
    return jnp.asarray(cos, F32), jnp.asarray(sin, F32)


RET_HEADS_PER_STEP = 2


def _ret_kernel(dec_ref, q_ref, k_ref, v_ref, g_ref, s0f_ref, s0b_ref,
                z_ref, sf_ref, sb_ref, oacc_ref, kt_ref, *, n_chunks):
    c = RET_CHUNK
    heads = RET_HEADS_PER_STEP
    half = n_chunks // 2
    ii = lax.broadcasted_iota(I32, (c, c), 0)
    jj = lax.broadcasted_iota(I32, (c, c), 1)
    ik = lax.broadcasted_iota(I32, (c, RET_DK), 0).astype(F32)
    jk = lax.broadcasted_iota(I32, (RET_DK, c), 1).astype(F32)

    def log_gamma(d, shape):
        return jnp.log1p(-jnp.exp2(-jnp.full(shape, d, F32)))

    consts = {}
    for hh in range(heads):
        h = pl.program_id(1) * heads + hh
        dec_f = dec_ref[0, h]
        dec_b = dec_ref[1, h]
        rel = (ii - jj).astype(F32)
        consts[hh, "f"] = (
            jnp.where(rel >= 0, jnp.exp(log_gamma(dec_f, (c, c)) * jnp.maximum(rel, 0.0)), 0.0),
            jnp.exp(log_gamma(dec_f, (c, RET_DK)) * (ik + 1.0)),
            jnp.exp(log_gamma(dec_f, (RET_DK, c)) * (c - 1.0 - jk)),
            jnp.exp(log_gamma(dec_f, (RET_DK, RET_DV)) * c))
        consts[hh, "b"] = (
            jnp.where(rel <= 0, jnp.exp(log_gamma(dec_b, (c, c)) * jnp.maximum(-rel, 0.0)), 0.0),
            jnp.exp(log_gamma(dec_b, (c, RET_DK)) * (c - ik)),
            jnp.exp(log_gamma(dec_b, (RET_DK, c)) * jk),
            jnp.exp(log_gamma(dec_b, (RET_DK, RET_DV)) * c))

    sf_ref[...] = s0f_ref[...]
    sb_ref[...] = s0b_ref[...]

    def transpose_keys(ci, carry):
        r = pl.ds(pl.multiple_of(ci * c, c), c)
        for hh in range(heads):
            kt_ref[hh, ci] = k_ref[r, hh * RET_DK:(hh + 1) * RET_DK].T
        return carry

    lax.fori_loop(0, n_chunks, transpose_keys, 0)

    def scores(ci, hh, direction):
        r = pl.ds(pl.multiple_of(ci * c, c), c)
        kcols = slice(hh * RET_DK, (hh + 1) * RET_DK)
        qc = q_ref[r, kcols]
        sc = lax.dot_general(qc, k_ref[r, kcols], (((1,), (1,)), ((), ())), preferred_element_type=F32)
        return ci, hh, direction, r, qc, sc

    def advance(job):
        ci, hh, direction, r, qc, sc = job
        dmask, qdec, kdec, cdec = consts[hh, direction]
        s_ref = sf_ref if direction == "f" else sb_ref
        vc = v_ref[r, hh * RET_DV:(hh + 1) * RET_DV]
        s = s_ref[hh]
        lhs = jnp.concatenate([(sc * dmask).astype(BF16), (qc.astype(F32) * qdec).astype(BF16)], axis=1)
        o = _dot(lhs, jnp.concatenate([vc, s.astype(BF16)], axis=0))
        kd_t = (kt_ref[hh, ci].astype(F32) * kdec).astype(BF16)
        s_ref[hh] = s * cdec + _dot(kd_t, vc)
        return o

    def emit(job, o, second):
        _, hh, _, r, _, _ = job
        vcols = slice(hh * RET_DV, (hh + 1) * RET_DV)
        if not second:
            oacc_ref[hh, r, :] = o
        else:
            o = o + oacc_ref[hh, r, :]
            o = o * lax.rsqrt(jnp.mean(o * o, axis=-1, keepdims=True) + EPS)
            g = g_ref[r, vcols].astype(F32)
            z_ref[r, vcols] = (_silu(g) * o).astype(BF16)

    def body(second):
        def run(t, carry):
            jobs = [scores(ci, hh, d) for hh in range(heads)
                    for ci, d in ((t, "f"), (n_chunks - 1 - t, "b"))]
            outs = [advance(job) for job in jobs]
            for job, o in zip(jobs, outs):
                emit(job, o, second)
            return carry
        return run

    lax.fori_loop(0, half, body(False), 0, unroll=4 if half % 4 == 0 else 1)
    lax.fori_loop(half, n_chunks, body(True), 0, unroll=2 if half % 2 == 0 else 1)


def _retention(q, k, v, gsw, dec, s0f, s0b, *, batch, seq_len):
    n_chunks = seq_len // RET_CHUNK
    assert n_chunks % 2 == 0
    heads = RET_HEADS_PER_STEP
    t = batch * seq_len
    st_spec = pl.BlockSpec((None, heads, RET_DK, RET_DV), lambda b, h: (b, h, 0, 0))
    st_shape = jax.ShapeDtypeStruct((batch, RET_HEADS, RET_DK, RET_DV), F32)
    kspec = pl.BlockSpec((seq_len, heads * RET_DK), lambda b, h: (b, h))
    vspec = pl.BlockSpec((seq_len, heads * RET_DV), lambda b, h: (b, h))
    return pl.pallas_call(
        functools.partial(_ret_kernel, n_chunks=n_chunks),
        grid=(batch, RET_HEADS // heads),
        in_specs=[pl.BlockSpec(memory_space=pltpu.SMEM), kspec, kspec, vspec, vspec, st_spec, st_spec],
        out_specs=[vspec, st_spec, st_spec],
        out_shape=[jax.ShapeDtypeStruct((t, RET_V_W), BF16), st_shape, st_shape],
        scratch_shapes=[pltpu.VMEM((heads, seq_len, RET_DV), F32),
                        pltpu.VMEM((heads, n_chunks, RET_DK, RET_CHUNK), BF16)],
        compiler_params=_cparams("parallel", "parallel"),
        name=f"retention_l{seq_len}",
    )(dec, q, k, v, gsw, s0f, s0b)


def _pool_kernel(u_ref, w_ref, sc_ref, o_ref, *, seq_len, on_grid):
    tok = lax.broadcasted_iota(jnp.int32, (seq_len, POOL_CH), 0)

    def shift(a, s, stride, pos, width):
        y = pltpu.roll(a, (-s * stride) % seq_len, axis=0)
        ok = (pos < width - s) if s > 0 else (pos >= -s)
        return jnp.where(ok, y, 0.0)

    def box_mean(a, window, stride, pos, width):
        half = window // 2
        fw = a
        bw = shift(a, -1, stride, pos, width)
        m = 1
        while m < half:
            fw = fw + shift(fw, m, stride, pos, width)
            bw = bw + shift(bw, -m, stride, pos, width)
            m *= 2
        cnt = jnp.minimum(pos + half, width) - jnp.maximum(pos - half, 0)
        return (fw + bw) / cnt.astype(F32)

    for g, window in enumerate(POOL_WINDOWS):
        cols = slice(g * POOL_CH, (g + 1) * POOL_CH)
        ug = u_ref[:, cols].astype(F32)
        if on_grid:
            pooled = box_mean(ug, window, 1, tok & (GRID_W - 1), GRID_W)
            pooled = box_mean(pooled, window, GRID_W, tok >> 6, seq_len // GRID_W)
        else:
            pooled = box_mean(ug, window, 1, tok, seq_len)
        d = (pooled - ug).astype(BF16)
        o_ref[:, cols] = (_dot(d, w_ref[g]) * sc_ref[:, cols]).astype(BF16)


def _pool(u, pool_w, pool_scale, *, batch, seq_len, on_grid):
    t = batch * seq_len
    return pl.pallas_call(
        functools.partial(_pool_kernel, seq_len=seq_len, on_grid=on_grid),
        grid=(batch,),
        in_specs=[pl.BlockSpec((seq_len, POOL_W), lambda b: (b, 0)),
                  pl.BlockSpec((POOL_GROUPS, POOL_CH, POOL_CH), lambda b: (0, 0, 0)),
                  pl.BlockSpec((1, POOL_W), lambda b: (0, 0))],
        out_specs=pl.BlockSpec((seq_len, POOL_W), lambda b: (b, 0)),
        out_shape=jax.ShapeDtypeStruct((t, POOL_W), BF16),
        compiler_params=_cparams("parallel"),
        name=f"pool_l{seq_len}",
    )(u, pool_w, pool_scale)


def _pack_rows(x):
    half = D_MODEL // 2
    lo = lax.bitcast_convert_type(x[:, :half].astype(BF16).astype(F32), U32) >> 16
    hi = lax.bitcast_convert_type(x[:, half:].astype(BF16).astype(F32), U32) & jnp.uint32(0xFFFF0000)
    word = lax.bitcast_convert_type(hi | lo, I32)
    return [word[:, c * LANES:(c + 1) * LANES] for c in range(N_PIECES)]


def _unpack_rows(pieces):
    words = [lax.bitcast_convert_type(p, U32) for p in pieces]
    lo = [lax.bitcast_convert_type(w << 16, F32) for w in words]
    hi = [lax.bitcast_convert_type(w & jnp.uint32(0xFFFF0000), F32) for w in words]
    return lo, hi


def _merge_kernel(x_ref, z_ref, p_ref, ga_ref, gb_ref, mod_ref, g2_ref, wr_ref, wp_ref, wo_ref,
                  x1_ref, h2_ref, *piece_refs):
    y_ret = _dot(z_ref[...], wr_ref[...])
    y_pool = _dot(p_ref[...], wp_ref[...])
    merged = (jax.nn.sigmoid(ga_ref[...].astype(F32)) * y_ret
              + jax.nn.sigmoid(gb_ref[...].astype(F32)) * y_pool)
    x1 = x_ref[...] + mod_ref[0, 2:3, :] * _dot(merged.astype(BF16), wo_ref[...])
    x1_ref[...] = x1
    h2 = _rms_mod(x1, g2_ref[...], mod_ref[0, 4:5, :], mod_ref[0, 3:4, :])
    h2_ref[...] = h2.astype(BF16)
    for ref, piece in zip(piece_refs, _pack_rows(h2)):
        ref[...] = piece


def _merge(x, z, p, ga, gb, mods, norm2_g, w_br_ret, w_br_pool, w_out, *, tokens_per_mod, tm):
    t = x.shape[0]
    tiles_per_mod = tokens_per_mod // tm
    row = lambda w: pl.BlockSpec((tm, w), lambda i: (i, 0))
    full = lambda a: pl.BlockSpec(a.shape, lambda i: (0,) * a.ndim)
    outs = pl.pallas_call(
        _merge_kernel,
        grid=(t // tm,),
        in_specs=[row(D_MODEL), row(RET_V_W), row(POOL_W), row(D_MODEL), row(D_MODEL),
                  pl.BlockSpec((1, 6, D_MODEL), lambda i: (i // tiles_per_mod, 0, 0)),
                  full(norm2_g), full(w_br_ret), full(w_br_pool), full(w_out)],
        out_specs=[row(D_MODEL), row(D_MODEL)] + [row(LANES)] * N_PIECES,
        out_shape=[jax.ShapeDtypeStruct((t, D_MODEL), F32), jax.ShapeDtypeStruct((t, D_MODEL), BF16)]
        + [jax.ShapeDtypeStruct((t, LANES), I32)] * N_PIECES,
        compiler_params=_cparams("parallel"),
        name="merge",
    )(x, z, p, ga, gb, mods, norm2_g, w_br_ret, w_br_pool, w_out)
    return outs[0], outs[1], outs[2:]


def _route_kernel(h_ref, rw_ref, bias_ref, idx_ref, rank_ref, wt_ref, cnt_ref, carry_ref):
    e = N_EXPERTS
    tm = h_ref.shape[0]
    neg = -jnp.inf

    @pl.when(pl.program_id(0) == 0)
    def _():
        carry_ref[...] = jnp.zeros(carry_ref.shape, F32)

    logits = lax.dot_general(rw_ref[...], h_ref[...], (((1,), (1,)), ((), ())),
                             preferred_element_type=F32)[:e]
    scores = jax.nn.sigmoid(logits)
    sel = scores + bias_ref[:e, 0:1]
    e_idx = lax.broadcasted_iota(I32, (e, tm), 0)

    grp = sel.reshape(N_EXPERT_GROUPS, GROUP_SIZE, tm)
    m_idx = lax.broadcasted_iota(I32, grp.shape, 1)
    m1 = jnp.max(grp, axis=1, keepdims=True)
    first = jnp.min(jnp.where(grp == m1, m_idx, GROUP_SIZE), axis=1, keepdims=True)
    m2 = jnp.max(jnp.where(m_idx == first, neg, grp), axis=1, keepdims=True)
    gscore = (m1 + m2).reshape(N_EXPERT_GROUPS, tm)

    g_idx = lax.broadcasted_iota(I32, gscore.shape, 0)
    grank = jnp.zeros(gscore.shape, I32)
    for g in range(N_EXPERT_GROUPS):
        other = gscore[g:g + 1, :]
        beats = jnp.where(other > gscore, 1, jnp.where(other == gscore, (g_idx > g).astype(I32), 0))
        grank = grank + beats
    gkeep = (grank < TOPK_GROUPS).astype(F32)
    ekeep = jnp.broadcast_to(gkeep.reshape(N_EXPERT_GROUPS, 1, tm), grp.shape).reshape(e, tm)
    masked = jnp.where(ekeep > 0, sel, neg)

    chosen = jnp.zeros((e, tm), F32)
    picks, hits = [], []
    for _ in range(TOP_K):
        m = jnp.max(masked, axis=0, keepdims=True)
        pick = jnp.min(jnp.where(masked == m, e_idx, e), axis=0, keepdims=True)
        hit = e_idx == pick
        chosen = jnp.where(hit, 1.0, chosen)
        masked = jnp.where(hit, neg, masked)
        picks.append(pick)
        hits.append(hit)

    w = scores * chosen
    comb = w / jnp.sum(w, axis=0, keepdims=True) * ROUTED_SCALE

    t_row = lax.broadcasted_iota(I32, (tm, tm), 0)
    t_col = lax.broadcasted_iota(I32, (tm, tm), 1)
    before = (t_row < t_col).astype(BF16)
    rankmat = _dot(chosen.astype(BF16), before) + carry_ref[:e, 0:1]
    carry_ref[:e, :] = carry_ref[:e, :] + jnp.sum(chosen, axis=1, keepdims=True)
    cnt_ref[...] = carry_ref[...]

    idx_ref[...] = jnp.concatenate(picks, axis=0)
    rank_ref[...] = jnp.concatenate(
        [jnp.sum(jnp.where(h, rankmat, 0.0), axis=0, keepdims=True) for h in hits], axis=0).astype(I32)
    w_rows = [jnp.sum(jnp.where(h, comb, 0.0), axis=0, keepdims=True) for h in hits]
    wt_ref[...] = jnp.concatenate(w_rows + [jnp.zeros((LANES - TOP_K, tm), F32)], axis=0).T


def _route(h2, router_wt, bias_col, *, tm):
    t = h2.shape[0]
    krow = pl.BlockSpec((TOP_K, tm), lambda i: (0, i))
    return pl.pallas_call(
        _route_kernel,
        grid=(t // tm,),
        in_specs=[pl.BlockSpec((tm, D_MODEL), lambda i: (i, 0)),
                  pl.BlockSpec((LANES, D_MODEL), lambda i: (0, 0)),
                  pl.BlockSpec((LANES, 1), lambda i: (0, 0))],
        out_specs=[krow, krow, pl.BlockSpec((tm, LANES), lambda i: (i, 0)),
                   pl.BlockSpec((LANES, LANES), lambda i: (0, 0))],
        out_shape=[jax.ShapeDtypeStruct((TOP_K, t), I32), jax.ShapeDtypeStruct((TOP_K, t), I32),
                   jax.ShapeDtypeStruct((t, LANES), F32), jax.ShapeDtypeStruct((LANES, LANES), F32)],
        scratch_shapes=[pltpu.VMEM((LANES, LANES), F32)],
        compiler_params=_cparams("arbitrary"),
        name="route",
    )(h2, router_wt, bias_col)


def _plan_kernel(idx_ref, rank_ref, cnt_ref, pos_ref, te_ref, nu_ref, *, group_tile, min_tiles):
    tf = idx_ref.shape[1]
    nt = te_ref.shape[1]
    cnt = cnt_ref[...].astype(I32)
    tiles = jnp.maximum((cnt + (group_tile - 1)) // group_tile, min_tiles)
    e_sub = lax.broadcasted_iota(I32, (LANES, LANES), 0)
    padded = jnp.where(e_sub < N_EXPERTS, tiles * group_tile, 0).astype(F32)
    e_lane = lax.broadcasted_iota(I32, (LANES, LANES), 1)
    base = jnp.sum(jnp.where(e_lane < e_sub, padded.T, 0.0), axis=1, keepdims=True)
    end = base + padded[:, 0:1]

    idx = idx_ref[...]
    start = jnp.zeros(idx.shape, F32)
    for e in range(N_EXPERTS):
        start = jnp.where(idx == e, base[e:e + 1, 0:1], start)
    pos = start.astype(I32) + rank_ref[...]
    for j in range(tf // SC_CHUNK):
        pos_ref[j] = pos[:, j * SC_CHUNK:(j + 1) * SC_CHUNK]

    tile_start = (lax.broadcasted_iota(I32, (N_EXPERTS, nt), 1) * group_tile).astype(F32)
    done = jnp.sum(jnp.where(end[:N_EXPERTS] <= tile_start, 1.0, 0.0), axis=0, keepdims=True)
    te_ref[...] = jnp.minimum(done, N_EXPERTS - 1.0).astype(I32)
    total = jnp.sum(padded[:, 0:1], axis=0, keepdims=True)
    nu_ref[...] = jnp.broadcast_to(total * (1.0 / group_tile), nu_ref.shape).astype(I32)


def _plan(idx, rank, counts, *, n_tiles, tf, group_tile, min_tiles):
    t = idx.shape[1]
    nt_pad = -(-n_tiles // LANES) * LANES
    krow = pl.BlockSpec((TOP_K, tf), lambda i: (0, i))
    return pl.pallas_call(
        functools.partial(_plan_kernel, group_tile=group_tile, min_tiles=min_tiles),
        grid=(t // tf,),
        in_specs=[krow, krow, pl.BlockSpec((LANES, LANES), lambda i: (0, 0))],
        out_specs=[pl.BlockSpec((tf // SC_CHUNK, TOP_K, SC_CHUNK), lambda i: (i, 0, 0)),
                   pl.BlockSpec((1, nt_pad), lambda i: (0, 0)),
                   pl.BlockSpec((1, LANES), lambda i: (0, 0))],
        out_shape=[jax.ShapeDtypeStruct((t // SC_CHUNK, TOP_K, SC_CHUNK), I32),
                   jax.ShapeDtypeStruct((1, nt_pad), I32), jax.ShapeDtypeStruct((1, LANES), I32)],
        compiler_params=_cparams("arbitrary"),
        name="moe_plan",
    )(idx, rank, counts)


def _sc_mesh_info():
    info = plsc.get_sparse_core_info()
    mesh = plsc.VectorSubcoreMesh(core_axis_name="c", subcore_axis_name="s")
    return mesh, info.num_cores, info.num_cores * info.num_subcores


def _sc_dispatch(pieces, pos, *, n_rows):
    t = pieces[0].shape[0]
    mesh, n_cores, n_workers = _sc_mesh_info()
    per_w = t // SC_CHUNK // n_workers

    @functools.partial(
        pl.kernel, mesh=mesh,
        out_type=[jax.ShapeDtypeStruct((n_rows, LANES), I32)] * N_PIECES,
        scratch_types=[pltpu.VMEM((TOP_K, SC_CHUNK), I32),
                       pltpu.VMEM((N_PIECES, SC_CHUNK, LANES), I32),
                       pltpu.SemaphoreType.DMA((N_PIECES,)),
                       pltpu.SemaphoreType.DMA],
        name="sc_dispatch",
    )
    def run(*refs):
        src = refs[:N_PIECES]
        pos_hbm = refs[N_PIECES]
        dst = refs[N_PIECES + 1:2 * N_PIECES + 1]
        idx_v, rows_v, load_sem, put_sem = refs[2 * N_PIECES + 1:]
        wid = lax.axis_index("s") * n_cores + lax.axis_index("c")

        @pl.loop(0, per_w)
        def _(j):
            ch = wid * per_w + j
            t0 = pl.multiple_of(ch * SC_CHUNK, SC_CHUNK)
            loads = [pltpu.make_async_copy(src[c].at[pl.ds(t0, SC_CHUNK)], rows_v.at[c], load_sem.at[c])
                     for c in range(N_PIECES)]
            for ld in loads:
                ld.start()
            pltpu.sync_copy(pos_hbm.at[ch], idx_v)
            puts = []
            for c in range(N_PIECES):
                loads[c].wait()
                for k in range(TOP_K):
                    puts.append(pltpu.make_async_copy(rows_v.at[c], dst[c].at[idx_v.at[k]], put_sem))
                    puts[-1].start()
            for cp in puts:
                cp.wait()

    return run(*pieces, pos)


def _sc_collect(pieces, pos, *, n_tokens):
    mesh, n_cores, n_workers = _sc_mesh_info()
    per_w = n_tokens // SC_CHUNK // n_workers
    n_buf = 4
    lag = n_buf // 2
    units = [(k, c) for k in range(TOP_K) for c in range(N_PIECES)]

    @functools.partial(
        pl.kernel, mesh=mesh,
        out_type=[jax.ShapeDtypeStruct((TOP_K, n_tokens, LANES), I32)] * N_PIECES,
        scratch_types=[pltpu.VMEM((TOP_K, SC_CHUNK), I32),
                       pltpu.VMEM((n_buf, SC_CHUNK, LANES), I32),
                       pltpu.SemaphoreType.DMA((n_buf,)),
                       pltpu.SemaphoreType.DMA((n_buf,))],
        name="sc_collect",
    )
    def run(*refs):
        src = refs[:N_PIECES]
        pos_hbm = refs[N_PIECES]
        dst = refs[N_PIECES + 1:2 * N_PIECES + 1]
        idx_v, buf, get_sem, put_sem = refs[2 * N_PIECES + 1:]
        wid = lax.axis_index("s") * n_cores + lax.axis_index("c")

        @pl.loop(0, per_w)
        def _(j):
            ch = wid * per_w + j
            t0 = pl.multiple_of(ch * SC_CHUNK, SC_CHUNK)
            pltpu.sync_copy(pos_hbm.at[ch], idx_v)

            def get(u):
                k, c = units[u]
                return pltpu.make_async_copy(src[c].at[idx_v.at[k]], buf.at[u % n_buf], get_sem.at[u % n_buf])

            def put(u):
                k, c = units[u]
                return pltpu.make_async_copy(buf.at[u % n_buf], dst[c].at[k, pl.ds(t0, SC_CHUNK)],
                                             put_sem.at[u % n_buf])

            n = len(units)
            for u in range(n + lag):
                if u < n:
                    if u >= n_buf:
                        put(u - n_buf).wait()
                    get(u).start()
                if 0 <= u - lag < n:
                    get(u - lag).wait()
                    put(u - lag).start()
            for u in range(n - n_buf, n):
                put(u).wait()

    return run(*pieces, pos)


def _group_tile(n_tokens):
    per_expert = n_tokens * TOP_K // N_EXPERTS
    return max(MXU_DIM, min(4 * MXU_DIM, per_expert // MXU_DIM * MXU_DIM))


SECOND_DMA_QUEUE = 1


def _experts_kernel(te_ref, nu_ref, *refs, cast_weights, tile):
    x_hbm = refs[:N_PIECES]
    w_in = refs[N_PIECES:N_PIECES + 3]
    y_refs = refs[N_PIECES + 3:2 * N_PIECES + 3]
    xbuf, xsem = refs[-2:]
    i = pl.program_id(0)
    n_used = nu_ref[0]
    slot = lax.rem(i, 2)

    def x_copies(step, into):
        r = pl.ds(pl.multiple_of(step * tile, tile), tile)
        return [pltpu.make_async_copy(x_hbm[c].at[r], xbuf.at[into, c], xsem.at[into, c])
                for c in range(N_PIECES)]

    @pl.when(i == 0)
    def _():
        for cp in x_copies(0, 0):
            cp.start(priority=SECOND_DMA_QUEUE)

    @pl.when(i + 1 < n_used)
    def _():
        for cp in x_copies(i + 1, 1 - slot):
            cp.start(priority=SECOND_DMA_QUEUE)

    if cast_weights:
        w_bf = refs[2 * N_PIECES + 3:-2]
        last = n_used - 1
        cur = te_ref[jnp.minimum(i, last)]
        prev = te_ref[jnp.minimum(jnp.maximum(i - 1, 0), last)]

        @pl.when((i == 0) | (cur != prev))
        def _():
            for dst, src in zip(w_bf, w_in):
                dst[...] = src[...].astype(BF16)
    else:
        w_bf = w_in
    wg_ref, wu_ref, wd_ref = w_bf

    @pl.when(i < n_used)
    def _():
        for cp in x_copies(i, slot):
            cp.wait()
        subs = [slice(s * MXU_DIM, (s + 1) * MXU_DIM) for s in range(tile // MXU_DIM)]
        xs = []
        for rows in subs:
            lo, hi = _unpack_rows([xbuf[slot, c, rows, :] for c in range(N_PIECES)])
            xs.append(jnp.concatenate(lo + hi, axis=1).astype(BF16))
        gates = [(_dot(x, wg_ref[...]), _dot(x, wu_ref[...])) for x in xs]
        ys = [_dot((_silu(g) * u).astype(BF16), wd_ref[...]) for g, u in gates]
        for rows, y in zip(subs, ys):
            for ref, piece in zip(y_refs, _pack_rows(y)):
                ref[rows, :] = piece


def _experts(x_pieces, tile_expert, n_used, wg, wu, wd, *, group_tile):
    n_rows = x_pieces[0].shape[0]
    n_tiles = n_rows // group_tile
    cast_weights = wg.dtype != BF16

    def tile(i, te, nu):
        return jnp.minimum(i, nu[0] - 1)

    row = pl.BlockSpec((group_tile, LANES), lambda i, te, nu: (tile(i, te, nu), 0))
    wspec = lambda a: pl.BlockSpec((None,) + a.shape[1:], lambda i, te, nu: (te[tile(i, te, nu)], 0, 0))
    w_specs = [wspec(wg), wspec(wu), wspec(wd)]
    y_shape = [jax.ShapeDtypeStruct((n_rows, LANES), I32)] * N_PIECES
    w_shape = [jax.ShapeDtypeStruct(a.shape, BF16) for a in (wg, wu, wd)]
    outs = pl.pallas_call(
        functools.partial(_experts_kernel, cast_weights=cast_weights, tile=group_tile),
        grid_spec=pltpu.PrefetchScalarGridSpec(
            num_scalar_prefetch=2,
            grid=(n_tiles,),
            in_specs=[pl.BlockSpec(memory_space=pl.ANY)] * N_PIECES + w_specs,
            out_specs=[row] * N_PIECES + (w_specs if cast_weights else []),
            scratch_shapes=[pltpu.VMEM((2, N_PIECES, group_tile, LANES), I32),
                            pltpu.SemaphoreType.DMA((2, N_PIECES))]),
        out_shape=y_shape + (w_shape if cast_weights else []),
        compiler_params=_cparams("arbitrary"),
        name="experts_cast" if cast_weights else "experts",
    )(tile_expert, n_used, *x_pieces, wg, wu, wd)
    return outs[:N_PIECES], (tuple(outs[N_PIECES:]) if cast_weights else (wg, wu, wd))


def _moe_out_kernel(h_ref, wt_ref, x1_ref, mod_ref, fg_ref, sg_ref, su_ref, sd_ref, *refs, tm):
    y_hbm = refs[:N_PIECES]
    out_ref = refs[N_PIECES]
    ybuf, ysem = refs[N_PIECES + 1:]
    i = pl.program_id(0)
    slot = lax.rem(i, 2)

    def y_copies(step, into):
        r = pl.ds(pl.multiple_of(step * tm, tm), tm)
        return [pltpu.make_async_copy(y_hbm[c].at[:, r, :], ybuf.at[into, c], ysem.at[into, c])
                for c in range(N_PIECES)]

    @pl.when(i == 0)
    def _():
        for cp in y_copies(0, 0):
            cp.start(priority=SECOND_DMA_QUEUE)

    @pl.when(i + 1 < pl.num_programs(0))
    def _():
        for cp in y_copies(i + 1, 1 - slot):
            cp.start(priority=SECOND_DMA_QUEUE)

    h = h_ref[...]
    hid = _silu(_dot(h, sg_ref[...])) * _dot(h, su_ref[...])
    shared = _dot(hid.astype(BF16), sd_ref[...])
    wt = wt_ref[...]
    for cp in y_copies(i, slot):
        cp.wait()
    lo_acc = [None] * N_PIECES
    hi_acc = [None] * N_PIECES
    for k in range(TOP_K):
        wk = wt[:, k:k + 1]
        lo, hi = _unpack_rows([ybuf[slot, c, k] for c in range(N_PIECES)])
        for c in range(N_PIECES):
            lo_acc[c] = wk * lo[c] if k == 0 else lo_acc[c] + wk * lo[c]
            hi_acc[c] = wk * hi[c] if k == 0 else hi_acc[c] + wk * hi[c]
    routed = jnp.concatenate(lo_acc + hi_acc, axis=1)
    x2 = x1_ref[...] + mod_ref[0, 5:6, :] * (shared + routed)
    out_ref[...] = x2 * lax.rsqrt(jnp.mean(x2 * x2, axis=-1, keepdims=True) + EPS) * fg_ref[...]


def _moe_out(h2, wt, x1, mods, final_g, sg, su, sd, y_pieces, *, tokens_per_mod, tm):
    t = h2.shape[0]
    tiles_per_mod = tokens_per_mod // tm
    row = lambda w: pl.BlockSpec((tm, w), lambda i: (i, 0))
    full = lambda a: pl.BlockSpec(a.shape, lambda i: (0,) * a.ndim)
    return pl.pallas_call(
        functools.partial(_moe_out_kernel, tm=tm),
        grid=(t // tm,),
        in_specs=[row(D_MODEL), row(LANES), row(D_MODEL),
                  pl.BlockSpec((1, 6, D_MODEL), lambda i: (i // tiles_per_mod, 0, 0)),
                  full(final_g), full(sg), full(su), full(sd)]
        + [pl.BlockSpec(memory_space=pl.ANY)] * N_PIECES,
        out_specs=row(D_MODEL),
        out_shape=jax.ShapeDtypeStruct((t, D_MODEL), F32),
        scratch_shapes=[pltpu.VMEM((2, N_PIECES, TOP_K, tm, LANES), I32),
                        pltpu.SemaphoreType.DMA((2, N_PIECES))],
        compiler_params=_cparams("arbitrary"),
        name="moe_out",
    )(h2, wt, x1, mods, final_g, sg, su, sd, *y_pieces)


def _trunk(x, mods, s0f, s0b, w, expert_w, *, batch, seq_len, on_grid):
    t = batch * seq_len
    tokens_per_mod = t // mods.shape[0]
    cos_t, sin_t = _rope_tables(seq_len)
    q, k, v, gsw, up, ga, gb = _inproj(x, mods, w["norm1_g"], w["w_in"], cos_t, sin_t,
                                       tokens_per_mod=tokens_per_mod, seq_len=seq_len,
                                       on_grid=on_grid, tm=256)
    z, s_f, s_b = _retention(q, k, v, gsw, w["dec"], s0f, s0b, batch=batch, seq_len=seq_len)
    p = _pool(up, w["pool_w"], w["pool_scale"], batch=batch, seq_len=seq_len, on_grid=on_grid)
    x1, h2, h2_pieces = _merge(x, z, p, ga, gb, mods, w["norm2_g"], w["w_br_ret"], w["w_br_pool"],
                               w["w_out"], tokens_per_mod=tokens_per_mod, tm=512)

    group_tile = _group_tile(t)
    n_rows = t * TOP_K + N_EXPERTS * group_tile
    idx, rank, wt, counts = _route(h2, w["router_wt"], w["router_bias"], tm=1024)
    min_tiles = int(expert_w[0].dtype != BF16)
    pos, tile_expert, n_used = _plan(idx, rank, counts, n_tiles=n_rows // group_tile, tf=2048,
                                     group_tile=group_tile, min_tiles=min_tiles)
    x_sorted = _sc_dispatch(h2_pieces, pos, n_rows=n_rows)
    y_sorted, expert_w = _experts(x_sorted, tile_expert.reshape(-1), n_used.reshape(-1), *expert_w,
                                  group_tile=group_tile)
    y_tok = _sc_collect(y_sorted, pos, n_tokens=t)
    y = _moe_out(h2, wt, x1, mods, w["final_g"], w["sh_w_gate"], w["sh_w_up"], w["sh_w_down"], y_tok,
                 tokens_per_mod=tokens_per_mod, tm=512)
    return y, s_f, s_b, expert_w


def kernel(x_prompt, x_sample, state_ret_fwd, state_ret_bwd, c, c_ctx, ada_w, ada_b, norm1_g, norm2_g, w_in,
           ret_decay_fwd, ret_decay_bwd, w_br_ret, pool_w, pool_scale, w_br_pool, w_out, router_w, router_bias,
           exp_w_gate, exp_w_up, exp_w_down, sh_w_gate, sh_w_up, sh_w_down, final_norm_g):
    n_req, seq, d = x_prompt.shape
    n_dec, dec_seq, _ = x_sample.shape
    depth = ada_w.shape[0]
    assert depth == 1 and d == D_MODEL

    xc = x_prompt.reshape(n_req * seq, d)
    xs = x_sample.reshape(n_dec * dec_seq, d)
    zero_state = jnp.zeros((n_req, RET_HEADS, RET_DK, RET_DV), F32)
    new_f, new_b = [], []
    for l in range(depth):
        c_rows = jnp.concatenate([c_ctx[None, :], c, jnp.zeros((8 - 1 - n_dec, d), F32)], axis=0)
        mods = _ada(c_rows, ada_w[l], ada_b[l]).reshape(8, 6, d)
        pad_rows = LANES - N_EXPERTS
        w = dict(
            norm1_g=norm1_g[l].reshape(1, d), norm2_g=norm2_g[l].reshape(1, d),
            final_g=final_norm_g.reshape(1, d),
            w_in=w_in[l].astype(BF16),
            dec=jnp.stack([ret_decay_fwd[l], ret_decay_bwd[l]]).astype(F32),
            w_br_ret=w_br_ret[l].astype(BF16), pool_w=pool_w[l].astype(BF16),
            pool_scale=pool_scale[l].reshape(1, POOL_W), w_br_pool=w_br_pool[l].astype(BF16),
            w_out=w_out[l].astype(BF16),
            router_wt=jnp.pad(router_w[l].T, ((0, pad_rows), (0, 0))).astype(BF16),
            router_bias=jnp.pad(router_bias[l].astype(F32).reshape(N_EXPERTS, 1), ((0, pad_rows), (0, 0))),
            sh_w_gate=sh_w_gate[l].astype(BF16),
            sh_w_up=sh_w_up[l].astype(BF16), sh_w_down=sh_w_down[l].astype(BF16),
        )
        expert_w = (exp_w_gate[l], exp_w_up[l], exp_w_down[l])
        xs, _, _, expert_w = _trunk(xs, mods[1:1 + n_dec], state_ret_fwd[:, l].astype(F32),
                                    state_ret_bwd[:, l].astype(F32), w, expert_w,
                                    batch=n_dec, seq_len=dec_seq, on_grid=True)
        xc, s_f, s_b, _ = _trunk(xc, mods[0:1], zero_state, zero_state, w, expert_w,
                                 batch=n_req, seq_len=seq, on_grid=False)
        new_f.append(s_f)
        new_b.append(s_b)
    y_prompt = xc.reshape(n_req, seq, d)
    y_sample = xs.reshape(n_dec, dec_seq, d)
    return (y_prompt, y_sample, jnp.stack(new_f, axis=1).astype(x_prompt.dtype),
            jnp.stack(new_b, axis=1).astype(x_prompt.dtype))
```

```python
import functools

import numpy as np
import jax
import jax.numpy as jnp
from jax import lax
from jax.experimental import pallas as pl
from jax.experimental.pallas import tpu as pltpu
from jax.experimental.pallas import tpu_sc as plsc

D_MODEL = 1024
GRID_W = 64
RET_HEADS = 4
RET_DK = 128
RET_DV = 256
RET_QK_W = RET_HEADS * RET_DK
RET_V_W = RET_HEADS * RET_DV
RET_CHUNK = 128
ROPE_BASE = 10000.0
POOL_GROUPS = 4
POOL_CH = 128
POOL_W = POOL_GROUPS * POOL_CH
POOL_WINDOWS = (2, 4, 8, 16)
N_EXPERTS = 64
TOP_K = 8
N_EXPERT_GROUPS = 8
GROUP_SIZE = N_EXPERTS // N_EXPERT_GROUPS
TOPK_GROUPS = 4
D_EXPERT = 256
ROUTED_SCALE = 2.5
EPS = 1e-6
IN_SIZES = (RET_QK_W, RET_QK_W, RET_V_W, RET_V_W, POOL_W, D_MODEL, D_MODEL)
IN_OFFS = tuple(sum(IN_SIZES[:i]) for i in range(len(IN_SIZES) + 1))
IN_W = IN_OFFS[-1]

LANES = 128
VMEM_LIMIT = 56 << 20
N_PIECES = D_MODEL // 2 // LANES
MXU_DIM = 256
SC_CHUNK = 128

F32 = jnp.float32
BF16 = jnp.bfloat16
I32 = jnp.int32
U32 = jnp.uint32


def _cparams(*sem):
    return pltpu.CompilerParams(dimension_semantics=sem, vmem_limit_bytes=VMEM_LIMIT)


def _dot(a, b):
    return jnp.dot(a, b, preferred_element_type=F32)


def _silu(x):
    return x * jax.nn.sigmoid(x)


def _rms_mod(x, g, scale, shift):
    y = x * lax.rsqrt(jnp.mean(x * x, axis=-1, keepdims=True) + EPS)
    return (y * g) * (1.0 + scale) + shift


def _ada_kernel(c_ref, w_ref, b_ref, o_ref):
    c = c_ref[...]
    o_ref[...] = jnp.dot(_silu(c), w_ref[...], preferred_element_type=F32,
                         precision=lax.Precision.HIGHEST) + b_ref[...]


def _ada(c_rows, ada_w, ada_b):
    r = c_rows.shape[0]
    n = ada_w.shape[1]
    tn = D_MODEL
    return pl.pallas_call(
        _ada_kernel,
        grid=(n // tn,),
        in_specs=[pl.BlockSpec((r, D_MODEL), lambda j: (0, 0)),
                  pl.BlockSpec((D_MODEL, tn), lambda j: (0, j)),
                  pl.BlockSpec((1, tn), lambda j: (0, j))],
        out_specs=pl.BlockSpec((r, tn), lambda j: (0, j)),
        out_shape=jax.ShapeDtypeStruct((r, n), F32),
        compiler_params=_cparams("parallel"),
        name="ada_mod",
    )(c_rows, ada_w, ada_b.reshape(1, n))


def _inproj_kernel(x_ref, mod_ref, g_ref, w_ref, cos_ref, sin_ref,
                   q_ref, k_ref, v_ref, gsw_ref, up_ref, ga_ref, gb_ref, *, on_grid):
    x = x_ref[...]
    h = _rms_mod(x, g_ref[...], mod_ref[0, 1:2, :], mod_ref[0, 0:1, :]).astype(BF16)

    def seg(i):
        return _dot(h, w_ref[:, IN_OFFS[i]:IN_OFFS[i + 1]])

    q = seg(0)
    k = seg(1)
    if on_grid:
        cos = jnp.concatenate([cos_ref[...]] * RET_HEADS, axis=1)
        sin = jnp.concatenate([sin_ref[...]] * RET_HEADS, axis=1)
        lane = lax.broadcasted_iota(jnp.int32, q.shape, 1)
        first = (lane & 63) < 32

        def rope(a):
            up = pltpu.roll(a, RET_QK_W - 32, axis=1)
            dn = pltpu.roll(a, 32, axis=1)
            return a * cos + jnp.where(first, up, dn) * sin

        q = rope(q)
        k = rope(k)
    q_ref[...] = q.astype(BF16)
    k_ref[...] = (k * (RET_DK ** -0.5)).astype(BF16)
    v_ref[...] = seg(2).astype(BF16)
    gsw_ref[...] = seg(3).astype(BF16)
    up_ref[...] = seg(4).astype(BF16)
    ga_ref[...] = seg(5).astype(BF16)
    gb_ref[...] = seg(6).astype(BF16)


def _inproj(x, mods, norm_g, w_in, cos_t, sin_t, *, tokens_per_mod, seq_len, on_grid, tm):
    t = x.shape[0]
    tiles_per_mod = tokens_per_mod // tm
    tiles_per_seq = seq_len // tm
    widths = IN_SIZES
    out_shape = [jax.ShapeDtypeStruct((t, w), BF16) for w in widths]
    out_specs = [pl.BlockSpec((tm, w), lambda i: (i, 0)) for w in widths]
    return pl.pallas_call(
        functools.partial(_inproj_kernel, on_grid=on_grid),
        grid=(t // tm,),
        in_specs=[pl.BlockSpec((tm, D_MODEL), lambda i: (i, 0)),
                  pl.BlockSpec((1, 6, D_MODEL), lambda i: (i // tiles_per_mod, 0, 0)),
                  pl.BlockSpec((1, D_MODEL), lambda i: (0, 0)),
                  pl.BlockSpec((D_MODEL, IN_W), lambda i: (0, 0)),
                  pl.BlockSpec((tm, RET_DK), lambda i: (i % tiles_per_seq, 0)),
                  pl.BlockSpec((tm, RET_DK), lambda i: (i % tiles_per_seq, 0))],
        out_specs=out_specs,
        out_shape=out_shape,
        compiler_params=_cparams("parallel"),
        name="inproj_grid" if on_grid else "inproj_seq",
    )(x, mods, norm_g, w_in, cos_t, sin_t)


def _rope_tables(seq_len):
    t = np.arange(seq_len)
    row = (t // GRID_W).astype(np.float32)
    col = (t % GRID_W).astype(np.float32)
    m = RET_DK // 4
    inv = (np.float32(ROPE_BASE) ** (-np.arange(m, dtype=np.float32) / np.float32(m))).astype(np.float32)
    ar = row[:, None] * inv
    ac = col[:, None] * inv
    cos = np.concatenate([np.cos(ar), np.cos(ar), np.cos(ac), np.cos(ac)], axis=1)
    sin = np.concatenate([-np.sin(ar), np.sin(ar), -np.sin(ac), np.sin(ac)], axis=1)
    return jnp.asarray(cos, F32), jnp.asarray(sin, F32)


RET_HEADS_PER_STEP = 2


def _ret_kernel(dec_ref, q_ref, k_ref, v_ref, g_ref, s0f_ref, s0b_ref,
                z_ref, sf_ref, sb_ref, oacc_ref, kt_ref, *, n_chunks):
    c = RET_CHUNK
    heads = RET_HEADS_PER_STEP
    half = n_chunks // 2
    ii = lax.broadcasted_iota(I32, (c, c), 0)
    jj = lax.broadcasted_iota(I32, (c, c), 1)
    ik = lax.broadcasted_iota(I32, (c, RET_DK), 0).astype(F32)
    jk = lax.broadcasted_iota(I32, (RET_DK, c), 1).astype(F32)

    def log_gamma(d, shape):
        return jnp.log1p(-jnp.exp2(-jnp.full(shape, d, F32)))

    consts = {}
    for hh in range(heads):
        h = pl.program_id(1) * heads + hh
        dec_f = dec_ref[0, h]
        dec_b = dec_ref[1, h]
        rel = (ii - jj).astype(F32)
        consts[hh, "f"] = (
            jnp.where(rel >= 0, jnp.exp(log_gamma(dec_f, (c, c)) * jnp.maximum(rel, 0.0)), 0.0),
            jnp.exp(log_gamma(dec_f, (c, RET_DK)) * (ik + 1.0)),
            jnp.exp(log_gamma(dec_f, (RET_DK, c)) * (c - 1.0 - jk)),
            jnp.exp(log_gamma(dec_f, (RET_DK, RET_DV)) * c))
        consts[hh, "b"] = (
            jnp.where(rel <= 0, jnp.exp(log_gamma(dec_b, (c, c)) * jnp.maximum(-rel, 0.0)), 0.0),
            jnp.exp(log_gamma(dec_b, (c, RET_DK)) * (c - ik)),
            jnp.exp(log_gamma(dec_b, (RET_DK, c)) * jk),
            jnp.exp(log_gamma(dec_b, (RET_DK, RET_DV)) * c))

    sf_ref[...] = s0f_ref[...]
    sb_ref[...] = s0b_ref[...]

    def transpose_keys(ci, carry):
        r = pl.ds(pl.multiple_of(ci * c, c), c)
        for hh in range(heads):
            kt_ref[hh, ci] = k_ref[r, hh * RET_DK:(hh + 1) * RET_DK].T
        return carry

    lax.fori_loop(0, n_chunks, transpose_keys, 0)

    def scores(ci, hh, direction):
        r = pl.ds(pl.multiple_of(ci * c, c), c)
        kcols = slice(hh * RET_DK, (hh + 1) * RET_DK)
        qc = q_ref[r, kcols]
        sc = lax.dot_general(qc, k_ref[r, kcols], (((1,), (1,)), ((), ())), preferred_element_type=F32)
        return ci, hh, direction, r, qc, sc

    def advance(job):
        ci, hh, direction, r, qc, sc = job
        dmask, qdec, kdec, cdec = consts[hh, direction]
        s_ref = sf_ref if direction == "f" else sb_ref
        vc = v_ref[r, hh * RET_DV:(hh + 1) * RET_DV]
        s = s_ref[hh]
        lhs = jnp.concatenate([(sc * dmask).astype(BF16), (qc.astype(F32) * qdec).astype(BF16)], axis=1)
        o = _dot(lhs, jnp.concatenate([vc, s.astype(BF16)], axis=0))
        kd_t = (kt_ref[hh, ci].astype(F32) * kdec).astype(BF16)
        s_ref[hh] = s * cdec + _dot(kd_t, vc)
        return o

    def emit(job, o, second):
        _, hh, _, r, _, _ = job
        vcols = slice(hh * RET_DV, (hh + 1) * RET_DV)
        if not second:
            oacc_ref[hh, r, :] = o
        else:
            o = o + oacc_ref[hh, r, :]
            o = o * lax.rsqrt(jnp.mean(o * o, axis=-1, keepdims=True) + EPS)
            g = g_ref[r, vcols].astype(F32)
            z_ref[r, vcols] = (_silu(g) * o).astype(BF16)

    def body(second):
        def run(t, carry):
            jobs = [scores(ci, hh, d) for hh in range(heads)
                    for ci, d in ((t, "f"), (n_chunks - 1 - t, "b"))]
            outs = [advance(job) for job in jobs]
            for job, o in zip(jobs, outs):
                emit(job, o, second)
            return carry
        return run

    lax.fori_loop(0, half, body(False), 0, unroll=4 if half % 4 == 0 else 1)
    lax.fori_loop(half, n_chunks, body(True), 0, unroll=2 if half % 2 == 0 else 1)


def _retention(q, k, v, gsw, dec, s0f, s0b, *, batch, seq_len):
    n_chunks = seq_len // RET_CHUNK
    assert n_chunks % 2 == 0
    heads = RET_HEADS_PER_STEP
    t = batch * seq_len
    st_spec = pl.BlockSpec((None, heads, RET_DK, RET_DV), lambda b, h: (b, h, 0, 0))
    st_shape = jax.ShapeDtypeStruct((batch, RET_HEADS, RET_DK, RET_DV), F32)
    kspec = pl.BlockSpec((seq_len, heads * RET_DK), lambda b, h: (b, h))
    vspec = pl.BlockSpec((seq_len, heads * RET_DV), lambda b, h: (b, h))
    return pl.pallas_call(
        functools.partial(_ret_kernel, n_chunks=n_chunks),
        grid=(batch, RET_HEADS // heads),
        in_specs=[pl.BlockSpec(memory_space=pltpu.SMEM), kspec, kspec, vspec, vspec, st_spec, st_spec],
        out_specs=[vspec, st_spec, st_spec],
        out_shape=[jax.ShapeDtypeStruct((t, RET_V_W), BF16), st_shape, st_shape],
        scratch_shapes=[pltpu.VMEM((heads, seq_len, RET_DV), F32),
                        pltpu.VMEM((heads, n_chunks, RET_DK, RET_CHUNK), BF16)],
        compiler_params=_cparams("parallel", "parallel"),
        name=f"retention_l{seq_len}",
    )(dec, q, k, v, gsw, s0f, s0b)


def _pool_kernel(u_ref, w_ref, sc_ref, o_ref, *, seq_len, on_grid):
    tok = lax.broadcasted_iota(jnp.int32, (seq_len, POOL_CH), 0)

    def shift(a, s, stride, pos, width):
        y = pltpu.roll(a, (-s * stride) % seq_len, axis=0)
        ok = (pos < width - s) if s > 0 else (pos >= -s)
        return jnp.where(ok, y, 0.0)

    def box_mean(a, window, stride, pos, width):
        half = window // 2
        fw = a
        bw = shift(a, -1, stride, pos, width)
        m = 1
        while m < half:
            fw = fw + shift(fw, m, stride, pos, width)
            bw = bw + shift(bw, -m, stride, pos, width)
            m *= 2
        cnt = jnp.minimum(pos + half, width) - jnp.maximum(pos - half, 0)
        return (fw + bw) / cnt.astype(F32)

    for g, window in enumerate(POOL_WINDOWS):
        cols = slice(g * POOL_CH, (g + 1) * POOL_CH)
        ug = u_ref[:, cols].astype(F32)
        if on_grid:
            pooled = box_mean(ug, window, 1, tok & (GRID_W - 1), GRID_W)
            pooled = box_mean(pooled, window, GRID_W, tok >> 6, seq_len // GRID_W)
        else:
            pooled = box_mean(ug, window, 1, tok, seq_len)
        d = (pooled - ug).astype(BF16)
        o_ref[:, cols] = (_dot(d, w_ref[g]) * sc_ref[:, cols]).astype(BF16)


def _pool(u, pool_w, pool_scale, *, batch, seq_len, on_grid):
    t = batch * seq_len
    return pl.pallas_call(
        functools.partial(_pool_kernel, seq_len=seq_len, on_grid=on_grid),
        grid=(batch,),
        in_specs=[pl.BlockSpec((seq_len, POOL_W), lambda b: (b, 0)),
                  pl.BlockSpec((POOL_GROUPS, POOL_CH, POOL_CH), lambda b: (0, 0, 0)),
                  pl.BlockSpec((1, POOL_W), lambda b: (0, 0))],
        out_specs=pl.BlockSpec((seq_len, POOL_W), lambda b: (b, 0)),
        out_shape=jax.ShapeDtypeStruct((t, POOL_W), BF16),
        compiler_params=_cparams("parallel"),
        name=f"pool_l{seq_len}",
    )(u, pool_w, pool_scale)


def _pack_rows(x):
    half = D_MODEL // 2
    lo = lax.bitcast_convert_type(x[:, :half].astype(BF16).astype(F32), U32) >> 16
    hi = lax.bitcast_convert_type(x[:, half:].astype(BF16).astype(F32), U32) & jnp.uint32(0xFFFF0000)
    word = lax.bitcast_convert_type(hi | lo, I32)
    return [word[:, c * LANES:(c + 1) * LANES] for c in range(N_PIECES)]


def _unpack_rows(pieces):
    words = [lax.bitcast_convert_type(p, U32) for p in pieces]
    lo = [lax.bitcast_convert_type(w << 16, F32) for w in words]
    hi = [lax.bitcast_convert_type(w & jnp.uint32(0xFFFF0000), F32) for w in words]
    return lo, hi


def _merge_kernel(x_ref, z_ref, p_ref, ga_ref, gb_ref, mod_ref, g2_ref, wr_ref, wp_ref, wo_ref,
                  x1_ref, *piece_refs):
    y_ret = _dot(z_ref[...], wr_ref[...])
    y_pool = _dot(p_ref[...], wp_ref[...])
    merged = (jax.nn.sigmoid(ga_ref[...].astype(F32)) * y_ret
              + jax.nn.sigmoid(gb_ref[...].astype(F32)) * y_pool)
    x1 = x_ref[...] + mod_ref[0, 2:3, :] * _dot(merged.astype(BF16), wo_ref[...])
    x1_ref[...] = x1
    h2 = _rms_mod(x1, g2_ref[...], mod_ref[0, 4:5, :], mod_ref[0, 3:4, :])
    for ref, piece in zip(piece_refs, _pack_rows(h2)):
        ref[...] = piece


def _merge(x, z, p, ga, gb, mods, norm2_g, w_br_ret, w_br_pool, w_out, *, tokens_per_mod, tm):
    t = x.shape[0]
    tiles_per_mod = tokens_per_mod // tm
    row = lambda w: pl.BlockSpec((tm, w), lambda i: (i, 0))
    full = lambda a: pl.BlockSpec(a.shape, lambda i: (0,) * a.ndim)
    outs = pl.pallas_call(
        _merge_kernel,
        grid=(t // tm,),
        in_specs=[row(D_MODEL), row(RET_V_W), row(POOL_W), row(D_MODEL), row(D_MODEL),
                  pl.BlockSpec((1, 6, D_MODEL), lambda i: (i // tiles_per_mod, 0, 0)),
                  full(norm2_g), full(w_br_ret), full(w_br_pool), full(w_out)],
        out_specs=[row(D_MODEL)] + [row(LANES)] * N_PIECES,
        out_shape=[jax.ShapeDtypeStruct((t, D_MODEL), F32)] + [jax.ShapeDtypeStruct((t, LANES), I32)] * N_PIECES,
        compiler_params=_cparams("parallel"),
        name="merge",
    )(x, z, p, ga, gb, mods, norm2_g, w_br_ret, w_br_pool, w_out)
    return outs[0], outs[1:]


def _route_kernel(*refs):
    h_refs = refs[:N_PIECES]
    rw_ref, bias_ref, idx_ref, rank_ref, wt_ref, cnt_ref, carry_ref = refs[N_PIECES:]
    e = N_EXPERTS
    tm = h_refs[0].shape[0]
    neg = -jnp.inf

    @pl.when(pl.program_id(0) == 0)
    def _():
        carry_ref[...] = jnp.zeros(carry_ref.shape, F32)

    lo, hi = _unpack_rows([r[...] for r in h_refs])
    h = jnp.concatenate(lo + hi, axis=1).astype(BF16)
    logits = lax.dot_general(rw_ref[...], h, (((1,), (1,)), ((), ())), preferred_element_type=F32)[:e]
    scores = jax.nn.sigmoid(logits)
    sel = scores + bias_ref[:e, 0:1]
    e_idx = lax.broadcasted_iota(I32, (e, tm), 0)

    grp = sel.reshape(N_EXPERT_GROUPS, GROUP_SIZE, tm)
    m_idx = lax.broadcasted_iota(I32, grp.shape, 1)
    m1 = jnp.max(grp, axis=1, keepdims=True)
    first = jnp.min(jnp.where(grp == m1, m_idx, GROUP_SIZE), axis=1, keepdims=True)
    m2 = jnp.max(jnp.where(m_idx == first, neg, grp), axis=1, keepdims=True)
    gscore = (m1 + m2).reshape(N_EXPERT_GROUPS, tm)

    g_idx = lax.broadcasted_iota(I32, gscore.shape, 0)
    grank = jnp.zeros(gscore.shape, I32)
    for g in range(N_EXPERT_GROUPS):
        other = gscore[g:g + 1, :]
        beats = jnp.where(other > gscore, 1, jnp.where(other == gscore, (g_idx > g).astype(I32), 0))
        grank = grank + beats
    gkeep = (grank < TOPK_GROUPS).astype(F32)
    ekeep = jnp.broadcast_to(gkeep.reshape(N_EXPERT_GROUPS, 1, tm), grp.shape).reshape(e, tm)
    masked = jnp.where(ekeep > 0, sel, neg)

    chosen = jnp.zeros((e, tm), F32)
    picks, hits = [], []
    for _ in range(TOP_K):
        m = jnp.max(masked, axis=0, keepdims=True)
        pick = jnp.min(jnp.where(masked == m, e_idx, e), axis=0, keepdims=True)
        hit = e_idx == pick
        chosen = jnp.where(hit, 1.0, chosen)
        masked = jnp.where(hit, neg, masked)
        picks.append(pick)
        hits.append(hit)

    w = scores * chosen
    comb = w / jnp.sum(w, axis=0, keepdims=True) * ROUTED_SCALE

    t_row = lax.broadcasted_iota(I32, (tm, tm), 0)
    t_col = lax.broadcasted_iota(I32, (tm, tm), 1)
    before = (t_row < t_col).astype(BF16)
    rankmat = _dot(chosen.astype(BF16), before) + carry_ref[:e, 0:1]
    carry_ref[:e, :] = carry_ref[:e, :] + jnp.sum(chosen, axis=1, keepdims=True)
    cnt_ref[...] = carry_ref[...]

    idx_ref[...] = jnp.concatenate(picks, axis=0)
    rank_ref[...] = jnp.concatenate(
        [jnp.sum(jnp.where(h, rankmat, 0.0), axis=0, keepdims=True) for h in hits], axis=0).astype(I32)
    w_rows = [jnp.sum(jnp.where(h, comb, 0.0), axis=0, keepdims=True) for h in hits]
    wt_ref[...] = jnp.concatenate(w_rows + [jnp.zeros((LANES - TOP_K, tm), F32)], axis=0).T


def _route(h2_pieces, router_wt, bias_col, *, tm):
    t = h2_pieces[0].shape[0]
    krow = pl.BlockSpec((TOP_K, tm), lambda i: (0, i))
    return pl.pallas_call(
        _route_kernel,
        grid=(t // tm,),
        in_specs=[pl.BlockSpec((tm, LANES), lambda i: (i, 0))] * N_PIECES
        + [pl.BlockSpec((LANES, D_MODEL), lambda i: (0, 0)), pl.BlockSpec((LANES, 1), lambda i: (0, 0))],
        out_specs=[krow, krow, pl.BlockSpec((tm, LANES), lambda i: (i, 0)),
                   pl.BlockSpec((LANES, LANES), lambda i: (0, 0))],
        out_shape=[jax.ShapeDtypeStruct((TOP_K, t), I32), jax.ShapeDtypeStruct((TOP_K, t), I32),
                   jax.ShapeDtypeStruct((t, LANES), F32), jax.ShapeDtypeStruct((LANES, LANES), F32)],
        scratch_shapes=[pltpu.VMEM((LANES, LANES), F32)],
        compiler_params=_cparams("arbitrary"),
        name="route",
    )(*h2_pieces, router_wt, bias_col)


def _plan_kernel(idx_ref, rank_ref, cnt_ref, pos_ref, te_ref, nu_ref, *, group_tile, min_tiles):
    tf = idx_ref.shape[1]
    nt = te_ref.shape[1]
    cnt = cnt_ref[...].astype(I32)
    tiles = jnp.maximum((cnt + (group_tile - 1)) // group_tile, min_tiles)
    e_sub = lax.broadcasted_iota(I32, (LANES, LANES), 0)
    padded = jnp.where(e_sub < N_EXPERTS, tiles * group_tile, 0).astype(F32)
    e_lane = lax.broadcasted_iota(I32, (LANES, LANES), 1)
    base = jnp.sum(jnp.where(e_lane < e_sub, padded.T, 0.0), axis=1, keepdims=True)
    end = base + padded[:, 0:1]

    idx = idx_ref[...]
    start = jnp.zeros(idx.shape, F32)
    for e in range(N_EXPERTS):
        start = jnp.where(idx == e, base[e:e + 1, 0:1], start)
    pos = start.astype(I32) + rank_ref[...]
    for j in range(tf // SC_CHUNK):
        pos_ref[j] = pos[:, j * SC_CHUNK:(j + 1) * SC_CHUNK]

    tile_start = (lax.broadcasted_iota(I32, (N_EXPERTS, nt), 1) * group_tile).astype(F32)
    done = jnp.sum(jnp.where(end[:N_EXPERTS] <= tile_start, 1.0, 0.0), axis=0, keepdims=True)
    te_ref[...] = jnp.minimum(done, N_EXPERTS - 1.0).astype(I32)
    total = jnp.sum(padded[:, 0:1], axis=0, keepdims=True)
    nu_ref[...] = jnp.broadcast_to(total * (1.0 / group_tile), nu_ref.shape).astype(I32)


def _plan(idx, rank, counts, *, n_tiles, tf, group_tile, min_tiles):
    t = idx.shape[1]
    nt_pad = -(-n_tiles // LANES) * LANES
    krow = pl.BlockSpec((TOP_K, tf), lambda i: (0, i))
    return pl.pallas_call(
        functools.partial(_plan_kernel, group_tile=group_tile, min_tiles=min_tiles),
        grid=(t // tf,),
        in_specs=[krow, krow, pl.BlockSpec((LANES, LANES), lambda i: (0, 0))],
        out_specs=[pl.BlockSpec((tf // SC_CHUNK, TOP_K, SC_CHUNK), lambda i: (i, 0, 0)),
                   pl.BlockSpec((1, nt_pad), lambda i: (0, 0)),
                   pl.BlockSpec((1, LANES), lambda i: (0, 0))],
        out_shape=[jax.ShapeDtypeStruct((t // SC_CHUNK, TOP_K, SC_CHUNK), I32),
                   jax.ShapeDtypeStruct((1, nt_pad), I32), jax.ShapeDtypeStruct((1, LANES), I32)],
        compiler_params=_cparams("arbitrary"),
        name="moe_plan",
    )(idx, rank, counts)


def _sc_mesh_info():
    info = plsc.get_sparse_core_info()
    mesh = plsc.VectorSubcoreMesh(core_axis_name="c", subcore_axis_name="s")
    return mesh, info.num_cores, info.num_cores * info.num_subcores


def _sc_dispatch(pieces, pos, *, n_rows):
    t = pieces[0].shape[0]
    mesh, n_cores, n_workers = _sc_mesh_info()
    per_w = t // SC_CHUNK // n_workers

    @functools.partial(
        pl.kernel, mesh=mesh,
        out_type=[jax.ShapeDtypeStruct((n_rows, LANES), I32)] * N_PIECES,
        scratch_types=[pltpu.VMEM((TOP_K, SC_CHUNK), I32),
                       pltpu.VMEM((N_PIECES, SC_CHUNK, LANES), I32),
                       pltpu.SemaphoreType.DMA((N_PIECES,)),
                       pltpu.SemaphoreType.DMA],
        name="sc_dispatch",
    )
    def run(*refs):
        src = refs[:N_PIECES]
        pos_hbm = refs[N_PIECES]
        dst = refs[N_PIECES + 1:2 * N_PIECES + 1]
        idx_v, rows_v, load_sem, put_sem = refs[2 * N_PIECES + 1:]
        wid = lax.axis_index("s") * n_cores + lax.axis_index("c")

        @pl.loop(0, per_w)
        def _(j):
            ch = wid * per_w + j
            t0 = pl.multiple_of(ch * SC_CHUNK, SC_CHUNK)
            loads = [pltpu.make_async_copy(src[c].at[pl.ds(t0, SC_CHUNK)], rows_v.at[c], load_sem.at[c])
                     for c in range(N_PIECES)]
            for ld in loads:
                ld.start()
            pltpu.sync_copy(pos_hbm.at[ch], idx_v)
            puts = []
            for c in range(N_PIECES):
                loads[c].wait()
                for k in range(TOP_K):
                    puts.append(pltpu.make_async_copy(rows_v.at[c], dst[c].at[idx_v.at[k]], put_sem))
                    puts[-1].start()
            for cp in puts:
                cp.wait()

    return run(*pieces, pos)


def _sc_collect(pieces, pos, *, n_tokens):
    mesh, n_cores, n_workers = _sc_mesh_info()
    per_w = n_tokens // SC_CHUNK // n_workers
    n_buf = 4
    lag = n_buf // 2
    units = [(k, c) for k in range(TOP_K) for c in range(N_PIECES)]

    @functools.partial(
        pl.kernel, mesh=mesh,
        out_type=[jax.ShapeDtypeStruct((TOP_K, n_tokens, LANES), I32)] * N_PIECES,
        scratch_types=[pltpu.VMEM((TOP_K, SC_CHUNK), I32),
                       pltpu.VMEM((n_buf, SC_CHUNK, LANES), I32),
                       pltpu.SemaphoreType.DMA((n_buf,)),
                       pltpu.SemaphoreType.DMA((n_buf,))],
        name="sc_collect",
    )
    def run(*refs):
        src = refs[:N_PIECES]
        pos_hbm = refs[N_PIECES]
        dst = refs[N_PIECES + 1:2 * N_PIECES + 1]
        idx_v, buf, get_sem, put_sem = refs[2 * N_PIECES + 1:]
        wid = lax.axis_index("s") * n_cores + lax.axis_index("c")

        @pl.loop(0, per_w)
        def _(j):
            ch = wid * per_w + j
            t0 = pl.multiple_of(ch * SC_CHUNK, SC_CHUNK)
            pltpu.sync_copy(pos_hbm.at[ch], idx_v)

            def get(u):
                k, c = units[u]
                return pltpu.make_async_copy(src[c].at[idx_v.at[k]], buf.at[u % n_buf], get_sem.at[u % n_buf])

            def put(u):
                k, c = units[u]
                return pltpu.make_async_copy(buf.at[u % n_buf], dst[c].at[k, pl.ds(t0, SC_CHUNK)],
                                             put_sem.at[u % n_buf])

            n = len(units)
            for u in range(n + lag):
                if u < n:
                    if u >= n_buf:
                        put(u - n_buf).wait()
                    get(u).start()
                if 0 <= u - lag < n:
                    get(u - lag).wait()
                    put(u - lag).start()
            for u in range(n - n_buf, n):
                put(u).wait()

    return run(*pieces, pos)


def _group_tile(n_tokens):
    per_expert = n_tokens * TOP_K // N_EXPERTS
    return max(MXU_DIM, min(4 * MXU_DIM, per_expert // MXU_DIM * MXU_DIM))


SECOND_DMA_QUEUE = 1


def _experts_kernel(te_ref, nu_ref, *refs, cast_weights, tile):
    x_hbm = refs[:N_PIECES]
    w_in = refs[N_PIECES:N_PIECES + 3]
    y_hbm = refs[N_PIECES + 3:2 * N_PIECES + 3]
    xbuf, xsem, ybuf, ysem = refs[-4:]
    i = pl.program_id(0)
    n_used = nu_ref[0]
    slot = lax.rem(i, 2)

    def x_copies(step, into):
        r = pl.ds(pl.multiple_of(step * tile, tile), tile)
        return [pltpu.make_async_copy(x_hbm[c].at[r], xbuf.at[into, c], xsem.at[into, c])
                for c in range(N_PIECES)]

    def y_copies(step, out_of):
        r = pl.ds(pl.multiple_of(step * tile, tile), tile)
        return [pltpu.make_async_copy(ybuf.at[out_of, c], y_hbm[c].at[r], ysem.at[out_of, c])
                for c in range(N_PIECES)]

    @pl.when(i == 0)
    def _():
        for cp in x_copies(0, 0):
            cp.start(priority=SECOND_DMA_QUEUE)

    @pl.when(i + 1 < n_used)
    def _():
        for cp in x_copies(i + 1, 1 - slot):
            cp.start(priority=SECOND_DMA_QUEUE)

    if cast_weights:
        w_bf = refs[2 * N_PIECES + 3:-4]
        last = n_used - 1
        cur = te_ref[jnp.minimum(i, last)]
        prev = te_ref[jnp.minimum(jnp.maximum(i - 1, 0), last)]

        @pl.when((i == 0) | (cur != prev))
        def _():
            for dst, src in zip(w_bf, w_in):
                dst[...] = src[...].astype(BF16)
    else:
        w_bf = w_in
    wg_ref, wu_ref, wd_ref = w_bf

    @pl.when(i < n_used)
    def _():
        for cp in x_copies(i, slot):
            cp.wait()
        subs = [slice(s * MXU_DIM, (s + 1) * MXU_DIM) for s in range(tile // MXU_DIM)]
        xs = []
        for rows in subs:
            lo, hi = _unpack_rows([xbuf[slot, c, rows, :] for c in range(N_PIECES)])
            xs.append(jnp.concatenate(lo + hi, axis=1).astype(BF16))
        gates = [(_dot(x, wg_ref[...]), _dot(x, wu_ref[...])) for x in xs]
        ys = [_dot((_silu(g) * u).astype(BF16), wd_ref[...]) for g, u in gates]
        for rows, y in zip(subs, ys):
            for c, piece in enumerate(_pack_rows(y)):
                ybuf[slot, c, rows, :] = piece

        @pl.when(i >= 1)
        def _():
            for cp in y_copies(i - 1, 1 - slot):
                cp.wait()

        for cp in y_copies(i, slot):
            cp.start(priority=SECOND_DMA_QUEUE)

        @pl.when(i == n_used - 1)
        def _():
            for cp in y_copies(i, slot):
                cp.wait()


def _experts(x_pieces, tile_expert, n_used, wg, wu, wd, *, group_tile):
    n_rows = x_pieces[0].shape[0]
    n_tiles = n_rows // group_tile
    cast_weights = wg.dtype != BF16

    def tile(i, te, nu):
        return jnp.minimum(i, nu[0] - 1)

    wspec = lambda a: pl.BlockSpec((None,) + a.shape[1:], lambda i, te, nu: (te[tile(i, te, nu)], 0, 0))
    w_specs = [wspec(wg), wspec(wu), wspec(wd)]
    y_shape = [jax.ShapeDtypeStruct((n_rows, LANES), I32)] * N_PIECES
    w_shape = [jax.ShapeDtypeStruct(a.shape, BF16) for a in (wg, wu, wd)]
    outs = pl.pallas_call(
        functools.partial(_experts_kernel, cast_weights=cast_weights, tile=group_tile),
        grid_spec=pltpu.PrefetchScalarGridSpec(
            num_scalar_prefetch=2,
            grid=(n_tiles,),
            in_specs=[pl.BlockSpec(memory_space=pl.ANY)] * N_PIECES + w_specs,
            out_specs=[pl.BlockSpec(memory_space=pl.ANY)] * N_PIECES + (w_specs if cast_weights else []),
            scratch_shapes=[pltpu.VMEM((2, N_PIECES, group_tile, LANES), I32),
                            pltpu.SemaphoreType.DMA((2, N_PIECES))] * 2),
        out_shape=y_shape + (w_shape if cast_weights else []),
        compiler_params=_cparams("arbitrary"),
        name="experts_cast" if cast_weights else "experts",
    )(tile_expert, n_used, *x_pieces, wg, wu, wd)
    return outs[:N_PIECES], (tuple(outs[N_PIECES:]) if cast_weights else (wg, wu, wd))


def _moe_out_kernel(wt_ref, x1_ref, mod_ref, fg_ref, sg_ref, su_ref, sd_ref, *refs, tm):
    h_refs = refs[:N_PIECES]
    y_hbm = refs[N_PIECES:2 * N_PIECES]
    out_ref = refs[2 * N_PIECES]
    ybuf, ysem = refs[2 * N_PIECES + 1:]
    i = pl.program_id(0)
    slot = lax.rem(i, 2)

    def y_copies(step, into):
        r = pl.ds(pl.multiple_of(step * tm, tm), tm)
        return [pltpu.make_async_copy(y_hbm[c].at[:, r, :], ybuf.at[into, c], ysem.at[into, c])
                for c in range(N_PIECES)]

    @pl.when(i == 0)
    def _():
        for cp in y_copies(0, 0):
            cp.start(priority=SECOND_DMA_QUEUE)

    @pl.when(i + 1 < pl.num_programs(0))
    def _():
        for cp in y_copies(i + 1, 1 - slot):
            cp.start(priority=SECOND_DMA_QUEUE)

    h_lo, h_hi = _unpack_rows([r[...] for r in h_refs])
    h = jnp.concatenate(h_lo + h_hi, axis=1).astype(BF16)
    hid = _silu(_dot(h, sg_ref[...])) * _dot(h, su_ref[...])
    shared = _dot(hid.astype(BF16), sd_ref[...])
    wt = wt_ref[...]
    for cp in y_copies(i, slot):
        cp.wait()
    lo_acc = [None] * N_PIECES
    hi_acc = [None] * N_PIECES
    for k in range(TOP_K):
        wk = wt[:, k:k + 1]
        lo, hi = _unpack_rows([ybuf[slot, c, k] for c in range(N_PIECES)])
        for c in range(N_PIECES):
            lo_acc[c] = wk * lo[c] if k == 0 else lo_acc[c] + wk * lo[c]
            hi_acc[c] = wk * hi[c] if k == 0 else hi_acc[c] + wk * hi[c]
    routed = jnp.concatenate(lo_acc + hi_acc, axis=1)
    x2 = x1_ref[...] + mod_ref[0, 5:6, :] * (shared + routed)
    out_ref[...] = x2 * lax.rsqrt(jnp.mean(x2 * x2, axis=-1, keepdims=True) + EPS) * fg_ref[...]


def _moe_out(h2_pieces, wt, x1, mods, final_g, sg, su, sd, y_pieces, *, tokens_per_mod, tm):
    t = x1.shape[0]
    tiles_per_mod = tokens_per_mod // tm
    row = lambda w: pl.BlockSpec((tm, w), lambda i: (i, 0))
    full = lambda a: pl.BlockSpec(a.shape, lambda i: (0,) * a.ndim)
    return pl.pallas_call(
        functools.partial(_moe_out_kernel, tm=tm),
        grid=(t // tm,),
        in_specs=[row(LANES), row(D_MODEL),
                  pl.BlockSpec((1, 6, D_MODEL), lambda i: (i // tiles_per_mod, 0, 0)),
                  full(final_g), full(sg), full(su), full(sd)]
        + [row(LANES)] * N_PIECES + [pl.BlockSpec(memory_space=pl.ANY)] * N_PIECES,
        out_specs=row(D_MODEL),
        out_shape=jax.ShapeDtypeStruct((t, D_MODEL), F32),
        scratch_shapes=[pltpu.VMEM((2, N_PIECES, TOP_K, tm, LANES), I32),
                        pltpu.SemaphoreType.DMA((2, N_PIECES))],
        compiler_params=_cparams("arbitrary"),
        name="moe_out",
    )(wt, x1, mods, final_g, sg, su, sd, *h2_pieces, *y_pieces)


def _trunk(x, mods, s0f, s0b, w, expert_w, *, batch, seq_len, on_grid):
    t = batch * seq_len
    tokens_per_mod = t // mods.shape[0]
    cos_t, sin_t = _rope_tables(seq_len)
    q, k, v, gsw, up, ga, gb = _inproj(x, mods, w["norm1_g"], w["w_in"], cos_t, sin_t,
                                       tokens_per_mod=tokens_per_mod, seq_len=seq_len,
                                       on_grid=on_grid, tm=256)
    z, s_f, s_b = _retention(q, k, v, gsw, w["dec"], s0f, s0b, batch=batch, seq_len=seq_len)
    p = _pool(up, w["pool_w"], w["pool_scale"], batch=batch, seq_len=seq_len, on_grid=on_grid)
    x1, h2_pieces = _merge(x, z, p, ga, gb, mods, w["norm2_g"], w["w_br_ret"], w["w_br_pool"],
                               w["w_out"], tokens_per_mod=tokens_per_mod, tm=512)

    group_tile = _group_tile(t)
    n_rows = t * TOP_K + N_EXPERTS * group_tile
    idx, rank, wt, counts = _route(h2_pieces, w["router_wt"], w["router_bias"], tm=1024)
    min_tiles = int(expert_w[0].dtype != BF16)
    pos, tile_expert, n_used = _plan(idx, rank, counts, n_tiles=n_rows // group_tile, tf=2048,
                                     group_tile=group_tile, min_tiles=min_tiles)
    x_sorted = _sc_dispatch(h2_pieces, pos, n_rows=n_rows)
    y_sorted, expert_w = _experts(x_sorted, tile_expert.reshape(-1), n_used.reshape(-1), *expert_w,
                                  group_tile=group_tile)
    y_tok = _sc_collect(y_sorted, pos, n_tokens=t)
    y = _moe_out(h2_pieces, wt, x1, mods, w["final_g"], w["sh_w_gate"], w["sh_w_up"], w["sh_w_down"], y_tok,
                 tokens_per_mod=tokens_per_mod, tm=512)
    return y, s_f, s_b, expert_w


def kernel(x_prompt, x_sample, state_ret_fwd, state_ret_bwd, c, c_ctx, ada_w, ada_b, norm1_g, norm2_g, w_in,
           ret_decay_fwd, ret_decay_bwd, w_br_ret, pool_w, pool_scale, w_br_pool, w_out, router_w, router_bias,
           exp_w_gate, exp_w_up, exp_w_down, sh_w_gate, sh_w_up, sh_w_down, final_norm_g):
    n_req, seq, d = x_prompt.shape
    n_dec, dec_seq, _ = x_sample.shape
    depth = ada_w.shape[0]
    assert depth == 1 and d == D_MODEL

    xc = x_prompt.reshape(n_req * seq, d)
    xs = x_sample.reshape(n_dec * dec_seq, d)
    zero_state = jnp.zeros((n_req, RET_HEADS, RET_DK, RET_DV), F32)
    new_f, new_b = [], []
    for l in range(depth):
        c_rows = jnp.concatenate([c_ctx[None, :], c, jnp.zeros((8 - 1 - n_dec, d), F32)], axis=0)
        mods = _ada(c_rows, ada_w[l], ada_b[l]).reshape(8, 6, d)
        pad_rows = LANES - N_EXPERTS
        w = dict(
            norm1_g=norm1_g[l].reshape(1, d), norm2_g=norm2_g[l].reshape(1, d),
            final_g=final_norm_g.reshape(1, d),
            w_in=w_in[l].astype(BF16),
            dec=jnp.stack([ret_decay_fwd[l], ret_decay_bwd[l]]).astype(F32),
            w_br_ret=w_br_ret[l].astype(BF16), pool_w=pool_w[l].astype(BF16),
            pool_scale=pool_scale[l].reshape(1, POOL_W), w_br_pool=w_br_pool[l].astype(BF16),
            w_out=w_out[l].astype(BF16),
            router_wt=jnp.pad(router_w[l].T, ((0, pad_rows), (0, 0))).astype(BF16),
            router_bias=jnp.pad(router_bias[l].astype(F32).reshape(N_EXPERTS, 1), ((0, pad_rows), (0, 0))),
            sh_w_gate=sh_w_gate[l].astype(BF16),
            sh_w_up=sh_w_up[l].astype(BF16), sh_w_down=sh_w_down[l].astype(BF16),
        )
        expert_w = (exp_w_gate[l], exp_w_up[l], exp_w_down[l])
        xs, _, _, expert_w = _trunk(xs, mods[1:1 + n_dec], state_ret_fwd[:, l].astype(F32),
                                    state_ret_bwd[:, l].astype(F32), w, expert_w,
                                    batch=n_dec, seq_len=dec_seq, on_grid=True)
        xc, s_f, s_b, _ = _trunk(xc, mods[0:1], zero_state, zero_state, w, expert_w,
                                 batch=n_req, seq_len=seq, on_grid=False)
        new_f.append(s_f)
        new_b.append(s_b)
    y_prompt = xc.reshape(n_req, seq, d)
    y_sample = xs.reshape(n_dec, dec_seq, d)
    return (y_prompt, y_sample, jnp.stack(new_f, axis=1).astype(x_prompt.dtype),
            jnp.stack(new_b, axis=1).astype(x_prompt.dtype))
```

```python
import functools

import numpy as np
import jax
import jax.numpy as jnp
from jax import lax
from jax.experimental import pallas as pl
from jax.experimental.pallas import tpu as pltpu
from jax.experimental.pallas import tpu_sc as plsc

D_MODEL = 1024
GRID_W = 64
RET_HEADS = 4
RET_DK = 128
RET_DV = 256
RET_QK_W = RET_HEADS * RET_DK
RET_V_W = RET_HEADS * RET_DV
RET_CHUNK = 128
ROPE_BASE = 10000.0
POOL_GROUPS = 4
POOL_CH = 128
POOL_W = POOL_GROUPS * POOL_CH
POOL_WINDOWS = (2, 4, 8, 16)
N_EXPERTS = 64
TOP_K = 8
N_EXPERT_GROUPS = 8
GROUP_SIZE = N_EXPERTS // N_EXPERT_GROUPS
TOPK_GROUPS = 4
D_EXPERT = 256
ROUTED_SCALE = 2.5
EPS = 1e-6
IN_SIZES = (RET_QK_W, RET_QK_W, RET_V_W, RET_V_W, POOL_W, D_MODEL, D_MODEL)
IN_OFFS = tuple(sum(IN_SIZES[:i]) for i in range(len(IN_SIZES) + 1))
IN_W = IN_OFFS[-1]

LANES = 128
VMEM_LIMIT = 56 << 20
N_PIECES = D_MODEL // 2 // LANES
MXU_DIM = 256
SC_CHUNK = 128

F32 = jnp.float32
BF16 = jnp.bfloat16
I32 = jnp.int32
U32 = jnp.uint32


def _cparams(*sem):
    return pltpu.CompilerParams(dimension_semantics=sem, vmem_limit_bytes=VMEM_LIMIT)


def _dot(a, b):
    return jnp.dot(a, b, preferred_element_type=F32)


def _silu(x):
    return x * jax.nn.sigmoid(x)


def _rms_mod(x, g, scale, shift):
    y = x * lax.rsqrt(jnp.mean(x * x, axis=-1, keepdims=True) + EPS)
    return (y * g) * (1.0 + scale) + shift


def _ada_kernel(c_ref, w_ref, b_ref, o_ref):
    c = c_ref[...]
    o_ref[...] = jnp.dot(_silu(c), w_ref[...], preferred_element_type=F32,
                         precision=lax.Precision.HIGHEST) + b_ref[...]


def _ada(c_rows, ada_w, ada_b):
    r = c_rows.shape[0]
    n = ada_w.shape[1]
    tn = D_MODEL
    return pl.pallas_call(
        _ada_kernel,
        grid=(n // tn,),
        in_specs=[pl.BlockSpec((r, D_MODEL), lambda j: (0, 0)),
                  pl.BlockSpec((D_MODEL, tn), lambda j: (0, j)),
                  pl.BlockSpec((1, tn), lambda j: (0, j))],
        out_specs=pl.BlockSpec((r, tn), lambda j: (0, j)),
        out_shape=jax.ShapeDtypeStruct((r, n), F32),
        compiler_params=_cparams("parallel"),
        name="ada_mod",
    )(c_rows, ada_w, ada_b.reshape(1, n))


def _inproj_kernel(x_ref, mod_ref, g_ref, w_ref, cos_ref, sin_ref,
                   q_ref, k_ref, v_ref, gsw_ref, up_ref, ga_ref, gb_ref, *, on_grid):
    x = x_ref[...]
    h = _rms_mod(x, g_ref[...], mod_ref[0, 1:2, :], mod_ref[0, 0:1, :]).astype(BF16)

    def seg(i):
        return _dot(h, w_ref[:, IN_OFFS[i]:IN_OFFS[i + 1]])

    q = seg(0)
    k = seg(1)
    if on_grid:
        cos = jnp.concatenate([cos_ref[...]] * RET_HEADS, axis=1)
        sin = jnp.concatenate([sin_ref[...]] * RET_HEADS, axis=1)
        lane = lax.broadcasted_iota(jnp.int32, q.shape, 1)
        first = (lane & 63) < 32

        def rope(a):
            up = pltpu.roll(a, RET_QK_W - 32, axis=1)
            dn = pltpu.roll(a, 32, axis=1)
            return a * cos + jnp.where(first, up, dn) * sin

        q = rope(q)
        k = rope(k)
    q_ref[...] = q.astype(BF16)
    k_ref[...] = (k * (RET_DK ** -0.5)).astype(BF16)
    v_ref[...] = seg(2).astype(BF16)
    gsw_ref[...] = seg(3).astype(BF16)
    up_ref[...] = seg(4).astype(BF16)
    ga_ref[...] = seg(5).astype(BF16)
    gb_ref[...] = seg(6).astype(BF16)


def _inproj(x, mods, norm_g, w_in, cos_t, sin_t, *, tokens_per_mod, seq_len, on_grid, tm):
    t = x.shape[0]
    tiles_per_mod = tokens_per_mod // tm
    tiles_per_seq = seq_len // tm
    widths = IN_SIZES
    out_shape = [jax.ShapeDtypeStruct((t, w), BF16) for w in widths]
    out_specs = [pl.BlockSpec((tm, w), lambda i: (i, 0)) for w in widths]
    return pl.pallas_call(
        functools.partial(_inproj_kernel, on_grid=on_grid),
        grid=(t // tm,),
        in_specs=[pl.BlockSpec((tm, D_MODEL), lambda i: (i, 0)),
                  pl.BlockSpec((1, 6, D_MODEL), lambda i: (i // tiles_per_mod, 0, 0)),
                  pl.BlockSpec((1, D_MODEL), lambda i: (0, 0)),
                  pl.BlockSpec((D_MODEL, IN_W), lambda i: (0, 0)),
                  pl.BlockSpec((tm, RET_DK), lambda i: (i % tiles_per_seq, 0)),
                  pl.BlockSpec((tm, RET_DK), lambda i: (i % tiles_per_seq, 0))],
        out_specs=out_specs,
        out_shape=out_shape,
        compiler_params=_cparams("parallel"),
        name="inproj_grid" if on_grid else "inproj_seq",
    )(x, mods, norm_g, w_in, cos_t, sin_t)


def _rope_tables(seq_len):
    t = np.arange(seq_len)
    row = (t // GRID_W).astype(np.float32)
    col = (t % GRID_W).astype(np.float32)
    m = RET_DK // 4
    inv = (np.float32(ROPE_BASE) ** (-np.arange(m, dtype=np.float32) / np.float32(m))).astype(np.float32)
    ar = row[:, None] * inv
    ac = col[:, None] * inv
    cos = np.concatenate([np.cos(ar), np.cos(ar), np.cos(ac), np.cos(ac)], axis=1)
    sin = np.concatenate([-np.sin(ar), np.sin(ar), -np.sin(ac), np.sin(ac)], axis=1)
    return jnp.asarray(cos, F32), jnp.asarray(sin, F32)


RET_HEADS_PER_STEP = 2


def _ret_kernel(dec_ref, q_ref, k_ref, v_ref, g_ref, s0f_ref, s0b_ref,
                z_ref, sf_ref, sb_ref, oacc_ref, kt_ref, *, n_chunks):
    c = RET_CHUNK
    heads = RET_HEADS_PER_STEP
    half = n_chunks // 2
    ii = lax.broadcasted_iota(I32, (c, c), 0)
    jj = lax.broadcasted_iota(I32, (c, c), 1)
    ik = lax.broadcasted_iota(I32, (c, RET_DK), 0).astype(F32)
    jk = lax.broadcasted_iota(I32, (RET_DK, c), 1).astype(F32)

    def log_gamma(d, shape):
        return jnp.log1p(-jnp.exp2(-jnp.full(shape, d, F32)))

    consts = {}
    for hh in range(heads):
        h = pl.program_id(1) * heads + hh
        dec_f = dec_ref[0, h]
        dec_b = dec_ref[1, h]
        rel = (ii - jj).astype(F32)
        consts[hh, "f"] = (
            jnp.where(rel >= 0, jnp.exp(log_gamma(dec_f, (c, c)) * jnp.maximum(rel, 0.0)), 0.0),
            jnp.exp(log_gamma(dec_f, (c, RET_DK)) * (ik + 1.0)),
            jnp.exp(log_gamma(dec_f, (RET_DK, c)) * (c - 1.0 - jk)),
            jnp.exp(log_gamma(dec_f, (RET_DK, RET_DV)) * c))
        consts[hh, "b"] = (
            jnp.where(rel <= 0, jnp.exp(log_gamma(dec_b, (c, c)) * jnp.maximum(-rel, 0.0)), 0.0),
            jnp.exp(log_gamma(dec_b, (c, RET_DK)) * (c - ik)),
            jnp.exp(log_gamma(dec_b, (RET_DK, c)) * jk),
            jnp.exp(log_gamma(dec_b, (RET_DK, RET_DV)) * c))

    sf_ref[...] = s0f_ref[...]
    sb_ref[...] = s0b_ref[...]

    def transpose_keys(ci, carry):
        r = pl.ds(pl.multiple_of(ci * c, c), c)
        for hh in range(heads):
            kt_ref[hh, ci] = k_ref[r, hh * RET_DK:(hh + 1) * RET_DK].T
        return carry

    lax.fori_loop(0, n_chunks, transpose_keys, 0)

    def scores(ci, hh, direction):
        r = pl.ds(pl.multiple_of(ci * c, c), c)
        kcols = slice(hh * RET_DK, (hh + 1) * RET_DK)
        qc = q_ref[r, kcols]
        sc = lax.dot_general(qc, k_ref[r, kcols], (((1,), (1,)), ((), ())), preferred_element_type=F32)
        return ci, hh, direction, r, qc, sc

    def advance(job):
        ci, hh, direction, r, qc, sc = job
        dmask, qdec, kdec, cdec = consts[hh, direction]
        s_ref = sf_ref if direction == "f" else sb_ref
        vc = v_ref[r, hh * RET_DV:(hh + 1) * RET_DV]
        s = s_ref[hh]
        lhs = jnp.concatenate([(sc * dmask).astype(BF16), (qc.astype(F32) * qdec).astype(BF16)], axis=1)
        o = _dot(lhs, jnp.concatenate([vc, s.astype(BF16)], axis=0))
        kd_t = (kt_ref[hh, ci].astype(F32) * kdec).astype(BF16)
        s_ref[hh] = s * cdec + _dot(kd_t, vc)
        return o

    def emit(job, o, second):
        _, hh, _, r, _, _ = job
        vcols = slice(hh * RET_DV, (hh + 1) * RET_DV)
        if not second:
            oacc_ref[hh, r, :] = o
        else:
            o = o + oacc_ref[hh, r, :]
            o = o * lax.rsqrt(jnp.mean(o * o, axis=-1, keepdims=True) + EPS)
            g = g_ref[r, vcols].astype(F32)
            z_ref[r, vcols] = (_silu(g) * o).astype(BF16)

    def body(second):
        def run(t, carry):
            jobs = [scores(ci, hh, d) for hh in range(heads)
                    for ci, d in ((t, "f"), (n_chunks - 1 - t, "b"))]
            outs = [advance(job) for job in jobs]
            for job, o in zip(jobs, outs):
                emit(job, o, second)
            return carry
        return run

    lax.fori_loop(0, half, body(False), 0, unroll=4 if half % 4 == 0 else 1)
    lax.fori_loop(half, n_chunks, body(True), 0, unroll=2 if half % 2 == 0 else 1)


def _retention(q, k, v, gsw, dec, s0f, s0b, *, batch, seq_len):
    n_chunks = seq_len // RET_CHUNK
    assert n_chunks % 2 == 0
    heads = RET_HEADS_PER_STEP
    t = batch * seq_len
    st_spec = pl.BlockSpec((None, heads, RET_DK, RET_DV), lambda b, h: (b, h, 0, 0))
    st_shape = jax.ShapeDtypeStruct((batch, RET_HEADS, RET_DK, RET_DV), F32)
    kspec = pl.BlockSpec((seq_len, heads * RET_DK), lambda b, h: (b, h))
    vspec = pl.BlockSpec((seq_len, heads * RET_DV), lambda b, h: (b, h))
    return pl.pallas_call(
        functools.partial(_ret_kernel, n_chunks=n_chunks),
        grid=(batch, RET_HEADS // heads),
        in_specs=[pl.BlockSpec(memory_space=pltpu.SMEM), kspec, kspec, vspec, vspec, st_spec, st_spec],
        out_specs=[vspec, st_spec, st_spec],
        out_shape=[jax.ShapeDtypeStruct((t, RET_V_W), BF16), st_shape, st_shape],
        scratch_shapes=[pltpu.VMEM((heads, seq_len, RET_DV), F32),
                        pltpu.VMEM((heads, n_chunks, RET_DK, RET_CHUNK), BF16)],
        compiler_params=_cparams("parallel", "parallel"),
        name=f"retention_l{seq_len}",
    )(dec, q, k, v, gsw, s0f, s0b)


def _pool_kernel(u_ref, w_ref, sc_ref, o_ref, *, seq_len, on_grid):
    tok = lax.broadcasted_iota(jnp.int32, (seq_len, POOL_CH), 0)

    def shift(a, s, stride, pos, width):
        y = pltpu.roll(a, (-s * stride) % seq_len, axis=0)
        ok = (pos < width - s) if s > 0 else (pos >= -s)
        return jnp.where(ok, y, 0.0)

    def box_mean(a, window, stride, pos, width):
        half = window // 2
        fw = a
        bw = shift(a, -1, stride, pos, width)
        m = 1
        while m < half:
            fw = fw + shift(fw, m, stride, pos, width)
            bw = bw + shift(bw, -m, stride, pos, width)
            m *= 2
        cnt = jnp.minimum(pos + half, width) - jnp.maximum(pos - half, 0)
        return (fw + bw) / cnt.astype(F32)

    for g, window in enumerate(POOL_WINDOWS):
        cols = slice(g * POOL_CH, (g + 1) * POOL_CH)
        ug = u_ref[:, cols].astype(F32)
        if on_grid:
            pooled = box_mean(ug, window, 1, tok & (GRID_W - 1), GRID_W)
            pooled = box_mean(pooled, window, GRID_W, tok >> 6, seq_len // GRID_W)
        else:
            pooled = box_mean(ug, window, 1, tok, seq_len)
        d = (pooled - ug).astype(BF16)
        o_ref[:, cols] = (_dot(d, w_ref[g]) * sc_ref[:, cols]).astype(BF16)


def _pool(u, pool_w, pool_scale, *, batch, seq_len, on_grid):
    t = batch * seq_len
    return pl.pallas_call(
        functools.partial(_pool_kernel, seq_len=seq_len, on_grid=on_grid),
        grid=(batch,),
        in_specs=[pl.BlockSpec((seq_len, POOL_W), lambda b: (b, 0)),
                  pl.BlockSpec((POOL_GROUPS, POOL_CH, POOL_CH), lambda b: (0, 0, 0)),
                  pl.BlockSpec((1, POOL_W), lambda b: (0, 0))],
        out_specs=pl.BlockSpec((seq_len, POOL_W), lambda b: (b, 0)),
        out_shape=jax.ShapeDtypeStruct((t, POOL_W), BF16),
        compiler_params=_cparams("parallel"),
        name=f"pool_l{seq_len}",
    )(u, pool_w, pool_scale)


def _pack_rows(x):
    half = D_MODEL // 2
    lo = lax.bitcast_convert_type(x[:, :half].astype(BF16).astype(F32), U32) >> 16
    hi = lax.bitcast_convert_type(x[:, half:].astype(BF16).astype(F32), U32) & jnp.uint32(0xFFFF0000)
    word = lax.bitcast_convert_type(hi | lo, I32)
    return [word[:, c * LANES:(c + 1) * LANES] for c in range(N_PIECES)]


def _unpack_rows(pieces):
    words = [lax.bitcast_convert_type(p, U32) for p in pieces]
    lo = [lax.bitcast_convert_type(w << 16, F32) for w in words]
    hi = [lax.bitcast_convert_type(w & jnp.uint32(0xFFFF0000), F32) for w in words]
    return lo, hi


def _merge_kernel(x_ref, z_ref, p_ref, ga_ref, gb_ref, mod_ref, g2_ref, wr_ref, wp_ref, wo_ref,
                  x1_ref, *piece_refs):
    y_ret = _dot(z_ref[...], wr_ref[...])
    y_pool = _dot(p_ref[...], wp_ref[...])
    merged = (jax.nn.sigmoid(ga_ref[...].astype(F32)) * y_ret
              + jax.nn.sigmoid(gb_ref[...].astype(F32)) * y_pool)
    x1 = x_ref[...] + mod_ref[0, 2:3, :] * _dot(merged.astype(BF16), wo_ref[...])
    x1_ref[...] = x1
    h2 = _rms_mod(x1, g2_ref[...], mod_ref[0, 4:5, :], mod_ref[0, 3:4, :])
    for ref, piece in zip(piece_refs, _pack_rows(h2)):
        ref[...] = piece


def _merge(x, z, p, ga, gb, mods, norm2_g, w_br_ret, w_br_pool, w_out, *, tokens_per_mod, tm):
    t = x.shape[0]
    tiles_per_mod = tokens_per_mod // tm
    row = lambda w: pl.BlockSpec((tm, w), lambda i: (i, 0))
    full = lambda a: pl.BlockSpec(a.shape, lambda i: (0,) * a.ndim)
    outs = pl.pallas_call(
        _merge_kernel,
        grid=(t // tm,),
        in_specs=[row(D_MODEL), row(RET_V_W), row(POOL_W), row(D_MODEL), row(D_MODEL),
                  pl.BlockSpec((1, 6, D_MODEL), lambda i: (i // tiles_per_mod, 0, 0)),
                  full(norm2_g), full(w_br_ret), full(w_br_pool), full(w_out)],
        out_specs=[row(D_MODEL)] + [row(LANES)] * N_PIECES,
        out_shape=[jax.ShapeDtypeStruct((t, D_MODEL), F32)] + [jax.ShapeDtypeStruct((t, LANES), I32)] * N_PIECES,
        compiler_params=_cparams("parallel"),
        name="merge",
    )(x, z, p, ga, gb, mods, norm2_g, w_br_ret, w_br_pool, w_out)
    return outs[0], outs[1:]


def _route_kernel(*refs):
    h_refs = refs[:N_PIECES]
    rw_ref, bias_ref, idx_ref, rank_ref, wk_ref, cnt_ref, carry_ref = refs[N_PIECES:]
    e = N_EXPERTS
    tm = h_refs[0].shape[0]
    neg = -jnp.inf

    @pl.when(pl.program_id(0) == 0)
    def _():
        carry_ref[...] = jnp.zeros(carry_ref.shape, F32)

    lo, hi = _unpack_rows([r[...] for r in h_refs])
    h = jnp.concatenate(lo + hi, axis=1).astype(BF16)
    logits = lax.dot_general(rw_ref[...], h, (((1,), (1,)), ((), ())), preferred_element_type=F32)[:e]
    scores = jax.nn.sigmoid(logits)
    sel = scores + bias_ref[:e, 0:1]
    e_idx = lax.broadcasted_iota(I32, (e, tm), 0)

    grp = sel.reshape(N_EXPERT_GROUPS, GROUP_SIZE, tm)
    m_idx = lax.broadcasted_iota(I32, grp.shape, 1)
    m1 = jnp.max(grp, axis=1, keepdims=True)
    first = jnp.min(jnp.where(grp == m1, m_idx, GROUP_SIZE), axis=1, keepdims=True)
    m2 = jnp.max(jnp.where(m_idx == first, neg, grp), axis=1, keepdims=True)
    gscore = (m1 + m2).reshape(N_EXPERT_GROUPS, tm)

    g_idx = lax.broadcasted_iota(I32, gscore.shape, 0)
    grank = jnp.zeros(gscore.shape, I32)
    for g in range(N_EXPERT_GROUPS):
        other = gscore[g:g + 1, :]
        beats = jnp.where(other > gscore, 1, jnp.where(other == gscore, (g_idx > g).astype(I32), 0))
        grank = grank + beats
    gkeep = (grank < TOPK_GROUPS).astype(F32)
    ekeep = jnp.broadcast_to(gkeep.reshape(N_EXPERT_GROUPS, 1, tm), grp.shape).reshape(e, tm)
    masked = jnp.where(ekeep > 0, sel, neg)

    chosen = jnp.zeros((e, tm), F32)
    picks, hits = [], []
    for _ in range(TOP_K):
        m = jnp.max(masked, axis=0, keepdims=True)
        pick = jnp.min(jnp.where(masked == m, e_idx, e), axis=0, keepdims=True)
        hit = e_idx == pick
        chosen = jnp.where(hit, 1.0, chosen)
        masked = jnp.where(hit, neg, masked)
        picks.append(pick)
        hits.append(hit)

    w = scores * chosen
    comb = w / jnp.sum(w, axis=0, keepdims=True) * ROUTED_SCALE

    t_row = lax.broadcasted_iota(I32, (tm, tm), 0)
    t_col = lax.broadcasted_iota(I32, (tm, tm), 1)
    before = (t_row < t_col).astype(BF16)
    rankmat = _dot(chosen.astype(BF16), before) + carry_ref[:e, 0:1]
    carry_ref[:e, :] = carry_ref[:e, :] + jnp.sum(chosen, axis=1, keepdims=True)
    cnt_ref[...] = carry_ref[...]

    idx_ref[...] = jnp.concatenate(picks, axis=0)
    rank_ref[...] = jnp.concatenate(
        [jnp.sum(jnp.where(h, rankmat, 0.0), axis=0, keepdims=True) for h in hits], axis=0).astype(I32)
    wk_ref[...] = jnp.concatenate(
        [jnp.sum(jnp.where(h, comb, 0.0), axis=0, keepdims=True) for h in hits], axis=0)


def _route(h2_pieces, router_wt, bias_col, *, tm):
    t = h2_pieces[0].shape[0]
    krow = pl.BlockSpec((TOP_K, tm), lambda i: (0, i))
    return pl.pallas_call(
        _route_kernel,
        grid=(t // tm,),
        in_specs=[pl.BlockSpec((tm, LANES), lambda i: (i, 0))] * N_PIECES
        + [pl.BlockSpec((LANES, D_MODEL), lambda i: (0, 0)), pl.BlockSpec((LANES, 1), lambda i: (0, 0))],
        out_specs=[krow, krow, krow, pl.BlockSpec((LANES, LANES), lambda i: (0, 0))],
        out_shape=[jax.ShapeDtypeStruct((TOP_K, t), I32), jax.ShapeDtypeStruct((TOP_K, t), I32),
                   jax.ShapeDtypeStruct((TOP_K, t), F32), jax.ShapeDtypeStruct((LANES, LANES), F32)],
        scratch_shapes=[pltpu.VMEM((LANES, LANES), F32)],
        compiler_params=_cparams("arbitrary"),
        name="route",
    )(*h2_pieces, router_wt, bias_col)


def _plan_kernel(idx_ref, rank_ref, cnt_ref, pos_ref, te_ref, nu_ref, *, group_tile, min_tiles):
    tf = idx_ref.shape[1]
    nt = te_ref.shape[1]
    cnt = cnt_ref[...].astype(I32)
    tiles = jnp.maximum((cnt + (group_tile - 1)) // group_tile, min_tiles)
    e_sub = lax.broadcasted_iota(I32, (LANES, LANES), 0)
    padded = jnp.where(e_sub < N_EXPERTS, tiles * group_tile, 0).astype(F32)
    e_lane = lax.broadcasted_iota(I32, (LANES, LANES), 1)
    base = jnp.sum(jnp.where(e_lane < e_sub, padded.T, 0.0), axis=1, keepdims=True)
    end = base + padded[:, 0:1]

    idx = idx_ref[...]
    start = jnp.zeros(idx.shape, F32)
    for e in range(N_EXPERTS):
        start = jnp.where(idx == e, base[e:e + 1, 0:1], start)
    pos = start.astype(I32) + rank_ref[...]
    for j in range(tf // SC_CHUNK):
        pos_ref[j] = pos[:, j * SC_CHUNK:(j + 1) * SC_CHUNK]

    tile_start = (lax.broadcasted_iota(I32, (N_EXPERTS, nt), 1) * group_tile).astype(F32)
    done = jnp.sum(jnp.where(end[:N_EXPERTS] <= tile_start, 1.0, 0.0), axis=0, keepdims=True)
    te_ref[...] = jnp.minimum(done, N_EXPERTS - 1.0).astype(I32)
    total = jnp.sum(padded[:, 0:1], axis=0, keepdims=True)
    nu_ref[...] = jnp.broadcast_to(total * (1.0 / group_tile), nu_ref.shape).astype(I32)


def _plan(idx, rank, counts, *, n_tiles, tf, group_tile, min_tiles):
    t = idx.shape[1]
    nt_pad = -(-n_tiles // LANES) * LANES
    krow = pl.BlockSpec((TOP_K, tf), lambda i: (0, i))
    return pl.pallas_call(
        functools.partial(_plan_kernel, group_tile=group_tile, min_tiles=min_tiles),
        grid=(t // tf,),
        in_specs=[krow, krow, pl.BlockSpec((LANES, LANES), lambda i: (0, 0))],
        out_specs=[pl.BlockSpec((tf // SC_CHUNK, TOP_K, SC_CHUNK), lambda i: (i, 0, 0)),
                   pl.BlockSpec((1, nt_pad), lambda i: (0, 0)),
                   pl.BlockSpec((1, LANES), lambda i: (0, 0))],
        out_shape=[jax.ShapeDtypeStruct((t // SC_CHUNK, TOP_K, SC_CHUNK), I32),
                   jax.ShapeDtypeStruct((1, nt_pad), I32), jax.ShapeDtypeStruct((1, LANES), I32)],
        compiler_params=_cparams("arbitrary"),
        name="moe_plan",
    )(idx, rank, counts)


def _sc_mesh_info():
    info = plsc.get_sparse_core_info()
    mesh = plsc.VectorSubcoreMesh(core_axis_name="c", subcore_axis_name="s")
    return mesh, info.num_cores, info.num_cores * info.num_subcores


def _sc_dispatch(pieces, pos, *, n_rows):
    t = pieces[0].shape[0]
    mesh, n_cores, n_workers = _sc_mesh_info()
    per_w = t // SC_CHUNK // n_workers

    @functools.partial(
        pl.kernel, mesh=mesh,
        out_type=[jax.ShapeDtypeStruct((n_rows, LANES), I32)] * N_PIECES,
        scratch_types=[pltpu.VMEM((TOP_K, SC_CHUNK), I32),
                       pltpu.VMEM((N_PIECES, SC_CHUNK, LANES), I32),
                       pltpu.SemaphoreType.DMA((N_PIECES,)),
                       pltpu.SemaphoreType.DMA],
        name="sc_dispatch",
    )
    def run(*refs):
        src = refs[:N_PIECES]
        pos_hbm = refs[N_PIECES]
        dst = refs[N_PIECES + 1:2 * N_PIECES + 1]
        idx_v, rows_v, load_sem, put_sem = refs[2 * N_PIECES + 1:]
        wid = lax.axis_index("s") * n_cores + lax.axis_index("c")

        @pl.loop(0, per_w)
        def _(j):
            ch = wid * per_w + j
            t0 = pl.multiple_of(ch * SC_CHUNK, SC_CHUNK)
            loads = [pltpu.make_async_copy(src[c].at[pl.ds(t0, SC_CHUNK)], rows_v.at[c], load_sem.at[c])
                     for c in range(N_PIECES)]
            for ld in loads:
                ld.start()
            pltpu.sync_copy(pos_hbm.at[ch], idx_v)
            puts = []
            for c in range(N_PIECES):
                loads[c].wait()
                for k in range(TOP_K):
                    puts.append(pltpu.make_async_copy(rows_v.at[c], dst[c].at[idx_v.at[k]], put_sem))
                    puts[-1].start()
            for cp in puts:
                cp.wait()

    return run(*pieces, pos)


SC_GROUP = 32


def _sc_combine(pieces, pos, wts, *, n_tokens):
    mesh, n_cores, n_workers = _sc_mesh_info()
    lanes = plsc.get_sparse_core_info().num_lanes
    per_w = n_tokens // SC_GROUP // n_workers

    @functools.partial(
        pl.kernel, mesh=mesh,
        out_type=[jax.ShapeDtypeStruct((n_tokens, LANES), F32)] * (2 * N_PIECES),
        scratch_types=[pltpu.VMEM((TOP_K, SC_GROUP), I32),
                       pltpu.VMEM((TOP_K, SC_GROUP), F32),
                       pltpu.VMEM((2, TOP_K, SC_GROUP, LANES), I32),
                       pltpu.VMEM((2, 2, SC_GROUP, LANES), F32),
                       pltpu.SemaphoreType.DMA((2,)),
                       pltpu.SemaphoreType.DMA((2,))],
        compiler_params=pltpu.CompilerParams(needs_layout_passes=False),
        name="sc_combine",
    )
    def run(*refs):
        src = refs[:N_PIECES]
        pos_hbm, wts_hbm = refs[N_PIECES:N_PIECES + 2]
        dst = refs[N_PIECES + 2:3 * N_PIECES + 2]
        idx_v, w_v, buf, acc, get_sem, put_sem = refs[3 * N_PIECES + 2:]
        wid = lax.axis_index("s") * n_cores + lax.axis_index("c")

        @pl.loop(0, per_w)
        def _(j):
            grp = wid * per_w + j
            t0 = pl.multiple_of(grp * SC_GROUP, SC_GROUP)
            pltpu.sync_copy(pos_hbm.at[grp], idx_v)
            pltpu.sync_copy(wts_hbm.at[grp], w_v)

            def gets(c, slot):
                return [pltpu.make_async_copy(src[c].at[idx_v.at[k]], buf.at[slot, k], get_sem.at[slot])
                        for k in range(TOP_K)]

            def puts(c, slot):
                r = pl.ds(t0, SC_GROUP)
                return [pltpu.make_async_copy(acc.at[slot, 0], dst[c].at[r], put_sem.at[slot]),
                        pltpu.make_async_copy(acc.at[slot, 1], dst[N_PIECES + c].at[r], put_sem.at[slot])]

            for cp in gets(0, 0):
                cp.start()
            for c in range(N_PIECES):
                slot = c % 2
                if c + 1 < N_PIECES:
                    for cp in gets(c + 1, 1 - slot):
                        cp.start()
                for cp in gets(c, slot):
                    cp.wait()
                if c >= 2:
                    for cp in puts(c - 2, slot):
                        cp.wait()

                @pl.loop(0, SC_GROUP)
                def _(r):
                    row = jnp.full((lanes,), r, I32)
                    w = [plsc.load_gather(w_v, [jnp.full((lanes,), k, I32), row]) for k in range(TOP_K)]
                    for q in range(LANES // lanes):
                        cols = pl.ds(q * lanes, lanes)
                        lo = hi = None
                        for k in range(TOP_K):
                            word = buf[slot, k, r, cols]
                            lo_k = plsc.bitcast(word << 16, F32) * w[k]
                            hi_k = plsc.bitcast(word & jnp.int32(-65536), F32) * w[k]
                            lo = lo_k if lo is None else lo + lo_k
                            hi = hi_k if hi is None else hi + hi_k
                        acc[slot, 0, r, cols] = lo
                        acc[slot, 1, r, cols] = hi

                for cp in puts(c, slot):
                    cp.start()
            for c in range(N_PIECES - 2, N_PIECES):
                for cp in puts(c, c % 2):
                    cp.wait()

    return run(*pieces, pos, wts)


def _group_tile(n_tokens):
    per_expert = n_tokens * TOP_K // N_EXPERTS
    return max(MXU_DIM, min(4 * MXU_DIM, per_expert // MXU_DIM * MXU_DIM))


SECOND_DMA_QUEUE = 1


def _experts_kernel(te_ref, nu_ref, *refs, cast_weights, tile):
    x_hbm = refs[:N_PIECES]
    w_in = refs[N_PIECES:N_PIECES + 3]
    y_hbm = refs[N_PIECES + 3:2 * N_PIECES + 3]
    xbuf, xsem, ybuf, ysem = refs[-4:]
    i = pl.program_id(0)
    n_used = nu_ref[0]
    slot = lax.rem(i, 2)

    def x_copies(step, into):
        r = pl.ds(pl.multiple_of(step * tile, tile), tile)
        return [pltpu.make_async_copy(x_hbm[c].at[r], xbuf.at[into, c], xsem.at[into, c])
                for c in range(N_PIECES)]

    def y_copies(step, out_of):
        r = pl.ds(pl.multiple_of(step * tile, tile), tile)
        return [pltpu.make_async_copy(ybuf.at[out_of, c], y_hbm[c].at[r], ysem.at[out_of, c])
                for c in range(N_PIECES)]

    @pl.when(i == 0)
    def _():
        for cp in x_copies(0, 0):
            cp.start(priority=SECOND_DMA_QUEUE)

    @pl.when(i + 1 < n_used)
    def _():
        for cp in x_copies(i + 1, 1 - slot):
            cp.start(priority=SECOND_DMA_QUEUE)

    if cast_weights:
        w_bf = refs[2 * N_PIECES + 3:-4]
        last = n_used - 1
        cur = te_ref[jnp.minimum(i, last)]
        prev = te_ref[jnp.minimum(jnp.maximum(i - 1, 0), last)]

        @pl.when((i == 0) | (cur != prev))
        def _():
            for dst, src in zip(w_bf, w_in):
                dst[...] = src[...].astype(BF16)
    else:
        w_bf = w_in
    wg_ref, wu_ref, wd_ref = w_bf

    @pl.when(i < n_used)
    def _():
        for cp in x_copies(i, slot):
            cp.wait()
        subs = [slice(s * MXU_DIM, (s + 1) * MXU_DIM) for s in range(tile // MXU_DIM)]
        xs = []
        for rows in subs:
            lo, hi = _unpack_rows([xbuf[slot, c, rows, :] for c in range(N_PIECES)])
            xs.append(jnp.concatenate(lo + hi, axis=1).astype(BF16))
        gates = [(_dot(x, wg_ref[...]), _dot(x, wu_ref[...])) for x in xs]
        ys = [_dot((_silu(g) * u).astype(BF16), wd_ref[...]) for g, u in gates]
        for rows, y in zip(subs, ys):
            for c, piece in enumerate(_pack_rows(y)):
                ybuf[slot, c, rows, :] = piece

        @pl.when(i >= 1)
        def _():
            for cp in y_copies(i - 1, 1 - slot):
                cp.wait()

        for cp in y_copies(i, slot):
            cp.start(priority=SECOND_DMA_QUEUE)

        @pl.when(i == n_used - 1)
        def _():
            for cp in y_copies(i, slot):
                cp.wait()


def _experts(x_pieces, tile_expert, n_used, wg, wu, wd, *, group_tile):
    n_rows = x_pieces[0].shape[0]
    n_tiles = n_rows // group_tile
    cast_weights = wg.dtype != BF16

    def tile(i, te, nu):
        return jnp.minimum(i, nu[0] - 1)

    wspec = lambda a: pl.BlockSpec((None,) + a.shape[1:], lambda i, te, nu: (te[tile(i, te, nu)], 0, 0))
    w_specs = [wspec(wg), wspec(wu), wspec(wd)]
    y_shape = [jax.ShapeDtypeStruct((n_rows, LANES), I32)] * N_PIECES
    w_shape = [jax.ShapeDtypeStruct(a.shape, BF16) for a in (wg, wu, wd)]
    outs = pl.pallas_call(
        functools.partial(_experts_kernel, cast_weights=cast_weights, tile=group_tile),
        grid_spec=pltpu.PrefetchScalarGridSpec(
            num_scalar_prefetch=2,
            grid=(n_tiles,),
            in_specs=[pl.BlockSpec(memory_space=pl.ANY)] * N_PIECES + w_specs,
            out_specs=[pl.BlockSpec(memory_space=pl.ANY)] * N_PIECES + (w_specs if cast_weights else []),
            scratch_shapes=[pltpu.VMEM((2, N_PIECES, group_tile, LANES), I32),
                            pltpu.SemaphoreType.DMA((2, N_PIECES))] * 2),
        out_shape=y_shape + (w_shape if cast_weights else []),
        compiler_params=_cparams("arbitrary"),
        name="experts_cast" if cast_weights else "experts",
    )(tile_expert, n_used, *x_pieces, wg, wu, wd)
    return outs[:N_PIECES], (tuple(outs[N_PIECES:]) if cast_weights else (wg, wu, wd))


def _moe_out_kernel(x1_ref, mod_ref, fg_ref, sg_ref, su_ref, sd_ref, *refs):
    h_refs = refs[:N_PIECES]
    routed_refs = refs[N_PIECES:3 * N_PIECES]
    out_ref = refs[3 * N_PIECES]
    h_lo, h_hi = _unpack_rows([r[...] for r in h_refs])
    h = jnp.concatenate(h_lo + h_hi, axis=1).astype(BF16)
    hid = _silu(_dot(h, sg_ref[...])) * _dot(h, su_ref[...])
    shared = _dot(hid.astype(BF16), sd_ref[...])
    routed = jnp.concatenate([r[...] for r in routed_refs], axis=1)
    x2 = x1_ref[...] + mod_ref[0, 5:6, :] * (shared + routed)
    out_ref[...] = x2 * lax.rsqrt(jnp.mean(x2 * x2, axis=-1, keepdims=True) + EPS) * fg_ref[...]


def _moe_out(h2_pieces, x1, mods, final_g, sg, su, sd, routed_pieces, *, tokens_per_mod, tm):
    t = x1.shape[0]
    tiles_per_mod = tokens_per_mod // tm
    row = lambda w: pl.BlockSpec((tm, w), lambda i: (i, 0))
    full = lambda a: pl.BlockSpec(a.shape, lambda i: (0,) * a.ndim)
    return pl.pallas_call(
        _moe_out_kernel,
        grid=(t // tm,),
        in_specs=[row(D_MODEL),
                  pl.BlockSpec((1, 6, D_MODEL), lambda i: (i // tiles_per_mod, 0, 0)),
                  full(final_g), full(sg), full(su), full(sd)]
        + [row(LANES)] * (3 * N_PIECES),
        out_specs=row(D_MODEL),
        out_shape=jax.ShapeDtypeStruct((t, D_MODEL), F32),
        compiler_params=_cparams("parallel"),
        name="moe_out",
    )(x1, mods, final_g, sg, su, sd, *h2_pieces, *routed_pieces)


def _trunk(x, mods, s0f, s0b, w, expert_w, *, batch, seq_len, on_grid):
    t = batch * seq_len
    tokens_per_mod = t // mods.shape[0]
    cos_t, sin_t = _rope_tables(seq_len)
    q, k, v, gsw, up, ga, gb = _inproj(x, mods, w["norm1_g"], w["w_in"], cos_t, sin_t,
                                       tokens_per_mod=tokens_per_mod, seq_len=seq_len,
                                       on_grid=on_grid, tm=256)
    z, s_f, s_b = _retention(q, k, v, gsw, w["dec"], s0f, s0b, batch=batch, seq_len=seq_len)
    p = _pool(up, w["pool_w"], w["pool_scale"], batch=batch, seq_len=seq_len, on_grid=on_grid)
    x1, h2_pieces = _merge(x, z, p, ga, gb, mods, w["norm2_g"], w["w_br_ret"], w["w_br_pool"],
                               w["w_out"], tokens_per_mod=tokens_per_mod, tm=512)

    group_tile = _group_tile(t)
    n_rows = t * TOP_K + N_EXPERTS * group_tile
    idx, rank, wts, counts = _route(h2_pieces, w["router_wt"], w["router_bias"], tm=1024)
    min_tiles = int(expert_w[0].dtype != BF16)
    pos, tile_expert, n_used = _plan(idx, rank, counts, n_tiles=n_rows // group_tile, tf=2048,
                                     group_tile=group_tile, min_tiles=min_tiles)
    x_sorted = _sc_dispatch(h2_pieces, pos, n_rows=n_rows)
    y_sorted, expert_w = _experts(x_sorted, tile_expert.reshape(-1), n_used.reshape(-1), *expert_w,
                                  group_tile=group_tile)
    regroup = lambda a: a.reshape(TOP_K, t // SC_GROUP, SC_GROUP).transpose(1, 0, 2)
    pos_rows = pos.transpose(1, 0, 2).reshape(TOP_K, t)
    routed = _sc_combine(y_sorted, regroup(pos_rows), regroup(wts), n_tokens=t)
    y = _moe_out(h2_pieces, x1, mods, w["final_g"], w["sh_w_gate"], w["sh_w_up"], w["sh_w_down"], routed,
                 tokens_per_mod=tokens_per_mod, tm=512)
    return y, s_f, s_b, expert_w


def kernel(x_prompt, x_sample, state_ret_fwd, state_ret_bwd, c, c_ctx, ada_w, ada_b, norm1_g, norm2_g, w_in,
           ret_decay_fwd, ret_decay_bwd, w_br_ret, pool_w, pool_scale, w_br_pool, w_out, router_w, router_bias,
           exp_w_gate, exp_w_up, exp_w_down, sh_w_gate, sh_w_up, sh_w_down, final_norm_g):
    n_req, seq, d = x_prompt.shape
    n_dec, dec_seq, _ = x_sample.shape
    depth = ada_w.shape[0]
    assert depth == 1 and d == D_MODEL

    xc = x_prompt.reshape(n_req * seq, d)
    xs = x_sample.reshape(n_dec * dec_seq, d)
    zero_state = jnp.zeros((n_req, RET_HEADS, RET_DK, RET_DV), F32)
    new_f, new_b = [], []
    for l in range(depth):
        c_rows = jnp.concatenate([c_ctx[None, :], c, jnp.zeros((8 - 1 - n_dec, d), F32)], axis=0)
        mods = _ada(c_rows, ada_w[l], ada_b[l]).reshape(8, 6, d)
        pad_rows = LANES - N_EXPERTS
        w = dict(
            norm1_g=norm1_g[l].reshape(1, d), norm2_g=norm2_g[l].reshape(1, d),
            final_g=final_norm_g.reshape(1, d),
            w_in=w_in[l].astype(BF16),
            dec=jnp.stack([ret_decay_fwd[l], ret_decay_bwd[l]]).astype(F32),
            w_br_ret=w_br_ret[l].astype(BF16), pool_w=pool_w[l].astype(BF16),
            pool_scale=pool_scale[l].reshape(1, POOL_W), w_br_pool=w_br_pool[l].astype(BF16),
            w_out=w_out[l].astype(BF16),
            router_wt=jnp.pad(router_w[l].T, ((0, pad_rows), (0, 0))).astype(BF16),
            router_bias=jnp.pad(router_bias[l].astype(F32).reshape(N_EXPERTS, 1), ((0, pad_rows), (0, 0))),
            sh_w_gate=sh_w_gate[l].astype(BF16),
            sh_w_up=sh_w_up[l].astype(BF16), sh_w_down=sh_w_down[l].astype(BF16),
        )
        expert_w = (exp_w_gate[l], exp_w_up[l], exp_w_down[l])
        xs, _, _, expert_w = _trunk(xs, mods[1:1 + n_dec], state_ret_fwd[:, l].astype(F32),
                                    state_ret_bwd[:, l].astype(F32), w, expert_w,
                                    batch=n_dec, seq_len=dec_seq, on_grid=True)
        xc, s_f, s_b, _ = _trunk(xc, mods[0:1], zero_state, zero_state, w, expert_w,
                                 batch=n_req, seq_len=seq, on_grid=False)
        new_f.append(s_f)
        new_b.append(s_b)
    y_prompt = xc.reshape(n_req, seq, d)
    y_sample = xs.reshape(n_dec, dec_seq, d)
    return (y_prompt, y_sample, jnp.stack(new_f, axis=1).astype(x_prompt.dtype),
            jnp.stack(new_b, axis=1).astype(x_prompt.dtype))
```

```python
import functools
import math

import numpy as np
import jax
import jax.numpy as jnp
from jax import lax
from jax.experimental import pallas as pl
from jax.experimental.pallas import tpu as pltpu
from jax.experimental.pallas import tpu_sc as plsc

D_MODEL = 1024
GRID_W = 64
RET_HEADS = 4
RET_DK = 128
RET_DV = 256
RET_QK_W = RET_HEADS * RET_DK
RET_V_W = RET_HEADS * RET_DV
RET_CHUNK = 128
ROPE_BASE = 10000.0
POOL_GROUPS = 4
POOL_CH = 128
POOL_W = POOL_GROUPS * POOL_CH
POOL_WINDOWS = (2, 4, 8, 16)
N_EXPERTS = 64
TOP_K = 8
N_EXPERT_GROUPS = 8
GROUP_SIZE = N_EXPERTS // N_EXPERT_GROUPS
TOPK_GROUPS = 4
D_EXPERT = 256
ROUTED_SCALE = 2.5
EPS = 1e-6
IN_SIZES = (RET_QK_W, RET_QK_W, RET_V_W, RET_V_W, POOL_W, D_MODEL, D_MODEL)
IN_OFFS = tuple(sum(IN_SIZES[:i]) for i in range(len(IN_SIZES) + 1))
IN_W = IN_OFFS[-1]

LANES = 128
VMEM_LIMIT = 56 << 20
N_PIECES = D_MODEL // 2 // LANES
MXU_DIM = 256
SC_CHUNK = 128

F32 = jnp.float32
BF16 = jnp.bfloat16
I32 = jnp.int32
U32 = jnp.uint32


def _cparams(*sem):
    return pltpu.CompilerParams(dimension_semantics=sem, vmem_limit_bytes=VMEM_LIMIT)


def _dot(a, b):
    return jnp.dot(a, b, preferred_element_type=F32)


def _silu(x):
    return x * jax.nn.sigmoid(x)


def _rms_mod(x, g, scale, shift):
    y = x * lax.rsqrt(jnp.mean(x * x, axis=-1, keepdims=True) + EPS)
    return (y * g) * (1.0 + scale) + shift


def _ada_kernel(c_ref, w_ref, b_ref, o_ref):
    c = c_ref[...]
    o_ref[...] = jnp.dot(_silu(c), w_ref[...], preferred_element_type=F32,
                         precision=lax.Precision.HIGHEST) + b_ref[...]


def _ada(c_rows, ada_w, ada_b):
    r = c_rows.shape[0]
    n = ada_w.shape[1]
    tn = D_MODEL
    return pl.pallas_call(
        _ada_kernel,
        grid=(n // tn,),
        in_specs=[pl.BlockSpec((r, D_MODEL), lambda j: (0, 0)),
                  pl.BlockSpec((D_MODEL, tn), lambda j: (0, j)),
                  pl.BlockSpec((1, tn), lambda j: (0, j))],
        out_specs=pl.BlockSpec((r, tn), lambda j: (0, j)),
        out_shape=jax.ShapeDtypeStruct((r, n), F32),
        compiler_params=_cparams("parallel"),
        name="ada_mod",
    )(c_rows, ada_w, ada_b.reshape(1, n))


def _inproj_kernel(x_ref, mod_ref, g_ref, w_ref, cos_ref, sin_ref,
                   q_ref, k_ref, v_ref, gsw_ref, up_ref, ga_ref, gb_ref, *, on_grid):
    x = x_ref[...]
    h = _rms_mod(x, g_ref[...], mod_ref[0, 1:2, :], mod_ref[0, 0:1, :]).astype(BF16)

    def seg(i):
        return _dot(h, w_ref[:, IN_OFFS[i]:IN_OFFS[i + 1]])

    q = seg(0)
    k = seg(1)
    if on_grid:
        cos = jnp.concatenate([cos_ref[...]] * RET_HEADS, axis=1)
        sin = jnp.concatenate([sin_ref[...]] * RET_HEADS, axis=1)
        lane = lax.broadcasted_iota(jnp.int32, q.shape, 1)
        first = (lane & 63) < 32

        def rope(a):
            up = pltpu.roll(a, RET_QK_W - 32, axis=1)
            dn = pltpu.roll(a, 32, axis=1)
            return a * cos + jnp.where(first, up, dn) * sin

        q = rope(q)
        k = rope(k)
    q_ref[...] = q.astype(BF16)
    k_ref[...] = (k * (RET_DK ** -0.5)).astype(BF16)
    v_ref[...] = seg(2).astype(BF16)
    gsw_ref[...] = seg(3).astype(BF16)
    up_ref[...] = seg(4).astype(BF16)
    ga_ref[...] = seg(5).astype(BF16)
    gb_ref[...] = seg(6).astype(BF16)


def _inproj(x, mods, norm_g, w_in, cos_t, sin_t, *, tokens_per_mod, seq_len, on_grid, tm):
    t = x.shape[0]
    tiles_per_mod = tokens_per_mod // tm
    tiles_per_seq = seq_len // tm
    widths = IN_SIZES
    out_shape = [jax.ShapeDtypeStruct((t, w), BF16) for w in widths]
    out_specs = [pl.BlockSpec((tm, w), lambda i: (i, 0)) for w in widths]
    return pl.pallas_call(
        functools.partial(_inproj_kernel, on_grid=on_grid),
        grid=(t // tm,),
        in_specs=[pl.BlockSpec((tm, D_MODEL), lambda i: (i, 0)),
                  pl.BlockSpec((1, 6, D_MODEL), lambda i: (i // tiles_per_mod, 0, 0)),
                  pl.BlockSpec((1, D_MODEL), lambda i: (0, 0)),
                  pl.BlockSpec((D_MODEL, IN_W), lambda i: (0, 0)),
                  pl.BlockSpec((tm, RET_DK), lambda i: (i % tiles_per_seq, 0)),
                  pl.BlockSpec((tm, RET_DK), lambda i: (i % tiles_per_seq, 0))],
        out_specs=out_specs,
        out_shape=out_shape,
        compiler_params=_cparams("parallel"),
        name="inproj_grid" if on_grid else "inproj_seq",
    )(x, mods, norm_g, w_in, cos_t, sin_t)


def _rope_tables(seq_len):
    t = np.arange(seq_len)
    row = (t // GRID_W).astype(np.float32)
    col = (t % GRID_W).astype(np.float32)
    m = RET_DK // 4
    inv = (np.float32(ROPE_BASE) ** (-np.arange(m, dtype=np.float32) / np.float32(m))).astype(np.float32)
    ar = row[:, None] * inv
    ac = col[:, None] * inv
    cos = np.concatenate([np.cos(ar), np.cos(ar), np.cos(ac), np.cos(ac)], axis=1)
    sin = np.concatenate([-np.sin(ar), np.sin(ar), -np.sin(ac), np.sin(ac)], axis=1)
    return jnp.asarray(cos, F32), jnp.asarray(sin, F32)


def _ret_heads_per_step(seq_len):
    per_head = seq_len * (2 * 2 * (2 * RET_DK + 3 * RET_DV) + 4 * RET_DV + 2 * RET_DK)
    heads = RET_HEADS
    while heads > 1 and heads * per_head > VMEM_LIMIT * 3 // 4:
        heads //= 2
    return heads


def _ret_kernel(dec_ref, q_ref, k_ref, v_ref, g_ref, s0f_ref, s0b_ref,
                z_ref, sf_ref, sb_ref, oacc_ref, kt_ref, *, n_chunks, heads):
    c = RET_CHUNK
    half = n_chunks // 2
    ii = lax.broadcasted_iota(I32, (c, c), 0)
    jj = lax.broadcasted_iota(I32, (c, c), 1)
    ik = lax.broadcasted_iota(I32, (c, RET_DK), 0).astype(F32)
    jk = lax.broadcasted_iota(I32, (RET_DK, c), 1).astype(F32)

    def log_gamma(d, shape):
        return jnp.log1p(-jnp.exp2(-jnp.full(shape, d, F32)))

    consts = {}
    for hh in range(heads):
        h = pl.program_id(1) * heads + hh
        dec_f = dec_ref[0, h]
        dec_b = dec_ref[1, h]
        rel = (ii - jj).astype(F32)
        consts[hh, "f"] = (
            jnp.where(rel >= 0, jnp.exp(log_gamma(dec_f, (c, c)) * jnp.maximum(rel, 0.0)), 0.0),
            jnp.exp(log_gamma(dec_f, (c, RET_DK)) * (ik + 1.0)),
            jnp.exp(log_gamma(dec_f, (RET_DK, c)) * (c - 1.0 - jk)),
            jnp.exp(log_gamma(dec_f, (RET_DK, RET_DV)) * c))
        consts[hh, "b"] = (
            jnp.where(rel <= 0, jnp.exp(log_gamma(dec_b, (c, c)) * jnp.maximum(-rel, 0.0)), 0.0),
            jnp.exp(log_gamma(dec_b, (c, RET_DK)) * (c - ik)),
            jnp.exp(log_gamma(dec_b, (RET_DK, c)) * jk),
            jnp.exp(log_gamma(dec_b, (RET_DK, RET_DV)) * c))

    sf_ref[...] = s0f_ref[...]
    sb_ref[...] = s0b_ref[...]

    def transpose_keys(ci, carry):
        r = pl.ds(pl.multiple_of(ci * c, c), c)
        for hh in range(heads):
            kt_ref[hh, ci] = k_ref[r, hh * RET_DK:(hh + 1) * RET_DK].T
        return carry

    lax.fori_loop(0, n_chunks, transpose_keys, 0)

    def scores(ci, hh, direction):
        r = pl.ds(pl.multiple_of(ci * c, c), c)
        kcols = slice(hh * RET_DK, (hh + 1) * RET_DK)
        qc = q_ref[r, kcols]
        sc = lax.dot_general(qc, k_ref[r, kcols], (((1,), (1,)), ((), ())), preferred_element_type=F32)
        return ci, hh, direction, r, qc, sc

    def advance(job):
        ci, hh, direction, r, qc, sc = job
        dmask, qdec, kdec, cdec = consts[hh, direction]
        s_ref = sf_ref if direction == "f" else sb_ref
        vc = v_ref[r, hh * RET_DV:(hh + 1) * RET_DV]
        s = s_ref[hh]
        lhs = jnp.concatenate([(sc * dmask).astype(BF16), (qc.astype(F32) * qdec).astype(BF16)], axis=1)
        o = _dot(lhs, jnp.concatenate([vc, s.astype(BF16)], axis=0))
        kd_t = (kt_ref[hh, ci].astype(F32) * kdec).astype(BF16)
        s_ref[hh] = s * cdec + _dot(kd_t, vc)
        return o

    def emit(job, o, second):
        _, hh, _, r, _, _ = job
        vcols = slice(hh * RET_DV, (hh + 1) * RET_DV)
        if not second:
            oacc_ref[hh, r, :] = o
        else:
            o = o + oacc_ref[hh, r, :]
            o = o * lax.rsqrt(jnp.mean(o * o, axis=-1, keepdims=True) + EPS)
            g = g_ref[r, vcols].astype(F32)
            z_ref[r, vcols] = (_silu(g) * o).astype(BF16)

    def body(second):
        def run(t, carry):
            jobs = [scores(ci, hh, d) for hh in range(heads)
                    for ci, d in ((t, "f"), (n_chunks - 1 - t, "b"))]
            outs = [advance(job) for job in jobs]
            for job, o in zip(jobs, outs):
                emit(job, o, second)
            return carry
        return run

    lax.fori_loop(0, half, body(False), 0, unroll=4 if half % 4 == 0 else 1)
    lax.fori_loop(half, n_chunks, body(True), 0, unroll=2 if half % 2 == 0 else 1)


def _retention(q, k, v, gsw, dec, s0f, s0b, *, batch, seq_len):
    n_chunks = seq_len // RET_CHUNK
    assert n_chunks % 2 == 0
    heads = _ret_heads_per_step(seq_len)
    t = batch * seq_len
    st_spec = pl.BlockSpec((None, heads, RET_DK, RET_DV), lambda b, h: (b, h, 0, 0))
    st_shape = jax.ShapeDtypeStruct((batch, RET_HEADS, RET_DK, RET_DV), F32)
    kspec = pl.BlockSpec((seq_len, heads * RET_DK), lambda b, h: (b, h))
    vspec = pl.BlockSpec((seq_len, heads * RET_DV), lambda b, h: (b, h))
    return pl.pallas_call(
        functools.partial(_ret_kernel, n_chunks=n_chunks, heads=heads),
        grid=(batch, RET_HEADS // heads),
        in_specs=[pl.BlockSpec(memory_space=pltpu.SMEM), kspec, kspec, vspec, vspec, st_spec, st_spec],
        out_specs=[vspec, st_spec, st_spec],
        out_shape=[jax.ShapeDtypeStruct((t, RET_V_W), BF16), st_shape, st_shape],
        scratch_shapes=[pltpu.VMEM((heads, seq_len, RET_DV), F32),
                        pltpu.VMEM((heads, n_chunks, RET_DK, RET_CHUNK), BF16)],
        compiler_params=_cparams("parallel", "parallel"),
        name=f"retention_l{seq_len}",
    )(dec, q, k, v, gsw, s0f, s0b)


def _pool_kernel(u_ref, w_ref, sc_ref, o_ref, *, n_tok, width, two_d):
    n_rows = n_tok // width
    pos = lax.broadcasted_iota(I32, (width, POOL_CH), 0)

    def every_row(a):
        return jnp.concatenate([a] * n_rows, axis=0) if n_rows > 1 else a

    def shift_in_row(a, s):
        ok = (pos < width - s) if s > 0 else (pos >= -s)
        return pltpu.roll(a, (-s) % n_tok, axis=0) * every_row(jnp.where(ok, 1.0, 0.0))

    def shift_rows(a, m):
        k = abs(m) * width
        zeros = jnp.zeros((k, POOL_CH), F32)
        return (jnp.concatenate([a[k:], zeros], axis=0) if m > 0
                else jnp.concatenate([zeros, a[:n_tok - k]], axis=0))

    def box_sum(a, half, shift):
        fw = a
        bw = shift(a, -1)
        m = 1
        while m < half:
            fw = fw + shift(fw, m)
            bw = bw + shift(bw, -m)
            m *= 2
        return fw + bw

    def inv_count(p, half, extent):
        return 1.0 / (jnp.minimum(p + half, extent) - jnp.maximum(p - half, 0)).astype(F32)

    for g, window in enumerate(POOL_WINDOWS):
        half = window // 2
        cols = slice(g * POOL_CH, (g + 1) * POOL_CH)
        ug = u_ref[:, cols].astype(F32)
        total = box_sum(ug, half, shift_in_row)
        inv = every_row(inv_count(pos, half, width))
        if two_d:
            total = box_sum(total, half, shift_rows)
            row = lax.broadcasted_iota(I32, (n_rows, 1, POOL_CH), 0)
            inv_r = jnp.broadcast_to(inv_count(row, half, n_rows), (n_rows, width, POOL_CH))
            inv = inv * inv_r.reshape(n_tok, POOL_CH)
        d = (total * inv - ug).astype(BF16)
        o_ref[:, cols] = (_dot(d, w_ref[g]) * sc_ref[:, cols]).astype(BF16)


def _pool(u, pool_w, pool_scale, *, batch, seq_len, on_grid):
    t = batch * seq_len
    width = GRID_W if on_grid else seq_len
    n_tok = seq_len if on_grid else seq_len * math.gcd(batch, 4)
    return pl.pallas_call(
        functools.partial(_pool_kernel, n_tok=n_tok, width=width, two_d=on_grid),
        grid=(t // n_tok,),
        in_specs=[pl.BlockSpec((n_tok, POOL_W), lambda b: (b, 0)),
                  pl.BlockSpec((POOL_GROUPS, POOL_CH, POOL_CH), lambda b: (0, 0, 0)),
                  pl.BlockSpec((1, POOL_W), lambda b: (0, 0))],
        out_specs=pl.BlockSpec((n_tok, POOL_W), lambda b: (b, 0)),
        out_shape=jax.ShapeDtypeStruct((t, POOL_W), BF16),
        compiler_params=_cparams("parallel"),
        name=f"pool_l{seq_len}",
    )(u, pool_w, pool_scale)


def _pack_rows(x):
    half = D_MODEL // 2
    lo = lax.bitcast_convert_type(x[:, :half].astype(BF16).astype(F32), U32) >> 16
    hi = lax.bitcast_convert_type(x[:, half:].astype(BF16).astype(F32), U32) & jnp.uint32(0xFFFF0000)
    word = lax.bitcast_convert_type(hi | lo, I32)
    return [word[:, c * LANES:(c + 1) * LANES] for c in range(N_PIECES)]


def _unpack_rows(pieces):
    words = [lax.bitcast_convert_type(p, U32) for p in pieces]
    lo = [lax.bitcast_convert_type(w << 16, F32) for w in words]
    hi = [lax.bitcast_convert_type(w & jnp.uint32(0xFFFF0000), F32) for w in words]
    return lo, hi


def _merge_kernel(x_ref, z_ref, p_ref, ga_ref, gb_ref, mod_ref, g2_ref, wr_ref, wp_ref, wo_ref,
                  x1_ref, *piece_refs):
    y_ret = _dot(z_ref[...], wr_ref[...])
    y_pool = _dot(p_ref[...], wp_ref[...])
    merged = (jax.nn.sigmoid(ga_ref[...].astype(F32)) * y_ret
              + jax.nn.sigmoid(gb_ref[...].astype(F32)) * y_pool)
    x1 = x_ref[...] + mod_ref[0, 2:3, :] * _dot(merged.astype(BF16), wo_ref[...])
    x1_ref[...] = x1
    h2 = _rms_mod(x1, g2_ref[...], mod_ref[0, 4:5, :], mod_ref[0, 3:4, :])
    for ref, piece in zip(piece_refs, _pack_rows(h2)):
        ref[...] = piece


def _merge(x, z, p, ga, gb, mods, norm2_g, w_br_ret, w_br_pool, w_out, *, tokens_per_mod, tm):
    t = x.shape[0]
    tiles_per_mod = tokens_per_mod // tm
    row = lambda w: pl.BlockSpec((tm, w), lambda i: (i, 0))
    full = lambda a: pl.BlockSpec(a.shape, lambda i: (0,) * a.ndim)
    outs = pl.pallas_call(
        _merge_kernel,
        grid=(t // tm,),
        in_specs=[row(D_MODEL), row(RET_V_W), row(POOL_W), row(D_MODEL), row(D_MODEL),
                  pl.BlockSpec((1, 6, D_MODEL), lambda i: (i // tiles_per_mod, 0, 0)),
                  full(norm2_g), full(w_br_ret), full(w_br_pool), full(w_out)],
        out_specs=[row(D_MODEL)] + [row(LANES)] * N_PIECES,
        out_shape=[jax.ShapeDtypeStruct((t, D_MODEL), F32)] + [jax.ShapeDtypeStruct((t, LANES), I32)] * N_PIECES,
        compiler_params=_cparams("parallel"),
        name="merge",
    )(x, z, p, ga, gb, mods, norm2_g, w_br_ret, w_br_pool, w_out)
    return outs[0], outs[1:]


def _route_kernel(*refs):
    h_refs = refs[:N_PIECES]
    rw_ref, bias_ref, idx_ref, rank_ref, wk_ref, cnt_ref, carry_ref = refs[N_PIECES:]
    e = N_EXPERTS
    tm = h_refs[0].shape[0]
    neg = -jnp.inf

    @pl.when(pl.program_id(0) == 0)
    def _():
        carry_ref[...] = jnp.zeros(carry_ref.shape, F32)

    lo, hi = _unpack_rows([r[...] for r in h_refs])
    h = jnp.concatenate(lo + hi, axis=1).astype(BF16)
    logits = lax.dot_general(rw_ref[...], h, (((1,), (1,)), ((), ())), preferred_element_type=F32)[:e]
    scores = jax.nn.sigmoid(logits)
    sel = scores + bias_ref[:e, 0:1]
    e_idx = lax.broadcasted_iota(I32, (e, tm), 0)

    grp = sel.reshape(N_EXPERT_GROUPS, GROUP_SIZE, tm)
    m_idx = lax.broadcasted_iota(I32, grp.shape, 1)
    m1 = jnp.max(grp, axis=1, keepdims=True)
    first = jnp.min(jnp.where(grp == m1, m_idx, GROUP_SIZE), axis=1, keepdims=True)
    m2 = jnp.max(jnp.where(m_idx == first, neg, grp), axis=1, keepdims=True)
    gscore = (m1 + m2).reshape(N_EXPERT_GROUPS, tm)

    g_idx = lax.broadcasted_iota(I32, gscore.shape, 0)
    grank = jnp.zeros(gscore.shape, I32)
    for g in range(N_EXPERT_GROUPS):
        other = gscore[g:g + 1, :]
        beats = jnp.where(other > gscore, 1, jnp.where(other == gscore, (g_idx > g).astype(I32), 0))
        grank = grank + beats
    gkeep = (grank < TOPK_GROUPS).astype(F32)
    ekeep = jnp.broadcast_to(gkeep.reshape(N_EXPERT_GROUPS, 1, tm), grp.shape).reshape(e, tm)
    masked = jnp.where(ekeep > 0, sel, neg)

    chosen = jnp.zeros((e, tm), F32)
    picks, hits = [], []
    for _ in range(TOP_K):
        m = jnp.max(masked, axis=0, keepdims=True)
        pick = jnp.min(jnp.where(masked == m, e_idx, e), axis=0, keepdims=True)
        hit = e_idx == pick
        chosen = jnp.where(hit, 1.0, chosen)
        masked = jnp.where(hit, neg, masked)
        picks.append(pick)
        hits.append(hit)

    w = scores * chosen
    comb = w / jnp.sum(w, axis=0, keepdims=True) * ROUTED_SCALE

    t_row = lax.broadcasted_iota(I32, (tm, tm), 0)
    t_col = lax.broadcasted_iota(I32, (tm, tm), 1)
    before = (t_row < t_col).astype(BF16)
    rankmat = _dot(chosen.astype(BF16), before) + carry_ref[:e, 0:1]
    carry_ref[:e, :] = carry_ref[:e, :] + jnp.sum(chosen, axis=1, keepdims=True)
    cnt_ref[...] = carry_ref[...]

    idx_ref[...] = jnp.concatenate(picks, axis=0)
    rank_ref[...] = jnp.concatenate(
        [jnp.sum(jnp.where(h, rankmat, 0.0), axis=0, keepdims=True) for h in hits], axis=0).astype(I32)
    wk_ref[...] = jnp.concatenate(
        [jnp.sum(jnp.where(h, comb, 0.0), axis=0, keepdims=True) for h in hits], axis=0)


def _route(h2_pieces, router_wt, bias_col, *, tm):
    t = h2_pieces[0].shape[0]
    krow = pl.BlockSpec((TOP_K, tm), lambda i: (0, i))
    return pl.pallas_call(
        _route_kernel,
        grid=(t // tm,),
        in_specs=[pl.BlockSpec((tm, LANES), lambda i: (i, 0))] * N_PIECES
        + [pl.BlockSpec((LANES, D_MODEL), lambda i: (0, 0)), pl.BlockSpec((LANES, 1), lambda i: (0, 0))],
        out_specs=[krow, krow, krow, pl.BlockSpec((LANES, LANES), lambda i: (0, 0))],
        out_shape=[jax.ShapeDtypeStruct((TOP_K, t), I32), jax.ShapeDtypeStruct((TOP_K, t), I32),
                   jax.ShapeDtypeStruct((TOP_K, t), F32), jax.ShapeDtypeStruct((LANES, LANES), F32)],
        scratch_shapes=[pltpu.VMEM((LANES, LANES), F32)],
        compiler_params=_cparams("arbitrary"),
        name="route",
    )(*h2_pieces, router_wt, bias_col)


def _plan_kernel(idx_ref, rank_ref, cnt_ref, pos_ref, te_ref, nu_ref, *, group_tile, min_tiles):
    tf = idx_ref.shape[1]
    nt = te_ref.shape[1]
    cnt = cnt_ref[...].astype(I32)
    tiles = jnp.maximum((cnt + (group_tile - 1)) // group_tile, min_tiles)
    e_sub = lax.broadcasted_iota(I32, (LANES, LANES), 0)
    padded = jnp.where(e_sub < N_EXPERTS, tiles * group_tile, 0).astype(F32)
    e_lane = lax.broadcasted_iota(I32, (LANES, LANES), 1)
    base = jnp.sum(jnp.where(e_lane < e_sub, padded.T, 0.0), axis=1, keepdims=True)
    end = base + padded[:, 0:1]

    idx = idx_ref[...]
    start = jnp.zeros(idx.shape, F32)
    for e in range(N_EXPERTS):
        start = jnp.where(idx == e, base[e:e + 1, 0:1], start)
    pos = start.astype(I32) + rank_ref[...]
    for j in range(tf // SC_CHUNK):
        pos_ref[j] = pos[:, j * SC_CHUNK:(j + 1) * SC_CHUNK]

    tile_start = (lax.broadcasted_iota(I32, (N_EXPERTS, nt), 1) * group_tile).astype(F32)
    done = jnp.sum(jnp.where(end[:N_EXPERTS] <= tile_start, 1.0, 0.0), axis=0, keepdims=True)
    te_ref[...] = jnp.minimum(done, N_EXPERTS - 1.0).astype(I32)
    total = jnp.sum(padded[:, 0:1], axis=0, keepdims=True)
    nu_ref[...] = jnp.broadcast_to(total * (1.0 / group_tile), nu_ref.shape).astype(I32)


def _plan(idx, rank, counts, *, n_tiles, tf, group_tile, min_tiles):
    t = idx.shape[1]
    nt_pad = -(-n_tiles // LANES) * LANES
    krow = pl.BlockSpec((TOP_K, tf), lambda i: (0, i))
    return pl.pallas_call(
        functools.partial(_plan_kernel, group_tile=group_tile, min_tiles=min_tiles),
        grid=(t // tf,),
        in_specs=[krow, krow, pl.BlockSpec((LANES, LANES), lambda i: (0, 0))],
        out_specs=[pl.BlockSpec((tf // SC_CHUNK, TOP_K, SC_CHUNK), lambda i: (i, 0, 0)),
                   pl.BlockSpec((1, nt_pad), lambda i: (0, 0)),
                   pl.BlockSpec((1, LANES), lambda i: (0, 0))],
        out_shape=[jax.ShapeDtypeStruct((t // SC_CHUNK, TOP_K, SC_CHUNK), I32),
                   jax.ShapeDtypeStruct((1, nt_pad), I32), jax.ShapeDtypeStruct((1, LANES), I32)],
        compiler_params=_cparams("arbitrary"),
        name="moe_plan",
    )(idx, rank, counts)


def _sc_mesh_info():
    info = plsc.get_sparse_core_info()
    mesh = plsc.VectorSubcoreMesh(core_axis_name="c", subcore_axis_name="s")
    return mesh, info.num_cores, info.num_cores * info.num_subcores


def _sc_dispatch(pieces, pos, *, n_rows):
    t = pieces[0].shape[0]
    mesh, n_cores, n_workers = _sc_mesh_info()
    per_w = t // SC_CHUNK // n_workers

    @functools.partial(
        pl.kernel, mesh=mesh,
        out_type=[jax.ShapeDtypeStruct((n_rows, LANES), I32)] * N_PIECES,
        scratch_types=[pltpu.VMEM((TOP_K, SC_CHUNK), I32),
                       pltpu.VMEM((N_PIECES, SC_CHUNK, LANES), I32),
                       pltpu.SemaphoreType.DMA((N_PIECES,)),
                       pltpu.SemaphoreType.DMA],
        name="sc_dispatch",
    )
    def run(*refs):
        src = refs[:N_PIECES]
        pos_hbm = refs[N_PIECES]
        dst = refs[N_PIECES + 1:2 * N_PIECES + 1]
        idx_v, rows_v, load_sem, put_sem = refs[2 * N_PIECES + 1:]
        wid = lax.axis_index("s") * n_cores + lax.axis_index("c")

        @pl.loop(0, per_w)
        def _(j):
            ch = wid * per_w + j
            t0 = pl.multiple_of(ch * SC_CHUNK, SC_CHUNK)
            loads = [pltpu.make_async_copy(src[c].at[pl.ds(t0, SC_CHUNK)], rows_v.at[c], load_sem.at[c])
                     for c in range(N_PIECES)]
            for ld in loads:
                ld.start()
            pltpu.sync_copy(pos_hbm.at[ch], idx_v)
            puts = []
            for c in range(N_PIECES):
                loads[c].wait()
                for k in range(TOP_K):
                    puts.append(pltpu.make_async_copy(rows_v.at[c], dst[c].at[idx_v.at[k]], put_sem))
                    puts[-1].start()
            for cp in puts:
                cp.wait()

    return run(*pieces, pos)


SC_GROUP = 32


def _sc_combine(pieces, pos, wts, *, n_tokens):
    mesh, n_cores, n_workers = _sc_mesh_info()
    lanes = plsc.get_sparse_core_info().num_lanes
    per_w = n_tokens // SC_GROUP // n_workers

    @functools.partial(
        pl.kernel, mesh=mesh,
        out_type=[jax.ShapeDtypeStruct((n_tokens, LANES), F32)] * (2 * N_PIECES),
        scratch_types=[pltpu.VMEM((TOP_K, SC_GROUP), I32),
                       pltpu.VMEM((TOP_K, SC_GROUP), F32),
                       pltpu.VMEM((2, TOP_K, SC_GROUP, LANES), I32),
                       pltpu.VMEM((2, 2, SC_GROUP, LANES), F32),
                       pltpu.SemaphoreType.DMA((2,)),
                       pltpu.SemaphoreType.DMA((2,))],
        compiler_params=pltpu.CompilerParams(needs_layout_passes=False),
        name="sc_combine",
    )
    def run(*refs):
        src = refs[:N_PIECES]
        pos_hbm, wts_hbm = refs[N_PIECES:N_PIECES + 2]
        dst = refs[N_PIECES + 2:3 * N_PIECES + 2]
        idx_v, w_v, buf, acc, get_sem, put_sem = refs[3 * N_PIECES + 2:]
        wid = lax.axis_index("s") * n_cores + lax.axis_index("c")

        @pl.loop(0, per_w)
        def _(j):
            grp = wid * per_w + j
            t0 = pl.multiple_of(grp * SC_GROUP, SC_GROUP)
            pltpu.sync_copy(pos_hbm.at[grp], idx_v)
            pltpu.sync_copy(wts_hbm.at[grp], w_v)

            def gets(c, slot):
                return [pltpu.make_async_copy(src[c].at[idx_v.at[k]], buf.at[slot, k], get_sem.at[slot])
                        for k in range(TOP_K)]

            def puts(c, slot):
                r = pl.ds(t0, SC_GROUP)
                return [pltpu.make_async_copy(acc.at[slot, 0], dst[c].at[r], put_sem.at[slot]),
                        pltpu.make_async_copy(acc.at[slot, 1], dst[N_PIECES + c].at[r], put_sem.at[slot])]

            for cp in gets(0, 0):
                cp.start()
            for c in range(N_PIECES):
                slot = c % 2
                if c + 1 < N_PIECES:
                    for cp in gets(c + 1, 1 - slot):
                        cp.start()
                for cp in gets(c, slot):
                    cp.wait()
                if c >= 2:
                    for cp in puts(c - 2, slot):
                        cp.wait()

                @pl.loop(0, SC_GROUP)
                def _(r):
                    row = jnp.full((lanes,), r, I32)
                    w = [plsc.load_gather(w_v, [jnp.full((lanes,), k, I32), row]) for k in range(TOP_K)]
                    for q in range(LANES // lanes):
                        cols = pl.ds(q * lanes, lanes)
                        lo = hi = None
                        for k in range(TOP_K):
                            word = buf[slot, k, r, cols]
                            lo_k = plsc.bitcast(word << 16, F32) * w[k]
                            hi_k = plsc.bitcast(word & jnp.int32(-65536), F32) * w[k]
                            lo = lo_k if lo is None else lo + lo_k
                            hi = hi_k if hi is None else hi + hi_k
                        acc[slot, 0, r, cols] = lo
                        acc[slot, 1, r, cols] = hi

                for cp in puts(c, slot):
                    cp.start()
            for c in range(N_PIECES - 2, N_PIECES):
                for cp in puts(c, c % 2):
                    cp.wait()

    return run(*pieces, pos, wts)


def _group_tile(n_tokens):
    per_expert = n_tokens * TOP_K // N_EXPERTS
    return max(MXU_DIM, min(4 * MXU_DIM, per_expert // MXU_DIM * MXU_DIM))


SECOND_DMA_QUEUE = 1


def _experts_kernel(te_ref, nu_ref, *refs, cast_weights, tile):
    x_hbm = refs[:N_PIECES]
    w_in = refs[N_PIECES:N_PIECES + 3]
    y_hbm = refs[N_PIECES + 3:2 * N_PIECES + 3]
    xbuf, xsem, ybuf, ysem = refs[-4:]
    i = pl.program_id(0)
    n_used = nu_ref[0]
    slot = lax.rem(i, 2)

    def x_copies(step, into):
        r = pl.ds(pl.multiple_of(step * tile, tile), tile)
        return [pltpu.make_async_copy(x_hbm[c].at[r], xbuf.at[into, c], xsem.at[into, c])
                for c in range(N_PIECES)]

    def y_copies(step, out_of):
        r = pl.ds(pl.multiple_of(step * tile, tile), tile)
        return [pltpu.make_async_copy(ybuf.at[out_of, c], y_hbm[c].at[r], ysem.at[out_of, c])
                for c in range(N_PIECES)]

    @pl.when(i == 0)
    def _():
        for cp in x_copies(0, 0):
            cp.start(priority=SECOND_DMA_QUEUE)

    @pl.when(i + 1 < n_used)
    def _():
        for cp in x_copies(i + 1, 1 - slot):
            cp.start(priority=SECOND_DMA_QUEUE)

    if cast_weights:
        w_bf = refs[2 * N_PIECES + 3:-4]
        last = n_used - 1
        cur = te_ref[jnp.minimum(i, last)]
        prev = te_ref[jnp.minimum(jnp.maximum(i - 1, 0), last)]

        @pl.when((i == 0) | (cur != prev))
        def _():
            for dst, src in zip(w_bf, w_in):
                dst[...] = src[...].astype(BF16)
    else:
        w_bf = w_in
    wg_ref, wu_ref, wd_ref = w_bf

    @pl.when(i < n_used)
    def _():
        for cp in x_copies(i, slot):
            cp.wait()
        subs = [slice(s * MXU_DIM, (s + 1) * MXU_DIM) for s in range(tile // MXU_DIM)]
        xs = []
        for rows in subs:
            lo, hi = _unpack_rows([xbuf[slot, c, rows, :] for c in range(N_PIECES)])
            xs.append(jnp.concatenate(lo + hi, axis=1).astype(BF16))
        gates = [(_dot(x, wg_ref[...]), _dot(x, wu_ref[...])) for x in xs]
        ys = [_dot((_silu(g) * u).astype(BF16), wd_ref[...]) for g, u in gates]
        for rows, y in zip(subs, ys):
            for c, piece in enumerate(_pack_rows(y)):
                ybuf[slot, c, rows, :] = piece

        @pl.when(i >= 1)
        def _():
            for cp in y_copies(i - 1, 1 - slot):
                cp.wait()

        for cp in y_copies(i, slot):
            cp.start(priority=SECOND_DMA_QUEUE)

        @pl.when(i == n_used - 1)
        def _():
            for cp in y_copies(i, slot):
                cp.wait()


def _experts(x_pieces, tile_expert, n_used, wg, wu, wd, *, group_tile):
    n_rows = x_pieces[0].shape[0]
    n_tiles = n_rows // group_tile
    cast_weights = wg.dtype != BF16

    def tile(i, te, nu):
        return jnp.minimum(i, nu[0] - 1)

    wspec = lambda a: pl.BlockSpec((None,) + a.shape[1:], lambda i, te, nu: (te[tile(i, te, nu)], 0, 0))
    w_specs = [wspec(wg), wspec(wu), wspec(wd)]
    y_shape = [jax.ShapeDtypeStruct((n_rows, LANES), I32)] * N_PIECES
    w_shape = [jax.ShapeDtypeStruct(a.shape, BF16) for a in (wg, wu, wd)]
    outs = pl.pallas_call(
        functools.partial(_experts_kernel, cast_weights=cast_weights, tile=group_tile),
        grid_spec=pltpu.PrefetchScalarGridSpec(
            num_scalar_prefetch=2,
            grid=(n_tiles,),
            in_specs=[pl.BlockSpec(memory_space=pl.ANY)] * N_PIECES + w_specs,
            out_specs=[pl.BlockSpec(memory_space=pl.ANY)] * N_PIECES + (w_specs if cast_weights else []),
            scratch_shapes=[pltpu.VMEM((2, N_PIECES, group_tile, LANES), I32),
                            pltpu.SemaphoreType.DMA((2, N_PIECES))] * 2),
        out_shape=y_shape + (w_shape if cast_weights else []),
        compiler_params=_cparams("arbitrary"),
        name="experts_cast" if cast_weights else "experts",
    )(tile_expert, n_used, *x_pieces, wg, wu, wd)
    return outs[:N_PIECES], (tuple(outs[N_PIECES:]) if cast_weights else (wg, wu, wd))


def _moe_out_kernel(x1_ref, mod_ref, fg_ref, sg_ref, su_ref, sd_ref, *refs):
    h_refs = refs[:N_PIECES]
    routed_refs = refs[N_PIECES:3 * N_PIECES]
    out_ref = refs[3 * N_PIECES]
    h_lo, h_hi = _unpack_rows([r[...] for r in h_refs])
    h = jnp.concatenate(h_lo + h_hi, axis=1).astype(BF16)
    hid = _silu(_dot(h, sg_ref[...])) * _dot(h, su_ref[...])
    shared = _dot(hid.astype(BF16), sd_ref[...])
    routed = jnp.concatenate([r[...] for r in routed_refs], axis=1)
    x2 = x1_ref[...] + mod_ref[0, 5:6, :] * (shared + routed)
    out_ref[...] = x2 * lax.rsqrt(jnp.mean(x2 * x2, axis=-1, keepdims=True) + EPS) * fg_ref[...]


def _moe_out(h2_pieces, x1, mods, final_g, sg, su, sd, routed_pieces, *, tokens_per_mod, tm):
    t = x1.shape[0]
    tiles_per_mod = tokens_per_mod // tm
    row = lambda w: pl.BlockSpec((tm, w), lambda i: (i, 0))
    full = lambda a: pl.BlockSpec(a.shape, lambda i: (0,) * a.ndim)
    return pl.pallas_call(
        _moe_out_kernel,
        grid=(t // tm,),
        in_specs=[row(D_MODEL),
                  pl.BlockSpec((1, 6, D_MODEL), lambda i: (i // tiles_per_mod, 0, 0)),
                  full(final_g), full(sg), full(su), full(sd)]
        + [row(LANES)] * (3 * N_PIECES),
        out_specs=row(D_MODEL),
        out_shape=jax.ShapeDtypeStruct((t, D_MODEL), F32),
        compiler_params=_cparams("parallel"),
        name="moe_out",
    )(x1, mods, final_g, sg, su, sd, *h2_pieces, *routed_pieces)


def _trunk(x, mods, s0f, s0b, w, expert_w, *, batch, seq_len, on_grid):
    t = batch * seq_len
    tokens_per_mod = t // mods.shape[0]
    cos_t, sin_t = _rope_tables(seq_len)
    q, k, v, gsw, up, ga, gb = _inproj(x, mods, w["norm1_g"], w["w_in"], cos_t, sin_t,
                                       tokens_per_mod=tokens_per_mod, seq_len=seq_len,
                                       on_grid=on_grid, tm=256)
    z, s_f, s_b = _retention(q, k, v, gsw, w["dec"], s0f, s0b, batch=batch, seq_len=seq_len)
    p = _pool(up, w["pool_w"], w["pool_scale"], batch=batch, seq_len=seq_len, on_grid=on_grid)
    x1, h2_pieces = _merge(x, z, p, ga, gb, mods, w["norm2_g"], w["w_br_ret"], w["w_br_pool"],
                               w["w_out"], tokens_per_mod=tokens_per_mod, tm=512)

    group_tile = _group_tile(t)
    n_rows = t * TOP_K + N_EXPERTS * group_tile
    idx, rank, wts, counts = _route(h2_pieces, w["router_wt"], w["router_bias"], tm=1024)
    min_tiles = int(expert_w[0].dtype != BF16)
    pos, tile_expert, n_used = _plan(idx, rank, counts, n_tiles=n_rows // group_tile, tf=2048,
                                     group_tile=group_tile, min_tiles=min_tiles)
    x_sorted = _sc_dispatch(h2_pieces, pos, n_rows=n_rows)
    y_sorted, expert_w = _experts(x_sorted, tile_expert.reshape(-1), n_used.reshape(-1), *expert_w,
                                  group_tile=group_tile)
    regroup = lambda a: a.reshape(TOP_K, t // SC_GROUP, SC_GROUP).transpose(1, 0, 2)
    pos_rows = pos.transpose(1, 0, 2).reshape(TOP_K, t)
    routed = _sc_combine(y_sorted, regroup(pos_rows), regroup(wts), n_tokens=t)
    y = _moe_out(h2_pieces, x1, mods, w["final_g"], w["sh_w_gate"], w["sh_w_up"], w["sh_w_down"], routed,
                 tokens_per_mod=tokens_per_mod, tm=512)
    return y, s_f, s_b, expert_w


def kernel(x_prompt, x_sample, state_ret_fwd, state_ret_bwd, c, c_ctx, ada_w, ada_b, norm1_g, norm2_g, w_in,
           ret_decay_fwd, ret_decay_bwd, w_br_ret, pool_w, pool_scale, w_br_pool, w_out, router_w, router_bias,
           exp_w_gate, exp_w_up, exp_w_down, sh_w_gate, sh_w_up, sh_w_down, final_norm_g):
    n_req, seq, d = x_prompt.shape
    n_dec, dec_seq, _ = x_sample.shape
    depth = ada_w.shape[0]
    assert depth == 1 and d == D_MODEL

    xc = x_prompt.reshape(n_req * seq, d)
    xs = x_sample.reshape(n_dec * dec_seq, d)
    zero_state = jnp.zeros((n_req, RET_HEADS, RET_DK, RET_DV), F32)
    new_f, new_b = [], []
    for l in range(depth):
        c_rows = jnp.concatenate([c_ctx[None, :], c, jnp.zeros((8 - 1 - n_dec, d), F32)], axis=0)
        mods = _ada(c_rows, ada_w[l], ada_b[l]).reshape(8, 6, d)
        pad_rows = LANES - N_EXPERTS
        w = dict(
            norm1_g=norm1_g[l].reshape(1, d), norm2_g=norm2_g[l].reshape(1, d),
            final_g=final_norm_g.reshape(1, d),
            w_in=w_in[l].astype(BF16),
            dec=jnp.stack([ret_decay_fwd[l], ret_decay_bwd[l]]).astype(F32),
            w_br_ret=w_br_ret[l].astype(BF16), pool_w=pool_w[l].astype(BF16),
            pool_scale=pool_scale[l].reshape(1, POOL_W), w_br_pool=w_br_pool[l].astype(BF16),
            w_out=w_out[l].astype(BF16),
            router_wt=jnp.pad(router_w[l].T, ((0, pad_rows), (0, 0))).astype(BF16),
            router_bias=jnp.pad(router_bias[l].astype(F32).reshape(N_EXPERTS, 1), ((0, pad_rows), (0, 0))),
            sh_w_gate=sh_w_gate[l].astype(BF16),
            sh_w_up=sh_w_up[l].astype(BF16), sh_w_down=sh_w_down[l].astype(BF16),
        )
        expert_w = (exp_w_gate[l], exp_w_up[l], exp_w_down[l])
        xs, _, _, expert_w = _trunk(xs, mods[1:1 + n_dec], state_ret_fwd[:, l].astype(F32),
                                    state_ret_bwd[:, l].astype(F32), w, expert_w,
                                    batch=n_dec, seq_len=dec_seq, on_grid=True)
        xc, s_f, s_b, _ = _trunk(xc, mods[0:1], zero_state, zero_state, w, expert_w,
                                 batch=n_req, seq_len=seq, on_grid=False)
        new_f.append(s_f)
        new_b.append(s_b)
    y_prompt = xc.reshape(n_req, seq, d)
    y_sample = xs.reshape(n_dec, dec_seq, d)
    return (y_prompt, y_sample, jnp.stack(new_f, axis=1).astype(x_prompt.dtype),
            jnp.stack(new_b, axis=1).astype(x_prompt.dtype))
```

```python
import functools
import math

import numpy as np
import jax
import jax.numpy as jnp
from jax import lax
from jax.experimental import pallas as pl
from jax.experimental.pallas import tpu as pltpu
from jax.experimental.pallas import tpu_sc as plsc

D_MODEL = 1024
GRID_W = 64
RET_HEADS = 4
RET_DK = 128
RET_DV = 256
RET_QK_W = RET_HEADS * RET_DK
RET_V_W = RET_HEADS * RET_DV
RET_CHUNK = 128
ROPE_BASE = 10000.0
POOL_GROUPS = 4
POOL_CH = 128
POOL_W = POOL_GROUPS * POOL_CH
POOL_WINDOWS = (2, 4, 8, 16)
N_EXPERTS = 64
TOP_K = 8
N_EXPERT_GROUPS = 8
GROUP_SIZE = N_EXPERTS // N_EXPERT_GROUPS
TOPK_GROUPS = 4
D_EXPERT = 256
ROUTED_SCALE = 2.5
EPS = 1e-6
IN_SIZES = (RET_QK_W, RET_QK_W, RET_V_W, RET_V_W, POOL_W, D_MODEL, D_MODEL)
IN_OFFS = tuple(sum(IN_SIZES[:i]) for i in range(len(IN_SIZES) + 1))
IN_W = IN_OFFS[-1]

LANES = 128
VMEM_LIMIT = 56 << 20
N_PIECES = D_MODEL // 2 // LANES
MXU_DIM = 256
SC_CHUNK = 128

F32 = jnp.float32
BF16 = jnp.bfloat16
I32 = jnp.int32
U32 = jnp.uint32


def _cparams(*sem):
    return pltpu.CompilerParams(dimension_semantics=sem, vmem_limit_bytes=VMEM_LIMIT)


def _dot(a, b):
    return jnp.dot(a, b, preferred_element_type=F32)


def _silu(x):
    return x * jax.nn.sigmoid(x)


def _rms_mod(x, g, scale, shift):
    y = x * lax.rsqrt(jnp.mean(x * x, axis=-1, keepdims=True) + EPS)
    return (y * g) * (1.0 + scale) + shift


def _ada_kernel(c_ref, w_ref, b_ref, o_ref):
    c = c_ref[...]
    o_ref[...] = jnp.dot(_silu(c), w_ref[...], preferred_element_type=F32,
                         precision=lax.Precision.HIGHEST) + b_ref[...]


def _ada(c_rows, ada_w, ada_b):
    r = c_rows.shape[0]
    n = ada_w.shape[1]
    tn = D_MODEL
    return pl.pallas_call(
        _ada_kernel,
        grid=(n // tn,),
        in_specs=[pl.BlockSpec((r, D_MODEL), lambda j: (0, 0)),
                  pl.BlockSpec((D_MODEL, tn), lambda j: (0, j)),
                  pl.BlockSpec((1, tn), lambda j: (0, j))],
        out_specs=pl.BlockSpec((r, tn), lambda j: (0, j)),
        out_shape=jax.ShapeDtypeStruct((r, n), F32),
        compiler_params=_cparams("parallel"),
        name="ada_mod",
    )(c_rows, ada_w, ada_b.reshape(1, n))


def _inproj_kernel(x_ref, mod_ref, g_ref, w_ref, cos_ref, sin_ref,
                   q_ref, k_ref, v_ref, gsw_ref, up_ref, ga_ref, gb_ref, *, on_grid):
    x = x_ref[...]
    h = _rms_mod(x, g_ref[...], mod_ref[0, 1:2, :], mod_ref[0, 0:1, :]).astype(BF16)

    def seg(i):
        return _dot(h, w_ref[:, IN_OFFS[i]:IN_OFFS[i + 1]])

    q = seg(0)
    k = seg(1)
    if on_grid:
        cos = jnp.concatenate([cos_ref[...]] * RET_HEADS, axis=1)
        sin = jnp.concatenate([sin_ref[...]] * RET_HEADS, axis=1)
        lane = lax.broadcasted_iota(jnp.int32, q.shape, 1)
        first = (lane & 63) < 32

        def rope(a):
            up = pltpu.roll(a, RET_QK_W - 32, axis=1)
            dn = pltpu.roll(a, 32, axis=1)
            return a * cos + jnp.where(first, up, dn) * sin

        q = rope(q)
        k = rope(k)
    q_ref[...] = q.astype(BF16)
    k_ref[...] = (k * (RET_DK ** -0.5)).astype(BF16)
    v_ref[...] = seg(2).astype(BF16)
    gsw_ref[...] = seg(3).astype(BF16)
    up_ref[...] = seg(4).astype(BF16)
    ga_ref[...] = seg(5).astype(BF16)
    gb_ref[...] = seg(6).astype(BF16)


def _inproj(x, mods, norm_g, w_in, cos_t, sin_t, *, tokens_per_mod, seq_len, on_grid, tm):
    t = x.shape[0]
    tiles_per_mod = tokens_per_mod // tm
    tiles_per_seq = seq_len // tm
    widths = IN_SIZES
    out_shape = [jax.ShapeDtypeStruct((t, w), BF16) for w in widths]
    out_specs = [pl.BlockSpec((tm, w), lambda i: (i, 0)) for w in widths]
    return pl.pallas_call(
        functools.partial(_inproj_kernel, on_grid=on_grid),
        grid=(t // tm,),
        in_specs=[pl.BlockSpec((tm, D_MODEL), lambda i: (i, 0)),
                  pl.BlockSpec((1, 6, D_MODEL), lambda i: (i // tiles_per_mod, 0, 0)),
                  pl.BlockSpec((1, D_MODEL), lambda i: (0, 0)),
                  pl.BlockSpec((D_MODEL, IN_W), lambda i: (0, 0)),
                  pl.BlockSpec((tm, RET_DK), lambda i: (i % tiles_per_seq, 0)),
                  pl.BlockSpec((tm, RET_DK), lambda i: (i % tiles_per_seq, 0))],
        out_specs=out_specs,
        out_shape=out_shape,
        compiler_params=_cparams("parallel"),
        name="inproj_grid" if on_grid else "inproj_seq",
    )(x, mods, norm_g, w_in, cos_t, sin_t)


def _rope_tables(seq_len):
    t = np.arange(seq_len)
    row = (t // GRID_W).astype(np.float32)
    col = (t % GRID_W).astype(np.float32)
    m = RET_DK // 4
    inv = (np.float32(ROPE_BASE) ** (-np.arange(m, dtype=np.float32) / np.float32(m))).astype(np.float32)
    ar = row[:, None] * inv
    ac = col[:, None] * inv
    cos = np.concatenate([np.cos(ar), np.cos(ar), np.cos(ac), np.cos(ac)], axis=1)
    sin = np.concatenate([-np.sin(ar), np.sin(ar), -np.sin(ac), np.sin(ac)], axis=1)
    return jnp.asarray(cos, F32), jnp.asarray(sin, F32)


def _ret_heads_per_step(seq_len):
    per_head = seq_len * (2 * 2 * (2 * RET_DK + 3 * RET_DV) + 4 * RET_DV + 2 * RET_DK)
    heads = RET_HEADS
    while heads > 1 and heads * per_head > VMEM_LIMIT * 3 // 4:
        heads //= 2
    return heads


def _ret_kernel(dec_ref, q_ref, k_ref, v_ref, g_ref, s0f_ref, s0b_ref,
                z_ref, sf_ref, sb_ref, oacc_ref, kt_ref, *, n_chunks, heads):
    c = RET_CHUNK
    half = n_chunks // 2
    ii = lax.broadcasted_iota(I32, (c, c), 0)
    jj = lax.broadcasted_iota(I32, (c, c), 1)
    ik = lax.broadcasted_iota(I32, (c, RET_DK), 0).astype(F32)
    jk = lax.broadcasted_iota(I32, (RET_DK, c), 1).astype(F32)

    def log_gamma(d, shape):
        return jnp.log1p(-jnp.exp2(-jnp.full(shape, d, F32)))

    consts = {}
    for hh in range(heads):
        h = pl.program_id(1) * heads + hh
        dec_f = dec_ref[0, h]
        dec_b = dec_ref[1, h]
        rel = (ii - jj).astype(F32)
        consts[hh, "f"] = (
            jnp.where(rel >= 0, jnp.exp(log_gamma(dec_f, (c, c)) * jnp.maximum(rel, 0.0)), 0.0),
            jnp.exp(log_gamma(dec_f, (c, RET_DK)) * (ik + 1.0)),
            jnp.exp(log_gamma(dec_f, (RET_DK, c)) * (c - 1.0 - jk)),
            jnp.exp(log_gamma(dec_f, (RET_DK, RET_DV)) * c))
        consts[hh, "b"] = (
            jnp.where(rel <= 0, jnp.exp(log_gamma(dec_b, (c, c)) * jnp.maximum(-rel, 0.0)), 0.0),
            jnp.exp(log_gamma(dec_b, (c, RET_DK)) * (c - ik)),
            jnp.exp(log_gamma(dec_b, (RET_DK, c)) * jk),
            jnp.exp(log_gamma(dec_b, (RET_DK, RET_DV)) * c))

    sf_ref[...] = s0f_ref[...]
    sb_ref[...] = s0b_ref[...]

    def transpose_keys(ci, carry):
        r = pl.ds(pl.multiple_of(ci * c, c), c)
        for hh in range(heads):
            kt_ref[hh, ci] = k_ref[r, hh * RET_DK:(hh + 1) * RET_DK].T
        return carry

    lax.fori_loop(0, n_chunks, transpose_keys, 0)

    def scores(ci, hh, direction):
        r = pl.ds(pl.multiple_of(ci * c, c), c)
        kcols = slice(hh * RET_DK, (hh + 1) * RET_DK)
        qc = q_ref[r, kcols]
        sc = lax.dot_general(qc, k_ref[r, kcols], (((1,), (1,)), ((), ())), preferred_element_type=F32)
        return ci, hh, direction, r, qc, sc

    def advance(job):
        ci, hh, direction, r, qc, sc = job
        dmask, qdec, kdec, cdec = consts[hh, direction]
        s_ref = sf_ref if direction == "f" else sb_ref
        vc = v_ref[r, hh * RET_DV:(hh + 1) * RET_DV]
        s = s_ref[hh]
        lhs = jnp.concatenate([(sc * dmask).astype(BF16), (qc.astype(F32) * qdec).astype(BF16)], axis=1)
        o = _dot(lhs, jnp.concatenate([vc, s.astype(BF16)], axis=0))
        kd_t = (kt_ref[hh, ci].astype(F32) * kdec).astype(BF16)
        s_ref[hh] = s * cdec + _dot(kd_t, vc)
        return o

    def emit(job, o, second):
        _, hh, _, r, _, _ = job
        vcols = slice(hh * RET_DV, (hh + 1) * RET_DV)
        if not second:
            oacc_ref[hh, r, :] = o
        else:
            o = o + oacc_ref[hh, r, :]
            o = o * lax.rsqrt(jnp.mean(o * o, axis=-1, keepdims=True) + EPS)
            g = g_ref[r, vcols].astype(F32)
            z_ref[r, vcols] = (_silu(g) * o).astype(BF16)

    def body(second):
        def run(t, carry):
            jobs = [scores(ci, hh, d) for hh in range(heads)
                    for ci, d in ((t, "f"), (n_chunks - 1 - t, "b"))]
            outs = [advance(job) for job in jobs]
            for job, o in zip(jobs, outs):
                emit(job, o, second)
            return carry
        return run

    lax.fori_loop(0, half, body(False), 0, unroll=4 if half % 4 == 0 else 1)
    lax.fori_loop(half, n_chunks, body(True), 0, unroll=2 if half % 2 == 0 else 1)


def _retention(q, k, v, gsw, dec, s0f, s0b, *, batch, seq_len):
    n_chunks = seq_len // RET_CHUNK
    assert n_chunks % 2 == 0
    heads = _ret_heads_per_step(seq_len)
    t = batch * seq_len
    st_spec = pl.BlockSpec((None, heads, RET_DK, RET_DV), lambda b, h: (b, h, 0, 0))
    st_shape = jax.ShapeDtypeStruct((batch, RET_HEADS, RET_DK, RET_DV), F32)
    kspec = pl.BlockSpec((seq_len, heads * RET_DK), lambda b, h: (b, h))
    vspec = pl.BlockSpec((seq_len, heads * RET_DV), lambda b, h: (b, h))
    return pl.pallas_call(
        functools.partial(_ret_kernel, n_chunks=n_chunks, heads=heads),
        grid=(batch, RET_HEADS // heads),
        in_specs=[pl.BlockSpec(memory_space=pltpu.SMEM), kspec, kspec, vspec, vspec, st_spec, st_spec],
        out_specs=[vspec, st_spec, st_spec],
        out_shape=[jax.ShapeDtypeStruct((t, RET_V_W), BF16), st_shape, st_shape],
        scratch_shapes=[pltpu.VMEM((heads, seq_len, RET_DV), F32),
                        pltpu.VMEM((heads, n_chunks, RET_DK, RET_CHUNK), BF16)],
        compiler_params=_cparams("parallel", "parallel"),
        name=f"retention_l{seq_len}",
    )(dec, q, k, v, gsw, s0f, s0b)


def _pool_kernel(u_ref, w_ref, sc_ref, o_ref, *, n_tok, width, two_d):
    n_rows = n_tok // width
    pos = lax.broadcasted_iota(I32, (width, POOL_CH), 0)

    def every_row(a):
        return jnp.concatenate([a] * n_rows, axis=0) if n_rows > 1 else a

    def shift_in_row(a, s):
        ok = (pos < width - s) if s > 0 else (pos >= -s)
        return pltpu.roll(a, (-s) % n_tok, axis=0) * every_row(jnp.where(ok, 1.0, 0.0))

    def shift_rows(a, m):
        k = abs(m) * width
        zeros = jnp.zeros((k, POOL_CH), F32)
        return (jnp.concatenate([a[k:], zeros], axis=0) if m > 0
                else jnp.concatenate([zeros, a[:n_tok - k]], axis=0))

    def box_sum(a, half, shift):
        fw = a
        bw = shift(a, -1)
        m = 1
        while m < half:
            fw = fw + shift(fw, m)
            bw = bw + shift(bw, -m)
            m *= 2
        return fw + bw

    def inv_count(p, half, extent):
        return 1.0 / (jnp.minimum(p + half, extent) - jnp.maximum(p - half, 0)).astype(F32)

    for g, window in enumerate(POOL_WINDOWS):
        half = window // 2
        cols = slice(g * POOL_CH, (g + 1) * POOL_CH)
        ug = u_ref[:, cols].astype(F32)
        total = box_sum(ug, half, shift_in_row)
        inv = every_row(inv_count(pos, half, width))
        if two_d:
            total = box_sum(total, half, shift_rows)
            row = lax.broadcasted_iota(I32, (n_rows, 1, POOL_CH), 0)
            inv_r = jnp.broadcast_to(inv_count(row, half, n_rows), (n_rows, width, POOL_CH))
            inv = inv * inv_r.reshape(n_tok, POOL_CH)
        d = (total * inv - ug).astype(BF16)
        o_ref[:, cols] = (_dot(d, w_ref[g]) * sc_ref[:, cols]).astype(BF16)


def _pool(u, pool_w, pool_scale, *, batch, seq_len, on_grid):
    t = batch * seq_len
    width = GRID_W if on_grid else seq_len
    n_tok = seq_len if on_grid else seq_len * math.gcd(batch, 4)
    return pl.pallas_call(
        functools.partial(_pool_kernel, n_tok=n_tok, width=width, two_d=on_grid),
        grid=(t // n_tok,),
        in_specs=[pl.BlockSpec((n_tok, POOL_W), lambda b: (b, 0)),
                  pl.BlockSpec((POOL_GROUPS, POOL_CH, POOL_CH), lambda b: (0, 0, 0)),
                  pl.BlockSpec((1, POOL_W), lambda b: (0, 0))],
        out_specs=pl.BlockSpec((n_tok, POOL_W), lambda b: (b, 0)),
        out_shape=jax.ShapeDtypeStruct((t, POOL_W), BF16),
        compiler_params=_cparams("parallel"),
        name=f"pool_l{seq_len}",
    )(u, pool_w, pool_scale)


def _pack_rows(x):
    half = D_MODEL // 2
    lo = lax.bitcast_convert_type(x[:, :half].astype(BF16).astype(F32), U32) >> 16
    hi = lax.bitcast_convert_type(x[:, half:].astype(BF16).astype(F32), U32) & jnp.uint32(0xFFFF0000)
    word = lax.bitcast_convert_type(hi | lo, I32)
    return [word[:, c * LANES:(c + 1) * LANES] for c in range(N_PIECES)]


def _unpack_rows(pieces):
    words = [lax.bitcast_convert_type(p, U32) for p in pieces]
    lo = [lax.bitcast_convert_type(w << 16, F32) for w in words]
    hi = [lax.bitcast_convert_type(w & jnp.uint32(0xFFFF0000), F32) for w in words]
    return lo, hi


def _merge_kernel(x_ref, z_ref, p_ref, ga_ref, gb_ref, mod_ref, g2_ref, wr_ref, wp_ref, wo_ref,
                  x1_ref, *piece_refs):
    y_ret = _dot(z_ref[...], wr_ref[...])
    y_pool = _dot(p_ref[...], wp_ref[...])
    merged = (jax.nn.sigmoid(ga_ref[...].astype(F32)) * y_ret
              + jax.nn.sigmoid(gb_ref[...].astype(F32)) * y_pool)
    x1 = x_ref[...] + mod_ref[0, 2:3, :] * _dot(merged.astype(BF16), wo_ref[...])
    x1_ref[...] = x1
    h2 = _rms_mod(x1, g2_ref[...], mod_ref[0, 4:5, :], mod_ref[0, 3:4, :])
    for ref, piece in zip(piece_refs, _pack_rows(h2)):
        ref[...] = piece


def _merge(x, z, p, ga, gb, mods, norm2_g, w_br_ret, w_br_pool, w_out, *, tokens_per_mod, tm):
    t = x.shape[0]
    tiles_per_mod = tokens_per_mod // tm
    row = lambda w: pl.BlockSpec((tm, w), lambda i: (i, 0))
    full = lambda a: pl.BlockSpec(a.shape, lambda i: (0,) * a.ndim)
    outs = pl.pallas_call(
        _merge_kernel,
        grid=(t // tm,),
        in_specs=[row(D_MODEL), row(RET_V_W), row(POOL_W), row(D_MODEL), row(D_MODEL),
                  pl.BlockSpec((1, 6, D_MODEL), lambda i: (i // tiles_per_mod, 0, 0)),
                  full(norm2_g), full(w_br_ret), full(w_br_pool), full(w_out)],
        out_specs=[row(D_MODEL)] + [row(LANES)] * N_PIECES,
        out_shape=[jax.ShapeDtypeStruct((t, D_MODEL), F32)] + [jax.ShapeDtypeStruct((t, LANES), I32)] * N_PIECES,
        compiler_params=_cparams("parallel"),
        name="merge",
    )(x, z, p, ga, gb, mods, norm2_g, w_br_ret, w_br_pool, w_out)
    return outs[0], outs[1:]


def _route_kernel(*refs):
    h_refs = refs[:N_PIECES]
    rw_ref, bias_ref, idx_ref, rank_ref, wk_ref, cnt_ref, carry_ref = refs[N_PIECES:]
    e = N_EXPERTS
    tm = h_refs[0].shape[0]
    neg = -jnp.inf

    @pl.when(pl.program_id(0) == 0)
    def _():
        carry_ref[...] = jnp.zeros(carry_ref.shape, F32)

    lo, hi = _unpack_rows([r[...] for r in h_refs])
    h = jnp.concatenate(lo + hi, axis=1).astype(BF16)
    logits = lax.dot_general(rw_ref[...], h, (((1,), (1,)), ((), ())), preferred_element_type=F32)[:e]
    scores = jax.nn.sigmoid(logits)
    sel = scores + bias_ref[:e, 0:1]
    e_idx = lax.broadcasted_iota(I32, (e, tm), 0)

    grp = sel.reshape(N_EXPERT_GROUPS, GROUP_SIZE, tm)
    m_idx = lax.broadcasted_iota(I32, grp.shape, 1)
    m1 = jnp.max(grp, axis=1, keepdims=True)
    first = jnp.min(jnp.where(grp == m1, m_idx, GROUP_SIZE), axis=1, keepdims=True)
    m2 = jnp.max(jnp.where(m_idx == first, neg, grp), axis=1, keepdims=True)
    gscore = (m1 + m2).reshape(N_EXPERT_GROUPS, tm)

    g_idx = lax.broadcasted_iota(I32, gscore.shape, 0)
    grank = jnp.zeros(gscore.shape, I32)
    for g in range(N_EXPERT_GROUPS):
        other = gscore[g:g + 1, :]
        beats = jnp.where(other > gscore, 1, jnp.where(other == gscore, (g_idx > g).astype(I32), 0))
        grank = grank + beats
    gkeep = (grank < TOPK_GROUPS).astype(F32)
    ekeep = jnp.broadcast_to(gkeep.reshape(N_EXPERT_GROUPS, 1, tm), grp.shape).reshape(e, tm)
    masked = jnp.where(ekeep > 0, sel, neg)

    chosen = jnp.zeros((e, tm), F32)
    picks, hits = [], []
    for _ in range(TOP_K):
        m = jnp.max(masked, axis=0, keepdims=True)
        pick = jnp.min(jnp.where(masked == m, e_idx, e), axis=0, keepdims=True)
        hit = e_idx == pick
        chosen = jnp.where(hit, 1.0, chosen)
        masked = jnp.where(hit, neg, masked)
        picks.append(pick)
        hits.append(hit)

    w = scores * chosen
    comb = w / jnp.sum(w, axis=0, keepdims=True) * ROUTED_SCALE

    t_row = lax.broadcasted_iota(I32, (tm, tm), 0)
    t_col = lax.broadcasted_iota(I32, (tm, tm), 1)
    before = (t_row < t_col).astype(BF16)
    rankmat = _dot(chosen.astype(BF16), before) + carry_ref[:e, 0:1]
    carry_ref[:e, :] = carry_ref[:e, :] + jnp.sum(chosen, axis=1, keepdims=True)
    cnt_ref[...] = carry_ref[...]

    idx_ref[...] = jnp.concatenate(picks, axis=0)
    rank_ref[...] = jnp.concatenate(
        [jnp.sum(jnp.where(h, rankmat, 0.0), axis=0, keepdims=True) for h in hits], axis=0).astype(I32)
    wk_ref[...] = jnp.concatenate(
        [jnp.sum(jnp.where(h, comb, 0.0), axis=0, keepdims=True) for h in hits], axis=0)


def _route(h2_pieces, router_wt, bias_col, *, tm):
    t = h2_pieces[0].shape[0]
    krow = pl.BlockSpec((TOP_K, tm), lambda i: (0, i))
    return pl.pallas_call(
        _route_kernel,
        grid=(t // tm,),
        in_specs=[pl.BlockSpec((tm, LANES), lambda i: (i, 0))] * N_PIECES
        + [pl.BlockSpec((LANES, D_MODEL), lambda i: (0, 0)), pl.BlockSpec((LANES, 1), lambda i: (0, 0))],
        out_specs=[krow, krow, krow, pl.BlockSpec((LANES, LANES), lambda i: (0, 0))],
        out_shape=[jax.ShapeDtypeStruct((TOP_K, t), I32), jax.ShapeDtypeStruct((TOP_K, t), I32),
                   jax.ShapeDtypeStruct((TOP_K, t), F32), jax.ShapeDtypeStruct((LANES, LANES), F32)],
        scratch_shapes=[pltpu.VMEM((LANES, LANES), F32)],
        compiler_params=_cparams("arbitrary"),
        name="route",
    )(*h2_pieces, router_wt, bias_col)


def _plan_kernel(idx_ref, rank_ref, cnt_ref, pos_ref, te_ref, nu_ref, *, group_tile):
    tf = idx_ref.shape[1]
    nt = te_ref.shape[1]
    cnt = cnt_ref[...].astype(I32)
    padded = (((cnt + (group_tile - 1)) // group_tile) * group_tile).astype(F32)
    e_sub = lax.broadcasted_iota(I32, (LANES, LANES), 0)
    e_lane = lax.broadcasted_iota(I32, (LANES, LANES), 1)
    base = jnp.sum(jnp.where(e_lane < e_sub, padded.T, 0.0), axis=1, keepdims=True)
    end = base + padded[:, 0:1]

    idx = idx_ref[...]
    start = jnp.zeros(idx.shape, F32)
    for e in range(N_EXPERTS):
        start = jnp.where(idx == e, base[e:e + 1, 0:1], start)
    pos = start.astype(I32) + rank_ref[...]
    for j in range(tf // SC_CHUNK):
        pos_ref[j] = pos[:, j * SC_CHUNK:(j + 1) * SC_CHUNK]

    tile_start = (lax.broadcasted_iota(I32, (N_EXPERTS, nt), 1) * group_tile).astype(F32)
    done = jnp.sum(jnp.where(end[:N_EXPERTS] <= tile_start, 1.0, 0.0), axis=0, keepdims=True)
    te_ref[...] = jnp.minimum(done, N_EXPERTS - 1.0).astype(I32)
    total = jnp.sum(padded[:, 0:1], axis=0, keepdims=True)
    nu_ref[...] = jnp.broadcast_to(total * (1.0 / group_tile), nu_ref.shape).astype(I32)


def _plan(idx, rank, counts, *, n_tiles, tf, group_tile):
    t = idx.shape[1]
    nt_pad = -(-n_tiles // LANES) * LANES
    krow = pl.BlockSpec((TOP_K, tf), lambda i: (0, i))
    return pl.pallas_call(
        functools.partial(_plan_kernel, group_tile=group_tile),
        grid=(t // tf,),
        in_specs=[krow, krow, pl.BlockSpec((LANES, LANES), lambda i: (0, 0))],
        out_specs=[pl.BlockSpec((tf // SC_CHUNK, TOP_K, SC_CHUNK), lambda i: (i, 0, 0)),
                   pl.BlockSpec((1, nt_pad), lambda i: (0, 0)),
                   pl.BlockSpec((1, LANES), lambda i: (0, 0))],
        out_shape=[jax.ShapeDtypeStruct((t // SC_CHUNK, TOP_K, SC_CHUNK), I32),
                   jax.ShapeDtypeStruct((1, nt_pad), I32), jax.ShapeDtypeStruct((1, LANES), I32)],
        compiler_params=_cparams("arbitrary"),
        name="moe_plan",
    )(idx, rank, counts)


def _sc_mesh_info():
    info = plsc.get_sparse_core_info()
    mesh = plsc.VectorSubcoreMesh(core_axis_name="c", subcore_axis_name="s")
    return mesh, info.num_cores, info.num_cores * info.num_subcores


def _sc_dispatch(pieces, pos, *, n_rows):
    t = pieces[0].shape[0]
    mesh, n_cores, n_workers = _sc_mesh_info()
    per_w = t // SC_CHUNK // n_workers

    @functools.partial(
        pl.kernel, mesh=mesh,
        out_type=[jax.ShapeDtypeStruct((n_rows, LANES), I32)] * N_PIECES,
        scratch_types=[pltpu.VMEM((TOP_K, SC_CHUNK), I32),
                       pltpu.VMEM((N_PIECES, SC_CHUNK, LANES), I32),
                       pltpu.SemaphoreType.DMA((N_PIECES,)),
                       pltpu.SemaphoreType.DMA],
        name="sc_dispatch",
    )
    def run(*refs):
        src = refs[:N_PIECES]
        pos_hbm = refs[N_PIECES]
        dst = refs[N_PIECES + 1:2 * N_PIECES + 1]
        idx_v, rows_v, load_sem, put_sem = refs[2 * N_PIECES + 1:]
        wid = lax.axis_index("s") * n_cores + lax.axis_index("c")

        @pl.loop(0, per_w)
        def _(j):
            ch = wid * per_w + j
            t0 = pl.multiple_of(ch * SC_CHUNK, SC_CHUNK)
            loads = [pltpu.make_async_copy(src[c].at[pl.ds(t0, SC_CHUNK)], rows_v.at[c], load_sem.at[c])
                     for c in range(N_PIECES)]
            for ld in loads:
                ld.start()
            pltpu.sync_copy(pos_hbm.at[ch], idx_v)
            puts = []
            for c in range(N_PIECES):
                loads[c].wait()
                for k in range(TOP_K):
                    puts.append(pltpu.make_async_copy(rows_v.at[c], dst[c].at[idx_v.at[k]], put_sem))
                    puts[-1].start()
            for cp in puts:
                cp.wait()

    return run(*pieces, pos)


SC_GROUP = 32


def _sc_combine(pieces, pos, wts, *, n_tokens):
    mesh, n_cores, n_workers = _sc_mesh_info()
    lanes = plsc.get_sparse_core_info().num_lanes
    per_w = n_tokens // SC_GROUP // n_workers

    @functools.partial(
        pl.kernel, mesh=mesh,
        out_type=[jax.ShapeDtypeStruct((n_tokens, LANES), F32)] * (2 * N_PIECES),
        scratch_types=[pltpu.VMEM((TOP_K, SC_GROUP), I32),
                       pltpu.VMEM((TOP_K, SC_GROUP), F32),
                       pltpu.VMEM((2, TOP_K, SC_GROUP, LANES), I32),
                       pltpu.VMEM((2, 2, SC_GROUP, LANES), F32),
                       pltpu.SemaphoreType.DMA((2,)),
                       pltpu.SemaphoreType.DMA((2,))],
        compiler_params=pltpu.CompilerParams(needs_layout_passes=False),
        name="sc_combine",
    )
    def run(*refs):
        src = refs[:N_PIECES]
        pos_hbm, wts_hbm = refs[N_PIECES:N_PIECES + 2]
        dst = refs[N_PIECES + 2:3 * N_PIECES + 2]
        idx_v, w_v, buf, acc, get_sem, put_sem = refs[3 * N_PIECES + 2:]
        wid = lax.axis_index("s") * n_cores + lax.axis_index("c")

        @pl.loop(0, per_w)
        def _(j):
            grp = wid * per_w + j
            t0 = pl.multiple_of(grp * SC_GROUP, SC_GROUP)
            pltpu.sync_copy(pos_hbm.at[grp], idx_v)
            pltpu.sync_copy(wts_hbm.at[grp], w_v)

            def gets(c, slot):
                return [pltpu.make_async_copy(src[c].at[idx_v.at[k]], buf.at[slot, k], get_sem.at[slot])
                        for k in range(TOP_K)]

            def puts(c, slot):
                r = pl.ds(t0, SC_GROUP)
                return [pltpu.make_async_copy(acc.at[slot, 0], dst[c].at[r], put_sem.at[slot]),
                        pltpu.make_async_copy(acc.at[slot, 1], dst[N_PIECES + c].at[r], put_sem.at[slot])]

            for cp in gets(0, 0):
                cp.start()
            for c in range(N_PIECES):
                slot = c % 2
                if c + 1 < N_PIECES:
                    for cp in gets(c + 1, 1 - slot):
                        cp.start()
                for cp in gets(c, slot):
                    cp.wait()
                if c >= 2:
                    for cp in puts(c - 2, slot):
                        cp.wait()

                @pl.loop(0, SC_GROUP)
                def _(r):
                    row = jnp.full((lanes,), r, I32)
                    w = [plsc.load_gather(w_v, [jnp.full((lanes,), k, I32), row]) for k in range(TOP_K)]
                    for q in range(LANES // lanes):
                        cols = pl.ds(q * lanes, lanes)
                        lo = hi = None
                        for k in range(TOP_K):
                            word = buf[slot, k, r, cols]
                            lo_k = plsc.bitcast(word << 16, F32) * w[k]
                            hi_k = plsc.bitcast(word & jnp.int32(-65536), F32) * w[k]
                            lo = lo_k if lo is None else lo + lo_k
                            hi = hi_k if hi is None else hi + hi_k
                        acc[slot, 0, r, cols] = lo
                        acc[slot, 1, r, cols] = hi

                for cp in puts(c, slot):
                    cp.start()
            for c in range(N_PIECES - 2, N_PIECES):
                for cp in puts(c, c % 2):
                    cp.wait()

    return run(*pieces, pos, wts)


def _group_tile(n_tokens):
    per_expert = n_tokens * TOP_K // N_EXPERTS
    return max(MXU_DIM, min(4 * MXU_DIM, per_expert // MXU_DIM * MXU_DIM))


SECOND_DMA_QUEUE = 1


def _experts_kernel(te_ref, nu_ref, *refs, tile):
    x_hbm = refs[:N_PIECES]
    w_f32 = refs[N_PIECES:N_PIECES + 3]
    y_hbm = refs[N_PIECES + 3:2 * N_PIECES + 3]
    wg_ref, wu_ref, wd_ref, xbuf, xsem, ybuf, ysem = refs[2 * N_PIECES + 3:]
    i = pl.program_id(0)
    n_used = nu_ref[0]
    slot = lax.rem(i, 2)

    def x_copies(step, into):
        r = pl.ds(pl.multiple_of(step * tile, tile), tile)
        return [pltpu.make_async_copy(x_hbm[c].at[r], xbuf.at[into, c], xsem.at[into, c])
                for c in range(N_PIECES)]

    def y_copies(step, out_of):
        r = pl.ds(pl.multiple_of(step * tile, tile), tile)
        return [pltpu.make_async_copy(ybuf.at[out_of, c], y_hbm[c].at[r], ysem.at[out_of, c])
                for c in range(N_PIECES)]

    @pl.when(i == 0)
    def _():
        for cp in x_copies(0, 0):
            cp.start(priority=SECOND_DMA_QUEUE)

    @pl.when(i + 1 < n_used)
    def _():
        for cp in x_copies(i + 1, 1 - slot):
            cp.start(priority=SECOND_DMA_QUEUE)

    last = n_used - 1
    cur = te_ref[jnp.minimum(i, last)]
    prev = te_ref[jnp.minimum(jnp.maximum(i - 1, 0), last)]

    @pl.when((i == 0) | (cur != prev))
    def _():
        for dst, src in zip((wg_ref, wu_ref, wd_ref), w_f32):
            dst[...] = src[...].astype(BF16)

    @pl.when(i < n_used)
    def _():
        for cp in x_copies(i, slot):
            cp.wait()
        subs = [slice(s * MXU_DIM, (s + 1) * MXU_DIM) for s in range(tile // MXU_DIM)]
        xs = []
        for rows in subs:
            lo, hi = _unpack_rows([xbuf[slot, c, rows, :] for c in range(N_PIECES)])
            xs.append(jnp.concatenate(lo + hi, axis=1).astype(BF16))
        gates = [(_dot(x, wg_ref[...]), _dot(x, wu_ref[...])) for x in xs]
        ys = [_dot((_silu(g) * u).astype(BF16), wd_ref[...]) for g, u in gates]
        for rows, y in zip(subs, ys):
            for c, piece in enumerate(_pack_rows(y)):
                ybuf[slot, c, rows, :] = piece

        @pl.when(i >= 1)
        def _():
            for cp in y_copies(i - 1, 1 - slot):
                cp.wait()

        for cp in y_copies(i, slot):
            cp.start(priority=SECOND_DMA_QUEUE)

        @pl.when(i == n_used - 1)
        def _():
            for cp in y_copies(i, slot):
                cp.wait()


def _experts(x_pieces, tile_expert, n_used, wg, wu, wd, *, group_tile):
    n_rows = x_pieces[0].shape[0]
    n_tiles = n_rows // group_tile

    def tile(i, te, nu):
        return jnp.minimum(i, nu[0] - 1)

    wspec = lambda a: pl.BlockSpec((None,) + a.shape[1:], lambda i, te, nu: (te[tile(i, te, nu)], 0, 0))
    return pl.pallas_call(
        functools.partial(_experts_kernel, tile=group_tile),
        grid_spec=pltpu.PrefetchScalarGridSpec(
            num_scalar_prefetch=2,
            grid=(n_tiles,),
            in_specs=[pl.BlockSpec(memory_space=pl.ANY)] * N_PIECES + [wspec(wg), wspec(wu), wspec(wd)],
            out_specs=[pl.BlockSpec(memory_space=pl.ANY)] * N_PIECES,
            scratch_shapes=[pltpu.VMEM(a.shape[1:], BF16) for a in (wg, wu, wd)]
            + [pltpu.VMEM((2, N_PIECES, group_tile, LANES), I32), pltpu.SemaphoreType.DMA((2, N_PIECES))] * 2),
        out_shape=[jax.ShapeDtypeStruct((n_rows, LANES), I32)] * N_PIECES,
        compiler_params=_cparams("arbitrary"),
        name="experts",
    )(tile_expert, n_used, *x_pieces, wg, wu, wd)


def _moe_out_kernel(x1_ref, mod_ref, fg_ref, sg_ref, su_ref, sd_ref, *refs):
    h_refs = refs[:N_PIECES]
    routed_refs = refs[N_PIECES:3 * N_PIECES]
    out_ref = refs[3 * N_PIECES]
    h_lo, h_hi = _unpack_rows([r[...] for r in h_refs])
    h = jnp.concatenate(h_lo + h_hi, axis=1).astype(BF16)
    hid = _silu(_dot(h, sg_ref[...])) * _dot(h, su_ref[...])
    shared = _dot(hid.astype(BF16), sd_ref[...])
    routed = jnp.concatenate([r[...] for r in routed_refs], axis=1)
    x2 = x1_ref[...] + mod_ref[0, 5:6, :] * (shared + routed)
    out_ref[...] = x2 * lax.rsqrt(jnp.mean(x2 * x2, axis=-1, keepdims=True) + EPS) * fg_ref[...]


def _moe_out(h2_pieces, x1, mods, final_g, sg, su, sd, routed_pieces, *, tokens_per_mod, tm):
    t = x1.shape[0]
    tiles_per_mod = tokens_per_mod // tm
    row = lambda w: pl.BlockSpec((tm, w), lambda i: (i, 0))
    full = lambda a: pl.BlockSpec(a.shape, lambda i: (0,) * a.ndim)
    return pl.pallas_call(
        _moe_out_kernel,
        grid=(t // tm,),
        in_specs=[row(D_MODEL),
                  pl.BlockSpec((1, 6, D_MODEL), lambda i: (i // tiles_per_mod, 0, 0)),
                  full(final_g), full(sg), full(su), full(sd)]
        + [row(LANES)] * (3 * N_PIECES),
        out_specs=row(D_MODEL),
        out_shape=jax.ShapeDtypeStruct((t, D_MODEL), F32),
        compiler_params=_cparams("parallel"),
        name="moe_out",
    )(x1, mods, final_g, sg, su, sd, *h2_pieces, *routed_pieces)


def _trunk(x, mods, s0f, s0b, w, *, batch, seq_len, on_grid):
    t = batch * seq_len
    tokens_per_mod = t // mods.shape[0]
    cos_t, sin_t = _rope_tables(seq_len)
    q, k, v, gsw, up, ga, gb = _inproj(x, mods, w["norm1_g"], w["w_in"], cos_t, sin_t,
                                       tokens_per_mod=tokens_per_mod, seq_len=seq_len,
                                       on_grid=on_grid, tm=256)
    z, s_f, s_b = _retention(q, k, v, gsw, w["dec"], s0f, s0b, batch=batch, seq_len=seq_len)
    p = _pool(up, w["pool_w"], w["pool_scale"], batch=batch, seq_len=seq_len, on_grid=on_grid)
    x1, h2_pieces = _merge(x, z, p, ga, gb, mods, w["norm2_g"], w["w_br_ret"], w["w_br_pool"],
                               w["w_out"], tokens_per_mod=tokens_per_mod, tm=512)

    group_tile = _group_tile(t)
    n_rows = t * TOP_K + N_EXPERTS * group_tile
    idx, rank, wts, counts = _route(h2_pieces, w["router_wt"], w["router_bias"], tm=1024)
    pos, tile_expert, n_used = _plan(idx, rank, counts, n_tiles=n_rows // group_tile, tf=2048,
                                     group_tile=group_tile)
    x_sorted = _sc_dispatch(h2_pieces, pos, n_rows=n_rows)
    y_sorted = _experts(x_sorted, tile_expert.reshape(-1), n_used.reshape(-1),
                        w["exp_w_gate"], w["exp_w_up"], w["exp_w_down"], group_tile=group_tile)
    regroup = lambda a: a.reshape(TOP_K, t // SC_GROUP, SC_GROUP).transpose(1, 0, 2)
    pos_rows = pos.transpose(1, 0, 2).reshape(TOP_K, t)
    routed = _sc_combine(y_sorted, regroup(pos_rows), regroup(wts), n_tokens=t)
    y = _moe_out(h2_pieces, x1, mods, w["final_g"], w["sh_w_gate"], w["sh_w_up"], w["sh_w_down"], routed,
                 tokens_per_mod=tokens_per_mod, tm=512)
    return y, s_f, s_b


def kernel(x_prompt, x_sample, state_ret_fwd, state_ret_bwd, c, c_ctx, ada_w, ada_b, norm1_g, norm2_g, w_in,
           ret_decay_fwd, ret_decay_bwd, w_br_ret, pool_w, pool_scale, w_br_pool, w_out, router_w, router_bias,
           exp_w_gate, exp_w_up, exp_w_down, sh_w_gate, sh_w_up, sh_w_down, final_norm_g):
    n_req, seq, d = x_prompt.shape
    n_dec, dec_seq, _ = x_sample.shape
    depth = ada_w.shape[0]
    assert depth == 1 and d == D_MODEL

    xc = x_prompt.reshape(n_req * seq, d)
    xs = x_sample.reshape(n_dec * dec_seq, d)
    zero_state = jnp.zeros((n_req, RET_HEADS, RET_DK, RET_DV), F32)
    new_f, new_b = [], []
    for l in range(depth):
        c_rows = jnp.concatenate([c_ctx[None, :], c, jnp.zeros((8 - 1 - n_dec, d), F32)], axis=0)
        mods = _ada(c_rows, ada_w[l], ada_b[l]).reshape(8, 6, d)
        pad_rows = LANES - N_EXPERTS
        w = dict(
            norm1_g=norm1_g[l].reshape(1, d), norm2_g=norm2_g[l].reshape(1, d),
            final_g=final_norm_g.reshape(1, d),
            w_in=w_in[l].astype(BF16),
            dec=jnp.stack([ret_decay_fwd[l], ret_decay_bwd[l]]).astype(F32),
            w_br_ret=w_br_ret[l].astype(BF16), pool_w=pool_w[l].astype(BF16),
            pool_scale=pool_scale[l].reshape(1, POOL_W), w_br_pool=w_br_pool[l].astype(BF16),
            w_out=w_out[l].astype(BF16),
            router_wt=jnp.pad(router_w[l].T, ((0, pad_rows), (0, 0))).astype(BF16),
            router_bias=jnp.pad(router_bias[l].astype(F32).reshape(N_EXPERTS, 1), ((0, pad_rows), (0, 0))),
            exp_w_gate=exp_w_gate[l], exp_w_up=exp_w_up[l], exp_w_down=exp_w_down[l],
            sh_w_gate=sh_w_gate[l].astype(BF16),
            sh_w_up=sh_w_up[l].astype(BF16), sh_w_down=sh_w_down[l].astype(BF16),
        )
        xs, _, _ = _trunk(xs, mods[1:1 + n_dec], state_ret_fwd[:, l].astype(F32),
                          state_ret_bwd[:, l].astype(F32), w,
                          batch=n_dec, seq_len=dec_seq, on_grid=True)
        xc, s_f, s_b = _trunk(xc, mods[0:1], zero_state, zero_state, w,
                              batch=n_req, seq_len=seq, on_grid=False)
        new_f.append(s_f)
        new_b.append(s_b)
    y_prompt = xc.reshape(n_req, seq, d)
    y_sample = xs.reshape(n_dec, dec_seq, d)
    return (y_prompt, y_sample, jnp.stack(new_f, axis=1).astype(x_prompt.dtype),
            jnp.stack(new_b, axis=1).astype(x_prompt.dtype))
```

```python
import functools
import math

import numpy as np
import jax
import jax.numpy as jnp
from jax import lax
from jax.experimental import pallas as pl
from jax.experimental.pallas import tpu as pltpu
from jax.experimental.pallas import tpu_sc as plsc

D_MODEL = 1024
GRID_W = 64
RET_HEADS = 4
RET_DK = 128
RET_DV = 256
RET_QK_W = RET_HEADS * RET_DK
RET_V_W = RET_HEADS * RET_DV
RET_CHUNK = 128
ROPE_BASE = 10000.0
POOL_GROUPS = 4
POOL_CH = 128
POOL_W = POOL_GROUPS * POOL_CH
POOL_WINDOWS = (2, 4, 8, 16)
N_EXPERTS = 64
TOP_K = 8
N_EXPERT_GROUPS = 8
GROUP_SIZE = N_EXPERTS // N_EXPERT_GROUPS
TOPK_GROUPS = 4
D_EXPERT = 256
ROUTED_SCALE = 2.5
EPS = 1e-6
IN_SIZES = (RET_QK_W, RET_QK_W, RET_V_W, RET_V_W, POOL_W, D_MODEL, D_MODEL)
IN_OFFS = tuple(sum(IN_SIZES[:i]) for i in range(len(IN_SIZES) + 1))
IN_W = IN_OFFS[-1]

LANES = 128
VMEM_LIMIT = 56 << 20
N_PIECES = D_MODEL // 2 // LANES
MXU_DIM = 256
SC_CHUNK = 128

F32 = jnp.float32
BF16 = jnp.bfloat16
I32 = jnp.int32
U32 = jnp.uint32


def _cparams(*sem):
    return pltpu.CompilerParams(dimension_semantics=sem, vmem_limit_bytes=VMEM_LIMIT)


def _dot(a, b):
    return jnp.dot(a, b, preferred_element_type=F32)


def _silu(x):
    return x * jax.nn.sigmoid(x)


def _rms_mod(x, g, scale, shift):
    y = x * lax.rsqrt(jnp.mean(x * x, axis=-1, keepdims=True) + EPS)
    return (y * g) * (1.0 + scale) + shift


def _ada_kernel(c_ref, w_ref, b_ref, o_ref):
    c = c_ref[...]
    o_ref[...] = jnp.dot(_silu(c), w_ref[...], preferred_element_type=F32,
                         precision=lax.Precision.HIGHEST) + b_ref[...]


def _ada(c_rows, ada_w, ada_b):
    r = c_rows.shape[0]
    n = ada_w.shape[1]
    tn = D_MODEL
    return pl.pallas_call(
        _ada_kernel,
        grid=(n // tn,),
        in_specs=[pl.BlockSpec((r, D_MODEL), lambda j: (0, 0)),
                  pl.BlockSpec((D_MODEL, tn), lambda j: (0, j)),
                  pl.BlockSpec((1, tn), lambda j: (0, j))],
        out_specs=pl.BlockSpec((r, tn), lambda j: (0, j)),
        out_shape=jax.ShapeDtypeStruct((r, n), F32),
        compiler_params=_cparams("parallel"),
        name="ada_mod",
    )(c_rows, ada_w, ada_b.reshape(1, n))


def _inproj_kernel(x_ref, mod_ref, g_ref, w_ref, cos_ref, sin_ref,
                   q_ref, k_ref, v_ref, gsw_ref, up_ref, ga_ref, gb_ref, *, on_grid):
    for s in range(x_ref.shape[0] // MXU_DIM):
        rows = slice(s * MXU_DIM, (s + 1) * MXU_DIM)
        h = _rms_mod(x_ref[rows, :], g_ref[...], mod_ref[0, 1:2, :], mod_ref[0, 0:1, :]).astype(BF16)

        def seg(i):
            return _dot(h, w_ref[:, IN_OFFS[i]:IN_OFFS[i + 1]])

        q = seg(0)
        k = seg(1)
        if on_grid:
            cos = jnp.concatenate([cos_ref[rows, :]] * RET_HEADS, axis=1)
            sin = jnp.concatenate([sin_ref[rows, :]] * RET_HEADS, axis=1)
            lane = lax.broadcasted_iota(jnp.int32, q.shape, 1)
            first = (lane & 63) < 32

            def rope(a):
                up = pltpu.roll(a, RET_QK_W - 32, axis=1)
                dn = pltpu.roll(a, 32, axis=1)
                return a * cos + jnp.where(first, up, dn) * sin

            q = rope(q)
            k = rope(k)
        q_ref[rows, :] = q.astype(BF16)
        k_ref[rows, :] = (k * (RET_DK ** -0.5)).astype(BF16)
        v_ref[rows, :] = seg(2).astype(BF16)
        gsw_ref[rows, :] = seg(3).astype(BF16)
        up_ref[rows, :] = seg(4).astype(BF16)
        ga_ref[rows, :] = seg(5).astype(BF16)
        gb_ref[rows, :] = seg(6).astype(BF16)


def _inproj(x, mods, norm_g, w_in, cos_t, sin_t, *, tokens_per_mod, seq_len, on_grid, tm):
    t = x.shape[0]
    tiles_per_mod = tokens_per_mod // tm
    tiles_per_seq = max(seq_len // tm, 1)
    widths = IN_SIZES
    out_shape = [jax.ShapeDtypeStruct((t, w), BF16) for w in widths]
    out_specs = [pl.BlockSpec((tm, w), lambda i: (i, 0)) for w in widths]
    return pl.pallas_call(
        functools.partial(_inproj_kernel, on_grid=on_grid),
        grid=(t // tm,),
        in_specs=[pl.BlockSpec((tm, D_MODEL), lambda i: (i, 0)),
                  pl.BlockSpec((1, 6, D_MODEL), lambda i: (i // tiles_per_mod, 0, 0)),
                  pl.BlockSpec((1, D_MODEL), lambda i: (0, 0)),
                  pl.BlockSpec((D_MODEL, IN_W), lambda i: (0, 0)),
                  pl.BlockSpec((tm, RET_DK), lambda i: (i % tiles_per_seq, 0)),
                  pl.BlockSpec((tm, RET_DK), lambda i: (i % tiles_per_seq, 0))],
        out_specs=out_specs,
        out_shape=out_shape,
        compiler_params=_cparams("parallel"),
        name="inproj_grid" if on_grid else "inproj_seq",
    )(x, mods, norm_g, w_in, cos_t, sin_t)


def _rope_tables(seq_len):
    t = np.arange(seq_len)
    row = (t // GRID_W).astype(np.float32)
    col = (t % GRID_W).astype(np.float32)
    m = RET_DK // 4
    inv = (np.float32(ROPE_BASE) ** (-np.arange(m, dtype=np.float32) / np.float32(m))).astype(np.float32)
    ar = row[:, None] * inv
    ac = col[:, None] * inv
    cos = np.concatenate([np.cos(ar), np.cos(ar), np.cos(ac), np.cos(ac)], axis=1)
    sin = np.concatenate([-np.sin(ar), np.sin(ar), -np.sin(ac), np.sin(ac)], axis=1)
    return jnp.asarray(cos, F32), jnp.asarray(sin, F32)


def _ret_heads_per_step(seq_len):
    per_head = seq_len * (2 * 2 * (2 * RET_DK + 3 * RET_DV) + 4 * RET_DV + 2 * RET_DK)
    heads = RET_HEADS
    while heads > 1 and heads * per_head > VMEM_LIMIT * 3 // 4:
        heads //= 2
    return heads


def _ret_kernel(dec_ref, q_ref, k_ref, v_ref, g_ref, *refs, n_chunks, heads, zero_init):
    s0_refs = () if zero_init else refs[:2]
    z_ref, sf_ref, sb_ref, oacc_ref, kt_ref = refs[len(s0_refs):]
    c = RET_CHUNK
    half = n_chunks // 2
    ii = lax.broadcasted_iota(I32, (c, c), 0)
    jj = lax.broadcasted_iota(I32, (c, c), 1)
    ik = lax.broadcasted_iota(I32, (c, RET_DK), 0).astype(F32)
    jk = lax.broadcasted_iota(I32, (RET_DK, c), 1).astype(F32)

    def log_gamma(d, shape):
        return jnp.log1p(-jnp.exp2(-jnp.full(shape, d, F32)))

    consts = {}
    for hh in range(heads):
        h = pl.program_id(1) * heads + hh
        dec_f = dec_ref[0, h]
        dec_b = dec_ref[1, h]
        rel = (ii - jj).astype(F32)
        consts[hh, "f"] = (
            jnp.where(rel >= 0, jnp.exp(log_gamma(dec_f, (c, c)) * jnp.maximum(rel, 0.0)), 0.0),
            jnp.exp(log_gamma(dec_f, (c, RET_DK)) * (ik + 1.0)),
            jnp.exp(log_gamma(dec_f, (RET_DK, c)) * (c - 1.0 - jk)),
            jnp.exp(log_gamma(dec_f, (RET_DK, RET_DV)) * c))
        consts[hh, "b"] = (
            jnp.where(rel <= 0, jnp.exp(log_gamma(dec_b, (c, c)) * jnp.maximum(-rel, 0.0)), 0.0),
            jnp.exp(log_gamma(dec_b, (c, RET_DK)) * (c - ik)),
            jnp.exp(log_gamma(dec_b, (RET_DK, c)) * jk),
            jnp.exp(log_gamma(dec_b, (RET_DK, RET_DV)) * c))

    for s_ref, s0_ref in zip((sf_ref, sb_ref), s0_refs or (None, None)):
        s_ref[...] = jnp.zeros(s_ref.shape, F32) if zero_init else s0_ref[...]

    def transpose_keys(ci, carry):
        r = pl.ds(pl.multiple_of(ci * c, c), c)
        for hh in range(heads):
            kt_ref[hh, ci] = k_ref[r, hh * RET_DK:(hh + 1) * RET_DK].T
        return carry

    lax.fori_loop(0, n_chunks, transpose_keys, 0)

    def scores(ci, hh, direction):
        r = pl.ds(pl.multiple_of(ci * c, c), c)
        kcols = slice(hh * RET_DK, (hh + 1) * RET_DK)
        qc = q_ref[r, kcols]
        sc = lax.dot_general(qc, k_ref[r, kcols], (((1,), (1,)), ((), ())), preferred_element_type=F32)
        return ci, hh, direction, r, qc, sc

    def advance(job):
        ci, hh, direction, r, qc, sc = job
        dmask, qdec, kdec, cdec = consts[hh, direction]
        s_ref = sf_ref if direction == "f" else sb_ref
        vc = v_ref[r, hh * RET_DV:(hh + 1) * RET_DV]
        s = s_ref[hh]
        lhs = jnp.concatenate([(sc * dmask).astype(BF16), (qc.astype(F32) * qdec).astype(BF16)], axis=1)
        o = _dot(lhs, jnp.concatenate([vc, s.astype(BF16)], axis=0))
        kd_t = (kt_ref[hh, ci].astype(F32) * kdec).astype(BF16)
        s_ref[hh] = s * cdec + _dot(kd_t, vc)
        return o

    def emit(job, o, second):
        _, hh, _, r, _, _ = job
        vcols = slice(hh * RET_DV, (hh + 1) * RET_DV)
        if not second:
            oacc_ref[hh, r, :] = o
        else:
            o = o + oacc_ref[hh, r, :]
            o = o * lax.rsqrt(jnp.mean(o * o, axis=-1, keepdims=True) + EPS)
            g = g_ref[r, vcols].astype(F32)
            z_ref[r, vcols] = (_silu(g) * o).astype(BF16)

    def body(second):
        def run(t, carry):
            jobs = [scores(ci, hh, d) for hh in range(heads)
                    for ci, d in ((t, "f"), (n_chunks - 1 - t, "b"))]
            outs = [advance(job) for job in jobs]
            for job, o in zip(jobs, outs):
                emit(job, o, second)
            return carry
        return run

    lax.fori_loop(0, half, body(False), 0, unroll=4 if half % 4 == 0 else 1)
    lax.fori_loop(half, n_chunks, body(True), 0, unroll=2 if half % 2 == 0 else 1)


def _retention(q, k, v, gsw, dec, s0, *, batch, seq_len):
    n_chunks = seq_len // RET_CHUNK
    assert n_chunks % 2 == 0
    heads = _ret_heads_per_step(seq_len)
    t = batch * seq_len
    st_spec = pl.BlockSpec((None, heads, RET_DK, RET_DV), lambda b, h: (b, h, 0, 0))
    st_shape = jax.ShapeDtypeStruct((batch, RET_HEADS, RET_DK, RET_DV), F32)
    kspec = pl.BlockSpec((seq_len, heads * RET_DK), lambda b, h: (b, h))
    vspec = pl.BlockSpec((seq_len, heads * RET_DV), lambda b, h: (b, h))
    return pl.pallas_call(
        functools.partial(_ret_kernel, n_chunks=n_chunks, heads=heads, zero_init=s0 is None),
        grid=(batch, RET_HEADS // heads),
        in_specs=[pl.BlockSpec(memory_space=pltpu.SMEM), kspec, kspec, vspec, vspec]
        + ([] if s0 is None else [st_spec, st_spec]),
        out_specs=[vspec, st_spec, st_spec],
        out_shape=[jax.ShapeDtypeStruct((t, RET_V_W), BF16), st_shape, st_shape],
        scratch_shapes=[pltpu.VMEM((heads, seq_len, RET_DV), F32),
                        pltpu.VMEM((heads, n_chunks, RET_DK, RET_CHUNK), BF16)],
        compiler_params=_cparams("parallel", "parallel"),
        name=f"retention_l{seq_len}",
    )(dec, q, k, v, gsw, *(s0 or ()))


def _pool_kernel(u_ref, w_ref, sc_ref, o_ref, *, n_tok, width, two_d):
    n_rows = n_tok // width
    pos = lax.broadcasted_iota(I32, (width, POOL_CH), 0)

    def every_row(a):
        return jnp.concatenate([a] * n_rows, axis=0) if n_rows > 1 else a

    def shift_in_row(a, s):
        ok = (pos < width - s) if s > 0 else (pos >= -s)
        return pltpu.roll(a, (-s) % n_tok, axis=0) * every_row(jnp.where(ok, 1.0, 0.0))

    def shift_rows(a, m):
        k = abs(m) * width
        zeros = jnp.zeros((k, POOL_CH), F32)
        return (jnp.concatenate([a[k:], zeros], axis=0) if m > 0
                else jnp.concatenate([zeros, a[:n_tok - k]], axis=0))

    def box_sum(a, half, shift):
        fw = a
        bw = shift(a, -1)
        m = 1
        while m < half:
            fw = fw + shift(fw, m)
            bw = bw + shift(bw, -m)
            m *= 2
        return fw + bw

    def inv_count(p, half, extent):
        return 1.0 / (jnp.minimum(p + half, extent) - jnp.maximum(p - half, 0)).astype(F32)

    for g, window in enumerate(POOL_WINDOWS):
        half = window // 2
        cols = slice(g * POOL_CH, (g + 1) * POOL_CH)
        ug = u_ref[:, cols].astype(F32)
        total = box_sum(ug, half, shift_in_row)
        inv = every_row(inv_count(pos, half, width))
        if two_d:
            total = box_sum(total, half, shift_rows)
            row = lax.broadcasted_iota(I32, (n_rows, 1, POOL_CH), 0)
            inv_r = jnp.broadcast_to(inv_count(row, half, n_rows), (n_rows, width, POOL_CH))
            inv = inv * inv_r.reshape(n_tok, POOL_CH)
        d = (total * inv - ug).astype(BF16)
        o_ref[:, cols] = (_dot(d, w_ref[g]) * sc_ref[:, cols]).astype(BF16)


def _pool(u, pool_w, pool_scale, *, batch, seq_len, on_grid):
    t = batch * seq_len
    width = GRID_W if on_grid else seq_len
    n_tok = seq_len if on_grid else seq_len * math.gcd(batch, 4)
    return pl.pallas_call(
        functools.partial(_pool_kernel, n_tok=n_tok, width=width, two_d=on_grid),
        grid=(t // n_tok,),
        in_specs=[pl.BlockSpec((n_tok, POOL_W), lambda b: (b, 0)),
                  pl.BlockSpec((POOL_GROUPS, POOL_CH, POOL_CH), lambda b: (0, 0, 0)),
                  pl.BlockSpec((1, POOL_W), lambda b: (0, 0))],
        out_specs=pl.BlockSpec((n_tok, POOL_W), lambda b: (b, 0)),
        out_shape=jax.ShapeDtypeStruct((t, POOL_W), BF16),
        compiler_params=_cparams("parallel"),
        name=f"pool_l{seq_len}",
    )(u, pool_w, pool_scale)


def _pack_rows(x):
    half = D_MODEL // 2
    lo = lax.bitcast_convert_type(x[:, :half].astype(BF16).astype(F32), U32) >> 16
    hi = lax.bitcast_convert_type(x[:, half:].astype(BF16).astype(F32), U32) & jnp.uint32(0xFFFF0000)
    word = lax.bitcast_convert_type(hi | lo, I32)
    return [word[:, c * LANES:(c + 1) * LANES] for c in range(N_PIECES)]


def _unpack_rows(pieces):
    words = [lax.bitcast_convert_type(p, U32) for p in pieces]
    lo = [lax.bitcast_convert_type(w << 16, F32) for w in words]
    hi = [lax.bitcast_convert_type(w & jnp.uint32(0xFFFF0000), F32) for w in words]
    return lo, hi


def _merge_kernel(x_ref, z_ref, p_ref, ga_ref, gb_ref, mod_ref, g2_ref, wr_ref, wp_ref, wo_ref,
                  x1_ref, *piece_refs):
    y_ret = _dot(z_ref[...], wr_ref[...])
    y_pool = _dot(p_ref[...], wp_ref[...])
    merged = (jax.nn.sigmoid(ga_ref[...].astype(F32)) * y_ret
              + jax.nn.sigmoid(gb_ref[...].astype(F32)) * y_pool)
    x1 = x_ref[...] + mod_ref[0, 2:3, :] * _dot(merged.astype(BF16), wo_ref[...])
    x1_ref[...] = x1
    h2 = _rms_mod(x1, g2_ref[...], mod_ref[0, 4:5, :], mod_ref[0, 3:4, :])
    for ref, piece in zip(piece_refs, _pack_rows(h2)):
        ref[...] = piece


def _merge(x, z, p, ga, gb, mods, norm2_g, w_br_ret, w_br_pool, w_out, *, tokens_per_mod, tm):
    t = x.shape[0]
    tiles_per_mod = tokens_per_mod // tm
    row = lambda w: pl.BlockSpec((tm, w), lambda i: (i, 0))
    full = lambda a: pl.BlockSpec(a.shape, lambda i: (0,) * a.ndim)
    outs = pl.pallas_call(
        _merge_kernel,
        grid=(t // tm,),
        in_specs=[row(D_MODEL), row(RET_V_W), row(POOL_W), row(D_MODEL), row(D_MODEL),
                  pl.BlockSpec((1, 6, D_MODEL), lambda i: (i // tiles_per_mod, 0, 0)),
                  full(norm2_g), full(w_br_ret), full(w_br_pool), full(w_out)],
        out_specs=[row(D_MODEL)] + [row(LANES)] * N_PIECES,
        out_shape=[jax.ShapeDtypeStruct((t, D_MODEL), F32)] + [jax.ShapeDtypeStruct((t, LANES), I32)] * N_PIECES,
        compiler_params=_cparams("parallel"),
        name="merge",
    )(x, z, p, ga, gb, mods, norm2_g, w_br_ret, w_br_pool, w_out)
    return outs[0], outs[1:]


def _route_kernel(*refs):
    h_refs = refs[:N_PIECES]
    rw_ref, bias_ref, idx_ref, rank_ref, wk_ref, cnt_ref, carry_ref, before_ref = refs[N_PIECES:]
    e = N_EXPERTS
    tm = h_refs[0].shape[0]
    neg = -jnp.inf

    @pl.when(pl.program_id(0) == 0)
    def _():
        carry_ref[...] = jnp.zeros(carry_ref.shape, F32)
        t_row = lax.broadcasted_iota(I32, (tm, tm), 0)
        t_col = lax.broadcasted_iota(I32, (tm, tm), 1)
        before_ref[...] = (t_row < t_col).astype(BF16)

    lo, hi = _unpack_rows([r[...] for r in h_refs])
    h = jnp.concatenate(lo + hi, axis=1).astype(BF16)
    logits = lax.dot_general(rw_ref[...], h, (((1,), (1,)), ((), ())), preferred_element_type=F32)[:e]
    scores = jax.nn.sigmoid(logits)
    sel = scores + bias_ref[:e, 0:1]
    e_idx = lax.broadcasted_iota(I32, (e, tm), 0)

    grp = sel.reshape(N_EXPERT_GROUPS, GROUP_SIZE, tm)
    m_idx = lax.broadcasted_iota(I32, grp.shape, 1)
    m1 = jnp.max(grp, axis=1, keepdims=True)
    first = jnp.min(jnp.where(grp == m1, m_idx, GROUP_SIZE), axis=1, keepdims=True)
    m2 = jnp.max(jnp.where(m_idx == first, neg, grp), axis=1, keepdims=True)
    gscore = (m1 + m2).reshape(N_EXPERT_GROUPS, tm)

    g_idx = lax.broadcasted_iota(I32, gscore.shape, 0)
    grank = jnp.zeros(gscore.shape, I32)
    for g in range(N_EXPERT_GROUPS):
        other = gscore[g:g + 1, :]
        beats = jnp.where(other > gscore, 1, jnp.where(other == gscore, (g_idx > g).astype(I32), 0))
        grank = grank + beats
    gkeep = (grank < TOPK_GROUPS).astype(F32)
    ekeep = jnp.broadcast_to(gkeep.reshape(N_EXPERT_GROUPS, 1, tm), grp.shape).reshape(e, tm)
    masked = jnp.where(ekeep > 0, sel, neg)

    chosen = jnp.zeros((e, tm), F32)
    picks, hits = [], []
    for _ in range(TOP_K):
        m = jnp.max(masked, axis=0, keepdims=True)
        pick = jnp.min(jnp.where(masked == m, e_idx, e), axis=0, keepdims=True)
        hit = e_idx == pick
        chosen = jnp.where(hit, 1.0, chosen)
        masked = jnp.where(hit, neg, masked)
        picks.append(pick)
        hits.append(hit)

    w = scores * chosen
    comb = w / jnp.sum(w, axis=0, keepdims=True) * ROUTED_SCALE

    rankmat = _dot(chosen.astype(BF16), before_ref[...]) + carry_ref[:e, 0:1]
    carry_ref[:e, :] = carry_ref[:e, :] + jnp.sum(chosen, axis=1, keepdims=True)
    cnt_ref[...] = carry_ref[...]

    idx_ref[...] = jnp.concatenate(picks, axis=0)
    rank_ref[...] = jnp.concatenate(
        [jnp.sum(jnp.where(h, rankmat, 0.0), axis=0, keepdims=True) for h in hits], axis=0).astype(I32)
    wk_ref[...] = jnp.concatenate(
        [jnp.sum(jnp.where(h, comb, 0.0), axis=0, keepdims=True) for h in hits], axis=0)


def _route(h2_pieces, router_wt, bias_col, *, tm):
    t = h2_pieces[0].shape[0]
    krow = pl.BlockSpec((TOP_K, tm), lambda i: (0, i))
    return pl.pallas_call(
        _route_kernel,
        grid=(t // tm,),
        in_specs=[pl.BlockSpec((tm, LANES), lambda i: (i, 0))] * N_PIECES
        + [pl.BlockSpec((LANES, D_MODEL), lambda i: (0, 0)), pl.BlockSpec((LANES, 1), lambda i: (0, 0))],
        out_specs=[krow, krow, krow, pl.BlockSpec((LANES, LANES), lambda i: (0, 0))],
        out_shape=[jax.ShapeDtypeStruct((TOP_K, t), I32), jax.ShapeDtypeStruct((TOP_K, t), I32),
                   jax.ShapeDtypeStruct((TOP_K, t), F32), jax.ShapeDtypeStruct((LANES, LANES), F32)],
        scratch_shapes=[pltpu.VMEM((LANES, LANES), F32), pltpu.VMEM((tm, tm), BF16)],
        compiler_params=_cparams("arbitrary"),
        name="route",
    )(*h2_pieces, router_wt, bias_col)


def _plan_kernel(idx_ref, rank_ref, cnt_ref, pos_ref, te_ref, nu_ref, *, group_tile):
    tf = idx_ref.shape[1]
    nt = te_ref.shape[1]
    cnt = cnt_ref[...].astype(I32)
    padded = (((cnt + (group_tile - 1)) // group_tile) * group_tile).astype(F32)
    e_sub = lax.broadcasted_iota(I32, (LANES, LANES), 0)
    e_lane = lax.broadcasted_iota(I32, (LANES, LANES), 1)
    base = jnp.sum(jnp.where(e_lane < e_sub, padded.T, 0.0), axis=1, keepdims=True)
    end = base + padded[:, 0:1]

    idx = idx_ref[...]
    start = jnp.zeros(idx.shape, F32)
    for e in range(N_EXPERTS):
        start = jnp.where(idx == e, base[e:e + 1, 0:1], start)
    pos = start.astype(I32) + rank_ref[...]
    for j in range(tf // SC_CHUNK):
        pos_ref[j] = pos[:, j * SC_CHUNK:(j + 1) * SC_CHUNK]

    tile_start = (lax.broadcasted_iota(I32, (N_EXPERTS, nt), 1) * group_tile).astype(F32)
    done = jnp.sum(jnp.where(end[:N_EXPERTS] <= tile_start, 1.0, 0.0), axis=0, keepdims=True)
    te_ref[...] = jnp.minimum(done, N_EXPERTS - 1.0).astype(I32)
    total = jnp.sum(padded[:, 0:1], axis=0, keepdims=True)
    nu_ref[...] = jnp.broadcast_to(total * (1.0 / group_tile), nu_ref.shape).astype(I32)


def _plan(idx, rank, counts, *, n_tiles, tf, group_tile):
    t = idx.shape[1]
    nt_pad = -(-n_tiles // LANES) * LANES
    krow = pl.BlockSpec((TOP_K, tf), lambda i: (0, i))
    return pl.pallas_call(
        functools.partial(_plan_kernel, group_tile=group_tile),
        grid=(t // tf,),
        in_specs=[krow, krow, pl.BlockSpec((LANES, LANES), lambda i: (0, 0))],
        out_specs=[pl.BlockSpec((tf // SC_CHUNK, TOP_K, SC_CHUNK), lambda i: (i, 0, 0)),
                   pl.BlockSpec((1, nt_pad), lambda i: (0, 0)),
                   pl.BlockSpec((1, LANES), lambda i: (0, 0))],
        out_shape=[jax.ShapeDtypeStruct((t // SC_CHUNK, TOP_K, SC_CHUNK), I32),
                   jax.ShapeDtypeStruct((1, nt_pad), I32), jax.ShapeDtypeStruct((1, LANES), I32)],
        compiler_params=_cparams("arbitrary"),
        name="moe_plan",
    )(idx, rank, counts)


def _sc_mesh_info():
    info = plsc.get_sparse_core_info()
    mesh = plsc.VectorSubcoreMesh(core_axis_name="c", subcore_axis_name="s")
    return mesh, info.num_cores, info.num_cores * info.num_subcores


def _sc_dispatch(pieces, pos, *, n_rows):
    t = pieces[0].shape[0]
    mesh, n_cores, n_workers = _sc_mesh_info()
    per_w = t // SC_CHUNK // n_workers

    @functools.partial(
        pl.kernel, mesh=mesh,
        out_type=[jax.ShapeDtypeStruct((n_rows, LANES), I32)] * N_PIECES,
        scratch_types=[pltpu.VMEM((TOP_K, SC_CHUNK), I32),
                       pltpu.VMEM((N_PIECES, SC_CHUNK, LANES), I32),
                       pltpu.SemaphoreType.DMA((N_PIECES,)),
                       pltpu.SemaphoreType.DMA],
        name="sc_dispatch",
    )
    def run(*refs):
        src = refs[:N_PIECES]
        pos_hbm = refs[N_PIECES]
        dst = refs[N_PIECES + 1:2 * N_PIECES + 1]
        idx_v, rows_v, load_sem, put_sem = refs[2 * N_PIECES + 1:]
        wid = lax.axis_index("s") * n_cores + lax.axis_index("c")

        @pl.loop(0, per_w)
        def _(j):
            ch = wid * per_w + j
            t0 = pl.multiple_of(ch * SC_CHUNK, SC_CHUNK)
            loads = [pltpu.make_async_copy(src[c].at[pl.ds(t0, SC_CHUNK)], rows_v.at[c], load_sem.at[c])
                     for c in range(N_PIECES)]
            for ld in loads:
                ld.start()
            pltpu.sync_copy(pos_hbm.at[ch], idx_v)
            puts = []
            for c in range(N_PIECES):
                loads[c].wait()
                for k in range(TOP_K):
                    puts.append(pltpu.make_async_copy(rows_v.at[c], dst[c].at[idx_v.at[k]], put_sem))
                    puts[-1].start()
            for cp in puts:
                cp.wait()

    return run(*pieces, pos)


SC_GROUP = 32


def _sc_combine(pieces, pos, wts, *, n_tokens):
    mesh, n_cores, n_workers = _sc_mesh_info()
    lanes = plsc.get_sparse_core_info().num_lanes
    per_w = n_tokens // SC_GROUP // n_workers

    @functools.partial(
        pl.kernel, mesh=mesh,
        out_type=[jax.ShapeDtypeStruct((n_tokens, LANES), F32)] * (2 * N_PIECES),
        scratch_types=[pltpu.VMEM((TOP_K, SC_GROUP), I32),
                       pltpu.VMEM((TOP_K, SC_GROUP), F32),
                       pltpu.VMEM((2, TOP_K, SC_GROUP, LANES), I32),
                       pltpu.VMEM((2, 2, SC_GROUP, LANES), F32),
                       pltpu.SemaphoreType.DMA((2,)),
                       pltpu.SemaphoreType.DMA((2,))],
        compiler_params=pltpu.CompilerParams(needs_layout_passes=False),
        name="sc_combine",
    )
    def run(*refs):
        src = refs[:N_PIECES]
        pos_hbm, wts_hbm = refs[N_PIECES:N_PIECES + 2]
        dst = refs[N_PIECES + 2:3 * N_PIECES + 2]
        idx_v, w_v, buf, acc, get_sem, put_sem = refs[3 * N_PIECES + 2:]
        wid = lax.axis_index("s") * n_cores + lax.axis_index("c")

        @pl.loop(0, per_w)
        def _(j):
            grp = wid * per_w + j
            t0 = pl.multiple_of(grp * SC_GROUP, SC_GROUP)
            pltpu.sync_copy(pos_hbm.at[grp], idx_v)
            pltpu.sync_copy(wts_hbm.at[grp], w_v)

            def gets(c, slot):
                return [pltpu.make_async_copy(src[c].at[idx_v.at[k]], buf.at[slot, k], get_sem.at[slot])
                        for k in range(TOP_K)]

            def puts(c, slot):
                r = pl.ds(t0, SC_GROUP)
                return [pltpu.make_async_copy(acc.at[slot, 0], dst[c].at[r], put_sem.at[slot]),
                        pltpu.make_async_copy(acc.at[slot, 1], dst[N_PIECES + c].at[r], put_sem.at[slot])]

            for cp in gets(0, 0):
                cp.start()
            for c in range(N_PIECES):
                slot = c % 2
                if c + 1 < N_PIECES:
                    for cp in gets(c + 1, 1 - slot):
                        cp.start()
                for cp in gets(c, slot):
                    cp.wait()
                if c >= 2:
                    for cp in puts(c - 2, slot):
                        cp.wait()

                @pl.loop(0, SC_GROUP)
                def _(r):
                    row = jnp.full((lanes,), r, I32)
                    w = [plsc.load_gather(w_v, [jnp.full((lanes,), k, I32), row]) for k in range(TOP_K)]
                    for q in range(LANES // lanes):
                        cols = pl.ds(q * lanes, lanes)
                        lo = hi = None
                        for k in range(TOP_K):
                            word = buf[slot, k, r, cols]
                            lo_k = plsc.bitcast(word << 16, F32) * w[k]
                            hi_k = plsc.bitcast(word & jnp.int32(-65536), F32) * w[k]
                            lo = lo_k if lo is None else lo + lo_k
                            hi = hi_k if hi is None else hi + hi_k
                        acc[slot, 0, r, cols] = lo
                        acc[slot, 1, r, cols] = hi

                for cp in puts(c, slot):
                    cp.start()
            for c in range(N_PIECES - 2, N_PIECES):
                for cp in puts(c, c % 2):
                    cp.wait()

    return run(*pieces, pos, wts)


def _group_tile(n_tokens):
    per_expert = n_tokens * TOP_K // N_EXPERTS
    return max(MXU_DIM, min(4 * MXU_DIM, per_expert // MXU_DIM * MXU_DIM))


SECOND_DMA_QUEUE = 1


def _experts_kernel(te_ref, nu_ref, *refs, tile):
    x_hbm = refs[:N_PIECES]
    w_f32 = refs[N_PIECES:N_PIECES + 3]
    y_hbm = refs[N_PIECES + 3:2 * N_PIECES + 3]
    wg_ref, wu_ref, wd_ref, xbuf, xsem, ybuf, ysem = refs[2 * N_PIECES + 3:]
    i = pl.program_id(0)
    n_used = nu_ref[0]
    slot = lax.rem(i, 2)

    def x_copies(step, into):
        r = pl.ds(pl.multiple_of(step * tile, tile), tile)
        return [pltpu.make_async_copy(x_hbm[c].at[r], xbuf.at[into, c], xsem.at[into, c])
                for c in range(N_PIECES)]

    def y_copies(step, out_of):
        r = pl.ds(pl.multiple_of(step * tile, tile), tile)
        return [pltpu.make_async_copy(ybuf.at[out_of, c], y_hbm[c].at[r], ysem.at[out_of, c])
                for c in range(N_PIECES)]

    @pl.when(i == 0)
    def _():
        for cp in x_copies(0, 0):
            cp.start(priority=SECOND_DMA_QUEUE)

    @pl.when(i + 1 < n_used)
    def _():
        for cp in x_copies(i + 1, 1 - slot):
            cp.start(priority=SECOND_DMA_QUEUE)

    last = n_used - 1
    cur = te_ref[jnp.minimum(i, last)]
    prev = te_ref[jnp.minimum(jnp.maximum(i - 1, 0), last)]

    @pl.when((i == 0) | (cur != prev))
    def _():
        for dst, src in zip((wg_ref, wu_ref, wd_ref), w_f32):
            dst[...] = src[...].astype(BF16)

    @pl.when(i < n_used)
    def _():
        for cp in x_copies(i, slot):
            cp.wait()
        subs = [slice(s * MXU_DIM, (s + 1) * MXU_DIM) for s in range(tile // MXU_DIM)]
        xs = []
        for rows in subs:
            lo, hi = _unpack_rows([xbuf[slot, c, rows, :] for c in range(N_PIECES)])
            xs.append(jnp.concatenate(lo + hi, axis=1).astype(BF16))
        gates = [(_dot(x, wg_ref[...]), _dot(x, wu_ref[...])) for x in xs]
        ys = [_dot((_silu(g) * u).astype(BF16), wd_ref[...]) for g, u in gates]
        for rows, y in zip(subs, ys):
            for c, piece in enumerate(_pack_rows(y)):
                ybuf[slot, c, rows, :] = piece

        @pl.when(i >= 1)
        def _():
            for cp in y_copies(i - 1, 1 - slot):
                cp.wait()

        for cp in y_copies(i, slot):
            cp.start(priority=SECOND_DMA_QUEUE)

        @pl.when(i == n_used - 1)
        def _():
            for cp in y_copies(i, slot):
                cp.wait()


def _experts(x_pieces, tile_expert, n_used, wg, wu, wd, *, group_tile):
    n_rows = x_pieces[0].shape[0]
    n_tiles = n_rows // group_tile

    def tile(i, te, nu):
        return jnp.minimum(i, nu[0] - 1)

    wspec = lambda a: pl.BlockSpec((None,) + a.shape[1:], lambda i, te, nu: (te[tile(i, te, nu)], 0, 0))
    return pl.pallas_call(
        functools.partial(_experts_kernel, tile=group_tile),
        grid_spec=pltpu.PrefetchScalarGridSpec(
            num_scalar_prefetch=2,
            grid=(n_tiles,),
            in_specs=[pl.BlockSpec(memory_space=pl.ANY)] * N_PIECES + [wspec(wg), wspec(wu), wspec(wd)],
            out_specs=[pl.BlockSpec(memory_space=pl.ANY)] * N_PIECES,
            scratch_shapes=[pltpu.VMEM(a.shape[1:], BF16) for a in (wg, wu, wd)]
            + [pltpu.VMEM((2, N_PIECES, group_tile, LANES), I32), pltpu.SemaphoreType.DMA((2, N_PIECES))] * 2),
        out_shape=[jax.ShapeDtypeStruct((n_rows, LANES), I32)] * N_PIECES,
        compiler_params=_cparams("arbitrary"),
        name="experts",
    )(tile_expert, n_used, *x_pieces, wg, wu, wd)


def _moe_out_kernel(x1_ref, mod_ref, fg_ref, sg_ref, su_ref, sd_ref, *refs):
    h_refs = refs[:N_PIECES]
    routed_refs = refs[N_PIECES:3 * N_PIECES]
    out_ref = refs[3 * N_PIECES]
    h_lo, h_hi = _unpack_rows([r[...] for r in h_refs])
    h = jnp.concatenate(h_lo + h_hi, axis=1).astype(BF16)
    hid = _silu(_dot(h, sg_ref[...])) * _dot(h, su_ref[...])
    shared = _dot(hid.astype(BF16), sd_ref[...])
    routed = jnp.concatenate([r[...] for r in routed_refs], axis=1)
    x2 = x1_ref[...] + mod_ref[0, 5:6, :] * (shared + routed)
    out_ref[...] = x2 * lax.rsqrt(jnp.mean(x2 * x2, axis=-1, keepdims=True) + EPS) * fg_ref[...]


def _moe_out(h2_pieces, x1, mods, final_g, sg, su, sd, routed_pieces, *, tokens_per_mod, tm):
    t = x1.shape[0]
    tiles_per_mod = tokens_per_mod // tm
    row = lambda w: pl.BlockSpec((tm, w), lambda i: (i, 0))
    full = lambda a: pl.BlockSpec(a.shape, lambda i: (0,) * a.ndim)
    return pl.pallas_call(
        _moe_out_kernel,
        grid=(t // tm,),
        in_specs=[row(D_MODEL),
                  pl.BlockSpec((1, 6, D_MODEL), lambda i: (i // tiles_per_mod, 0, 0)),
                  full(final_g), full(sg), full(su), full(sd)]
        + [row(LANES)] * (3 * N_PIECES),
        out_specs=row(D_MODEL),
        out_shape=jax.ShapeDtypeStruct((t, D_MODEL), F32),
        compiler_params=_cparams("parallel"),
        name="moe_out",
    )(x1, mods, final_g, sg, su, sd, *h2_pieces, *routed_pieces)


def _trunk(x, mods, s0, w, *, batch, seq_len, on_grid):
    t = batch * seq_len
    tokens_per_mod = t // mods.shape[0]
    tm_in = 512
    cos_t, sin_t = _rope_tables(max(seq_len, tm_in))
    q, k, v, gsw, up, ga, gb = _inproj(x, mods, w["norm1_g"], w["w_in"], cos_t, sin_t,
                                       tokens_per_mod=tokens_per_mod, seq_len=seq_len,
                                       on_grid=on_grid, tm=tm_in)
    z, s_f, s_b = _retention(q, k, v, gsw, w["dec"], s0, batch=batch, seq_len=seq_len)
    p = _pool(up, w["pool_w"], w["pool_scale"], batch=batch, seq_len=seq_len, on_grid=on_grid)
    x1, h2_pieces = _merge(x, z, p, ga, gb, mods, w["norm2_g"], w["w_br_ret"], w["w_br_pool"],
                               w["w_out"], tokens_per_mod=tokens_per_mod, tm=512)

    group_tile = _group_tile(t)
    n_rows = t * TOP_K + N_EXPERTS * group_tile
    idx, rank, wts, counts = _route(h2_pieces, w["router_wt"], w["router_bias"], tm=1024)
    pos, tile_expert, n_used = _plan(idx, rank, counts, n_tiles=n_rows // group_tile, tf=2048,
                                     group_tile=group_tile)
    x_sorted = _sc_dispatch(h2_pieces, pos, n_rows=n_rows)
    y_sorted = _experts(x_sorted, tile_expert.reshape(-1), n_used.reshape(-1),
                        w["exp_w_gate"], w["exp_w_up"], w["exp_w_down"], group_tile=group_tile)
    regroup = lambda a: a.reshape(TOP_K, t // SC_GROUP, SC_GROUP).transpose(1, 0, 2)
    pos_rows = pos.transpose(1, 0, 2).reshape(TOP_K, t)
    routed = _sc_combine(y_sorted, regroup(pos_rows), regroup(wts), n_tokens=t)
    y = _moe_out(h2_pieces, x1, mods, w["final_g"], w["sh_w_gate"], w["sh_w_up"], w["sh_w_down"], routed,
                 tokens_per_mod=tokens_per_mod, tm=512)
    return y, s_f, s_b


def kernel(x_prompt, x_sample, state_ret_fwd, state_ret_bwd, c, c_ctx, ada_w, ada_b, norm1_g, norm2_g, w_in,
           ret_decay_fwd, ret_decay_bwd, w_br_ret, pool_w, pool_scale, w_br_pool, w_out, router_w, router_bias,
           exp_w_gate, exp_w_up, exp_w_down, sh_w_gate, sh_w_up, sh_w_down, final_norm_g):
    n_req, seq, d = x_prompt.shape
    n_dec, dec_seq, _ = x_sample.shape
    depth = ada_w.shape[0]
    assert depth == 1 and d == D_MODEL

    xc = x_prompt.reshape(n_req * seq, d)
    xs = x_sample.reshape(n_dec * dec_seq, d)
    new_f, new_b = [], []
    for l in range(depth):
        c_rows = jnp.concatenate([c_ctx[None, :], c, jnp.zeros((8 - 1 - n_dec, d), F32)], axis=0)
        mods = _ada(c_rows, ada_w[l], ada_b[l]).reshape(8, 6, d)
        pad_rows = LANES - N_EXPERTS
        w = dict(
            norm1_g=norm1_g[l].reshape(1, d), norm2_g=norm2_g[l].reshape(1, d),
            final_g=final_norm_g.reshape(1, d),
            w_in=w_in[l].astype(BF16),
            dec=jnp.stack([ret_decay_fwd[l], ret_decay_bwd[l]]).astype(F32),
            w_br_ret=w_br_ret[l].astype(BF16), pool_w=pool_w[l].astype(BF16),
            pool_scale=pool_scale[l].reshape(1, POOL_W), w_br_pool=w_br_pool[l].astype(BF16),
            w_out=w_out[l].astype(BF16),
            router_wt=jnp.pad(router_w[l].T, ((0, pad_rows), (0, 0))).astype(BF16),
            router_bias=jnp.pad(router_bias[l].astype(F32).reshape(N_EXPERTS, 1), ((0, pad_rows), (0, 0))),
            exp_w_gate=exp_w_gate[l], exp_w_up=exp_w_up[l], exp_w_down=exp_w_down[l],
            sh_w_gate=sh_w_gate[l].astype(BF16),
            sh_w_up=sh_w_up[l].astype(BF16), sh_w_down=sh_w_down[l].astype(BF16),
        )
        cached = (state_ret_fwd[:, l].astype(F32), state_ret_bwd[:, l].astype(F32))
        xs, _, _ = _trunk(xs, mods[1:1 + n_dec], cached, w, batch=n_dec, seq_len=dec_seq, on_grid=True)
        xc, s_f, s_b = _trunk(xc, mods[0:1], None, w, batch=n_req, seq_len=seq, on_grid=False)
        new_f.append(s_f)
        new_b.append(s_b)
    y_prompt = xc.reshape(n_req, seq, d)
    y_sample = xs.reshape(n_dec, dec_seq, d)
    return (y_prompt, y_sample, jnp.stack(new_f, axis=1).astype(x_prompt.dtype),
            jnp.stack(new_b, axis=1).astype(x_prompt.dtype))
```

```python
import functools
import math

import numpy as np
import jax
import jax.numpy as jnp
from jax import lax
from jax.experimental import pallas as pl
from jax.experimental.pallas import tpu as pltpu
from jax.experimental.pallas import tpu_sc as plsc

D_MODEL = 1024
GRID_W = 64
RET_HEADS = 4
RET_DK = 128
RET_DV = 256
RET_QK_W = RET_HEADS * RET_DK
RET_V_W = RET_HEADS * RET_DV
RET_CHUNK = 128
ROPE_BASE = 10000.0
POOL_GROUPS = 4
POOL_CH = 128
POOL_W = POOL_GROUPS * POOL_CH
POOL_WINDOWS = (2, 4, 8, 16)
N_EXPERTS = 64
TOP_K = 8
N_EXPERT_GROUPS = 8
GROUP_SIZE = N_EXPERTS // N_EXPERT_GROUPS
TOPK_GROUPS = 4
D_EXPERT = 256
ROUTED_SCALE = 2.5
EPS = 1e-6
IN_SIZES = (RET_QK_W, RET_QK_W, RET_V_W, RET_V_W, POOL_W, D_MODEL, D_MODEL)
IN_OFFS = tuple(sum(IN_SIZES[:i]) for i in range(len(IN_SIZES) + 1))
IN_W = IN_OFFS[-1]

LANES = 128
VMEM_LIMIT = 56 << 20
N_PIECES = D_MODEL // 2 // LANES
MXU_DIM = 256
SC_CHUNK = 128

F32 = jnp.float32
BF16 = jnp.bfloat16
I32 = jnp.int32
U32 = jnp.uint32


def _cparams(*sem):
    return pltpu.CompilerParams(dimension_semantics=sem, vmem_limit_bytes=VMEM_LIMIT)


def _dot(a, b):
    return jnp.dot(a, b, preferred_element_type=F32)


def _silu(x):
    return x * jax.nn.sigmoid(x)


def _rms_mod(x, g, scale, shift):
    y = x * lax.rsqrt(jnp.mean(x * x, axis=-1, keepdims=True) + EPS)
    return (y * g) * (1.0 + scale) + shift


def _ada_kernel(c_ref, w_ref, b_ref, o_ref):
    c = c_ref[...]
    o_ref[...] = jnp.dot(_silu(c), w_ref[...], preferred_element_type=F32,
                         precision=lax.Precision.HIGHEST) + b_ref[...]


def _ada(c_rows, ada_w, ada_b):
    r = c_rows.shape[0]
    n = ada_w.shape[1]
    tn = 2 * D_MODEL
    return pl.pallas_call(
        _ada_kernel,
        grid=(n // tn,),
        in_specs=[pl.BlockSpec((r, D_MODEL), lambda j: (0, 0)),
                  pl.BlockSpec((D_MODEL, tn), lambda j: (0, j)),
                  pl.BlockSpec((1, tn), lambda j: (0, j))],
        out_specs=pl.BlockSpec((r, tn), lambda j: (0, j)),
        out_shape=jax.ShapeDtypeStruct((r, n), F32),
        compiler_params=_cparams("parallel"),
        name="ada_mod",
    )(c_rows, ada_w, ada_b.reshape(1, n))


def _inproj_kernel(x_ref, mod_ref, g_ref, w_ref, cos_ref, sin_ref,
                   q_ref, k_ref, v_ref, gsw_ref, up_ref, ga_ref, gb_ref, *, on_grid):
    for s in range(x_ref.shape[0] // MXU_DIM):
        rows = slice(s * MXU_DIM, (s + 1) * MXU_DIM)
        h = _rms_mod(x_ref[rows, :], g_ref[...], mod_ref[0, 1:2, :], mod_ref[0, 0:1, :]).astype(BF16)

        def seg(i):
            return _dot(h, w_ref[:, IN_OFFS[i]:IN_OFFS[i + 1]])

        q = seg(0)
        k = seg(1)
        if on_grid:
            cos = jnp.concatenate([cos_ref[rows, :]] * RET_HEADS, axis=1)
            sin = jnp.concatenate([sin_ref[rows, :]] * RET_HEADS, axis=1)
            lane = lax.broadcasted_iota(jnp.int32, q.shape, 1)
            first = (lane & 63) < 32

            def rope(a):
                up = pltpu.roll(a, RET_QK_W - 32, axis=1)
                dn = pltpu.roll(a, 32, axis=1)
                return a * cos + jnp.where(first, up, dn) * sin

            q = rope(q)
            k = rope(k)
        q_ref[rows, :] = q.astype(BF16)
        k_ref[rows, :] = (k * (RET_DK ** -0.5)).astype(BF16)
        v_ref[rows, :] = seg(2).astype(BF16)
        gsw_ref[rows, :] = seg(3).astype(BF16)
        up_ref[rows, :] = seg(4).astype(BF16)
        ga_ref[rows, :] = seg(5).astype(BF16)
        gb_ref[rows, :] = seg(6).astype(BF16)


def _inproj(x, mods, norm_g, w_in, cos_t, sin_t, *, tokens_per_mod, seq_len, on_grid, tm):
    t = x.shape[0]
    tiles_per_mod = tokens_per_mod // tm
    tiles_per_seq = max(seq_len // tm, 1)
    widths = IN_SIZES
    out_shape = [jax.ShapeDtypeStruct((t, w), BF16) for w in widths]
    out_specs = [pl.BlockSpec((tm, w), lambda i: (i, 0)) for w in widths]
    return pl.pallas_call(
        functools.partial(_inproj_kernel, on_grid=on_grid),
        grid=(t // tm,),
        in_specs=[pl.BlockSpec((tm, D_MODEL), lambda i: (i, 0)),
                  pl.BlockSpec((1, 6, D_MODEL), lambda i: (i // tiles_per_mod, 0, 0)),
                  pl.BlockSpec((1, D_MODEL), lambda i: (0, 0)),
                  pl.BlockSpec((D_MODEL, IN_W), lambda i: (0, 0)),
                  pl.BlockSpec((tm, RET_DK), lambda i: (i % tiles_per_seq, 0)),
                  pl.BlockSpec((tm, RET_DK), lambda i: (i % tiles_per_seq, 0))],
        out_specs=out_specs,
        out_shape=out_shape,
        compiler_params=_cparams("parallel"),
        name="inproj_grid" if on_grid else "inproj_seq",
    )(x, mods, norm_g, w_in, cos_t, sin_t)


def _rope_tables(seq_len):
    t = np.arange(seq_len)
    row = (t // GRID_W).astype(np.float32)
    col = (t % GRID_W).astype(np.float32)
    m = RET_DK // 4
    inv = (np.float32(ROPE_BASE) ** (-np.arange(m, dtype=np.float32) / np.float32(m))).astype(np.float32)
    ar = row[:, None] * inv
    ac = col[:, None] * inv
    cos = np.concatenate([np.cos(ar), np.cos(ar), np.cos(ac), np.cos(ac)], axis=1)
    sin = np.concatenate([-np.sin(ar), np.sin(ar), -np.sin(ac), np.sin(ac)], axis=1)
    return jnp.asarray(cos, F32), jnp.asarray(sin, F32)


def _ret_heads_per_step(seq_len):
    per_head = seq_len * (2 * 2 * (2 * RET_DK + 3 * RET_DV) + 4 * RET_DV + 2 * RET_DK)
    heads = RET_HEADS
    while heads > 1 and heads * per_head > VMEM_LIMIT * 3 // 4:
        heads //= 2
    return heads


def _ret_kernel(dec_ref, q_ref, k_ref, v_ref, g_ref, *refs, n_chunks, heads, zero_init):
    s0_refs = () if zero_init else refs[:2]
    z_ref, sf_ref, sb_ref, oacc_ref, kt_ref = refs[len(s0_refs):]
    c = RET_CHUNK
    half = n_chunks // 2
    ii = lax.broadcasted_iota(I32, (c, c), 0)
    jj = lax.broadcasted_iota(I32, (c, c), 1)
    ik = lax.broadcasted_iota(I32, (c, RET_DK), 0).astype(F32)
    jk = lax.broadcasted_iota(I32, (RET_DK, c), 1).astype(F32)

    def log_gamma(d, shape):
        return jnp.log1p(-jnp.exp2(-jnp.full(shape, d, F32)))

    consts = {}
    for hh in range(heads):
        h = pl.program_id(1) * heads + hh
        dec_f = dec_ref[0, h]
        dec_b = dec_ref[1, h]
        rel = (ii - jj).astype(F32)
        consts[hh, "f"] = (
            jnp.where(rel >= 0, jnp.exp(log_gamma(dec_f, (c, c)) * jnp.maximum(rel, 0.0)), 0.0),
            jnp.exp(log_gamma(dec_f, (c, RET_DK)) * (ik + 1.0)),
            jnp.exp(log_gamma(dec_f, (RET_DK, c)) * (c - 1.0 - jk)),
            jnp.exp(log_gamma(dec_f, (RET_DK, RET_DV)) * c))
        consts[hh, "b"] = (
            jnp.where(rel <= 0, jnp.exp(log_gamma(dec_b, (c, c)) * jnp.maximum(-rel, 0.0)), 0.0),
            jnp.exp(log_gamma(dec_b, (c, RET_DK)) * (c - ik)),
            jnp.exp(log_gamma(dec_b, (RET_DK, c)) * jk),
            jnp.exp(log_gamma(dec_b, (RET_DK, RET_DV)) * c))

    for s_ref, s0_ref in zip((sf_ref, sb_ref), s0_refs or (None, None)):
        s_ref[...] = jnp.zeros(s_ref.shape, F32) if zero_init else s0_ref[...]

    def transpose_keys(ci, carry):
        r = pl.ds(pl.multiple_of(ci * c, c), c)
        for hh in range(heads):
            kt_ref[hh, ci] = k_ref[r, hh * RET_DK:(hh + 1) * RET_DK].T
        return carry

    lax.fori_loop(0, n_chunks, transpose_keys, 0)

    def scores(ci, hh, direction):
        r = pl.ds(pl.multiple_of(ci * c, c), c)
        kcols = slice(hh * RET_DK, (hh + 1) * RET_DK)
        qc = q_ref[r, kcols]
        sc = lax.dot_general(qc, k_ref[r, kcols], (((1,), (1,)), ((), ())), preferred_element_type=F32)
        return ci, hh, direction, r, qc, sc

    def advance(job):
        ci, hh, direction, r, qc, sc = job
        dmask, qdec, kdec, cdec = consts[hh, direction]
        s_ref = sf_ref if direction == "f" else sb_ref
        vc = v_ref[r, hh * RET_DV:(hh + 1) * RET_DV]
        s = s_ref[hh]
        lhs = jnp.concatenate([(sc * dmask).astype(BF16), (qc.astype(F32) * qdec).astype(BF16)], axis=1)
        o = _dot(lhs, jnp.concatenate([vc, s.astype(BF16)], axis=0))
        kd_t = (kt_ref[hh, ci].astype(F32) * kdec).astype(BF16)
        s_ref[hh] = s * cdec + _dot(kd_t, vc)
        return o

    def emit(job, o, second):
        _, hh, _, r, _, _ = job
        vcols = slice(hh * RET_DV, (hh + 1) * RET_DV)
        if not second:
            oacc_ref[hh, r, :] = o
        else:
            o = o + oacc_ref[hh, r, :]
            o = o * lax.rsqrt(jnp.mean(o * o, axis=-1, keepdims=True) + EPS)
            g = g_ref[r, vcols].astype(F32)
            z_ref[r, vcols] = (_silu(g) * o).astype(BF16)

    def body(second):
        def run(t, carry):
            jobs = [scores(ci, hh, d) for hh in range(heads)
                    for ci, d in ((t, "f"), (n_chunks - 1 - t, "b"))]
            outs = [advance(job) for job in jobs]
            for job, o in zip(jobs, outs):
                emit(job, o, second)
            return carry
        return run

    lax.fori_loop(0, half, body(False), 0, unroll=4 if half % 4 == 0 else 1)
    lax.fori_loop(half, n_chunks, body(True), 0, unroll=2 if half % 2 == 0 else 1)


def _retention(q, k, v, gsw, dec, s0, *, batch, seq_len):
    n_chunks = seq_len // RET_CHUNK
    assert n_chunks % 2 == 0
    heads = _ret_heads_per_step(seq_len)
    t = batch * seq_len
    st_spec = pl.BlockSpec((None, heads, RET_DK, RET_DV), lambda b, h: (b, h, 0, 0))
    st_shape = jax.ShapeDtypeStruct((batch, RET_HEADS, RET_DK, RET_DV), F32)
    kspec = pl.BlockSpec((seq_len, heads * RET_DK), lambda b, h: (b, h))
    vspec = pl.BlockSpec((seq_len, heads * RET_DV), lambda b, h: (b, h))
    return pl.pallas_call(
        functools.partial(_ret_kernel, n_chunks=n_chunks, heads=heads, zero_init=s0 is None),
        grid=(batch, RET_HEADS // heads),
        in_specs=[pl.BlockSpec(memory_space=pltpu.SMEM), kspec, kspec, vspec, vspec]
        + ([] if s0 is None else [st_spec, st_spec]),
        out_specs=[vspec, st_spec, st_spec],
        out_shape=[jax.ShapeDtypeStruct((t, RET_V_W), BF16), st_shape, st_shape],
        scratch_shapes=[pltpu.VMEM((heads, seq_len, RET_DV), F32),
                        pltpu.VMEM((heads, n_chunks, RET_DK, RET_CHUNK), BF16)],
        compiler_params=_cparams("parallel", "parallel"),
        name=f"retention_l{seq_len}",
    )(dec, q, k, v, gsw, *(s0 or ()))


def _pool_kernel(u_ref, w_ref, sc_ref, o_ref, *, n_tok, width, two_d):
    n_rows = n_tok // width
    pos = lax.broadcasted_iota(I32, (width, POOL_CH), 0)

    def every_row(a):
        return jnp.concatenate([a] * n_rows, axis=0) if n_rows > 1 else a

    def shift_in_row(a, s):
        ok = (pos < width - s) if s > 0 else (pos >= -s)
        return pltpu.roll(a, (-s) % n_tok, axis=0) * every_row(jnp.where(ok, 1.0, 0.0))

    def shift_rows(a, m):
        k = abs(m) * width
        zeros = jnp.zeros((k, POOL_CH), F32)
        return (jnp.concatenate([a[k:], zeros], axis=0) if m > 0
                else jnp.concatenate([zeros, a[:n_tok - k]], axis=0))

    def box_sum(a, half, shift):
        fw = a
        bw = shift(a, -1)
        m = 1
        while m < half:
            fw = fw + shift(fw, m)
            bw = bw + shift(bw, -m)
            m *= 2
        return fw + bw

    def inv_count(p, half, extent):
        return 1.0 / (jnp.minimum(p + half, extent) - jnp.maximum(p - half, 0)).astype(F32)

    for g, window in enumerate(POOL_WINDOWS):
        half = window // 2
        cols = slice(g * POOL_CH, (g + 1) * POOL_CH)
        ug = u_ref[:, cols].astype(F32)
        total = box_sum(ug, half, shift_in_row)
        inv = every_row(inv_count(pos, half, width))
        if two_d:
            total = box_sum(total, half, shift_rows)
            row = lax.broadcasted_iota(I32, (n_rows, 1, POOL_CH), 0)
            inv_r = jnp.broadcast_to(inv_count(row, half, n_rows), (n_rows, width, POOL_CH))
            inv = inv * inv_r.reshape(n_tok, POOL_CH)
        d = (total * inv - ug).astype(BF16)
        o_ref[:, cols] = (_dot(d, w_ref[g]) * sc_ref[:, cols]).astype(BF16)


def _pool(u, pool_w, pool_scale, *, batch, seq_len, on_grid):
    t = batch * seq_len
    width = GRID_W if on_grid else seq_len
    n_tok = seq_len if on_grid else seq_len * math.gcd(batch, 4)
    return pl.pallas_call(
        functools.partial(_pool_kernel, n_tok=n_tok, width=width, two_d=on_grid),
        grid=(t // n_tok,),
        in_specs=[pl.BlockSpec((n_tok, POOL_W), lambda b: (b, 0)),
                  pl.BlockSpec((POOL_GROUPS, POOL_CH, POOL_CH), lambda b: (0, 0, 0)),
                  pl.BlockSpec((1, POOL_W), lambda b: (0, 0))],
        out_specs=pl.BlockSpec((n_tok, POOL_W), lambda b: (b, 0)),
        out_shape=jax.ShapeDtypeStruct((t, POOL_W), BF16),
        compiler_params=_cparams("parallel"),
        name=f"pool_l{seq_len}",
    )(u, pool_w, pool_scale)


def _pack_rows(x):
    half = D_MODEL // 2
    lo = lax.bitcast_convert_type(x[:, :half].astype(BF16).astype(F32), U32) >> 16
    hi = lax.bitcast_convert_type(x[:, half:].astype(BF16).astype(F32), U32) & jnp.uint32(0xFFFF0000)
    word = lax.bitcast_convert_type(hi | lo, I32)
    return [word[:, c * LANES:(c + 1) * LANES] for c in range(N_PIECES)]


def _unpack_rows(pieces):
    words = [lax.bitcast_convert_type(p, U32) for p in pieces]
    lo = [lax.bitcast_convert_type(w << 16, F32) for w in words]
    hi = [lax.bitcast_convert_type(w & jnp.uint32(0xFFFF0000), F32) for w in words]
    return lo, hi


def _merge_kernel(x_ref, z_ref, p_ref, ga_ref, gb_ref, mod_ref, g2_ref, wr_ref, wp_ref, wo_ref,
                  x1_ref, *piece_refs):
    y_ret = _dot(z_ref[...], wr_ref[...])
    y_pool = _dot(p_ref[...], wp_ref[...])
    merged = (jax.nn.sigmoid(ga_ref[...].astype(F32)) * y_ret
              + jax.nn.sigmoid(gb_ref[...].astype(F32)) * y_pool)
    x1 = x_ref[...] + mod_ref[0, 2:3, :] * _dot(merged.astype(BF16), wo_ref[...])
    x1_ref[...] = x1
    h2 = _rms_mod(x1, g2_ref[...], mod_ref[0, 4:5, :], mod_ref[0, 3:4, :])
    for ref, piece in zip(piece_refs, _pack_rows(h2)):
        ref[...] = piece


def _merge(x, z, p, ga, gb, mods, norm2_g, w_br_ret, w_br_pool, w_out, *, tokens_per_mod, tm):
    t = x.shape[0]
    tiles_per_mod = tokens_per_mod // tm
    row = lambda w: pl.BlockSpec((tm, w), lambda i: (i, 0))
    full = lambda a: pl.BlockSpec(a.shape, lambda i: (0,) * a.ndim)
    outs = pl.pallas_call(
        _merge_kernel,
        grid=(t // tm,),
        in_specs=[row(D_MODEL), row(RET_V_W), row(POOL_W), row(D_MODEL), row(D_MODEL),
                  pl.BlockSpec((1, 6, D_MODEL), lambda i: (i // tiles_per_mod, 0, 0)),
                  full(norm2_g), full(w_br_ret), full(w_br_pool), full(w_out)],
        out_specs=[row(D_MODEL)] + [row(LANES)] * N_PIECES,
        out_shape=[jax.ShapeDtypeStruct((t, D_MODEL), F32)] + [jax.ShapeDtypeStruct((t, LANES), I32)] * N_PIECES,
        compiler_params=_cparams("parallel"),
        name="merge",
    )(x, z, p, ga, gb, mods, norm2_g, w_br_ret, w_br_pool, w_out)
    return outs[0], outs[1:]


def _route_kernel(*refs):
    h_refs = refs[:N_PIECES]
    rw_ref, bias_ref, idx_ref, rank_ref, wk_ref, cnt_ref, carry_ref, before_ref = refs[N_PIECES:]
    e = N_EXPERTS
    tm = h_refs[0].shape[0]
    neg = -jnp.inf

    @pl.when(pl.program_id(0) == 0)
    def _():
        carry_ref[...] = jnp.zeros(carry_ref.shape, F32)
        t_row = lax.broadcasted_iota(I32, (tm, tm), 0)
        t_col = lax.broadcasted_iota(I32, (tm, tm), 1)
        before_ref[...] = (t_row < t_col).astype(BF16)

    lo, hi = _unpack_rows([r[...] for r in h_refs])
    h = jnp.concatenate(lo + hi, axis=1).astype(BF16)
    logits = lax.dot_general(rw_ref[...], h, (((1,), (1,)), ((), ())), preferred_element_type=F32)[:e]
    scores = jax.nn.sigmoid(logits)
    sel = scores + bias_ref[:e, 0:1]
    e_idx = lax.broadcasted_iota(I32, (e, tm), 0)

    grp = sel.reshape(N_EXPERT_GROUPS, GROUP_SIZE, tm)
    m_idx = lax.broadcasted_iota(I32, grp.shape, 1)
    m1 = jnp.max(grp, axis=1, keepdims=True)
    first = jnp.min(jnp.where(grp == m1, m_idx, GROUP_SIZE), axis=1, keepdims=True)
    m2 = jnp.max(jnp.where(m_idx == first, neg, grp), axis=1, keepdims=True)
    gscore = (m1 + m2).reshape(N_EXPERT_GROUPS, tm)

    g_idx = lax.broadcasted_iota(I32, gscore.shape, 0)
    grank = jnp.zeros(gscore.shape, I32)
    for g in range(N_EXPERT_GROUPS):
        other = gscore[g:g + 1, :]
        beats = jnp.where(other > gscore, 1, jnp.where(other == gscore, (g_idx > g).astype(I32), 0))
        grank = grank + beats
    gkeep = (grank < TOPK_GROUPS).astype(F32)
    ekeep = jnp.broadcast_to(gkeep.reshape(N_EXPERT_GROUPS, 1, tm), grp.shape).reshape(e, tm)
    masked = jnp.where(ekeep > 0, sel, neg)

    chosen = jnp.zeros((e, tm), F32)
    picks, hits = [], []
    for _ in range(TOP_K):
        m = jnp.max(masked, axis=0, keepdims=True)
        pick = jnp.min(jnp.where(masked == m, e_idx, e), axis=0, keepdims=True)
        hit = e_idx == pick
        chosen = jnp.where(hit, 1.0, chosen)
        masked = jnp.where(hit, neg, masked)
        picks.append(pick)
        hits.append(hit)

    w = scores * chosen
    comb = w / jnp.sum(w, axis=0, keepdims=True) * ROUTED_SCALE

    rankmat = _dot(chosen.astype(BF16), before_ref[...]) + carry_ref[:e, 0:1]
    carry_ref[:e, :] = carry_ref[:e, :] + jnp.sum(chosen, axis=1, keepdims=True)
    cnt_ref[...] = carry_ref[...]

    idx_ref[...] = jnp.concatenate(picks, axis=0)
    rank_ref[...] = jnp.concatenate(
        [jnp.sum(jnp.where(h, rankmat, 0.0), axis=0, keepdims=True) for h in hits], axis=0).astype(I32)
    wk_ref[...] = jnp.concatenate(
        [jnp.sum(jnp.where(h, comb, 0.0), axis=0, keepdims=True) for h in hits], axis=0)


def _route(h2_pieces, router_wt, bias_col, *, tm):
    t = h2_pieces[0].shape[0]
    krow = pl.BlockSpec((TOP_K, tm), lambda i: (0, i))
    return pl.pallas_call(
        _route_kernel,
        grid=(t // tm,),
        in_specs=[pl.BlockSpec((tm, LANES), lambda i: (i, 0))] * N_PIECES
        + [pl.BlockSpec((LANES, D_MODEL), lambda i: (0, 0)), pl.BlockSpec((LANES, 1), lambda i: (0, 0))],
        out_specs=[krow, krow, krow, pl.BlockSpec((LANES, LANES), lambda i: (0, 0))],
        out_shape=[jax.ShapeDtypeStruct((TOP_K, t), I32), jax.ShapeDtypeStruct((TOP_K, t), I32),
                   jax.ShapeDtypeStruct((TOP_K, t), F32), jax.ShapeDtypeStruct((LANES, LANES), F32)],
        scratch_shapes=[pltpu.VMEM((LANES, LANES), F32), pltpu.VMEM((tm, tm), BF16)],
        compiler_params=_cparams("arbitrary"),
        name="route",
    )(*h2_pieces, router_wt, bias_col)


def _plan_kernel(idx_ref, rank_ref, cnt_ref, pos_ref, te_ref, nv_ref, nu_ref, *, group_tile):
    tf = idx_ref.shape[1]
    nt = te_ref.shape[1]
    cnt = cnt_ref[...].astype(I32)
    padded = (((cnt + (group_tile - 1)) // group_tile) * group_tile).astype(F32)
    e_sub = lax.broadcasted_iota(I32, (LANES, LANES), 0)
    e_lane = lax.broadcasted_iota(I32, (LANES, LANES), 1)
    base = jnp.sum(jnp.where(e_lane < e_sub, padded.T, 0.0), axis=1, keepdims=True)
    end = base + padded[:, 0:1]

    idx = idx_ref[...]
    start = jnp.zeros(idx.shape, F32)
    for e in range(N_EXPERTS):
        start = jnp.where(idx == e, base[e:e + 1, 0:1], start)
    pos = start.astype(I32) + rank_ref[...]
    for j in range(tf // SC_CHUNK):
        pos_ref[j] = pos[:, j * SC_CHUNK:(j + 1) * SC_CHUNK]

    tile_start = (lax.broadcasted_iota(I32, (N_EXPERTS, nt), 1) * group_tile).astype(F32)
    done = jnp.sum(jnp.where(end[:N_EXPERTS] <= tile_start, 1.0, 0.0), axis=0, keepdims=True)
    te_ref[...] = jnp.minimum(done, N_EXPERTS - 1.0).astype(I32)
    in_group = (base[:N_EXPERTS] <= tile_start) & (tile_start < end[:N_EXPERTS])
    real = jnp.clip(base[:N_EXPERTS] + cnt[:N_EXPERTS, 0:1].astype(F32) - tile_start, 0.0, float(group_tile))
    nv_ref[...] = jnp.sum(jnp.where(in_group, real, 0.0), axis=0, keepdims=True).astype(I32)
    total = jnp.sum(padded[:, 0:1], axis=0, keepdims=True)
    nu_ref[...] = jnp.broadcast_to(total * (1.0 / group_tile), nu_ref.shape).astype(I32)


def _plan(idx, rank, counts, *, n_tiles, tf, group_tile):
    t = idx.shape[1]
    nt_pad = -(-n_tiles // LANES) * LANES
    krow = pl.BlockSpec((TOP_K, tf), lambda i: (0, i))
    return pl.pallas_call(
        functools.partial(_plan_kernel, group_tile=group_tile),
        grid=(t // tf,),
        in_specs=[krow, krow, pl.BlockSpec((LANES, LANES), lambda i: (0, 0))],
        out_specs=[pl.BlockSpec((tf // SC_CHUNK, TOP_K, SC_CHUNK), lambda i: (i, 0, 0)),
                   pl.BlockSpec((1, nt_pad), lambda i: (0, 0)),
                   pl.BlockSpec((1, nt_pad), lambda i: (0, 0)),
                   pl.BlockSpec((1, LANES), lambda i: (0, 0))],
        out_shape=[jax.ShapeDtypeStruct((t // SC_CHUNK, TOP_K, SC_CHUNK), I32),
                   jax.ShapeDtypeStruct((1, nt_pad), I32), jax.ShapeDtypeStruct((1, nt_pad), I32),
                   jax.ShapeDtypeStruct((1, LANES), I32)],
        compiler_params=_cparams("arbitrary"),
        name="moe_plan",
    )(idx, rank, counts)


def _sc_mesh_info():
    info = plsc.get_sparse_core_info()
    mesh = plsc.VectorSubcoreMesh(core_axis_name="c", subcore_axis_name="s")
    return mesh, info.num_cores, info.num_cores * info.num_subcores


def _sc_dispatch(pieces, pos, *, n_rows):
    t = pieces[0].shape[0]
    mesh, n_cores, n_workers = _sc_mesh_info()
    per_w = t // SC_CHUNK // n_workers

    @functools.partial(
        pl.kernel, mesh=mesh,
        out_type=[jax.ShapeDtypeStruct((n_rows, LANES), I32)] * N_PIECES,
        scratch_types=[pltpu.VMEM((TOP_K, SC_CHUNK), I32),
                       pltpu.VMEM((N_PIECES, SC_CHUNK, LANES), I32),
                       pltpu.SemaphoreType.DMA((N_PIECES,)),
                       pltpu.SemaphoreType.DMA],
        name="sc_dispatch",
    )
    def run(*refs):
        src = refs[:N_PIECES]
        pos_hbm = refs[N_PIECES]
        dst = refs[N_PIECES + 1:2 * N_PIECES + 1]
        idx_v, rows_v, load_sem, put_sem = refs[2 * N_PIECES + 1:]
        wid = lax.axis_index("s") * n_cores + lax.axis_index("c")

        @pl.loop(0, per_w)
        def _(j):
            ch = wid * per_w + j
            t0 = pl.multiple_of(ch * SC_CHUNK, SC_CHUNK)
            loads = [pltpu.make_async_copy(src[c].at[pl.ds(t0, SC_CHUNK)], rows_v.at[c], load_sem.at[c])
                     for c in range(N_PIECES)]
            for ld in loads:
                ld.start()
            pltpu.sync_copy(pos_hbm.at[ch], idx_v)
            puts = []
            for c in range(N_PIECES):
                loads[c].wait()
                for k in range(TOP_K):
                    puts.append(pltpu.make_async_copy(rows_v.at[c], dst[c].at[idx_v.at[k]], put_sem))
                    puts[-1].start()
            for cp in puts:
                cp.wait()

    return run(*pieces, pos)


SC_GROUP = 32


def _sc_combine(pieces, pos, wts, *, n_tokens):
    mesh, n_cores, n_workers = _sc_mesh_info()
    lanes = plsc.get_sparse_core_info().num_lanes
    per_w = n_tokens // SC_GROUP // n_workers

    @functools.partial(
        pl.kernel, mesh=mesh,
        out_type=[jax.ShapeDtypeStruct((n_tokens, LANES), F32)] * (2 * N_PIECES),
        scratch_types=[pltpu.VMEM((TOP_K, SC_GROUP), I32),
                       pltpu.VMEM((TOP_K, SC_GROUP), F32),
                       pltpu.VMEM((2, TOP_K, SC_GROUP, LANES), I32),
                       pltpu.VMEM((2, 2, SC_GROUP, LANES), F32),
                       pltpu.SemaphoreType.DMA((2,)),
                       pltpu.SemaphoreType.DMA((2,))],
        compiler_params=pltpu.CompilerParams(needs_layout_passes=False),
        name="sc_combine",
    )
    def run(*refs):
        src = refs[:N_PIECES]
        pos_hbm, wts_hbm = refs[N_PIECES:N_PIECES + 2]
        dst = refs[N_PIECES + 2:3 * N_PIECES + 2]
        idx_v, w_v, buf, acc, get_sem, put_sem = refs[3 * N_PIECES + 2:]
        wid = lax.axis_index("s") * n_cores + lax.axis_index("c")

        @pl.loop(0, per_w)
        def _(j):
            grp = wid * per_w + j
            t0 = pl.multiple_of(grp * SC_GROUP, SC_GROUP)
            pltpu.sync_copy(pos_hbm.at[grp], idx_v)
            pltpu.sync_copy(wts_hbm.at[grp], w_v)

            def gets(c, slot):
                return [pltpu.make_async_copy(src[c].at[idx_v.at[k]], buf.at[slot, k], get_sem.at[slot])
                        for k in range(TOP_K)]

            def puts(c, slot):
                r = pl.ds(t0, SC_GROUP)
                return [pltpu.make_async_copy(acc.at[slot, 0], dst[c].at[r], put_sem.at[slot]),
                        pltpu.make_async_copy(acc.at[slot, 1], dst[N_PIECES + c].at[r], put_sem.at[slot])]

            for cp in gets(0, 0):
                cp.start()
            for c in range(N_PIECES):
                slot = c % 2
                if c + 1 < N_PIECES:
                    for cp in gets(c + 1, 1 - slot):
                        cp.start()
                for cp in gets(c, slot):
                    cp.wait()
                if c >= 2:
                    for cp in puts(c - 2, slot):
                        cp.wait()

                @pl.loop(0, SC_GROUP)
                def _(r):
                    row = jnp.full((lanes,), r, I32)
                    w = [plsc.load_gather(w_v, [jnp.full((lanes,), k, I32), row]) for k in range(TOP_K)]
                    for q in range(LANES // lanes):
                        cols = pl.ds(q * lanes, lanes)
                        lo = hi = None
                        for k in range(TOP_K):
                            word = buf[slot, k, r, cols]
                            lo_k = plsc.bitcast(word << 16, F32) * w[k]
                            hi_k = plsc.bitcast(word & jnp.int32(-65536), F32) * w[k]
                            lo = lo_k if lo is None else lo + lo_k
                            hi = hi_k if hi is None else hi + hi_k
                        acc[slot, 0, r, cols] = lo
                        acc[slot, 1, r, cols] = hi

                for cp in puts(c, slot):
                    cp.start()
            for c in range(N_PIECES - 2, N_PIECES):
                for cp in puts(c, c % 2):
                    cp.wait()

    return run(*pieces, pos, wts)


def _group_tile(n_tokens):
    per_expert = n_tokens * TOP_K // N_EXPERTS
    return max(MXU_DIM, min(4 * MXU_DIM, per_expert // MXU_DIM * MXU_DIM))


SECOND_DMA_QUEUE = 1


def _experts_kernel(te_ref, nv_ref, nu_ref, *refs, tile):
    x_hbm = refs[:N_PIECES]
    w_f32 = refs[N_PIECES:N_PIECES + 3]
    y_hbm = refs[N_PIECES + 3:2 * N_PIECES + 3]
    wg_ref, wu_ref, wd_ref, xbuf, xsem, ybuf, ysem = refs[2 * N_PIECES + 3:]
    i = pl.program_id(0)
    n_used = nu_ref[0]
    slot = lax.rem(i, 2)

    half = tile // 2

    def x_copy(step, into, h, c):
        r = pl.ds(pl.multiple_of(step * tile + h * half, half), half)
        return pltpu.make_async_copy(x_hbm[c].at[r], xbuf.at[into, c, pl.ds(h * half, half)], xsem.at[into, c])

    def y_copy(step, out_of, h, c):
        r = pl.ds(pl.multiple_of(step * tile + h * half, half), half)
        return pltpu.make_async_copy(ybuf.at[out_of, c, pl.ds(h * half, half)], y_hbm[c].at[r], ysem.at[out_of, c])

    def real_halves(copy, step, buf, act):
        for c in range(N_PIECES):
            act(copy(step, buf, 0, c))

        @pl.when(nv_ref[step] > half)
        def _():
            for c in range(N_PIECES):
                act(copy(step, buf, 1, c))

    start = lambda cp: cp.start(priority=SECOND_DMA_QUEUE)
    wait = lambda cp: cp.wait()

    @pl.when(i == 0)
    def _():
        real_halves(x_copy, 0, 0, start)

    @pl.when(i + 1 < n_used)
    def _():
        real_halves(x_copy, i + 1, 1 - slot, start)

    last = n_used - 1
    cur = te_ref[jnp.minimum(i, last)]
    prev = te_ref[jnp.minimum(jnp.maximum(i - 1, 0), last)]

    @pl.when((i == 0) | (cur != prev))
    def _():
        for dst, src in zip((wg_ref, wu_ref, wd_ref), w_f32):
            dst[...] = src[...].astype(BF16)

    @pl.when(i < n_used)
    def _():
        real_halves(x_copy, i, slot, wait)
        subs = [slice(s * MXU_DIM, (s + 1) * MXU_DIM) for s in range(tile // MXU_DIM)]
        xs = []
        for rows in subs:
            lo, hi = _unpack_rows([xbuf[slot, c, rows, :] for c in range(N_PIECES)])
            xs.append(jnp.concatenate(lo + hi, axis=1).astype(BF16))
        gates = [(_dot(x, wg_ref[...]), _dot(x, wu_ref[...])) for x in xs]
        ys = [_dot((_silu(g) * u).astype(BF16), wd_ref[...]) for g, u in gates]
        for rows, y in zip(subs, ys):
            for c, piece in enumerate(_pack_rows(y)):
                ybuf[slot, c, rows, :] = piece

        @pl.when(i >= 1)
        def _():
            real_halves(y_copy, i - 1, 1 - slot, wait)

        real_halves(y_copy, i, slot, start)

        @pl.when(i == n_used - 1)
        def _():
            real_halves(y_copy, i, slot, wait)


def _experts(x_pieces, tile_expert, tile_rows, n_used, wg, wu, wd, *, group_tile):
    n_rows = x_pieces[0].shape[0]
    n_tiles = n_rows // group_tile

    wspec = lambda a: pl.BlockSpec((None,) + a.shape[1:],
                                   lambda i, te, nv, nu: (te[jnp.minimum(i, nu[0] - 1)], 0, 0))
    return pl.pallas_call(
        functools.partial(_experts_kernel, tile=group_tile),
        grid_spec=pltpu.PrefetchScalarGridSpec(
            num_scalar_prefetch=3,
            grid=(n_tiles,),
            in_specs=[pl.BlockSpec(memory_space=pl.ANY)] * N_PIECES + [wspec(wg), wspec(wu), wspec(wd)],
            out_specs=[pl.BlockSpec(memory_space=pl.ANY)] * N_PIECES,
            scratch_shapes=[pltpu.VMEM(a.shape[1:], BF16) for a in (wg, wu, wd)]
            + [pltpu.VMEM((2, N_PIECES, group_tile, LANES), I32), pltpu.SemaphoreType.DMA((2, N_PIECES))] * 2),
        out_shape=[jax.ShapeDtypeStruct((n_rows, LANES), I32)] * N_PIECES,
        compiler_params=_cparams("arbitrary"),
        name="experts",
    )(tile_expert, tile_rows, n_used, *x_pieces, wg, wu, wd)


def _moe_out_kernel(x1_ref, mod_ref, fg_ref, sg_ref, su_ref, sd_ref, *refs):
    h_refs = refs[:N_PIECES]
    routed_refs = refs[N_PIECES:3 * N_PIECES]
    out_ref = refs[3 * N_PIECES]
    h_lo, h_hi = _unpack_rows([r[...] for r in h_refs])
    h = jnp.concatenate(h_lo + h_hi, axis=1).astype(BF16)
    hid = _silu(_dot(h, sg_ref[...])) * _dot(h, su_ref[...])
    shared = _dot(hid.astype(BF16), sd_ref[...])
    routed = jnp.concatenate([r[...] for r in routed_refs], axis=1)
    x2 = x1_ref[...] + mod_ref[0, 5:6, :] * (shared + routed)
    out_ref[...] = x2 * lax.rsqrt(jnp.mean(x2 * x2, axis=-1, keepdims=True) + EPS) * fg_ref[...]


def _moe_out(h2_pieces, x1, mods, final_g, sg, su, sd, routed_pieces, *, tokens_per_mod, tm):
    t = x1.shape[0]
    tiles_per_mod = tokens_per_mod // tm
    row = lambda w: pl.BlockSpec((tm, w), lambda i: (i, 0))
    full = lambda a: pl.BlockSpec(a.shape, lambda i: (0,) * a.ndim)
    return pl.pallas_call(
        _moe_out_kernel,
        grid=(t // tm,),
        in_specs=[row(D_MODEL),
                  pl.BlockSpec((1, 6, D_MODEL), lambda i: (i // tiles_per_mod, 0, 0)),
                  full(final_g), full(sg), full(su), full(sd)]
        + [row(LANES)] * (3 * N_PIECES),
        out_specs=row(D_MODEL),
        out_shape=jax.ShapeDtypeStruct((t, D_MODEL), F32),
        compiler_params=_cparams("parallel"),
        name="moe_out",
    )(x1, mods, final_g, sg, su, sd, *h2_pieces, *routed_pieces)


def _trunk(x, mods, s0, w, *, batch, seq_len, on_grid):
    t = batch * seq_len
    tokens_per_mod = t // mods.shape[0]
    tm_in = 512
    cos_t, sin_t = _rope_tables(max(seq_len, tm_in))
    q, k, v, gsw, up, ga, gb = _inproj(x, mods, w["norm1_g"], w["w_in"], cos_t, sin_t,
                                       tokens_per_mod=tokens_per_mod, seq_len=seq_len,
                                       on_grid=on_grid, tm=tm_in)
    z, s_f, s_b = _retention(q, k, v, gsw, w["dec"], s0, batch=batch, seq_len=seq_len)
    p = _pool(up, w["pool_w"], w["pool_scale"], batch=batch, seq_len=seq_len, on_grid=on_grid)
    x1, h2_pieces = _merge(x, z, p, ga, gb, mods, w["norm2_g"], w["w_br_ret"], w["w_br_pool"],
                               w["w_out"], tokens_per_mod=tokens_per_mod, tm=512)

    group_tile = _group_tile(t)
    n_rows = t * TOP_K + N_EXPERTS * group_tile
    idx, rank, wts, counts = _route(h2_pieces, w["router_wt"], w["router_bias"], tm=1024)
    pos, tile_expert, tile_rows, n_used = _plan(idx, rank, counts, n_tiles=n_rows // group_tile, tf=2048,
                                                group_tile=group_tile)
    x_sorted = _sc_dispatch(h2_pieces, pos, n_rows=n_rows)
    y_sorted = _experts(x_sorted, tile_expert.reshape(-1), tile_rows.reshape(-1), n_used.reshape(-1),
                        w["exp_w_gate"], w["exp_w_up"], w["exp_w_down"], group_tile=group_tile)
    regroup = lambda a: a.reshape(TOP_K, t // SC_GROUP, SC_GROUP).transpose(1, 0, 2)
    pos_rows = pos.transpose(1, 0, 2).reshape(TOP_K, t)
    routed = _sc_combine(y_sorted, regroup(pos_rows), regroup(wts), n_tokens=t)
    y = _moe_out(h2_pieces, x1, mods, w["final_g"], w["sh_w_gate"], w["sh_w_up"], w["sh_w_down"], routed,
                 tokens_per_mod=tokens_per_mod, tm=512)
    return y, s_f, s_b


def kernel(x_prompt, x_sample, state_ret_fwd, state_ret_bwd, c, c_ctx, ada_w, ada_b, norm1_g, norm2_g, w_in,
           ret_decay_fwd, ret_decay_bwd, w_br_ret, pool_w, pool_scale, w_br_pool, w_out, router_w, router_bias,
           exp_w_gate, exp_w_up, exp_w_down, sh_w_gate, sh_w_up, sh_w_down, final_norm_g):
    n_req, seq, d = x_prompt.shape
    n_dec, dec_seq, _ = x_sample.shape
    depth = ada_w.shape[0]
    assert depth == 1 and d == D_MODEL

    xc = x_prompt.reshape(n_req * seq, d)
    xs = x_sample.reshape(n_dec * dec_seq, d)
    new_f, new_b = [], []
    for l in range(depth):
        c_rows = jnp.concatenate([c_ctx[None, :], c, jnp.zeros((8 - 1 - n_dec, d), F32)], axis=0)
        mods = _ada(c_rows, ada_w[l], ada_b[l]).reshape(8, 6, d)
        pad_rows = LANES - N_EXPERTS
        w = dict(
            norm1_g=norm1_g[l].reshape(1, d), norm2_g=norm2_g[l].reshape(1, d),
            final_g=final_norm_g.reshape(1, d),
            w_in=w_in[l].astype(BF16),
            dec=jnp.stack([ret_decay_fwd[l], ret_decay_bwd[l]]).astype(F32),
            w_br_ret=w_br_ret[l].astype(BF16), pool_w=pool_w[l].astype(BF16),
            pool_scale=pool_scale[l].reshape(1, POOL_W), w_br_pool=w_br_pool[l].astype(BF16),
            w_out=w_out[l].astype(BF16),
            router_wt=jnp.pad(router_w[l].T, ((0, pad_rows), (0, 0))).astype(BF16),
            router_bias=jnp.pad(router_bias[l].astype(F32).reshape(N_EXPERTS, 1), ((0, pad_rows), (0, 0))),
            exp_w_gate=exp_w_gate[l], exp_w_up=exp_w_up[l], exp_w_down=exp_w_down[l],
            sh_w_gate=sh_w_gate[l].astype(BF16),
            sh_w_up=sh_w_up[l].astype(BF16), sh_w_down=sh_w_down[l].astype(BF16),
        )
        cached = (state_ret_fwd[:, l].astype(F32), state_ret_bwd[:, l].astype(F32))
        xs, _, _ = _trunk(xs, mods[1:1 + n_dec], cached, w, batch=n_dec, seq_len=dec_seq, on_grid=True)
        xc, s_f, s_b = _trunk(xc, mods[0:1], None, w, batch=n_req, seq_len=seq, on_grid=False)
        new_f.append(s_f)
        new_b.append(s_b)
    y_prompt = xc.reshape(n_req, seq, d)
    y_sample = xs.reshape(n_dec, dec_seq, d)
    return (y_prompt, y_sample, jnp.stack(new_f, axis=1).astype(x_prompt.dtype),
            jnp.stack(new_b, axis=1).astype(x_prompt.dtype))
```

```python
import functools
import math

import numpy as np
import jax
import jax.numpy as jnp
from jax import lax
from jax.experimental import pallas as pl
from jax.experimental.pallas import tpu as pltpu
from jax.experimental.pallas import tpu_sc as plsc

D_MODEL = 1024
GRID_W = 64
RET_HEADS = 4
RET_DK = 128
RET_DV = 256
RET_QK_W = RET_HEADS * RET_DK
RET_V_W = RET_HEADS * RET_DV
RET_CHUNK = 128
ROPE_BASE = 10000.0
POOL_GROUPS = 4
POOL_CH = 128
POOL_W = POOL_GROUPS * POOL_CH
POOL_WINDOWS = (2, 4, 8, 16)
N_EXPERTS = 64
TOP_K = 8
N_EXPERT_GROUPS = 8
GROUP_SIZE = N_EXPERTS // N_EXPERT_GROUPS
TOPK_GROUPS = 4
D_EXPERT = 256
ROUTED_SCALE = 2.5
EPS = 1e-6
IN_SIZES = (RET_QK_W, RET_QK_W, RET_V_W, RET_V_W, POOL_W, D_MODEL, D_MODEL)
IN_OFFS = tuple(sum(IN_SIZES[:i]) for i in range(len(IN_SIZES) + 1))
IN_W = IN_OFFS[-1]

LANES = 128
VMEM_LIMIT = 56 << 20
N_PIECES = D_MODEL // 2 // LANES
MXU_DIM = 256
SC_CHUNK = 128

F32 = jnp.float32
BF16 = jnp.bfloat16
I32 = jnp.int32
U32 = jnp.uint32


def _cparams(*sem):
    return pltpu.CompilerParams(dimension_semantics=sem, vmem_limit_bytes=VMEM_LIMIT)


def _dot(a, b):
    return jnp.dot(a, b, preferred_element_type=F32)


def _silu(x):
    return x * jax.nn.sigmoid(x)


def _rms_mod(x, g, scale, shift):
    y = x * lax.rsqrt(jnp.mean(x * x, axis=-1, keepdims=True) + EPS)
    return (y * g) * (1.0 + scale) + shift


def _ada_kernel(c_ref, w_ref, b_ref, o_ref):
    c = c_ref[...]
    o_ref[...] = jnp.dot(_silu(c), w_ref[...], preferred_element_type=F32,
                         precision=lax.Precision.HIGHEST) + b_ref[...]


def _ada(c_rows, ada_w, ada_b):
    r = c_rows.shape[0]
    n = ada_w.shape[1]
    tn = 2 * D_MODEL
    return pl.pallas_call(
        _ada_kernel,
        grid=(n // tn,),
        in_specs=[pl.BlockSpec((r, D_MODEL), lambda j: (0, 0)),
                  pl.BlockSpec((D_MODEL, tn), lambda j: (0, j)),
                  pl.BlockSpec((1, tn), lambda j: (0, j))],
        out_specs=pl.BlockSpec((r, tn), lambda j: (0, j)),
        out_shape=jax.ShapeDtypeStruct((r, n), F32),
        compiler_params=_cparams("parallel"),
        name="ada_mod",
    )(c_rows, ada_w, ada_b.reshape(1, n))


def _inproj_kernel(x_ref, mod_ref, g_ref, w_ref, cos_ref, sin_ref, *refs, on_grid):
    n_side = (len(refs) - len(IN_SIZES)) // 2
    side_in = refs[:n_side]
    q_ref, k_ref, v_ref, gsw_ref, up_ref, ga_ref, gb_ref = refs[n_side:n_side + len(IN_SIZES)]
    side_out = refs[n_side + len(IN_SIZES):]
    for src, dst in zip(side_in, side_out):
        dst[...] = src[...].astype(BF16)
    for s in range(x_ref.shape[0] // MXU_DIM):
        rows = slice(s * MXU_DIM, (s + 1) * MXU_DIM)
        h = _rms_mod(x_ref[rows, :], g_ref[...], mod_ref[0, 1:2, :], mod_ref[0, 0:1, :]).astype(BF16)

        def seg(i):
            return _dot(h, w_ref[:, IN_OFFS[i]:IN_OFFS[i + 1]])

        q = seg(0)
        k = seg(1)
        if on_grid:
            cos = jnp.concatenate([cos_ref[rows, :]] * RET_HEADS, axis=1)
            sin = jnp.concatenate([sin_ref[rows, :]] * RET_HEADS, axis=1)
            lane = lax.broadcasted_iota(jnp.int32, q.shape, 1)
            first = (lane & 63) < 32

            def rope(a):
                up = pltpu.roll(a, RET_QK_W - 32, axis=1)
                dn = pltpu.roll(a, 32, axis=1)
                return a * cos + jnp.where(first, up, dn) * sin

            q = rope(q)
            k = rope(k)
        q_ref[rows, :] = q.astype(BF16)
        k_ref[rows, :] = (k * (RET_DK ** -0.5)).astype(BF16)
        v_ref[rows, :] = seg(2).astype(BF16)
        gsw_ref[rows, :] = seg(3).astype(BF16)
        up_ref[rows, :] = seg(4).astype(BF16)
        ga_ref[rows, :] = seg(5).astype(BF16)
        gb_ref[rows, :] = seg(6).astype(BF16)


def _inproj(x, mods, norm_g, w_in, cos_t, sin_t, side_cast=(), *, tokens_per_mod, seq_len, on_grid, tm):
    t = x.shape[0]
    steps = t // tm
    tiles_per_mod = tokens_per_mod // tm
    tiles_per_seq = max(seq_len // tm, 1)
    widths = IN_SIZES
    out_shape = [jax.ShapeDtypeStruct((t, w), BF16) for w in widths]
    out_specs = [pl.BlockSpec((tm, w), lambda i: (i, 0)) for w in widths]
    side_specs = [pl.BlockSpec((a.shape[0] // steps,) + a.shape[1:], lambda i: (i, 0, 0)) for a in side_cast]
    outs = pl.pallas_call(
        functools.partial(_inproj_kernel, on_grid=on_grid),
        grid=(steps,),
        in_specs=[pl.BlockSpec((tm, D_MODEL), lambda i: (i, 0)),
                  pl.BlockSpec((1, 6, D_MODEL), lambda i: (i // tiles_per_mod, 0, 0)),
                  pl.BlockSpec((1, D_MODEL), lambda i: (0, 0)),
                  pl.BlockSpec((D_MODEL, IN_W), lambda i: (0, 0), pipeline_mode=pl.Buffered(1)),
                  pl.BlockSpec((tm, RET_DK), lambda i: (i % tiles_per_seq, 0)),
                  pl.BlockSpec((tm, RET_DK), lambda i: (i % tiles_per_seq, 0))] + side_specs,
        out_specs=out_specs + side_specs,
        out_shape=out_shape + [jax.ShapeDtypeStruct(a.shape, BF16) for a in side_cast],
        compiler_params=_cparams("parallel"),
        name="inproj_grid" if on_grid else "inproj_seq",
    )(x, mods, norm_g, w_in, cos_t, sin_t, *side_cast)
    return outs[:len(widths)], tuple(outs[len(widths):])


def _rope_tables(seq_len):
    t = np.arange(seq_len)
    row = (t // GRID_W).astype(np.float32)
    col = (t % GRID_W).astype(np.float32)
    m = RET_DK // 4
    inv = (np.float32(ROPE_BASE) ** (-np.arange(m, dtype=np.float32) / np.float32(m))).astype(np.float32)
    ar = row[:, None] * inv
    ac = col[:, None] * inv
    cos = np.concatenate([np.cos(ar), np.cos(ar), np.cos(ac), np.cos(ac)], axis=1)
    sin = np.concatenate([-np.sin(ar), np.sin(ar), -np.sin(ac), np.sin(ac)], axis=1)
    return jnp.asarray(cos, F32), jnp.asarray(sin, F32)


def _ret_heads_per_step(seq_len):
    per_head = seq_len * (2 * 2 * (2 * RET_DK + 3 * RET_DV) + 4 * RET_DV + 2 * RET_DK)
    heads = RET_HEADS
    while heads > 1 and heads * per_head > VMEM_LIMIT * 3 // 4:
        heads //= 2
    return heads


def _ret_kernel(dec_ref, q_ref, k_ref, v_ref, g_ref, *refs, n_chunks, heads, zero_init):
    s0_refs = () if zero_init else refs[:2]
    z_ref, sf_ref, sb_ref, oacc_ref, kt_ref = refs[len(s0_refs):]
    c = RET_CHUNK
    half = n_chunks // 2
    ii = lax.broadcasted_iota(I32, (c, c), 0)
    jj = lax.broadcasted_iota(I32, (c, c), 1)
    ik = lax.broadcasted_iota(I32, (c, RET_DK), 0).astype(F32)
    jk = lax.broadcasted_iota(I32, (RET_DK, c), 1).astype(F32)

    def log_gamma(d, shape):
        return jnp.log1p(-jnp.exp2(-jnp.full(shape, d, F32)))

    consts = {}
    for hh in range(heads):
        h = pl.program_id(1) * heads + hh
        dec_f = dec_ref[0, h]
        dec_b = dec_ref[1, h]
        rel = (ii - jj).astype(F32)
        consts[hh, "f"] = (
            jnp.where(rel >= 0, jnp.exp(log_gamma(dec_f, (c, c)) * jnp.maximum(rel, 0.0)), 0.0),
            jnp.exp(log_gamma(dec_f, (c, RET_DK)) * (ik + 1.0)),
            jnp.exp(log_gamma(dec_f, (RET_DK, c)) * (c - 1.0 - jk)),
            jnp.exp(log_gamma(dec_f, (RET_DK, RET_DV)) * c))
        consts[hh, "b"] = (
            jnp.where(rel <= 0, jnp.exp(log_gamma(dec_b, (c, c)) * jnp.maximum(-rel, 0.0)), 0.0),
            jnp.exp(log_gamma(dec_b, (c, RET_DK)) * (c - ik)),
            jnp.exp(log_gamma(dec_b, (RET_DK, c)) * jk),
            jnp.exp(log_gamma(dec_b, (RET_DK, RET_DV)) * c))

    for s_ref, s0_ref in zip((sf_ref, sb_ref), s0_refs or (None, None)):
        s_ref[...] = jnp.zeros(s_ref.shape, F32) if zero_init else s0_ref[...]

    def transpose_keys(ci, carry):
        r = pl.ds(pl.multiple_of(ci * c, c), c)
        for hh in range(heads):
            kt_ref[hh, ci] = k_ref[r, hh * RET_DK:(hh + 1) * RET_DK].T
        return carry

    lax.fori_loop(0, n_chunks, transpose_keys, 0)

    def scores(ci, hh, direction):
        r = pl.ds(pl.multiple_of(ci * c, c), c)
        kcols = slice(hh * RET_DK, (hh + 1) * RET_DK)
        qc = q_ref[r, kcols]
        sc = lax.dot_general(qc, k_ref[r, kcols], (((1,), (1,)), ((), ())), preferred_element_type=F32)
        return ci, hh, direction, r, qc, sc

    def advance(job):
        ci, hh, direction, r, qc, sc = job
        dmask, qdec, kdec, cdec = consts[hh, direction]
        s_ref = sf_ref if direction == "f" else sb_ref
        vc = v_ref[r, hh * RET_DV:(hh + 1) * RET_DV]
        s = s_ref[hh]
        lhs = jnp.concatenate([(sc * dmask).astype(BF16), (qc.astype(F32) * qdec).astype(BF16)], axis=1)
        o = _dot(lhs, jnp.concatenate([vc, s.astype(BF16)], axis=0))
        kd_t = (kt_ref[hh, ci].astype(F32) * kdec).astype(BF16)
        s_ref[hh] = s * cdec + _dot(kd_t, vc)
        return o

    def emit(job, o, second):
        _, hh, _, r, _, _ = job
        vcols = slice(hh * RET_DV, (hh + 1) * RET_DV)
        if not second:
            oacc_ref[hh, r, :] = o
        else:
            o = o + oacc_ref[hh, r, :]
            o = o * lax.rsqrt(jnp.mean(o * o, axis=-1, keepdims=True) + EPS)
            g = g_ref[r, vcols].astype(F32)
            z_ref[r, vcols] = (_silu(g) * o).astype(BF16)

    def body(second):
        def run(t, carry):
            jobs = [scores(ci, hh, d) for hh in range(heads)
                    for ci, d in ((t, "f"), (n_chunks - 1 - t, "b"))]
            outs = [advance(job) for job in jobs]
            for job, o in zip(jobs, outs):
                emit(job, o, second)
            return carry
        return run

    lax.fori_loop(0, half, body(False), 0, unroll=4 if half % 4 == 0 else 1)
    lax.fori_loop(half, n_chunks, body(True), 0, unroll=2 if half % 2 == 0 else 1)


def _retention(q, k, v, gsw, dec, s0, *, batch, seq_len):
    n_chunks = seq_len // RET_CHUNK
    assert n_chunks % 2 == 0
    heads = _ret_heads_per_step(seq_len)
    t = batch * seq_len
    st_spec = pl.BlockSpec((None, heads, RET_DK, RET_DV), lambda b, h: (b, h, 0, 0))
    st_shape = jax.ShapeDtypeStruct((batch, RET_HEADS, RET_DK, RET_DV), F32)
    kspec = pl.BlockSpec((seq_len, heads * RET_DK), lambda b, h: (b, h))
    vspec = pl.BlockSpec((seq_len, heads * RET_DV), lambda b, h: (b, h))
    return pl.pallas_call(
        functools.partial(_ret_kernel, n_chunks=n_chunks, heads=heads, zero_init=s0 is None),
        grid=(batch, RET_HEADS // heads),
        in_specs=[pl.BlockSpec(memory_space=pltpu.SMEM), kspec, kspec, vspec, vspec]
        + ([] if s0 is None else [st_spec, st_spec]),
        out_specs=[vspec, st_spec, st_spec],
        out_shape=[jax.ShapeDtypeStruct((t, RET_V_W), BF16), st_shape, st_shape],
        scratch_shapes=[pltpu.VMEM((heads, seq_len, RET_DV), F32),
                        pltpu.VMEM((heads, n_chunks, RET_DK, RET_CHUNK), BF16)],
        compiler_params=_cparams("parallel", "parallel"),
        name=f"retention_l{seq_len}",
    )(dec, q, k, v, gsw, *(s0 or ()))


def _pool_kernel(u_ref, w_ref, sc_ref, o_ref, *, n_tok, width, two_d):
    n_rows = n_tok // width
    pos = lax.broadcasted_iota(I32, (width, POOL_CH), 0)

    def every_row(a):
        return jnp.concatenate([a] * n_rows, axis=0) if n_rows > 1 else a

    def shift_in_row(a, s):
        ok = (pos < width - s) if s > 0 else (pos >= -s)
        return pltpu.roll(a, (-s) % n_tok, axis=0) * every_row(jnp.where(ok, 1.0, 0.0))

    def shift_rows(a, m):
        k = abs(m) * width
        zeros = jnp.zeros((k, POOL_CH), F32)
        return (jnp.concatenate([a[k:], zeros], axis=0) if m > 0
                else jnp.concatenate([zeros, a[:n_tok - k]], axis=0))

    def box_sum(a, half, shift):
        fw = a
        bw = shift(a, -1)
        m = 1
        while m < half:
            fw = fw + shift(fw, m)
            bw = bw + shift(bw, -m)
            m *= 2
        return fw + bw

    def inv_count(p, half, extent):
        return 1.0 / (jnp.minimum(p + half, extent) - jnp.maximum(p - half, 0)).astype(F32)

    for g, window in enumerate(POOL_WINDOWS):
        half = window // 2
        cols = slice(g * POOL_CH, (g + 1) * POOL_CH)
        ug = u_ref[:, cols].astype(F32)
        total = box_sum(ug, half, shift_in_row)
        inv = every_row(inv_count(pos, half, width))
        if two_d:
            total = box_sum(total, half, shift_rows)
            row = lax.broadcasted_iota(I32, (n_rows, 1, POOL_CH), 0)
            inv_r = jnp.broadcast_to(inv_count(row, half, n_rows), (n_rows, width, POOL_CH))
            inv = inv * inv_r.reshape(n_tok, POOL_CH)
        d = (total * inv - ug).astype(BF16)
        o_ref[:, cols] = (_dot(d, w_ref[g]) * sc_ref[:, cols]).astype(BF16)


def _pool(u, pool_w, pool_scale, *, batch, seq_len, on_grid):
    t = batch * seq_len
    width = GRID_W if on_grid else seq_len
    n_tok = seq_len if on_grid else seq_len * math.gcd(batch, 4)
    return pl.pallas_call(
        functools.partial(_pool_kernel, n_tok=n_tok, width=width, two_d=on_grid),
        grid=(t // n_tok,),
        in_specs=[pl.BlockSpec((n_tok, POOL_W), lambda b: (b, 0)),
                  pl.BlockSpec((POOL_GROUPS, POOL_CH, POOL_CH), lambda b: (0, 0, 0)),
                  pl.BlockSpec((1, POOL_W), lambda b: (0, 0))],
        out_specs=pl.BlockSpec((n_tok, POOL_W), lambda b: (b, 0)),
        out_shape=jax.ShapeDtypeStruct((t, POOL_W), BF16),
        compiler_params=_cparams("parallel"),
        name=f"pool_l{seq_len}",
    )(u, pool_w, pool_scale)


def _pack_rows(x):
    half = D_MODEL // 2
    lo = lax.bitcast_convert_type(x[:, :half].astype(BF16).astype(F32), U32) >> 16
    hi = lax.bitcast_convert_type(x[:, half:].astype(BF16).astype(F32), U32) & jnp.uint32(0xFFFF0000)
    word = lax.bitcast_convert_type(hi | lo, I32)
    return [word[:, c * LANES:(c + 1) * LANES] for c in range(N_PIECES)]


def _unpack_rows(pieces):
    words = [lax.bitcast_convert_type(p, U32) for p in pieces]
    lo = [lax.bitcast_convert_type(w << 16, F32) for w in words]
    hi = [lax.bitcast_convert_type(w & jnp.uint32(0xFFFF0000), F32) for w in words]
    return lo, hi


def _merge_kernel(x_ref, z_ref, p_ref, ga_ref, gb_ref, mod_ref, g2_ref, wr_ref, wp_ref, wo_ref,
                  x1_ref, *piece_refs):
    y_ret = _dot(z_ref[...], wr_ref[...])
    y_pool = _dot(p_ref[...], wp_ref[...])
    merged = (jax.nn.sigmoid(ga_ref[...].astype(F32)) * y_ret
              + jax.nn.sigmoid(gb_ref[...].astype(F32)) * y_pool)
    x1 = x_ref[...] + mod_ref[0, 2:3, :] * _dot(merged.astype(BF16), wo_ref[...])
    x1_ref[...] = x1
    h2 = _rms_mod(x1, g2_ref[...], mod_ref[0, 4:5, :], mod_ref[0, 3:4, :])
    for ref, piece in zip(piece_refs, _pack_rows(h2)):
        ref[...] = piece


def _merge(x, z, p, ga, gb, mods, norm2_g, w_br_ret, w_br_pool, w_out, *, tokens_per_mod, tm):
    t = x.shape[0]
    tiles_per_mod = tokens_per_mod // tm
    row = lambda w: pl.BlockSpec((tm, w), lambda i: (i, 0))
    full = lambda a: pl.BlockSpec(a.shape, lambda i: (0,) * a.ndim)
    outs = pl.pallas_call(
        _merge_kernel,
        grid=(t // tm,),
        in_specs=[row(D_MODEL), row(RET_V_W), row(POOL_W), row(D_MODEL), row(D_MODEL),
                  pl.BlockSpec((1, 6, D_MODEL), lambda i: (i // tiles_per_mod, 0, 0)),
                  full(norm2_g), full(w_br_ret), full(w_br_pool), full(w_out)],
        out_specs=[row(D_MODEL)] + [row(LANES)] * N_PIECES,
        out_shape=[jax.ShapeDtypeStruct((t, D_MODEL), F32)] + [jax.ShapeDtypeStruct((t, LANES), I32)] * N_PIECES,
        compiler_params=_cparams("parallel"),
        name="merge",
    )(x, z, p, ga, gb, mods, norm2_g, w_br_ret, w_br_pool, w_out)
    return outs[0], outs[1:]


def _route_kernel(*refs):
    h_refs = refs[:N_PIECES]
    rw_ref, bias_ref, idx_ref, rank_ref, wk_ref, cnt_ref, carry_ref, before_ref = refs[N_PIECES:]
    e = N_EXPERTS
    tm = h_refs[0].shape[0]
    neg = -jnp.inf

    @pl.when(pl.program_id(0) == 0)
    def _():
        carry_ref[...] = jnp.zeros(carry_ref.shape, F32)
        t_row = lax.broadcasted_iota(I32, (tm, tm), 0)
        t_col = lax.broadcasted_iota(I32, (tm, tm), 1)
        before_ref[...] = (t_row < t_col).astype(BF16)

    lo, hi = _unpack_rows([r[...] for r in h_refs])
    h = jnp.concatenate(lo + hi, axis=1).astype(BF16)
    logits = lax.dot_general(rw_ref[...], h, (((1,), (1,)), ((), ())), preferred_element_type=F32)[:e]
    scores = jax.nn.sigmoid(logits)
    sel = scores + bias_ref[:e, 0:1]
    e_idx = lax.broadcasted_iota(I32, (e, tm), 0)

    grp = sel.reshape(N_EXPERT_GROUPS, GROUP_SIZE, tm)
    m_idx = lax.broadcasted_iota(I32, grp.shape, 1)
    m1 = jnp.max(grp, axis=1, keepdims=True)
    first = jnp.min(jnp.where(grp == m1, m_idx, GROUP_SIZE), axis=1, keepdims=True)
    m2 = jnp.max(jnp.where(m_idx == first, neg, grp), axis=1, keepdims=True)
    gscore = (m1 + m2).reshape(N_EXPERT_GROUPS, tm)

    g_idx = lax.broadcasted_iota(I32, gscore.shape, 0)
    grank = jnp.zeros(gscore.shape, I32)
    for g in range(N_EXPERT_GROUPS):
        other = gscore[g:g + 1, :]
        beats = jnp.where(other > gscore, 1, jnp.where(other == gscore, (g_idx > g).astype(I32), 0))
        grank = grank + beats
    gkeep = (grank < TOPK_GROUPS).astype(F32)
    ekeep = jnp.broadcast_to(gkeep.reshape(N_EXPERT_GROUPS, 1, tm), grp.shape).reshape(e, tm)
    masked = jnp.where(ekeep > 0, sel, neg)

    chosen = jnp.zeros((e, tm), F32)
    picks, hits = [], []
    for _ in range(TOP_K):
        m = jnp.max(masked, axis=0, keepdims=True)
        pick = jnp.min(jnp.where(masked == m, e_idx, e), axis=0, keepdims=True)
        hit = e_idx == pick
        chosen = jnp.where(hit, 1.0, chosen)
        masked = jnp.where(hit, neg, masked)
        picks.append(pick)
        hits.append(hit)

    w = scores * chosen
    comb = w / jnp.sum(w, axis=0, keepdims=True) * ROUTED_SCALE

    rankmat = _dot(chosen.astype(BF16), before_ref[...]) + carry_ref[:e, 0:1]
    carry_ref[:e, :] = carry_ref[:e, :] + jnp.sum(chosen, axis=1, keepdims=True)
    cnt_ref[...] = carry_ref[...]

    idx_ref[...] = jnp.concatenate(picks, axis=0)
    rank_ref[...] = jnp.concatenate(
        [jnp.sum(jnp.where(h, rankmat, 0.0), axis=0, keepdims=True) for h in hits], axis=0).astype(I32)
    wk_ref[...] = jnp.concatenate(
        [jnp.sum(jnp.where(h, comb, 0.0), axis=0, keepdims=True) for h in hits], axis=0)


def _route(h2_pieces, router_wt, bias_col, *, tm):
    t = h2_pieces[0].shape[0]
    krow = pl.BlockSpec((TOP_K, tm), lambda i: (0, i))
    return pl.pallas_call(
        _route_kernel,
        grid=(t // tm,),
        in_specs=[pl.BlockSpec((tm, LANES), lambda i: (i, 0))] * N_PIECES
        + [pl.BlockSpec((LANES, D_MODEL), lambda i: (0, 0)), pl.BlockSpec((LANES, 1), lambda i: (0, 0))],
        out_specs=[krow, krow, krow, pl.BlockSpec((LANES, LANES), lambda i: (0, 0))],
        out_shape=[jax.ShapeDtypeStruct((TOP_K, t), I32), jax.ShapeDtypeStruct((TOP_K, t), I32),
                   jax.ShapeDtypeStruct((TOP_K, t), F32), jax.ShapeDtypeStruct((LANES, LANES), F32)],
        scratch_shapes=[pltpu.VMEM((LANES, LANES), F32), pltpu.VMEM((tm, tm), BF16)],
        compiler_params=_cparams("arbitrary"),
        name="route",
    )(*h2_pieces, router_wt, bias_col)


def _plan_kernel(idx_ref, rank_ref, cnt_ref, pos_ref, te_ref, nv_ref, nu_ref, *, group_tile):
    tf = idx_ref.shape[1]
    nt = te_ref.shape[1]
    cnt = cnt_ref[...].astype(I32)
    padded = (((cnt + (group_tile - 1)) // group_tile) * group_tile).astype(F32)
    e_sub = lax.broadcasted_iota(I32, (LANES, LANES), 0)
    e_lane = lax.broadcasted_iota(I32, (LANES, LANES), 1)
    base = jnp.sum(jnp.where(e_lane < e_sub, padded.T, 0.0), axis=1, keepdims=True)
    end = base + padded[:, 0:1]

    idx = idx_ref[...]
    start = jnp.zeros(idx.shape, F32)
    for e in range(N_EXPERTS):
        start = jnp.where(idx == e, base[e:e + 1, 0:1], start)
    pos = start.astype(I32) + rank_ref[...]
    for j in range(tf // SC_CHUNK):
        pos_ref[j] = pos[:, j * SC_CHUNK:(j + 1) * SC_CHUNK]

    tile_start = (lax.broadcasted_iota(I32, (N_EXPERTS, nt), 1) * group_tile).astype(F32)
    done = jnp.sum(jnp.where(end[:N_EXPERTS] <= tile_start, 1.0, 0.0), axis=0, keepdims=True)
    te_ref[...] = jnp.minimum(done, N_EXPERTS - 1.0).astype(I32)
    in_group = (base[:N_EXPERTS] <= tile_start) & (tile_start < end[:N_EXPERTS])
    real = jnp.clip(base[:N_EXPERTS] + cnt[:N_EXPERTS, 0:1].astype(F32) - tile_start, 0.0, float(group_tile))
    nv_ref[...] = jnp.sum(jnp.where(in_group, real, 0.0), axis=0, keepdims=True).astype(I32)
    total = jnp.sum(padded[:, 0:1], axis=0, keepdims=True)
    nu_ref[...] = jnp.broadcast_to(total * (1.0 / group_tile), nu_ref.shape).astype(I32)


def _plan(idx, rank, counts, *, n_tiles, tf, group_tile):
    t = idx.shape[1]
    nt_pad = -(-n_tiles // LANES) * LANES
    krow = pl.BlockSpec((TOP_K, tf), lambda i: (0, i))
    return pl.pallas_call(
        functools.partial(_plan_kernel, group_tile=group_tile),
        grid=(t // tf,),
        in_specs=[krow, krow, pl.BlockSpec((LANES, LANES), lambda i: (0, 0))],
        out_specs=[pl.BlockSpec((tf // SC_CHUNK, TOP_K, SC_CHUNK), lambda i: (i, 0, 0)),
                   pl.BlockSpec((1, nt_pad), lambda i: (0, 0)),
                   pl.BlockSpec((1, nt_pad), lambda i: (0, 0)),
                   pl.BlockSpec((1, LANES), lambda i: (0, 0))],
        out_shape=[jax.ShapeDtypeStruct((t // SC_CHUNK, TOP_K, SC_CHUNK), I32),
                   jax.ShapeDtypeStruct((1, nt_pad), I32), jax.ShapeDtypeStruct((1, nt_pad), I32),
                   jax.ShapeDtypeStruct((1, LANES), I32)],
        compiler_params=_cparams("arbitrary"),
        name="moe_plan",
    )(idx, rank, counts)


def _sc_mesh_info():
    info = plsc.get_sparse_core_info()
    mesh = plsc.VectorSubcoreMesh(core_axis_name="c", subcore_axis_name="s")
    return mesh, info.num_cores, info.num_cores * info.num_subcores


def _sc_dispatch(pieces, pos, *, n_rows):
    t = pieces[0].shape[0]
    mesh, n_cores, n_workers = _sc_mesh_info()
    per_w = t // SC_CHUNK // n_workers

    @functools.partial(
        pl.kernel, mesh=mesh,
        out_type=[jax.ShapeDtypeStruct((n_rows, LANES), I32)] * N_PIECES,
        scratch_types=[pltpu.VMEM((TOP_K, SC_CHUNK), I32),
                       pltpu.VMEM((N_PIECES, SC_CHUNK, LANES), I32),
                       pltpu.SemaphoreType.DMA((N_PIECES,)),
                       pltpu.SemaphoreType.DMA],
        name="sc_dispatch",
    )
    def run(*refs):
        src = refs[:N_PIECES]
        pos_hbm = refs[N_PIECES]
        dst = refs[N_PIECES + 1:2 * N_PIECES + 1]
        idx_v, rows_v, load_sem, put_sem = refs[2 * N_PIECES + 1:]
        wid = lax.axis_index("s") * n_cores + lax.axis_index("c")

        @pl.loop(0, per_w)
        def _(j):
            ch = wid * per_w + j
            t0 = pl.multiple_of(ch * SC_CHUNK, SC_CHUNK)
            loads = [pltpu.make_async_copy(src[c].at[pl.ds(t0, SC_CHUNK)], rows_v.at[c], load_sem.at[c])
                     for c in range(N_PIECES)]
            for ld in loads:
                ld.start()
            pltpu.sync_copy(pos_hbm.at[ch], idx_v)
            puts = []
            for c in range(N_PIECES):
                loads[c].wait()
                for k in range(TOP_K):
                    puts.append(pltpu.make_async_copy(rows_v.at[c], dst[c].at[idx_v.at[k]], put_sem))
                    puts[-1].start()
            for cp in puts:
                cp.wait()

    return run(*pieces, pos)


SC_GROUP = 32


def _sc_combine(pieces, pos, wts, *, n_tokens):
    mesh, n_cores, n_workers = _sc_mesh_info()
    lanes = plsc.get_sparse_core_info().num_lanes
    per_w = n_tokens // SC_GROUP // n_workers

    @functools.partial(
        pl.kernel, mesh=mesh,
        out_type=[jax.ShapeDtypeStruct((n_tokens, LANES), F32)] * (2 * N_PIECES),
        scratch_types=[pltpu.VMEM((TOP_K, SC_GROUP), I32),
                       pltpu.VMEM((TOP_K, SC_GROUP), F32),
                       pltpu.VMEM((2, TOP_K, SC_GROUP, LANES), I32),
                       pltpu.VMEM((2, 2, SC_GROUP, LANES), F32),
                       pltpu.SemaphoreType.DMA((2,)),
                       pltpu.SemaphoreType.DMA((2,))],
        compiler_params=pltpu.CompilerParams(needs_layout_passes=False),
        name="sc_combine",
    )
    def run(*refs):
        src = refs[:N_PIECES]
        pos_hbm, wts_hbm = refs[N_PIECES:N_PIECES + 2]
        dst = refs[N_PIECES + 2:3 * N_PIECES + 2]
        idx_v, w_v, buf, acc, get_sem, put_sem = refs[3 * N_PIECES + 2:]
        wid = lax.axis_index("s") * n_cores + lax.axis_index("c")

        @pl.loop(0, per_w)
        def _(j):
            grp = wid * per_w + j
            t0 = pl.multiple_of(grp * SC_GROUP, SC_GROUP)
            pltpu.sync_copy(pos_hbm.at[grp], idx_v)
            pltpu.sync_copy(wts_hbm.at[grp], w_v)

            def gets(c, slot):
                return [pltpu.make_async_copy(src[c].at[idx_v.at[k]], buf.at[slot, k], get_sem.at[slot])
                        for k in range(TOP_K)]

            def puts(c, slot):
                r = pl.ds(t0, SC_GROUP)
                return [pltpu.make_async_copy(acc.at[slot, 0], dst[c].at[r], put_sem.at[slot]),
                        pltpu.make_async_copy(acc.at[slot, 1], dst[N_PIECES + c].at[r], put_sem.at[slot])]

            for cp in gets(0, 0):
                cp.start()
            for c in range(N_PIECES):
                slot = c % 2
                if c + 1 < N_PIECES:
                    for cp in gets(c + 1, 1 - slot):
                        cp.start()
                for cp in gets(c, slot):
                    cp.wait()
                if c >= 2:
                    for cp in puts(c - 2, slot):
                        cp.wait()

                @pl.loop(0, SC_GROUP)
                def _(r):
                    row = jnp.full((lanes,), r, I32)
                    w = [plsc.load_gather(w_v, [jnp.full((lanes,), k, I32), row]) for k in range(TOP_K)]
                    for q in range(LANES // lanes):
                        cols = pl.ds(q * lanes, lanes)
                        lo = hi = None
                        for k in range(TOP_K):
                            word = buf[slot, k, r, cols]
                            lo_k = plsc.bitcast(word << 16, F32) * w[k]
                            hi_k = plsc.bitcast(word & jnp.int32(-65536), F32) * w[k]
                            lo = lo_k if lo is None else lo + lo_k
                            hi = hi_k if hi is None else hi + hi_k
                        acc[slot, 0, r, cols] = lo
                        acc[slot, 1, r, cols] = hi

                for cp in puts(c, slot):
                    cp.start()
            for c in range(N_PIECES - 2, N_PIECES):
                for cp in puts(c, c % 2):
                    cp.wait()

    return run(*pieces, pos, wts)


def _group_tile(n_tokens):
    per_expert = n_tokens * TOP_K // N_EXPERTS
    return max(MXU_DIM, min(4 * MXU_DIM, per_expert // MXU_DIM * MXU_DIM))


SECOND_DMA_QUEUE = 1


def _experts_kernel(te_ref, nv_ref, nu_ref, *refs, tile):
    x_hbm = refs[:N_PIECES]
    wg_ref, wu_ref, wd_ref = refs[N_PIECES:N_PIECES + 3]
    y_hbm = refs[N_PIECES + 3:2 * N_PIECES + 3]
    xbuf, xsem, ybuf, ysem = refs[2 * N_PIECES + 3:]
    i = pl.program_id(0)
    n_used = nu_ref[0]
    slot = lax.rem(i, 2)

    half = tile // 2

    def x_copy(step, into, h, c):
        r = pl.ds(pl.multiple_of(step * tile + h * half, half), half)
        return pltpu.make_async_copy(x_hbm[c].at[r], xbuf.at[into, c, pl.ds(h * half, half)], xsem.at[into, c])

    def y_copy(step, out_of, h, c):
        r = pl.ds(pl.multiple_of(step * tile + h * half, half), half)
        return pltpu.make_async_copy(ybuf.at[out_of, c, pl.ds(h * half, half)], y_hbm[c].at[r], ysem.at[out_of, c])

    def real_halves(copy, step, buf, act):
        for c in range(N_PIECES):
            act(copy(step, buf, 0, c))

        @pl.when(nv_ref[step] > half)
        def _():
            for c in range(N_PIECES):
                act(copy(step, buf, 1, c))

    start = lambda cp: cp.start(priority=SECOND_DMA_QUEUE)
    wait = lambda cp: cp.wait()

    @pl.when(i == 0)
    def _():
        real_halves(x_copy, 0, 0, start)

    @pl.when(i + 1 < n_used)
    def _():
        real_halves(x_copy, i + 1, 1 - slot, start)

    @pl.when(i < n_used)
    def _():
        real_halves(x_copy, i, slot, wait)
        subs = [slice(s * MXU_DIM, (s + 1) * MXU_DIM) for s in range(tile // MXU_DIM)]
        xs = []
        for rows in subs:
            lo, hi = _unpack_rows([xbuf[slot, c, rows, :] for c in range(N_PIECES)])
            xs.append(jnp.concatenate(lo + hi, axis=1).astype(BF16))
        gates = [(_dot(x, wg_ref[...]), _dot(x, wu_ref[...])) for x in xs]
        ys = [_dot((_silu(g) * u).astype(BF16), wd_ref[...]) for g, u in gates]
        for rows, y in zip(subs, ys):
            for c, piece in enumerate(_pack_rows(y)):
                ybuf[slot, c, rows, :] = piece

        @pl.when(i >= 1)
        def _():
            real_halves(y_copy, i - 1, 1 - slot, wait)

        real_halves(y_copy, i, slot, start)

        @pl.when(i == n_used - 1)
        def _():
            real_halves(y_copy, i, slot, wait)


def _experts(x_pieces, tile_expert, tile_rows, n_used, wg, wu, wd, *, group_tile):
    n_rows = x_pieces[0].shape[0]
    n_tiles = n_rows // group_tile

    wspec = lambda a: pl.BlockSpec((None,) + a.shape[1:],
                                   lambda i, te, nv, nu: (te[jnp.minimum(i, nu[0] - 1)], 0, 0))
    return pl.pallas_call(
        functools.partial(_experts_kernel, tile=group_tile),
        grid_spec=pltpu.PrefetchScalarGridSpec(
            num_scalar_prefetch=3,
            grid=(n_tiles,),
            in_specs=[pl.BlockSpec(memory_space=pl.ANY)] * N_PIECES + [wspec(wg), wspec(wu), wspec(wd)],
            out_specs=[pl.BlockSpec(memory_space=pl.ANY)] * N_PIECES,
            scratch_shapes=[pltpu.VMEM((2, N_PIECES, group_tile, LANES), I32),
                            pltpu.SemaphoreType.DMA((2, N_PIECES))] * 2),
        out_shape=[jax.ShapeDtypeStruct((n_rows, LANES), I32)] * N_PIECES,
        compiler_params=_cparams("arbitrary"),
        name="experts",
    )(tile_expert, tile_rows, n_used, *x_pieces, wg, wu, wd)


def _moe_out_kernel(x1_ref, mod_ref, fg_ref, sg_ref, su_ref, sd_ref, *refs):
    h_refs = refs[:N_PIECES]
    routed_refs = refs[N_PIECES:3 * N_PIECES]
    out_ref = refs[3 * N_PIECES]
    h_lo, h_hi = _unpack_rows([r[...] for r in h_refs])
    h = jnp.concatenate(h_lo + h_hi, axis=1).astype(BF16)
    hid = _silu(_dot(h, sg_ref[...])) * _dot(h, su_ref[...])
    shared = _dot(hid.astype(BF16), sd_ref[...])
    routed = jnp.concatenate([r[...] for r in routed_refs], axis=1)
    x2 = x1_ref[...] + mod_ref[0, 5:6, :] * (shared + routed)
    out_ref[...] = x2 * lax.rsqrt(jnp.mean(x2 * x2, axis=-1, keepdims=True) + EPS) * fg_ref[...]


def _moe_out(h2_pieces, x1, mods, final_g, sg, su, sd, routed_pieces, *, tokens_per_mod, tm):
    t = x1.shape[0]
    tiles_per_mod = tokens_per_mod // tm
    row = lambda w: pl.BlockSpec((tm, w), lambda i: (i, 0))
    full = lambda a: pl.BlockSpec(a.shape, lambda i: (0,) * a.ndim)
    return pl.pallas_call(
        _moe_out_kernel,
        grid=(t // tm,),
        in_specs=[row(D_MODEL),
                  pl.BlockSpec((1, 6, D_MODEL), lambda i: (i // tiles_per_mod, 0, 0)),
                  full(final_g), full(sg), full(su), full(sd)]
        + [row(LANES)] * (3 * N_PIECES),
        out_specs=row(D_MODEL),
        out_shape=jax.ShapeDtypeStruct((t, D_MODEL), F32),
        compiler_params=_cparams("parallel"),
        name="moe_out",
    )(x1, mods, final_g, sg, su, sd, *h2_pieces, *routed_pieces)


def _trunk(x, mods, s0, w, expert_w, *, batch, seq_len, on_grid):
    t = batch * seq_len
    tokens_per_mod = t // mods.shape[0]
    tm_in = 512
    cos_t, sin_t = _rope_tables(max(seq_len, tm_in))
    to_cast = expert_w if expert_w[0].dtype != BF16 else ()
    (q, k, v, gsw, up, ga, gb), casted = _inproj(x, mods, w["norm1_g"], w["w_in"], cos_t, sin_t, to_cast,
                                                 tokens_per_mod=tokens_per_mod, seq_len=seq_len,
                                                 on_grid=on_grid, tm=tm_in)
    expert_w = casted or expert_w
    z, s_f, s_b = _retention(q, k, v, gsw, w["dec"], s0, batch=batch, seq_len=seq_len)
    p = _pool(up, w["pool_w"], w["pool_scale"], batch=batch, seq_len=seq_len, on_grid=on_grid)
    x1, h2_pieces = _merge(x, z, p, ga, gb, mods, w["norm2_g"], w["w_br_ret"], w["w_br_pool"],
                               w["w_out"], tokens_per_mod=tokens_per_mod, tm=512)

    group_tile = _group_tile(t)
    n_rows = t * TOP_K + N_EXPERTS * group_tile
    idx, rank, wts, counts = _route(h2_pieces, w["router_wt"], w["router_bias"], tm=1024)
    pos, tile_expert, tile_rows, n_used = _plan(idx, rank, counts, n_tiles=n_rows // group_tile, tf=2048,
                                                group_tile=group_tile)
    x_sorted = _sc_dispatch(h2_pieces, pos, n_rows=n_rows)
    y_sorted = _experts(x_sorted, tile_expert.reshape(-1), tile_rows.reshape(-1), n_used.reshape(-1),
                        *expert_w, group_tile=group_tile)
    regroup = lambda a: a.reshape(TOP_K, t // SC_GROUP, SC_GROUP).transpose(1, 0, 2)
    pos_rows = pos.transpose(1, 0, 2).reshape(TOP_K, t)
    routed = _sc_combine(y_sorted, regroup(pos_rows), regroup(wts), n_tokens=t)
    y = _moe_out(h2_pieces, x1, mods, w["final_g"], w["sh_w_gate"], w["sh_w_up"], w["sh_w_down"], routed,
                 tokens_per_mod=tokens_per_mod, tm=512)
    return y, s_f, s_b, expert_w


def kernel(x_prompt, x_sample, state_ret_fwd, state_ret_bwd, c, c_ctx, ada_w, ada_b, norm1_g, norm2_g, w_in,
           ret_decay_fwd, ret_decay_bwd, w_br_ret, pool_w, pool_scale, w_br_pool, w_out, router_w, router_bias,
           exp_w_gate, exp_w_up, exp_w_down, sh_w_gate, sh_w_up, sh_w_down, final_norm_g):
    n_req, seq, d = x_prompt.shape
    n_dec, dec_seq, _ = x_sample.shape
    depth = ada_w.shape[0]
    assert depth == 1 and d == D_MODEL

    xc = x_prompt.reshape(n_req * seq, d)
    xs = x_sample.reshape(n_dec * dec_seq, d)
    new_f, new_b = [], []
    for l in range(depth):
        c_rows = jnp.concatenate([c_ctx[None, :], c, jnp.zeros((8 - 1 - n_dec, d), F32)], axis=0)
        mods = _ada(c_rows, ada_w[l], ada_b[l]).reshape(8, 6, d)
        pad_rows = LANES - N_EXPERTS
        w = dict(
            norm1_g=norm1_g[l].reshape(1, d), norm2_g=norm2_g[l].reshape(1, d),
            final_g=final_norm_g.reshape(1, d),
            w_in=w_in[l].astype(BF16),
            dec=jnp.stack([ret_decay_fwd[l], ret_decay_bwd[l]]).astype(F32),
            w_br_ret=w_br_ret[l].astype(BF16), pool_w=pool_w[l].astype(BF16),
            pool_scale=pool_scale[l].reshape(1, POOL_W), w_br_pool=w_br_pool[l].astype(BF16),
            w_out=w_out[l].astype(BF16),
            router_wt=jnp.pad(router_w[l].T, ((0, pad_rows), (0, 0))).astype(BF16),
            router_bias=jnp.pad(router_bias[l].astype(F32).reshape(N_EXPERTS, 1), ((0, pad_rows), (0, 0))),
            sh_w_gate=sh_w_gate[l].astype(BF16),
            sh_w_up=sh_w_up[l].astype(BF16), sh_w_down=sh_w_down[l].astype(BF16),
        )
        cached = (state_ret_fwd[:, l].astype(F32), state_ret_bwd[:, l].astype(F32))
        expert_w = (exp_w_gate[l], exp_w_up[l], exp_w_down[l])
        xs, _, _, expert_w = _trunk(xs, mods[1:1 + n_dec], cached, w, expert_w,
                                    batch=n_dec, seq_len=dec_seq, on_grid=True)
        xc, s_f, s_b, _ = _trunk(xc, mods[0:1], None, w, expert_w, batch=n_req, seq_len=seq, on_grid=False)
        new_f.append(s_f)
        new_b.append(s_b)
    y_prompt = xc.reshape(n_req, seq, d)
    y_sample = xs.reshape(n_dec, dec_seq, d)
    return (y_prompt, y_sample, jnp.stack(new_f, axis=1).astype(x_prompt.dtype),
            jnp.stack(new_b, axis=1).astype(x_prompt.dtype))
```

```python
import functools
import math

import numpy as np
import jax
import jax.numpy as jnp
from jax import lax
from jax.experimental import pallas as pl
from jax.experimental.pallas import tpu as pltpu
from jax.experimental.pallas import tpu_sc as plsc

D_MODEL = 1024
GRID_W = 64
RET_HEADS = 4
RET_DK = 128
RET_DV = 256
RET_QK_W = RET_HEADS * RET_DK
RET_V_W = RET_HEADS * RET_DV
RET_CHUNK = 128
ROPE_BASE = 10000.0
POOL_GROUPS = 4
POOL_CH = 128
POOL_W = POOL_GROUPS * POOL_CH
POOL_WINDOWS = (2, 4, 8, 16)
N_EXPERTS = 64
TOP_K = 8
N_EXPERT_GROUPS = 8
GROUP_SIZE = N_EXPERTS // N_EXPERT_GROUPS
TOPK_GROUPS = 4
D_EXPERT = 256
ROUTED_SCALE = 2.5
EPS = 1e-6
IN_SIZES = (RET_QK_W, RET_QK_W, RET_V_W, RET_V_W, POOL_W, D_MODEL, D_MODEL)
IN_OFFS = tuple(sum(IN_SIZES[:i]) for i in range(len(IN_SIZES) + 1))
IN_W = IN_OFFS[-1]

LANES = 128
VMEM_LIMIT = 56 << 20
N_PIECES = D_MODEL // 2 // LANES
MXU_DIM = 256
SC_CHUNK = 128

F32 = jnp.float32
BF16 = jnp.bfloat16
I32 = jnp.int32
U32 = jnp.uint32


def _cparams(*sem):
    return pltpu.CompilerParams(dimension_semantics=sem, vmem_limit_bytes=VMEM_LIMIT)


def _dot(a, b):
    return jnp.dot(a, b, preferred_element_type=F32)


def _silu(x):
    return x * jax.nn.sigmoid(x)


def _rms_mod(x, g, scale, shift):
    y = x * lax.rsqrt(jnp.mean(x * x, axis=-1, keepdims=True) + EPS)
    return (y * g) * (1.0 + scale) + shift


def _ada_kernel(c_ref, w_ref, b_ref, o_ref):
    c = c_ref[...]
    o_ref[...] = jnp.dot(_silu(c), w_ref[...], preferred_element_type=F32,
                         precision=lax.Precision.HIGHEST) + b_ref[...]


def _ada(c_rows, ada_w, ada_b):
    r = c_rows.shape[0]
    n = ada_w.shape[1]
    tn = 2 * D_MODEL
    return pl.pallas_call(
        _ada_kernel,
        grid=(n // tn,),
        in_specs=[pl.BlockSpec((r, D_MODEL), lambda j: (0, 0)),
                  pl.BlockSpec((D_MODEL, tn), lambda j: (0, j)),
                  pl.BlockSpec((1, tn), lambda j: (0, j))],
        out_specs=pl.BlockSpec((r, tn), lambda j: (0, j)),
        out_shape=jax.ShapeDtypeStruct((r, n), F32),
        compiler_params=_cparams("parallel"),
        name="ada_mod",
    )(c_rows, ada_w, ada_b.reshape(1, n))


def _inproj_kernel(x_ref, mod_ref, g_ref, w_ref, cos_ref, sin_ref, *refs, on_grid):
    n_side = (len(refs) - len(IN_SIZES)) // 2
    side_in = refs[:n_side]
    q_ref, k_ref, v_ref, gsw_ref, up_ref, ga_ref, gb_ref = refs[n_side:n_side + len(IN_SIZES)]
    side_out = refs[n_side + len(IN_SIZES):]
    for src, dst in zip(side_in, side_out):
        dst[...] = src[...].astype(BF16)
    for s in range(x_ref.shape[0] // MXU_DIM):
        rows = slice(s * MXU_DIM, (s + 1) * MXU_DIM)
        h = _rms_mod(x_ref[rows, :], g_ref[...], mod_ref[0, 1:2, :], mod_ref[0, 0:1, :]).astype(BF16)

        def seg(i):
            return _dot(h, w_ref[:, IN_OFFS[i]:IN_OFFS[i + 1]])

        q = seg(0)
        k = seg(1)
        if on_grid:
            cos = jnp.concatenate([cos_ref[rows, :]] * RET_HEADS, axis=1)
            sin = jnp.concatenate([sin_ref[rows, :]] * RET_HEADS, axis=1)
            lane = lax.broadcasted_iota(jnp.int32, q.shape, 1)
            first = (lane & 63) < 32

            def rope(a):
                up = pltpu.roll(a, RET_QK_W - 32, axis=1)
                dn = pltpu.roll(a, 32, axis=1)
                return a * cos + jnp.where(first, up, dn) * sin

            q = rope(q)
            k = rope(k)
        q_ref[rows, :] = q.astype(BF16)
        k_ref[rows, :] = (k * (RET_DK ** -0.5)).astype(BF16)
        v_ref[rows, :] = seg(2).astype(BF16)
        gsw_ref[rows, :] = seg(3).astype(BF16)
        up_ref[rows, :] = seg(4).astype(BF16)
        ga_ref[rows, :] = seg(5).astype(BF16)
        gb_ref[rows, :] = seg(6).astype(BF16)


def _inproj(x, mods, norm_g, w_in, cos_t, sin_t, side_cast=(), *, tokens_per_mod, seq_len, on_grid, tm):
    t = x.shape[0]
    steps = t // tm
    tiles_per_mod = tokens_per_mod // tm
    tiles_per_seq = max(seq_len // tm, 1)
    widths = IN_SIZES
    out_shape = [jax.ShapeDtypeStruct((t, w), BF16) for w in widths]
    out_specs = [pl.BlockSpec((tm, w), lambda i: (i, 0)) for w in widths]
    side_specs = [pl.BlockSpec((a.shape[0] // steps,) + a.shape[1:], lambda i: (i, 0, 0)) for a in side_cast]
    outs = pl.pallas_call(
        functools.partial(_inproj_kernel, on_grid=on_grid),
        grid=(steps,),
        in_specs=[pl.BlockSpec((tm, D_MODEL), lambda i: (i, 0)),
                  pl.BlockSpec((1, 6, D_MODEL), lambda i: (i // tiles_per_mod, 0, 0)),
                  pl.BlockSpec((1, D_MODEL), lambda i: (0, 0)),
                  pl.BlockSpec((D_MODEL, IN_W), lambda i: (0, 0), pipeline_mode=pl.Buffered(1)),
                  pl.BlockSpec((tm, RET_DK), lambda i: (i % tiles_per_seq, 0)),
                  pl.BlockSpec((tm, RET_DK), lambda i: (i % tiles_per_seq, 0))] + side_specs,
        out_specs=out_specs + side_specs,
        out_shape=out_shape + [jax.ShapeDtypeStruct(a.shape, BF16) for a in side_cast],
        compiler_params=_cparams("parallel"),
        name="inproj_grid" if on_grid else "inproj_seq",
    )(x, mods, norm_g, w_in, cos_t, sin_t, *side_cast)
    return outs[:len(widths)], tuple(outs[len(widths):])


def _rope_tables(seq_len):
    t = np.arange(seq_len)
    row = (t // GRID_W).astype(np.float32)
    col = (t % GRID_W).astype(np.float32)
    m = RET_DK // 4
    inv = (np.float32(ROPE_BASE) ** (-np.arange(m, dtype=np.float32) / np.float32(m))).astype(np.float32)
    ar = row[:, None] * inv
    ac = col[:, None] * inv
    cos = np.concatenate([np.cos(ar), np.cos(ar), np.cos(ac), np.cos(ac)], axis=1)
    sin = np.concatenate([-np.sin(ar), np.sin(ar), -np.sin(ac), np.sin(ac)], axis=1)
    return jnp.asarray(cos, F32), jnp.asarray(sin, F32)


def _ret_heads_per_step(seq_len):
    per_head = seq_len * (2 * 2 * (2 * RET_DK + 3 * RET_DV) + 4 * RET_DV + 2 * RET_DK)
    heads = RET_HEADS
    while heads > 1 and heads * per_head > VMEM_LIMIT * 3 // 4:
        heads //= 2
    return heads


def _ret_kernel(dec_ref, q_ref, k_ref, v_ref, g_ref, *refs, n_chunks, heads, zero_init):
    s0_refs = () if zero_init else refs[:2]
    z_ref, sf_ref, sb_ref, oacc_ref, kt_ref = refs[len(s0_refs):]
    c = RET_CHUNK
    half = n_chunks // 2
    ii = lax.broadcasted_iota(I32, (c, c), 0)
    jj = lax.broadcasted_iota(I32, (c, c), 1)
    ik = lax.broadcasted_iota(I32, (c, RET_DK), 0).astype(F32)
    jk = lax.broadcasted_iota(I32, (RET_DK, c), 1).astype(F32)

    def log_gamma(d, shape):
        return jnp.log1p(-jnp.exp2(-jnp.full(shape, d, F32)))

    consts = {}
    for hh in range(heads):
        h = pl.program_id(1) * heads + hh
        dec_f = dec_ref[0, h]
        dec_b = dec_ref[1, h]
        rel = (ii - jj).astype(F32)
        consts[hh, "f"] = (
            jnp.where(rel >= 0, jnp.exp(log_gamma(dec_f, (c, c)) * jnp.maximum(rel, 0.0)), 0.0),
            jnp.exp(log_gamma(dec_f, (c, RET_DK)) * (ik + 1.0)),
            jnp.exp(log_gamma(dec_f, (RET_DK, c)) * (c - 1.0 - jk)),
            jnp.exp(log_gamma(dec_f, (RET_DK, RET_DV)) * c))
        consts[hh, "b"] = (
            jnp.where(rel <= 0, jnp.exp(log_gamma(dec_b, (c, c)) * jnp.maximum(-rel, 0.0)), 0.0),
            jnp.exp(log_gamma(dec_b, (c, RET_DK)) * (c - ik)),
            jnp.exp(log_gamma(dec_b, (RET_DK, c)) * jk),
            jnp.exp(log_gamma(dec_b, (RET_DK, RET_DV)) * c))

    for s_ref, s0_ref in zip((sf_ref, sb_ref), s0_refs or (None, None)):
        s_ref[...] = jnp.zeros(s_ref.shape, F32) if zero_init else s0_ref[...]

    def transpose_keys(ci, carry):
        r = pl.ds(pl.multiple_of(ci * c, c), c)
        for hh in range(heads):
            kt_ref[hh, ci] = k_ref[r, hh * RET_DK:(hh + 1) * RET_DK].T
        return carry

    lax.fori_loop(0, n_chunks, transpose_keys, 0)

    def scores(ci, hh, direction):
        r = pl.ds(pl.multiple_of(ci * c, c), c)
        kcols = slice(hh * RET_DK, (hh + 1) * RET_DK)
        qc = q_ref[r, kcols]
        sc = lax.dot_general(qc, k_ref[r, kcols], (((1,), (1,)), ((), ())), preferred_element_type=F32)
        return ci, hh, direction, r, qc, sc

    def advance(job):
        ci, hh, direction, r, qc, sc = job
        dmask, qdec, kdec, cdec = consts[hh, direction]
        s_ref = sf_ref if direction == "f" else sb_ref
        vc = v_ref[r, hh * RET_DV:(hh + 1) * RET_DV]
        s = s_ref[hh]
        lhs = jnp.concatenate([(sc * dmask).astype(BF16), (qc.astype(F32) * qdec).astype(BF16)], axis=1)
        o = _dot(lhs, jnp.concatenate([vc, s.astype(BF16)], axis=0))
        kd_t = (kt_ref[hh, ci].astype(F32) * kdec).astype(BF16)
        s_ref[hh] = s * cdec + _dot(kd_t, vc)
        return o

    def emit(job, o, second):
        _, hh, _, r, _, _ = job
        vcols = slice(hh * RET_DV, (hh + 1) * RET_DV)
        if not second:
            oacc_ref[hh, r, :] = o
        else:
            o = o + oacc_ref[hh, r, :]
            o = o * lax.rsqrt(jnp.mean(o * o, axis=-1, keepdims=True) + EPS)
            g = g_ref[r, vcols].astype(F32)
            z_ref[r, vcols] = (_silu(g) * o).astype(BF16)

    def body(second):
        def run(t, carry):
            jobs = [scores(ci, hh, d) for hh in range(heads)
                    for ci, d in ((t, "f"), (n_chunks - 1 - t, "b"))]
            outs = [advance(job) for job in jobs]
            for job, o in zip(jobs, outs):
                emit(job, o, second)
            return carry
        return run

    lax.fori_loop(0, half, body(False), 0, unroll=4 if half % 4 == 0 else 1)
    lax.fori_loop(half, n_chunks, body(True), 0, unroll=2 if half % 2 == 0 else 1)


def _retention(q, k, v, gsw, dec, s0, *, batch, seq_len):
    n_chunks = seq_len // RET_CHUNK
    assert n_chunks % 2 == 0
    heads = _ret_heads_per_step(seq_len)
    t = batch * seq_len
    st_spec = pl.BlockSpec((None, heads, RET_DK, RET_DV), lambda b, h: (b, h, 0, 0))
    st_shape = jax.ShapeDtypeStruct((batch, RET_HEADS, RET_DK, RET_DV), F32)
    kspec = pl.BlockSpec((seq_len, heads * RET_DK), lambda b, h: (b, h))
    vspec = pl.BlockSpec((seq_len, heads * RET_DV), lambda b, h: (b, h))
    return pl.pallas_call(
        functools.partial(_ret_kernel, n_chunks=n_chunks, heads=heads, zero_init=s0 is None),
        grid=(batch, RET_HEADS // heads),
        in_specs=[pl.BlockSpec(memory_space=pltpu.SMEM), kspec, kspec, vspec, vspec]
        + ([] if s0 is None else [st_spec, st_spec]),
        out_specs=[vspec, st_spec, st_spec],
        out_shape=[jax.ShapeDtypeStruct((t, RET_V_W), BF16), st_shape, st_shape],
        scratch_shapes=[pltpu.VMEM((heads, seq_len, RET_DV), F32),
                        pltpu.VMEM((heads, n_chunks, RET_DK, RET_CHUNK), BF16)],
        compiler_params=_cparams("parallel", "parallel"),
        name=f"retention_l{seq_len}",
    )(dec, q, k, v, gsw, *(s0 or ()))


def _pool_kernel(u_ref, w_ref, sc_ref, o_ref, *, n_tok, width, two_d):
    n_rows = n_tok // width
    pos = lax.broadcasted_iota(I32, (width, POOL_CH), 0)

    def every_row(a):
        return jnp.concatenate([a] * n_rows, axis=0) if n_rows > 1 else a

    def shift_in_row(a, s):
        ok = (pos < width - s) if s > 0 else (pos >= -s)
        return pltpu.roll(a, (-s) % n_tok, axis=0) * every_row(jnp.where(ok, 1.0, 0.0))

    def shift_rows(a, m):
        k = abs(m) * width
        zeros = jnp.zeros((k, POOL_CH), F32)
        return (jnp.concatenate([a[k:], zeros], axis=0) if m > 0
                else jnp.concatenate([zeros, a[:n_tok - k]], axis=0))

    def box_sum(a, half, shift):
        fw = a
        bw = shift(a, -1)
        m = 1
        while m < half:
            fw = fw + shift(fw, m)
            bw = bw + shift(bw, -m)
            m *= 2
        return fw + bw

    def inv_count(p, half, extent):
        return 1.0 / (jnp.minimum(p + half, extent) - jnp.maximum(p - half, 0)).astype(F32)

    for g, window in enumerate(POOL_WINDOWS):
        half = window // 2
        cols = slice(g * POOL_CH, (g + 1) * POOL_CH)
        ug = u_ref[:, cols].astype(F32)
        total = box_sum(ug, half, shift_in_row)
        inv = every_row(inv_count(pos, half, width))
        if two_d:
            total = box_sum(total, half, shift_rows)
            row = lax.broadcasted_iota(I32, (n_rows, 1, POOL_CH), 0)
            inv_r = jnp.broadcast_to(inv_count(row, half, n_rows), (n_rows, width, POOL_CH))
            inv = inv * inv_r.reshape(n_tok, POOL_CH)
        d = (total * inv - ug).astype(BF16)
        o_ref[:, cols] = (_dot(d, w_ref[g]) * sc_ref[:, cols]).astype(BF16)


def _pool(u, pool_w, pool_scale, *, batch, seq_len, on_grid):
    t = batch * seq_len
    width = GRID_W if on_grid else seq_len
    n_tok = seq_len if on_grid else seq_len * math.gcd(batch, 4)
    return pl.pallas_call(
        functools.partial(_pool_kernel, n_tok=n_tok, width=width, two_d=on_grid),
        grid=(t // n_tok,),
        in_specs=[pl.BlockSpec((n_tok, POOL_W), lambda b: (b, 0)),
                  pl.BlockSpec((POOL_GROUPS, POOL_CH, POOL_CH), lambda b: (0, 0, 0)),
                  pl.BlockSpec((1, POOL_W), lambda b: (0, 0))],
        out_specs=pl.BlockSpec((n_tok, POOL_W), lambda b: (b, 0)),
        out_shape=jax.ShapeDtypeStruct((t, POOL_W), BF16),
        compiler_params=_cparams("parallel"),
        name=f"pool_l{seq_len}",
    )(u, pool_w, pool_scale)


def _pack_rows(x):
    half = D_MODEL // 2
    lo = lax.bitcast_convert_type(x[:, :half].astype(BF16).astype(F32), U32) >> 16
    hi = lax.bitcast_convert_type(x[:, half:].astype(BF16).astype(F32), U32) & jnp.uint32(0xFFFF0000)
    word = lax.bitcast_convert_type(hi | lo, I32)
    return [word[:, c * LANES:(c + 1) * LANES] for c in range(N_PIECES)]


def _unpack_rows(pieces):
    words = [lax.bitcast_convert_type(p, U32) for p in pieces]
    lo = [lax.bitcast_convert_type(w << 16, F32) for w in words]
    hi = [lax.bitcast_convert_type(w & jnp.uint32(0xFFFF0000), F32) for w in words]
    return lo, hi


def _merge_kernel(x_ref, z_ref, p_ref, ga_ref, gb_ref, mod_ref, g2_ref, wr_ref, wp_ref, wo_ref,
                  x1_ref, *piece_refs):
    y_ret = _dot(z_ref[...], wr_ref[...])
    y_pool = _dot(p_ref[...], wp_ref[...])
    merged = (jax.nn.sigmoid(ga_ref[...].astype(F32)) * y_ret
              + jax.nn.sigmoid(gb_ref[...].astype(F32)) * y_pool)
    x1 = x_ref[...] + mod_ref[0, 2:3, :] * _dot(merged.astype(BF16), wo_ref[...])
    x1_ref[...] = x1
    h2 = _rms_mod(x1, g2_ref[...], mod_ref[0, 4:5, :], mod_ref[0, 3:4, :])
    for ref, piece in zip(piece_refs, _pack_rows(h2)):
        ref[...] = piece


def _merge(x, z, p, ga, gb, mods, norm2_g, w_br_ret, w_br_pool, w_out, *, tokens_per_mod, tm):
    t = x.shape[0]
    tiles_per_mod = tokens_per_mod // tm
    row = lambda w: pl.BlockSpec((tm, w), lambda i: (i, 0))
    full = lambda a: pl.BlockSpec(a.shape, lambda i: (0,) * a.ndim)
    outs = pl.pallas_call(
        _merge_kernel,
        grid=(t // tm,),
        in_specs=[row(D_MODEL), row(RET_V_W), row(POOL_W), row(D_MODEL), row(D_MODEL),
                  pl.BlockSpec((1, 6, D_MODEL), lambda i: (i // tiles_per_mod, 0, 0)),
                  full(norm2_g), full(w_br_ret), full(w_br_pool), full(w_out)],
        out_specs=[row(D_MODEL)] + [row(LANES)] * N_PIECES,
        out_shape=[jax.ShapeDtypeStruct((t, D_MODEL), F32)] + [jax.ShapeDtypeStruct((t, LANES), I32)] * N_PIECES,
        compiler_params=_cparams("parallel"),
        name="merge",
    )(x, z, p, ga, gb, mods, norm2_g, w_br_ret, w_br_pool, w_out)
    return outs[0], outs[1:]


def _route_kernel(*refs):
    h_refs = refs[:N_PIECES]
    rw_ref, bias_ref, idx_ref, rank_ref, wk_ref, cnt_ref, carry_ref, before_ref = refs[N_PIECES:]
    e = N_EXPERTS
    tm = h_refs[0].shape[0]
    neg = -jnp.inf

    @pl.when(pl.program_id(0) == 0)
    def _():
        carry_ref[...] = jnp.zeros(carry_ref.shape, F32)
        t_row = lax.broadcasted_iota(I32, (tm, tm), 0)
        t_col = lax.broadcasted_iota(I32, (tm, tm), 1)
        before_ref[...] = (t_row < t_col).astype(BF16)

    lo, hi = _unpack_rows([r[...] for r in h_refs])
    h = jnp.concatenate(lo + hi, axis=1).astype(BF16)
    logits = lax.dot_general(rw_ref[...], h, (((1,), (1,)), ((), ())), preferred_element_type=F32)[:e]
    scores = jax.nn.sigmoid(logits)
    sel = scores + bias_ref[:e, 0:1]
    e_idx = lax.broadcasted_iota(I32, (e, tm), 0)

    grp = sel.reshape(N_EXPERT_GROUPS, GROUP_SIZE, tm)
    m_idx = lax.broadcasted_iota(I32, grp.shape, 1)
    m1 = jnp.max(grp, axis=1, keepdims=True)
    first = jnp.min(jnp.where(grp == m1, m_idx, GROUP_SIZE), axis=1, keepdims=True)
    m2 = jnp.max(jnp.where(m_idx == first, neg, grp), axis=1, keepdims=True)
    gscore = (m1 + m2).reshape(N_EXPERT_GROUPS, tm)

    g_idx = lax.broadcasted_iota(I32, gscore.shape, 0)
    grank = jnp.zeros(gscore.shape, I32)
    for g in range(N_EXPERT_GROUPS):
        other = gscore[g:g + 1, :]
        beats = jnp.where(other > gscore, 1, jnp.where(other == gscore, (g_idx > g).astype(I32), 0))
        grank = grank + beats
    gkeep = (grank < TOPK_GROUPS).astype(F32)
    ekeep = jnp.broadcast_to(gkeep.reshape(N_EXPERT_GROUPS, 1, tm), grp.shape).reshape(e, tm)
    masked = jnp.where(ekeep > 0, sel, neg)

    chosen = jnp.zeros((e, tm), F32)
    picks, hits = [], []
    for _ in range(TOP_K):
        m = jnp.max(masked, axis=0, keepdims=True)
        pick = jnp.min(jnp.where(masked == m, e_idx, e), axis=0, keepdims=True)
        hit = e_idx == pick
        chosen = jnp.where(hit, 1.0, chosen)
        masked = jnp.where(hit, neg, masked)
        picks.append(pick)
        hits.append(hit)

    w = scores * chosen
    comb = w / jnp.sum(w, axis=0, keepdims=True) * ROUTED_SCALE

    rankmat = _dot(chosen.astype(BF16), before_ref[...]) + carry_ref[:e, 0:1]
    carry_ref[:e, :] = carry_ref[:e, :] + jnp.sum(chosen, axis=1, keepdims=True)
    cnt_ref[...] = carry_ref[...]

    idx_ref[...] = jnp.concatenate(picks, axis=0)
    rank_ref[...] = jnp.concatenate(
        [jnp.sum(jnp.where(h, rankmat, 0.0), axis=0, keepdims=True) for h in hits], axis=0).astype(I32)
    wk_ref[...] = jnp.concatenate(
        [jnp.sum(jnp.where(h, comb, 0.0), axis=0, keepdims=True) for h in hits], axis=0)


def _route(h2_pieces, router_wt, bias_col, *, tm):
    t = h2_pieces[0].shape[0]
    krow = pl.BlockSpec((TOP_K, tm), lambda i: (0, i))
    return pl.pallas_call(
        _route_kernel,
        grid=(t // tm,),
        in_specs=[pl.BlockSpec((tm, LANES), lambda i: (i, 0))] * N_PIECES
        + [pl.BlockSpec((LANES, D_MODEL), lambda i: (0, 0)), pl.BlockSpec((LANES, 1), lambda i: (0, 0))],
        out_specs=[krow, krow, krow, pl.BlockSpec((LANES, LANES), lambda i: (0, 0))],
        out_shape=[jax.ShapeDtypeStruct((TOP_K, t), I32), jax.ShapeDtypeStruct((TOP_K, t), I32),
                   jax.ShapeDtypeStruct((TOP_K, t), F32), jax.ShapeDtypeStruct((LANES, LANES), F32)],
        scratch_shapes=[pltpu.VMEM((LANES, LANES), F32), pltpu.VMEM((tm, tm), BF16)],
        compiler_params=_cparams("arbitrary"),
        name="route",
    )(*h2_pieces, router_wt, bias_col)


def _plan_kernel(idx_ref, rank_ref, cnt_ref, pos_ref, te_ref, nv_ref, nu_ref, *, group_tile):
    tf = idx_ref.shape[1]
    nt = te_ref.shape[1]
    cnt = cnt_ref[...].astype(I32)
    padded = (((cnt + (group_tile - 1)) // group_tile) * group_tile).astype(F32)
    e_sub = lax.broadcasted_iota(I32, (LANES, LANES), 0)
    e_lane = lax.broadcasted_iota(I32, (LANES, LANES), 1)
    base = jnp.sum(jnp.where(e_lane < e_sub, padded.T, 0.0), axis=1, keepdims=True)
    end = base + padded[:, 0:1]

    idx = idx_ref[...]
    start = jnp.zeros(idx.shape, F32)
    for e in range(N_EXPERTS):
        start = jnp.where(idx == e, base[e:e + 1, 0:1], start)
    pos = start.astype(I32) + rank_ref[...]
    for j in range(tf // SC_CHUNK):
        pos_ref[j] = pos[:, j * SC_CHUNK:(j + 1) * SC_CHUNK]

    tile_start = (lax.broadcasted_iota(I32, (N_EXPERTS, nt), 1) * group_tile).astype(F32)
    done = jnp.sum(jnp.where(end[:N_EXPERTS] <= tile_start, 1.0, 0.0), axis=0, keepdims=True)
    te_ref[...] = jnp.minimum(done, N_EXPERTS - 1.0).astype(I32)
    in_group = (base[:N_EXPERTS] <= tile_start) & (tile_start < end[:N_EXPERTS])
    real = jnp.clip(base[:N_EXPERTS] + cnt[:N_EXPERTS, 0:1].astype(F32) - tile_start, 0.0, float(group_tile))
    nv_ref[...] = jnp.sum(jnp.where(in_group, real, 0.0), axis=0, keepdims=True).astype(I32)
    total = jnp.sum(padded[:, 0:1], axis=0, keepdims=True)
    nu_ref[...] = jnp.broadcast_to(total * (1.0 / group_tile), nu_ref.shape).astype(I32)


def _plan(idx, rank, counts, *, n_tiles, tf, group_tile):
    t = idx.shape[1]
    nt_pad = -(-n_tiles // LANES) * LANES
    krow = pl.BlockSpec((TOP_K, tf), lambda i: (0, i))
    return pl.pallas_call(
        functools.partial(_plan_kernel, group_tile=group_tile),
        grid=(t // tf,),
        in_specs=[krow, krow, pl.BlockSpec((LANES, LANES), lambda i: (0, 0))],
        out_specs=[pl.BlockSpec((tf // SC_CHUNK, TOP_K, SC_CHUNK), lambda i: (i, 0, 0)),
                   pl.BlockSpec((1, nt_pad), lambda i: (0, 0)),
                   pl.BlockSpec((1, nt_pad), lambda i: (0, 0)),
                   pl.BlockSpec((1, LANES), lambda i: (0, 0))],
        out_shape=[jax.ShapeDtypeStruct((t // SC_CHUNK, TOP_K, SC_CHUNK), I32),
                   jax.ShapeDtypeStruct((1, nt_pad), I32), jax.ShapeDtypeStruct((1, nt_pad), I32),
                   jax.ShapeDtypeStruct((1, LANES), I32)],
        compiler_params=_cparams("arbitrary"),
        name="moe_plan",
    )(idx, rank, counts)


def _sc_mesh_info():
    info = plsc.get_sparse_core_info()
    mesh = plsc.VectorSubcoreMesh(core_axis_name="c", subcore_axis_name="s")
    return mesh, info.num_cores, info.num_cores * info.num_subcores


def _sc_dispatch(pieces, pos, *, n_rows):
    t = pieces[0].shape[0]
    mesh, n_cores, n_workers = _sc_mesh_info()
    per_w = t // SC_CHUNK // n_workers

    @functools.partial(
        pl.kernel, mesh=mesh,
        out_type=[jax.ShapeDtypeStruct((n_rows, LANES), I32)] * N_PIECES,
        scratch_types=[pltpu.VMEM((TOP_K, SC_CHUNK), I32),
                       pltpu.VMEM((N_PIECES, SC_CHUNK, LANES), I32),
                       pltpu.SemaphoreType.DMA((N_PIECES,)),
                       pltpu.SemaphoreType.DMA],
        name="sc_dispatch",
    )
    def run(*refs):
        src = refs[:N_PIECES]
        pos_hbm = refs[N_PIECES]
        dst = refs[N_PIECES + 1:2 * N_PIECES + 1]
        idx_v, rows_v, load_sem, put_sem = refs[2 * N_PIECES + 1:]
        wid = lax.axis_index("s") * n_cores + lax.axis_index("c")

        @pl.loop(0, per_w)
        def _(j):
            ch = wid * per_w + j
            t0 = pl.multiple_of(ch * SC_CHUNK, SC_CHUNK)
            loads = [pltpu.make_async_copy(src[c].at[pl.ds(t0, SC_CHUNK)], rows_v.at[c], load_sem.at[c])
                     for c in range(N_PIECES)]
            for ld in loads:
                ld.start()
            pltpu.sync_copy(pos_hbm.at[ch], idx_v)
            puts = []
            for c in range(N_PIECES):
                loads[c].wait()
                for k in range(TOP_K):
                    puts.append(pltpu.make_async_copy(rows_v.at[c], dst[c].at[idx_v.at[k]], put_sem))
                    puts[-1].start()
            for cp in puts:
                cp.wait()

    return run(*pieces, pos)


SC_GROUP = 32


def _sc_combine(pieces, pos, wts, *, n_tokens):
    mesh, n_cores, n_workers = _sc_mesh_info()
    lanes = plsc.get_sparse_core_info().num_lanes
    per_w = n_tokens // SC_GROUP // n_workers

    @functools.partial(
        pl.kernel, mesh=mesh,
        out_type=[jax.ShapeDtypeStruct((n_tokens, LANES), I32)] * N_PIECES,
        scratch_types=[pltpu.VMEM((TOP_K, SC_GROUP), I32),
                       pltpu.VMEM((TOP_K, SC_GROUP), F32),
                       pltpu.VMEM((2, TOP_K, SC_GROUP, LANES), I32),
                       pltpu.VMEM((2, SC_GROUP, LANES), I32),
                       pltpu.SemaphoreType.DMA((2,)),
                       pltpu.SemaphoreType.DMA((2,))],
        compiler_params=pltpu.CompilerParams(needs_layout_passes=False),
        name="sc_combine",
    )
    def run(*refs):
        src = refs[:N_PIECES]
        pos_hbm, wts_hbm = refs[N_PIECES:N_PIECES + 2]
        dst = refs[N_PIECES + 2:2 * N_PIECES + 2]
        idx_v, w_v, buf, acc, get_sem, put_sem = refs[2 * N_PIECES + 2:]
        wid = lax.axis_index("s") * n_cores + lax.axis_index("c")

        @pl.loop(0, per_w)
        def _(j):
            grp = wid * per_w + j
            t0 = pl.multiple_of(grp * SC_GROUP, SC_GROUP)
            pltpu.sync_copy(pos_hbm.at[grp], idx_v)
            pltpu.sync_copy(wts_hbm.at[grp], w_v)

            def gets(c, slot):
                return [pltpu.make_async_copy(src[c].at[idx_v.at[k]], buf.at[slot, k], get_sem.at[slot])
                        for k in range(TOP_K)]

            def puts(c, slot):
                return [pltpu.make_async_copy(acc.at[slot], dst[c].at[pl.ds(t0, SC_GROUP)], put_sem.at[slot])]

            for cp in gets(0, 0):
                cp.start()
            for c in range(N_PIECES):
                slot = c % 2
                if c + 1 < N_PIECES:
                    for cp in gets(c + 1, 1 - slot):
                        cp.start()
                for cp in gets(c, slot):
                    cp.wait()
                if c >= 2:
                    for cp in puts(c - 2, slot):
                        cp.wait()

                @pl.loop(0, SC_GROUP)
                def _(r):
                    row = jnp.full((lanes,), r, I32)
                    w = []
                    for k in range(TOP_K):
                        w_k = plsc.load_gather(w_v, [jnp.full((lanes,), k, I32), row])
                        w.append(plsc.pack(w_k, w_k, format=plsc.PackFormat.INTERLEAVED))
                    for q in range(LANES // lanes):
                        cols = pl.ds(q * lanes, lanes)
                        total = None
                        for k in range(TOP_K):
                            term = plsc.bitcast(buf[slot, k, r, cols], BF16) * w[k]
                            total = term if total is None else total + term
                        acc[slot, r, cols] = plsc.bitcast(total, I32)

                for cp in puts(c, slot):
                    cp.start()
            for c in range(N_PIECES - 2, N_PIECES):
                for cp in puts(c, c % 2):
                    cp.wait()

    return run(*pieces, pos, wts)


def _group_tile(n_tokens):
    per_expert = n_tokens * TOP_K // N_EXPERTS
    return max(MXU_DIM, min(4 * MXU_DIM, per_expert // 2 // MXU_DIM * MXU_DIM))


SECOND_DMA_QUEUE = 1


def _experts_kernel(te_ref, nv_ref, nu_ref, *refs, tile):
    x_hbm = refs[:N_PIECES]
    wg_ref, wu_ref, wd_ref = refs[N_PIECES:N_PIECES + 3]
    y_hbm = refs[N_PIECES + 3:2 * N_PIECES + 3]
    xbuf, xsem, ybuf, ysem = refs[2 * N_PIECES + 3:]
    i = pl.program_id(0)
    n_used = nu_ref[0]
    slot = lax.rem(i, 2)

    half = tile // 2

    def x_copy(step, into, h, c):
        r = pl.ds(pl.multiple_of(step * tile + h * half, half), half)
        return pltpu.make_async_copy(x_hbm[c].at[r], xbuf.at[into, c, pl.ds(h * half, half)], xsem.at[into, c])

    def y_copy(step, out_of, h, c):
        r = pl.ds(pl.multiple_of(step * tile + h * half, half), half)
        return pltpu.make_async_copy(ybuf.at[out_of, c, pl.ds(h * half, half)], y_hbm[c].at[r], ysem.at[out_of, c])

    def real_halves(copy, step, buf, act):
        for c in range(N_PIECES):
            act(copy(step, buf, 0, c))

        @pl.when(nv_ref[step] > half)
        def _():
            for c in range(N_PIECES):
                act(copy(step, buf, 1, c))

    start = lambda cp: cp.start(priority=SECOND_DMA_QUEUE)
    wait = lambda cp: cp.wait()

    @pl.when(i == 0)
    def _():
        real_halves(x_copy, 0, 0, start)

    @pl.when(i + 1 < n_used)
    def _():
        real_halves(x_copy, i + 1, 1 - slot, start)

    @pl.when(i < n_used)
    def _():
        real_halves(x_copy, i, slot, wait)
        subs = [slice(s * MXU_DIM, (s + 1) * MXU_DIM) for s in range(tile // MXU_DIM)]
        xs = []
        for rows in subs:
            lo, hi = _unpack_rows([xbuf[slot, c, rows, :] for c in range(N_PIECES)])
            xs.append(jnp.concatenate(lo + hi, axis=1).astype(BF16))
        gates = [(_dot(x, wg_ref[...]), _dot(x, wu_ref[...])) for x in xs]
        ys = [_dot((_silu(g) * u).astype(BF16), wd_ref[...]) for g, u in gates]
        for rows, y in zip(subs, ys):
            for c, piece in enumerate(_pack_rows(y)):
                ybuf[slot, c, rows, :] = piece

        @pl.when(i >= 1)
        def _():
            real_halves(y_copy, i - 1, 1 - slot, wait)

        real_halves(y_copy, i, slot, start)

        @pl.when(i == n_used - 1)
        def _():
            real_halves(y_copy, i, slot, wait)


def _experts(x_pieces, tile_expert, tile_rows, n_used, wg, wu, wd, *, group_tile):
    n_rows = x_pieces[0].shape[0]
    n_tiles = n_rows // group_tile

    wspec = lambda a: pl.BlockSpec((None,) + a.shape[1:],
                                   lambda i, te, nv, nu: (te[jnp.minimum(i, nu[0] - 1)], 0, 0))
    return pl.pallas_call(
        functools.partial(_experts_kernel, tile=group_tile),
        grid_spec=pltpu.PrefetchScalarGridSpec(
            num_scalar_prefetch=3,
            grid=(n_tiles,),
            in_specs=[pl.BlockSpec(memory_space=pl.ANY)] * N_PIECES + [wspec(wg), wspec(wu), wspec(wd)],
            out_specs=[pl.BlockSpec(memory_space=pl.ANY)] * N_PIECES,
            scratch_shapes=[pltpu.VMEM((2, N_PIECES, group_tile, LANES), I32),
                            pltpu.SemaphoreType.DMA((2, N_PIECES))] * 2),
        out_shape=[jax.ShapeDtypeStruct((n_rows, LANES), I32)] * N_PIECES,
        compiler_params=_cparams("arbitrary"),
        name="experts",
    )(tile_expert, tile_rows, n_used, *x_pieces, wg, wu, wd)


def _moe_out_kernel(x1_ref, mod_ref, fg_ref, sg_ref, su_ref, sd_ref, *refs):
    h_refs = refs[:N_PIECES]
    routed_refs = refs[N_PIECES:2 * N_PIECES]
    out_ref = refs[2 * N_PIECES]
    h_lo, h_hi = _unpack_rows([r[...] for r in h_refs])
    h = jnp.concatenate(h_lo + h_hi, axis=1).astype(BF16)
    hid = _silu(_dot(h, sg_ref[...])) * _dot(h, su_ref[...])
    shared = _dot(hid.astype(BF16), sd_ref[...])
    r_lo, r_hi = _unpack_rows([r[...] for r in routed_refs])
    routed = jnp.concatenate(r_lo + r_hi, axis=1)
    x2 = x1_ref[...] + mod_ref[0, 5:6, :] * (shared + routed)
    out_ref[...] = x2 * lax.rsqrt(jnp.mean(x2 * x2, axis=-1, keepdims=True) + EPS) * fg_ref[...]


def _moe_out(h2_pieces, x1, mods, final_g, sg, su, sd, routed_pieces, *, tokens_per_mod, tm):
    t = x1.shape[0]
    tiles_per_mod = tokens_per_mod // tm
    row = lambda w: pl.BlockSpec((tm, w), lambda i: (i, 0))
    full = lambda a: pl.BlockSpec(a.shape, lambda i: (0,) * a.ndim)
    return pl.pallas_call(
        _moe_out_kernel,
        grid=(t // tm,),
        in_specs=[row(D_MODEL),
                  pl.BlockSpec((1, 6, D_MODEL), lambda i: (i // tiles_per_mod, 0, 0)),
                  full(final_g), full(sg), full(su), full(sd)]
        + [row(LANES)] * (2 * N_PIECES),
        out_specs=row(D_MODEL),
        out_shape=jax.ShapeDtypeStruct((t, D_MODEL), F32),
        compiler_params=_cparams("parallel"),
        name="moe_out",
    )(x1, mods, final_g, sg, su, sd, *h2_pieces, *routed_pieces)


def _trunk(x, mods, s0, w, expert_w, *, batch, seq_len, on_grid):
    t = batch * seq_len
    tokens_per_mod = t // mods.shape[0]
    tm_in = 512
    cos_t, sin_t = _rope_tables(max(seq_len, tm_in))
    to_cast = expert_w if expert_w[0].dtype != BF16 else ()
    (q, k, v, gsw, up, ga, gb), casted = _inproj(x, mods, w["norm1_g"], w["w_in"], cos_t, sin_t, to_cast,
                                                 tokens_per_mod=tokens_per_mod, seq_len=seq_len,
                                                 on_grid=on_grid, tm=tm_in)
    expert_w = casted or expert_w
    z, s_f, s_b = _retention(q, k, v, gsw, w["dec"], s0, batch=batch, seq_len=seq_len)
    p = _pool(up, w["pool_w"], w["pool_scale"], batch=batch, seq_len=seq_len, on_grid=on_grid)
    x1, h2_pieces = _merge(x, z, p, ga, gb, mods, w["norm2_g"], w["w_br_ret"], w["w_br_pool"],
                               w["w_out"], tokens_per_mod=tokens_per_mod, tm=512)

    group_tile = _group_tile(t)
    n_rows = t * TOP_K + N_EXPERTS * group_tile
    idx, rank, wts, counts = _route(h2_pieces, w["router_wt"], w["router_bias"], tm=1024)
    pos, tile_expert, tile_rows, n_used = _plan(idx, rank, counts, n_tiles=n_rows // group_tile, tf=2048,
                                                group_tile=group_tile)
    x_sorted = _sc_dispatch(h2_pieces, pos, n_rows=n_rows)
    y_sorted = _experts(x_sorted, tile_expert.reshape(-1), tile_rows.reshape(-1), n_used.reshape(-1),
                        *expert_w, group_tile=group_tile)
    regroup = lambda a: a.reshape(TOP_K, t // SC_GROUP, SC_GROUP).transpose(1, 0, 2)
    pos_rows = pos.transpose(1, 0, 2).reshape(TOP_K, t)
    routed = _sc_combine(y_sorted, regroup(pos_rows), regroup(wts), n_tokens=t)
    y = _moe_out(h2_pieces, x1, mods, w["final_g"], w["sh_w_gate"], w["sh_w_up"], w["sh_w_down"], routed,
                 tokens_per_mod=tokens_per_mod, tm=1024)
    return y, s_f, s_b, expert_w


def kernel(x_prompt, x_sample, state_ret_fwd, state_ret_bwd, c, c_ctx, ada_w, ada_b, norm1_g, norm2_g, w_in,
           ret_decay_fwd, ret_decay_bwd, w_br_ret, pool_w, pool_scale, w_br_pool, w_out, router_w, router_bias,
           exp_w_gate, exp_w_up, exp_w_down, sh_w_gate, sh_w_up, sh_w_down, final_norm_g):
    n_req, seq, d = x_prompt.shape
    n_dec, dec_seq, _ = x_sample.shape
    depth = ada_w.shape[0]
    assert depth == 1 and d == D_MODEL

    xc = x_prompt.reshape(n_req * seq, d)
    xs = x_sample.reshape(n_dec * dec_seq, d)
    new_f, new_b = [], []
    for l in range(depth):
        c_rows = jnp.concatenate([c_ctx[None, :], c, jnp.zeros((8 - 1 - n_dec, d), F32)], axis=0)
        mods = _ada(c_rows, ada_w[l], ada_b[l]).reshape(8, 6, d)
        pad_rows = LANES - N_EXPERTS
        w = dict(
            norm1_g=norm1_g[l].reshape(1, d), norm2_g=norm2_g[l].reshape(1, d),
            final_g=final_norm_g.reshape(1, d),
            w_in=w_in[l].astype(BF16),
            dec=jnp.stack([ret_decay_fwd[l], ret_decay_bwd[l]]).astype(F32),
            w_br_ret=w_br_ret[l].astype(BF16), pool_w=pool_w[l].astype(BF16),
            pool_scale=pool_scale[l].reshape(1, POOL_W), w_br_pool=w_br_pool[l].astype(BF16),
            w_out=w_out[l].astype(BF16),
            router_wt=jnp.pad(router_w[l].T, ((0, pad_rows), (0, 0))).astype(BF16),
            router_bias=jnp.pad(router_bias[l].astype(F32).reshape(N_EXPERTS, 1), ((0, pad_rows), (0, 0))),
            sh_w_gate=sh_w_gate[l].astype(BF16),
            sh_w_up=sh_w_up[l].astype(BF16), sh_w_down=sh_w_down[l].astype(BF16),
        )
        cached = (state_ret_fwd[:, l].astype(F32), state_ret_bwd[:, l].astype(F32))
        expert_w = (exp_w_gate[l], exp_w_up[l], exp_w_down[l])
        xs, _, _, expert_w = _trunk(xs, mods[1:1 + n_dec], cached, w, expert_w,
                                    batch=n_dec, seq_len=dec_seq, on_grid=True)
        xc, s_f, s_b, _ = _trunk(xc, mods[0:1], None, w, expert_w, batch=n_req, seq_len=seq, on_grid=False)
        new_f.append(s_f)
        new_b.append(s_b)
    y_prompt = xc.reshape(n_req, seq, d)
    y_sample = xs.reshape(n_dec, dec_seq, d)
    return (y_prompt, y_sample, jnp.stack(new_f, axis=1).astype(x_prompt.dtype),
            jnp.stack(new_b, axis=1).astype(x_prompt.dtype))
```

```python
import functools
import math

import numpy as np
import jax
import jax.numpy as jnp
from jax import lax
from jax.experimental import pallas as pl
from jax.experimental.pallas import tpu as pltpu
from jax.experimental.pallas import tpu_sc as plsc

D_MODEL = 1024
GRID_W = 64
RET_HEADS = 4
RET_DK = 128
RET_DV = 256
RET_QK_W = RET_HEADS * RET_DK
RET_V_W = RET_HEADS * RET_DV
RET_CHUNK = 128
ROPE_BASE = 10000.0
POOL_GROUPS = 4
POOL_CH = 128
POOL_W = POOL_GROUPS * POOL_CH
POOL_WINDOWS = (2, 4, 8, 16)
N_EXPERTS = 64
TOP_K = 8
N_EXPERT_GROUPS = 8
GROUP_SIZE = N_EXPERTS // N_EXPERT_GROUPS
TOPK_GROUPS = 4
D_EXPERT = 256
ROUTED_SCALE = 2.5
EPS = 1e-6
IN_SIZES = (RET_QK_W, RET_QK_W, RET_V_W, RET_V_W, POOL_W, D_MODEL, D_MODEL)
IN_OFFS = tuple(sum(IN_SIZES[:i]) for i in range(len(IN_SIZES) + 1))
IN_W = IN_OFFS[-1]

LANES = 128
VMEM_LIMIT = 56 << 20
N_PIECES = D_MODEL // 2 // LANES
MXU_DIM = 256
SC_CHUNK = 128
PROJ_TILE = 512
ROUTE_TILE = 1024
PLAN_TILE = 2048
OUT_TILE = 1024

F32 = jnp.float32
BF16 = jnp.bfloat16
I32 = jnp.int32
U32 = jnp.uint32


def _cparams(*sem):
    return pltpu.CompilerParams(dimension_semantics=sem, vmem_limit_bytes=VMEM_LIMIT)


def _dot(a, b):
    return jnp.dot(a, b, preferred_element_type=F32)


def _silu(x):
    return x * jax.nn.sigmoid(x)


def _rms_mod(x, g, scale, shift):
    y = x * lax.rsqrt(jnp.mean(x * x, axis=-1, keepdims=True) + EPS)
    return (y * g) * (1.0 + scale) + shift


def _ada_kernel(c_ref, w_ref, b_ref, o_ref):
    c = c_ref[...]
    o_ref[...] = jnp.dot(_silu(c), w_ref[...], preferred_element_type=F32,
                         precision=lax.Precision.HIGHEST) + b_ref[...]


def _ada(c_rows, ada_w, ada_b):
    r = c_rows.shape[0]
    n = ada_w.shape[1]
    tn = 2 * D_MODEL
    return pl.pallas_call(
        _ada_kernel,
        grid=(n // tn,),
        in_specs=[pl.BlockSpec((r, D_MODEL), lambda j: (0, 0)),
                  pl.BlockSpec((D_MODEL, tn), lambda j: (0, j)),
                  pl.BlockSpec((1, tn), lambda j: (0, j))],
        out_specs=pl.BlockSpec((r, tn), lambda j: (0, j)),
        out_shape=jax.ShapeDtypeStruct((r, n), F32),
        compiler_params=_cparams("parallel"),
        name="ada_mod",
    )(c_rows, ada_w, ada_b.reshape(1, n))


def _inproj_kernel(x_ref, mod_ref, g_ref, w_ref, cos_ref, sin_ref, *refs, on_grid):
    n_side = (len(refs) - len(IN_SIZES)) // 2
    side_in = refs[:n_side]
    q_ref, k_ref, v_ref, gsw_ref, up_ref, ga_ref, gb_ref = refs[n_side:n_side + len(IN_SIZES)]
    side_out = refs[n_side + len(IN_SIZES):]
    for src, dst in zip(side_in, side_out):
        dst[...] = src[...].astype(BF16)
    for s in range(x_ref.shape[0] // MXU_DIM):
        rows = slice(s * MXU_DIM, (s + 1) * MXU_DIM)
        h = _rms_mod(x_ref[rows, :], g_ref[...], mod_ref[0, 1:2, :], mod_ref[0, 0:1, :]).astype(BF16)

        def seg(i):
            return _dot(h, w_ref[:, IN_OFFS[i]:IN_OFFS[i + 1]])

        q = seg(0)
        k = seg(1)
        if on_grid:
            cos = jnp.concatenate([cos_ref[rows, :]] * RET_HEADS, axis=1)
            sin = jnp.concatenate([sin_ref[rows, :]] * RET_HEADS, axis=1)
            lane = lax.broadcasted_iota(jnp.int32, q.shape, 1)
            first = (lane & 63) < 32

            def rope(a):
                up = pltpu.roll(a, RET_QK_W - 32, axis=1)
                dn = pltpu.roll(a, 32, axis=1)
                return a * cos + jnp.where(first, up, dn) * sin

            q = rope(q)
            k = rope(k)
        q_ref[rows, :] = q.astype(BF16)
        k_ref[rows, :] = (k * (RET_DK ** -0.5)).astype(BF16)
        v_ref[rows, :] = seg(2).astype(BF16)
        gsw_ref[rows, :] = seg(3).astype(BF16)
        up_ref[rows, :] = seg(4).astype(BF16)
        ga_ref[rows, :] = seg(5).astype(BF16)
        gb_ref[rows, :] = seg(6).astype(BF16)


def _inproj(x, mods, norm_g, w_in, cos_t, sin_t, side_cast=(), *, tokens_per_mod, seq_len, on_grid, tm):
    t = x.shape[0]
    steps = t // tm
    tiles_per_mod = tokens_per_mod // tm
    tiles_per_seq = max(seq_len // tm, 1)
    widths = IN_SIZES
    out_shape = [jax.ShapeDtypeStruct((t, w), BF16) for w in widths]
    out_specs = [pl.BlockSpec((tm, w), lambda i: (i, 0)) for w in widths]
    side_specs = [pl.BlockSpec((a.shape[0] // steps,) + a.shape[1:], lambda i: (i, 0, 0)) for a in side_cast]
    outs = pl.pallas_call(
        functools.partial(_inproj_kernel, on_grid=on_grid),
        grid=(steps,),
        in_specs=[pl.BlockSpec((tm, D_MODEL), lambda i: (i, 0)),
                  pl.BlockSpec((1, 6, D_MODEL), lambda i: (i // tiles_per_mod, 0, 0)),
                  pl.BlockSpec((1, D_MODEL), lambda i: (0, 0)),
                  pl.BlockSpec((D_MODEL, IN_W), lambda i: (0, 0), pipeline_mode=pl.Buffered(1)),
                  pl.BlockSpec((tm, RET_DK), lambda i: (i % tiles_per_seq, 0)),
                  pl.BlockSpec((tm, RET_DK), lambda i: (i % tiles_per_seq, 0))] + side_specs,
        out_specs=out_specs + side_specs,
        out_shape=out_shape + [jax.ShapeDtypeStruct(a.shape, BF16) for a in side_cast],
        compiler_params=_cparams("parallel"),
        name="inproj_grid" if on_grid else "inproj_seq",
    )(x, mods, norm_g, w_in, cos_t, sin_t, *side_cast)
    return outs[:len(widths)], tuple(outs[len(widths):])


def _rope_tables(seq_len):
    t = np.arange(seq_len)
    row = (t // GRID_W).astype(np.float32)
    col = (t % GRID_W).astype(np.float32)
    m = RET_DK // 4
    inv = (np.float32(ROPE_BASE) ** (-np.arange(m, dtype=np.float32) / np.float32(m))).astype(np.float32)
    ar = row[:, None] * inv
    ac = col[:, None] * inv
    cos = np.concatenate([np.cos(ar), np.cos(ar), np.cos(ac), np.cos(ac)], axis=1)
    sin = np.concatenate([-np.sin(ar), np.sin(ar), -np.sin(ac), np.sin(ac)], axis=1)
    return jnp.asarray(cos, F32), jnp.asarray(sin, F32)


def _ret_heads_per_step(seq_len):
    per_head = seq_len * (2 * 2 * (2 * RET_DK + 3 * RET_DV) + 4 * RET_DV + 2 * RET_DK)
    heads = RET_HEADS
    while heads > 1 and heads * per_head > VMEM_LIMIT * 3 // 4:
        heads //= 2
    return heads


def _ret_kernel(dec_ref, q_ref, k_ref, v_ref, g_ref, *refs, n_chunks, heads, zero_init):
    s0_refs = () if zero_init else refs[:2]
    z_ref, sf_ref, sb_ref, oacc_ref, kt_ref = refs[len(s0_refs):]
    c = RET_CHUNK
    half = n_chunks // 2
    ii = lax.broadcasted_iota(I32, (c, c), 0)
    jj = lax.broadcasted_iota(I32, (c, c), 1)
    ik = lax.broadcasted_iota(I32, (c, RET_DK), 0).astype(F32)
    jk = lax.broadcasted_iota(I32, (RET_DK, c), 1).astype(F32)

    def log_gamma(d, shape):
        return jnp.log1p(-jnp.exp2(-jnp.full(shape, d, F32)))

    consts = {}
    for hh in range(heads):
        h = pl.program_id(1) * heads + hh
        dec_f = dec_ref[0, h]
        dec_b = dec_ref[1, h]
        rel = (ii - jj).astype(F32)
        consts[hh, "f"] = (
            jnp.where(rel >= 0, jnp.exp(log_gamma(dec_f, (c, c)) * jnp.maximum(rel, 0.0)), 0.0),
            jnp.exp(log_gamma(dec_f, (c, RET_DK)) * (ik + 1.0)),
            jnp.exp(log_gamma(dec_f, (RET_DK, c)) * (c - 1.0 - jk)),
            jnp.exp(log_gamma(dec_f, (RET_DK, RET_DV)) * c))
        consts[hh, "b"] = (
            jnp.where(rel <= 0, jnp.exp(log_gamma(dec_b, (c, c)) * jnp.maximum(-rel, 0.0)), 0.0),
            jnp.exp(log_gamma(dec_b, (c, RET_DK)) * (c - ik)),
            jnp.exp(log_gamma(dec_b, (RET_DK, c)) * jk),
            jnp.exp(log_gamma(dec_b, (RET_DK, RET_DV)) * c))

    for s_ref, s0_ref in zip((sf_ref, sb_ref), s0_refs or (None, None)):
        s_ref[...] = jnp.zeros(s_ref.shape, F32) if zero_init else s0_ref[...]

    def transpose_keys(ci, carry):
        r = pl.ds(pl.multiple_of(ci * c, c), c)
        for hh in range(heads):
            kt_ref[hh, ci] = k_ref[r, hh * RET_DK:(hh + 1) * RET_DK].T
        return carry

    lax.fori_loop(0, n_chunks, transpose_keys, 0)

    def scores(ci, hh, direction):
        r = pl.ds(pl.multiple_of(ci * c, c), c)
        kcols = slice(hh * RET_DK, (hh + 1) * RET_DK)
        qc = q_ref[r, kcols]
        sc = lax.dot_general(qc, k_ref[r, kcols], (((1,), (1,)), ((), ())), preferred_element_type=F32)
        return ci, hh, direction, r, qc, sc

    def advance(job):
        ci, hh, direction, r, qc, sc = job
        dmask, qdec, kdec, cdec = consts[hh, direction]
        s_ref = sf_ref if direction == "f" else sb_ref
        vc = v_ref[r, hh * RET_DV:(hh + 1) * RET_DV]
        s = s_ref[hh]
        lhs = jnp.concatenate([(sc * dmask).astype(BF16), (qc.astype(F32) * qdec).astype(BF16)], axis=1)
        o = _dot(lhs, jnp.concatenate([vc, s.astype(BF16)], axis=0))
        kd_t = (kt_ref[hh, ci].astype(F32) * kdec).astype(BF16)
        s_ref[hh] = s * cdec + _dot(kd_t, vc)
        return o

    def emit(job, o, second):
        _, hh, _, r, _, _ = job
        vcols = slice(hh * RET_DV, (hh + 1) * RET_DV)
        if not second:
            oacc_ref[hh, r, :] = o
        else:
            o = o + oacc_ref[hh, r, :]
            o = o * lax.rsqrt(jnp.mean(o * o, axis=-1, keepdims=True) + EPS)
            g = g_ref[r, vcols].astype(F32)
            z_ref[r, vcols] = (_silu(g) * o).astype(BF16)

    def body(second):
        def run(t, carry):
            jobs = [scores(ci, hh, d) for hh in range(heads)
                    for ci, d in ((t, "f"), (n_chunks - 1 - t, "b"))]
            outs = [advance(job) for job in jobs]
            for job, o in zip(jobs, outs):
                emit(job, o, second)
            return carry
        return run

    lax.fori_loop(0, half, body(False), 0, unroll=4 if half % 4 == 0 else 1)
    lax.fori_loop(half, n_chunks, body(True), 0, unroll=2 if half % 2 == 0 else 1)


def _retention(q, k, v, gsw, dec, s0, *, batch, seq_len):
    n_chunks = seq_len // RET_CHUNK
    assert n_chunks % 2 == 0
    heads = _ret_heads_per_step(seq_len)
    t = batch * seq_len
    st_spec = pl.BlockSpec((None, heads, RET_DK, RET_DV), lambda b, h: (b, h, 0, 0))
    st_shape = jax.ShapeDtypeStruct((batch, RET_HEADS, RET_DK, RET_DV), F32)
    kspec = pl.BlockSpec((seq_len, heads * RET_DK), lambda b, h: (b, h))
    vspec = pl.BlockSpec((seq_len, heads * RET_DV), lambda b, h: (b, h))
    return pl.pallas_call(
        functools.partial(_ret_kernel, n_chunks=n_chunks, heads=heads, zero_init=s0 is None),
        grid=(batch, RET_HEADS // heads),
        in_specs=[pl.BlockSpec(memory_space=pltpu.SMEM), kspec, kspec, vspec, vspec]
        + ([] if s0 is None else [st_spec, st_spec]),
        out_specs=[vspec, st_spec, st_spec],
        out_shape=[jax.ShapeDtypeStruct((t, RET_V_W), BF16), st_shape, st_shape],
        scratch_shapes=[pltpu.VMEM((heads, seq_len, RET_DV), F32),
                        pltpu.VMEM((heads, n_chunks, RET_DK, RET_CHUNK), BF16)],
        compiler_params=_cparams("parallel", "parallel"),
        name=f"retention_l{seq_len}",
    )(dec, q, k, v, gsw, *(s0 or ()))


def _pool_kernel(u_ref, w_ref, sc_ref, o_ref, *, n_tok, width, two_d):
    n_rows = n_tok // width
    pos = lax.broadcasted_iota(I32, (width, POOL_CH), 0)

    def every_row(a):
        return jnp.concatenate([a] * n_rows, axis=0) if n_rows > 1 else a

    def shift_in_row(a, s):
        ok = (pos < width - s) if s > 0 else (pos >= -s)
        return pltpu.roll(a, (-s) % n_tok, axis=0) * every_row(jnp.where(ok, 1.0, 0.0))

    def shift_rows(a, m):
        k = abs(m) * width
        zeros = jnp.zeros((k, POOL_CH), F32)
        return (jnp.concatenate([a[k:], zeros], axis=0) if m > 0
                else jnp.concatenate([zeros, a[:n_tok - k]], axis=0))

    def box_sum(a, half, shift):
        fw = a
        bw = shift(a, -1)
        m = 1
        while m < half:
            fw = fw + shift(fw, m)
            bw = bw + shift(bw, -m)
            m *= 2
        return fw + bw

    def inv_count(p, half, extent):
        return 1.0 / (jnp.minimum(p + half, extent) - jnp.maximum(p - half, 0)).astype(F32)

    for g, window in enumerate(POOL_WINDOWS):
        half = window // 2
        cols = slice(g * POOL_CH, (g + 1) * POOL_CH)
        ug = u_ref[:, cols].astype(F32)
        total = box_sum(ug, half, shift_in_row)
        inv = every_row(inv_count(pos, half, width))
        if two_d:
            total = box_sum(total, half, shift_rows)
            row = lax.broadcasted_iota(I32, (n_rows, 1, POOL_CH), 0)
            inv_r = jnp.broadcast_to(inv_count(row, half, n_rows), (n_rows, width, POOL_CH))
            inv = inv * inv_r.reshape(n_tok, POOL_CH)
        d = (total * inv - ug).astype(BF16)
        o_ref[:, cols] = (_dot(d, w_ref[g]) * sc_ref[:, cols]).astype(BF16)


def _pool(u, pool_w, pool_scale, *, batch, seq_len, on_grid):
    t = batch * seq_len
    width = GRID_W if on_grid else seq_len
    n_tok = seq_len if on_grid else seq_len * math.gcd(batch, 4)
    return pl.pallas_call(
        functools.partial(_pool_kernel, n_tok=n_tok, width=width, two_d=on_grid),
        grid=(t // n_tok,),
        in_specs=[pl.BlockSpec((n_tok, POOL_W), lambda b: (b, 0)),
                  pl.BlockSpec((POOL_GROUPS, POOL_CH, POOL_CH), lambda b: (0, 0, 0)),
                  pl.BlockSpec((1, POOL_W), lambda b: (0, 0))],
        out_specs=pl.BlockSpec((n_tok, POOL_W), lambda b: (b, 0)),
        out_shape=jax.ShapeDtypeStruct((t, POOL_W), BF16),
        compiler_params=_cparams("parallel"),
        name=f"pool_l{seq_len}",
    )(u, pool_w, pool_scale)


def _pack_rows(x):
    half = D_MODEL // 2
    lo = lax.bitcast_convert_type(x[:, :half].astype(BF16).astype(F32), U32) >> 16
    hi = lax.bitcast_convert_type(x[:, half:].astype(BF16).astype(F32), U32) & jnp.uint32(0xFFFF0000)
    word = lax.bitcast_convert_type(hi | lo, I32)
    return [word[:, c * LANES:(c + 1) * LANES] for c in range(N_PIECES)]


def _unpack_rows(pieces):
    words = [lax.bitcast_convert_type(p, U32) for p in pieces]
    lo = [lax.bitcast_convert_type(w << 16, F32) for w in words]
    hi = [lax.bitcast_convert_type(w & jnp.uint32(0xFFFF0000), F32) for w in words]
    return lo, hi


def _merge_kernel(x_ref, z_ref, p_ref, ga_ref, gb_ref, mod_ref, g2_ref, wr_ref, wp_ref, wo_ref,
                  x1_ref, *piece_refs):
    y_ret = _dot(z_ref[...], wr_ref[...])
    y_pool = _dot(p_ref[...], wp_ref[...])
    merged = (jax.nn.sigmoid(ga_ref[...].astype(F32)) * y_ret
              + jax.nn.sigmoid(gb_ref[...].astype(F32)) * y_pool)
    x1 = x_ref[...] + mod_ref[0, 2:3, :] * _dot(merged.astype(BF16), wo_ref[...])
    x1_ref[...] = x1
    h2 = _rms_mod(x1, g2_ref[...], mod_ref[0, 4:5, :], mod_ref[0, 3:4, :])
    for ref, piece in zip(piece_refs, _pack_rows(h2)):
        ref[...] = piece


def _merge(x, z, p, ga, gb, mods, norm2_g, w_br_ret, w_br_pool, w_out, *, tokens_per_mod, tm):
    t = x.shape[0]
    tiles_per_mod = tokens_per_mod // tm
    row = lambda w: pl.BlockSpec((tm, w), lambda i: (i, 0))
    full = lambda a: pl.BlockSpec(a.shape, lambda i: (0,) * a.ndim)
    outs = pl.pallas_call(
        _merge_kernel,
        grid=(t // tm,),
        in_specs=[row(D_MODEL), row(RET_V_W), row(POOL_W), row(D_MODEL), row(D_MODEL),
                  pl.BlockSpec((1, 6, D_MODEL), lambda i: (i // tiles_per_mod, 0, 0)),
                  full(norm2_g), full(w_br_ret), full(w_br_pool), full(w_out)],
        out_specs=[row(D_MODEL)] + [row(LANES)] * N_PIECES,
        out_shape=[jax.ShapeDtypeStruct((t, D_MODEL), F32)] + [jax.ShapeDtypeStruct((t, LANES), I32)] * N_PIECES,
        compiler_params=_cparams("parallel"),
        name="merge",
    )(x, z, p, ga, gb, mods, norm2_g, w_br_ret, w_br_pool, w_out)
    return outs[0], outs[1:]


def _route_kernel(*refs):
    h_refs = refs[:N_PIECES]
    rw_ref, bias_ref, idx_ref, rank_ref, wk_ref, cnt_ref, carry_ref = refs[N_PIECES:]
    e = N_EXPERTS
    tm = h_refs[0].shape[0]
    neg = -jnp.inf

    @pl.when(pl.program_id(0) == 0)
    def _():
        carry_ref[...] = jnp.zeros(carry_ref.shape, F32)

    lo, hi = _unpack_rows([r[...] for r in h_refs])
    h = jnp.concatenate(lo + hi, axis=1).astype(BF16)
    logits = lax.dot_general(rw_ref[...], h, (((1,), (1,)), ((), ())), preferred_element_type=F32)[:e]
    scores = jax.nn.sigmoid(logits)
    sel = scores + bias_ref[:e, 0:1]
    e_idx = lax.broadcasted_iota(I32, (e, tm), 0)

    grp = sel.reshape(N_EXPERT_GROUPS, GROUP_SIZE, tm)
    m_idx = lax.broadcasted_iota(I32, grp.shape, 1)
    m1 = jnp.max(grp, axis=1, keepdims=True)
    first = jnp.min(jnp.where(grp == m1, m_idx, GROUP_SIZE), axis=1, keepdims=True)
    m2 = jnp.max(jnp.where(m_idx == first, neg, grp), axis=1, keepdims=True)
    gscore = (m1 + m2).reshape(N_EXPERT_GROUPS, tm)

    g_idx = lax.broadcasted_iota(I32, gscore.shape, 0)
    grank = jnp.zeros(gscore.shape, I32)
    for g in range(N_EXPERT_GROUPS):
        other = gscore[g:g + 1, :]
        beats = jnp.where(other > gscore, 1, jnp.where(other == gscore, (g_idx > g).astype(I32), 0))
        grank = grank + beats
    gkeep = (grank < TOPK_GROUPS).astype(F32)
    ekeep = jnp.broadcast_to(gkeep.reshape(N_EXPERT_GROUPS, 1, tm), grp.shape).reshape(e, tm)
    masked = jnp.where(ekeep > 0, sel, neg)

    chosen = jnp.zeros((e, tm), F32)
    picks, hits = [], []
    for _ in range(TOP_K):
        m = jnp.max(masked, axis=0, keepdims=True)
        pick = jnp.min(jnp.where(masked == m, e_idx, e), axis=0, keepdims=True)
        hit = e_idx == pick
        chosen = jnp.where(hit, 1.0, chosen)
        masked = jnp.where(hit, neg, masked)
        picks.append(pick)
        hits.append(hit)

    w = scores * chosen
    comb = w / jnp.sum(w, axis=0, keepdims=True) * ROUTED_SCALE

    t_row = lax.broadcasted_iota(I32, (tm, tm), 0)
    t_col = lax.broadcasted_iota(I32, (tm, tm), 1)
    before = (t_row < t_col).astype(BF16)
    rankmat = _dot(chosen.astype(BF16), before) + carry_ref[:e, 0:1]
    carry_ref[:e, :] = carry_ref[:e, :] + jnp.sum(chosen, axis=1, keepdims=True)
    cnt_ref[...] = carry_ref[...]

    idx_ref[...] = jnp.concatenate(picks, axis=0)
    rank_ref[...] = jnp.concatenate(
        [jnp.sum(jnp.where(h, rankmat, 0.0), axis=0, keepdims=True) for h in hits], axis=0).astype(I32)
    wk_ref[...] = jnp.concatenate(
        [jnp.sum(jnp.where(h, comb, 0.0), axis=0, keepdims=True) for h in hits], axis=0)


def _route(h2_pieces, router_wt, bias_col, *, tm):
    t = h2_pieces[0].shape[0]
    krow = pl.BlockSpec((TOP_K, tm), lambda i: (0, i))
    return pl.pallas_call(
        _route_kernel,
        grid=(t // tm,),
        in_specs=[pl.BlockSpec((tm, LANES), lambda i: (i, 0))] * N_PIECES
        + [pl.BlockSpec((LANES, D_MODEL), lambda i: (0, 0)), pl.BlockSpec((LANES, 1), lambda i: (0, 0))],
        out_specs=[krow, krow, krow, pl.BlockSpec((LANES, LANES), lambda i: (0, 0))],
        out_shape=[jax.ShapeDtypeStruct((TOP_K, t), I32), jax.ShapeDtypeStruct((TOP_K, t), I32),
                   jax.ShapeDtypeStruct((TOP_K, t), F32), jax.ShapeDtypeStruct((LANES, LANES), F32)],
        scratch_shapes=[pltpu.VMEM((LANES, LANES), F32)],
        compiler_params=_cparams("arbitrary"),
        name="route",
    )(*h2_pieces, router_wt, bias_col)


def _plan_kernel(idx_ref, rank_ref, cnt_ref, pos_ref, te_ref, nv_ref, nu_ref, *, group_tile):
    tf = idx_ref.shape[1]
    nt = te_ref.shape[1]
    cnt = cnt_ref[...].astype(I32)
    padded = (((cnt + (group_tile - 1)) // group_tile) * group_tile).astype(F32)
    e_sub = lax.broadcasted_iota(I32, (LANES, LANES), 0)
    e_lane = lax.broadcasted_iota(I32, (LANES, LANES), 1)
    base = jnp.sum(jnp.where(e_lane < e_sub, padded.T, 0.0), axis=1, keepdims=True)
    end = base + padded[:, 0:1]

    idx = idx_ref[...]
    start = jnp.zeros(idx.shape, F32)
    for e in range(N_EXPERTS):
        start = jnp.where(idx == e, base[e:e + 1, 0:1], start)
    pos = start.astype(I32) + rank_ref[...]
    for j in range(tf // SC_CHUNK):
        pos_ref[j] = pos[:, j * SC_CHUNK:(j + 1) * SC_CHUNK]

    tile_start = (lax.broadcasted_iota(I32, (N_EXPERTS, nt), 1) * group_tile).astype(F32)
    done = jnp.sum(jnp.where(end[:N_EXPERTS] <= tile_start, 1.0, 0.0), axis=0, keepdims=True)
    te_ref[...] = jnp.minimum(done, N_EXPERTS - 1.0).astype(I32)
    in_group = (base[:N_EXPERTS] <= tile_start) & (tile_start < end[:N_EXPERTS])
    real = jnp.clip(base[:N_EXPERTS] + cnt[:N_EXPERTS, 0:1].astype(F32) - tile_start, 0.0, float(group_tile))
    nv_ref[...] = jnp.sum(jnp.where(in_group, real, 0.0), axis=0, keepdims=True).astype(I32)
    total = jnp.sum(padded[:, 0:1], axis=0, keepdims=True)
    nu_ref[...] = jnp.broadcast_to(total * (1.0 / group_tile), nu_ref.shape).astype(I32)


def _plan(idx, rank, counts, *, n_tiles, tf, group_tile):
    t = idx.shape[1]
    nt_pad = -(-n_tiles // LANES) * LANES
    krow = pl.BlockSpec((TOP_K, tf), lambda i: (0, i))
    return pl.pallas_call(
        functools.partial(_plan_kernel, group_tile=group_tile),
        grid=(t // tf,),
        in_specs=[krow, krow, pl.BlockSpec((LANES, LANES), lambda i: (0, 0))],
        out_specs=[pl.BlockSpec((tf // SC_CHUNK, TOP_K, SC_CHUNK), lambda i: (i, 0, 0)),
                   pl.BlockSpec((1, nt_pad), lambda i: (0, 0)),
                   pl.BlockSpec((1, nt_pad), lambda i: (0, 0)),
                   pl.BlockSpec((1, LANES), lambda i: (0, 0))],
        out_shape=[jax.ShapeDtypeStruct((t // SC_CHUNK, TOP_K, SC_CHUNK), I32),
                   jax.ShapeDtypeStruct((1, nt_pad), I32), jax.ShapeDtypeStruct((1, nt_pad), I32),
                   jax.ShapeDtypeStruct((1, LANES), I32)],
        compiler_params=_cparams("arbitrary"),
        name="moe_plan",
    )(idx, rank, counts)


def _sc_mesh_info():
    info = plsc.get_sparse_core_info()
    mesh = plsc.VectorSubcoreMesh(core_axis_name="c", subcore_axis_name="s")
    return mesh, info.num_cores, info.num_cores * info.num_subcores


def _sc_dispatch(pieces, pos, *, n_rows):
    t = pieces[0].shape[0]
    mesh, n_cores, n_workers = _sc_mesh_info()
    per_w = t // SC_CHUNK // n_workers

    @functools.partial(
        pl.kernel, mesh=mesh,
        out_type=[jax.ShapeDtypeStruct((n_rows, LANES), I32)] * N_PIECES,
        scratch_types=[pltpu.VMEM((TOP_K, SC_CHUNK), I32),
                       pltpu.VMEM((N_PIECES, SC_CHUNK, LANES), I32),
                       pltpu.SemaphoreType.DMA((N_PIECES,)),
                       pltpu.SemaphoreType.DMA],
        name="sc_dispatch",
    )
    def run(*refs):
        src = refs[:N_PIECES]
        pos_hbm = refs[N_PIECES]
        dst = refs[N_PIECES + 1:2 * N_PIECES + 1]
        idx_v, rows_v, load_sem, put_sem = refs[2 * N_PIECES + 1:]
        wid = lax.axis_index("s") * n_cores + lax.axis_index("c")

        @pl.loop(0, per_w)
        def _(j):
            ch = wid * per_w + j
            t0 = pl.multiple_of(ch * SC_CHUNK, SC_CHUNK)
            loads = [pltpu.make_async_copy(src[c].at[pl.ds(t0, SC_CHUNK)], rows_v.at[c], load_sem.at[c])
                     for c in range(N_PIECES)]
            for ld in loads:
                ld.start()
            pltpu.sync_copy(pos_hbm.at[ch], idx_v)
            puts = []
            for c in range(N_PIECES):
                loads[c].wait()
                for k in range(TOP_K):
                    puts.append(pltpu.make_async_copy(rows_v.at[c], dst[c].at[idx_v.at[k]], put_sem))
                    puts[-1].start()
            for cp in puts:
                cp.wait()

    return run(*pieces, pos)


SC_GROUP = 32


def _sc_combine(pieces, pos, wts, *, n_tokens):
    mesh, n_cores, n_workers = _sc_mesh_info()
    lanes = plsc.get_sparse_core_info().num_lanes
    per_w = n_tokens // SC_GROUP // n_workers

    @functools.partial(
        pl.kernel, mesh=mesh,
        out_type=[jax.ShapeDtypeStruct((n_tokens, LANES), I32)] * N_PIECES,
        scratch_types=[pltpu.VMEM((TOP_K, SC_GROUP), I32),
                       pltpu.VMEM((TOP_K, SC_GROUP), F32),
                       pltpu.VMEM((2, TOP_K, SC_GROUP, LANES), I32),
                       pltpu.VMEM((2, SC_GROUP, LANES), I32),
                       pltpu.SemaphoreType.DMA((2,)),
                       pltpu.SemaphoreType.DMA((2,))],
        compiler_params=pltpu.CompilerParams(needs_layout_passes=False),
        name="sc_combine",
    )
    def run(*refs):
        src = refs[:N_PIECES]
        pos_hbm, wts_hbm = refs[N_PIECES:N_PIECES + 2]
        dst = refs[N_PIECES + 2:2 * N_PIECES + 2]
        idx_v, w_v, buf, acc, get_sem, put_sem = refs[2 * N_PIECES + 2:]
        wid = lax.axis_index("s") * n_cores + lax.axis_index("c")

        @pl.loop(0, per_w)
        def _(j):
            grp = wid * per_w + j
            t0 = pl.multiple_of(grp * SC_GROUP, SC_GROUP)
            pltpu.sync_copy(pos_hbm.at[grp], idx_v)
            pltpu.sync_copy(wts_hbm.at[grp], w_v)

            def gets(c, slot):
                return [pltpu.make_async_copy(src[c].at[idx_v.at[k]], buf.at[slot, k], get_sem.at[slot])
                        for k in range(TOP_K)]

            def puts(c, slot):
                return [pltpu.make_async_copy(acc.at[slot], dst[c].at[pl.ds(t0, SC_GROUP)], put_sem.at[slot])]

            for cp in gets(0, 0):
                cp.start()
            for c in range(N_PIECES):
                slot = c % 2
                if c + 1 < N_PIECES:
                    for cp in gets(c + 1, 1 - slot):
                        cp.start()
                for cp in gets(c, slot):
                    cp.wait()
                if c >= 2:
                    for cp in puts(c - 2, slot):
                        cp.wait()

                @pl.loop(0, SC_GROUP)
                def _(r):
                    row = jnp.full((lanes,), r, I32)
                    w = []
                    for k in range(TOP_K):
                        w_k = plsc.load_gather(w_v, [jnp.full((lanes,), k, I32), row])
                        w.append(plsc.pack(w_k, w_k, format=plsc.PackFormat.INTERLEAVED))
                    for q in range(LANES // lanes):
                        cols = pl.ds(q * lanes, lanes)
                        total = None
                        for k in range(TOP_K):
                            term = plsc.bitcast(buf[slot, k, r, cols], BF16) * w[k]
                            total = term if total is None else total + term
                        acc[slot, r, cols] = plsc.bitcast(total, I32)

                for cp in puts(c, slot):
                    cp.start()
            for c in range(N_PIECES - 2, N_PIECES):
                for cp in puts(c, c % 2):
                    cp.wait()

    return run(*pieces, pos, wts)


def _group_tile(n_tokens):
    per_expert = n_tokens * TOP_K // N_EXPERTS
    return max(MXU_DIM, min(4 * MXU_DIM, per_expert // MXU_DIM * MXU_DIM))


SECOND_DMA_QUEUE = 1


def _experts_kernel(te_ref, nv_ref, nu_ref, *refs, tile):
    x_hbm = refs[:N_PIECES]
    wg_ref, wu_ref, wd_ref = refs[N_PIECES:N_PIECES + 3]
    y_hbm = refs[N_PIECES + 3:2 * N_PIECES + 3]
    xbuf, xsem, ybuf, ysem = refs[2 * N_PIECES + 3:]
    i = pl.program_id(0)
    n_used = nu_ref[0]
    slot = lax.rem(i, 2)

    half = tile // 2

    def x_copy(step, into, h, c):
        r = pl.ds(pl.multiple_of(step * tile + h * half, half), half)
        return pltpu.make_async_copy(x_hbm[c].at[r], xbuf.at[into, c, pl.ds(h * half, half)], xsem.at[into, c])

    def y_copy(step, out_of, h, c):
        r = pl.ds(pl.multiple_of(step * tile + h * half, half), half)
        return pltpu.make_async_copy(ybuf.at[out_of, c, pl.ds(h * half, half)], y_hbm[c].at[r], ysem.at[out_of, c])

    def real_halves(copy, step, buf, act):
        for c in range(N_PIECES):
            act(copy(step, buf, 0, c))

        @pl.when(nv_ref[step] > half)
        def _():
            for c in range(N_PIECES):
                act(copy(step, buf, 1, c))

    start = lambda cp: cp.start(priority=SECOND_DMA_QUEUE)
    wait = lambda cp: cp.wait()

    @pl.when(i == 0)
    def _():
        real_halves(x_copy, 0, 0, start)

    @pl.when(i + 1 < n_used)
    def _():
        real_halves(x_copy, i + 1, 1 - slot, start)

    @pl.when(i < n_used)
    def _():
        real_halves(x_copy, i, slot, wait)
        subs = [slice(s * MXU_DIM, (s + 1) * MXU_DIM) for s in range(tile // MXU_DIM)]
        xs = []
        for rows in subs:
            lo, hi = _unpack_rows([xbuf[slot, c, rows, :] for c in range(N_PIECES)])
            xs.append(jnp.concatenate(lo + hi, axis=1).astype(BF16))
        gates = [(_dot(x, wg_ref[...]), _dot(x, wu_ref[...])) for x in xs]
        ys = [_dot((_silu(g) * u).astype(BF16), wd_ref[...]) for g, u in gates]
        for rows, y in zip(subs, ys):
            for c, piece in enumerate(_pack_rows(y)):
                ybuf[slot, c, rows, :] = piece

        @pl.when(i >= 1)
        def _():
            real_halves(y_copy, i - 1, 1 - slot, wait)

        real_halves(y_copy, i, slot, start)

        @pl.when(i == n_used - 1)
        def _():
            real_halves(y_copy, i, slot, wait)


def _experts(x_pieces, tile_expert, tile_rows, n_used, wg, wu, wd, *, group_tile):
    n_rows = x_pieces[0].shape[0]
    n_tiles = n_rows // group_tile

    wspec = lambda a: pl.BlockSpec((None,) + a.shape[1:],
                                   lambda i, te, nv, nu: (te[jnp.minimum(i, nu[0] - 1)], 0, 0))
    return pl.pallas_call(
        functools.partial(_experts_kernel, tile=group_tile),
        grid_spec=pltpu.PrefetchScalarGridSpec(
            num_scalar_prefetch=3,
            grid=(n_tiles,),
            in_specs=[pl.BlockSpec(memory_space=pl.ANY)] * N_PIECES + [wspec(wg), wspec(wu), wspec(wd)],
            out_specs=[pl.BlockSpec(memory_space=pl.ANY)] * N_PIECES,
            scratch_shapes=[pltpu.VMEM((2, N_PIECES, group_tile, LANES), I32),
                            pltpu.SemaphoreType.DMA((2, N_PIECES))] * 2),
        out_shape=[jax.ShapeDtypeStruct((n_rows, LANES), I32)] * N_PIECES,
        compiler_params=_cparams("arbitrary"),
        name="experts",
    )(tile_expert, tile_rows, n_used, *x_pieces, wg, wu, wd)


def _moe_out_kernel(x1_ref, mod_ref, fg_ref, sg_ref, su_ref, sd_ref, *refs):
    h_refs = refs[:N_PIECES]
    routed_refs = refs[N_PIECES:2 * N_PIECES]
    out_ref = refs[2 * N_PIECES]
    h_lo, h_hi = _unpack_rows([r[...] for r in h_refs])
    h = jnp.concatenate(h_lo + h_hi, axis=1).astype(BF16)
    hid = _silu(_dot(h, sg_ref[...])) * _dot(h, su_ref[...])
    shared = _dot(hid.astype(BF16), sd_ref[...])
    r_lo, r_hi = _unpack_rows([r[...] for r in routed_refs])
    routed = jnp.concatenate(r_lo + r_hi, axis=1)
    x2 = x1_ref[...] + mod_ref[0, 5:6, :] * (shared + routed)
    out_ref[...] = x2 * lax.rsqrt(jnp.mean(x2 * x2, axis=-1, keepdims=True) + EPS) * fg_ref[...]


def _moe_out(h2_pieces, x1, mods, final_g, sg, su, sd, routed_pieces, *, tokens_per_mod, tm):
    t = x1.shape[0]
    tiles_per_mod = tokens_per_mod // tm
    row = lambda w: pl.BlockSpec((tm, w), lambda i: (i, 0))
    full = lambda a: pl.BlockSpec(a.shape, lambda i: (0,) * a.ndim)
    return pl.pallas_call(
        _moe_out_kernel,
        grid=(t // tm,),
        in_specs=[row(D_MODEL),
                  pl.BlockSpec((1, 6, D_MODEL), lambda i: (i // tiles_per_mod, 0, 0)),
                  full(final_g), full(sg), full(su), full(sd)]
        + [row(LANES)] * (2 * N_PIECES),
        out_specs=row(D_MODEL),
        out_shape=jax.ShapeDtypeStruct((t, D_MODEL), F32),
        compiler_params=_cparams("parallel"),
        name="moe_out",
    )(x1, mods, final_g, sg, su, sd, *h2_pieces, *routed_pieces)


def _trunk(x, mods, s0, w, expert_w, *, batch, seq_len, on_grid):
    t = batch * seq_len
    tokens_per_mod = t // mods.shape[0]
    cos_t, sin_t = _rope_tables(max(seq_len, PROJ_TILE))
    to_cast = expert_w if expert_w[0].dtype != BF16 else ()
    (q, k, v, gsw, up, ga, gb), casted = _inproj(x, mods, w["norm1_g"], w["w_in"], cos_t, sin_t, to_cast,
                                                 tokens_per_mod=tokens_per_mod, seq_len=seq_len,
                                                 on_grid=on_grid, tm=PROJ_TILE)
    expert_w = casted or expert_w
    z, s_f, s_b = _retention(q, k, v, gsw, w["dec"], s0, batch=batch, seq_len=seq_len)
    p = _pool(up, w["pool_w"], w["pool_scale"], batch=batch, seq_len=seq_len, on_grid=on_grid)
    x1, h2_pieces = _merge(x, z, p, ga, gb, mods, w["norm2_g"], w["w_br_ret"], w["w_br_pool"],
                               w["w_out"], tokens_per_mod=tokens_per_mod, tm=PROJ_TILE)

    group_tile = _group_tile(t)
    n_rows = t * TOP_K + N_EXPERTS * group_tile
    idx, rank, wts, counts = _route(h2_pieces, w["router_wt"], w["router_bias"], tm=ROUTE_TILE)
    pos, tile_expert, tile_rows, n_used = _plan(idx, rank, counts, n_tiles=n_rows // group_tile, tf=PLAN_TILE,
                                                group_tile=group_tile)
    x_sorted = _sc_dispatch(h2_pieces, pos, n_rows=n_rows)
    y_sorted = _experts(x_sorted, tile_expert.reshape(-1), tile_rows.reshape(-1), n_used.reshape(-1),
                        *expert_w, group_tile=group_tile)
    regroup = lambda a: a.reshape(TOP_K, t // SC_GROUP, SC_GROUP).transpose(1, 0, 2)
    pos_rows = pos.transpose(1, 0, 2).reshape(TOP_K, t)
    routed = _sc_combine(y_sorted, regroup(pos_rows), regroup(wts), n_tokens=t)
    y = _moe_out(h2_pieces, x1, mods, w["final_g"], w["sh_w_gate"], w["sh_w_up"], w["sh_w_down"], routed,
                 tokens_per_mod=tokens_per_mod, tm=OUT_TILE)
    return y, s_f, s_b, expert_w


def kernel(x_prompt, x_sample, state_ret_fwd, state_ret_bwd, c, c_ctx, ada_w, ada_b, norm1_g, norm2_g, w_in,
           ret_decay_fwd, ret_decay_bwd, w_br_ret, pool_w, pool_scale, w_br_pool, w_out, router_w, router_bias,
           exp_w_gate, exp_w_up, exp_w_down, sh_w_gate, sh_w_up, sh_w_down, final_norm_g):
    n_req, seq, d = x_prompt.shape
    n_dec, dec_seq, _ = x_sample.shape
    depth = ada_w.shape[0]
    assert depth == 1 and d == D_MODEL

    xc = x_prompt.reshape(n_req * seq, d)
    xs = x_sample.reshape(n_dec * dec_seq, d)
    new_f, new_b = [], []
    for l in range(depth):
        c_rows = jnp.concatenate([c_ctx[None, :], c, jnp.zeros((8 - 1 - n_dec, d), F32)], axis=0)
        mods = _ada(c_rows, ada_w[l], ada_b[l]).reshape(8, 6, d)
        pad_rows = LANES - N_EXPERTS
        w = dict(
            norm1_g=norm1_g[l].reshape(1, d), norm2_g=norm2_g[l].reshape(1, d),
            final_g=final_norm_g.reshape(1, d),
            w_in=w_in[l].astype(BF16),
            dec=jnp.stack([ret_decay_fwd[l], ret_decay_bwd[l]]).astype(F32),
            w_br_ret=w_br_ret[l].astype(BF16), pool_w=pool_w[l].astype(BF16),
            pool_scale=pool_scale[l].reshape(1, POOL_W), w_br_pool=w_br_pool[l].astype(BF16),
            w_out=w_out[l].astype(BF16),
            router_wt=jnp.pad(router_w[l].T, ((0, pad_rows), (0, 0))).astype(BF16),
            router_bias=jnp.pad(router_bias[l].astype(F32).reshape(N_EXPERTS, 1), ((0, pad_rows), (0, 0))),
            sh_w_gate=sh_w_gate[l].astype(BF16),
            sh_w_up=sh_w_up[l].astype(BF16), sh_w_down=sh_w_down[l].astype(BF16),
        )
        cached = (state_ret_fwd[:, l].astype(F32), state_ret_bwd[:, l].astype(F32))
        expert_w = (exp_w_gate[l], exp_w_up[l], exp_w_down[l])
        xs, _, _, expert_w = _trunk(xs, mods[1:1 + n_dec], cached, w, expert_w,
                                    batch=n_dec, seq_len=dec_seq, on_grid=True)
        xc, s_f, s_b, _ = _trunk(xc, mods[0:1], None, w, expert_w, batch=n_req, seq_len=seq, on_grid=False)
        new_f.append(s_f)
        new_b.append(s_b)
    y_prompt = xc.reshape(n_req, seq, d)
    y_sample = xs.reshape(n_dec, dec_seq, d)
    return (y_prompt, y_sample, jnp.stack(new_f, axis=1).astype(x_prompt.dtype),
            jnp.stack(new_b, axis=1).astype(x_prompt.dtype))
```

```python
import functools
import math

import numpy as np
import jax
import jax.numpy as jnp
from jax import lax
from jax.experimental import pallas as pl
from jax.experimental.pallas import tpu as pltpu
from jax.experimental.pallas import tpu_sc as plsc

D_MODEL = 1024
GRID_W = 64
RET_HEADS = 4
RET_DK = 128
RET_DV = 256
RET_QK_W = RET_HEADS * RET_DK
RET_V_W = RET_HEADS * RET_DV
RET_CHUNK = 128
ROPE_BASE = 10000.0
POOL_GROUPS = 4
POOL_CH = 128
POOL_W = POOL_GROUPS * POOL_CH
POOL_WINDOWS = (2, 4, 8, 16)
N_EXPERTS = 64
TOP_K = 8
N_EXPERT_GROUPS = 8
GROUP_SIZE = N_EXPERTS // N_EXPERT_GROUPS
TOPK_GROUPS = 4
D_EXPERT = 256
ROUTED_SCALE = 2.5
EPS = 1e-6
IN_SIZES = (RET_QK_W, RET_QK_W, RET_V_W, RET_V_W, POOL_W, D_MODEL, D_MODEL)
IN_OFFS = tuple(sum(IN_SIZES[:i]) for i in range(len(IN_SIZES) + 1))
IN_W = IN_OFFS[-1]

LANES = 128
VMEM_LIMIT = 56 << 20
N_PIECES = D_MODEL // 2 // LANES
MXU_DIM = 256
SC_CHUNK = 128
PROJ_TILE = 512
ROUTE_TILE = 1024
PLAN_TILE = 2048
OUT_TILE = 1024

F32 = jnp.float32
BF16 = jnp.bfloat16
I32 = jnp.int32
U32 = jnp.uint32


def _cparams(*sem):
    return pltpu.CompilerParams(dimension_semantics=sem, vmem_limit_bytes=VMEM_LIMIT)


def _dot(a, b):
    return jnp.dot(a, b, preferred_element_type=F32)


def _silu(x):
    return x * jax.nn.sigmoid(x)


def _rms_mod(x, g, scale, shift):
    y = x * lax.rsqrt(jnp.mean(x * x, axis=-1, keepdims=True) + EPS)
    return (y * g) * (1.0 + scale) + shift


def _ada_kernel(c_ref, w_ref, b_ref, o_ref):
    c = c_ref[...]
    o_ref[...] = jnp.dot(_silu(c), w_ref[...], preferred_element_type=F32,
                         precision=lax.Precision.HIGHEST) + b_ref[...]


def _ada(c_rows, ada_w, ada_b):
    r = c_rows.shape[0]
    n = ada_w.shape[1]
    tn = 2 * D_MODEL
    return pl.pallas_call(
        _ada_kernel,
        grid=(n // tn,),
        in_specs=[pl.BlockSpec((r, D_MODEL), lambda j: (0, 0)),
                  pl.BlockSpec((D_MODEL, tn), lambda j: (0, j)),
                  pl.BlockSpec((1, tn), lambda j: (0, j))],
        out_specs=pl.BlockSpec((r, tn), lambda j: (0, j)),
        out_shape=jax.ShapeDtypeStruct((r, n), F32),
        compiler_params=_cparams("parallel"),
        name="ada_mod",
    )(c_rows, ada_w, ada_b.reshape(1, n))


def _inproj_kernel(x_ref, mod_ref, g_ref, w_ref, cos_ref, sin_ref, *refs, on_grid):
    n_side = (len(refs) - len(IN_SIZES)) // 2
    side_in = refs[:n_side]
    q_ref, k_ref, v_ref, gsw_ref, up_ref, ga_ref, gb_ref = refs[n_side:n_side + len(IN_SIZES)]
    side_out = refs[n_side + len(IN_SIZES):]
    for src, dst in zip(side_in, side_out):
        dst[...] = src[...].astype(BF16)
    for s in range(x_ref.shape[0] // MXU_DIM):
        rows = slice(s * MXU_DIM, (s + 1) * MXU_DIM)
        h = _rms_mod(x_ref[rows, :], g_ref[...], mod_ref[0, 1:2, :], mod_ref[0, 0:1, :]).astype(BF16)

        def seg(i):
            return _dot(h, w_ref[:, IN_OFFS[i]:IN_OFFS[i + 1]])

        q = seg(0)
        k = seg(1)
        if on_grid:
            cos = jnp.concatenate([cos_ref[rows, :]] * RET_HEADS, axis=1)
            sin = jnp.concatenate([sin_ref[rows, :]] * RET_HEADS, axis=1)
            lane = lax.broadcasted_iota(jnp.int32, q.shape, 1)
            first = (lane & 63) < 32

            def rope(a):
                up = pltpu.roll(a, RET_QK_W - 32, axis=1)
                dn = pltpu.roll(a, 32, axis=1)
                return a * cos + jnp.where(first, up, dn) * sin

            q = rope(q)
            k = rope(k)
        q_ref[rows, :] = q.astype(BF16)
        k_ref[rows, :] = (k * (RET_DK ** -0.5)).astype(BF16)
        v_ref[rows, :] = seg(2).astype(BF16)
        gsw_ref[rows, :] = seg(3).astype(BF16)
        up_ref[rows, :] = seg(4).astype(BF16)
        ga_ref[rows, :] = seg(5).astype(BF16)
        gb_ref[rows, :] = seg(6).astype(BF16)


def _inproj(x, mods, norm_g, w_in, cos_t, sin_t, side_cast=(), *, tokens_per_mod, seq_len, on_grid, tm):
    t = x.shape[0]
    steps = t // tm
    tiles_per_mod = tokens_per_mod // tm
    tiles_per_seq = max(seq_len // tm, 1)
    widths = IN_SIZES
    out_shape = [jax.ShapeDtypeStruct((t, w), BF16) for w in widths]
    out_specs = [pl.BlockSpec((tm, w), lambda i: (i, 0)) for w in widths]
    side_specs = [pl.BlockSpec((a.shape[0] // steps,) + a.shape[1:], lambda i: (i, 0, 0)) for a in side_cast]
    outs = pl.pallas_call(
        functools.partial(_inproj_kernel, on_grid=on_grid),
        grid=(steps,),
        in_specs=[pl.BlockSpec((tm, D_MODEL), lambda i: (i, 0)),
                  pl.BlockSpec((1, 6, D_MODEL), lambda i: (i // tiles_per_mod, 0, 0)),
                  pl.BlockSpec((1, D_MODEL), lambda i: (0, 0)),
                  pl.BlockSpec((D_MODEL, IN_W), lambda i: (0, 0), pipeline_mode=pl.Buffered(1)),
                  pl.BlockSpec((tm, RET_DK), lambda i: (i % tiles_per_seq, 0)),
                  pl.BlockSpec((tm, RET_DK), lambda i: (i % tiles_per_seq, 0))] + side_specs,
        out_specs=out_specs + side_specs,
        out_shape=out_shape + [jax.ShapeDtypeStruct(a.shape, BF16) for a in side_cast],
        compiler_params=_cparams("parallel"),
        name="inproj_grid" if on_grid else "inproj_seq",
    )(x, mods, norm_g, w_in, cos_t, sin_t, *side_cast)
    return outs[:len(widths)], tuple(outs[len(widths):])


def _rope_tables(seq_len):
    t = np.arange(seq_len)
    row = (t // GRID_W).astype(np.float32)
    col = (t % GRID_W).astype(np.float32)
    m = RET_DK // 4
    inv = (np.float32(ROPE_BASE) ** (-np.arange(m, dtype=np.float32) / np.float32(m))).astype(np.float32)
    ar = row[:, None] * inv
    ac = col[:, None] * inv
    cos = np.concatenate([np.cos(ar), np.cos(ar), np.cos(ac), np.cos(ac)], axis=1)
    sin = np.concatenate([-np.sin(ar), np.sin(ar), -np.sin(ac), np.sin(ac)], axis=1)
    return jnp.asarray(cos, F32), jnp.asarray(sin, F32)


def _ret_heads_per_step(seq_len):
    per_head = seq_len * (2 * 2 * (2 * RET_DK + 3 * RET_DV) + 4 * RET_DV + 2 * RET_DK)
    heads = RET_HEADS
    while heads > 1 and heads * per_head > VMEM_LIMIT * 3 // 4:
        heads //= 2
    return heads


def _ret_kernel(dec_ref, q_ref, k_ref, v_ref, g_ref, *refs, n_chunks, heads, zero_init):
    s0_refs = () if zero_init else refs[:2]
    z_ref, sf_ref, sb_ref, oacc_ref, kt_ref = refs[len(s0_refs):]
    c = RET_CHUNK
    half = n_chunks // 2
    ii = lax.broadcasted_iota(I32, (c, c), 0)
    jj = lax.broadcasted_iota(I32, (c, c), 1)
    ik = lax.broadcasted_iota(I32, (c, RET_DK), 0).astype(F32)
    jk = lax.broadcasted_iota(I32, (RET_DK, c), 1).astype(F32)

    def log_gamma(d, shape):
        return jnp.log1p(-jnp.exp2(-jnp.full(shape, d, F32)))

    consts = {}
    for hh in range(heads):
        h = pl.program_id(1) * heads + hh
        dec_f = dec_ref[0, h]
        dec_b = dec_ref[1, h]
        rel = (ii - jj).astype(F32)
        consts[hh, "f"] = (
            jnp.where(rel >= 0, jnp.exp(log_gamma(dec_f, (c, c)) * jnp.maximum(rel, 0.0)), 0.0),
            jnp.exp(log_gamma(dec_f, (c, RET_DK)) * (ik + 1.0)),
            jnp.exp(log_gamma(dec_f, (RET_DK, c)) * (c - 1.0 - jk)),
            jnp.exp(log_gamma(dec_f, (RET_DK, RET_DV)) * c))
        consts[hh, "b"] = (
            jnp.where(rel <= 0, jnp.exp(log_gamma(dec_b, (c, c)) * jnp.maximum(-rel, 0.0)), 0.0),
            jnp.exp(log_gamma(dec_b, (c, RET_DK)) * (c - ik)),
            jnp.exp(log_gamma(dec_b, (RET_DK, c)) * jk),
            jnp.exp(log_gamma(dec_b, (RET_DK, RET_DV)) * c))

    for s_ref, s0_ref in zip((sf_ref, sb_ref), s0_refs or (None, None)):
        s_ref[...] = jnp.zeros(s_ref.shape, F32) if zero_init else s0_ref[...]

    def scores(ci, hh, direction, second):
        r = pl.ds(pl.multiple_of(ci * c, c), c)
        kcols = slice(hh * RET_DK, (hh + 1) * RET_DK)
        qc = q_ref[r, kcols]
        kc = k_ref[r, kcols]
        if not second:
            kt_ref[hh, ci] = kc.T
        sc = lax.dot_general(qc, kc, (((1,), (1,)), ((), ())), preferred_element_type=F32)
        return ci, hh, direction, r, qc, sc

    def advance(job):
        ci, hh, direction, r, qc, sc = job
        dmask, qdec, kdec, cdec = consts[hh, direction]
        s_ref = sf_ref if direction == "f" else sb_ref
        vc = v_ref[r, hh * RET_DV:(hh + 1) * RET_DV]
        s = s_ref[hh]
        lhs = jnp.concatenate([(sc * dmask).astype(BF16), (qc.astype(F32) * qdec).astype(BF16)], axis=1)
        o = _dot(lhs, jnp.concatenate([vc, s.astype(BF16)], axis=0))
        kd_t = (kt_ref[hh, ci].astype(F32) * kdec).astype(BF16)
        s_ref[hh] = s * cdec + _dot(kd_t, vc)
        return o

    def emit(job, o, second):
        _, hh, _, r, _, _ = job
        vcols = slice(hh * RET_DV, (hh + 1) * RET_DV)
        if not second:
            oacc_ref[hh, r, :] = o
        else:
            o = o + oacc_ref[hh, r, :]
            o = o * lax.rsqrt(jnp.mean(o * o, axis=-1, keepdims=True) + EPS)
            g = g_ref[r, vcols].astype(F32)
            z_ref[r, vcols] = (_silu(g) * o).astype(BF16)

    def body(second):
        def run(t, carry):
            jobs = [scores(ci, hh, d, second) for hh in range(heads)
                    for ci, d in ((t, "f"), (n_chunks - 1 - t, "b"))]
            outs = [advance(job) for job in jobs]
            for job, o in zip(jobs, outs):
                emit(job, o, second)
            return carry
        return run

    lax.fori_loop(0, half, body(False), 0, unroll=4 if half % 4 == 0 else 1)
    lax.fori_loop(half, n_chunks, body(True), 0, unroll=2 if half % 2 == 0 else 1)


def _retention(q, k, v, gsw, dec, s0, *, batch, seq_len):
    n_chunks = seq_len // RET_CHUNK
    assert n_chunks % 2 == 0
    heads = _ret_heads_per_step(seq_len)
    t = batch * seq_len
    st_spec = pl.BlockSpec((None, heads, RET_DK, RET_DV), lambda b, h: (b, h, 0, 0))
    st_shape = jax.ShapeDtypeStruct((batch, RET_HEADS, RET_DK, RET_DV), F32)
    kspec = pl.BlockSpec((seq_len, heads * RET_DK), lambda b, h: (b, h))
    vspec = pl.BlockSpec((seq_len, heads * RET_DV), lambda b, h: (b, h))
    return pl.pallas_call(
        functools.partial(_ret_kernel, n_chunks=n_chunks, heads=heads, zero_init=s0 is None),
        grid=(batch, RET_HEADS // heads),
        in_specs=[pl.BlockSpec(memory_space=pltpu.SMEM), kspec, kspec, vspec, vspec]
        + ([] if s0 is None else [st_spec, st_spec]),
        out_specs=[vspec, st_spec, st_spec],
        out_shape=[jax.ShapeDtypeStruct((t, RET_V_W), BF16), st_shape, st_shape],
        scratch_shapes=[pltpu.VMEM((heads, seq_len, RET_DV), F32),
                        pltpu.VMEM((heads, n_chunks, RET_DK, RET_CHUNK), BF16)],
        compiler_params=_cparams("parallel", "parallel"),
        name=f"retention_l{seq_len}",
    )(dec, q, k, v, gsw, *(s0 or ()))


def _pool_kernel(u_ref, w_ref, sc_ref, o_ref, *, n_tok, width, two_d):
    n_rows = n_tok // width
    pos = lax.broadcasted_iota(I32, (width, POOL_CH), 0)

    def every_row(a):
        return jnp.concatenate([a] * n_rows, axis=0) if n_rows > 1 else a

    def shift_in_row(a, s):
        ok = (pos < width - s) if s > 0 else (pos >= -s)
        return pltpu.roll(a, (-s) % n_tok, axis=0) * every_row(jnp.where(ok, 1.0, 0.0))

    def shift_rows(a, m):
        k = abs(m) * width
        zeros = jnp.zeros((k, POOL_CH), F32)
        return (jnp.concatenate([a[k:], zeros], axis=0) if m > 0
                else jnp.concatenate([zeros, a[:n_tok - k]], axis=0))

    def box_sum(a, half, shift):
        fw = a
        bw = shift(a, -1)
        m = 1
        while m < half:
            fw = fw + shift(fw, m)
            bw = bw + shift(bw, -m)
            m *= 2
        return fw + bw

    def inv_count(p, half, extent):
        return 1.0 / (jnp.minimum(p + half, extent) - jnp.maximum(p - half, 0)).astype(F32)

    for g, window in enumerate(POOL_WINDOWS):
        half = window // 2
        cols = slice(g * POOL_CH, (g + 1) * POOL_CH)
        ug = u_ref[:, cols].astype(F32)
        total = box_sum(ug, half, shift_in_row)
        inv = every_row(inv_count(pos, half, width))
        if two_d:
            total = box_sum(total, half, shift_rows)
            row = lax.broadcasted_iota(I32, (n_rows, 1, POOL_CH), 0)
            inv_r = jnp.broadcast_to(inv_count(row, half, n_rows), (n_rows, width, POOL_CH))
            inv = inv * inv_r.reshape(n_tok, POOL_CH)
        d = (total * inv - ug).astype(BF16)
        o_ref[:, cols] = (_dot(d, w_ref[g]) * sc_ref[:, cols]).astype(BF16)


def _pool(u, pool_w, pool_scale, *, batch, seq_len, on_grid):
    t = batch * seq_len
    width = GRID_W if on_grid else seq_len
    n_tok = seq_len if on_grid else seq_len * math.gcd(batch, 4)
    return pl.pallas_call(
        functools.partial(_pool_kernel, n_tok=n_tok, width=width, two_d=on_grid),
        grid=(t // n_tok,),
        in_specs=[pl.BlockSpec((n_tok, POOL_W), lambda b: (b, 0)),
                  pl.BlockSpec((POOL_GROUPS, POOL_CH, POOL_CH), lambda b: (0, 0, 0)),
                  pl.BlockSpec((1, POOL_W), lambda b: (0, 0))],
        out_specs=pl.BlockSpec((n_tok, POOL_W), lambda b: (b, 0)),
        out_shape=jax.ShapeDtypeStruct((t, POOL_W), BF16),
        compiler_params=_cparams("parallel"),
        name=f"pool_l{seq_len}",
    )(u, pool_w, pool_scale)


def _pack_rows(x):
    half = D_MODEL // 2
    lo = lax.bitcast_convert_type(x[:, :half].astype(BF16).astype(F32), U32) >> 16
    hi = lax.bitcast_convert_type(x[:, half:].astype(BF16).astype(F32), U32) & jnp.uint32(0xFFFF0000)
    word = lax.bitcast_convert_type(hi | lo, I32)
    return [word[:, c * LANES:(c + 1) * LANES] for c in range(N_PIECES)]


def _unpack_rows(pieces):
    words = [lax.bitcast_convert_type(p, U32) for p in pieces]
    lo = [lax.bitcast_convert_type(w << 16, F32) for w in words]
    hi = [lax.bitcast_convert_type(w & jnp.uint32(0xFFFF0000), F32) for w in words]
    return lo, hi


def _merge_kernel(x_ref, z_ref, p_ref, ga_ref, gb_ref, mod_ref, g2_ref, wr_ref, wp_ref, wo_ref,
                  x1_ref, *piece_refs):
    y_ret = _dot(z_ref[...], wr_ref[...])
    y_pool = _dot(p_ref[...], wp_ref[...])
    merged = (jax.nn.sigmoid(ga_ref[...].astype(F32)) * y_ret
              + jax.nn.sigmoid(gb_ref[...].astype(F32)) * y_pool)
    x1 = x_ref[...] + mod_ref[0, 2:3, :] * _dot(merged.astype(BF16), wo_ref[...])
    x1_ref[...] = x1
    h2 = _rms_mod(x1, g2_ref[...], mod_ref[0, 4:5, :], mod_ref[0, 3:4, :])
    for ref, piece in zip(piece_refs, _pack_rows(h2)):
        ref[...] = piece


def _merge(x, z, p, ga, gb, mods, norm2_g, w_br_ret, w_br_pool, w_out, *, tokens_per_mod, tm):
    t = x.shape[0]
    tiles_per_mod = tokens_per_mod // tm
    row = lambda w: pl.BlockSpec((tm, w), lambda i: (i, 0))
    full = lambda a: pl.BlockSpec(a.shape, lambda i: (0,) * a.ndim)
    outs = pl.pallas_call(
        _merge_kernel,
        grid=(t // tm,),
        in_specs=[row(D_MODEL), row(RET_V_W), row(POOL_W), row(D_MODEL), row(D_MODEL),
                  pl.BlockSpec((1, 6, D_MODEL), lambda i: (i // tiles_per_mod, 0, 0)),
                  full(norm2_g), full(w_br_ret), full(w_br_pool), full(w_out)],
        out_specs=[row(D_MODEL)] + [row(LANES)] * N_PIECES,
        out_shape=[jax.ShapeDtypeStruct((t, D_MODEL), F32)] + [jax.ShapeDtypeStruct((t, LANES), I32)] * N_PIECES,
        compiler_params=_cparams("parallel"),
        name="merge",
    )(x, z, p, ga, gb, mods, norm2_g, w_br_ret, w_br_pool, w_out)
    return outs[0], outs[1:]


def _route_kernel(*refs):
    h_refs = refs[:N_PIECES]
    rw_ref, bias_ref, idx_ref, rank_ref, wk_ref, cnt_ref, carry_ref = refs[N_PIECES:]
    e = N_EXPERTS
    tm = h_refs[0].shape[0]
    neg = -jnp.inf

    @pl.when(pl.program_id(0) == 0)
    def _():
        carry_ref[...] = jnp.zeros(carry_ref.shape, F32)

    lo, hi = _unpack_rows([r[...] for r in h_refs])
    h = jnp.concatenate(lo + hi, axis=1).astype(BF16)
    logits = lax.dot_general(rw_ref[...], h, (((1,), (1,)), ((), ())), preferred_element_type=F32)[:e]
    scores = jax.nn.sigmoid(logits)
    sel = scores + bias_ref[:e, 0:1]
    e_idx = lax.broadcasted_iota(I32, (e, tm), 0)

    grp = sel.reshape(N_EXPERT_GROUPS, GROUP_SIZE, tm)
    m_idx = lax.broadcasted_iota(I32, grp.shape, 1)
    m1 = jnp.max(grp, axis=1, keepdims=True)
    first = jnp.min(jnp.where(grp == m1, m_idx, GROUP_SIZE), axis=1, keepdims=True)
    m2 = jnp.max(jnp.where(m_idx == first, neg, grp), axis=1, keepdims=True)
    gscore = (m1 + m2).reshape(N_EXPERT_GROUPS, tm)

    g_idx = lax.broadcasted_iota(I32, gscore.shape, 0)
    grank = jnp.zeros(gscore.shape, I32)
    for g in range(N_EXPERT_GROUPS):
        other = gscore[g:g + 1, :]
        beats = jnp.where(other > gscore, 1, jnp.where(other == gscore, (g_idx > g).astype(I32), 0))
        grank = grank + beats
    gkeep = (grank < TOPK_GROUPS).astype(F32)
    ekeep = jnp.broadcast_to(gkeep.reshape(N_EXPERT_GROUPS, 1, tm), grp.shape).reshape(e, tm)
    masked = jnp.where(ekeep > 0, sel, neg)

    chosen = jnp.zeros((e, tm), F32)
    picks, hits = [], []
    for _ in range(TOP_K):
        m = jnp.max(masked, axis=0, keepdims=True)
        pick = jnp.min(jnp.where(masked == m, e_idx, e), axis=0, keepdims=True)
        hit = e_idx == pick
        chosen = jnp.where(hit, 1.0, chosen)
        masked = jnp.where(hit, neg, masked)
        picks.append(pick)
        hits.append(hit)

    w = scores * chosen
    comb = w / jnp.sum(w, axis=0, keepdims=True) * ROUTED_SCALE

    t_row = lax.broadcasted_iota(I32, (tm, tm), 0)
    t_col = lax.broadcasted_iota(I32, (tm, tm), 1)
    before = (t_row < t_col).astype(BF16)
    rankmat = _dot(chosen.astype(BF16), before) + carry_ref[:e, 0:1]
    carry_ref[:e, :] = carry_ref[:e, :] + jnp.sum(chosen, axis=1, keepdims=True)
    cnt_ref[...] = carry_ref[...]

    idx_ref[...] = jnp.concatenate(picks, axis=0)
    rank_ref[...] = jnp.concatenate(
        [jnp.sum(jnp.where(h, rankmat, 0.0), axis=0, keepdims=True) for h in hits], axis=0).astype(I32)
    wk_ref[...] = jnp.concatenate(
        [jnp.sum(jnp.where(h, comb, 0.0), axis=0, keepdims=True) for h in hits], axis=0)


def _route(h2_pieces, router_wt, bias_col, *, tm):
    t = h2_pieces[0].shape[0]
    krow = pl.BlockSpec((TOP_K, tm), lambda i: (0, i))
    return pl.pallas_call(
        _route_kernel,
        grid=(t // tm,),
        in_specs=[pl.BlockSpec((tm, LANES), lambda i: (i, 0))] * N_PIECES
        + [pl.BlockSpec((LANES, D_MODEL), lambda i: (0, 0)), pl.BlockSpec((LANES, 1), lambda i: (0, 0))],
        out_specs=[krow, krow, krow, pl.BlockSpec((LANES, LANES), lambda i: (0, 0))],
        out_shape=[jax.ShapeDtypeStruct((TOP_K, t), I32), jax.ShapeDtypeStruct((TOP_K, t), I32),
                   jax.ShapeDtypeStruct((TOP_K, t), F32), jax.ShapeDtypeStruct((LANES, LANES), F32)],
        scratch_shapes=[pltpu.VMEM((LANES, LANES), F32)],
        compiler_params=_cparams("arbitrary"),
        name="route",
    )(*h2_pieces, router_wt, bias_col)


def _plan_kernel(idx_ref, rank_ref, cnt_ref, pos_ref, te_ref, nv_ref, nu_ref, *, group_tile):
    tf = idx_ref.shape[1]
    nt = te_ref.shape[1]
    cnt = cnt_ref[...].astype(I32)
    padded = (((cnt + (group_tile - 1)) // group_tile) * group_tile).astype(F32)
    e_sub = lax.broadcasted_iota(I32, (LANES, LANES), 0)
    e_lane = lax.broadcasted_iota(I32, (LANES, LANES), 1)
    base = jnp.sum(jnp.where(e_lane < e_sub, padded.T, 0.0), axis=1, keepdims=True)
    end = base + padded[:, 0:1]

    idx = idx_ref[...]
    start = jnp.zeros(idx.shape, F32)
    for e in range(N_EXPERTS):
        start = jnp.where(idx == e, base[e:e + 1, 0:1], start)
    pos = start.astype(I32) + rank_ref[...]
    for j in range(tf // SC_CHUNK):
        pos_ref[j] = pos[:, j * SC_CHUNK:(j + 1) * SC_CHUNK]

    tile_start = (lax.broadcasted_iota(I32, (N_EXPERTS, nt), 1) * group_tile).astype(F32)
    done = jnp.sum(jnp.where(end[:N_EXPERTS] <= tile_start, 1.0, 0.0), axis=0, keepdims=True)
    te_ref[...] = jnp.minimum(done, N_EXPERTS - 1.0).astype(I32)
    in_group = (base[:N_EXPERTS] <= tile_start) & (tile_start < end[:N_EXPERTS])
    real = jnp.clip(base[:N_EXPERTS] + cnt[:N_EXPERTS, 0:1].astype(F32) - tile_start, 0.0, float(group_tile))
    nv_ref[...] = jnp.sum(jnp.where(in_group, real, 0.0), axis=0, keepdims=True).astype(I32)
    total = jnp.sum(padded[:, 0:1], axis=0, keepdims=True)
    nu_ref[...] = jnp.broadcast_to(total * (1.0 / group_tile), nu_ref.shape).astype(I32)


def _plan(idx, rank, counts, *, n_tiles, tf, group_tile):
    t = idx.shape[1]
    nt_pad = -(-n_tiles // LANES) * LANES
    krow = pl.BlockSpec((TOP_K, tf), lambda i: (0, i))
    return pl.pallas_call(
        functools.partial(_plan_kernel, group_tile=group_tile),
        grid=(t // tf,),
        in_specs=[krow, krow, pl.BlockSpec((LANES, LANES), lambda i: (0, 0))],
        out_specs=[pl.BlockSpec((tf // SC_CHUNK, TOP_K, SC_CHUNK), lambda i: (i, 0, 0)),
                   pl.BlockSpec((1, nt_pad), lambda i: (0, 0)),
                   pl.BlockSpec((1, nt_pad), lambda i: (0, 0)),
                   pl.BlockSpec((1, LANES), lambda i: (0, 0))],
        out_shape=[jax.ShapeDtypeStruct((t // SC_CHUNK, TOP_K, SC_CHUNK), I32),
                   jax.ShapeDtypeStruct((1, nt_pad), I32), jax.ShapeDtypeStruct((1, nt_pad), I32),
                   jax.ShapeDtypeStruct((1, LANES), I32)],
        compiler_params=_cparams("arbitrary"),
        name="moe_plan",
    )(idx, rank, counts)


def _sc_mesh_info():
    info = plsc.get_sparse_core_info()
    mesh = plsc.VectorSubcoreMesh(core_axis_name="c", subcore_axis_name="s")
    return mesh, info.num_cores, info.num_cores * info.num_subcores


def _sc_dispatch(pieces, pos, *, n_rows):
    t = pieces[0].shape[0]
    mesh, n_cores, n_workers = _sc_mesh_info()
    per_w = t // SC_CHUNK // n_workers

    @functools.partial(
        pl.kernel, mesh=mesh,
        out_type=[jax.ShapeDtypeStruct((n_rows, LANES), I32)] * N_PIECES,
        scratch_types=[pltpu.VMEM((TOP_K, SC_CHUNK), I32),
                       pltpu.VMEM((N_PIECES, SC_CHUNK, LANES), I32),
                       pltpu.SemaphoreType.DMA((N_PIECES,)),
                       pltpu.SemaphoreType.DMA],
        name="sc_dispatch",
    )
    def run(*refs):
        src = refs[:N_PIECES]
        pos_hbm = refs[N_PIECES]
        dst = refs[N_PIECES + 1:2 * N_PIECES + 1]
        idx_v, rows_v, load_sem, put_sem = refs[2 * N_PIECES + 1:]
        wid = lax.axis_index("s") * n_cores + lax.axis_index("c")

        @pl.loop(0, per_w)
        def _(j):
            ch = wid * per_w + j
            t0 = pl.multiple_of(ch * SC_CHUNK, SC_CHUNK)
            loads = [pltpu.make_async_copy(src[c].at[pl.ds(t0, SC_CHUNK)], rows_v.at[c], load_sem.at[c])
                     for c in range(N_PIECES)]
            for ld in loads:
                ld.start()
            pltpu.sync_copy(pos_hbm.at[ch], idx_v)
            puts = []
            for c in range(N_PIECES):
                loads[c].wait()
                for k in range(TOP_K):
                    puts.append(pltpu.make_async_copy(rows_v.at[c], dst[c].at[idx_v.at[k]], put_sem))
                    puts[-1].start()
            for cp in puts:
                cp.wait()

    return run(*pieces, pos)


SC_GROUP = 32


def _sc_combine(pieces, pos, wts, *, n_tokens):
    mesh, n_cores, n_workers = _sc_mesh_info()
    lanes = plsc.get_sparse_core_info().num_lanes
    per_w = n_tokens // SC_GROUP // n_workers

    @functools.partial(
        pl.kernel, mesh=mesh,
        out_type=[jax.ShapeDtypeStruct((n_tokens, LANES), I32)] * N_PIECES,
        scratch_types=[pltpu.VMEM((TOP_K, SC_GROUP), I32),
                       pltpu.VMEM((TOP_K, SC_GROUP), F32),
                       pltpu.VMEM((2, TOP_K, SC_GROUP, LANES), I32),
                       pltpu.VMEM((2, SC_GROUP, LANES), I32),
                       pltpu.SemaphoreType.DMA((2,)),
                       pltpu.SemaphoreType.DMA((2,))],
        compiler_params=pltpu.CompilerParams(needs_layout_passes=False),
        name="sc_combine",
    )
    def run(*refs):
        src = refs[:N_PIECES]
        pos_hbm, wts_hbm = refs[N_PIECES:N_PIECES + 2]
        dst = refs[N_PIECES + 2:2 * N_PIECES + 2]
        idx_v, w_v, buf, acc, get_sem, put_sem = refs[2 * N_PIECES + 2:]
        wid = lax.axis_index("s") * n_cores + lax.axis_index("c")

        @pl.loop(0, per_w)
        def _(j):
            grp = wid * per_w + j
            t0 = pl.multiple_of(grp * SC_GROUP, SC_GROUP)
            pltpu.sync_copy(pos_hbm.at[grp], idx_v)
            pltpu.sync_copy(wts_hbm.at[grp], w_v)

            def gets(c, slot):
                return [pltpu.make_async_copy(src[c].at[idx_v.at[k]], buf.at[slot, k], get_sem.at[slot])
                        for k in range(TOP_K)]

            def puts(c, slot):
                return [pltpu.make_async_copy(acc.at[slot], dst[c].at[pl.ds(t0, SC_GROUP)], put_sem.at[slot])]

            for cp in gets(0, 0):
                cp.start()
            for c in range(N_PIECES):
                slot = c % 2
                if c + 1 < N_PIECES:
                    for cp in gets(c + 1, 1 - slot):
                        cp.start()
                for cp in gets(c, slot):
                    cp.wait()
                if c >= 2:
                    for cp in puts(c - 2, slot):
                        cp.wait()

                @pl.loop(0, SC_GROUP)
                def _(r):
                    row = jnp.full((lanes,), r, I32)
                    w = []
                    for k in range(TOP_K):
                        w_k = plsc.load_gather(w_v, [jnp.full((lanes,), k, I32), row])
                        w.append(plsc.pack(w_k, w_k, format=plsc.PackFormat.INTERLEAVED))
                    for q in range(LANES // lanes):
                        cols = pl.ds(q * lanes, lanes)
                        total = None
                        for k in range(TOP_K):
                            term = plsc.bitcast(buf[slot, k, r, cols], BF16) * w[k]
                            total = term if total is None else total + term
                        acc[slot, r, cols] = plsc.bitcast(total, I32)

                for cp in puts(c, slot):
                    cp.start()
            for c in range(N_PIECES - 2, N_PIECES):
                for cp in puts(c, c % 2):
                    cp.wait()

    return run(*pieces, pos, wts)


def _group_tile(n_tokens):
    per_expert = n_tokens * TOP_K // N_EXPERTS
    return max(MXU_DIM, min(4 * MXU_DIM, per_expert // MXU_DIM * MXU_DIM))


SECOND_DMA_QUEUE = 1


def _experts_kernel(te_ref, nv_ref, nu_ref, *refs, tile):
    x_hbm = refs[:N_PIECES]
    wg_ref, wu_ref, wd_ref = refs[N_PIECES:N_PIECES + 3]
    y_hbm = refs[N_PIECES + 3:2 * N_PIECES + 3]
    xbuf, xsem, ybuf, ysem = refs[2 * N_PIECES + 3:]
    i = pl.program_id(0)
    n_used = nu_ref[0]
    slot = lax.rem(i, 2)

    half = tile // 2

    def x_copy(step, into, h, c):
        r = pl.ds(pl.multiple_of(step * tile + h * half, half), half)
        return pltpu.make_async_copy(x_hbm[c].at[r], xbuf.at[into, c, pl.ds(h * half, half)], xsem.at[into, c])

    def y_copy(step, out_of, h, c):
        r = pl.ds(pl.multiple_of(step * tile + h * half, half), half)
        return pltpu.make_async_copy(ybuf.at[out_of, c, pl.ds(h * half, half)], y_hbm[c].at[r], ysem.at[out_of, c])

    def real_halves(copy, step, buf, act):
        for c in range(N_PIECES):
            act(copy(step, buf, 0, c))

        @pl.when(nv_ref[step] > half)
        def _():
            for c in range(N_PIECES):
                act(copy(step, buf, 1, c))

    start = lambda cp: cp.start(priority=SECOND_DMA_QUEUE)
    wait = lambda cp: cp.wait()

    @pl.when(i == 0)
    def _():
        real_halves(x_copy, 0, 0, start)

    @pl.when(i + 1 < n_used)
    def _():
        real_halves(x_copy, i + 1, 1 - slot, start)

    @pl.when(i < n_used)
    def _():
        real_halves(x_copy, i, slot, wait)
        subs = [slice(s * MXU_DIM, (s + 1) * MXU_DIM) for s in range(tile // MXU_DIM)]
        xs = []
        for rows in subs:
            lo, hi = _unpack_rows([xbuf[slot, c, rows, :] for c in range(N_PIECES)])
            xs.append(jnp.concatenate(lo + hi, axis=1).astype(BF16))
        gates = [(_dot(x, wg_ref[...]), _dot(x, wu_ref[...])) for x in xs]
        ys = [_dot((_silu(g) * u).astype(BF16), wd_ref[...]) for g, u in gates]
        for rows, y in zip(subs, ys):
            for c, piece in enumerate(_pack_rows(y)):
                ybuf[slot, c, rows, :] = piece

        @pl.when(i >= 1)
        def _():
            real_halves(y_copy, i - 1, 1 - slot, wait)

        real_halves(y_copy, i, slot, start)

        @pl.when(i == n_used - 1)
        def _():
            real_halves(y_copy, i, slot, wait)


def _experts(x_pieces, tile_expert, tile_rows, n_used, wg, wu, wd, *, group_tile):
    n_rows = x_pieces[0].shape[0]
    n_tiles = n_rows // group_tile

    wspec = lambda a: pl.BlockSpec((None,) + a.shape[1:],
                                   lambda i, te, nv, nu: (te[jnp.minimum(i, nu[0] - 1)], 0, 0))
    return pl.pallas_call(
        functools.partial(_experts_kernel, tile=group_tile),
        grid_spec=pltpu.PrefetchScalarGridSpec(
            num_scalar_prefetch=3,
            grid=(n_tiles,),
            in_specs=[pl.BlockSpec(memory_space=pl.ANY)] * N_PIECES + [wspec(wg), wspec(wu), wspec(wd)],
            out_specs=[pl.BlockSpec(memory_space=pl.ANY)] * N_PIECES,
            scratch_shapes=[pltpu.VMEM((2, N_PIECES, group_tile, LANES), I32),
                            pltpu.SemaphoreType.DMA((2, N_PIECES))] * 2),
        out_shape=[jax.ShapeDtypeStruct((n_rows, LANES), I32)] * N_PIECES,
        compiler_params=_cparams("arbitrary"),
        name="experts",
    )(tile_expert, tile_rows, n_used, *x_pieces, wg, wu, wd)


def _moe_out_kernel(x1_ref, mod_ref, fg_ref, sg_ref, su_ref, sd_ref, *refs):
    h_refs = refs[:N_PIECES]
    routed_refs = refs[N_PIECES:2 * N_PIECES]
    out_ref = refs[2 * N_PIECES]
    h_lo, h_hi = _unpack_rows([r[...] for r in h_refs])
    h = jnp.concatenate(h_lo + h_hi, axis=1).astype(BF16)
    hid = _silu(_dot(h, sg_ref[...])) * _dot(h, su_ref[...])
    shared = _dot(hid.astype(BF16), sd_ref[...])
    r_lo, r_hi = _unpack_rows([r[...] for r in routed_refs])
    routed = jnp.concatenate(r_lo + r_hi, axis=1)
    x2 = x1_ref[...] + mod_ref[0, 5:6, :] * (shared + routed)
    out_ref[...] = x2 * lax.rsqrt(jnp.mean(x2 * x2, axis=-1, keepdims=True) + EPS) * fg_ref[...]


def _moe_out(h2_pieces, x1, mods, final_g, sg, su, sd, routed_pieces, *, tokens_per_mod, tm):
    t = x1.shape[0]
    tiles_per_mod = tokens_per_mod // tm
    row = lambda w: pl.BlockSpec((tm, w), lambda i: (i, 0))
    full = lambda a: pl.BlockSpec(a.shape, lambda i: (0,) * a.ndim)
    return pl.pallas_call(
        _moe_out_kernel,
        grid=(t // tm,),
        in_specs=[row(D_MODEL),
                  pl.BlockSpec((1, 6, D_MODEL), lambda i: (i // tiles_per_mod, 0, 0)),
                  full(final_g), full(sg), full(su), full(sd)]
        + [row(LANES)] * (2 * N_PIECES),
        out_specs=row(D_MODEL),
        out_shape=jax.ShapeDtypeStruct((t, D_MODEL), F32),
        compiler_params=_cparams("parallel"),
        name="moe_out",
    )(x1, mods, final_g, sg, su, sd, *h2_pieces, *routed_pieces)


def _trunk(x, mods, s0, w, expert_w, *, batch, seq_len, on_grid):
    t = batch * seq_len
    tokens_per_mod = t // mods.shape[0]
    cos_t, sin_t = _rope_tables(max(seq_len, PROJ_TILE))
    to_cast = expert_w if expert_w[0].dtype != BF16 else ()
    (q, k, v, gsw, up, ga, gb), casted = _inproj(x, mods, w["norm1_g"], w["w_in"], cos_t, sin_t, to_cast,
                                                 tokens_per_mod=tokens_per_mod, seq_len=seq_len,
                                                 on_grid=on_grid, tm=PROJ_TILE)
    expert_w = casted or expert_w
    z, s_f, s_b = _retention(q, k, v, gsw, w["dec"], s0, batch=batch, seq_len=seq_len)
    p = _pool(up, w["pool_w"], w["pool_scale"], batch=batch, seq_len=seq_len, on_grid=on_grid)
    x1, h2_pieces = _merge(x, z, p, ga, gb, mods, w["norm2_g"], w["w_br_ret"], w["w_br_pool"],
                               w["w_out"], tokens_per_mod=tokens_per_mod, tm=PROJ_TILE)

    group_tile = _group_tile(t)
    n_rows = t * TOP_K + N_EXPERTS * group_tile
    idx, rank, wts, counts = _route(h2_pieces, w["router_wt"], w["router_bias"], tm=ROUTE_TILE)
    pos, tile_expert, tile_rows, n_used = _plan(idx, rank, counts, n_tiles=n_rows // group_tile, tf=PLAN_TILE,
                                                group_tile=group_tile)
    x_sorted = _sc_dispatch(h2_pieces, pos, n_rows=n_rows)
    y_sorted = _experts(x_sorted, tile_expert.reshape(-1), tile_rows.reshape(-1), n_used.reshape(-1),
                        *expert_w, group_tile=group_tile)
    regroup = lambda a: a.reshape(TOP_K, t // SC_GROUP, SC_GROUP).transpose(1, 0, 2)
    pos_rows = pos.transpose(1, 0, 2).reshape(TOP_K, t)
    routed = _sc_combine(y_sorted, regroup(pos_rows), regroup(wts), n_tokens=t)
    y = _moe_out(h2_pieces, x1, mods, w["final_g"], w["sh_w_gate"], w["sh_w_up"], w["sh_w_down"], routed,
                 tokens_per_mod=tokens_per_mod, tm=OUT_TILE)
    return y, s_f, s_b, expert_w


def kernel(x_prompt, x_sample, state_ret_fwd, state_ret_bwd, c, c_ctx, ada_w, ada_b, norm1_g, norm2_g, w_in,
           ret_decay_fwd, ret_decay_bwd, w_br_ret, pool_w, pool_scale, w_br_pool, w_out, router_w, router_bias,
           exp_w_gate, exp_w_up, exp_w_down, sh_w_gate, sh_w_up, sh_w_down, final_norm_g):
    n_req, seq, d = x_prompt.shape
    n_dec, dec_seq, _ = x_sample.shape
    depth = ada_w.shape[0]
    assert depth == 1 and d == D_MODEL

    xc = x_prompt.reshape(n_req * seq, d)
    xs = x_sample.reshape(n_dec * dec_seq, d)
    new_f, new_b = [], []
    for l in range(depth):
        c_rows = jnp.concatenate([c_ctx[None, :], c, jnp.zeros((8 - 1 - n_dec, d), F32)], axis=0)
        mods = _ada(c_rows, ada_w[l], ada_b[l]).reshape(8, 6, d)
        pad_rows = LANES - N_EXPERTS
        w = dict(
            norm1_g=norm1_g[l].reshape(1, d), norm2_g=norm2_g[l].reshape(1, d),
            final_g=final_norm_g.reshape(1, d),
            w_in=w_in[l].astype(BF16),
            dec=jnp.stack([ret_decay_fwd[l], ret_decay_bwd[l]]).astype(F32),
            w_br_ret=w_br_ret[l].astype(BF16), pool_w=pool_w[l].astype(BF16),
            pool_scale=pool_scale[l].reshape(1, POOL_W), w_br_pool=w_br_pool[l].astype(BF16),
            w_out=w_out[l].astype(BF16),
            router_wt=jnp.pad(router_w[l].T, ((0, pad_rows), (0, 0))).astype(BF16),
            router_bias=jnp.pad(router_bias[l].astype(F32).reshape(N_EXPERTS, 1), ((0, pad_rows), (0, 0))),
            sh_w_gate=sh_w_gate[l].astype(BF16),
            sh_w_up=sh_w_up[l].astype(BF16), sh_w_down=sh_w_down[l].astype(BF16),
        )
        cached = (state_ret_fwd[:, l].astype(F32), state_ret_bwd[:, l].astype(F32))
        expert_w = (exp_w_gate[l], exp_w_up[l], exp_w_down[l])
        xs, _, _, expert_w = _trunk(xs, mods[1:1 + n_dec], cached, w, expert_w,
                                    batch=n_dec, seq_len=dec_seq, on_grid=True)
        xc, s_f, s_b, _ = _trunk(xc, mods[0:1], None, w, expert_w, batch=n_req, seq_len=seq, on_grid=False)
        new_f.append(s_f)
        new_b.append(s_b)
    y_prompt = xc.reshape(n_req, seq, d)
    y_sample = xs.reshape(n_dec, dec_seq, d)
    return (y_prompt, y_sample, jnp.stack(new_f, axis=1).astype(x_prompt.dtype),
            jnp.stack(new_b, axis=1).astype(x_prompt.dtype))
```

```python
import functools
import math

import numpy as np
import jax
import jax.numpy as jnp
from jax import lax
from jax.experimental import pallas as pl
from jax.experimental.pallas import tpu as pltpu
from jax.experimental.pallas import tpu_sc as plsc

D_MODEL = 1024
GRID_W = 64
RET_HEADS = 4
RET_DK = 128
RET_DV = 256
RET_QK_W = RET_HEADS * RET_DK
RET_V_W = RET_HEADS * RET_DV
RET_CHUNK = 128
ROPE_BASE = 10000.0
POOL_GROUPS = 4
POOL_CH = 128
POOL_W = POOL_GROUPS * POOL_CH
POOL_WINDOWS = (2, 4, 8, 16)
N_EXPERTS = 64
TOP_K = 8
N_EXPERT_GROUPS = 8
GROUP_SIZE = N_EXPERTS // N_EXPERT_GROUPS
TOPK_GROUPS = 4
D_EXPERT = 256
ROUTED_SCALE = 2.5
EPS = 1e-6
IN_SIZES = (RET_QK_W, RET_QK_W, RET_V_W, RET_V_W, POOL_W, D_MODEL, D_MODEL)
IN_OFFS = tuple(sum(IN_SIZES[:i]) for i in range(len(IN_SIZES) + 1))
IN_W = IN_OFFS[-1]

LANES = 128
VMEM_LIMIT = 56 << 20
N_PIECES = D_MODEL // 2 // LANES
MXU_DIM = 256
SC_CHUNK = 128
PROJ_TILE = 512
ROUTE_TILE = 1024
PLAN_TILE = 2048
OUT_TILE = 1024

F32 = jnp.float32
BF16 = jnp.bfloat16
I32 = jnp.int32
U32 = jnp.uint32


def _cparams(*sem):
    return pltpu.CompilerParams(dimension_semantics=sem, vmem_limit_bytes=VMEM_LIMIT)


def _dot(a, b):
    return jnp.dot(a, b, preferred_element_type=F32)


def _silu(x):
    return x * jax.nn.sigmoid(x)


def _rms_mod(x, g, scale, shift):
    y = x * lax.rsqrt(jnp.mean(x * x, axis=-1, keepdims=True) + EPS)
    return (y * g) * (1.0 + scale) + shift


def _ada_kernel(c_ref, w_ref, b_ref, o_ref):
    c = c_ref[...]
    o_ref[...] = jnp.dot(_silu(c), w_ref[...], preferred_element_type=F32,
                         precision=lax.Precision.HIGHEST) + b_ref[...]


def _ada(c_rows, ada_w, ada_b):
    r = c_rows.shape[0]
    n = ada_w.shape[1]
    tn = 2 * D_MODEL
    return pl.pallas_call(
        _ada_kernel,
        grid=(n // tn,),
        in_specs=[pl.BlockSpec((r, D_MODEL), lambda j: (0, 0)),
                  pl.BlockSpec((D_MODEL, tn), lambda j: (0, j)),
                  pl.BlockSpec((1, tn), lambda j: (0, j))],
        out_specs=pl.BlockSpec((r, tn), lambda j: (0, j)),
        out_shape=jax.ShapeDtypeStruct((r, n), F32),
        compiler_params=_cparams("parallel"),
        name="ada_mod",
    )(c_rows, ada_w, ada_b.reshape(1, n))


def _inproj_kernel(x_ref, mod_ref, g_ref, w_ref, cos_ref, sin_ref, *refs, on_grid):
    n_side = (len(refs) - len(IN_SIZES)) // 2
    side_in = refs[:n_side]
    q_ref, k_ref, v_ref, gsw_ref, up_ref, ga_ref, gb_ref = refs[n_side:n_side + len(IN_SIZES)]
    side_out = refs[n_side + len(IN_SIZES):]
    for src, dst in zip(side_in, side_out):
        dst[...] = src[...].astype(BF16)
    for s in range(x_ref.shape[0] // MXU_DIM):
        rows = slice(s * MXU_DIM, (s + 1) * MXU_DIM)
        h = _rms_mod(x_ref[rows, :], g_ref[...], mod_ref[0, 1:2, :], mod_ref[0, 0:1, :]).astype(BF16)

        def seg(i):
            return _dot(h, w_ref[:, IN_OFFS[i]:IN_OFFS[i + 1]])

        q = seg(0)
        k = seg(1)
        if on_grid:
            cos = jnp.concatenate([cos_ref[rows, :]] * RET_HEADS, axis=1)
            sin = jnp.concatenate([sin_ref[rows, :]] * RET_HEADS, axis=1)
            lane = lax.broadcasted_iota(jnp.int32, q.shape, 1)
            first = (lane & 63) < 32

            def rope(a):
                up = pltpu.roll(a, RET_QK_W - 32, axis=1)
                dn = pltpu.roll(a, 32, axis=1)
                return a * cos + jnp.where(first, up, dn) * sin

            q = rope(q)
            k = rope(k)
        q_ref[rows, :] = q.astype(BF16)
        k_ref[rows, :] = (k * (RET_DK ** -0.5)).astype(BF16)
        v_ref[rows, :] = seg(2).astype(BF16)
        gsw_ref[rows, :] = seg(3).astype(BF16)
        up_ref[rows, :] = seg(4).astype(BF16)
        ga_ref[rows, :] = seg(5).astype(BF16)
        gb_ref[rows, :] = seg(6).astype(BF16)


def _inproj(x, mods, norm_g, w_in, cos_t, sin_t, side_cast=(), *, tokens_per_mod, seq_len, on_grid, tm):
    t = x.shape[0]
    steps = t // tm
    tiles_per_mod = tokens_per_mod // tm
    tiles_per_seq = max(seq_len // tm, 1)
    widths = IN_SIZES
    out_shape = [jax.ShapeDtypeStruct((t, w), BF16) for w in widths]
    out_specs = [pl.BlockSpec((tm, w), lambda i: (i, 0)) for w in widths]
    side_specs = [pl.BlockSpec((a.shape[0] // steps,) + a.shape[1:], lambda i: (i, 0, 0)) for a in side_cast]
    outs = pl.pallas_call(
        functools.partial(_inproj_kernel, on_grid=on_grid),
        grid=(steps,),
        in_specs=[pl.BlockSpec((tm, D_MODEL), lambda i: (i, 0)),
                  pl.BlockSpec((1, 6, D_MODEL), lambda i: (i // tiles_per_mod, 0, 0)),
                  pl.BlockSpec((1, D_MODEL), lambda i: (0, 0)),
                  pl.BlockSpec((D_MODEL, IN_W), lambda i: (0, 0), pipeline_mode=pl.Buffered(1)),
                  pl.BlockSpec((tm, RET_DK), lambda i: (i % tiles_per_seq, 0)),
                  pl.BlockSpec((tm, RET_DK), lambda i: (i % tiles_per_seq, 0))] + side_specs,
        out_specs=out_specs + side_specs,
        out_shape=out_shape + [jax.ShapeDtypeStruct(a.shape, BF16) for a in side_cast],
        compiler_params=_cparams("parallel"),
        name="inproj_grid" if on_grid else "inproj_seq",
    )(x, mods, norm_g, w_in, cos_t, sin_t, *side_cast)
    return outs[:len(widths)], tuple(outs[len(widths):])


def _rope_tables(seq_len):
    t = np.arange(seq_len)
    row = (t // GRID_W).astype(np.float32)
    col = (t % GRID_W).astype(np.float32)
    m = RET_DK // 4
    inv = (np.float32(ROPE_BASE) ** (-np.arange(m, dtype=np.float32) / np.float32(m))).astype(np.float32)
    ar = row[:, None] * inv
    ac = col[:, None] * inv
    cos = np.concatenate([np.cos(ar), np.cos(ar), np.cos(ac), np.cos(ac)], axis=1)
    sin = np.concatenate([-np.sin(ar), np.sin(ar), -np.sin(ac), np.sin(ac)], axis=1)
    return jnp.asarray(cos, F32), jnp.asarray(sin, F32)


def _ret_heads_per_step(seq_len):
    per_head = seq_len * (2 * 2 * (2 * RET_DK + 3 * RET_DV) + 4 * RET_DV + 2 * RET_DK)
    heads = RET_HEADS
    while heads > 1 and heads * per_head > VMEM_LIMIT * 3 // 4:
        heads //= 2
    return heads


def _ret_kernel(dec_ref, q_ref, k_ref, v_ref, g_ref, *refs, n_chunks, heads, zero_init):
    s0_refs = () if zero_init else refs[:2]
    z_ref, sf_ref, sb_ref, oacc_ref, kt_ref = refs[len(s0_refs):]
    c = RET_CHUNK
    half = n_chunks // 2
    ii = lax.broadcasted_iota(I32, (c, c), 0)
    jj = lax.broadcasted_iota(I32, (c, c), 1)
    ik = lax.broadcasted_iota(I32, (c, RET_DK), 0).astype(F32)
    jk = lax.broadcasted_iota(I32, (RET_DK, c), 1).astype(F32)

    def log_gamma(d, shape):
        return jnp.log1p(-jnp.exp2(-jnp.full(shape, d, F32)))

    consts = {}
    for hh in range(heads):
        h = pl.program_id(1) * heads + hh
        dec_f = dec_ref[0, h]
        dec_b = dec_ref[1, h]
        rel = (ii - jj).astype(F32)
        consts[hh, "f"] = (
            jnp.where(rel >= 0, jnp.exp(log_gamma(dec_f, (c, c)) * jnp.maximum(rel, 0.0)), 0.0),
            jnp.exp(log_gamma(dec_f, (c, RET_DK)) * (ik + 1.0)),
            jnp.exp(log_gamma(dec_f, (RET_DK, c)) * (c - 1.0 - jk)),
            jnp.exp(log_gamma(dec_f, (RET_DK, RET_DV)) * c))
        consts[hh, "b"] = (
            jnp.where(rel <= 0, jnp.exp(log_gamma(dec_b, (c, c)) * jnp.maximum(-rel, 0.0)), 0.0),
            jnp.exp(log_gamma(dec_b, (c, RET_DK)) * (c - ik)),
            jnp.exp(log_gamma(dec_b, (RET_DK, c)) * jk),
            jnp.exp(log_gamma(dec_b, (RET_DK, RET_DV)) * c))

    for s_ref, s0_ref in zip((sf_ref, sb_ref), s0_refs or (None, None)):
        s_ref[...] = jnp.zeros(s_ref.shape, F32) if zero_init else s0_ref[...]

    def scores(ci, hh, direction, second):
        r = pl.ds(pl.multiple_of(ci * c, c), c)
        kcols = slice(hh * RET_DK, (hh + 1) * RET_DK)
        qc = q_ref[r, kcols]
        kc = k_ref[r, kcols]
        if not second:
            kt_ref[hh, ci] = kc.T
        sc = lax.dot_general(qc, kc, (((1,), (1,)), ((), ())), preferred_element_type=F32)
        return ci, hh, direction, r, qc, sc

    def advance(job):
        ci, hh, direction, r, qc, sc = job
        dmask, qdec, kdec, cdec = consts[hh, direction]
        s_ref = sf_ref if direction == "f" else sb_ref
        vc = v_ref[r, hh * RET_DV:(hh + 1) * RET_DV]
        s = s_ref[hh]
        lhs = jnp.concatenate([(sc * dmask).astype(BF16), (qc.astype(F32) * qdec).astype(BF16)], axis=1)
        o = _dot(lhs, jnp.concatenate([vc, s.astype(BF16)], axis=0))
        kd_t = (kt_ref[hh, ci].astype(F32) * kdec).astype(BF16)
        s_ref[hh] = s * cdec + _dot(kd_t, vc)
        return o

    def emit(job, o, second):
        _, hh, _, r, _, _ = job
        vcols = slice(hh * RET_DV, (hh + 1) * RET_DV)
        if not second:
            oacc_ref[hh, r, :] = o
        else:
            o = o + oacc_ref[hh, r, :]
            o = o * lax.rsqrt(jnp.mean(o * o, axis=-1, keepdims=True) + EPS)
            g = g_ref[r, vcols].astype(F32)
            z_ref[r, vcols] = (_silu(g) * o).astype(BF16)

    def body(second):
        def run(t, carry):
            jobs = [scores(ci, hh, d, second) for hh in range(heads)
                    for ci, d in ((t, "f"), (n_chunks - 1 - t, "b"))]
            outs = [advance(job) for job in jobs]
            for job, o in zip(jobs, outs):
                emit(job, o, second)
            return carry
        return run

    lax.fori_loop(0, half, body(False), 0, unroll=8 if half % 8 == 0 else 1)
    lax.fori_loop(half, n_chunks, body(True), 0, unroll=4 if half % 4 == 0 else 1)


def _retention(q, k, v, gsw, dec, s0, *, batch, seq_len):
    n_chunks = seq_len // RET_CHUNK
    assert n_chunks % 2 == 0
    heads = _ret_heads_per_step(seq_len)
    t = batch * seq_len
    st_spec = pl.BlockSpec((None, heads, RET_DK, RET_DV), lambda b, h: (b, h, 0, 0))
    st_shape = jax.ShapeDtypeStruct((batch, RET_HEADS, RET_DK, RET_DV), F32)
    kspec = pl.BlockSpec((seq_len, heads * RET_DK), lambda b, h: (b, h))
    vspec = pl.BlockSpec((seq_len, heads * RET_DV), lambda b, h: (b, h))
    return pl.pallas_call(
        functools.partial(_ret_kernel, n_chunks=n_chunks, heads=heads, zero_init=s0 is None),
        grid=(batch, RET_HEADS // heads),
        in_specs=[pl.BlockSpec(memory_space=pltpu.SMEM), kspec, kspec, vspec, vspec]
        + ([] if s0 is None else [st_spec, st_spec]),
        out_specs=[vspec, st_spec, st_spec],
        out_shape=[jax.ShapeDtypeStruct((t, RET_V_W), BF16), st_shape, st_shape],
        scratch_shapes=[pltpu.VMEM((heads, seq_len, RET_DV), F32),
                        pltpu.VMEM((heads, n_chunks, RET_DK, RET_CHUNK), BF16)],
        compiler_params=_cparams("parallel", "parallel"),
        name=f"retention_l{seq_len}",
    )(dec, q, k, v, gsw, *(s0 or ()))


def _pool_kernel(u_ref, w_ref, sc_ref, o_ref, *, n_tok, width, two_d):
    n_rows = n_tok // width
    pos = lax.broadcasted_iota(I32, (width, POOL_CH), 0)

    def every_row(a):
        return jnp.concatenate([a] * n_rows, axis=0) if n_rows > 1 else a

    def shift_in_row(a, s):
        ok = (pos < width - s) if s > 0 else (pos >= -s)
        return pltpu.roll(a, (-s) % n_tok, axis=0) * every_row(jnp.where(ok, 1.0, 0.0))

    def shift_rows(a, m):
        k = abs(m) * width
        zeros = jnp.zeros((k, POOL_CH), F32)
        return (jnp.concatenate([a[k:], zeros], axis=0) if m > 0
                else jnp.concatenate([zeros, a[:n_tok - k]], axis=0))

    def box_sum(a, half, shift):
        fw = a
        bw = shift(a, -1)
        m = 1
        while m < half:
            fw = fw + shift(fw, m)
            bw = bw + shift(bw, -m)
            m *= 2
        return fw + bw

    def inv_count(p, half, extent):
        return 1.0 / (jnp.minimum(p + half, extent) - jnp.maximum(p - half, 0)).astype(F32)

    for g, window in enumerate(POOL_WINDOWS):
        half = window // 2
        cols = slice(g * POOL_CH, (g + 1) * POOL_CH)
        ug = u_ref[:, cols].astype(F32)
        total = box_sum(ug, half, shift_in_row)
        inv = every_row(inv_count(pos, half, width))
        if two_d:
            total = box_sum(total, half, shift_rows)
            row = lax.broadcasted_iota(I32, (n_rows, 1, POOL_CH), 0)
            inv_r = jnp.broadcast_to(inv_count(row, half, n_rows), (n_rows, width, POOL_CH))
            inv = inv * inv_r.reshape(n_tok, POOL_CH)
        d = (total * inv - ug).astype(BF16)
        o_ref[:, cols] = (_dot(d, w_ref[g]) * sc_ref[:, cols]).astype(BF16)


def _pool(u, pool_w, pool_scale, *, batch, seq_len, on_grid):
    t = batch * seq_len
    width = GRID_W if on_grid else seq_len
    n_tok = seq_len if on_grid else seq_len * math.gcd(batch, 4)
    return pl.pallas_call(
        functools.partial(_pool_kernel, n_tok=n_tok, width=width, two_d=on_grid),
        grid=(t // n_tok,),
        in_specs=[pl.BlockSpec((n_tok, POOL_W), lambda b: (b, 0)),
                  pl.BlockSpec((POOL_GROUPS, POOL_CH, POOL_CH), lambda b: (0, 0, 0)),
                  pl.BlockSpec((1, POOL_W), lambda b: (0, 0))],
        out_specs=pl.BlockSpec((n_tok, POOL_W), lambda b: (b, 0)),
        out_shape=jax.ShapeDtypeStruct((t, POOL_W), BF16),
        compiler_params=_cparams("parallel"),
        name=f"pool_l{seq_len}",
    )(u, pool_w, pool_scale)


def _pack_rows(x):
    half = D_MODEL // 2
    lo = lax.bitcast_convert_type(x[:, :half].astype(BF16).astype(F32), U32) >> 16
    hi = lax.bitcast_convert_type(x[:, half:].astype(BF16).astype(F32), U32) & jnp.uint32(0xFFFF0000)
    word = lax.bitcast_convert_type(hi | lo, I32)
    return [word[:, c * LANES:(c + 1) * LANES] for c in range(N_PIECES)]


def _unpack_rows(pieces):
    words = [lax.bitcast_convert_type(p, U32) for p in pieces]
    lo = [lax.bitcast_convert_type(w << 16, F32) for w in words]
    hi = [lax.bitcast_convert_type(w & jnp.uint32(0xFFFF0000), F32) for w in words]
    return lo, hi


def _merge_kernel(x_ref, z_ref, p_ref, ga_ref, gb_ref, mod_ref, g2_ref, wr_ref, wp_ref, wo_ref,
                  x1_ref, *piece_refs):
    subs = [slice(s * MXU_DIM, (s + 1) * MXU_DIM) for s in range(x_ref.shape[0] // MXU_DIM)]
    branches = [(_dot(z_ref[r, :], wr_ref[...]), _dot(p_ref[r, :], wp_ref[...])) for r in subs]
    merged = [(jax.nn.sigmoid(ga_ref[r, :].astype(F32)) * y_ret
               + jax.nn.sigmoid(gb_ref[r, :].astype(F32)) * y_pool).astype(BF16)
              for r, (y_ret, y_pool) in zip(subs, branches)]
    outs = [_dot(m, wo_ref[...]) for m in merged]
    for r, o in zip(subs, outs):
        x1 = x_ref[r, :] + mod_ref[0, 2:3, :] * o
        x1_ref[r, :] = x1
        h2 = _rms_mod(x1, g2_ref[...], mod_ref[0, 4:5, :], mod_ref[0, 3:4, :])
        for ref, piece in zip(piece_refs, _pack_rows(h2)):
            ref[r, :] = piece


def _merge(x, z, p, ga, gb, mods, norm2_g, w_br_ret, w_br_pool, w_out, *, tokens_per_mod, tm):
    t = x.shape[0]
    tiles_per_mod = tokens_per_mod // tm
    row = lambda w: pl.BlockSpec((tm, w), lambda i: (i, 0))
    full = lambda a: pl.BlockSpec(a.shape, lambda i: (0,) * a.ndim)
    outs = pl.pallas_call(
        _merge_kernel,
        grid=(t // tm,),
        in_specs=[row(D_MODEL), row(RET_V_W), row(POOL_W), row(D_MODEL), row(D_MODEL),
                  pl.BlockSpec((1, 6, D_MODEL), lambda i: (i // tiles_per_mod, 0, 0)),
                  full(norm2_g), full(w_br_ret), full(w_br_pool), full(w_out)],
        out_specs=[row(D_MODEL)] + [row(LANES)] * N_PIECES,
        out_shape=[jax.ShapeDtypeStruct((t, D_MODEL), F32)] + [jax.ShapeDtypeStruct((t, LANES), I32)] * N_PIECES,
        compiler_params=_cparams("parallel"),
        name="merge",
    )(x, z, p, ga, gb, mods, norm2_g, w_br_ret, w_br_pool, w_out)
    return outs[0], outs[1:]


def _route_kernel(*refs):
    h_refs = refs[:N_PIECES]
    rw_ref, bias_ref, idx_ref, rank_ref, wk_ref, cnt_ref, carry_ref = refs[N_PIECES:]
    e = N_EXPERTS
    tm = h_refs[0].shape[0]
    neg = -jnp.inf

    @pl.when(pl.program_id(0) == 0)
    def _():
        carry_ref[...] = jnp.zeros(carry_ref.shape, F32)

    lo, hi = _unpack_rows([r[...] for r in h_refs])
    h = jnp.concatenate(lo + hi, axis=1).astype(BF16)
    logits = lax.dot_general(rw_ref[...], h, (((1,), (1,)), ((), ())), preferred_element_type=F32)[:e]
    scores = jax.nn.sigmoid(logits)
    sel = scores + bias_ref[:e, 0:1]
    e_idx = lax.broadcasted_iota(I32, (e, tm), 0)

    grp = sel.reshape(N_EXPERT_GROUPS, GROUP_SIZE, tm)
    m_idx = lax.broadcasted_iota(I32, grp.shape, 1)
    m1 = jnp.max(grp, axis=1, keepdims=True)
    first = jnp.min(jnp.where(grp == m1, m_idx, GROUP_SIZE), axis=1, keepdims=True)
    m2 = jnp.max(jnp.where(m_idx == first, neg, grp), axis=1, keepdims=True)
    gscore = (m1 + m2).reshape(N_EXPERT_GROUPS, tm)

    g_idx = lax.broadcasted_iota(I32, gscore.shape, 0)
    grank = jnp.zeros(gscore.shape, I32)
    for g in range(N_EXPERT_GROUPS):
        other = gscore[g:g + 1, :]
        beats = jnp.where(other > gscore, 1, jnp.where(other == gscore, (g_idx > g).astype(I32), 0))
        grank = grank + beats
    gkeep = (grank < TOPK_GROUPS).astype(F32)
    ekeep = jnp.broadcast_to(gkeep.reshape(N_EXPERT_GROUPS, 1, tm), grp.shape).reshape(e, tm)
    masked = jnp.where(ekeep > 0, sel, neg)

    chosen = jnp.zeros((e, tm), F32)
    picks, hits = [], []
    for _ in range(TOP_K):
        m = jnp.max(masked, axis=0, keepdims=True)
        pick = jnp.min(jnp.where(masked == m, e_idx, e), axis=0, keepdims=True)
        hit = e_idx == pick
        chosen = jnp.where(hit, 1.0, chosen)
        masked = jnp.where(hit, neg, masked)
        picks.append(pick)
        hits.append(hit)

    w = scores * chosen
    comb = w / jnp.sum(w, axis=0, keepdims=True) * ROUTED_SCALE

    t_row = lax.broadcasted_iota(I32, (tm, tm), 0)
    t_col = lax.broadcasted_iota(I32, (tm, tm), 1)
    before = (t_row < t_col).astype(BF16)
    rankmat = _dot(chosen.astype(BF16), before) + carry_ref[:e, 0:1]
    carry_ref[:e, :] = carry_ref[:e, :] + jnp.sum(chosen, axis=1, keepdims=True)
    cnt_ref[...] = carry_ref[...]

    idx_ref[...] = jnp.concatenate(picks, axis=0)
    rank_ref[...] = jnp.concatenate(
        [jnp.sum(jnp.where(h, rankmat, 0.0), axis=0, keepdims=True) for h in hits], axis=0).astype(I32)
    wk_ref[...] = jnp.concatenate(
        [jnp.sum(jnp.where(h, comb, 0.0), axis=0, keepdims=True) for h in hits], axis=0)


def _route(h2_pieces, router_wt, bias_col, *, tm):
    t = h2_pieces[0].shape[0]
    krow = pl.BlockSpec((TOP_K, tm), lambda i: (0, i))
    return pl.pallas_call(
        _route_kernel,
        grid=(t // tm,),
        in_specs=[pl.BlockSpec((tm, LANES), lambda i: (i, 0))] * N_PIECES
        + [pl.BlockSpec((LANES, D_MODEL), lambda i: (0, 0)), pl.BlockSpec((LANES, 1), lambda i: (0, 0))],
        out_specs=[krow, krow, krow, pl.BlockSpec((LANES, LANES), lambda i: (0, 0))],
        out_shape=[jax.ShapeDtypeStruct((TOP_K, t), I32), jax.ShapeDtypeStruct((TOP_K, t), I32),
                   jax.ShapeDtypeStruct((TOP_K, t), F32), jax.ShapeDtypeStruct((LANES, LANES), F32)],
        scratch_shapes=[pltpu.VMEM((LANES, LANES), F32)],
        compiler_params=_cparams("arbitrary"),
        name="route",
    )(*h2_pieces, router_wt, bias_col)


def _plan_kernel(idx_ref, rank_ref, cnt_ref, pos_ref, te_ref, nv_ref, nu_ref, *, group_tile):
    tf = idx_ref.shape[1]
    nt = te_ref.shape[1]
    cnt = cnt_ref[...].astype(I32)
    padded = (((cnt + (group_tile - 1)) // group_tile) * group_tile).astype(F32)
    e_sub = lax.broadcasted_iota(I32, (LANES, LANES), 0)
    e_lane = lax.broadcasted_iota(I32, (LANES, LANES), 1)
    base = jnp.sum(jnp.where(e_lane < e_sub, padded.T, 0.0), axis=1, keepdims=True)
    end = base + padded[:, 0:1]

    idx = idx_ref[...]
    start = jnp.zeros(idx.shape, F32)
    for e in range(N_EXPERTS):
        start = jnp.where(idx == e, base[e:e + 1, 0:1], start)
    pos = start.astype(I32) + rank_ref[...]
    for j in range(tf // SC_CHUNK):
        pos_ref[j] = pos[:, j * SC_CHUNK:(j + 1) * SC_CHUNK]

    tile_start = (lax.broadcasted_iota(I32, (N_EXPERTS, nt), 1) * group_tile).astype(F32)
    done = jnp.sum(jnp.where(end[:N_EXPERTS] <= tile_start, 1.0, 0.0), axis=0, keepdims=True)
    te_ref[...] = jnp.minimum(done, N_EXPERTS - 1.0).astype(I32)
    in_group = (base[:N_EXPERTS] <= tile_start) & (tile_start < end[:N_EXPERTS])
    real = jnp.clip(base[:N_EXPERTS] + cnt[:N_EXPERTS, 0:1].astype(F32) - tile_start, 0.0, float(group_tile))
    nv_ref[...] = jnp.sum(jnp.where(in_group, real, 0.0), axis=0, keepdims=True).astype(I32)
    total = jnp.sum(padded[:, 0:1], axis=0, keepdims=True)
    nu_ref[...] = jnp.broadcast_to(total * (1.0 / group_tile), nu_ref.shape).astype(I32)


def _plan(idx, rank, counts, *, n_tiles, tf, group_tile):
    t = idx.shape[1]
    nt_pad = -(-n_tiles // LANES) * LANES
    krow = pl.BlockSpec((TOP_K, tf), lambda i: (0, i))
    return pl.pallas_call(
        functools.partial(_plan_kernel, group_tile=group_tile),
        grid=(t // tf,),
        in_specs=[krow, krow, pl.BlockSpec((LANES, LANES), lambda i: (0, 0))],
        out_specs=[pl.BlockSpec((tf // SC_CHUNK, TOP_K, SC_CHUNK), lambda i: (i, 0, 0)),
                   pl.BlockSpec((1, nt_pad), lambda i: (0, 0)),
                   pl.BlockSpec((1, nt_pad), lambda i: (0, 0)),
                   pl.BlockSpec((1, LANES), lambda i: (0, 0))],
        out_shape=[jax.ShapeDtypeStruct((t // SC_CHUNK, TOP_K, SC_CHUNK), I32),
                   jax.ShapeDtypeStruct((1, nt_pad), I32), jax.ShapeDtypeStruct((1, nt_pad), I32),
                   jax.ShapeDtypeStruct((1, LANES), I32)],
        compiler_params=_cparams("arbitrary"),
        name="moe_plan",
    )(idx, rank, counts)


def _sc_mesh_info():
    info = plsc.get_sparse_core_info()
    mesh = plsc.VectorSubcoreMesh(core_axis_name="c", subcore_axis_name="s")
    return mesh, info.num_cores, info.num_cores * info.num_subcores


def _sc_dispatch(pieces, pos, *, n_rows):
    t = pieces[0].shape[0]
    mesh, n_cores, n_workers = _sc_mesh_info()
    per_w = t // SC_CHUNK // n_workers

    @functools.partial(
        pl.kernel, mesh=mesh,
        out_type=[jax.ShapeDtypeStruct((n_rows, LANES), I32)] * N_PIECES,
        scratch_types=[pltpu.VMEM((TOP_K, SC_CHUNK), I32),
                       pltpu.VMEM((N_PIECES, SC_CHUNK, LANES), I32),
                       pltpu.SemaphoreType.DMA((N_PIECES,)),
                       pltpu.SemaphoreType.DMA],
        name="sc_dispatch",
    )
    def run(*refs):
        src = refs[:N_PIECES]
        pos_hbm = refs[N_PIECES]
        dst = refs[N_PIECES + 1:2 * N_PIECES + 1]
        idx_v, rows_v, load_sem, put_sem = refs[2 * N_PIECES + 1:]
        wid = lax.axis_index("s") * n_cores + lax.axis_index("c")

        @pl.loop(0, per_w)
        def _(j):
            ch = wid * per_w + j
            t0 = pl.multiple_of(ch * SC_CHUNK, SC_CHUNK)
            loads = [pltpu.make_async_copy(src[c].at[pl.ds(t0, SC_CHUNK)], rows_v.at[c], load_sem.at[c])
                     for c in range(N_PIECES)]
            for ld in loads:
                ld.start()
            pltpu.sync_copy(pos_hbm.at[ch], idx_v)
            puts = []
            for c in range(N_PIECES):
                loads[c].wait()
                for k in range(TOP_K):
                    puts.append(pltpu.make_async_copy(rows_v.at[c], dst[c].at[idx_v.at[k]], put_sem))
                    puts[-1].start()
            for cp in puts:
                cp.wait()

    return run(*pieces, pos)


SC_GROUP = 32


def _sc_combine(pieces, pos, wts, *, n_tokens):
    mesh, n_cores, n_workers = _sc_mesh_info()
    lanes = plsc.get_sparse_core_info().num_lanes
    per_w = n_tokens // SC_GROUP // n_workers

    @functools.partial(
        pl.kernel, mesh=mesh,
        out_type=[jax.ShapeDtypeStruct((n_tokens, LANES), I32)] * N_PIECES,
        scratch_types=[pltpu.VMEM((TOP_K, SC_GROUP), I32),
                       pltpu.VMEM((TOP_K, SC_GROUP), F32),
                       pltpu.VMEM((2, TOP_K, SC_GROUP, LANES), I32),
                       pltpu.VMEM((2, SC_GROUP, LANES), I32),
                       pltpu.SemaphoreType.DMA((2,)),
                       pltpu.SemaphoreType.DMA((2,))],
        compiler_params=pltpu.CompilerParams(needs_layout_passes=False),
        name="sc_combine",
    )
    def run(*refs):
        src = refs[:N_PIECES]
        pos_hbm, wts_hbm = refs[N_PIECES:N_PIECES + 2]
        dst = refs[N_PIECES + 2:2 * N_PIECES + 2]
        idx_v, w_v, buf, acc, get_sem, put_sem = refs[2 * N_PIECES + 2:]
        wid = lax.axis_index("s") * n_cores + lax.axis_index("c")

        @pl.loop(0, per_w)
        def _(j):
            grp = wid * per_w + j
            t0 = pl.multiple_of(grp * SC_GROUP, SC_GROUP)
            pltpu.sync_copy(pos_hbm.at[grp], idx_v)
            pltpu.sync_copy(wts_hbm.at[grp], w_v)

            def gets(c, slot):
                return [pltpu.make_async_copy(src[c].at[idx_v.at[k]], buf.at[slot, k], get_sem.at[slot])
                        for k in range(TOP_K)]

            def puts(c, slot):
                return [pltpu.make_async_copy(acc.at[slot], dst[c].at[pl.ds(t0, SC_GROUP)], put_sem.at[slot])]

            for cp in gets(0, 0):
                cp.start()
            for c in range(N_PIECES):
                slot = c % 2
                if c + 1 < N_PIECES:
                    for cp in gets(c + 1, 1 - slot):
                        cp.start()
                for cp in gets(c, slot):
                    cp.wait()
                if c >= 2:
                    for cp in puts(c - 2, slot):
                        cp.wait()

                @pl.loop(0, SC_GROUP)
                def _(r):
                    row = jnp.full((lanes,), r, I32)
                    w = []
                    for k in range(TOP_K):
                        w_k = plsc.load_gather(w_v, [jnp.full((lanes,), k, I32), row])
                        w.append(plsc.pack(w_k, w_k, format=plsc.PackFormat.INTERLEAVED))
                    for q in range(LANES // lanes):
                        cols = pl.ds(q * lanes, lanes)
                        total = None
                        for k in range(TOP_K):
                            term = plsc.bitcast(buf[slot, k, r, cols], BF16) * w[k]
                            total = term if total is None else total + term
                        acc[slot, r, cols] = plsc.bitcast(total, I32)

                for cp in puts(c, slot):
                    cp.start()
            for c in range(N_PIECES - 2, N_PIECES):
                for cp in puts(c, c % 2):
                    cp.wait()

    return run(*pieces, pos, wts)


def _group_tile(n_tokens):
    per_expert = n_tokens * TOP_K // N_EXPERTS
    return max(MXU_DIM, min(4 * MXU_DIM, per_expert // MXU_DIM * MXU_DIM))


SECOND_DMA_QUEUE = 1


def _experts_kernel(te_ref, nv_ref, nu_ref, *refs, tile):
    x_hbm = refs[:N_PIECES]
    wg_ref, wu_ref, wd_ref = refs[N_PIECES:N_PIECES + 3]
    y_hbm = refs[N_PIECES + 3:2 * N_PIECES + 3]
    xbuf, xsem, ybuf, ysem = refs[2 * N_PIECES + 3:]
    i = pl.program_id(0)
    n_used = nu_ref[0]
    slot = lax.rem(i, 2)

    half = tile // 2

    def x_copy(step, into, h, c):
        r = pl.ds(pl.multiple_of(step * tile + h * half, half), half)
        return pltpu.make_async_copy(x_hbm[c].at[r], xbuf.at[into, c, pl.ds(h * half, half)], xsem.at[into, c])

    def y_copy(step, out_of, h, c):
        r = pl.ds(pl.multiple_of(step * tile + h * half, half), half)
        return pltpu.make_async_copy(ybuf.at[out_of, c, pl.ds(h * half, half)], y_hbm[c].at[r], ysem.at[out_of, c])

    def real_halves(copy, step, buf, act):
        for c in range(N_PIECES):
            act(copy(step, buf, 0, c))

        @pl.when(nv_ref[step] > half)
        def _():
            for c in range(N_PIECES):
                act(copy(step, buf, 1, c))

    start = lambda cp: cp.start(priority=SECOND_DMA_QUEUE)
    wait = lambda cp: cp.wait()

    @pl.when(i == 0)
    def _():
        real_halves(x_copy, 0, 0, start)

    @pl.when(i + 1 < n_used)
    def _():
        real_halves(x_copy, i + 1, 1 - slot, start)

    @pl.when(i < n_used)
    def _():
        real_halves(x_copy, i, slot, wait)
        subs = [slice(s * MXU_DIM, (s + 1) * MXU_DIM) for s in range(tile // MXU_DIM)]
        xs = []
        for rows in subs:
            lo, hi = _unpack_rows([xbuf[slot, c, rows, :] for c in range(N_PIECES)])
            xs.append(jnp.concatenate(lo + hi, axis=1).astype(BF16))
        gates = [(_dot(x, wg_ref[...]), _dot(x, wu_ref[...])) for x in xs]
        ys = [_dot((_silu(g) * u).astype(BF16), wd_ref[...]) for g, u in gates]
        for rows, y in zip(subs, ys):
            for c, piece in enumerate(_pack_rows(y)):
                ybuf[slot, c, rows, :] = piece

        @pl.when(i >= 1)
        def _():
            real_halves(y_copy, i - 1, 1 - slot, wait)

        real_halves(y_copy, i, slot, start)

        @pl.when(i == n_used - 1)
        def _():
            real_halves(y_copy, i, slot, wait)


def _experts(x_pieces, tile_expert, tile_rows, n_used, wg, wu, wd, *, group_tile):
    n_rows = x_pieces[0].shape[0]
    n_tiles = n_rows // group_tile

    wspec = lambda a: pl.BlockSpec((None,) + a.shape[1:],
                                   lambda i, te, nv, nu: (te[jnp.minimum(i, nu[0] - 1)], 0, 0))
    return pl.pallas_call(
        functools.partial(_experts_kernel, tile=group_tile),
        grid_spec=pltpu.PrefetchScalarGridSpec(
            num_scalar_prefetch=3,
            grid=(n_tiles,),
            in_specs=[pl.BlockSpec(memory_space=pl.ANY)] * N_PIECES + [wspec(wg), wspec(wu), wspec(wd)],
            out_specs=[pl.BlockSpec(memory_space=pl.ANY)] * N_PIECES,
            scratch_shapes=[pltpu.VMEM((2, N_PIECES, group_tile, LANES), I32),
                            pltpu.SemaphoreType.DMA((2, N_PIECES))] * 2),
        out_shape=[jax.ShapeDtypeStruct((n_rows, LANES), I32)] * N_PIECES,
        compiler_params=_cparams("arbitrary"),
        name="experts",
    )(tile_expert, tile_rows, n_used, *x_pieces, wg, wu, wd)


def _moe_out_kernel(x1_ref, mod_ref, fg_ref, sg_ref, su_ref, sd_ref, *refs):
    h_refs = refs[:N_PIECES]
    routed_refs = refs[N_PIECES:2 * N_PIECES]
    out_ref = refs[2 * N_PIECES]
    h_lo, h_hi = _unpack_rows([r[...] for r in h_refs])
    h = jnp.concatenate(h_lo + h_hi, axis=1).astype(BF16)
    hid = _silu(_dot(h, sg_ref[...])) * _dot(h, su_ref[...])
    shared = _dot(hid.astype(BF16), sd_ref[...])
    r_lo, r_hi = _unpack_rows([r[...] for r in routed_refs])
    routed = jnp.concatenate(r_lo + r_hi, axis=1)
    x2 = x1_ref[...] + mod_ref[0, 5:6, :] * (shared + routed)
    out_ref[...] = x2 * lax.rsqrt(jnp.mean(x2 * x2, axis=-1, keepdims=True) + EPS) * fg_ref[...]


def _moe_out(h2_pieces, x1, mods, final_g, sg, su, sd, routed_pieces, *, tokens_per_mod, tm):
    t = x1.shape[0]
    tiles_per_mod = tokens_per_mod // tm
    row = lambda w: pl.BlockSpec((tm, w), lambda i: (i, 0))
    full = lambda a: pl.BlockSpec(a.shape, lambda i: (0,) * a.ndim)
    return pl.pallas_call(
        _moe_out_kernel,
        grid=(t // tm,),
        in_specs=[row(D_MODEL),
                  pl.BlockSpec((1, 6, D_MODEL), lambda i: (i // tiles_per_mod, 0, 0)),
                  full(final_g), full(sg), full(su), full(sd)]
        + [row(LANES)] * (2 * N_PIECES),
        out_specs=row(D_MODEL),
        out_shape=jax.ShapeDtypeStruct((t, D_MODEL), F32),
        compiler_params=_cparams("parallel"),
        name="moe_out",
    )(x1, mods, final_g, sg, su, sd, *h2_pieces, *routed_pieces)


def _trunk(x, mods, s0, w, expert_w, *, batch, seq_len, on_grid):
    t = batch * seq_len
    tokens_per_mod = t // mods.shape[0]
    cos_t, sin_t = _rope_tables(max(seq_len, PROJ_TILE))
    to_cast = expert_w if expert_w[0].dtype != BF16 else ()
    (q, k, v, gsw, up, ga, gb), casted = _inproj(x, mods, w["norm1_g"], w["w_in"], cos_t, sin_t, to_cast,
                                                 tokens_per_mod=tokens_per_mod, seq_len=seq_len,
                                                 on_grid=on_grid, tm=PROJ_TILE)
    expert_w = casted or expert_w
    z, s_f, s_b = _retention(q, k, v, gsw, w["dec"], s0, batch=batch, seq_len=seq_len)
    p = _pool(up, w["pool_w"], w["pool_scale"], batch=batch, seq_len=seq_len, on_grid=on_grid)
    x1, h2_pieces = _merge(x, z, p, ga, gb, mods, w["norm2_g"], w["w_br_ret"], w["w_br_pool"],
                               w["w_out"], tokens_per_mod=tokens_per_mod, tm=PROJ_TILE)

    group_tile = _group_tile(t)
    n_rows = t * TOP_K + N_EXPERTS * group_tile
    idx, rank, wts, counts = _route(h2_pieces, w["router_wt"], w["router_bias"], tm=ROUTE_TILE)
    pos, tile_expert, tile_rows, n_used = _plan(idx, rank, counts, n_tiles=n_rows // group_tile, tf=PLAN_TILE,
                                                group_tile=group_tile)
    x_sorted = _sc_dispatch(h2_pieces, pos, n_rows=n_rows)
    y_sorted = _experts(x_sorted, tile_expert.reshape(-1), tile_rows.reshape(-1), n_used.reshape(-1),
                        *expert_w, group_tile=group_tile)
    regroup = lambda a: a.reshape(TOP_K, t // SC_GROUP, SC_GROUP).transpose(1, 0, 2)
    pos_rows = pos.transpose(1, 0, 2).reshape(TOP_K, t)
    routed = _sc_combine(y_sorted, regroup(pos_rows), regroup(wts), n_tokens=t)
    y = _moe_out(h2_pieces, x1, mods, w["final_g"], w["sh_w_gate"], w["sh_w_up"], w["sh_w_down"], routed,
                 tokens_per_mod=tokens_per_mod, tm=OUT_TILE)
    return y, s_f, s_b, expert_w


def kernel(x_prompt, x_sample, state_ret_fwd, state_ret_bwd, c, c_ctx, ada_w, ada_b, norm1_g, norm2_g, w_in,
           ret_decay_fwd, ret_decay_bwd, w_br_ret, pool_w, pool_scale, w_br_pool, w_out, router_w, router_bias,
           exp_w_gate, exp_w_up, exp_w_down, sh_w_gate, sh_w_up, sh_w_down, final_norm_g):
    n_req, seq, d = x_prompt.shape
    n_dec, dec_seq, _ = x_sample.shape
    depth = ada_w.shape[0]
    assert depth == 1 and d == D_MODEL

    xc = x_prompt.reshape(n_req * seq, d)
    xs = x_sample.reshape(n_dec * dec_seq, d)
    new_f, new_b = [], []
    for l in range(depth):
        c_rows = jnp.concatenate([c_ctx[None, :], c, jnp.zeros((8 - 1 - n_dec, d), F32)], axis=0)
        mods = _ada(c_rows, ada_w[l], ada_b[l]).reshape(8, 6, d)
        pad_rows = LANES - N_EXPERTS
        w = dict(
            norm1_g=norm1_g[l].reshape(1, d), norm2_g=norm2_g[l].reshape(1, d),
            final_g=final_norm_g.reshape(1, d),
            w_in=w_in[l].astype(BF16),
            dec=jnp.stack([ret_decay_fwd[l], ret_decay_bwd[l]]).astype(F32),
            w_br_ret=w_br_ret[l].astype(BF16), pool_w=pool_w[l].astype(BF16),
            pool_scale=pool_scale[l].reshape(1, POOL_W), w_br_pool=w_br_pool[l].astype(BF16),
            w_out=w_out[l].astype(BF16),
            router_wt=jnp.pad(router_w[l].T, ((0, pad_rows), (0, 0))).astype(BF16),
            router_bias=jnp.pad(router_bias[l].astype(F32).reshape(N_EXPERTS, 1), ((0, pad_rows), (0, 0))),
            sh_w_gate=sh_w_gate[l].astype(BF16),
            sh_w_up=sh_w_up[l].astype(BF16), sh_w_down=sh_w_down[l].astype(BF16),
        )
        cached = (state_ret_fwd[:, l].astype(F32), state_ret_bwd[:, l].astype(F32))
        expert_w = (exp_w_gate[l], exp_w_up[l], exp_w_down[l])
        xs, _, _, expert_w = _trunk(xs, mods[1:1 + n_dec], cached, w, expert_w,
                                    batch=n_dec, seq_len=dec_seq, on_grid=True)
        xc, s_f, s_b, _ = _trunk(xc, mods[0:1], None, w, expert_w, batch=n_req, seq_len=seq, on_grid=False)
        new_f.append(s_f)
        new_b.append(s_b)
    y_prompt = xc.reshape(n_req, seq, d)
    y_sample = xs.reshape(n_dec, dec_seq, d)
    return (y_prompt, y_sample, jnp.stack(new_f, axis=1).astype(x_prompt.dtype),
            jnp.stack(new_b, axis=1).astype(x_prompt.dtype))
```

```python
import functools
import math

import numpy as np
import jax
import jax.numpy as jnp
from jax import lax
from jax.experimental import pallas as pl
from jax.experimental.pallas import tpu as pltpu
from jax.experimental.pallas import tpu_sc as plsc

D_MODEL = 1024
GRID_W = 64
RET_HEADS = 4
RET_DK = 128
RET_DV = 256
RET_QK_W = RET_HEADS * RET_DK
RET_V_W = RET_HEADS * RET_DV
RET_CHUNK = 128
ROPE_BASE = 10000.0
POOL_GROUPS = 4
POOL_CH = 128
POOL_W = POOL_GROUPS * POOL_CH
POOL_WINDOWS = (2, 4, 8, 16)
N_EXPERTS = 64
TOP_K = 8
N_EXPERT_GROUPS = 8
GROUP_SIZE = N_EXPERTS // N_EXPERT_GROUPS
TOPK_GROUPS = 4
D_EXPERT = 256
ROUTED_SCALE = 2.5
EPS = 1e-6
IN_SIZES = (RET_QK_W, RET_QK_W, RET_V_W, RET_V_W, POOL_W, D_MODEL, D_MODEL)
IN_OFFS = tuple(sum(IN_SIZES[:i]) for i in range(len(IN_SIZES) + 1))
IN_W = IN_OFFS[-1]

LANES = 128
VMEM_LIMIT = 56 << 20
N_PIECES = D_MODEL // 2 // LANES
MXU_DIM = 256
SC_CHUNK = 128
PROJ_TILE = 512
ROUTE_TILE = 1024
PLAN_TILE = 2048
OUT_TILE = 1024

F32 = jnp.float32
BF16 = jnp.bfloat16
I32 = jnp.int32
U32 = jnp.uint32


def _cparams(*sem):
    return pltpu.CompilerParams(dimension_semantics=sem, vmem_limit_bytes=VMEM_LIMIT)


def _dot(a, b):
    return jnp.dot(a, b, preferred_element_type=F32)


def _silu(x):
    return x * jax.nn.sigmoid(x)


def _rms_mod(x, g, scale, shift):
    y = x * lax.rsqrt(jnp.mean(x * x, axis=-1, keepdims=True) + EPS)
    return (y * g) * (1.0 + scale) + shift


def _ada_kernel(c_ref, w_ref, b_ref, o_ref):
    c = c_ref[...]
    o_ref[...] = jnp.dot(_silu(c), w_ref[...], preferred_element_type=F32,
                         precision=lax.Precision.HIGHEST) + b_ref[...]


def _ada(c_rows, ada_w, ada_b):
    r = c_rows.shape[0]
    n = ada_w.shape[1]
    tn = 2 * D_MODEL
    return pl.pallas_call(
        _ada_kernel,
        grid=(n // tn,),
        in_specs=[pl.BlockSpec((r, D_MODEL), lambda j: (0, 0)),
                  pl.BlockSpec((D_MODEL, tn), lambda j: (0, j)),
                  pl.BlockSpec((1, tn), lambda j: (0, j))],
        out_specs=pl.BlockSpec((r, tn), lambda j: (0, j)),
        out_shape=jax.ShapeDtypeStruct((r, n), F32),
        compiler_params=_cparams("parallel"),
        name="ada_mod",
    )(c_rows, ada_w, ada_b.reshape(1, n))


def _inproj_kernel(x_ref, mod_ref, g_ref, w_ref, cos_ref, sin_ref, *refs, on_grid):
    n_side = (len(refs) - len(IN_SIZES)) // 2
    side_in = refs[:n_side]
    q_ref, k_ref, v_ref, gsw_ref, up_ref, ga_ref, gb_ref = refs[n_side:n_side + len(IN_SIZES)]
    side_out = refs[n_side + len(IN_SIZES):]
    for src, dst in zip(side_in, side_out):
        dst[...] = src[...].astype(BF16)
    for s in range(x_ref.shape[0] // MXU_DIM):
        rows = slice(s * MXU_DIM, (s + 1) * MXU_DIM)
        h = _rms_mod(x_ref[rows, :], g_ref[...], mod_ref[0, 1:2, :], mod_ref[0, 0:1, :]).astype(BF16)

        def seg(i):
            return _dot(h, w_ref[:, IN_OFFS[i]:IN_OFFS[i + 1]])

        q = seg(0)
        k = seg(1)
        if on_grid:
            cos = jnp.concatenate([cos_ref[rows, :]] * RET_HEADS, axis=1)
            sin = jnp.concatenate([sin_ref[rows, :]] * RET_HEADS, axis=1)
            lane = lax.broadcasted_iota(jnp.int32, q.shape, 1)
            first = (lane & 63) < 32

            def rope(a):
                up = pltpu.roll(a, RET_QK_W - 32, axis=1)
                dn = pltpu.roll(a, 32, axis=1)
                return a * cos + jnp.where(first, up, dn) * sin

            q = rope(q)
            k = rope(k)
        q_ref[rows, :] = q.astype(BF16)
        k_ref[rows, :] = (k * (RET_DK ** -0.5)).astype(BF16)
        v_ref[rows, :] = seg(2).astype(BF16)
        gsw_ref[rows, :] = seg(3).astype(BF16)
        up_ref[rows, :] = seg(4).astype(BF16)
        ga_ref[rows, :] = seg(5).astype(BF16)
        gb_ref[rows, :] = seg(6).astype(BF16)


def _inproj(x, mods, norm_g, w_in, cos_t, sin_t, side_cast=(), *, tokens_per_mod, seq_len, on_grid, tm):
    t = x.shape[0]
    steps = t // tm
    tiles_per_mod = tokens_per_mod // tm
    tiles_per_seq = max(seq_len // tm, 1)
    widths = IN_SIZES
    out_shape = [jax.ShapeDtypeStruct((t, w), BF16) for w in widths]
    out_specs = [pl.BlockSpec((tm, w), lambda i: (i, 0)) for w in widths]
    side_specs = [pl.BlockSpec((a.shape[0] // steps,) + a.shape[1:], lambda i: (i, 0, 0)) for a in side_cast]
    outs = pl.pallas_call(
        functools.partial(_inproj_kernel, on_grid=on_grid),
        grid=(steps,),
        in_specs=[pl.BlockSpec((tm, D_MODEL), lambda i: (i, 0)),
                  pl.BlockSpec((1, 6, D_MODEL), lambda i: (i // tiles_per_mod, 0, 0)),
                  pl.BlockSpec((1, D_MODEL), lambda i: (0, 0)),
                  pl.BlockSpec((D_MODEL, IN_W), lambda i: (0, 0), pipeline_mode=pl.Buffered(1)),
                  pl.BlockSpec((tm, RET_DK), lambda i: (i % tiles_per_seq, 0)),
                  pl.BlockSpec((tm, RET_DK), lambda i: (i % tiles_per_seq, 0))] + side_specs,
        out_specs=out_specs + side_specs,
        out_shape=out_shape + [jax.ShapeDtypeStruct(a.shape, BF16) for a in side_cast],
        compiler_params=_cparams("parallel"),
        name="inproj_grid" if on_grid else "inproj_seq",
    )(x, mods, norm_g, w_in, cos_t, sin_t, *side_cast)
    return outs[:len(widths)], tuple(outs[len(widths):])


def _rope_tables(seq_len):
    t = np.arange(seq_len)
    row = (t // GRID_W).astype(np.float32)
    col = (t % GRID_W).astype(np.float32)
    m = RET_DK // 4
    inv = (np.float32(ROPE_BASE) ** (-np.arange(m, dtype=np.float32) / np.float32(m))).astype(np.float32)
    ar = row[:, None] * inv
    ac = col[:, None] * inv
    cos = np.concatenate([np.cos(ar), np.cos(ar), np.cos(ac), np.cos(ac)], axis=1)
    sin = np.concatenate([-np.sin(ar), np.sin(ar), -np.sin(ac), np.sin(ac)], axis=1)
    return jnp.asarray(cos, F32), jnp.asarray(sin, F32)


def _ret_heads_per_step(seq_len):
    per_head = seq_len * (2 * 2 * (2 * RET_DK + 3 * RET_DV) + 4 * RET_DV + 2 * RET_DK)
    heads = RET_HEADS
    while heads > 1 and heads * per_head > VMEM_LIMIT * 3 // 4:
        heads //= 2
    return heads


def _ret_kernel(dec_ref, q_ref, k_ref, v_ref, g_ref, *refs, n_chunks, heads, zero_init):
    s0_refs = () if zero_init else refs[:2]
    z_ref, sf_ref, sb_ref, oacc_ref, kt_ref = refs[len(s0_refs):]
    c = RET_CHUNK
    half = n_chunks // 2
    ii = lax.broadcasted_iota(I32, (c, c), 0)
    jj = lax.broadcasted_iota(I32, (c, c), 1)
    ik = lax.broadcasted_iota(I32, (c, RET_DK), 0).astype(F32)
    jk = lax.broadcasted_iota(I32, (RET_DK, c), 1).astype(F32)

    def log_gamma(d, shape):
        return jnp.log1p(-jnp.exp2(-jnp.full(shape, d, F32)))

    consts = {}
    for hh in range(heads):
        h = pl.program_id(1) * heads + hh
        dec_f = dec_ref[0, h]
        dec_b = dec_ref[1, h]
        rel = (ii - jj).astype(F32)
        consts[hh, "f"] = (
            jnp.where(rel >= 0, jnp.exp(log_gamma(dec_f, (c, c)) * jnp.maximum(rel, 0.0)), 0.0),
            jnp.exp(log_gamma(dec_f, (c, RET_DK)) * (ik + 1.0)),
            jnp.exp(log_gamma(dec_f, (RET_DK, c)) * (c - 1.0 - jk)),
            jnp.exp(log_gamma(dec_f, (RET_DK, RET_DV)) * c))
        consts[hh, "b"] = (
            jnp.where(rel <= 0, jnp.exp(log_gamma(dec_b, (c, c)) * jnp.maximum(-rel, 0.0)), 0.0),
            jnp.exp(log_gamma(dec_b, (c, RET_DK)) * (c - ik)),
            jnp.exp(log_gamma(dec_b, (RET_DK, c)) * jk),
            jnp.exp(log_gamma(dec_b, (RET_DK, RET_DV)) * c))

    for s_ref, s0_ref in zip((sf_ref, sb_ref), s0_refs or (None, None)):
        s_ref[...] = jnp.zeros(s_ref.shape, F32) if zero_init else s0_ref[...]

    def scores(ci, hh, direction, second):
        r = pl.ds(pl.multiple_of(ci * c, c), c)
        kcols = slice(hh * RET_DK, (hh + 1) * RET_DK)
        qc = q_ref[r, kcols]
        kc = k_ref[r, kcols]
        if not second:
            kt_ref[hh, ci] = kc.T
        sc = lax.dot_general(qc, kc, (((1,), (1,)), ((), ())), preferred_element_type=F32)
        return ci, hh, direction, r, qc, sc

    def advance(job):
        ci, hh, direction, r, qc, sc = job
        dmask, qdec, kdec, cdec = consts[hh, direction]
        s_ref = sf_ref if direction == "f" else sb_ref
        vc = v_ref[r, hh * RET_DV:(hh + 1) * RET_DV]
        s = s_ref[hh]
        lhs = jnp.concatenate([(sc * dmask).astype(BF16), (qc.astype(F32) * qdec).astype(BF16)], axis=1)
        o = _dot(lhs, jnp.concatenate([vc, s.astype(BF16)], axis=0))
        kd_t = (kt_ref[hh, ci].astype(F32) * kdec).astype(BF16)
        s_ref[hh] = s * cdec + _dot(kd_t, vc)
        return o

    def emit(job, o, second):
        _, hh, _, r, _, _ = job
        vcols = slice(hh * RET_DV, (hh + 1) * RET_DV)
        if not second:
            oacc_ref[hh, r, :] = o
        else:
            o = o + oacc_ref[hh, r, :]
            o = o * lax.rsqrt(jnp.mean(o * o, axis=-1, keepdims=True) + EPS)
            g = g_ref[r, vcols].astype(F32)
            z_ref[r, vcols] = (_silu(g) * o).astype(BF16)

    def body(second):
        def run(t, carry):
            jobs = [scores(ci, hh, d, second) for hh in range(heads)
                    for ci, d in ((t, "f"), (n_chunks - 1 - t, "b"))]
            outs = [advance(job) for job in jobs]
            for job, o in zip(jobs, outs):
                emit(job, o, second)
            return carry
        return run

    lax.fori_loop(0, half, body(False), 0, unroll=8 if half % 8 == 0 else 1)
    lax.fori_loop(half, n_chunks, body(True), 0, unroll=4 if half % 4 == 0 else 1)


def _retention(q, k, v, gsw, dec, s0, *, batch, seq_len):
    n_chunks = seq_len // RET_CHUNK
    assert n_chunks % 2 == 0
    heads = _ret_heads_per_step(seq_len)
    t = batch * seq_len
    st_spec = pl.BlockSpec((None, heads, RET_DK, RET_DV), lambda b, h: (b, h, 0, 0))
    st_shape = jax.ShapeDtypeStruct((batch, RET_HEADS, RET_DK, RET_DV), F32)
    kspec = pl.BlockSpec((seq_len, heads * RET_DK), lambda b, h: (b, h))
    vspec = pl.BlockSpec((seq_len, heads * RET_DV), lambda b, h: (b, h))
    return pl.pallas_call(
        functools.partial(_ret_kernel, n_chunks=n_chunks, heads=heads, zero_init=s0 is None),
        grid=(batch, RET_HEADS // heads),
        in_specs=[pl.BlockSpec(memory_space=pltpu.SMEM), kspec, kspec, vspec, vspec]
        + ([] if s0 is None else [st_spec, st_spec]),
        out_specs=[vspec, st_spec, st_spec],
        out_shape=[jax.ShapeDtypeStruct((t, RET_V_W), BF16), st_shape, st_shape],
        scratch_shapes=[pltpu.VMEM((heads, seq_len, RET_DV), F32),
                        pltpu.VMEM((heads, n_chunks, RET_DK, RET_CHUNK), BF16)],
        compiler_params=_cparams("parallel", "parallel"),
        name=f"retention_l{seq_len}",
    )(dec, q, k, v, gsw, *(s0 or ()))


def _pool_kernel(u_ref, w_ref, sc_ref, o_ref, *, n_tok, width, two_d):
    n_rows = n_tok // width
    pos = lax.broadcasted_iota(I32, (width, POOL_CH), 0)

    def every_row(a):
        return jnp.concatenate([a] * n_rows, axis=0) if n_rows > 1 else a

    def shift_in_row(a, s):
        ok = (pos < width - s) if s > 0 else (pos >= -s)
        return pltpu.roll(a, (-s) % n_tok, axis=0) * every_row(jnp.where(ok, 1.0, 0.0))

    def shift_rows(a, m):
        k = abs(m) * width
        zeros = jnp.zeros((k, POOL_CH), F32)
        return (jnp.concatenate([a[k:], zeros], axis=0) if m > 0
                else jnp.concatenate([zeros, a[:n_tok - k]], axis=0))

    def box_sum(a, half, shift):
        fw = a
        bw = shift(a, -1)
        m = 1
        while m < half:
            fw = fw + shift(fw, m)
            bw = bw + shift(bw, -m)
            m *= 2
        return fw + bw

    def inv_count(p, half, extent):
        return 1.0 / (jnp.minimum(p + half, extent) - jnp.maximum(p - half, 0)).astype(F32)

    for g, window in enumerate(POOL_WINDOWS):
        half = window // 2
        cols = slice(g * POOL_CH, (g + 1) * POOL_CH)
        ug = u_ref[:, cols].astype(F32)
        total = box_sum(ug, half, shift_in_row)
        inv = every_row(inv_count(pos, half, width))
        if two_d:
            total = box_sum(total, half, shift_rows)
            row = lax.broadcasted_iota(I32, (n_rows, 1, POOL_CH), 0)
            inv_r = jnp.broadcast_to(inv_count(row, half, n_rows), (n_rows, width, POOL_CH))
            inv = inv * inv_r.reshape(n_tok, POOL_CH)
        d = (total * inv - ug).astype(BF16)
        o_ref[:, cols] = (_dot(d, w_ref[g]) * sc_ref[:, cols]).astype(BF16)


def _pool(u, pool_w, pool_scale, *, batch, seq_len, on_grid):
    t = batch * seq_len
    width = GRID_W if on_grid else seq_len
    n_tok = seq_len if on_grid else seq_len * math.gcd(batch, 4)
    return pl.pallas_call(
        functools.partial(_pool_kernel, n_tok=n_tok, width=width, two_d=on_grid),
        grid=(t // n_tok,),
        in_specs=[pl.BlockSpec((n_tok, POOL_W), lambda b: (b, 0)),
                  pl.BlockSpec((POOL_GROUPS, POOL_CH, POOL_CH), lambda b: (0, 0, 0)),
                  pl.BlockSpec((1, POOL_W), lambda b: (0, 0))],
        out_specs=pl.BlockSpec((n_tok, POOL_W), lambda b: (b, 0)),
        out_shape=jax.ShapeDtypeStruct((t, POOL_W), BF16),
        compiler_params=_cparams("parallel"),
        name=f"pool_l{seq_len}",
    )(u, pool_w, pool_scale)


def _pack_rows(x):
    half = D_MODEL // 2
    lo = lax.bitcast_convert_type(x[:, :half].astype(BF16).astype(F32), U32) >> 16
    hi = lax.bitcast_convert_type(x[:, half:].astype(BF16).astype(F32), U32) & jnp.uint32(0xFFFF0000)
    word = lax.bitcast_convert_type(hi | lo, I32)
    return [word[:, c * LANES:(c + 1) * LANES] for c in range(N_PIECES)]


def _unpack_rows(pieces):
    words = [lax.bitcast_convert_type(p, U32) for p in pieces]
    lo = [lax.bitcast_convert_type(w << 16, F32) for w in words]
    hi = [lax.bitcast_convert_type(w & jnp.uint32(0xFFFF0000), F32) for w in words]
    return lo, hi


def _merge_kernel(x_ref, z_ref, p_ref, ga_ref, gb_ref, mod_ref, g2_ref, wr_ref, wp_ref, wo_ref,
                  x1_ref, *piece_refs):
    subs = [slice(s * MXU_DIM, (s + 1) * MXU_DIM) for s in range(x_ref.shape[0] // MXU_DIM)]
    branches = [(_dot(z_ref[r, :], wr_ref[...]), _dot(p_ref[r, :], wp_ref[...])) for r in subs]
    merged = [(jax.nn.sigmoid(ga_ref[r, :].astype(F32)) * y_ret
               + jax.nn.sigmoid(gb_ref[r, :].astype(F32)) * y_pool).astype(BF16)
              for r, (y_ret, y_pool) in zip(subs, branches)]
    outs = [_dot(m, wo_ref[...]) for m in merged]
    for r, o in zip(subs, outs):
        x1 = x_ref[r, :] + mod_ref[0, 2:3, :] * o
        x1_ref[r, :] = x1
        h2 = _rms_mod(x1, g2_ref[...], mod_ref[0, 4:5, :], mod_ref[0, 3:4, :])
        for ref, piece in zip(piece_refs, _pack_rows(h2)):
            ref[r, :] = piece


def _merge(x, z, p, ga, gb, mods, norm2_g, w_br_ret, w_br_pool, w_out, *, tokens_per_mod, tm):
    t = x.shape[0]
    tiles_per_mod = tokens_per_mod // tm
    row = lambda w: pl.BlockSpec((tm, w), lambda i: (i, 0))
    full = lambda a: pl.BlockSpec(a.shape, lambda i: (0,) * a.ndim)
    outs = pl.pallas_call(
        _merge_kernel,
        grid=(t // tm,),
        in_specs=[row(D_MODEL), row(RET_V_W), row(POOL_W), row(D_MODEL), row(D_MODEL),
                  pl.BlockSpec((1, 6, D_MODEL), lambda i: (i // tiles_per_mod, 0, 0)),
                  full(norm2_g), full(w_br_ret), full(w_br_pool), full(w_out)],
        out_specs=[row(D_MODEL)] + [row(LANES)] * N_PIECES,
        out_shape=[jax.ShapeDtypeStruct((t, D_MODEL), F32)] + [jax.ShapeDtypeStruct((t, LANES), I32)] * N_PIECES,
        compiler_params=_cparams("parallel"),
        name="merge",
    )(x, z, p, ga, gb, mods, norm2_g, w_br_ret, w_br_pool, w_out)
    return outs[0], outs[1:]


def _route_kernel(*refs):
    h_refs = refs[:N_PIECES]
    rw_ref, bias_ref, idx_ref, rank_ref, wk_ref, cnt_ref, carry_ref = refs[N_PIECES:]
    e = N_EXPERTS
    tm = h_refs[0].shape[0]
    neg = -jnp.inf

    @pl.when(pl.program_id(0) == 0)
    def _():
        carry_ref[...] = jnp.zeros(carry_ref.shape, F32)

    lo, hi = _unpack_rows([r[...] for r in h_refs])
    h = jnp.concatenate(lo + hi, axis=1).astype(BF16)
    logits = lax.dot_general(rw_ref[...], h, (((1,), (1,)), ((), ())), preferred_element_type=F32)[:e]
    scores = jax.nn.sigmoid(logits)
    sel = scores + bias_ref[:e, 0:1]
    e_idx = lax.broadcasted_iota(I32, (e, tm), 0)

    grp = sel.reshape(N_EXPERT_GROUPS, GROUP_SIZE, tm)
    m_idx = lax.broadcasted_iota(I32, grp.shape, 1)
    m1 = jnp.max(grp, axis=1, keepdims=True)
    first = jnp.min(jnp.where(grp == m1, m_idx, GROUP_SIZE), axis=1, keepdims=True)
    m2 = jnp.max(jnp.where(m_idx == first, neg, grp), axis=1, keepdims=True)
    gscore = (m1 + m2).reshape(N_EXPERT_GROUPS, tm)

    g_idx = lax.broadcasted_iota(I32, gscore.shape, 0)
    grank = jnp.zeros(gscore.shape, I32)
    for g in range(N_EXPERT_GROUPS):
        other = gscore[g:g + 1, :]
        beats = jnp.where(other > gscore, 1, jnp.where(other == gscore, (g_idx > g).astype(I32), 0))
        grank = grank + beats
    gkeep = (grank < TOPK_GROUPS).astype(F32)
    ekeep = jnp.broadcast_to(gkeep.reshape(N_EXPERT_GROUPS, 1, tm), grp.shape).reshape(e, tm)
    masked = jnp.where(ekeep > 0, sel, neg)

    chosen = jnp.zeros((e, tm), F32)
    picks, hits = [], []
    for _ in range(TOP_K):
        m = jnp.max(masked, axis=0, keepdims=True)
        pick = jnp.min(jnp.where(masked == m, e_idx, e), axis=0, keepdims=True)
        hit = e_idx == pick
        chosen = jnp.where(hit, 1.0, chosen)
        masked = jnp.where(hit, neg, masked)
        picks.append(pick)
        hits.append(hit)

    w = scores * chosen
    comb = w / jnp.sum(w, axis=0, keepdims=True) * ROUTED_SCALE

    t_row = lax.broadcasted_iota(I32, (tm, tm), 0)
    t_col = lax.broadcasted_iota(I32, (tm, tm), 1)
    before = (t_row < t_col).astype(BF16)
    rankmat = _dot(chosen.astype(BF16), before) + carry_ref[:e, 0:1]
    carry_ref[:e, :] = carry_ref[:e, :] + jnp.sum(chosen, axis=1, keepdims=True)
    cnt_ref[...] = carry_ref[...]

    idx_ref[...] = jnp.concatenate(picks, axis=0)
    rank_ref[...] = jnp.concatenate(
        [jnp.sum(jnp.where(h, rankmat, 0.0), axis=0, keepdims=True) for h in hits], axis=0).astype(I32)
    wk_ref[...] = jnp.concatenate(
        [jnp.sum(jnp.where(h, comb, 0.0), axis=0, keepdims=True) for h in hits], axis=0)


def _route(h2_pieces, router_wt, bias_col, *, tm):
    t = h2_pieces[0].shape[0]
    krow = pl.BlockSpec((TOP_K, tm), lambda i: (0, i))
    return pl.pallas_call(
        _route_kernel,
        grid=(t // tm,),
        in_specs=[pl.BlockSpec((tm, LANES), lambda i: (i, 0))] * N_PIECES
        + [pl.BlockSpec((LANES, D_MODEL), lambda i: (0, 0)), pl.BlockSpec((LANES, 1), lambda i: (0, 0))],
        out_specs=[krow, krow, krow, pl.BlockSpec((LANES, LANES), lambda i: (0, 0))],
        out_shape=[jax.ShapeDtypeStruct((TOP_K, t), I32), jax.ShapeDtypeStruct((TOP_K, t), I32),
                   jax.ShapeDtypeStruct((TOP_K, t), F32), jax.ShapeDtypeStruct((LANES, LANES), F32)],
        scratch_shapes=[pltpu.VMEM((LANES, LANES), F32)],
        compiler_params=_cparams("arbitrary"),
        name="route",
    )(*h2_pieces, router_wt, bias_col)


def _plan_kernel(idx_ref, rank_ref, cnt_ref, pos_ref, te_ref, nv_ref, nu_ref, *, group_tile):
    tf = idx_ref.shape[1]
    nt = te_ref.shape[1]
    cnt = cnt_ref[...].astype(I32)
    padded = (((cnt + (group_tile - 1)) // group_tile) * group_tile).astype(F32)
    e_sub = lax.broadcasted_iota(I32, (LANES, LANES), 0)
    e_lane = lax.broadcasted_iota(I32, (LANES, LANES), 1)
    base = jnp.sum(jnp.where(e_lane < e_sub, padded.T, 0.0), axis=1, keepdims=True)
    end = base + padded[:, 0:1]

    idx = idx_ref[...]
    start = jnp.zeros(idx.shape, F32)
    for e in range(N_EXPERTS):
        start = jnp.where(idx == e, base[e:e + 1, 0:1], start)
    pos = start.astype(I32) + rank_ref[...]
    for j in range(tf // SC_CHUNK):
        pos_ref[j] = pos[:, j * SC_CHUNK:(j + 1) * SC_CHUNK]

    tile_start = (lax.broadcasted_iota(I32, (N_EXPERTS, nt), 1) * group_tile).astype(F32)
    done = jnp.sum(jnp.where(end[:N_EXPERTS] <= tile_start, 1.0, 0.0), axis=0, keepdims=True)
    te_ref[...] = jnp.minimum(done, N_EXPERTS - 1.0).astype(I32)
    in_group = (base[:N_EXPERTS] <= tile_start) & (tile_start < end[:N_EXPERTS])
    real = jnp.clip(base[:N_EXPERTS] + cnt[:N_EXPERTS, 0:1].astype(F32) - tile_start, 0.0, float(group_tile))
    nv_ref[...] = jnp.sum(jnp.where(in_group, real, 0.0), axis=0, keepdims=True).astype(I32)
    total = jnp.sum(padded[:, 0:1], axis=0, keepdims=True)
    nu_ref[...] = jnp.broadcast_to(total * (1.0 / group_tile), nu_ref.shape).astype(I32)


def _plan(idx, rank, counts, *, n_tiles, tf, group_tile):
    t = idx.shape[1]
    nt_pad = -(-n_tiles // LANES) * LANES
    krow = pl.BlockSpec((TOP_K, tf), lambda i: (0, i))
    return pl.pallas_call(
        functools.partial(_plan_kernel, group_tile=group_tile),
        grid=(t // tf,),
        in_specs=[krow, krow, pl.BlockSpec((LANES, LANES), lambda i: (0, 0))],
        out_specs=[pl.BlockSpec((tf // SC_CHUNK, TOP_K, SC_CHUNK), lambda i: (i, 0, 0)),
                   pl.BlockSpec((1, nt_pad), lambda i: (0, 0)),
                   pl.BlockSpec((1, nt_pad), lambda i: (0, 0)),
                   pl.BlockSpec((1, LANES), lambda i: (0, 0))],
        out_shape=[jax.ShapeDtypeStruct((t // SC_CHUNK, TOP_K, SC_CHUNK), I32),
                   jax.ShapeDtypeStruct((1, nt_pad), I32), jax.ShapeDtypeStruct((1, nt_pad), I32),
                   jax.ShapeDtypeStruct((1, LANES), I32)],
        compiler_params=_cparams("arbitrary"),
        name="moe_plan",
    )(idx, rank, counts)


def _sc_mesh_info():
    info = plsc.get_sparse_core_info()
    mesh = plsc.VectorSubcoreMesh(core_axis_name="c", subcore_axis_name="s")
    return mesh, info.num_cores, info.num_cores * info.num_subcores


def _sc_dispatch(pieces, pos, *, n_rows):
    t = pieces[0].shape[0]
    mesh, n_cores, n_workers = _sc_mesh_info()
    per_w = t // SC_CHUNK // n_workers

    @functools.partial(
        pl.kernel, mesh=mesh,
        out_type=[jax.ShapeDtypeStruct((n_rows, LANES), I32)] * N_PIECES,
        scratch_types=[pltpu.VMEM((TOP_K, SC_CHUNK), I32),
                       pltpu.VMEM((N_PIECES, SC_CHUNK, LANES), I32),
                       pltpu.SemaphoreType.DMA((N_PIECES,)),
                       pltpu.SemaphoreType.DMA],
        name="sc_dispatch",
    )
    def run(*refs):
        src = refs[:N_PIECES]
        pos_hbm = refs[N_PIECES]
        dst = refs[N_PIECES + 1:2 * N_PIECES + 1]
        idx_v, rows_v, load_sem, put_sem = refs[2 * N_PIECES + 1:]
        wid = lax.axis_index("s") * n_cores + lax.axis_index("c")

        @pl.loop(0, per_w)
        def _(j):
            ch = wid * per_w + j
            t0 = pl.multiple_of(ch * SC_CHUNK, SC_CHUNK)
            loads = [pltpu.make_async_copy(src[c].at[pl.ds(t0, SC_CHUNK)], rows_v.at[c], load_sem.at[c])
                     for c in range(N_PIECES)]
            for ld in loads:
                ld.start()
            pltpu.sync_copy(pos_hbm.at[ch], idx_v)
            puts = []
            for c in range(N_PIECES):
                loads[c].wait()
                for k in range(TOP_K):
                    puts.append(pltpu.make_async_copy(rows_v.at[c], dst[c].at[idx_v.at[k]], put_sem))
                    puts[-1].start()
            for cp in puts:
                cp.wait()

    return run(*pieces, pos)


SC_GROUP = 32


def _sc_combine(pieces, pos, wts, *, n_tokens):
    mesh, n_cores, n_workers = _sc_mesh_info()
    lanes = plsc.get_sparse_core_info().num_lanes
    per_w = n_tokens // SC_GROUP // n_workers

    @functools.partial(
        pl.kernel, mesh=mesh,
        out_type=[jax.ShapeDtypeStruct((n_tokens, LANES), I32)] * N_PIECES,
        scratch_types=[pltpu.VMEM((TOP_K, SC_GROUP), I32),
                       pltpu.VMEM((TOP_K, SC_GROUP), F32),
                       pltpu.VMEM((2, TOP_K, SC_GROUP, LANES), I32),
                       pltpu.VMEM((2, SC_GROUP, LANES), I32),
                       pltpu.SemaphoreType.DMA((2,)),
                       pltpu.SemaphoreType.DMA((2,))],
        compiler_params=pltpu.CompilerParams(needs_layout_passes=False),
        name="sc_combine",
    )
    def run(*refs):
        src = refs[:N_PIECES]
        pos_hbm, wts_hbm = refs[N_PIECES:N_PIECES + 2]
        dst = refs[N_PIECES + 2:2 * N_PIECES + 2]
        idx_v, w_v, buf, acc, get_sem, put_sem = refs[2 * N_PIECES + 2:]
        wid = lax.axis_index("s") * n_cores + lax.axis_index("c")

        @pl.loop(0, per_w)
        def _(j):
            grp = wid * per_w + j
            t0 = pl.multiple_of(grp * SC_GROUP, SC_GROUP)
            pltpu.sync_copy(pos_hbm.at[grp], idx_v)
            pltpu.sync_copy(wts_hbm.at[grp], w_v)

            def gets(c, slot):
                return [pltpu.make_async_copy(src[c].at[idx_v.at[k]], buf.at[slot, k], get_sem.at[slot])
                        for k in range(TOP_K)]

            def puts(c, slot):
                return [pltpu.make_async_copy(acc.at[slot], dst[c].at[pl.ds(t0, SC_GROUP)], put_sem.at[slot])]

            for cp in gets(0, 0):
                cp.start()
            for c in range(N_PIECES):
                slot = c % 2
                if c + 1 < N_PIECES:
                    for cp in gets(c + 1, 1 - slot):
                        cp.start()
                for cp in gets(c, slot):
                    cp.wait()
                if c >= 2:
                    for cp in puts(c - 2, slot):
                        cp.wait()

                @pl.loop(0, SC_GROUP)
                def _(r):
                    row = jnp.full((lanes,), r, I32)
                    w = []
                    for k in range(TOP_K):
                        w_k = plsc.load_gather(w_v, [jnp.full((lanes,), k, I32), row])
                        w.append(plsc.pack(w_k, w_k, format=plsc.PackFormat.INTERLEAVED))
                    for q in range(LANES // lanes):
                        cols = pl.ds(q * lanes, lanes)
                        total = None
                        for k in range(TOP_K):
                            term = plsc.bitcast(buf[slot, k, r, cols], BF16) * w[k]
                            total = term if total is None else total + term
                        acc[slot, r, cols] = plsc.bitcast(total, I32)

                for cp in puts(c, slot):
                    cp.start()
            for c in range(N_PIECES - 2, N_PIECES):
                for cp in puts(c, c % 2):
                    cp.wait()

    return run(*pieces, pos, wts)


def _group_tile(n_tokens):
    per_expert = n_tokens * TOP_K // N_EXPERTS
    return max(MXU_DIM, min(4 * MXU_DIM, per_expert // MXU_DIM * MXU_DIM))


SECOND_DMA_QUEUE = 1


def _experts_kernel(te_ref, nv_ref, nu_ref, *refs, tile):
    x_hbm = refs[:N_PIECES]
    wg_ref, wu_ref, wd_ref = refs[N_PIECES:N_PIECES + 3]
    y_hbm = refs[N_PIECES + 3:2 * N_PIECES + 3]
    xbuf, xsem, ybuf, ysem = refs[2 * N_PIECES + 3:]
    i = pl.program_id(0)
    n_used = nu_ref[0]
    slot = lax.rem(i, 2)

    half = tile // 2

    def x_copy(step, into, h, c):
        r = pl.ds(pl.multiple_of(step * tile + h * half, half), half)
        return pltpu.make_async_copy(x_hbm[c].at[r], xbuf.at[into, c, pl.ds(h * half, half)], xsem.at[into, c])

    def y_copy(step, out_of, h, c):
        r = pl.ds(pl.multiple_of(step * tile + h * half, half), half)
        return pltpu.make_async_copy(ybuf.at[out_of, c, pl.ds(h * half, half)], y_hbm[c].at[r], ysem.at[out_of, c])

    def real_halves(copy, step, buf, act):
        for c in range(N_PIECES):
            act(copy(step, buf, 0, c))

        @pl.when(nv_ref[step] > half)
        def _():
            for c in range(N_PIECES):
                act(copy(step, buf, 1, c))

    start = lambda cp: cp.start(priority=SECOND_DMA_QUEUE)
    wait = lambda cp: cp.wait()

    @pl.when(i == 0)
    def _():
        real_halves(x_copy, 0, 0, start)

    @pl.when(i + 1 < n_used)
    def _():
        real_halves(x_copy, i + 1, 1 - slot, start)

    @pl.when(i < n_used)
    def _():
        real_halves(x_copy, i, slot, wait)

        def mlp(n_sub):
            subs = [slice(s * MXU_DIM, (s + 1) * MXU_DIM) for s in range(n_sub)]
            xs = []
            for rows in subs:
                lo, hi = _unpack_rows([xbuf[slot, c, rows, :] for c in range(N_PIECES)])
                xs.append(jnp.concatenate(lo + hi, axis=1).astype(BF16))
            gates = [(_dot(x, wg_ref[...]), _dot(x, wu_ref[...])) for x in xs]
            ys = [_dot((_silu(g) * u).astype(BF16), wd_ref[...]) for g, u in gates]
            for rows, y in zip(subs, ys):
                for c, piece in enumerate(_pack_rows(y)):
                    ybuf[slot, c, rows, :] = piece

        n_sub = tile // MXU_DIM
        pl.when(nv_ref[i] > half)(lambda: mlp(n_sub))
        pl.when(nv_ref[i] <= half)(lambda: mlp(max(n_sub // 2, 1)))

        @pl.when(i >= 1)
        def _():
            real_halves(y_copy, i - 1, 1 - slot, wait)

        real_halves(y_copy, i, slot, start)

        @pl.when(i == n_used - 1)
        def _():
            real_halves(y_copy, i, slot, wait)


def _experts(x_pieces, tile_expert, tile_rows, n_used, wg, wu, wd, *, group_tile):
    n_rows = x_pieces[0].shape[0]
    n_tiles = n_rows // group_tile

    wspec = lambda a: pl.BlockSpec((None,) + a.shape[1:],
                                   lambda i, te, nv, nu: (te[jnp.minimum(i, nu[0] - 1)], 0, 0))
    return pl.pallas_call(
        functools.partial(_experts_kernel, tile=group_tile),
        grid_spec=pltpu.PrefetchScalarGridSpec(
            num_scalar_prefetch=3,
            grid=(n_tiles,),
            in_specs=[pl.BlockSpec(memory_space=pl.ANY)] * N_PIECES + [wspec(wg), wspec(wu), wspec(wd)],
            out_specs=[pl.BlockSpec(memory_space=pl.ANY)] * N_PIECES,
            scratch_shapes=[pltpu.VMEM((2, N_PIECES, group_tile, LANES), I32),
                            pltpu.SemaphoreType.DMA((2, N_PIECES))] * 2),
        out_shape=[jax.ShapeDtypeStruct((n_rows, LANES), I32)] * N_PIECES,
        compiler_params=_cparams("arbitrary"),
        name="experts",
    )(tile_expert, tile_rows, n_used, *x_pieces, wg, wu, wd)


def _moe_out_kernel(x1_ref, mod_ref, fg_ref, sg_ref, su_ref, sd_ref, *refs):
    h_refs = refs[:N_PIECES]
    routed_refs = refs[N_PIECES:2 * N_PIECES]
    out_ref = refs[2 * N_PIECES]
    h_lo, h_hi = _unpack_rows([r[...] for r in h_refs])
    h = jnp.concatenate(h_lo + h_hi, axis=1).astype(BF16)
    hid = _silu(_dot(h, sg_ref[...])) * _dot(h, su_ref[...])
    shared = _dot(hid.astype(BF16), sd_ref[...])
    r_lo, r_hi = _unpack_rows([r[...] for r in routed_refs])
    routed = jnp.concatenate(r_lo + r_hi, axis=1)
    x2 = x1_ref[...] + mod_ref[0, 5:6, :] * (shared + routed)
    out_ref[...] = x2 * lax.rsqrt(jnp.mean(x2 * x2, axis=-1, keepdims=True) + EPS) * fg_ref[...]


def _moe_out(h2_pieces, x1, mods, final_g, sg, su, sd, routed_pieces, *, tokens_per_mod, tm):
    t = x1.shape[0]
    tiles_per_mod = tokens_per_mod // tm
    row = lambda w: pl.BlockSpec((tm, w), lambda i: (i, 0))
    full = lambda a: pl.BlockSpec(a.shape, lambda i: (0,) * a.ndim)
    return pl.pallas_call(
        _moe_out_kernel,
        grid=(t // tm,),
        in_specs=[row(D_MODEL),
                  pl.BlockSpec((1, 6, D_MODEL), lambda i: (i // tiles_per_mod, 0, 0)),
                  full(final_g), full(sg), full(su), full(sd)]
        + [row(LANES)] * (2 * N_PIECES),
        out_specs=row(D_MODEL),
        out_shape=jax.ShapeDtypeStruct((t, D_MODEL), F32),
        compiler_params=_cparams("parallel"),
        name="moe_out",
    )(x1, mods, final_g, sg, su, sd, *h2_pieces, *routed_pieces)


def _trunk(x, mods, s0, w, expert_w, *, batch, seq_len, on_grid):
    t = batch * seq_len
    tokens_per_mod = t // mods.shape[0]
    cos_t, sin_t = _rope_tables(max(seq_len, PROJ_TILE))
    to_cast = expert_w if expert_w[0].dtype != BF16 else ()
    (q, k, v, gsw, up, ga, gb), casted = _inproj(x, mods, w["norm1_g"], w["w_in"], cos_t, sin_t, to_cast,
                                                 tokens_per_mod=tokens_per_mod, seq_len=seq_len,
                                                 on_grid=on_grid, tm=PROJ_TILE)
    expert_w = casted or expert_w
    z, s_f, s_b = _retention(q, k, v, gsw, w["dec"], s0, batch=batch, seq_len=seq_len)
    p = _pool(up, w["pool_w"], w["pool_scale"], batch=batch, seq_len=seq_len, on_grid=on_grid)
    x1, h2_pieces = _merge(x, z, p, ga, gb, mods, w["norm2_g"], w["w_br_ret"], w["w_br_pool"],
                               w["w_out"], tokens_per_mod=tokens_per_mod, tm=PROJ_TILE)

    group_tile = _group_tile(t)
    n_rows = t * TOP_K + N_EXPERTS * group_tile
    idx, rank, wts, counts = _route(h2_pieces, w["router_wt"], w["router_bias"], tm=ROUTE_TILE)
    pos, tile_expert, tile_rows, n_used = _plan(idx, rank, counts, n_tiles=n_rows // group_tile, tf=PLAN_TILE,
                                                group_tile=group_tile)
    x_sorted = _sc_dispatch(h2_pieces, pos, n_rows=n_rows)
    y_sorted = _experts(x_sorted, tile_expert.reshape(-1), tile_rows.reshape(-1), n_used.reshape(-1),
                        *expert_w, group_tile=group_tile)
    regroup = lambda a: a.reshape(TOP_K, t // SC_GROUP, SC_GROUP).transpose(1, 0, 2)
    pos_rows = pos.transpose(1, 0, 2).reshape(TOP_K, t)
    routed = _sc_combine(y_sorted, regroup(pos_rows), regroup(wts), n_tokens=t)
    y = _moe_out(h2_pieces, x1, mods, w["final_g"], w["sh_w_gate"], w["sh_w_up"], w["sh_w_down"], routed,
                 tokens_per_mod=tokens_per_mod, tm=OUT_TILE)
    return y, s_f, s_b, expert_w


def kernel(x_prompt, x_sample, state_ret_fwd, state_ret_bwd, c, c_ctx, ada_w, ada_b, norm1_g, norm2_g, w_in,
           ret_decay_fwd, ret_decay_bwd, w_br_ret, pool_w, pool_scale, w_br_pool, w_out, router_w, router_bias,
           exp_w_gate, exp_w_up, exp_w_down, sh_w_gate, sh_w_up, sh_w_down, final_norm_g):
    n_req, seq, d = x_prompt.shape
    n_dec, dec_seq, _ = x_sample.shape
    depth = ada_w.shape[0]
    assert depth == 1 and d == D_MODEL

    xc = x_prompt.reshape(n_req * seq, d)
    xs = x_sample.reshape(n_dec * dec_seq, d)
    new_f, new_b = [], []
    for l in range(depth):
        c_rows = jnp.concatenate([c_ctx[None, :], c, jnp.zeros((8 - 1 - n_dec, d), F32)], axis=0)
        mods = _ada(c_rows, ada_w[l], ada_b[l]).reshape(8, 6, d)
        pad_rows = LANES - N_EXPERTS
        w = dict(
            norm1_g=norm1_g[l].reshape(1, d), norm2_g=norm2_g[l].reshape(1, d),
            final_g=final_norm_g.reshape(1, d),
            w_in=w_in[l].astype(BF16),
            dec=jnp.stack([ret_decay_fwd[l], ret_decay_bwd[l]]).astype(F32),
            w_br_ret=w_br_ret[l].astype(BF16), pool_w=pool_w[l].astype(BF16),
            pool_scale=pool_scale[l].reshape(1, POOL_W), w_br_pool=w_br_pool[l].astype(BF16),
            w_out=w_out[l].astype(BF16),
            router_wt=jnp.pad(router_w[l].T, ((0, pad_rows), (0, 0))).astype(BF16),
            router_bias=jnp.pad(router_bias[l].astype(F32).reshape(N_EXPERTS, 1), ((0, pad_rows), (0, 0))),
            sh_w_gate=sh_w_gate[l].astype(BF16),
            sh_w_up=sh_w_up[l].astype(BF16), sh_w_down=sh_w_down[l].astype(BF16),
        )
        cached = (state_ret_fwd[:, l].astype(F32), state_ret_bwd[:, l].astype(F32))
        expert_w = (exp_w_gate[l], exp_w_up[l], exp_w_down[l])
        xs, _, _, expert_w = _trunk(xs, mods[1:1 + n_dec], cached, w, expert_w,
                                    batch=n_dec, seq_len=dec_seq, on_grid=True)
        xc, s_f, s_b, _ = _trunk(xc, mods[0:1], None, w, expert_w, batch=n_req, seq_len=seq, on_grid=False)
        new_f.append(s_f)
        new_b.append(s_b)
    y_prompt = xc.reshape(n_req, seq, d)
    y_sample = xs.reshape(n_dec, dec_seq, d)
    return (y_prompt, y_sample, jnp.stack(new_f, axis=1).astype(x_prompt.dtype),
            jnp.stack(new_b, axis=1).astype(x_prompt.dtype))
```

```python
import functools
import math

import numpy as np
import jax
import jax.numpy as jnp
from jax import lax
from jax.experimental import pallas as pl
from jax.experimental.pallas import tpu as pltpu
from jax.experimental.pallas import tpu_sc as plsc

D_MODEL = 1024
GRID_W = 64
RET_HEADS = 4
RET_DK = 128
RET_DV = 256
RET_QK_W = RET_HEADS * RET_DK
RET_V_W = RET_HEADS * RET_DV
RET_CHUNK = 128
ROPE_BASE = 10000.0
POOL_GROUPS = 4
POOL_CH = 128
POOL_W = POOL_GROUPS * POOL_CH
POOL_WINDOWS = (2, 4, 8, 16)
N_EXPERTS = 64
TOP_K = 8
N_EXPERT_GROUPS = 8
GROUP_SIZE = N_EXPERTS // N_EXPERT_GROUPS
TOPK_GROUPS = 4
D_EXPERT = 256
ROUTED_SCALE = 2.5
EPS = 1e-6
IN_SIZES = (RET_QK_W, RET_QK_W, RET_V_W, RET_V_W, POOL_W, D_MODEL, D_MODEL)
IN_OFFS = tuple(sum(IN_SIZES[:i]) for i in range(len(IN_SIZES) + 1))
IN_W = IN_OFFS[-1]

LANES = 128
VMEM_LIMIT = 56 << 20
N_PIECES = D_MODEL // 2 // LANES
MXU_DIM = 256
SC_CHUNK = 128
PROJ_TILE = 512
PLAN_TILE = 2048
OUT_TILE = 1024
POOL_SEQS_PER_STEP = 4
HIGH_HALF = 0xFFFF0000

F32 = jnp.float32
BF16 = jnp.bfloat16
I32 = jnp.int32
U32 = jnp.uint32


def _cparams(*sem):
    return pltpu.CompilerParams(dimension_semantics=sem, vmem_limit_bytes=VMEM_LIMIT)


def _dot(a, b):
    return jnp.dot(a, b, preferred_element_type=F32)


def _silu(x):
    return x * jax.nn.sigmoid(x)


def _rms_mod(x, g, scale, shift):
    y = x * lax.rsqrt(jnp.mean(x * x, axis=-1, keepdims=True) + EPS)
    return (y * g) * (1.0 + scale) + shift


def _ada_kernel(c_ref, w_ref, b_ref, o_ref):
    c = c_ref[...]
    o_ref[...] = jnp.dot(_silu(c), w_ref[...], preferred_element_type=F32,
                         precision=lax.Precision.HIGHEST) + b_ref[...]


def _ada(c_rows, ada_w, ada_b):
    r = c_rows.shape[0]
    n = ada_w.shape[1]
    tn = 2 * D_MODEL
    return pl.pallas_call(
        _ada_kernel,
        grid=(n // tn,),
        in_specs=[pl.BlockSpec((r, D_MODEL), lambda j: (0, 0)),
                  pl.BlockSpec((D_MODEL, tn), lambda j: (0, j)),
                  pl.BlockSpec((1, tn), lambda j: (0, j))],
        out_specs=pl.BlockSpec((r, tn), lambda j: (0, j)),
        out_shape=jax.ShapeDtypeStruct((r, n), F32),
        compiler_params=_cparams("parallel"),
        name="ada_mod",
    )(c_rows, ada_w, ada_b.reshape(1, n))


def _inproj_kernel(x_ref, mod_ref, g_ref, w_ref, cos_ref, sin_ref, *refs, on_grid):
    n_side = (len(refs) - len(IN_SIZES)) // 2
    side_in = refs[:n_side]
    q_ref, k_ref, v_ref, gsw_ref, up_ref, ga_ref, gb_ref = refs[n_side:n_side + len(IN_SIZES)]
    side_out = refs[n_side + len(IN_SIZES):]
    for src, dst in zip(side_in, side_out):
        dst[...] = src[...].astype(BF16)
    for s in range(x_ref.shape[0] // MXU_DIM):
        rows = slice(s * MXU_DIM, (s + 1) * MXU_DIM)
        h = _rms_mod(x_ref[rows, :], g_ref[...], mod_ref[0, 1:2, :], mod_ref[0, 0:1, :]).astype(BF16)

        def seg(i):
            return _dot(h, w_ref[:, IN_OFFS[i]:IN_OFFS[i + 1]])

        q = seg(0)
        k = seg(1)
        if on_grid:
            cos = jnp.concatenate([cos_ref[rows, :]] * RET_HEADS, axis=1)
            sin = jnp.concatenate([sin_ref[rows, :]] * RET_HEADS, axis=1)
            quarter = RET_DK // 4
            lane = lax.broadcasted_iota(jnp.int32, q.shape, 1)
            first = (lane & (2 * quarter - 1)) < quarter

            def rope(a):
                up = pltpu.roll(a, RET_QK_W - quarter, axis=1)
                dn = pltpu.roll(a, quarter, axis=1)
                return a * cos + jnp.where(first, up, dn) * sin

            q = rope(q)
            k = rope(k)
        q_ref[rows, :] = q.astype(BF16)
        k_ref[rows, :] = (k * (RET_DK ** -0.5)).astype(BF16)
        v_ref[rows, :] = seg(2).astype(BF16)
        gsw_ref[rows, :] = seg(3).astype(BF16)
        up_ref[rows, :] = seg(4).astype(BF16)
        ga_ref[rows, :] = seg(5).astype(BF16)
        gb_ref[rows, :] = seg(6).astype(BF16)


def _inproj(x, mods, norm_g, w_in, cos_t, sin_t, side_cast=(), *, tokens_per_mod, seq_len, on_grid, tm):
    t = x.shape[0]
    steps = t // tm
    tiles_per_mod = tokens_per_mod // tm
    tiles_per_seq = max(seq_len // tm, 1)
    widths = IN_SIZES
    out_shape = [jax.ShapeDtypeStruct((t, w), BF16) for w in widths]
    out_specs = [pl.BlockSpec((tm, w), lambda i: (i, 0)) for w in widths]
    side_specs = [pl.BlockSpec((a.shape[0] // steps,) + a.shape[1:], lambda i: (i, 0, 0)) for a in side_cast]
    outs = pl.pallas_call(
        functools.partial(_inproj_kernel, on_grid=on_grid),
        grid=(steps,),
        in_specs=[pl.BlockSpec((tm, D_MODEL), lambda i: (i, 0)),
                  pl.BlockSpec((1, 6, D_MODEL), lambda i: (i // tiles_per_mod, 0, 0)),
                  pl.BlockSpec((1, D_MODEL), lambda i: (0, 0)),
                  pl.BlockSpec((D_MODEL, IN_W), lambda i: (0, 0), pipeline_mode=pl.Buffered(1)),
                  pl.BlockSpec((tm, RET_DK), lambda i: (i % tiles_per_seq, 0)),
                  pl.BlockSpec((tm, RET_DK), lambda i: (i % tiles_per_seq, 0))] + side_specs,
        out_specs=out_specs + side_specs,
        out_shape=out_shape + [jax.ShapeDtypeStruct(a.shape, BF16) for a in side_cast],
        compiler_params=_cparams("parallel"),
        name="inproj_grid" if on_grid else "inproj_seq",
    )(x, mods, norm_g, w_in, cos_t, sin_t, *side_cast)
    return outs[:len(widths)], tuple(outs[len(widths):])


def _rope_tables(seq_len):
    t = np.arange(seq_len)
    row = (t // GRID_W).astype(np.float32)
    col = (t % GRID_W).astype(np.float32)
    m = RET_DK // 4
    inv = (np.float32(ROPE_BASE) ** (-np.arange(m, dtype=np.float32) / np.float32(m))).astype(np.float32)
    ar = row[:, None] * inv
    ac = col[:, None] * inv
    cos = np.concatenate([np.cos(ar), np.cos(ar), np.cos(ac), np.cos(ac)], axis=1)
    sin = np.concatenate([-np.sin(ar), np.sin(ar), -np.sin(ac), np.sin(ac)], axis=1)
    return jnp.asarray(cos, F32), jnp.asarray(sin, F32)


def _ret_heads_per_step(seq_len):
    per_head = seq_len * (2 * 2 * (2 * RET_DK + 3 * RET_DV) + 4 * RET_DV + 2 * RET_DK)
    heads = RET_HEADS
    while heads > 1 and heads * per_head > VMEM_LIMIT * 3 // 4:
        heads //= 2
    return heads


def _ret_kernel(dec_ref, q_ref, k_ref, v_ref, g_ref, *refs, n_chunks, heads, zero_init):
    s0_refs = () if zero_init else refs[:2]
    z_ref, sf_ref, sb_ref, oacc_ref, kt_ref = refs[len(s0_refs):]
    c = RET_CHUNK
    half = n_chunks // 2
    ii = lax.broadcasted_iota(I32, (c, c), 0)
    jj = lax.broadcasted_iota(I32, (c, c), 1)
    ik = lax.broadcasted_iota(I32, (c, RET_DK), 0).astype(F32)
    jk = lax.broadcasted_iota(I32, (RET_DK, c), 1).astype(F32)

    def log_gamma(d, shape):
        return jnp.log1p(-jnp.exp2(-jnp.full(shape, d, F32)))

    consts = {}
    for hh in range(heads):
        h = pl.program_id(1) * heads + hh
        dec_f = dec_ref[0, h]
        dec_b = dec_ref[1, h]
        rel = (ii - jj).astype(F32)
        consts[hh, "f"] = (
            jnp.where(rel >= 0, jnp.exp(log_gamma(dec_f, (c, c)) * jnp.maximum(rel, 0.0)), 0.0),
            jnp.exp(log_gamma(dec_f, (c, RET_DK)) * (ik + 1.0)),
            jnp.exp(log_gamma(dec_f, (RET_DK, c)) * (c - 1.0 - jk)),
            jnp.exp(log_gamma(dec_f, (RET_DK, RET_DV)) * c))
        consts[hh, "b"] = (
            jnp.where(rel <= 0, jnp.exp(log_gamma(dec_b, (c, c)) * jnp.maximum(-rel, 0.0)), 0.0),
            jnp.exp(log_gamma(dec_b, (c, RET_DK)) * (c - ik)),
            jnp.exp(log_gamma(dec_b, (RET_DK, c)) * jk),
            jnp.exp(log_gamma(dec_b, (RET_DK, RET_DV)) * c))

    for s_ref, s0_ref in zip((sf_ref, sb_ref), s0_refs or (None, None)):
        s_ref[...] = jnp.zeros(s_ref.shape, F32) if zero_init else s0_ref[...]

    def scores(ci, hh, direction, second):
        r = pl.ds(pl.multiple_of(ci * c, c), c)
        kcols = slice(hh * RET_DK, (hh + 1) * RET_DK)
        qc = q_ref[r, kcols]
        kc = k_ref[r, kcols]
        if not second:
            kt_ref[hh, ci] = kc.T
        sc = lax.dot_general(qc, kc, (((1,), (1,)), ((), ())), preferred_element_type=F32)
        return ci, hh, direction, r, qc, sc

    def advance(job):
        ci, hh, direction, r, qc, sc = job
        dmask, qdec, kdec, cdec = consts[hh, direction]
        s_ref = sf_ref if direction == "f" else sb_ref
        vc = v_ref[r, hh * RET_DV:(hh + 1) * RET_DV]
        s = s_ref[hh]
        lhs = jnp.concatenate([(sc * dmask).astype(BF16), (qc.astype(F32) * qdec).astype(BF16)], axis=1)
        o = _dot(lhs, jnp.concatenate([vc, s.astype(BF16)], axis=0))
        kd_t = (kt_ref[hh, ci].astype(F32) * kdec).astype(BF16)
        s_ref[hh] = s * cdec + _dot(kd_t, vc)
        return o

    def emit(job, o, second):
        _, hh, _, r, _, _ = job
        vcols = slice(hh * RET_DV, (hh + 1) * RET_DV)
        if not second:
            oacc_ref[hh, r, :] = o
        else:
            o = o + oacc_ref[hh, r, :]
            o = o * lax.rsqrt(jnp.mean(o * o, axis=-1, keepdims=True) + EPS)
            g = g_ref[r, vcols].astype(F32)
            z_ref[r, vcols] = (_silu(g) * o).astype(BF16)

    def body(second):
        def run(t, carry):
            jobs = [scores(ci, hh, d, second) for hh in range(heads)
                    for ci, d in ((t, "f"), (n_chunks - 1 - t, "b"))]
            outs = [advance(job) for job in jobs]
            for job, o in zip(jobs, outs):
                emit(job, o, second)
            return carry
        return run

    lax.fori_loop(0, half, body(False), 0, unroll=8 if half % 8 == 0 else 1)
    lax.fori_loop(half, n_chunks, body(True), 0, unroll=4 if half % 4 == 0 else 1)


def _retention(q, k, v, gsw, dec, s0, *, batch, seq_len):
    n_chunks = seq_len // RET_CHUNK
    assert n_chunks % 2 == 0
    heads = _ret_heads_per_step(seq_len)
    t = batch * seq_len
    st_spec = pl.BlockSpec((None, heads, RET_DK, RET_DV), lambda b, h: (b, h, 0, 0))
    st_shape = jax.ShapeDtypeStruct((batch, RET_HEADS, RET_DK, RET_DV), F32)
    kspec = pl.BlockSpec((seq_len, heads * RET_DK), lambda b, h: (b, h))
    vspec = pl.BlockSpec((seq_len, heads * RET_DV), lambda b, h: (b, h))
    return pl.pallas_call(
        functools.partial(_ret_kernel, n_chunks=n_chunks, heads=heads, zero_init=s0 is None),
        grid=(batch, RET_HEADS // heads),
        in_specs=[pl.BlockSpec(memory_space=pltpu.SMEM), kspec, kspec, vspec, vspec]
        + ([] if s0 is None else [st_spec, st_spec]),
        out_specs=[vspec, st_spec, st_spec],
        out_shape=[jax.ShapeDtypeStruct((t, RET_V_W), BF16), st_shape, st_shape],
        scratch_shapes=[pltpu.VMEM((heads, seq_len, RET_DV), F32),
                        pltpu.VMEM((heads, n_chunks, RET_DK, RET_CHUNK), BF16)],
        compiler_params=_cparams("parallel", "parallel"),
        name=f"retention_l{seq_len}",
    )(dec, q, k, v, gsw, *(s0 or ()))


def _pool_kernel(u_ref, w_ref, sc_ref, o_ref, *, n_tok, width, two_d):
    n_rows = n_tok // width
    pos = lax.broadcasted_iota(I32, (width, POOL_CH), 0)

    def every_row(a):
        return jnp.concatenate([a] * n_rows, axis=0) if n_rows > 1 else a

    def shift_in_row(a, s):
        ok = (pos < width - s) if s > 0 else (pos >= -s)
        return pltpu.roll(a, (-s) % n_tok, axis=0) * every_row(jnp.where(ok, 1.0, 0.0))

    def shift_rows(a, m):
        k = abs(m) * width
        zeros = jnp.zeros((k, POOL_CH), F32)
        return (jnp.concatenate([a[k:], zeros], axis=0) if m > 0
                else jnp.concatenate([zeros, a[:n_tok - k]], axis=0))

    def box_sum(a, half, shift):
        fw = a
        bw = shift(a, -1)
        m = 1
        while m < half:
            fw = fw + shift(fw, m)
            bw = bw + shift(bw, -m)
            m *= 2
        return fw + bw

    def inv_count(p, half, extent):
        return 1.0 / (jnp.minimum(p + half, extent) - jnp.maximum(p - half, 0)).astype(F32)

    for g, window in enumerate(POOL_WINDOWS):
        half = window // 2
        cols = slice(g * POOL_CH, (g + 1) * POOL_CH)
        ug = u_ref[:, cols].astype(F32)
        total = box_sum(ug, half, shift_in_row)
        inv = every_row(inv_count(pos, half, width))
        if two_d:
            total = box_sum(total, half, shift_rows)
            row = lax.broadcasted_iota(I32, (n_rows, 1, POOL_CH), 0)
            inv_r = jnp.broadcast_to(inv_count(row, half, n_rows), (n_rows, width, POOL_CH))
            inv = inv * inv_r.reshape(n_tok, POOL_CH)
        d = (total * inv - ug).astype(BF16)
        o_ref[:, cols] = (_dot(d, w_ref[g]) * sc_ref[:, cols]).astype(BF16)


def _pool(u, pool_w, pool_scale, *, batch, seq_len, on_grid):
    t = batch * seq_len
    width = GRID_W if on_grid else seq_len
    n_tok = seq_len if on_grid else seq_len * math.gcd(batch, POOL_SEQS_PER_STEP)
    return pl.pallas_call(
        functools.partial(_pool_kernel, n_tok=n_tok, width=width, two_d=on_grid),
        grid=(t // n_tok,),
        in_specs=[pl.BlockSpec((n_tok, POOL_W), lambda b: (b, 0)),
                  pl.BlockSpec((POOL_GROUPS, POOL_CH, POOL_CH), lambda b: (0, 0, 0)),
                  pl.BlockSpec((1, POOL_W), lambda b: (0, 0))],
        out_specs=pl.BlockSpec((n_tok, POOL_W), lambda b: (b, 0)),
        out_shape=jax.ShapeDtypeStruct((t, POOL_W), BF16),
        compiler_params=_cparams("parallel"),
        name=f"pool_l{seq_len}",
    )(u, pool_w, pool_scale)


def _pack_rows(x):
    half = D_MODEL // 2
    lo = lax.bitcast_convert_type(x[:, :half].astype(BF16).astype(F32), U32) >> 16
    hi = lax.bitcast_convert_type(x[:, half:].astype(BF16).astype(F32), U32) & jnp.uint32(HIGH_HALF)
    word = lax.bitcast_convert_type(hi | lo, I32)
    return [word[:, c * LANES:(c + 1) * LANES] for c in range(N_PIECES)]


def _unpack_rows(pieces):
    words = [lax.bitcast_convert_type(p, U32) for p in pieces]
    lo = [lax.bitcast_convert_type(w << 16, F32) for w in words]
    hi = [lax.bitcast_convert_type(w & jnp.uint32(HIGH_HALF), F32) for w in words]
    return lo, hi


def _merge_kernel(x_ref, z_ref, p_ref, ga_ref, gb_ref, mod_ref, g2_ref, wr_ref, wp_ref, wo_ref, rw_ref, bias_ref,
                  x1_ref, *refs):
    piece_refs, route_refs = refs[:N_PIECES], refs[N_PIECES:]
    subs = [slice(s * MXU_DIM, (s + 1) * MXU_DIM) for s in range(x_ref.shape[0] // MXU_DIM)]
    branches = [(_dot(z_ref[r, :], wr_ref[...]), _dot(p_ref[r, :], wp_ref[...])) for r in subs]
    merged = [(jax.nn.sigmoid(ga_ref[r, :].astype(F32)) * y_ret
               + jax.nn.sigmoid(gb_ref[r, :].astype(F32)) * y_pool).astype(BF16)
              for r, (y_ret, y_pool) in zip(subs, branches)]
    outs = [_dot(m, wo_ref[...]) for m in merged]
    logits = []
    for r, o in zip(subs, outs):
        x1 = x_ref[r, :] + mod_ref[0, 2:3, :] * o
        x1_ref[r, :] = x1
        h2 = _rms_mod(x1, g2_ref[...], mod_ref[0, 4:5, :], mod_ref[0, 3:4, :])
        logits.append(lax.dot_general(rw_ref[...], h2.astype(BF16), (((1,), (1,)), ((), ())),
                                      preferred_element_type=F32)[:N_EXPERTS])
        for ref, piece in zip(piece_refs, _pack_rows(h2)):
            ref[r, :] = piece
    _route_tile(jnp.concatenate(logits, axis=1), bias_ref, *route_refs)


def _merge(x, z, p, ga, gb, mods, norm2_g, w_br_ret, w_br_pool, w_out, router_wt, bias_col, *, tokens_per_mod, tm):
    t = x.shape[0]
    tiles_per_mod = tokens_per_mod // tm
    row = lambda w: pl.BlockSpec((tm, w), lambda i: (i, 0))
    full = lambda a: pl.BlockSpec(a.shape, lambda i: (0,) * len(a.shape))
    krow = pl.BlockSpec((TOP_K, tm), lambda i: (0, i))
    counts = jax.ShapeDtypeStruct((LANES, LANES), F32)
    outs = pl.pallas_call(
        _merge_kernel,
        grid=(t // tm,),
        in_specs=[row(D_MODEL), row(RET_V_W), row(POOL_W), row(D_MODEL), row(D_MODEL),
                  pl.BlockSpec((1, 6, D_MODEL), lambda i: (i // tiles_per_mod, 0, 0)),
                  full(norm2_g), full(w_br_ret), full(w_br_pool), full(w_out), full(router_wt), full(bias_col)],
        out_specs=[row(D_MODEL)] + [row(LANES)] * N_PIECES + [krow, krow, krow, full(counts)],
        out_shape=[jax.ShapeDtypeStruct((t, D_MODEL), F32)] + [jax.ShapeDtypeStruct((t, LANES), I32)] * N_PIECES
        + [jax.ShapeDtypeStruct((TOP_K, t), I32), jax.ShapeDtypeStruct((TOP_K, t), I32),
           jax.ShapeDtypeStruct((TOP_K, t), F32), counts],
        scratch_shapes=[pltpu.VMEM(counts.shape, F32)],
        compiler_params=_cparams("arbitrary"),
        name="merge",
    )(x, z, p, ga, gb, mods, norm2_g, w_br_ret, w_br_pool, w_out, router_wt, bias_col)
    return outs[0], outs[1:1 + N_PIECES], outs[1 + N_PIECES:]


def _route_tile(logits, bias_ref, idx_ref, rank_ref, wk_ref, cnt_ref, carry_ref):
    e = N_EXPERTS
    tm = logits.shape[1]
    neg = -jnp.inf

    @pl.when(pl.program_id(0) == 0)
    def _():
        carry_ref[...] = jnp.zeros(carry_ref.shape, F32)

    scores = jax.nn.sigmoid(logits)
    sel = scores + bias_ref[:e, 0:1]
    e_idx = lax.broadcasted_iota(I32, (e, tm), 0)

    grp = sel.reshape(N_EXPERT_GROUPS, GROUP_SIZE, tm)
    m_idx = lax.broadcasted_iota(I32, grp.shape, 1)
    m1 = jnp.max(grp, axis=1, keepdims=True)
    first = jnp.min(jnp.where(grp == m1, m_idx, GROUP_SIZE), axis=1, keepdims=True)
    m2 = jnp.max(jnp.where(m_idx == first, neg, grp), axis=1, keepdims=True)
    gscore = (m1 + m2).reshape(N_EXPERT_GROUPS, tm)

    g_idx = lax.broadcasted_iota(I32, gscore.shape, 0)
    grank = jnp.zeros(gscore.shape, I32)
    for g in range(N_EXPERT_GROUPS):
        other = gscore[g:g + 1, :]
        beats = jnp.where(other > gscore, 1, jnp.where(other == gscore, (g_idx > g).astype(I32), 0))
        grank = grank + beats
    gkeep = (grank < TOPK_GROUPS).astype(F32)
    ekeep = jnp.broadcast_to(gkeep.reshape(N_EXPERT_GROUPS, 1, tm), grp.shape).reshape(e, tm)
    masked = jnp.where(ekeep > 0, sel, neg)

    chosen = jnp.zeros((e, tm), F32)
    picks, hits = [], []
    for _ in range(TOP_K):
        m = jnp.max(masked, axis=0, keepdims=True)
        pick = jnp.min(jnp.where(masked == m, e_idx, e), axis=0, keepdims=True)
        hit = e_idx == pick
        chosen = jnp.where(hit, 1.0, chosen)
        masked = jnp.where(hit, neg, masked)
        picks.append(pick)
        hits.append(hit)

    w = scores * chosen
    comb = w / jnp.sum(w, axis=0, keepdims=True) * ROUTED_SCALE

    t_row = lax.broadcasted_iota(I32, (tm, tm), 0)
    t_col = lax.broadcasted_iota(I32, (tm, tm), 1)
    before = (t_row < t_col).astype(BF16)
    rankmat = _dot(chosen.astype(BF16), before) + carry_ref[:e, 0:1]
    carry_ref[:e, :] = carry_ref[:e, :] + jnp.sum(chosen, axis=1, keepdims=True)
    cnt_ref[...] = carry_ref[...]

    idx_ref[...] = jnp.concatenate(picks, axis=0)
    rank_ref[...] = jnp.concatenate(
        [jnp.sum(jnp.where(h, rankmat, 0.0), axis=0, keepdims=True) for h in hits], axis=0).astype(I32)
    wk_ref[...] = jnp.concatenate(
        [jnp.sum(jnp.where(h, comb, 0.0), axis=0, keepdims=True) for h in hits], axis=0)


def _plan_kernel(idx_ref, rank_ref, cnt_ref, pos_ref, te_ref, nv_ref, nu_ref, *, group_tile):
    tf = idx_ref.shape[1]
    nt = te_ref.shape[1]
    cnt = cnt_ref[...].astype(I32)
    padded = (((cnt + (group_tile - 1)) // group_tile) * group_tile).astype(F32)
    e_sub = lax.broadcasted_iota(I32, (LANES, LANES), 0)
    e_lane = lax.broadcasted_iota(I32, (LANES, LANES), 1)
    base = jnp.sum(jnp.where(e_lane < e_sub, padded.T, 0.0), axis=1, keepdims=True)
    end = base + padded[:, 0:1]

    idx = idx_ref[...]
    start = jnp.zeros(idx.shape, F32)
    for e in range(N_EXPERTS):
        start = jnp.where(idx == e, base[e:e + 1, 0:1], start)
    pos = start.astype(I32) + rank_ref[...]
    for j in range(tf // SC_CHUNK):
        pos_ref[j] = pos[:, j * SC_CHUNK:(j + 1) * SC_CHUNK]

    tile_start = (lax.broadcasted_iota(I32, (N_EXPERTS, nt), 1) * group_tile).astype(F32)
    done = jnp.sum(jnp.where(end[:N_EXPERTS] <= tile_start, 1.0, 0.0), axis=0, keepdims=True)
    te_ref[...] = jnp.minimum(done, N_EXPERTS - 1.0).astype(I32)
    in_group = (base[:N_EXPERTS] <= tile_start) & (tile_start < end[:N_EXPERTS])
    real = jnp.clip(base[:N_EXPERTS] + cnt[:N_EXPERTS, 0:1].astype(F32) - tile_start, 0.0, float(group_tile))
    nv_ref[...] = jnp.sum(jnp.where(in_group, real, 0.0), axis=0, keepdims=True).astype(I32)
    total = jnp.sum(padded[:, 0:1], axis=0, keepdims=True)
    nu_ref[...] = jnp.broadcast_to(total * (1.0 / group_tile), nu_ref.shape).astype(I32)


def _plan(idx, rank, counts, *, n_tiles, tf, group_tile):
    t = idx.shape[1]
    nt_pad = -(-n_tiles // LANES) * LANES
    krow = pl.BlockSpec((TOP_K, tf), lambda i: (0, i))
    return pl.pallas_call(
        functools.partial(_plan_kernel, group_tile=group_tile),
        grid=(t // tf,),
        in_specs=[krow, krow, pl.BlockSpec((LANES, LANES), lambda i: (0, 0))],
        out_specs=[pl.BlockSpec((tf // SC_CHUNK, TOP_K, SC_CHUNK), lambda i: (i, 0, 0)),
                   pl.BlockSpec((1, nt_pad), lambda i: (0, 0)),
                   pl.BlockSpec((1, nt_pad), lambda i: (0, 0)),
                   pl.BlockSpec((1, LANES), lambda i: (0, 0))],
        out_shape=[jax.ShapeDtypeStruct((t // SC_CHUNK, TOP_K, SC_CHUNK), I32),
                   jax.ShapeDtypeStruct((1, nt_pad), I32), jax.ShapeDtypeStruct((1, nt_pad), I32),
                   jax.ShapeDtypeStruct((1, LANES), I32)],
        compiler_params=_cparams("arbitrary"),
        name="moe_plan",
    )(idx, rank, counts)


def _sc_mesh_info():
    info = plsc.get_sparse_core_info()
    mesh = plsc.VectorSubcoreMesh(core_axis_name="c", subcore_axis_name="s")
    return mesh, info.num_cores, info.num_cores * info.num_subcores


def _sc_dispatch(pieces, pos, *, n_rows):
    t = pieces[0].shape[0]
    mesh, n_cores, n_workers = _sc_mesh_info()
    per_w = t // SC_CHUNK // n_workers

    @functools.partial(
        pl.kernel, mesh=mesh,
        out_type=[jax.ShapeDtypeStruct((n_rows, LANES), I32)] * N_PIECES,
        scratch_types=[pltpu.VMEM((TOP_K, SC_CHUNK), I32),
                       pltpu.VMEM((N_PIECES, SC_CHUNK, LANES), I32),
                       pltpu.SemaphoreType.DMA((N_PIECES,)),
                       pltpu.SemaphoreType.DMA],
        name="sc_dispatch",
    )
    def run(*refs):
        src = refs[:N_PIECES]
        pos_hbm = refs[N_PIECES]
        dst = refs[N_PIECES + 1:2 * N_PIECES + 1]
        idx_v, rows_v, load_sem, put_sem = refs[2 * N_PIECES + 1:]
        wid = lax.axis_index("s") * n_cores + lax.axis_index("c")

        @pl.loop(0, per_w)
        def _(j):
            ch = wid * per_w + j
            t0 = pl.multiple_of(ch * SC_CHUNK, SC_CHUNK)
            loads = [pltpu.make_async_copy(src[c].at[pl.ds(t0, SC_CHUNK)], rows_v.at[c], load_sem.at[c])
                     for c in range(N_PIECES)]
            for ld in loads:
                ld.start()
            pltpu.sync_copy(pos_hbm.at[ch], idx_v)
            puts = []
            for c in range(N_PIECES):
                loads[c].wait()
                for k in range(TOP_K):
                    puts.append(pltpu.make_async_copy(rows_v.at[c], dst[c].at[idx_v.at[k]], put_sem))
                    puts[-1].start()
            for cp in puts:
                cp.wait()

    return run(*pieces, pos)


SC_GROUP = 32


def _sc_combine(pieces, pos, wts, *, n_tokens):
    mesh, n_cores, n_workers = _sc_mesh_info()
    lanes = plsc.get_sparse_core_info().num_lanes
    per_w = n_tokens // SC_GROUP // n_workers

    @functools.partial(
        pl.kernel, mesh=mesh,
        out_type=[jax.ShapeDtypeStruct((n_tokens, LANES), I32)] * N_PIECES,
        scratch_types=[pltpu.VMEM((TOP_K, SC_GROUP), I32),
                       pltpu.VMEM((TOP_K, SC_GROUP), F32),
                       pltpu.VMEM((2, TOP_K, SC_GROUP, LANES), I32),
                       pltpu.VMEM((2, SC_GROUP, LANES), I32),
                       pltpu.SemaphoreType.DMA((2,)),
                       pltpu.SemaphoreType.DMA((2,))],
        compiler_params=pltpu.CompilerParams(needs_layout_passes=False),
        name="sc_combine",
    )
    def run(*refs):
        src = refs[:N_PIECES]
        pos_hbm, wts_hbm = refs[N_PIECES:N_PIECES + 2]
        dst = refs[N_PIECES + 2:2 * N_PIECES + 2]
        idx_v, w_v, buf, acc, get_sem, put_sem = refs[2 * N_PIECES + 2:]
        wid = lax.axis_index("s") * n_cores + lax.axis_index("c")

        @pl.loop(0, per_w)
        def _(j):
            grp = wid * per_w + j
            t0 = pl.multiple_of(grp * SC_GROUP, SC_GROUP)
            pltpu.sync_copy(pos_hbm.at[grp], idx_v)
            pltpu.sync_copy(wts_hbm.at[grp], w_v)

            def gets(c, slot):
                return [pltpu.make_async_copy(src[c].at[idx_v.at[k]], buf.at[slot, k], get_sem.at[slot])
                        for k in range(TOP_K)]

            def puts(c, slot):
                return [pltpu.make_async_copy(acc.at[slot], dst[c].at[pl.ds(t0, SC_GROUP)], put_sem.at[slot])]

            for cp in gets(0, 0):
                cp.start()
            for c in range(N_PIECES):
                slot = c % 2
                if c + 1 < N_PIECES:
                    for cp in gets(c + 1, 1 - slot):
                        cp.start()
                for cp in gets(c, slot):
                    cp.wait()
                if c >= 2:
                    for cp in puts(c - 2, slot):
                        cp.wait()

                @pl.loop(0, SC_GROUP)
                def _(r):
                    row = jnp.full((lanes,), r, I32)
                    w = []
                    for k in range(TOP_K):
                        w_k = plsc.load_gather(w_v, [jnp.full((lanes,), k, I32), row])
                        w.append(plsc.pack(w_k, w_k, format=plsc.PackFormat.INTERLEAVED))
                    for q in range(LANES // lanes):
                        cols = pl.ds(q * lanes, lanes)
                        total = None
                        for k in range(TOP_K):
                            term = plsc.bitcast(buf[slot, k, r, cols], BF16) * w[k]
                            total = term if total is None else total + term
                        acc[slot, r, cols] = plsc.bitcast(total, I32)

                for cp in puts(c, slot):
                    cp.start()
            for c in range(N_PIECES - 2, N_PIECES):
                for cp in puts(c, c % 2):
                    cp.wait()

    return run(*pieces, pos, wts)


def _group_tile(n_tokens):
    per_expert = n_tokens * TOP_K // N_EXPERTS
    return max(MXU_DIM, min(4 * MXU_DIM, per_expert // MXU_DIM * MXU_DIM))


SECOND_DMA_QUEUE = 1


def _experts_kernel(te_ref, nv_ref, nu_ref, *refs, tile):
    x_hbm = refs[:N_PIECES]
    wg_ref, wu_ref, wd_ref = refs[N_PIECES:N_PIECES + 3]
    y_hbm = refs[N_PIECES + 3:2 * N_PIECES + 3]
    xbuf, xsem, ybuf, ysem = refs[2 * N_PIECES + 3:]
    i = pl.program_id(0)
    n_used = nu_ref[0]
    slot = lax.rem(i, 2)

    half = tile // 2

    def x_copy(step, into, h, c):
        r = pl.ds(pl.multiple_of(step * tile + h * half, half), half)
        return pltpu.make_async_copy(x_hbm[c].at[r], xbuf.at[into, c, pl.ds(h * half, half)], xsem.at[into, c])

    def y_copy(step, out_of, h, c):
        r = pl.ds(pl.multiple_of(step * tile + h * half, half), half)
        return pltpu.make_async_copy(ybuf.at[out_of, c, pl.ds(h * half, half)], y_hbm[c].at[r], ysem.at[out_of, c])

    def real_halves(copy, step, buf, act):
        for c in range(N_PIECES):
            act(copy(step, buf, 0, c))

        @pl.when(nv_ref[step] > half)
        def _():
            for c in range(N_PIECES):
                act(copy(step, buf, 1, c))

    start = lambda cp: cp.start(priority=SECOND_DMA_QUEUE)
    wait = lambda cp: cp.wait()

    @pl.when(i == 0)
    def _():
        real_halves(x_copy, 0, 0, start)

    @pl.when(i + 1 < n_used)
    def _():
        real_halves(x_copy, i + 1, 1 - slot, start)

    @pl.when(i < n_used)
    def _():
        real_halves(x_copy, i, slot, wait)
        subs = [slice(s * MXU_DIM, (s + 1) * MXU_DIM) for s in range(tile // MXU_DIM)]
        xs = []
        for rows in subs:
            lo, hi = _unpack_rows([xbuf[slot, c, rows, :] for c in range(N_PIECES)])
            xs.append(jnp.concatenate(lo + hi, axis=1).astype(BF16))
        gates = [(_dot(x, wg_ref[...]), _dot(x, wu_ref[...])) for x in xs]
        ys = [_dot((_silu(g) * u).astype(BF16), wd_ref[...]) for g, u in gates]
        for rows, y in zip(subs, ys):
            for c, piece in enumerate(_pack_rows(y)):
                ybuf[slot, c, rows, :] = piece

        @pl.when(i >= 1)
        def _():
            real_halves(y_copy, i - 1, 1 - slot, wait)

        real_halves(y_copy, i, slot, start)

        @pl.when(i == n_used - 1)
        def _():
            real_halves(y_copy, i, slot, wait)


def _experts(x_pieces, tile_expert, tile_rows, n_used, wg, wu, wd, *, group_tile):
    n_rows = x_pieces[0].shape[0]
    n_tiles = n_rows // group_tile

    wspec = lambda a: pl.BlockSpec((None,) + a.shape[1:],
                                   lambda i, te, nv, nu: (te[jnp.minimum(i, nu[0] - 1)], 0, 0))
    return pl.pallas_call(
        functools.partial(_experts_kernel, tile=group_tile),
        grid_spec=pltpu.PrefetchScalarGridSpec(
            num_scalar_prefetch=3,
            grid=(n_tiles,),
            in_specs=[pl.BlockSpec(memory_space=pl.ANY)] * N_PIECES + [wspec(wg), wspec(wu), wspec(wd)],
            out_specs=[pl.BlockSpec(memory_space=pl.ANY)] * N_PIECES,
            scratch_shapes=[pltpu.VMEM((2, N_PIECES, group_tile, LANES), I32),
                            pltpu.SemaphoreType.DMA((2, N_PIECES))] * 2),
        out_shape=[jax.ShapeDtypeStruct((n_rows, LANES), I32)] * N_PIECES,
        compiler_params=_cparams("arbitrary"),
        name="experts",
    )(tile_expert, tile_rows, n_used, *x_pieces, wg, wu, wd)


def _moe_out_kernel(x1_ref, mod_ref, fg_ref, sg_ref, su_ref, sd_ref, *refs):
    h_refs = refs[:N_PIECES]
    routed_refs = refs[N_PIECES:2 * N_PIECES]
    out_ref = refs[2 * N_PIECES]
    h_lo, h_hi = _unpack_rows([r[...] for r in h_refs])
    h = jnp.concatenate(h_lo + h_hi, axis=1).astype(BF16)
    hid = _silu(_dot(h, sg_ref[...])) * _dot(h, su_ref[...])
    shared = _dot(hid.astype(BF16), sd_ref[...])
    r_lo, r_hi = _unpack_rows([r[...] for r in routed_refs])
    routed = jnp.concatenate(r_lo + r_hi, axis=1)
    x2 = x1_ref[...] + mod_ref[0, 5:6, :] * (shared + routed)
    out_ref[...] = x2 * lax.rsqrt(jnp.mean(x2 * x2, axis=-1, keepdims=True) + EPS) * fg_ref[...]


def _moe_out(h2_pieces, x1, mods, final_g, sg, su, sd, routed_pieces, *, tokens_per_mod, tm):
    t = x1.shape[0]
    tiles_per_mod = tokens_per_mod // tm
    row = lambda w: pl.BlockSpec((tm, w), lambda i: (i, 0))
    full = lambda a: pl.BlockSpec(a.shape, lambda i: (0,) * a.ndim)
    return pl.pallas_call(
        _moe_out_kernel,
        grid=(t // tm,),
        in_specs=[row(D_MODEL),
                  pl.BlockSpec((1, 6, D_MODEL), lambda i: (i // tiles_per_mod, 0, 0)),
                  full(final_g), full(sg), full(su), full(sd)]
        + [row(LANES)] * (2 * N_PIECES),
        out_specs=row(D_MODEL),
        out_shape=jax.ShapeDtypeStruct((t, D_MODEL), F32),
        compiler_params=_cparams("parallel"),
        name="moe_out",
    )(x1, mods, final_g, sg, su, sd, *h2_pieces, *routed_pieces)


def _trunk(x, mods, s0, w, expert_w, *, batch, seq_len, on_grid):
    t = batch * seq_len
    tokens_per_mod = t // mods.shape[0]
    cos_t, sin_t = _rope_tables(max(seq_len, PROJ_TILE))
    to_cast = expert_w if expert_w[0].dtype != BF16 else ()
    (q, k, v, gsw, up, ga, gb), casted = _inproj(x, mods, w["norm1_g"], w["w_in"], cos_t, sin_t, to_cast,
                                                 tokens_per_mod=tokens_per_mod, seq_len=seq_len,
                                                 on_grid=on_grid, tm=PROJ_TILE)
    expert_w = casted or expert_w
    z, s_f, s_b = _retention(q, k, v, gsw, w["dec"], s0, batch=batch, seq_len=seq_len)
    p = _pool(up, w["pool_w"], w["pool_scale"], batch=batch, seq_len=seq_len, on_grid=on_grid)
    x1, h2_pieces, (idx, rank, wts, counts) = _merge(
        x, z, p, ga, gb, mods, w["norm2_g"], w["w_br_ret"], w["w_br_pool"], w["w_out"],
        w["router_wt"], w["router_bias"], tokens_per_mod=tokens_per_mod, tm=PROJ_TILE)

    group_tile = _group_tile(t)
    n_rows = t * TOP_K + N_EXPERTS * group_tile
    pos, tile_expert, tile_rows, n_used = _plan(idx, rank, counts, n_tiles=n_rows // group_tile, tf=PLAN_TILE,
                                                group_tile=group_tile)
    x_sorted = _sc_dispatch(h2_pieces, pos, n_rows=n_rows)
    y_sorted = _experts(x_sorted, tile_expert.reshape(-1), tile_rows.reshape(-1), n_used.reshape(-1),
                        *expert_w, group_tile=group_tile)
    regroup = lambda a: a.reshape(TOP_K, t // SC_GROUP, SC_GROUP).transpose(1, 0, 2)
    pos_rows = pos.transpose(1, 0, 2).reshape(TOP_K, t)
    routed = _sc_combine(y_sorted, regroup(pos_rows), regroup(wts), n_tokens=t)
    y = _moe_out(h2_pieces, x1, mods, w["final_g"], w["sh_w_gate"], w["sh_w_up"], w["sh_w_down"], routed,
                 tokens_per_mod=tokens_per_mod, tm=OUT_TILE)
    return y, s_f, s_b, expert_w


def kernel(x_prompt, x_sample, state_ret_fwd, state_ret_bwd, c, c_ctx, ada_w, ada_b, norm1_g, norm2_g, w_in,
           ret_decay_fwd, ret_decay_bwd, w_br_ret, pool_w, pool_scale, w_br_pool, w_out, router_w, router_bias,
           exp_w_gate, exp_w_up, exp_w_down, sh_w_gate, sh_w_up, sh_w_down, final_norm_g):
    n_req, seq, d = x_prompt.shape
    n_dec, dec_seq, _ = x_sample.shape
    depth = ada_w.shape[0]
    assert depth == 1 and d == D_MODEL

    xc = x_prompt.reshape(n_req * seq, d)
    xs = x_sample.reshape(n_dec * dec_seq, d)
    new_f, new_b = [], []
    for l in range(depth):
        c_rows = jnp.concatenate([c_ctx[None, :], c, jnp.zeros((8 - 1 - n_dec, d), F32)], axis=0)
        mods = _ada(c_rows, ada_w[l], ada_b[l]).reshape(8, 6, d)
        pad_rows = LANES - N_EXPERTS
        w = dict(
            norm1_g=norm1_g[l].reshape(1, d), norm2_g=norm2_g[l].reshape(1, d),
            final_g=final_norm_g.reshape(1, d),
            w_in=w_in[l].astype(BF16),
            dec=jnp.stack([ret_decay_fwd[l], ret_decay_bwd[l]]).astype(F32),
            w_br_ret=w_br_ret[l].astype(BF16), pool_w=pool_w[l].astype(BF16),
            pool_scale=pool_scale[l].reshape(1, POOL_W), w_br_pool=w_br_pool[l].astype(BF16),
            w_out=w_out[l].astype(BF16),
            router_wt=jnp.pad(router_w[l].T, ((0, pad_rows), (0, 0))).astype(BF16),
            router_bias=jnp.pad(router_bias[l].astype(F32).reshape(N_EXPERTS, 1), ((0, pad_rows), (0, 0))),
            sh_w_gate=sh_w_gate[l].astype(BF16),
            sh_w_up=sh_w_up[l].astype(BF16), sh_w_down=sh_w_down[l].astype(BF16),
        )
        cached = (state_ret_fwd[:, l].astype(F32), state_ret_bwd[:, l].astype(F32))
        expert_w = (exp_w_gate[l], exp_w_up[l], exp_w_down[l])
        xs, _, _, expert_w = _trunk(xs, mods[1:1 + n_dec], cached, w, expert_w,
                                    batch=n_dec, seq_len=dec_seq, on_grid=True)
        xc, s_f, s_b, _ = _trunk(xc, mods[0:1], None, w, expert_w, batch=n_req, seq_len=seq, on_grid=False)
        new_f.append(s_f)
        new_b.append(s_b)
    y_prompt = xc.reshape(n_req, seq, d)
    y_sample = xs.reshape(n_dec, dec_seq, d)
    return (y_prompt, y_sample, jnp.stack(new_f, axis=1).astype(x_prompt.dtype),
            jnp.stack(new_b, axis=1).astype(x_prompt.dtype))
```

```python
import functools
import math

import numpy as np
import jax
import jax.numpy as jnp
from jax import lax
from jax.experimental import pallas as pl
from jax.experimental.pallas import tpu as pltpu
from jax.experimental.pallas import tpu_sc as plsc

D_MODEL = 1024
GRID_W = 64
RET_HEADS = 4
RET_DK = 128
RET_DV = 256
RET_QK_W = RET_HEADS * RET_DK
RET_V_W = RET_HEADS * RET_DV
RET_CHUNK = 128
ROPE_BASE = 10000.0
POOL_GROUPS = 4
POOL_CH = 128
POOL_W = POOL_GROUPS * POOL_CH
POOL_WINDOWS = (2, 4, 8, 16)
N_EXPERTS = 64
TOP_K = 8
N_EXPERT_GROUPS = 8
GROUP_SIZE = N_EXPERTS // N_EXPERT_GROUPS
TOPK_GROUPS = 4
D_EXPERT = 256
ROUTED_SCALE = 2.5
EPS = 1e-6
IN_SIZES = (RET_QK_W, RET_QK_W, RET_V_W, RET_V_W, POOL_W, D_MODEL, D_MODEL)
IN_OFFS = tuple(sum(IN_SIZES[:i]) for i in range(len(IN_SIZES) + 1))
IN_W = IN_OFFS[-1]

LANES = 128
VMEM_LIMIT = 56 << 20
N_PIECES = D_MODEL // 2 // LANES
MXU_DIM = 256
SC_CHUNK = 128
PROJ_TILE = 512
MERGE_TILE = 1024
PLAN_TILE = 2048
OUT_TILE = 1024
POOL_SEQS_PER_STEP = 4
HIGH_HALF = 0xFFFF0000

F32 = jnp.float32
BF16 = jnp.bfloat16
I32 = jnp.int32
U32 = jnp.uint32


def _cparams(*sem):
    return pltpu.CompilerParams(dimension_semantics=sem, vmem_limit_bytes=VMEM_LIMIT)


def _dot(a, b):
    return jnp.dot(a, b, preferred_element_type=F32)


def _silu(x):
    return x * jax.nn.sigmoid(x)


def _rms_mod(x, g, scale, shift):
    y = x * lax.rsqrt(jnp.mean(x * x, axis=-1, keepdims=True) + EPS)
    return (y * g) * (1.0 + scale) + shift


def _ada_kernel(c_ref, w_ref, b_ref, o_ref):
    c = c_ref[...]
    o_ref[...] = jnp.dot(_silu(c), w_ref[...], preferred_element_type=F32,
                         precision=lax.Precision.HIGHEST) + b_ref[...]


def _ada(c_rows, ada_w, ada_b):
    r = c_rows.shape[0]
    n = ada_w.shape[1]
    tn = 2 * D_MODEL
    return pl.pallas_call(
        _ada_kernel,
        grid=(n // tn,),
        in_specs=[pl.BlockSpec((r, D_MODEL), lambda j: (0, 0)),
                  pl.BlockSpec((D_MODEL, tn), lambda j: (0, j)),
                  pl.BlockSpec((1, tn), lambda j: (0, j))],
        out_specs=pl.BlockSpec((r, tn), lambda j: (0, j)),
        out_shape=jax.ShapeDtypeStruct((r, n), F32),
        compiler_params=_cparams("parallel"),
        name="ada_mod",
    )(c_rows, ada_w, ada_b.reshape(1, n))


def _inproj_kernel(x_ref, mod_ref, g_ref, w_ref, cos_ref, sin_ref, *refs, on_grid):
    n_side = (len(refs) - len(IN_SIZES)) // 2
    side_in = refs[:n_side]
    q_ref, k_ref, v_ref, gsw_ref, up_ref, ga_ref, gb_ref = refs[n_side:n_side + len(IN_SIZES)]
    side_out = refs[n_side + len(IN_SIZES):]
    for src, dst in zip(side_in, side_out):
        dst[...] = src[...].astype(BF16)
    for s in range(x_ref.shape[0] // MXU_DIM):
        rows = slice(s * MXU_DIM, (s + 1) * MXU_DIM)
        h = _rms_mod(x_ref[rows, :], g_ref[...], mod_ref[0, 1:2, :], mod_ref[0, 0:1, :]).astype(BF16)

        def seg(i):
            return _dot(h, w_ref[:, IN_OFFS[i]:IN_OFFS[i + 1]])

        q = seg(0)
        k = seg(1)
        if on_grid:
            cos = jnp.concatenate([cos_ref[rows, :]] * RET_HEADS, axis=1)
            sin = jnp.concatenate([sin_ref[rows, :]] * RET_HEADS, axis=1)
            quarter = RET_DK // 4
            lane = lax.broadcasted_iota(jnp.int32, q.shape, 1)
            first = (lane & (2 * quarter - 1)) < quarter

            def rope(a):
                up = pltpu.roll(a, RET_QK_W - quarter, axis=1)
                dn = pltpu.roll(a, quarter, axis=1)
                return a * cos + jnp.where(first, up, dn) * sin

            q = rope(q)
            k = rope(k)
        q_ref[rows, :] = q.astype(BF16)
        k_ref[rows, :] = (k * (RET_DK ** -0.5)).astype(BF16)
        v_ref[rows, :] = seg(2).astype(BF16)
        gsw_ref[rows, :] = seg(3).astype(BF16)
        up_ref[rows, :] = seg(4).astype(BF16)
        ga_ref[rows, :] = seg(5).astype(BF16)
        gb_ref[rows, :] = seg(6).astype(BF16)


def _inproj(x, mods, norm_g, w_in, cos_t, sin_t, side_cast=(), *, tokens_per_mod, seq_len, on_grid, tm):
    t = x.shape[0]
    steps = t // tm
    tiles_per_mod = tokens_per_mod // tm
    tiles_per_seq = max(seq_len // tm, 1)
    widths = IN_SIZES
    out_shape = [jax.ShapeDtypeStruct((t, w), BF16) for w in widths]
    out_specs = [pl.BlockSpec((tm, w), lambda i: (i, 0)) for w in widths]
    side_specs = [pl.BlockSpec((a.shape[0] // steps,) + a.shape[1:], lambda i: (i, 0, 0)) for a in side_cast]
    outs = pl.pallas_call(
        functools.partial(_inproj_kernel, on_grid=on_grid),
        grid=(steps,),
        in_specs=[pl.BlockSpec((tm, D_MODEL), lambda i: (i, 0)),
                  pl.BlockSpec((1, 6, D_MODEL), lambda i: (i // tiles_per_mod, 0, 0)),
                  pl.BlockSpec((1, D_MODEL), lambda i: (0, 0)),
                  pl.BlockSpec((D_MODEL, IN_W), lambda i: (0, 0), pipeline_mode=pl.Buffered(1)),
                  pl.BlockSpec((tm, RET_DK), lambda i: (i % tiles_per_seq, 0)),
                  pl.BlockSpec((tm, RET_DK), lambda i: (i % tiles_per_seq, 0))] + side_specs,
        out_specs=out_specs + side_specs,
        out_shape=out_shape + [jax.ShapeDtypeStruct(a.shape, BF16) for a in side_cast],
        compiler_params=_cparams("parallel"),
        name="inproj_grid" if on_grid else "inproj_seq",
    )(x, mods, norm_g, w_in, cos_t, sin_t, *side_cast)
    return outs[:len(widths)], tuple(outs[len(widths):])


def _rope_tables(seq_len):
    t = np.arange(seq_len)
    row = (t // GRID_W).astype(np.float32)
    col = (t % GRID_W).astype(np.float32)
    m = RET_DK // 4
    inv = (np.float32(ROPE_BASE) ** (-np.arange(m, dtype=np.float32) / np.float32(m))).astype(np.float32)
    ar = row[:, None] * inv
    ac = col[:, None] * inv
    cos = np.concatenate([np.cos(ar), np.cos(ar), np.cos(ac), np.cos(ac)], axis=1)
    sin = np.concatenate([-np.sin(ar), np.sin(ar), -np.sin(ac), np.sin(ac)], axis=1)
    return jnp.asarray(cos, F32), jnp.asarray(sin, F32)


def _ret_heads_per_step(seq_len):
    per_head = seq_len * (2 * 2 * (2 * RET_DK + 3 * RET_DV) + 4 * RET_DV + 2 * RET_DK)
    heads = RET_HEADS
    while heads > 1 and heads * per_head > VMEM_LIMIT * 3 // 4:
        heads //= 2
    return heads


def _ret_kernel(dec_ref, q_ref, k_ref, v_ref, g_ref, *refs, n_chunks, heads, zero_init):
    s0_refs = () if zero_init else refs[:2]
    z_ref, sf_ref, sb_ref, oacc_ref, kt_ref = refs[len(s0_refs):]
    c = RET_CHUNK
    half = n_chunks // 2
    ii = lax.broadcasted_iota(I32, (c, c), 0)
    jj = lax.broadcasted_iota(I32, (c, c), 1)
    ik = lax.broadcasted_iota(I32, (c, RET_DK), 0).astype(F32)
    jk = lax.broadcasted_iota(I32, (RET_DK, c), 1).astype(F32)

    def log_gamma(d, shape):
        return jnp.log1p(-jnp.exp2(-jnp.full(shape, d, F32)))

    consts = {}
    for hh in range(heads):
        h = pl.program_id(1) * heads + hh
        dec_f = dec_ref[0, h]
        dec_b = dec_ref[1, h]
        rel = (ii - jj).astype(F32)
        consts[hh, "f"] = (
            jnp.where(rel >= 0, jnp.exp(log_gamma(dec_f, (c, c)) * jnp.maximum(rel, 0.0)), 0.0),
            jnp.exp(log_gamma(dec_f, (c, RET_DK)) * (ik + 1.0)),
            jnp.exp(log_gamma(dec_f, (RET_DK, c)) * (c - 1.0 - jk)),
            jnp.exp(log_gamma(dec_f, (RET_DK, RET_DV)) * c))
        consts[hh, "b"] = (
            jnp.where(rel <= 0, jnp.exp(log_gamma(dec_b, (c, c)) * jnp.maximum(-rel, 0.0)), 0.0),
            jnp.exp(log_gamma(dec_b, (c, RET_DK)) * (c - ik)),
            jnp.exp(log_gamma(dec_b, (RET_DK, c)) * jk),
            jnp.exp(log_gamma(dec_b, (RET_DK, RET_DV)) * c))

    for s_ref, s0_ref in zip((sf_ref, sb_ref), s0_refs or (None, None)):
        s_ref[...] = jnp.zeros(s_ref.shape, F32) if zero_init else s0_ref[...]

    def scores(ci, hh, direction, second):
        r = pl.ds(pl.multiple_of(ci * c, c), c)
        kcols = slice(hh * RET_DK, (hh + 1) * RET_DK)
        qc = q_ref[r, kcols]
        kc = k_ref[r, kcols]
        if not second:
            kt_ref[hh, ci] = kc.T
        sc = lax.dot_general(qc, kc, (((1,), (1,)), ((), ())), preferred_element_type=F32)
        return ci, hh, direction, r, qc, sc

    def advance(job):
        ci, hh, direction, r, qc, sc = job
        dmask, qdec, kdec, cdec = consts[hh, direction]
        s_ref = sf_ref if direction == "f" else sb_ref
        vc = v_ref[r, hh * RET_DV:(hh + 1) * RET_DV]
        s = s_ref[hh]
        lhs = jnp.concatenate([(sc * dmask).astype(BF16), (qc.astype(F32) * qdec).astype(BF16)], axis=1)
        o = _dot(lhs, jnp.concatenate([vc, s.astype(BF16)], axis=0))
        kd_t = (kt_ref[hh, ci].astype(F32) * kdec).astype(BF16)
        s_ref[hh] = s * cdec + _dot(kd_t, vc)
        return o

    def emit(job, o, second):
        _, hh, _, r, _, _ = job
        vcols = slice(hh * RET_DV, (hh + 1) * RET_DV)
        if not second:
            oacc_ref[hh, r, :] = o
        else:
            o = o + oacc_ref[hh, r, :]
            o = o * lax.rsqrt(jnp.mean(o * o, axis=-1, keepdims=True) + EPS)
            g = g_ref[r, vcols].astype(F32)
            z_ref[r, vcols] = (_silu(g) * o).astype(BF16)

    def body(second):
        def run(t, carry):
            jobs = [scores(ci, hh, d, second) for hh in range(heads)
                    for ci, d in ((t, "f"), (n_chunks - 1 - t, "b"))]
            outs = [advance(job) for job in jobs]
            for job, o in zip(jobs, outs):
                emit(job, o, second)
            return carry
        return run

    lax.fori_loop(0, half, body(False), 0, unroll=8 if half % 8 == 0 else 1)
    lax.fori_loop(half, n_chunks, body(True), 0, unroll=4 if half % 4 == 0 else 1)


def _retention(q, k, v, gsw, dec, s0, *, batch, seq_len):
    n_chunks = seq_len // RET_CHUNK
    assert n_chunks % 2 == 0
    heads = _ret_heads_per_step(seq_len)
    t = batch * seq_len
    st_spec = pl.BlockSpec((None, heads, RET_DK, RET_DV), lambda b, h: (b, h, 0, 0))
    st_shape = jax.ShapeDtypeStruct((batch, RET_HEADS, RET_DK, RET_DV), F32)
    kspec = pl.BlockSpec((seq_len, heads * RET_DK), lambda b, h: (b, h))
    vspec = pl.BlockSpec((seq_len, heads * RET_DV), lambda b, h: (b, h))
    return pl.pallas_call(
        functools.partial(_ret_kernel, n_chunks=n_chunks, heads=heads, zero_init=s0 is None),
        grid=(batch, RET_HEADS // heads),
        in_specs=[pl.BlockSpec(memory_space=pltpu.SMEM), kspec, kspec, vspec, vspec]
        + ([] if s0 is None else [st_spec, st_spec]),
        out_specs=[vspec, st_spec, st_spec],
        out_shape=[jax.ShapeDtypeStruct((t, RET_V_W), BF16), st_shape, st_shape],
        scratch_shapes=[pltpu.VMEM((heads, seq_len, RET_DV), F32),
                        pltpu.VMEM((heads, n_chunks, RET_DK, RET_CHUNK), BF16)],
        compiler_params=_cparams("parallel", "parallel"),
        name=f"retention_l{seq_len}",
    )(dec, q, k, v, gsw, *(s0 or ()))


def _pool_kernel(u_ref, w_ref, sc_ref, o_ref, *, n_tok, width, two_d):
    n_rows = n_tok // width
    pos = lax.broadcasted_iota(I32, (width, POOL_CH), 0)

    def every_row(a):
        return jnp.concatenate([a] * n_rows, axis=0) if n_rows > 1 else a

    def shift_in_row(a, s):
        ok = (pos < width - s) if s > 0 else (pos >= -s)
        return pltpu.roll(a, (-s) % n_tok, axis=0) * every_row(jnp.where(ok, 1.0, 0.0))

    def shift_rows(a, m):
        k = abs(m) * width
        zeros = jnp.zeros((k, POOL_CH), F32)
        return (jnp.concatenate([a[k:], zeros], axis=0) if m > 0
                else jnp.concatenate([zeros, a[:n_tok - k]], axis=0))

    def box_sum(a, half, shift):
        fw = a
        bw = shift(a, -1)
        m = 1
        while m < half:
            fw = fw + shift(fw, m)
            bw = bw + shift(bw, -m)
            m *= 2
        return fw + bw

    def inv_count(p, half, extent):
        return 1.0 / (jnp.minimum(p + half, extent) - jnp.maximum(p - half, 0)).astype(F32)

    for g, window in enumerate(POOL_WINDOWS):
        half = window // 2
        cols = slice(g * POOL_CH, (g + 1) * POOL_CH)
        ug = u_ref[:, cols].astype(F32)
        total = box_sum(ug, half, shift_in_row)
        inv = every_row(inv_count(pos, half, width))
        if two_d:
            total = box_sum(total, half, shift_rows)
            row = lax.broadcasted_iota(I32, (n_rows, 1, POOL_CH), 0)
            inv_r = jnp.broadcast_to(inv_count(row, half, n_rows), (n_rows, width, POOL_CH))
            inv = inv * inv_r.reshape(n_tok, POOL_CH)
        d = (total * inv - ug).astype(BF16)
        o_ref[:, cols] = (_dot(d, w_ref[g]) * sc_ref[:, cols]).astype(BF16)


def _pool(u, pool_w, pool_scale, *, batch, seq_len, on_grid):
    t = batch * seq_len
    width = GRID_W if on_grid else seq_len
    n_tok = seq_len if on_grid else seq_len * math.gcd(batch, POOL_SEQS_PER_STEP)
    return pl.pallas_call(
        functools.partial(_pool_kernel, n_tok=n_tok, width=width, two_d=on_grid),
        grid=(t // n_tok,),
        in_specs=[pl.BlockSpec((n_tok, POOL_W), lambda b: (b, 0)),
                  pl.BlockSpec((POOL_GROUPS, POOL_CH, POOL_CH), lambda b: (0, 0, 0)),
                  pl.BlockSpec((1, POOL_W), lambda b: (0, 0))],
        out_specs=pl.BlockSpec((n_tok, POOL_W), lambda b: (b, 0)),
        out_shape=jax.ShapeDtypeStruct((t, POOL_W), BF16),
        compiler_params=_cparams("parallel"),
        name=f"pool_l{seq_len}",
    )(u, pool_w, pool_scale)


def _pack_rows(x):
    half = D_MODEL // 2
    lo = lax.bitcast_convert_type(x[:, :half].astype(BF16).astype(F32), U32) >> 16
    hi = lax.bitcast_convert_type(x[:, half:].astype(BF16).astype(F32), U32) & jnp.uint32(HIGH_HALF)
    word = lax.bitcast_convert_type(hi | lo, I32)
    return [word[:, c * LANES:(c + 1) * LANES] for c in range(N_PIECES)]


def _unpack_rows(pieces):
    words = [lax.bitcast_convert_type(p, U32) for p in pieces]
    lo = [lax.bitcast_convert_type(w << 16, F32) for w in words]
    hi = [lax.bitcast_convert_type(w & jnp.uint32(HIGH_HALF), F32) for w in words]
    return lo, hi


def _merge_kernel(x_ref, z_ref, p_ref, ga_ref, gb_ref, mod_ref, g2_ref, wr_ref, wp_ref, wo_ref, rw_ref, bias_ref,
                  x1_ref, *refs):
    piece_refs, route_refs = refs[:N_PIECES], refs[N_PIECES:]

    @pl.when(pl.program_id(0) == 0)
    def _():
        route_refs[-1][...] = jnp.zeros(route_refs[-1].shape, F32)

    subs = [slice(s * MXU_DIM, (s + 1) * MXU_DIM) for s in range(x_ref.shape[0] // MXU_DIM)]
    branches = [(_dot(z_ref[r, :], wr_ref[...]), _dot(p_ref[r, :], wp_ref[...])) for r in subs]
    merged = [(jax.nn.sigmoid(ga_ref[r, :].astype(F32)) * y_ret
               + jax.nn.sigmoid(gb_ref[r, :].astype(F32)) * y_pool).astype(BF16)
              for r, (y_ret, y_pool) in zip(subs, branches)]
    outs = [_dot(m, wo_ref[...]) for m in merged]
    logits = []
    for r, o in zip(subs, outs):
        x1 = x_ref[r, :] + mod_ref[0, 2:3, :] * o
        x1_ref[r, :] = x1
        h2 = _rms_mod(x1, g2_ref[...], mod_ref[0, 4:5, :], mod_ref[0, 3:4, :])
        logits.append(lax.dot_general(rw_ref[...], h2.astype(BF16), (((1,), (1,)), ((), ())),
                                      preferred_element_type=F32)[:N_EXPERTS])
        for ref, piece in zip(piece_refs, _pack_rows(h2)):
            ref[r, :] = piece
    _route_tile(jnp.concatenate(logits, axis=1), bias_ref, *route_refs)


def _merge(x, z, p, ga, gb, mods, norm2_g, w_br_ret, w_br_pool, w_out, router_wt, bias_col, *, tokens_per_mod, tm):
    t = x.shape[0]
    tiles_per_mod = tokens_per_mod // tm
    row = lambda w: pl.BlockSpec((tm, w), lambda i: (i, 0))
    full = lambda a: pl.BlockSpec(a.shape, lambda i: (0,) * len(a.shape))
    krow = pl.BlockSpec((TOP_K, tm), lambda i: (0, i))
    counts = jax.ShapeDtypeStruct((LANES, LANES), F32)
    outs = pl.pallas_call(
        _merge_kernel,
        grid=(t // tm,),
        in_specs=[row(D_MODEL), row(RET_V_W), row(POOL_W), row(D_MODEL), row(D_MODEL),
                  pl.BlockSpec((1, 6, D_MODEL), lambda i: (i // tiles_per_mod, 0, 0)),
                  full(norm2_g), full(w_br_ret), full(w_br_pool), full(w_out), full(router_wt), full(bias_col)],
        out_specs=[row(D_MODEL)] + [row(LANES)] * N_PIECES + [krow, krow, krow, full(counts)],
        out_shape=[jax.ShapeDtypeStruct((t, D_MODEL), F32)] + [jax.ShapeDtypeStruct((t, LANES), I32)] * N_PIECES
        + [jax.ShapeDtypeStruct((TOP_K, t), I32), jax.ShapeDtypeStruct((TOP_K, t), I32),
           jax.ShapeDtypeStruct((TOP_K, t), F32), counts],
        scratch_shapes=[pltpu.VMEM(counts.shape, F32)],
        compiler_params=_cparams("arbitrary"),
        name="merge",
    )(x, z, p, ga, gb, mods, norm2_g, w_br_ret, w_br_pool, w_out, router_wt, bias_col)
    return outs[0], outs[1:1 + N_PIECES], outs[1 + N_PIECES:]


def _route_tile(logits, bias_ref, idx_ref, rank_ref, wk_ref, cnt_ref, carry_ref):
    e = N_EXPERTS
    tm = logits.shape[1]
    neg = -jnp.inf

    scores = jax.nn.sigmoid(logits)
    sel = scores + bias_ref[:e, 0:1]
    e_idx = lax.broadcasted_iota(I32, (e, tm), 0)

    grp = sel.reshape(N_EXPERT_GROUPS, GROUP_SIZE, tm)
    m_idx = lax.broadcasted_iota(I32, grp.shape, 1)
    m1 = jnp.max(grp, axis=1, keepdims=True)
    first = jnp.min(jnp.where(grp == m1, m_idx, GROUP_SIZE), axis=1, keepdims=True)
    m2 = jnp.max(jnp.where(m_idx == first, neg, grp), axis=1, keepdims=True)
    gscore = (m1 + m2).reshape(N_EXPERT_GROUPS, tm)

    g_idx = lax.broadcasted_iota(I32, gscore.shape, 0)
    grank = jnp.zeros(gscore.shape, I32)
    for g in range(N_EXPERT_GROUPS):
        other = gscore[g:g + 1, :]
        beats = jnp.where(other > gscore, 1, jnp.where(other == gscore, (g_idx > g).astype(I32), 0))
        grank = grank + beats
    gkeep = (grank < TOPK_GROUPS).astype(F32)
    ekeep = jnp.broadcast_to(gkeep.reshape(N_EXPERT_GROUPS, 1, tm), grp.shape).reshape(e, tm)
    masked = jnp.where(ekeep > 0, sel, neg)

    chosen = jnp.zeros((e, tm), F32)
    picks, hits = [], []
    for _ in range(TOP_K):
        m = jnp.max(masked, axis=0, keepdims=True)
        pick = jnp.min(jnp.where(masked == m, e_idx, e), axis=0, keepdims=True)
        hit = e_idx == pick
        chosen = jnp.where(hit, 1.0, chosen)
        masked = jnp.where(hit, neg, masked)
        picks.append(pick)
        hits.append(hit)

    w = scores * chosen
    comb = w / jnp.sum(w, axis=0, keepdims=True) * ROUTED_SCALE

    t_row = lax.broadcasted_iota(I32, (tm, tm), 0)
    t_col = lax.broadcasted_iota(I32, (tm, tm), 1)
    before = (t_row < t_col).astype(BF16)
    rankmat = _dot(chosen.astype(BF16), before) + carry_ref[:e, 0:1]
    carry_ref[:e, :] = carry_ref[:e, :] + jnp.sum(chosen, axis=1, keepdims=True)
    cnt_ref[...] = carry_ref[...]

    idx_ref[...] = jnp.concatenate(picks, axis=0)
    rank_ref[...] = jnp.concatenate(
        [jnp.sum(jnp.where(h, rankmat, 0.0), axis=0, keepdims=True) for h in hits], axis=0).astype(I32)
    wk_ref[...] = jnp.concatenate(
        [jnp.sum(jnp.where(h, comb, 0.0), axis=0, keepdims=True) for h in hits], axis=0)


def _plan_kernel(idx_ref, rank_ref, cnt_ref, pos_ref, te_ref, nv_ref, nu_ref, *, group_tile):
    tf = idx_ref.shape[1]
    nt = te_ref.shape[1]
    cnt = cnt_ref[...].astype(I32)
    padded = (((cnt + (group_tile - 1)) // group_tile) * group_tile).astype(F32)
    e_sub = lax.broadcasted_iota(I32, (LANES, LANES), 0)
    e_lane = lax.broadcasted_iota(I32, (LANES, LANES), 1)
    base = jnp.sum(jnp.where(e_lane < e_sub, padded.T, 0.0), axis=1, keepdims=True)
    end = base + padded[:, 0:1]

    idx = idx_ref[...]
    start = jnp.zeros(idx.shape, F32)
    for e in range(N_EXPERTS):
        start = jnp.where(idx == e, base[e:e + 1, 0:1], start)
    pos = start.astype(I32) + rank_ref[...]
    for j in range(tf // SC_CHUNK):
        pos_ref[j] = pos[:, j * SC_CHUNK:(j + 1) * SC_CHUNK]

    tile_start = (lax.broadcasted_iota(I32, (N_EXPERTS, nt), 1) * group_tile).astype(F32)
    done = jnp.sum(jnp.where(end[:N_EXPERTS] <= tile_start, 1.0, 0.0), axis=0, keepdims=True)
    te_ref[...] = jnp.minimum(done, N_EXPERTS - 1.0).astype(I32)
    in_group = (base[:N_EXPERTS] <= tile_start) & (tile_start < end[:N_EXPERTS])
    real = jnp.clip(base[:N_EXPERTS] + cnt[:N_EXPERTS, 0:1].astype(F32) - tile_start, 0.0, float(group_tile))
    nv_ref[...] = jnp.sum(jnp.where(in_group, real, 0.0), axis=0, keepdims=True).astype(I32)
    total = jnp.sum(padded[:, 0:1], axis=0, keepdims=True)
    nu_ref[...] = jnp.broadcast_to(total * (1.0 / group_tile), nu_ref.shape).astype(I32)


def _plan(idx, rank, counts, *, n_tiles, tf, group_tile):
    t = idx.shape[1]
    nt_pad = -(-n_tiles // LANES) * LANES
    krow = pl.BlockSpec((TOP_K, tf), lambda i: (0, i))
    return pl.pallas_call(
        functools.partial(_plan_kernel, group_tile=group_tile),
        grid=(t // tf,),
        in_specs=[krow, krow, pl.BlockSpec((LANES, LANES), lambda i: (0, 0))],
        out_specs=[pl.BlockSpec((tf // SC_CHUNK, TOP_K, SC_CHUNK), lambda i: (i, 0, 0)),
                   pl.BlockSpec((1, nt_pad), lambda i: (0, 0)),
                   pl.BlockSpec((1, nt_pad), lambda i: (0, 0)),
                   pl.BlockSpec((1, LANES), lambda i: (0, 0))],
        out_shape=[jax.ShapeDtypeStruct((t // SC_CHUNK, TOP_K, SC_CHUNK), I32),
                   jax.ShapeDtypeStruct((1, nt_pad), I32), jax.ShapeDtypeStruct((1, nt_pad), I32),
                   jax.ShapeDtypeStruct((1, LANES), I32)],
        compiler_params=_cparams("arbitrary"),
        name="moe_plan",
    )(idx, rank, counts)


def _sc_mesh_info():
    info = plsc.get_sparse_core_info()
    mesh = plsc.VectorSubcoreMesh(core_axis_name="c", subcore_axis_name="s")
    return mesh, info.num_cores, info.num_cores * info.num_subcores


def _sc_dispatch(pieces, pos, *, n_rows):
    t = pieces[0].shape[0]
    mesh, n_cores, n_workers = _sc_mesh_info()
    per_w = t // SC_CHUNK // n_workers

    @functools.partial(
        pl.kernel, mesh=mesh,
        out_type=[jax.ShapeDtypeStruct((n_rows, LANES), I32)] * N_PIECES,
        scratch_types=[pltpu.VMEM((TOP_K, SC_CHUNK), I32),
                       pltpu.VMEM((N_PIECES, SC_CHUNK, LANES), I32),
                       pltpu.SemaphoreType.DMA((N_PIECES,)),
                       pltpu.SemaphoreType.DMA],
        name="sc_dispatch",
    )
    def run(*refs):
        src = refs[:N_PIECES]
        pos_hbm = refs[N_PIECES]
        dst = refs[N_PIECES + 1:2 * N_PIECES + 1]
        idx_v, rows_v, load_sem, put_sem = refs[2 * N_PIECES + 1:]
        wid = lax.axis_index("s") * n_cores + lax.axis_index("c")

        @pl.loop(0, per_w)
        def _(j):
            ch = wid * per_w + j
            t0 = pl.multiple_of(ch * SC_CHUNK, SC_CHUNK)
            loads = [pltpu.make_async_copy(src[c].at[pl.ds(t0, SC_CHUNK)], rows_v.at[c], load_sem.at[c])
                     for c in range(N_PIECES)]
            for ld in loads:
                ld.start()
            pltpu.sync_copy(pos_hbm.at[ch], idx_v)
            puts = []
            for c in range(N_PIECES):
                loads[c].wait()
                for k in range(TOP_K):
                    puts.append(pltpu.make_async_copy(rows_v.at[c], dst[c].at[idx_v.at[k]], put_sem))
                    puts[-1].start()
            for cp in puts:
                cp.wait()

    return run(*pieces, pos)


SC_GROUP = 32


def _sc_combine(pieces, pos, wts, *, n_tokens):
    mesh, n_cores, n_workers = _sc_mesh_info()
    lanes = plsc.get_sparse_core_info().num_lanes
    per_w = n_tokens // SC_GROUP // n_workers

    @functools.partial(
        pl.kernel, mesh=mesh,
        out_type=[jax.ShapeDtypeStruct((n_tokens, LANES), I32)] * N_PIECES,
        scratch_types=[pltpu.VMEM((TOP_K, SC_GROUP), I32),
                       pltpu.VMEM((TOP_K, SC_GROUP), F32),
                       pltpu.VMEM((2, TOP_K, SC_GROUP, LANES), I32),
                       pltpu.VMEM((2, SC_GROUP, LANES), I32),
                       pltpu.SemaphoreType.DMA((2,)),
                       pltpu.SemaphoreType.DMA((2,))],
        compiler_params=pltpu.CompilerParams(needs_layout_passes=False),
        name="sc_combine",
    )
    def run(*refs):
        src = refs[:N_PIECES]
        pos_hbm, wts_hbm = refs[N_PIECES:N_PIECES + 2]
        dst = refs[N_PIECES + 2:2 * N_PIECES + 2]
        idx_v, w_v, buf, acc, get_sem, put_sem = refs[2 * N_PIECES + 2:]
        wid = lax.axis_index("s") * n_cores + lax.axis_index("c")

        @pl.loop(0, per_w)
        def _(j):
            grp = wid * per_w + j
            t0 = pl.multiple_of(grp * SC_GROUP, SC_GROUP)
            pltpu.sync_copy(pos_hbm.at[grp], idx_v)
            pltpu.sync_copy(wts_hbm.at[grp], w_v)

            def gets(c, slot):
                return [pltpu.make_async_copy(src[c].at[idx_v.at[k]], buf.at[slot, k], get_sem.at[slot])
                        for k in range(TOP_K)]

            def puts(c, slot):
                return [pltpu.make_async_copy(acc.at[slot], dst[c].at[pl.ds(t0, SC_GROUP)], put_sem.at[slot])]

            for cp in gets(0, 0):
                cp.start()
            for c in range(N_PIECES):
                slot = c % 2
                if c + 1 < N_PIECES:
                    for cp in gets(c + 1, 1 - slot):
                        cp.start()
                for cp in gets(c, slot):
                    cp.wait()
                if c >= 2:
                    for cp in puts(c - 2, slot):
                        cp.wait()

                @pl.loop(0, SC_GROUP)
                def _(r):
                    row = jnp.full((lanes,), r, I32)
                    w = []
                    for k in range(TOP_K):
                        w_k = plsc.load_gather(w_v, [jnp.full((lanes,), k, I32), row])
                        w.append(plsc.pack(w_k, w_k, format=plsc.PackFormat.INTERLEAVED))
                    for q in range(LANES // lanes):
                        cols = pl.ds(q * lanes, lanes)
                        total = None
                        for k in range(TOP_K):
                            term = plsc.bitcast(buf[slot, k, r, cols], BF16) * w[k]
                            total = term if total is None else total + term
                        acc[slot, r, cols] = plsc.bitcast(total, I32)

                for cp in puts(c, slot):
                    cp.start()
            for c in range(N_PIECES - 2, N_PIECES):
                for cp in puts(c, c % 2):
                    cp.wait()

    return run(*pieces, pos, wts)


def _group_tile(n_tokens):
    per_expert = n_tokens * TOP_K // N_EXPERTS
    return max(MXU_DIM, min(4 * MXU_DIM, per_expert // MXU_DIM * MXU_DIM))


SECOND_DMA_QUEUE = 1


def _experts_kernel(te_ref, nv_ref, nu_ref, *refs, tile):
    x_hbm = refs[:N_PIECES]
    wg_ref, wu_ref, wd_ref = refs[N_PIECES:N_PIECES + 3]
    y_hbm = refs[N_PIECES + 3:2 * N_PIECES + 3]
    xbuf, xsem, ybuf, ysem = refs[2 * N_PIECES + 3:]
    i = pl.program_id(0)
    n_used = nu_ref[0]
    slot = lax.rem(i, 2)

    half = tile // 2

    def x_copy(step, into, h, c):
        r = pl.ds(pl.multiple_of(step * tile + h * half, half), half)
        return pltpu.make_async_copy(x_hbm[c].at[r], xbuf.at[into, c, pl.ds(h * half, half)], xsem.at[into, c])

    def y_copy(step, out_of, h, c):
        r = pl.ds(pl.multiple_of(step * tile + h * half, half), half)
        return pltpu.make_async_copy(ybuf.at[out_of, c, pl.ds(h * half, half)], y_hbm[c].at[r], ysem.at[out_of, c])

    def real_halves(copy, step, buf, act):
        for c in range(N_PIECES):
            act(copy(step, buf, 0, c))

        @pl.when(nv_ref[step] > half)
        def _():
            for c in range(N_PIECES):
                act(copy(step, buf, 1, c))

    start = lambda cp: cp.start(priority=SECOND_DMA_QUEUE)
    wait = lambda cp: cp.wait()

    @pl.when(i == 0)
    def _():
        real_halves(x_copy, 0, 0, start)

    @pl.when(i + 1 < n_used)
    def _():
        real_halves(x_copy, i + 1, 1 - slot, start)

    @pl.when(i < n_used)
    def _():
        real_halves(x_copy, i, slot, wait)
        subs = [slice(s * MXU_DIM, (s + 1) * MXU_DIM) for s in range(tile // MXU_DIM)]
        xs = []
        for rows in subs:
            lo, hi = _unpack_rows([xbuf[slot, c, rows, :] for c in range(N_PIECES)])
            xs.append(jnp.concatenate(lo + hi, axis=1).astype(BF16))
        gates = [(_dot(x, wg_ref[...]), _dot(x, wu_ref[...])) for x in xs]
        ys = [_dot((_silu(g) * u).astype(BF16), wd_ref[...]) for g, u in gates]
        for rows, y in zip(subs, ys):
            for c, piece in enumerate(_pack_rows(y)):
                ybuf[slot, c, rows, :] = piece

        @pl.when(i >= 1)
        def _():
            real_halves(y_copy, i - 1, 1 - slot, wait)

        real_halves(y_copy, i, slot, start)

        @pl.when(i == n_used - 1)
        def _():
            real_halves(y_copy, i, slot, wait)


def _experts(x_pieces, tile_expert, tile_rows, n_used, wg, wu, wd, *, group_tile):
    n_rows = x_pieces[0].shape[0]
    n_tiles = n_rows // group_tile

    wspec = lambda a: pl.BlockSpec((None,) + a.shape[1:],
                                   lambda i, te, nv, nu: (te[jnp.minimum(i, nu[0] - 1)], 0, 0))
    return pl.pallas_call(
        functools.partial(_experts_kernel, tile=group_tile),
        grid_spec=pltpu.PrefetchScalarGridSpec(
            num_scalar_prefetch=3,
            grid=(n_tiles,),
            in_specs=[pl.BlockSpec(memory_space=pl.ANY)] * N_PIECES + [wspec(wg), wspec(wu), wspec(wd)],
            out_specs=[pl.BlockSpec(memory_space=pl.ANY)] * N_PIECES,
            scratch_shapes=[pltpu.VMEM((2, N_PIECES, group_tile, LANES), I32),
                            pltpu.SemaphoreType.DMA((2, N_PIECES))] * 2),
        out_shape=[jax.ShapeDtypeStruct((n_rows, LANES), I32)] * N_PIECES,
        compiler_params=_cparams("arbitrary"),
        name="experts",
    )(tile_expert, tile_rows, n_used, *x_pieces, wg, wu, wd)


def _moe_out_kernel(x1_ref, mod_ref, fg_ref, sg_ref, su_ref, sd_ref, *refs):
    h_refs = refs[:N_PIECES]
    routed_refs = refs[N_PIECES:2 * N_PIECES]
    out_ref = refs[2 * N_PIECES]
    h_lo, h_hi = _unpack_rows([r[...] for r in h_refs])
    h = jnp.concatenate(h_lo + h_hi, axis=1).astype(BF16)
    hid = _silu(_dot(h, sg_ref[...])) * _dot(h, su_ref[...])
    shared = _dot(hid.astype(BF16), sd_ref[...])
    r_lo, r_hi = _unpack_rows([r[...] for r in routed_refs])
    routed = jnp.concatenate(r_lo + r_hi, axis=1)
    x2 = x1_ref[...] + mod_ref[0, 5:6, :] * (shared + routed)
    out_ref[...] = x2 * lax.rsqrt(jnp.mean(x2 * x2, axis=-1, keepdims=True) + EPS) * fg_ref[...]


def _moe_out(h2_pieces, x1, mods, final_g, sg, su, sd, routed_pieces, *, tokens_per_mod, tm):
    t = x1.shape[0]
    tiles_per_mod = tokens_per_mod // tm
    row = lambda w: pl.BlockSpec((tm, w), lambda i: (i, 0))
    full = lambda a: pl.BlockSpec(a.shape, lambda i: (0,) * a.ndim)
    return pl.pallas_call(
        _moe_out_kernel,
        grid=(t // tm,),
        in_specs=[row(D_MODEL),
                  pl.BlockSpec((1, 6, D_MODEL), lambda i: (i // tiles_per_mod, 0, 0)),
                  full(final_g), full(sg), full(su), full(sd)]
        + [row(LANES)] * (2 * N_PIECES),
        out_specs=row(D_MODEL),
        out_shape=jax.ShapeDtypeStruct((t, D_MODEL), F32),
        compiler_params=_cparams("parallel"),
        name="moe_out",
    )(x1, mods, final_g, sg, su, sd, *h2_pieces, *routed_pieces)


def _trunk(x, mods, s0, w, expert_w, *, batch, seq_len, on_grid):
    t = batch * seq_len
    tokens_per_mod = t // mods.shape[0]
    cos_t, sin_t = _rope_tables(max(seq_len, PROJ_TILE))
    to_cast = expert_w if expert_w[0].dtype != BF16 else ()
    (q, k, v, gsw, up, ga, gb), casted = _inproj(x, mods, w["norm1_g"], w["w_in"], cos_t, sin_t, to_cast,
                                                 tokens_per_mod=tokens_per_mod, seq_len=seq_len,
                                                 on_grid=on_grid, tm=PROJ_TILE)
    expert_w = casted or expert_w
    z, s_f, s_b = _retention(q, k, v, gsw, w["dec"], s0, batch=batch, seq_len=seq_len)
    p = _pool(up, w["pool_w"], w["pool_scale"], batch=batch, seq_len=seq_len, on_grid=on_grid)
    x1, h2_pieces, (idx, rank, wts, counts) = _merge(
        x, z, p, ga, gb, mods, w["norm2_g"], w["w_br_ret"], w["w_br_pool"], w["w_out"],
        w["router_wt"], w["router_bias"], tokens_per_mod=tokens_per_mod, tm=MERGE_TILE)

    group_tile = _group_tile(t)
    n_rows = t * TOP_K + N_EXPERTS * group_tile
    pos, tile_expert, tile_rows, n_used = _plan(idx, rank, counts, n_tiles=n_rows // group_tile, tf=PLAN_TILE,
                                                group_tile=group_tile)
    x_sorted = _sc_dispatch(h2_pieces, pos, n_rows=n_rows)
    y_sorted = _experts(x_sorted, tile_expert.reshape(-1), tile_rows.reshape(-1), n_used.reshape(-1),
                        *expert_w, group_tile=group_tile)
    regroup = lambda a: a.reshape(TOP_K, t // SC_GROUP, SC_GROUP).transpose(1, 0, 2)
    pos_rows = pos.transpose(1, 0, 2).reshape(TOP_K, t)
    routed = _sc_combine(y_sorted, regroup(pos_rows), regroup(wts), n_tokens=t)
    y = _moe_out(h2_pieces, x1, mods, w["final_g"], w["sh_w_gate"], w["sh_w_up"], w["sh_w_down"], routed,
                 tokens_per_mod=tokens_per_mod, tm=OUT_TILE)
    return y, s_f, s_b, expert_w


def kernel(x_prompt, x_sample, state_ret_fwd, state_ret_bwd, c, c_ctx, ada_w, ada_b, norm1_g, norm2_g, w_in,
           ret_decay_fwd, ret_decay_bwd, w_br_ret, pool_w, pool_scale, w_br_pool, w_out, router_w, router_bias,
           exp_w_gate, exp_w_up, exp_w_down, sh_w_gate, sh_w_up, sh_w_down, final_norm_g):
    n_req, seq, d = x_prompt.shape
    n_dec, dec_seq, _ = x_sample.shape
    depth = ada_w.shape[0]
    assert depth == 1 and d == D_MODEL

    xc = x_prompt.reshape(n_req * seq, d)
    xs = x_sample.reshape(n_dec * dec_seq, d)
    new_f, new_b = [], []
    for l in range(depth):
        c_rows = jnp.concatenate([c_ctx[None, :], c, jnp.zeros((8 - 1 - n_dec, d), F32)], axis=0)
        mods = _ada(c_rows, ada_w[l], ada_b[l]).reshape(8, 6, d)
        pad_rows = LANES - N_EXPERTS
        w = dict(
            norm1_g=norm1_g[l].reshape(1, d), norm2_g=norm2_g[l].reshape(1, d),
            final_g=final_norm_g.reshape(1, d),
            w_in=w_in[l].astype(BF16),
            dec=jnp.stack([ret_decay_fwd[l], ret_decay_bwd[l]]).astype(F32),
            w_br_ret=w_br_ret[l].astype(BF16), pool_w=pool_w[l].astype(BF16),
            pool_scale=pool_scale[l].reshape(1, POOL_W), w_br_pool=w_br_pool[l].astype(BF16),
            w_out=w_out[l].astype(BF16),
            router_wt=jnp.pad(router_w[l].T, ((0, pad_rows), (0, 0))).astype(BF16),
            router_bias=jnp.pad(router_bias[l].astype(F32).reshape(N_EXPERTS, 1), ((0, pad_rows), (0, 0))),
            sh_w_gate=sh_w_gate[l].astype(BF16),
            sh_w_up=sh_w_up[l].astype(BF16), sh_w_down=sh_w_down[l].astype(BF16),
        )
        cached = (state_ret_fwd[:, l].astype(F32), state_ret_bwd[:, l].astype(F32))
        expert_w = (exp_w_gate[l], exp_w_up[l], exp_w_down[l])
        xs, _, _, expert_w = _trunk(xs, mods[1:1 + n_dec], cached, w, expert_w,
                                    batch=n_dec, seq_len=dec_seq, on_grid=True)
        xc, s_f, s_b, _ = _trunk(xc, mods[0:1], None, w, expert_w, batch=n_req, seq_len=seq, on_grid=False)
        new_f.append(s_f)
        new_b.append(s_b)
    y_prompt = xc.reshape(n_req, seq, d)
    y_sample = xs.reshape(n_dec, dec_seq, d)
    return (y_prompt, y_sample, jnp.stack(new_f, axis=1).astype(x_prompt.dtype),
            jnp.stack(new_b, axis=1).astype(x_prompt.dtype))
```

```python
import functools
import math

import numpy as np
import jax
import jax.numpy as jnp
from jax import lax
from jax.experimental import pallas as pl
from jax.experimental.pallas import tpu as pltpu
from jax.experimental.pallas import tpu_sc as plsc

D_MODEL = 1024
GRID_W = 64
RET_HEADS = 4
RET_DK = 128
RET_DV = 256
RET_QK_W = RET_HEADS * RET_DK
RET_V_W = RET_HEADS * RET_DV
RET_CHUNK = 128
ROPE_BASE = 10000.0
POOL_GROUPS = 4
POOL_CH = 128
POOL_W = POOL_GROUPS * POOL_CH
POOL_WINDOWS = (2, 4, 8, 16)
N_EXPERTS = 64
TOP_K = 8
N_EXPERT_GROUPS = 8
GROUP_SIZE = N_EXPERTS // N_EXPERT_GROUPS
TOPK_GROUPS = 4
D_EXPERT = 256
ROUTED_SCALE = 2.5
EPS = 1e-6
IN_SIZES = (RET_QK_W, RET_QK_W, RET_V_W, RET_V_W, POOL_W, D_MODEL, D_MODEL)
IN_OFFS = tuple(sum(IN_SIZES[:i]) for i in range(len(IN_SIZES) + 1))
IN_W = IN_OFFS[-1]

LANES = 128
VMEM_LIMIT = 56 << 20
N_PIECES = D_MODEL // 2 // LANES
MXU_DIM = 256
SC_CHUNK = 128
PROJ_TILE = 512
PLAN_TILE = 2048
OUT_TILE = 1024
POOL_SEQS_PER_STEP = 4
HIGH_HALF = 0xFFFF0000

F32 = jnp.float32
BF16 = jnp.bfloat16
I32 = jnp.int32
U32 = jnp.uint32


def _cparams(*sem):
    return pltpu.CompilerParams(dimension_semantics=sem, vmem_limit_bytes=VMEM_LIMIT)


def _dot(a, b):
    return jnp.dot(a, b, preferred_element_type=F32)


def _silu(x):
    return x * jax.nn.sigmoid(x)


def _rms_mod(x, g, scale, shift):
    y = x * lax.rsqrt(jnp.mean(x * x, axis=-1, keepdims=True) + EPS)
    return (y * g) * (1.0 + scale) + shift


def _ada_kernel(c_ref, w_ref, b_ref, o_ref):
    c = c_ref[...]
    o_ref[...] = jnp.dot(_silu(c), w_ref[...], preferred_element_type=F32,
                         precision=lax.Precision.HIGHEST) + b_ref[...]


def _ada(c_rows, ada_w, ada_b):
    r = c_rows.shape[0]
    n = ada_w.shape[1]
    tn = 2 * D_MODEL
    return pl.pallas_call(
        _ada_kernel,
        grid=(n // tn,),
        in_specs=[pl.BlockSpec((r, D_MODEL), lambda j: (0, 0)),
                  pl.BlockSpec((D_MODEL, tn), lambda j: (0, j)),
                  pl.BlockSpec((1, tn), lambda j: (0, j))],
        out_specs=pl.BlockSpec((r, tn), lambda j: (0, j)),
        out_shape=jax.ShapeDtypeStruct((r, n), F32),
        compiler_params=_cparams("parallel"),
        name="ada_mod",
    )(c_rows, ada_w, ada_b.reshape(1, n))


def _inproj_kernel(x_ref, mod_ref, g_ref, w_ref, cos_ref, sin_ref, *refs, on_grid):
    n_side = (len(refs) - len(IN_SIZES)) // 2
    side_in = refs[:n_side]
    q_ref, k_ref, v_ref, gsw_ref, up_ref, ga_ref, gb_ref = refs[n_side:n_side + len(IN_SIZES)]
    side_out = refs[n_side + len(IN_SIZES):]
    for src, dst in zip(side_in, side_out):
        dst[...] = src[...].astype(BF16)
    for s in range(x_ref.shape[0] // MXU_DIM):
        rows = slice(s * MXU_DIM, (s + 1) * MXU_DIM)
        h = _rms_mod(x_ref[rows, :], g_ref[...], mod_ref[0, 1:2, :], mod_ref[0, 0:1, :]).astype(BF16)

        def seg(i):
            return _dot(h, w_ref[:, IN_OFFS[i]:IN_OFFS[i + 1]])

        q = seg(0)
        k = seg(1)
        if on_grid:
            cos = jnp.concatenate([cos_ref[rows, :]] * RET_HEADS, axis=1)
            sin = jnp.concatenate([sin_ref[rows, :]] * RET_HEADS, axis=1)
            quarter = RET_DK // 4
            lane = lax.broadcasted_iota(jnp.int32, q.shape, 1)
            first = (lane & (2 * quarter - 1)) < quarter

            def rope(a):
                up = pltpu.roll(a, RET_QK_W - quarter, axis=1)
                dn = pltpu.roll(a, quarter, axis=1)
                return a * cos + jnp.where(first, up, dn) * sin

            q = rope(q)
            k = rope(k)
        q_ref[rows, :] = q.astype(BF16)
        k_ref[rows, :] = (k * (RET_DK ** -0.5)).astype(BF16)
        v_ref[rows, :] = seg(2).astype(BF16)
        gsw_ref[rows, :] = seg(3).astype(BF16)
        up_ref[rows, :] = seg(4).astype(BF16)
        ga_ref[rows, :] = seg(5).astype(BF16)
        gb_ref[rows, :] = seg(6).astype(BF16)


def _inproj(x, mods, norm_g, w_in, cos_t, sin_t, side_cast=(), *, tokens_per_mod, seq_len, on_grid, tm):
    t = x.shape[0]
    steps = t // tm
    tiles_per_mod = tokens_per_mod // tm
    tiles_per_seq = max(seq_len // tm, 1)
    widths = IN_SIZES
    out_shape = [jax.ShapeDtypeStruct((t, w), BF16) for w in widths]
    out_specs = [pl.BlockSpec((tm, w), lambda i: (i, 0)) for w in widths]
    side_specs = [pl.BlockSpec((a.shape[0] // steps,) + a.shape[1:], lambda i: (i, 0, 0)) for a in side_cast]
    outs = pl.pallas_call(
        functools.partial(_inproj_kernel, on_grid=on_grid),
        grid=(steps,),
        in_specs=[pl.BlockSpec((tm, D_MODEL), lambda i: (i, 0)),
                  pl.BlockSpec((1, 6, D_MODEL), lambda i: (i // tiles_per_mod, 0, 0)),
                  pl.BlockSpec((1, D_MODEL), lambda i: (0, 0)),
                  pl.BlockSpec((D_MODEL, IN_W), lambda i: (0, 0), pipeline_mode=pl.Buffered(1)),
                  pl.BlockSpec((tm, RET_DK), lambda i: (i % tiles_per_seq, 0)),
                  pl.BlockSpec((tm, RET_DK), lambda i: (i % tiles_per_seq, 0))] + side_specs,
        out_specs=out_specs + side_specs,
        out_shape=out_shape + [jax.ShapeDtypeStruct(a.shape, BF16) for a in side_cast],
        compiler_params=_cparams("parallel"),
        name="inproj_grid" if on_grid else "inproj_seq",
    )(x, mods, norm_g, w_in, cos_t, sin_t, *side_cast)
    return outs[:len(widths)], tuple(outs[len(widths):])


def _rope_tables(seq_len):
    t = np.arange(seq_len)
    row = (t // GRID_W).astype(np.float32)
    col = (t % GRID_W).astype(np.float32)
    m = RET_DK // 4
    inv = (np.float32(ROPE_BASE) ** (-np.arange(m, dtype=np.float32) / np.float32(m))).astype(np.float32)
    ar = row[:, None] * inv
    ac = col[:, None] * inv
    cos = np.concatenate([np.cos(ar), np.cos(ar), np.cos(ac), np.cos(ac)], axis=1)
    sin = np.concatenate([-np.sin(ar), np.sin(ar), -np.sin(ac), np.sin(ac)], axis=1)
    return jnp.asarray(cos, F32), jnp.asarray(sin, F32)


def _ret_heads_per_step(seq_len):
    per_head = seq_len * (2 * 2 * (2 * RET_DK + 3 * RET_DV) + 4 * RET_DV + 2 * RET_DK)
    heads = RET_HEADS
    while heads > 1 and heads * per_head > VMEM_LIMIT * 3 // 4:
        heads //= 2
    return heads


def _ret_kernel(dec_ref, q_ref, k_ref, v_ref, g_ref, *refs, n_chunks, heads, zero_init):
    s0_refs = () if zero_init else refs[:2]
    z_ref, sf_ref, sb_ref, oacc_ref, kt_ref = refs[len(s0_refs):]
    c = RET_CHUNK
    half = n_chunks // 2
    ii = lax.broadcasted_iota(I32, (c, c), 0)
    jj = lax.broadcasted_iota(I32, (c, c), 1)
    ik = lax.broadcasted_iota(I32, (c, RET_DK), 0).astype(F32)
    jk = lax.broadcasted_iota(I32, (RET_DK, c), 1).astype(F32)

    def log_gamma(d, shape):
        return jnp.log1p(-jnp.exp2(-jnp.full(shape, d, F32)))

    consts = {}
    for hh in range(heads):
        h = pl.program_id(1) * heads + hh
        dec_f = dec_ref[0, h]
        dec_b = dec_ref[1, h]
        rel = (ii - jj).astype(F32)
        consts[hh, "f"] = (
            jnp.where(rel >= 0, jnp.exp(log_gamma(dec_f, (c, c)) * jnp.maximum(rel, 0.0)), 0.0),
            jnp.exp(log_gamma(dec_f, (c, RET_DK)) * (ik + 1.0)),
            jnp.exp(log_gamma(dec_f, (RET_DK, c)) * (c - 1.0 - jk)),
            jnp.exp(log_gamma(dec_f, (RET_DK, RET_DV)) * c))
        consts[hh, "b"] = (
            jnp.where(rel <= 0, jnp.exp(log_gamma(dec_b, (c, c)) * jnp.maximum(-rel, 0.0)), 0.0),
            jnp.exp(log_gamma(dec_b, (c, RET_DK)) * (c - ik)),
            jnp.exp(log_gamma(dec_b, (RET_DK, c)) * jk),
            jnp.exp(log_gamma(dec_b, (RET_DK, RET_DV)) * c))

    for s_ref, s0_ref in zip((sf_ref, sb_ref), s0_refs or (None, None)):
        s_ref[...] = jnp.zeros(s_ref.shape, F32) if zero_init else s0_ref[...]

    def scores(ci, hh, direction, second):
        r = pl.ds(pl.multiple_of(ci * c, c), c)
        kcols = slice(hh * RET_DK, (hh + 1) * RET_DK)
        qc = q_ref[r, kcols]
        kc = k_ref[r, kcols]
        if not second:
            kt_ref[hh, ci] = kc.T
        sc = lax.dot_general(qc, kc, (((1,), (1,)), ((), ())), preferred_element_type=F32)
        return ci, hh, direction, r, qc, sc

    def advance(job):
        ci, hh, direction, r, qc, sc = job
        dmask, qdec, kdec, cdec = consts[hh, direction]
        s_ref = sf_ref if direction == "f" else sb_ref
        vc = v_ref[r, hh * RET_DV:(hh + 1) * RET_DV]
        s = s_ref[hh]
        lhs = jnp.concatenate([(sc * dmask).astype(BF16), (qc.astype(F32) * qdec).astype(BF16)], axis=1)
        o = _dot(lhs, jnp.concatenate([vc, s.astype(BF16)], axis=0))
        kd_t = (kt_ref[hh, ci].astype(F32) * kdec).astype(BF16)
        s_ref[hh] = s * cdec + _dot(kd_t, vc)
        return o

    def emit(job, o, second):
        _, hh, _, r, _, _ = job
        vcols = slice(hh * RET_DV, (hh + 1) * RET_DV)
        if not second:
            oacc_ref[hh, r, :] = o
        else:
            o = o + oacc_ref[hh, r, :]
            o = o * lax.rsqrt(jnp.mean(o * o, axis=-1, keepdims=True) + EPS)
            g = g_ref[r, vcols].astype(F32)
            z_ref[r, vcols] = (_silu(g) * o).astype(BF16)

    def body(second):
        def run(t, carry):
            jobs = [scores(ci, hh, d, second) for hh in range(heads)
                    for ci, d in ((t, "f"), (n_chunks - 1 - t, "b"))]
            outs = [advance(job) for job in jobs]
            for job, o in zip(jobs, outs):
                emit(job, o, second)
            return carry
        return run

    lax.fori_loop(0, half, body(False), 0, unroll=8 if half % 8 == 0 else 1)
    lax.fori_loop(half, n_chunks, body(True), 0, unroll=4 if half % 4 == 0 else 1)


def _retention(q, k, v, gsw, dec, s0, *, batch, seq_len):
    n_chunks = seq_len // RET_CHUNK
    assert n_chunks % 2 == 0
    heads = _ret_heads_per_step(seq_len)
    t = batch * seq_len
    st_spec = pl.BlockSpec((None, heads, RET_DK, RET_DV), lambda b, h: (b, h, 0, 0))
    st_shape = jax.ShapeDtypeStruct((batch, RET_HEADS, RET_DK, RET_DV), F32)
    kspec = pl.BlockSpec((seq_len, heads * RET_DK), lambda b, h: (b, h))
    vspec = pl.BlockSpec((seq_len, heads * RET_DV), lambda b, h: (b, h))
    return pl.pallas_call(
        functools.partial(_ret_kernel, n_chunks=n_chunks, heads=heads, zero_init=s0 is None),
        grid=(batch, RET_HEADS // heads),
        in_specs=[pl.BlockSpec(memory_space=pltpu.SMEM), kspec, kspec, vspec, vspec]
        + ([] if s0 is None else [st_spec, st_spec]),
        out_specs=[vspec, st_spec, st_spec],
        out_shape=[jax.ShapeDtypeStruct((t, RET_V_W), BF16), st_shape, st_shape],
        scratch_shapes=[pltpu.VMEM((heads, seq_len, RET_DV), F32),
                        pltpu.VMEM((heads, n_chunks, RET_DK, RET_CHUNK), BF16)],
        compiler_params=_cparams("parallel", "parallel"),
        name=f"retention_l{seq_len}",
    )(dec, q, k, v, gsw, *(s0 or ()))


def _pool_kernel(u_ref, w_ref, sc_ref, o_ref, *, n_tok, width, two_d):
    n_rows = n_tok // width
    pos = lax.broadcasted_iota(I32, (width, POOL_CH), 0)

    def every_row(a):
        return jnp.concatenate([a] * n_rows, axis=0) if n_rows > 1 else a

    def shift_in_row(a, s):
        ok = (pos < width - s) if s > 0 else (pos >= -s)
        return pltpu.roll(a, (-s) % n_tok, axis=0) * every_row(jnp.where(ok, 1.0, 0.0))

    def shift_rows(a, m):
        k = abs(m) * width
        zeros = jnp.zeros((k, POOL_CH), F32)
        return (jnp.concatenate([a[k:], zeros], axis=0) if m > 0
                else jnp.concatenate([zeros, a[:n_tok - k]], axis=0))

    def box_sum(a, half, shift):
        fw = a
        bw = shift(a, -1)
        m = 1
        while m < half:
            fw = fw + shift(fw, m)
            bw = bw + shift(bw, -m)
            m *= 2
        return fw + bw

    def inv_count(p, half, extent):
        return 1.0 / (jnp.minimum(p + half, extent) - jnp.maximum(p - half, 0)).astype(F32)

    for g, window in enumerate(POOL_WINDOWS):
        half = window // 2
        cols = slice(g * POOL_CH, (g + 1) * POOL_CH)
        ug = u_ref[:, cols].astype(F32)
        total = box_sum(ug, half, shift_in_row)
        inv = every_row(inv_count(pos, half, width))
        if two_d:
            total = box_sum(total, half, shift_rows)
            row = lax.broadcasted_iota(I32, (n_rows, 1, POOL_CH), 0)
            inv_r = jnp.broadcast_to(inv_count(row, half, n_rows), (n_rows, width, POOL_CH))
            inv = inv * inv_r.reshape(n_tok, POOL_CH)
        d = (total * inv - ug).astype(BF16)
        o_ref[:, cols] = (_dot(d, w_ref[g]) * sc_ref[:, cols]).astype(BF16)


def _pool(u, pool_w, pool_scale, *, batch, seq_len, on_grid):
    t = batch * seq_len
    width = GRID_W if on_grid else seq_len
    n_tok = seq_len if on_grid else seq_len * math.gcd(batch, POOL_SEQS_PER_STEP)
    return pl.pallas_call(
        functools.partial(_pool_kernel, n_tok=n_tok, width=width, two_d=on_grid),
        grid=(t // n_tok,),
        in_specs=[pl.BlockSpec((n_tok, POOL_W), lambda b: (b, 0)),
                  pl.BlockSpec((POOL_GROUPS, POOL_CH, POOL_CH), lambda b: (0, 0, 0)),
                  pl.BlockSpec((1, POOL_W), lambda b: (0, 0))],
        out_specs=pl.BlockSpec((n_tok, POOL_W), lambda b: (b, 0)),
        out_shape=jax.ShapeDtypeStruct((t, POOL_W), BF16),
        compiler_params=_cparams("parallel"),
        name=f"pool_l{seq_len}",
    )(u, pool_w, pool_scale)


def _pack_rows(x):
    half = D_MODEL // 2
    lo = lax.bitcast_convert_type(x[:, :half].astype(BF16).astype(F32), U32) >> 16
    hi = lax.bitcast_convert_type(x[:, half:].astype(BF16).astype(F32), U32) & jnp.uint32(HIGH_HALF)
    word = lax.bitcast_convert_type(hi | lo, I32)
    return [word[:, c * LANES:(c + 1) * LANES] for c in range(N_PIECES)]


def _unpack_rows(pieces):
    words = [lax.bitcast_convert_type(p, U32) for p in pieces]
    lo = [lax.bitcast_convert_type(w << 16, F32) for w in words]
    hi = [lax.bitcast_convert_type(w & jnp.uint32(HIGH_HALF), F32) for w in words]
    return lo, hi


def _merge_kernel(x_ref, z_ref, p_ref, ga_ref, gb_ref, mod_ref, g2_ref, wr_ref, wp_ref, wo_ref, rw_ref, bias_ref,
                  x1_ref, *refs):
    piece_refs, route_refs = refs[:N_PIECES], refs[N_PIECES:]
    subs = [slice(s * MXU_DIM, (s + 1) * MXU_DIM) for s in range(x_ref.shape[0] // MXU_DIM)]
    branches = [(_dot(z_ref[r, :], wr_ref[...]), _dot(p_ref[r, :], wp_ref[...])) for r in subs]
    merged = [(jax.nn.sigmoid(ga_ref[r, :].astype(F32)) * y_ret
               + jax.nn.sigmoid(gb_ref[r, :].astype(F32)) * y_pool).astype(BF16)
              for r, (y_ret, y_pool) in zip(subs, branches)]
    outs = [_dot(m, wo_ref[...]) for m in merged]
    logits = []
    for r, o in zip(subs, outs):
        x1 = x_ref[r, :] + mod_ref[0, 2:3, :] * o
        x1_ref[r, :] = x1
        h2 = _rms_mod(x1, g2_ref[...], mod_ref[0, 4:5, :], mod_ref[0, 3:4, :])
        logits.append(lax.dot_general(rw_ref[...], h2.astype(BF16), (((1,), (1,)), ((), ())),
                                      preferred_element_type=F32)[:N_EXPERTS])
        for ref, piece in zip(piece_refs, _pack_rows(h2)):
            ref[r, :] = piece
    _route_tile(jnp.concatenate(logits, axis=1), bias_ref, *route_refs)


def _merge(x, z, p, ga, gb, mods, norm2_g, w_br_ret, w_br_pool, w_out, router_wt, bias_col, *, tokens_per_mod, tm):
    t = x.shape[0]
    tiles_per_mod = tokens_per_mod // tm
    row = lambda w: pl.BlockSpec((tm, w), lambda i: (i, 0))
    full = lambda a: pl.BlockSpec(a.shape, lambda i: (0,) * len(a.shape))
    krow = pl.BlockSpec((TOP_K, tm), lambda i: (0, i))
    counts = jax.ShapeDtypeStruct((LANES, LANES), F32)
    outs = pl.pallas_call(
        _merge_kernel,
        grid=(t // tm,),
        in_specs=[row(D_MODEL), row(RET_V_W), row(POOL_W), row(D_MODEL), row(D_MODEL),
                  pl.BlockSpec((1, 6, D_MODEL), lambda i: (i // tiles_per_mod, 0, 0)),
                  full(norm2_g), full(w_br_ret), full(w_br_pool), full(w_out), full(router_wt), full(bias_col)],
        out_specs=[row(D_MODEL)] + [row(LANES)] * N_PIECES + [krow, krow, krow, full(counts)],
        out_shape=[jax.ShapeDtypeStruct((t, D_MODEL), F32)] + [jax.ShapeDtypeStruct((t, LANES), I32)] * N_PIECES
        + [jax.ShapeDtypeStruct((TOP_K, t), I32), jax.ShapeDtypeStruct((TOP_K, t), I32),
           jax.ShapeDtypeStruct((TOP_K, t), F32), counts],
        scratch_shapes=[pltpu.VMEM(counts.shape, F32)],
        compiler_params=_cparams("arbitrary"),
        name="merge",
    )(x, z, p, ga, gb, mods, norm2_g, w_br_ret, w_br_pool, w_out, router_wt, bias_col)
    return outs[0], outs[1:1 + N_PIECES], outs[1 + N_PIECES:]


def _route_tile(logits, bias_ref, idx_ref, rank_ref, wk_ref, cnt_ref, carry_ref):
    e = N_EXPERTS
    tm = logits.shape[1]
    neg = -jnp.inf

    @pl.when(pl.program_id(0) == 0)
    def _():
        carry_ref[...] = jnp.zeros(carry_ref.shape, F32)

    scores = jax.nn.sigmoid(logits)
    sel = scores + bias_ref[:e, 0:1]
    e_idx = lax.broadcasted_iota(I32, (e, tm), 0)

    grp = sel.reshape(N_EXPERT_GROUPS, GROUP_SIZE, tm)
    m_idx = lax.broadcasted_iota(I32, grp.shape, 1)
    m1 = jnp.max(grp, axis=1, keepdims=True)
    first = jnp.min(jnp.where(grp == m1, m_idx, GROUP_SIZE), axis=1, keepdims=True)
    m2 = jnp.max(jnp.where(m_idx == first, neg, grp), axis=1, keepdims=True)
    gscore = (m1 + m2).reshape(N_EXPERT_GROUPS, tm)

    g_idx = lax.broadcasted_iota(I32, gscore.shape, 0)
    grank = jnp.zeros(gscore.shape, I32)
    for g in range(N_EXPERT_GROUPS):
        other = gscore[g:g + 1, :]
        beats = jnp.where(other > gscore, 1, jnp.where(other == gscore, (g_idx > g).astype(I32), 0))
        grank = grank + beats
    gkeep = (grank < TOPK_GROUPS).astype(F32)
    ekeep = jnp.broadcast_to(gkeep.reshape(N_EXPERT_GROUPS, 1, tm), grp.shape).reshape(e, tm)
    masked = jnp.where(ekeep > 0, sel, neg)

    chosen = jnp.zeros((e, tm), F32)
    picks, hits = [], []
    for _ in range(TOP_K):
        m = jnp.max(masked, axis=0, keepdims=True)
        pick = jnp.min(jnp.where(masked == m, e_idx, e), axis=0, keepdims=True)
        hit = e_idx == pick
        chosen = jnp.where(hit, 1.0, chosen)
        masked = jnp.where(hit, neg, masked)
        picks.append(pick)
        hits.append(hit)

    w = scores * chosen
    comb = w / jnp.sum(w, axis=0, keepdims=True) * ROUTED_SCALE

    t_row = lax.broadcasted_iota(I32, (tm, tm), 0)
    t_col = lax.broadcasted_iota(I32, (tm, tm), 1)
    before = (t_row < t_col).astype(BF16)
    rankmat = _dot(chosen.astype(BF16), before) + carry_ref[:e, 0:1]
    carry_ref[:e, :] = carry_ref[:e, :] + jnp.sum(chosen, axis=1, keepdims=True)
    cnt_ref[...] = carry_ref[...]

    idx_ref[...] = jnp.concatenate(picks, axis=0)
    rank_ref[...] = jnp.concatenate(
        [jnp.sum(jnp.where(h, rankmat, 0.0), axis=0, keepdims=True) for h in hits], axis=0).astype(I32)
    wk_ref[...] = jnp.concatenate(
        [jnp.sum(jnp.where(h, comb, 0.0), axis=0, keepdims=True) for h in hits], axis=0)


def _plan_kernel(idx_ref, rank_ref, cnt_ref, pos_ref, te_ref, nv_ref, nu_ref, *, group_tile):
    tf = idx_ref.shape[1]
    nt = te_ref.shape[1]
    cnt = cnt_ref[...].astype(I32)
    padded = (((cnt + (group_tile - 1)) // group_tile) * group_tile).astype(F32)
    e_sub = lax.broadcasted_iota(I32, (LANES, LANES), 0)
    e_lane = lax.broadcasted_iota(I32, (LANES, LANES), 1)
    base = jnp.sum(jnp.where(e_lane < e_sub, padded.T, 0.0), axis=1, keepdims=True)
    end = base + padded[:, 0:1]

    idx = idx_ref[...]
    start = jnp.zeros(idx.shape, F32)
    for e in range(N_EXPERTS):
        start = jnp.where(idx == e, base[e:e + 1, 0:1], start)
    pos = start.astype(I32) + rank_ref[...]
    for j in range(tf // SC_CHUNK):
        pos_ref[j] = pos[:, j * SC_CHUNK:(j + 1) * SC_CHUNK]

    tile_start = (lax.broadcasted_iota(I32, (N_EXPERTS, nt), 1) * group_tile).astype(F32)
    done = jnp.sum(jnp.where(end[:N_EXPERTS] <= tile_start, 1.0, 0.0), axis=0, keepdims=True)
    te_ref[...] = jnp.minimum(done, N_EXPERTS - 1.0).astype(I32)
    in_group = (base[:N_EXPERTS] <= tile_start) & (tile_start < end[:N_EXPERTS])
    real = jnp.clip(base[:N_EXPERTS] + cnt[:N_EXPERTS, 0:1].astype(F32) - tile_start, 0.0, float(group_tile))
    nv_ref[...] = jnp.sum(jnp.where(in_group, real, 0.0), axis=0, keepdims=True).astype(I32)
    total = jnp.sum(padded[:, 0:1], axis=0, keepdims=True)
    nu_ref[...] = jnp.broadcast_to(total * (1.0 / group_tile), nu_ref.shape).astype(I32)


def _plan(idx, rank, counts, *, n_tiles, tf, group_tile):
    t = idx.shape[1]
    nt_pad = -(-n_tiles // LANES) * LANES
    krow = pl.BlockSpec((TOP_K, tf), lambda i: (0, i))
    return pl.pallas_call(
        functools.partial(_plan_kernel, group_tile=group_tile),
        grid=(t // tf,),
        in_specs=[krow, krow, pl.BlockSpec((LANES, LANES), lambda i: (0, 0))],
        out_specs=[pl.BlockSpec((tf // SC_CHUNK, TOP_K, SC_CHUNK), lambda i: (i, 0, 0)),
                   pl.BlockSpec((1, nt_pad), lambda i: (0, 0)),
                   pl.BlockSpec((1, nt_pad), lambda i: (0, 0)),
                   pl.BlockSpec((1, LANES), lambda i: (0, 0))],
        out_shape=[jax.ShapeDtypeStruct((t // SC_CHUNK, TOP_K, SC_CHUNK), I32),
                   jax.ShapeDtypeStruct((1, nt_pad), I32), jax.ShapeDtypeStruct((1, nt_pad), I32),
                   jax.ShapeDtypeStruct((1, LANES), I32)],
        compiler_params=_cparams("arbitrary"),
        name="moe_plan",
    )(idx, rank, counts)


def _sc_mesh_info():
    info = plsc.get_sparse_core_info()
    mesh = plsc.VectorSubcoreMesh(core_axis_name="c", subcore_axis_name="s")
    return mesh, info.num_cores, info.num_cores * info.num_subcores


def _sc_dispatch(pieces, pos, *, n_rows):
    t = pieces[0].shape[0]
    mesh, n_cores, n_workers = _sc_mesh_info()
    per_w = t // SC_CHUNK // n_workers

    @functools.partial(
        pl.kernel, mesh=mesh,
        out_type=[jax.ShapeDtypeStruct((n_rows, LANES), I32)] * N_PIECES,
        scratch_types=[pltpu.VMEM((TOP_K, SC_CHUNK), I32),
                       pltpu.VMEM((N_PIECES, SC_CHUNK, LANES), I32),
                       pltpu.SemaphoreType.DMA((N_PIECES,)),
                       pltpu.SemaphoreType.DMA],
        name="sc_dispatch",
    )
    def run(*refs):
        src = refs[:N_PIECES]
        pos_hbm = refs[N_PIECES]
        dst = refs[N_PIECES + 1:2 * N_PIECES + 1]
        idx_v, rows_v, load_sem, put_sem = refs[2 * N_PIECES + 1:]
        wid = lax.axis_index("s") * n_cores + lax.axis_index("c")

        @pl.loop(0, per_w)
        def _(j):
            ch = wid * per_w + j
            t0 = pl.multiple_of(ch * SC_CHUNK, SC_CHUNK)
            loads = [pltpu.make_async_copy(src[c].at[pl.ds(t0, SC_CHUNK)], rows_v.at[c], load_sem.at[c])
                     for c in range(N_PIECES)]
            for ld in loads:
                ld.start()
            pltpu.sync_copy(pos_hbm.at[ch], idx_v)
            puts = []
            for c in range(N_PIECES):
                loads[c].wait()
                for k in range(TOP_K):
                    puts.append(pltpu.make_async_copy(rows_v.at[c], dst[c].at[idx_v.at[k]], put_sem))
                    puts[-1].start()
            for cp in puts:
                cp.wait()

    return run(*pieces, pos)


SC_GROUP = 32


def _sc_combine(pieces, pos, wts, *, n_tokens):
    mesh, n_cores, n_workers = _sc_mesh_info()
    lanes = plsc.get_sparse_core_info().num_lanes
    per_w = n_tokens // SC_GROUP // n_workers

    @functools.partial(
        pl.kernel, mesh=mesh,
        out_type=[jax.ShapeDtypeStruct((n_tokens, LANES), I32)] * N_PIECES,
        scratch_types=[pltpu.VMEM((TOP_K, SC_GROUP), I32),
                       pltpu.VMEM((TOP_K, SC_GROUP), F32),
                       pltpu.VMEM((2, TOP_K, SC_GROUP, LANES), I32),
                       pltpu.VMEM((2, SC_GROUP, LANES), I32),
                       pltpu.SemaphoreType.DMA((2,)),
                       pltpu.SemaphoreType.DMA((2,))],
        compiler_params=pltpu.CompilerParams(needs_layout_passes=False),
        name="sc_combine",
    )
    def run(*refs):
        src = refs[:N_PIECES]
        pos_hbm, wts_hbm = refs[N_PIECES:N_PIECES + 2]
        dst = refs[N_PIECES + 2:2 * N_PIECES + 2]
        idx_v, w_v, buf, acc, get_sem, put_sem = refs[2 * N_PIECES + 2:]
        wid = lax.axis_index("s") * n_cores + lax.axis_index("c")

        @pl.loop(0, per_w)
        def _(j):
            grp = wid * per_w + j
            t0 = pl.multiple_of(grp * SC_GROUP, SC_GROUP)
            pltpu.sync_copy(pos_hbm.at[grp], idx_v)
            pltpu.sync_copy(wts_hbm.at[grp], w_v)

            def gets(c, slot):
                return [pltpu.make_async_copy(src[c].at[idx_v.at[k]], buf.at[slot, k], get_sem.at[slot])
                        for k in range(TOP_K)]

            def puts(c, slot):
                return [pltpu.make_async_copy(acc.at[slot], dst[c].at[pl.ds(t0, SC_GROUP)], put_sem.at[slot])]

            for cp in gets(0, 0):
                cp.start()
            for c in range(N_PIECES):
                slot = c % 2
                if c + 1 < N_PIECES:
                    for cp in gets(c + 1, 1 - slot):
                        cp.start()
                for cp in gets(c, slot):
                    cp.wait()
                if c >= 2:
                    for cp in puts(c - 2, slot):
                        cp.wait()

                @pl.loop(0, SC_GROUP)
                def _(r):
                    row = jnp.full((lanes,), r, I32)
                    w = []
                    for k in range(TOP_K):
                        w_k = plsc.load_gather(w_v, [jnp.full((lanes,), k, I32), row])
                        w.append(plsc.pack(w_k, w_k, format=plsc.PackFormat.INTERLEAVED))
                    for q in range(LANES // lanes):
                        cols = pl.ds(q * lanes, lanes)
                        total = None
                        for k in range(TOP_K):
                            term = plsc.bitcast(buf[slot, k, r, cols], BF16) * w[k]
                            total = term if total is None else total + term
                        acc[slot, r, cols] = plsc.bitcast(total, I32)

                for cp in puts(c, slot):
                    cp.start()
            for c in range(N_PIECES - 2, N_PIECES):
                for cp in puts(c, c % 2):
                    cp.wait()

    return run(*pieces, pos, wts)


def _group_tile(n_tokens):
    per_expert = n_tokens * TOP_K // N_EXPERTS
    return max(MXU_DIM, min(4 * MXU_DIM, per_expert // MXU_DIM * MXU_DIM))


SECOND_DMA_QUEUE = 1
X_RING = 3


def _experts_kernel(te_ref, nv_ref, nu_ref, *refs, tile):
    x_hbm = refs[:N_PIECES]
    wg_ref, wu_ref, wd_ref = refs[N_PIECES:N_PIECES + 3]
    y_hbm = refs[N_PIECES + 3:2 * N_PIECES + 3]
    xbuf, xsem, ybuf, ysem = refs[2 * N_PIECES + 3:]
    i = pl.program_id(0)
    n_used = nu_ref[0]
    slot = lax.rem(i, 2)
    xslot = lax.rem(i, X_RING)

    half = tile // 2

    def x_copy(step, into, h, c):
        r = pl.ds(pl.multiple_of(step * tile + h * half, half), half)
        return pltpu.make_async_copy(x_hbm[c].at[r], xbuf.at[into, c, pl.ds(h * half, half)], xsem.at[into, c])

    def y_copy(step, out_of, h, c):
        r = pl.ds(pl.multiple_of(step * tile + h * half, half), half)
        return pltpu.make_async_copy(ybuf.at[out_of, c, pl.ds(h * half, half)], y_hbm[c].at[r], ysem.at[out_of, c])

    def real_halves(copy, step, buf, act):
        for c in range(N_PIECES):
            act(copy(step, buf, 0, c))

        @pl.when(nv_ref[step] > half)
        def _():
            for c in range(N_PIECES):
                act(copy(step, buf, 1, c))

    start = lambda cp: cp.start(priority=SECOND_DMA_QUEUE)
    wait = lambda cp: cp.wait()

    @pl.when(i == 0)
    def _():
        real_halves(x_copy, 0, 0, start)
        for ahead in range(1, X_RING - 1):
            @pl.when(ahead < n_used)
            def _():
                real_halves(x_copy, ahead, ahead, start)

    @pl.when(i + X_RING - 1 < n_used)
    def _():
        real_halves(x_copy, i + X_RING - 1, lax.rem(i + X_RING - 1, X_RING), start)

    @pl.when(i < n_used)
    def _():
        real_halves(x_copy, i, xslot, wait)
        subs = [slice(s * MXU_DIM, (s + 1) * MXU_DIM) for s in range(tile // MXU_DIM)]
        xs = []
        for rows in subs:
            lo, hi = _unpack_rows([xbuf[xslot, c, rows, :] for c in range(N_PIECES)])
            xs.append(jnp.concatenate(lo + hi, axis=1).astype(BF16))
        gates = [(_dot(x, wg_ref[...]), _dot(x, wu_ref[...])) for x in xs]
        ys = [_dot((_silu(g) * u).astype(BF16), wd_ref[...]) for g, u in gates]
        for rows, y in zip(subs, ys):
            for c, piece in enumerate(_pack_rows(y)):
                ybuf[slot, c, rows, :] = piece

        @pl.when(i >= 1)
        def _():
            real_halves(y_copy, i - 1, 1 - slot, wait)

        real_halves(y_copy, i, slot, start)

        @pl.when(i == n_used - 1)
        def _():
            real_halves(y_copy, i, slot, wait)


def _experts(x_pieces, tile_expert, tile_rows, n_used, wg, wu, wd, *, group_tile):
    n_rows = x_pieces[0].shape[0]
    n_tiles = n_rows // group_tile

    wspec = lambda a: pl.BlockSpec((None,) + a.shape[1:],
                                   lambda i, te, nv, nu: (te[jnp.minimum(i, nu[0] - 1)], 0, 0))
    return pl.pallas_call(
        functools.partial(_experts_kernel, tile=group_tile),
        grid_spec=pltpu.PrefetchScalarGridSpec(
            num_scalar_prefetch=3,
            grid=(n_tiles,),
            in_specs=[pl.BlockSpec(memory_space=pl.ANY)] * N_PIECES + [wspec(wg), wspec(wu), wspec(wd)],
            out_specs=[pl.BlockSpec(memory_space=pl.ANY)] * N_PIECES,
            scratch_shapes=[pltpu.VMEM((X_RING, N_PIECES, group_tile, LANES), I32),
                            pltpu.SemaphoreType.DMA((X_RING, N_PIECES)),
                            pltpu.VMEM((2, N_PIECES, group_tile, LANES), I32),
                            pltpu.SemaphoreType.DMA((2, N_PIECES))]),
        out_shape=[jax.ShapeDtypeStruct((n_rows, LANES), I32)] * N_PIECES,
        compiler_params=_cparams("arbitrary"),
        name="experts",
    )(tile_expert, tile_rows, n_used, *x_pieces, wg, wu, wd)


def _moe_out_kernel(x1_ref, mod_ref, fg_ref, sg_ref, su_ref, sd_ref, *refs):
    h_refs = refs[:N_PIECES]
    routed_refs = refs[N_PIECES:2 * N_PIECES]
    out_ref = refs[2 * N_PIECES]
    h_lo, h_hi = _unpack_rows([r[...] for r in h_refs])
    h = jnp.concatenate(h_lo + h_hi, axis=1).astype(BF16)
    hid = _silu(_dot(h, sg_ref[...])) * _dot(h, su_ref[...])
    shared = _dot(hid.astype(BF16), sd_ref[...])
    r_lo, r_hi = _unpack_rows([r[...] for r in routed_refs])
    routed = jnp.concatenate(r_lo + r_hi, axis=1)
    x2 = x1_ref[...] + mod_ref[0, 5:6, :] * (shared + routed)
    out_ref[...] = x2 * lax.rsqrt(jnp.mean(x2 * x2, axis=-1, keepdims=True) + EPS) * fg_ref[...]


def _moe_out(h2_pieces, x1, mods, final_g, sg, su, sd, routed_pieces, *, tokens_per_mod, tm):
    t = x1.shape[0]
    tiles_per_mod = tokens_per_mod // tm
    row = lambda w: pl.BlockSpec((tm, w), lambda i: (i, 0))
    full = lambda a: pl.BlockSpec(a.shape, lambda i: (0,) * a.ndim)
    return pl.pallas_call(
        _moe_out_kernel,
        grid=(t // tm,),
        in_specs=[row(D_MODEL),
                  pl.BlockSpec((1, 6, D_MODEL), lambda i: (i // tiles_per_mod, 0, 0)),
                  full(final_g), full(sg), full(su), full(sd)]
        + [row(LANES)] * (2 * N_PIECES),
        out_specs=row(D_MODEL),
        out_shape=jax.ShapeDtypeStruct((t, D_MODEL), F32),
        compiler_params=_cparams("parallel"),
        name="moe_out",
    )(x1, mods, final_g, sg, su, sd, *h2_pieces, *routed_pieces)


def _trunk(x, mods, s0, w, expert_w, *, batch, seq_len, on_grid):
    t = batch * seq_len
    tokens_per_mod = t // mods.shape[0]
    cos_t, sin_t = _rope_tables(max(seq_len, PROJ_TILE))
    to_cast = expert_w if expert_w[0].dtype != BF16 else ()
    (q, k, v, gsw, up, ga, gb), casted = _inproj(x, mods, w["norm1_g"], w["w_in"], cos_t, sin_t, to_cast,
                                                 tokens_per_mod=tokens_per_mod, seq_len=seq_len,
                                                 on_grid=on_grid, tm=PROJ_TILE)
    expert_w = casted or expert_w
    z, s_f, s_b = _retention(q, k, v, gsw, w["dec"], s0, batch=batch, seq_len=seq_len)
    p = _pool(up, w["pool_w"], w["pool_scale"], batch=batch, seq_len=seq_len, on_grid=on_grid)
    x1, h2_pieces, (idx, rank, wts, counts) = _merge(
        x, z, p, ga, gb, mods, w["norm2_g"], w["w_br_ret"], w["w_br_pool"], w["w_out"],
        w["router_wt"], w["router_bias"], tokens_per_mod=tokens_per_mod, tm=PROJ_TILE)

    group_tile = _group_tile(t)
    n_rows = t * TOP_K + N_EXPERTS * group_tile
    pos, tile_expert, tile_rows, n_used = _plan(idx, rank, counts, n_tiles=n_rows // group_tile, tf=PLAN_TILE,
                                                group_tile=group_tile)
    x_sorted = _sc_dispatch(h2_pieces, pos, n_rows=n_rows)
    y_sorted = _experts(x_sorted, tile_expert.reshape(-1), tile_rows.reshape(-1), n_used.reshape(-1),
                        *expert_w, group_tile=group_tile)
    regroup = lambda a: a.reshape(TOP_K, t // SC_GROUP, SC_GROUP).transpose(1, 0, 2)
    pos_rows = pos.transpose(1, 0, 2).reshape(TOP_K, t)
    routed = _sc_combine(y_sorted, regroup(pos_rows), regroup(wts), n_tokens=t)
    y = _moe_out(h2_pieces, x1, mods, w["final_g"], w["sh_w_gate"], w["sh_w_up"], w["sh_w_down"], routed,
                 tokens_per_mod=tokens_per_mod, tm=OUT_TILE)
    return y, s_f, s_b, expert_w


def kernel(x_prompt, x_sample, state_ret_fwd, state_ret_bwd, c, c_ctx, ada_w, ada_b, norm1_g, norm2_g, w_in,
           ret_decay_fwd, ret_decay_bwd, w_br_ret, pool_w, pool_scale, w_br_pool, w_out, router_w, router_bias,
           exp_w_gate, exp_w_up, exp_w_down, sh_w_gate, sh_w_up, sh_w_down, final_norm_g):
    n_req, seq, d = x_prompt.shape
    n_dec, dec_seq, _ = x_sample.shape
    depth = ada_w.shape[0]
    assert depth == 1 and d == D_MODEL

    xc = x_prompt.reshape(n_req * seq, d)
    xs = x_sample.reshape(n_dec * dec_seq, d)
    new_f, new_b = [], []
    for l in range(depth):
        c_rows = jnp.concatenate([c_ctx[None, :], c, jnp.zeros((8 - 1 - n_dec, d), F32)], axis=0)
        mods = _ada(c_rows, ada_w[l], ada_b[l]).reshape(8, 6, d)
        pad_rows = LANES - N_EXPERTS
        w = dict(
            norm1_g=norm1_g[l].reshape(1, d), norm2_g=norm2_g[l].reshape(1, d),
            final_g=final_norm_g.reshape(1, d),
            w_in=w_in[l].astype(BF16),
            dec=jnp.stack([ret_decay_fwd[l], ret_decay_bwd[l]]).astype(F32),
            w_br_ret=w_br_ret[l].astype(BF16), pool_w=pool_w[l].astype(BF16),
            pool_scale=pool_scale[l].reshape(1, POOL_W), w_br_pool=w_br_pool[l].astype(BF16),
            w_out=w_out[l].astype(BF16),
            router_wt=jnp.pad(router_w[l].T, ((0, pad_rows), (0, 0))).astype(BF16),
            router_bias=jnp.pad(router_bias[l].astype(F32).reshape(N_EXPERTS, 1), ((0, pad_rows), (0, 0))),
            sh_w_gate=sh_w_gate[l].astype(BF16),
            sh_w_up=sh_w_up[l].astype(BF16), sh_w_down=sh_w_down[l].astype(BF16),
        )
        cached = (state_ret_fwd[:, l].astype(F32), state_ret_bwd[:, l].astype(F32))
        expert_w = (exp_w_gate[l], exp_w_up[l], exp_w_down[l])
        xs, _, _, expert_w = _trunk(xs, mods[1:1 + n_dec], cached, w, expert_w,
                                    batch=n_dec, seq_len=dec_seq, on_grid=True)
        xc, s_f, s_b, _ = _trunk(xc, mods[0:1], None, w, expert_w, batch=n_req, seq_len=seq, on_grid=False)
        new_f.append(s_f)
        new_b.append(s_b)
    y_prompt = xc.reshape(n_req, seq, d)
    y_sample = xs.reshape(n_dec, dec_seq, d)
    return (y_prompt, y_sample, jnp.stack(new_f, axis=1).astype(x_prompt.dtype),
            jnp.stack(new_b, axis=1).astype(x_prompt.dtype))
```

```python
import functools
import math

import numpy as np
import jax
import jax.numpy as jnp
from jax import lax
from jax.experimental import pallas as pl
from jax.experimental.pallas import tpu as pltpu
from jax.experimental.pallas import tpu_sc as plsc

D_MODEL = 1024
GRID_W = 64
RET_HEADS = 4
RET_DK = 128
RET_DV = 256
RET_QK_W = RET_HEADS * RET_DK
RET_V_W = RET_HEADS * RET_DV
RET_CHUNK = 128
ROPE_BASE = 10000.0
POOL_GROUPS = 4
POOL_CH = 128
POOL_W = POOL_GROUPS * POOL_CH
POOL_WINDOWS = (2, 4, 8, 16)
N_EXPERTS = 64
TOP_K = 8
N_EXPERT_GROUPS = 8
GROUP_SIZE = N_EXPERTS // N_EXPERT_GROUPS
TOPK_GROUPS = 4
D_EXPERT = 256
ROUTED_SCALE = 2.5
EPS = 1e-6
IN_SIZES = (RET_QK_W, RET_QK_W, RET_V_W, RET_V_W, POOL_W, D_MODEL, D_MODEL)
IN_OFFS = tuple(sum(IN_SIZES[:i]) for i in range(len(IN_SIZES) + 1))
IN_W = IN_OFFS[-1]

LANES = 128
VMEM_LIMIT = 56 << 20
N_PIECES = D_MODEL // 2 // LANES
MXU_DIM = 256
SC_CHUNK = 128
PROJ_TILE = 512
PLAN_TILE = 2048
OUT_TILE = 1024
POOL_SEQS_PER_STEP = 4
HIGH_HALF = 0xFFFF0000

F32 = jnp.float32
BF16 = jnp.bfloat16
I32 = jnp.int32
U32 = jnp.uint32


def _cparams(*sem):
    return pltpu.CompilerParams(dimension_semantics=sem, vmem_limit_bytes=VMEM_LIMIT)


def _dot(a, b):
    return jnp.dot(a, b, preferred_element_type=F32)


def _silu(x):
    return x * jax.nn.sigmoid(x)


def _rms_mod(x, g, scale, shift):
    y = x * lax.rsqrt(jnp.mean(x * x, axis=-1, keepdims=True) + EPS)
    return (y * g) * (1.0 + scale) + shift


def _ada_kernel(c_ref, w_ref, b_ref, o_ref):
    c = c_ref[...]
    o_ref[...] = jnp.dot(_silu(c), w_ref[...], preferred_element_type=F32,
                         precision=lax.Precision.HIGHEST) + b_ref[...]


def _ada(c_rows, ada_w, ada_b):
    r = c_rows.shape[0]
    n = ada_w.shape[1]
    tn = 2 * D_MODEL
    return pl.pallas_call(
        _ada_kernel,
        grid=(n // tn,),
        in_specs=[pl.BlockSpec((r, D_MODEL), lambda j: (0, 0)),
                  pl.BlockSpec((D_MODEL, tn), lambda j: (0, j)),
                  pl.BlockSpec((1, tn), lambda j: (0, j))],
        out_specs=pl.BlockSpec((r, tn), lambda j: (0, j)),
        out_shape=jax.ShapeDtypeStruct((r, n), F32),
        compiler_params=_cparams("parallel"),
        name="ada_mod",
    )(c_rows, ada_w, ada_b.reshape(1, n))


def _inproj_kernel(x_ref, mod_ref, g_ref, w_ref, cos_ref, sin_ref, *refs, on_grid):
    n_side = (len(refs) - len(IN_SIZES)) // 2
    side_in = refs[:n_side]
    q_ref, k_ref, v_ref, gsw_ref, up_ref, ga_ref, gb_ref = refs[n_side:n_side + len(IN_SIZES)]
    side_out = refs[n_side + len(IN_SIZES):]
    for src, dst in zip(side_in, side_out):
        dst[...] = src[...].astype(BF16)
    for s in range(x_ref.shape[0] // MXU_DIM):
        rows = slice(s * MXU_DIM, (s + 1) * MXU_DIM)
        h = _rms_mod(x_ref[rows, :], g_ref[...], mod_ref[0, 1:2, :], mod_ref[0, 0:1, :]).astype(BF16)

        def seg(i):
            return _dot(h, w_ref[:, IN_OFFS[i]:IN_OFFS[i + 1]])

        q = seg(0)
        k = seg(1)
        if on_grid:
            cos = jnp.concatenate([cos_ref[rows, :]] * RET_HEADS, axis=1)
            sin = jnp.concatenate([sin_ref[rows, :]] * RET_HEADS, axis=1)
            quarter = RET_DK // 4
            lane = lax.broadcasted_iota(jnp.int32, q.shape, 1)
            first = (lane & (2 * quarter - 1)) < quarter

            def rope(a):
                up = pltpu.roll(a, RET_QK_W - quarter, axis=1)
                dn = pltpu.roll(a, quarter, axis=1)
                return a * cos + jnp.where(first, up, dn) * sin

            q = rope(q)
            k = rope(k)
        q_ref[rows, :] = q.astype(BF16)
        k_ref[rows, :] = (k * (RET_DK ** -0.5)).astype(BF16)
        v_ref[rows, :] = seg(2).astype(BF16)
        gsw_ref[rows, :] = seg(3).astype(BF16)
        up_ref[rows, :] = seg(4).astype(BF16)
        ga_ref[rows, :] = seg(5).astype(BF16)
        gb_ref[rows, :] = seg(6).astype(BF16)


def _inproj(x, mods, norm_g, w_in, cos_t, sin_t, side_cast=(), *, tokens_per_mod, seq_len, on_grid, tm):
    t = x.shape[0]
    steps = t // tm
    tiles_per_mod = tokens_per_mod // tm
    tiles_per_seq = max(seq_len // tm, 1)
    widths = IN_SIZES
    out_shape = [jax.ShapeDtypeStruct((t, w), BF16) for w in widths]
    out_specs = [pl.BlockSpec((tm, w), lambda i: (i, 0)) for w in widths]
    side_specs = [pl.BlockSpec((a.shape[0] // steps,) + a.shape[1:], lambda i: (i, 0, 0)) for a in side_cast]
    outs = pl.pallas_call(
        functools.partial(_inproj_kernel, on_grid=on_grid),
        grid=(steps,),
        in_specs=[pl.BlockSpec((tm, D_MODEL), lambda i: (i, 0)),
                  pl.BlockSpec((1, 6, D_MODEL), lambda i: (i // tiles_per_mod, 0, 0)),
                  pl.BlockSpec((1, D_MODEL), lambda i: (0, 0)),
                  pl.BlockSpec((D_MODEL, IN_W), lambda i: (0, 0), pipeline_mode=pl.Buffered(1)),
                  pl.BlockSpec((tm, RET_DK), lambda i: (i % tiles_per_seq, 0)),
                  pl.BlockSpec((tm, RET_DK), lambda i: (i % tiles_per_seq, 0))] + side_specs,
        out_specs=out_specs + side_specs,
        out_shape=out_shape + [jax.ShapeDtypeStruct(a.shape, BF16) for a in side_cast],
        compiler_params=_cparams("parallel"),
        name="inproj_grid" if on_grid else "inproj_seq",
    )(x, mods, norm_g, w_in, cos_t, sin_t, *side_cast)
    return outs[:len(widths)], tuple(outs[len(widths):])


def _rope_tables(seq_len):
    t = np.arange(seq_len)
    row = (t // GRID_W).astype(np.float32)
    col = (t % GRID_W).astype(np.float32)
    m = RET_DK // 4
    inv = (np.float32(ROPE_BASE) ** (-np.arange(m, dtype=np.float32) / np.float32(m))).astype(np.float32)
    ar = row[:, None] * inv
    ac = col[:, None] * inv
    cos = np.concatenate([np.cos(ar), np.cos(ar), np.cos(ac), np.cos(ac)], axis=1)
    sin = np.concatenate([-np.sin(ar), np.sin(ar), -np.sin(ac), np.sin(ac)], axis=1)
    return jnp.asarray(cos, F32), jnp.asarray(sin, F32)


def _ret_heads_per_step(seq_len):
    per_head = seq_len * (2 * 2 * (2 * RET_DK + 3 * RET_DV) + 4 * RET_DV + 2 * RET_DK)
    heads = RET_HEADS
    while heads > 1 and heads * per_head > VMEM_LIMIT * 3 // 4:
        heads //= 2
    return heads


def _ret_kernel(dec_ref, q_ref, k_ref, v_ref, g_ref, *refs, n_chunks, heads, zero_init):
    s0_refs = () if zero_init else refs[:2]
    z_ref, sf_ref, sb_ref, oacc_ref, kt_ref = refs[len(s0_refs):]
    c = RET_CHUNK
    half = n_chunks // 2
    ii = lax.broadcasted_iota(I32, (c, c), 0)
    jj = lax.broadcasted_iota(I32, (c, c), 1)
    ik = lax.broadcasted_iota(I32, (c, RET_DK), 0).astype(F32)
    jk = lax.broadcasted_iota(I32, (RET_DK, c), 1).astype(F32)

    def log_gamma(d, shape):
        return jnp.log1p(-jnp.exp2(-jnp.full(shape, d, F32)))

    consts = {}
    for hh in range(heads):
        h = pl.program_id(1) * heads + hh
        dec_f = dec_ref[0, h]
        dec_b = dec_ref[1, h]
        rel = (ii - jj).astype(F32)
        consts[hh, "f"] = (
            jnp.where(rel >= 0, jnp.exp(log_gamma(dec_f, (c, c)) * jnp.maximum(rel, 0.0)), 0.0),
            jnp.exp(log_gamma(dec_f, (c, RET_DK)) * (ik + 1.0)),
            jnp.exp(log_gamma(dec_f, (RET_DK, c)) * (c - 1.0 - jk)),
            jnp.exp(log_gamma(dec_f, (RET_DK, RET_DV)) * c))
        consts[hh, "b"] = (
            jnp.where(rel <= 0, jnp.exp(log_gamma(dec_b, (c, c)) * jnp.maximum(-rel, 0.0)), 0.0),
            jnp.exp(log_gamma(dec_b, (c, RET_DK)) * (c - ik)),
            jnp.exp(log_gamma(dec_b, (RET_DK, c)) * jk),
            jnp.exp(log_gamma(dec_b, (RET_DK, RET_DV)) * c))

    for s_ref, s0_ref in zip((sf_ref, sb_ref), s0_refs or (None, None)):
        s_ref[...] = jnp.zeros(s_ref.shape, F32) if zero_init else s0_ref[...]

    def scores(ci, hh, direction, second):
        r = pl.ds(pl.multiple_of(ci * c, c), c)
        kcols = slice(hh * RET_DK, (hh + 1) * RET_DK)
        qc = q_ref[r, kcols]
        kc = k_ref[r, kcols]
        if not second:
            kt_ref[hh, ci] = kc.T
        sc = lax.dot_general(qc, kc, (((1,), (1,)), ((), ())), preferred_element_type=F32)
        return ci, hh, direction, r, qc, sc

    def advance(job):
        ci, hh, direction, r, qc, sc = job
        dmask, qdec, kdec, cdec = consts[hh, direction]
        s_ref = sf_ref if direction == "f" else sb_ref
        vc = v_ref[r, hh * RET_DV:(hh + 1) * RET_DV]
        s = s_ref[hh]
        lhs = jnp.concatenate([(sc * dmask).astype(BF16), (qc.astype(F32) * qdec).astype(BF16)], axis=1)
        o = _dot(lhs, jnp.concatenate([vc, s.astype(BF16)], axis=0))
        kd_t = (kt_ref[hh, ci].astype(F32) * kdec).astype(BF16)
        s_ref[hh] = s * cdec + _dot(kd_t, vc)
        return o

    def emit(job, o, second):
        _, hh, _, r, _, _ = job
        vcols = slice(hh * RET_DV, (hh + 1) * RET_DV)
        if not second:
            oacc_ref[hh, r, :] = o
        else:
            o = o + oacc_ref[hh, r, :]
            o = o * lax.rsqrt(jnp.mean(o * o, axis=-1, keepdims=True) + EPS)
            g = g_ref[r, vcols].astype(F32)
            z_ref[r, vcols] = (_silu(g) * o).astype(BF16)

    def body(second):
        def run(t, carry):
            jobs = [scores(ci, hh, d, second) for hh in range(heads)
                    for ci, d in ((t, "f"), (n_chunks - 1 - t, "b"))]
            outs = [advance(job) for job in jobs]
            for job, o in zip(jobs, outs):
                emit(job, o, second)
            return carry
        return run

    lax.fori_loop(0, half, body(False), 0, unroll=8 if half % 8 == 0 else 1)
    lax.fori_loop(half, n_chunks, body(True), 0, unroll=4 if half % 4 == 0 else 1)


def _retention(q, k, v, gsw, dec, s0, *, batch, seq_len):
    n_chunks = seq_len // RET_CHUNK
    assert n_chunks % 2 == 0
    heads = _ret_heads_per_step(seq_len)
    t = batch * seq_len
    st_spec = pl.BlockSpec((None, heads, RET_DK, RET_DV), lambda b, h: (b, h, 0, 0))
    st_shape = jax.ShapeDtypeStruct((batch, RET_HEADS, RET_DK, RET_DV), F32)
    kspec = pl.BlockSpec((seq_len, heads * RET_DK), lambda b, h: (b, h))
    vspec = pl.BlockSpec((seq_len, heads * RET_DV), lambda b, h: (b, h))
    return pl.pallas_call(
        functools.partial(_ret_kernel, n_chunks=n_chunks, heads=heads, zero_init=s0 is None),
        grid=(batch, RET_HEADS // heads),
        in_specs=[pl.BlockSpec(memory_space=pltpu.SMEM), kspec, kspec, vspec, vspec]
        + ([] if s0 is None else [st_spec, st_spec]),
        out_specs=[vspec, st_spec, st_spec],
        out_shape=[jax.ShapeDtypeStruct((t, RET_V_W), BF16), st_shape, st_shape],
        scratch_shapes=[pltpu.VMEM((heads, seq_len, RET_DV), F32),
                        pltpu.VMEM((heads, n_chunks, RET_DK, RET_CHUNK), BF16)],
        compiler_params=_cparams("parallel", "parallel"),
        name=f"retention_l{seq_len}",
    )(dec, q, k, v, gsw, *(s0 or ()))


def _pool_kernel(u_ref, w_ref, sc_ref, o_ref, *, n_tok, width, two_d):
    n_rows = n_tok // width
    pos = lax.broadcasted_iota(I32, (width, POOL_CH), 0)

    def every_row(a):
        return jnp.concatenate([a] * n_rows, axis=0) if n_rows > 1 else a

    def shift_in_row(a, s):
        ok = (pos < width - s) if s > 0 else (pos >= -s)
        return pltpu.roll(a, (-s) % n_tok, axis=0) * every_row(jnp.where(ok, 1.0, 0.0))

    def shift_rows(a, m):
        k = abs(m) * width
        zeros = jnp.zeros((k, POOL_CH), F32)
        return (jnp.concatenate([a[k:], zeros], axis=0) if m > 0
                else jnp.concatenate([zeros, a[:n_tok - k]], axis=0))

    def box_sum(a, half, shift):
        fw = a
        bw = shift(a, -1)
        m = 1
        while m < half:
            fw = fw + shift(fw, m)
            bw = bw + shift(bw, -m)
            m *= 2
        return fw + bw

    def inv_count(p, half, extent):
        return 1.0 / (jnp.minimum(p + half, extent) - jnp.maximum(p - half, 0)).astype(F32)

    for g, window in enumerate(POOL_WINDOWS):
        half = window // 2
        cols = slice(g * POOL_CH, (g + 1) * POOL_CH)
        ug = u_ref[:, cols].astype(F32)
        total = box_sum(ug, half, shift_in_row)
        inv = every_row(inv_count(pos, half, width))
        if two_d:
            total = box_sum(total, half, shift_rows)
            row = lax.broadcasted_iota(I32, (n_rows, 1, POOL_CH), 0)
            inv_r = jnp.broadcast_to(inv_count(row, half, n_rows), (n_rows, width, POOL_CH))
            inv = inv * inv_r.reshape(n_tok, POOL_CH)
        d = (total * inv - ug).astype(BF16)
        o_ref[:, cols] = (_dot(d, w_ref[g]) * sc_ref[:, cols]).astype(BF16)


def _pool(u, pool_w, pool_scale, *, batch, seq_len, on_grid):
    t = batch * seq_len
    width = GRID_W if on_grid else seq_len
    n_tok = seq_len if on_grid else seq_len * math.gcd(batch, POOL_SEQS_PER_STEP)
    return pl.pallas_call(
        functools.partial(_pool_kernel, n_tok=n_tok, width=width, two_d=on_grid),
        grid=(t // n_tok,),
        in_specs=[pl.BlockSpec((n_tok, POOL_W), lambda b: (b, 0)),
                  pl.BlockSpec((POOL_GROUPS, POOL_CH, POOL_CH), lambda b: (0, 0, 0)),
                  pl.BlockSpec((1, POOL_W), lambda b: (0, 0))],
        out_specs=pl.BlockSpec((n_tok, POOL_W), lambda b: (b, 0)),
        out_shape=jax.ShapeDtypeStruct((t, POOL_W), BF16),
        compiler_params=_cparams("parallel"),
        name=f"pool_l{seq_len}",
    )(u, pool_w, pool_scale)


def _pack_rows(x):
    half = D_MODEL // 2
    lo = lax.bitcast_convert_type(x[:, :half].astype(BF16).astype(F32), U32) >> 16
    hi = lax.bitcast_convert_type(x[:, half:].astype(BF16).astype(F32), U32) & jnp.uint32(HIGH_HALF)
    word = lax.bitcast_convert_type(hi | lo, I32)
    return [word[:, c * LANES:(c + 1) * LANES] for c in range(N_PIECES)]


def _unpack_rows(pieces):
    words = [lax.bitcast_convert_type(p, U32) for p in pieces]
    lo = [lax.bitcast_convert_type(w << 16, F32) for w in words]
    hi = [lax.bitcast_convert_type(w & jnp.uint32(HIGH_HALF), F32) for w in words]
    return lo, hi


def _merge_kernel(x_ref, z_ref, p_ref, ga_ref, gb_ref, mod_ref, g2_ref, wr_ref, wp_ref, wo_ref, rw_ref, bias_ref,
                  x1_ref, *refs):
    piece_refs, route_refs = refs[:N_PIECES], refs[N_PIECES:]
    subs = [slice(s * MXU_DIM, (s + 1) * MXU_DIM) for s in range(x_ref.shape[0] // MXU_DIM)]
    branches = [(_dot(z_ref[r, :], wr_ref[...]), _dot(p_ref[r, :], wp_ref[...])) for r in subs]
    merged = [(jax.nn.sigmoid(ga_ref[r, :].astype(F32)) * y_ret
               + jax.nn.sigmoid(gb_ref[r, :].astype(F32)) * y_pool).astype(BF16)
              for r, (y_ret, y_pool) in zip(subs, branches)]
    outs = [_dot(m, wo_ref[...]) for m in merged]
    logits = []
    for r, o in zip(subs, outs):
        x1 = x_ref[r, :] + mod_ref[0, 2:3, :] * o
        x1_ref[r, :] = x1
        h2 = _rms_mod(x1, g2_ref[...], mod_ref[0, 4:5, :], mod_ref[0, 3:4, :])
        logits.append(lax.dot_general(rw_ref[...], h2.astype(BF16), (((1,), (1,)), ((), ())),
                                      preferred_element_type=F32)[:N_EXPERTS])
        for ref, piece in zip(piece_refs, _pack_rows(h2)):
            ref[r, :] = piece
    _route_tile(jnp.concatenate(logits, axis=1), bias_ref, *route_refs)


def _merge(x, z, p, ga, gb, mods, norm2_g, w_br_ret, w_br_pool, w_out, router_wt, bias_col, *, tokens_per_mod, tm):
    t = x.shape[0]
    tiles_per_mod = tokens_per_mod // tm
    row = lambda w: pl.BlockSpec((tm, w), lambda i: (i, 0))
    full = lambda a: pl.BlockSpec(a.shape, lambda i: (0,) * len(a.shape))
    krow = pl.BlockSpec((TOP_K, tm), lambda i: (0, i))
    counts = jax.ShapeDtypeStruct((LANES, LANES), F32)
    outs = pl.pallas_call(
        _merge_kernel,
        grid=(t // tm,),
        in_specs=[row(D_MODEL), row(RET_V_W), row(POOL_W), row(D_MODEL), row(D_MODEL),
                  pl.BlockSpec((1, 6, D_MODEL), lambda i: (i // tiles_per_mod, 0, 0)),
                  full(norm2_g), full(w_br_ret), full(w_br_pool), full(w_out), full(router_wt), full(bias_col)],
        out_specs=[row(D_MODEL)] + [row(LANES)] * N_PIECES + [krow, krow, krow, full(counts)],
        out_shape=[jax.ShapeDtypeStruct((t, D_MODEL), F32)] + [jax.ShapeDtypeStruct((t, LANES), I32)] * N_PIECES
        + [jax.ShapeDtypeStruct((TOP_K, t), I32), jax.ShapeDtypeStruct((TOP_K, t), I32),
           jax.ShapeDtypeStruct((TOP_K, t), F32), counts],
        scratch_shapes=[pltpu.VMEM(counts.shape, F32)],
        compiler_params=_cparams("arbitrary"),
        name="merge",
    )(x, z, p, ga, gb, mods, norm2_g, w_br_ret, w_br_pool, w_out, router_wt, bias_col)
    return outs[0], outs[1:1 + N_PIECES], outs[1 + N_PIECES:]


def _route_tile(logits, bias_ref, idx_ref, rank_ref, wk_ref, cnt_ref, carry_ref):
    e = N_EXPERTS
    tm = logits.shape[1]
    neg = -jnp.inf

    @pl.when(pl.program_id(0) == 0)
    def _():
        carry_ref[...] = jnp.zeros(carry_ref.shape, F32)

    scores = jax.nn.sigmoid(logits)
    sel = scores + bias_ref[:e, 0:1]
    e_idx = lax.broadcasted_iota(I32, (e, tm), 0)

    grp = sel.reshape(N_EXPERT_GROUPS, GROUP_SIZE, tm)
    m_idx = lax.broadcasted_iota(I32, grp.shape, 1)
    m1 = jnp.max(grp, axis=1, keepdims=True)
    first = jnp.min(jnp.where(grp == m1, m_idx, GROUP_SIZE), axis=1, keepdims=True)
    m2 = jnp.max(jnp.where(m_idx == first, neg, grp), axis=1, keepdims=True)
    gscore = (m1 + m2).reshape(N_EXPERT_GROUPS, tm)

    g_idx = lax.broadcasted_iota(I32, gscore.shape, 0)
    grank = jnp.zeros(gscore.shape, I32)
    for g in range(N_EXPERT_GROUPS):
        other = gscore[g:g + 1, :]
        beats = jnp.where(other > gscore, 1, jnp.where(other == gscore, (g_idx > g).astype(I32), 0))
        grank = grank + beats
    gkeep = (grank < TOPK_GROUPS).astype(F32)
    ekeep = jnp.broadcast_to(gkeep.reshape(N_EXPERT_GROUPS, 1, tm), grp.shape).reshape(e, tm)
    masked = jnp.where(ekeep > 0, sel, neg)

    chosen = jnp.zeros((e, tm), F32)
    picks, hits = [], []
    for _ in range(TOP_K):
        m = jnp.max(masked, axis=0, keepdims=True)
        pick = jnp.min(jnp.where(masked == m, e_idx, e), axis=0, keepdims=True)
        hit = e_idx == pick
        chosen = jnp.where(hit, 1.0, chosen)
        masked = jnp.where(hit, neg, masked)
        picks.append(pick)
        hits.append(hit)

    w = scores * chosen
    comb = w / jnp.sum(w, axis=0, keepdims=True) * ROUTED_SCALE

    t_row = lax.broadcasted_iota(I32, (tm, tm), 0)
    t_col = lax.broadcasted_iota(I32, (tm, tm), 1)
    before = (t_row < t_col).astype(BF16)
    rankmat = _dot(chosen.astype(BF16), before) + carry_ref[:e, 0:1]
    carry_ref[:e, :] = carry_ref[:e, :] + jnp.sum(chosen, axis=1, keepdims=True)
    cnt_ref[...] = carry_ref[...]

    idx_ref[...] = jnp.concatenate(picks, axis=0)
    rank_ref[...] = jnp.concatenate(
        [jnp.sum(jnp.where(h, rankmat, 0.0), axis=0, keepdims=True) for h in hits], axis=0).astype(I32)
    wk_ref[...] = jnp.concatenate(
        [jnp.sum(jnp.where(h, comb, 0.0), axis=0, keepdims=True) for h in hits], axis=0)


def _plan_kernel(idx_ref, rank_ref, cnt_ref, pos_ref, te_ref, nv_ref, nu_ref, *, group_tile):
    tf = idx_ref.shape[1]
    nt = te_ref.shape[1]
    cnt = cnt_ref[...].astype(I32)
    padded = (((cnt + (group_tile - 1)) // group_tile) * group_tile).astype(F32)
    e_sub = lax.broadcasted_iota(I32, (LANES, LANES), 0)
    e_lane = lax.broadcasted_iota(I32, (LANES, LANES), 1)
    base = jnp.sum(jnp.where(e_lane < e_sub, padded.T, 0.0), axis=1, keepdims=True)
    end = base + padded[:, 0:1]

    idx = idx_ref[...]
    start = jnp.zeros(idx.shape, F32)
    for e in range(N_EXPERTS):
        start = jnp.where(idx == e, base[e:e + 1, 0:1], start)
    pos = start.astype(I32) + rank_ref[...]
    for j in range(tf // SC_CHUNK):
        pos_ref[j] = pos[:, j * SC_CHUNK:(j + 1) * SC_CHUNK]

    tile_start = (lax.broadcasted_iota(I32, (N_EXPERTS, nt), 1) * group_tile).astype(F32)
    done = jnp.sum(jnp.where(end[:N_EXPERTS] <= tile_start, 1.0, 0.0), axis=0, keepdims=True)
    te_ref[...] = jnp.minimum(done, N_EXPERTS - 1.0).astype(I32)
    in_group = (base[:N_EXPERTS] <= tile_start) & (tile_start < end[:N_EXPERTS])
    real = jnp.clip(base[:N_EXPERTS] + cnt[:N_EXPERTS, 0:1].astype(F32) - tile_start, 0.0, float(group_tile))
    nv_ref[...] = jnp.sum(jnp.where(in_group, real, 0.0), axis=0, keepdims=True).astype(I32)
    total = jnp.sum(padded[:, 0:1], axis=0, keepdims=True)
    nu_ref[...] = jnp.broadcast_to(total * (1.0 / group_tile), nu_ref.shape).astype(I32)


def _plan(idx, rank, counts, *, n_tiles, tf, group_tile):
    t = idx.shape[1]
    nt_pad = -(-n_tiles // LANES) * LANES
    krow = pl.BlockSpec((TOP_K, tf), lambda i: (0, i))
    return pl.pallas_call(
        functools.partial(_plan_kernel, group_tile=group_tile),
        grid=(t // tf,),
        in_specs=[krow, krow, pl.BlockSpec((LANES, LANES), lambda i: (0, 0))],
        out_specs=[pl.BlockSpec((tf // SC_CHUNK, TOP_K, SC_CHUNK), lambda i: (i, 0, 0)),
                   pl.BlockSpec((1, nt_pad), lambda i: (0, 0)),
                   pl.BlockSpec((1, nt_pad), lambda i: (0, 0)),
                   pl.BlockSpec((1, LANES), lambda i: (0, 0))],
        out_shape=[jax.ShapeDtypeStruct((t // SC_CHUNK, TOP_K, SC_CHUNK), I32),
                   jax.ShapeDtypeStruct((1, nt_pad), I32), jax.ShapeDtypeStruct((1, nt_pad), I32),
                   jax.ShapeDtypeStruct((1, LANES), I32)],
        compiler_params=_cparams("arbitrary"),
        name="moe_plan",
    )(idx, rank, counts)


def _sc_mesh_info():
    info = plsc.get_sparse_core_info()
    mesh = plsc.VectorSubcoreMesh(core_axis_name="c", subcore_axis_name="s")
    return mesh, info.num_cores, info.num_cores * info.num_subcores


def _sc_dispatch(pieces, pos, *, n_rows):
    t = pieces[0].shape[0]
    mesh, n_cores, n_workers = _sc_mesh_info()
    per_w = t // SC_CHUNK // n_workers

    @functools.partial(
        pl.kernel, mesh=mesh,
        out_type=[jax.ShapeDtypeStruct((n_rows, LANES), I32)] * N_PIECES,
        scratch_types=[pltpu.VMEM((TOP_K, SC_CHUNK), I32),
                       pltpu.VMEM((N_PIECES, SC_CHUNK, LANES), I32),
                       pltpu.SemaphoreType.DMA((N_PIECES,)),
                       pltpu.SemaphoreType.DMA],
        name="sc_dispatch",
    )
    def run(*refs):
        src = refs[:N_PIECES]
        pos_hbm = refs[N_PIECES]
        dst = refs[N_PIECES + 1:2 * N_PIECES + 1]
        idx_v, rows_v, load_sem, put_sem = refs[2 * N_PIECES + 1:]
        wid = lax.axis_index("s") * n_cores + lax.axis_index("c")

        @pl.loop(0, per_w)
        def _(j):
            ch = wid * per_w + j
            t0 = pl.multiple_of(ch * SC_CHUNK, SC_CHUNK)
            loads = [pltpu.make_async_copy(src[c].at[pl.ds(t0, SC_CHUNK)], rows_v.at[c], load_sem.at[c])
                     for c in range(N_PIECES)]
            for ld in loads:
                ld.start()
            pltpu.sync_copy(pos_hbm.at[ch], idx_v)
            puts = []
            for c in range(N_PIECES):
                loads[c].wait()
                for k in range(TOP_K):
                    puts.append(pltpu.make_async_copy(rows_v.at[c], dst[c].at[idx_v.at[k]], put_sem))
                    puts[-1].start()
            for cp in puts:
                cp.wait()

    return run(*pieces, pos)


SC_GROUP = 32


def _sc_combine(pieces, pos, wts, *, n_tokens):
    mesh, n_cores, n_workers = _sc_mesh_info()
    lanes = plsc.get_sparse_core_info().num_lanes
    per_w = n_tokens // SC_GROUP // n_workers

    @functools.partial(
        pl.kernel, mesh=mesh,
        out_type=[jax.ShapeDtypeStruct((n_tokens, LANES), I32)] * N_PIECES,
        scratch_types=[pltpu.VMEM((TOP_K, SC_GROUP), I32),
                       pltpu.VMEM((TOP_K, SC_GROUP), F32),
                       pltpu.VMEM((2, TOP_K, SC_GROUP, LANES), I32),
                       pltpu.VMEM((2, SC_GROUP, LANES), I32),
                       pltpu.SemaphoreType.DMA((2,)),
                       pltpu.SemaphoreType.DMA((2,))],
        compiler_params=pltpu.CompilerParams(needs_layout_passes=False),
        name="sc_combine",
    )
    def run(*refs):
        src = refs[:N_PIECES]
        pos_hbm, wts_hbm = refs[N_PIECES:N_PIECES + 2]
        dst = refs[N_PIECES + 2:2 * N_PIECES + 2]
        idx_v, w_v, buf, acc, get_sem, put_sem = refs[2 * N_PIECES + 2:]
        wid = lax.axis_index("s") * n_cores + lax.axis_index("c")

        @pl.loop(0, per_w)
        def _(j):
            grp = wid * per_w + j
            t0 = pl.multiple_of(grp * SC_GROUP, SC_GROUP)
            pltpu.sync_copy(pos_hbm.at[grp], idx_v)
            pltpu.sync_copy(wts_hbm.at[grp], w_v)

            def gets(c, slot):
                return [pltpu.make_async_copy(src[c].at[idx_v.at[k]], buf.at[slot, k], get_sem.at[slot])
                        for k in range(TOP_K)]

            def puts(c, slot):
                return [pltpu.make_async_copy(acc.at[slot], dst[c].at[pl.ds(t0, SC_GROUP)], put_sem.at[slot])]

            for cp in gets(0, 0):
                cp.start()
            for c in range(N_PIECES):
                slot = c % 2
                if c + 1 < N_PIECES:
                    for cp in gets(c + 1, 1 - slot):
                        cp.start()
                for cp in gets(c, slot):
                    cp.wait()
                if c >= 2:
                    for cp in puts(c - 2, slot):
                        cp.wait()

                @pl.loop(0, SC_GROUP)
                def _(r):
                    row = jnp.full((lanes,), r, I32)
                    w = []
                    for k in range(TOP_K):
                        w_k = plsc.load_gather(w_v, [jnp.full((lanes,), k, I32), row])
                        w.append(plsc.pack(w_k, w_k, format=plsc.PackFormat.INTERLEAVED))
                    for q in range(LANES // lanes):
                        cols = pl.ds(q * lanes, lanes)
                        total = None
                        for k in range(TOP_K):
                            term = plsc.bitcast(buf[slot, k, r, cols], BF16) * w[k]
                            total = term if total is None else total + term
                        acc[slot, r, cols] = plsc.bitcast(total, I32)

                for cp in puts(c, slot):
                    cp.start()
            for c in range(N_PIECES - 2, N_PIECES):
                for cp in puts(c, c % 2):
                    cp.wait()

    return run(*pieces, pos, wts)


def _group_tile(n_tokens):
    per_expert = n_tokens * TOP_K // N_EXPERTS
    return max(MXU_DIM, min(4 * MXU_DIM, per_expert // MXU_DIM * MXU_DIM))


SECOND_DMA_QUEUE = 1
X_RING = 3


def _experts_kernel(te_ref, nv_ref, nu_ref, *refs, tile):
    x_hbm = refs[:N_PIECES]
    wg_ref, wu_ref, wd_ref = refs[N_PIECES:N_PIECES + 3]
    y_hbm = refs[N_PIECES + 3:2 * N_PIECES + 3]
    xbuf, xsem, ybuf, ysem = refs[2 * N_PIECES + 3:]
    i = pl.program_id(0)
    n_used = nu_ref[0]
    slot = lax.rem(i, X_RING)

    half = tile // 2

    def x_copy(step, into, h, c):
        r = pl.ds(pl.multiple_of(step * tile + h * half, half), half)
        return pltpu.make_async_copy(x_hbm[c].at[r], xbuf.at[into, c, pl.ds(h * half, half)], xsem.at[into, c])

    def y_copy(step, out_of, h, c):
        r = pl.ds(pl.multiple_of(step * tile + h * half, half), half)
        return pltpu.make_async_copy(ybuf.at[out_of, c, pl.ds(h * half, half)], y_hbm[c].at[r], ysem.at[out_of, c])

    def real_halves(copy, step, buf, act):
        for c in range(N_PIECES):
            act(copy(step, buf, 0, c))

        @pl.when(nv_ref[step] > half)
        def _():
            for c in range(N_PIECES):
                act(copy(step, buf, 1, c))

    start = lambda cp: cp.start(priority=SECOND_DMA_QUEUE)
    wait = lambda cp: cp.wait()

    @pl.when(i == 0)
    def _():
        real_halves(x_copy, 0, 0, start)
        for ahead in range(1, X_RING - 1):
            @pl.when(ahead < n_used)
            def _():
                real_halves(x_copy, ahead, ahead, start)

    @pl.when(i + X_RING - 1 < n_used)
    def _():
        real_halves(x_copy, i + X_RING - 1, lax.rem(i + X_RING - 1, X_RING), start)

    @pl.when(i < n_used)
    def _():
        real_halves(x_copy, i, slot, wait)
        subs = [slice(s * MXU_DIM, (s + 1) * MXU_DIM) for s in range(tile // MXU_DIM)]
        xs = []
        for rows in subs:
            lo, hi = _unpack_rows([xbuf[slot, c, rows, :] for c in range(N_PIECES)])
            xs.append(jnp.concatenate(lo + hi, axis=1).astype(BF16))
        gates = [(_dot(x, wg_ref[...]), _dot(x, wu_ref[...])) for x in xs]
        ys = [_dot((_silu(g) * u).astype(BF16), wd_ref[...]) for g, u in gates]
        for rows, y in zip(subs, ys):
            for c, piece in enumerate(_pack_rows(y)):
                ybuf[slot, c, rows, :] = piece

        behind = X_RING - 1

        @pl.when(i >= behind)
        def _():
            real_halves(y_copy, i - behind, lax.rem(i - behind, X_RING), wait)

        real_halves(y_copy, i, slot, start)

        @pl.when(i == n_used - 1)
        def _():
            for back in range(behind - 1, -1, -1):
                @pl.when(i >= back)
                def _():
                    real_halves(y_copy, i - back, lax.rem(i - back, X_RING), wait)


def _experts(x_pieces, tile_expert, tile_rows, n_used, wg, wu, wd, *, group_tile):
    n_rows = x_pieces[0].shape[0]
    n_tiles = n_rows // group_tile

    wspec = lambda a: pl.BlockSpec((None,) + a.shape[1:],
                                   lambda i, te, nv, nu: (te[jnp.minimum(i, nu[0] - 1)], 0, 0))
    return pl.pallas_call(
        functools.partial(_experts_kernel, tile=group_tile),
        grid_spec=pltpu.PrefetchScalarGridSpec(
            num_scalar_prefetch=3,
            grid=(n_tiles,),
            in_specs=[pl.BlockSpec(memory_space=pl.ANY)] * N_PIECES + [wspec(wg), wspec(wu), wspec(wd)],
            out_specs=[pl.BlockSpec(memory_space=pl.ANY)] * N_PIECES,
            scratch_shapes=[pltpu.VMEM((X_RING, N_PIECES, group_tile, LANES), I32),
                            pltpu.SemaphoreType.DMA((X_RING, N_PIECES))] * 2),
        out_shape=[jax.ShapeDtypeStruct((n_rows, LANES), I32)] * N_PIECES,
        compiler_params=_cparams("arbitrary"),
        name="experts",
    )(tile_expert, tile_rows, n_used, *x_pieces, wg, wu, wd)


def _moe_out_kernel(x1_ref, mod_ref, fg_ref, sg_ref, su_ref, sd_ref, *refs):
    h_refs = refs[:N_PIECES]
    routed_refs = refs[N_PIECES:2 * N_PIECES]
    out_ref = refs[2 * N_PIECES]
    h_lo, h_hi = _unpack_rows([r[...] for r in h_refs])
    h = jnp.concatenate(h_lo + h_hi, axis=1).astype(BF16)
    hid = _silu(_dot(h, sg_ref[...])) * _dot(h, su_ref[...])
    shared = _dot(hid.astype(BF16), sd_ref[...])
    r_lo, r_hi = _unpack_rows([r[...] for r in routed_refs])
    routed = jnp.concatenate(r_lo + r_hi, axis=1)
    x2 = x1_ref[...] + mod_ref[0, 5:6, :] * (shared + routed)
    out_ref[...] = x2 * lax.rsqrt(jnp.mean(x2 * x2, axis=-1, keepdims=True) + EPS) * fg_ref[...]


def _moe_out(h2_pieces, x1, mods, final_g, sg, su, sd, routed_pieces, *, tokens_per_mod, tm):
    t = x1.shape[0]
    tiles_per_mod = tokens_per_mod // tm
    row = lambda w: pl.BlockSpec((tm, w), lambda i: (i, 0))
    full = lambda a: pl.BlockSpec(a.shape, lambda i: (0,) * a.ndim)
    return pl.pallas_call(
        _moe_out_kernel,
        grid=(t // tm,),
        in_specs=[row(D_MODEL),
                  pl.BlockSpec((1, 6, D_MODEL), lambda i: (i // tiles_per_mod, 0, 0)),
                  full(final_g), full(sg), full(su), full(sd)]
        + [row(LANES)] * (2 * N_PIECES),
        out_specs=row(D_MODEL),
        out_shape=jax.ShapeDtypeStruct((t, D_MODEL), F32),
        compiler_params=_cparams("parallel"),
        name="moe_out",
    )(x1, mods, final_g, sg, su, sd, *h2_pieces, *routed_pieces)


def _trunk(x, mods, s0, w, expert_w, *, batch, seq_len, on_grid):
    t = batch * seq_len
    tokens_per_mod = t // mods.shape[0]
    cos_t, sin_t = _rope_tables(max(seq_len, PROJ_TILE))
    to_cast = expert_w if expert_w[0].dtype != BF16 else ()
    (q, k, v, gsw, up, ga, gb), casted = _inproj(x, mods, w["norm1_g"], w["w_in"], cos_t, sin_t, to_cast,
                                                 tokens_per_mod=tokens_per_mod, seq_len=seq_len,
                                                 on_grid=on_grid, tm=PROJ_TILE)
    expert_w = casted or expert_w
    z, s_f, s_b = _retention(q, k, v, gsw, w["dec"], s0, batch=batch, seq_len=seq_len)
    p = _pool(up, w["pool_w"], w["pool_scale"], batch=batch, seq_len=seq_len, on_grid=on_grid)
    x1, h2_pieces, (idx, rank, wts, counts) = _merge(
        x, z, p, ga, gb, mods, w["norm2_g"], w["w_br_ret"], w["w_br_pool"], w["w_out"],
        w["router_wt"], w["router_bias"], tokens_per_mod=tokens_per_mod, tm=PROJ_TILE)

    group_tile = _group_tile(t)
    n_rows = t * TOP_K + N_EXPERTS * group_tile
    pos, tile_expert, tile_rows, n_used = _plan(idx, rank, counts, n_tiles=n_rows // group_tile, tf=PLAN_TILE,
                                                group_tile=group_tile)
    x_sorted = _sc_dispatch(h2_pieces, pos, n_rows=n_rows)
    y_sorted = _experts(x_sorted, tile_expert.reshape(-1), tile_rows.reshape(-1), n_used.reshape(-1),
                        *expert_w, group_tile=group_tile)
    regroup = lambda a: a.reshape(TOP_K, t // SC_GROUP, SC_GROUP).transpose(1, 0, 2)
    pos_rows = pos.transpose(1, 0, 2).reshape(TOP_K, t)
    routed = _sc_combine(y_sorted, regroup(pos_rows), regroup(wts), n_tokens=t)
    y = _moe_out(h2_pieces, x1, mods, w["final_g"], w["sh_w_gate"], w["sh_w_up"], w["sh_w_down"], routed,
                 tokens_per_mod=tokens_per_mod, tm=OUT_TILE)
    return y, s_f, s_b, expert_w


def kernel(x_prompt, x_sample, state_ret_fwd, state_ret_bwd, c, c_ctx, ada_w, ada_b, norm1_g, norm2_g, w_in,
           ret_decay_fwd, ret_decay_bwd, w_br_ret, pool_w, pool_scale, w_br_pool, w_out, router_w, router_bias,
           exp_w_gate, exp_w_up, exp_w_down, sh_w_gate, sh_w_up, sh_w_down, final_norm_g):
    n_req, seq, d = x_prompt.shape
    n_dec, dec_seq, _ = x_sample.shape
    depth = ada_w.shape[0]
    assert depth == 1 and d == D_MODEL

    xc = x_prompt.reshape(n_req * seq, d)
    xs = x_sample.reshape(n_dec * dec_seq, d)
    new_f, new_b = [], []
    for l in range(depth):
        c_rows = jnp.concatenate([c_ctx[None, :], c, jnp.zeros((8 - 1 - n_dec, d), F32)], axis=0)
        mods = _ada(c_rows, ada_w[l], ada_b[l]).reshape(8, 6, d)
        pad_rows = LANES - N_EXPERTS
        w = dict(
            norm1_g=norm1_g[l].reshape(1, d), norm2_g=norm2_g[l].reshape(1, d),
            final_g=final_norm_g.reshape(1, d),
            w_in=w_in[l].astype(BF16),
            dec=jnp.stack([ret_decay_fwd[l], ret_decay_bwd[l]]).astype(F32),
            w_br_ret=w_br_ret[l].astype(BF16), pool_w=pool_w[l].astype(BF16),
            pool_scale=pool_scale[l].reshape(1, POOL_W), w_br_pool=w_br_pool[l].astype(BF16),
            w_out=w_out[l].astype(BF16),
            router_wt=jnp.pad(router_w[l].T, ((0, pad_rows), (0, 0))).astype(BF16),
            router_bias=jnp.pad(router_bias[l].astype(F32).reshape(N_EXPERTS, 1), ((0, pad_rows), (0, 0))),
            sh_w_gate=sh_w_gate[l].astype(BF16),
            sh_w_up=sh_w_up[l].astype(BF16), sh_w_down=sh_w_down[l].astype(BF16),
        )
        cached = (state_ret_fwd[:, l].astype(F32), state_ret_bwd[:, l].astype(F32))
        expert_w = (exp_w_gate[l], exp_w_up[l], exp_w_down[l])
        xs, _, _, expert_w = _trunk(xs, mods[1:1 + n_dec], cached, w, expert_w,
                                    batch=n_dec, seq_len=dec_seq, on_grid=True)
        xc, s_f, s_b, _ = _trunk(xc, mods[0:1], None, w, expert_w, batch=n_req, seq_len=seq, on_grid=False)
        new_f.append(s_f)
        new_b.append(s_b)
    y_prompt = xc.reshape(n_req, seq, d)
    y_sample = xs.reshape(n_dec, dec_seq, d)
    return (y_prompt, y_sample, jnp.stack(new_f, axis=1).astype(x_prompt.dtype),
            jnp.stack(new_b, axis=1).astype(x_prompt.dtype))
```

```python
import functools
import math

import numpy as np
import jax
import jax.numpy as jnp
from jax import lax
from jax.experimental import pallas as pl
from jax.experimental.pallas import tpu as pltpu
from jax.experimental.pallas import tpu_sc as plsc

D_MODEL = 1024
GRID_W = 64
RET_HEADS = 4
RET_DK = 128
RET_DV = 256
RET_QK_W = RET_HEADS * RET_DK
RET_V_W = RET_HEADS * RET_DV
RET_CHUNK = 128
ROPE_BASE = 10000.0
POOL_GROUPS = 4
POOL_CH = 128
POOL_W = POOL_GROUPS * POOL_CH
POOL_WINDOWS = (2, 4, 8, 16)
N_EXPERTS = 64
TOP_K = 8
N_EXPERT_GROUPS = 8
GROUP_SIZE = N_EXPERTS // N_EXPERT_GROUPS
TOPK_GROUPS = 4
D_EXPERT = 256
ROUTED_SCALE = 2.5
EPS = 1e-6
IN_SIZES = (RET_QK_W, RET_QK_W, RET_V_W, RET_V_W, POOL_W, D_MODEL, D_MODEL)
IN_OFFS = tuple(sum(IN_SIZES[:i]) for i in range(len(IN_SIZES) + 1))
IN_W = IN_OFFS[-1]

LANES = 128
VMEM_LIMIT = 56 << 20
N_PIECES = D_MODEL // 2 // LANES
MXU_DIM = 256
SC_CHUNK = 128
PROJ_TILE = 512
PLAN_TILE = 2048
OUT_TILE = 1024
POOL_SEQS_PER_STEP = 4
HIGH_HALF = 0xFFFF0000

F32 = jnp.float32
BF16 = jnp.bfloat16
I32 = jnp.int32
U32 = jnp.uint32


def _cparams(*sem):
    return pltpu.CompilerParams(dimension_semantics=sem, vmem_limit_bytes=VMEM_LIMIT)


def _dot(a, b):
    return jnp.dot(a, b, preferred_element_type=F32)


def _silu(x):
    return x * jax.nn.sigmoid(x)


def _rms_mod(x, g, scale, shift):
    y = x * lax.rsqrt(jnp.mean(x * x, axis=-1, keepdims=True) + EPS)
    return (y * g) * (1.0 + scale) + shift


def _ada_kernel(c_ref, w_ref, b_ref, o_ref):
    c = c_ref[...]
    o_ref[...] = jnp.dot(_silu(c), w_ref[...], preferred_element_type=F32,
                         precision=lax.Precision.HIGHEST) + b_ref[...]


def _ada(c_rows, ada_w, ada_b):
    r = c_rows.shape[0]
    n = ada_w.shape[1]
    tn = 2 * D_MODEL
    return pl.pallas_call(
        _ada_kernel,
        grid=(n // tn,),
        in_specs=[pl.BlockSpec((r, D_MODEL), lambda j: (0, 0)),
                  pl.BlockSpec((D_MODEL, tn), lambda j: (0, j)),
                  pl.BlockSpec((1, tn), lambda j: (0, j))],
        out_specs=pl.BlockSpec((r, tn), lambda j: (0, j)),
        out_shape=jax.ShapeDtypeStruct((r, n), F32),
        compiler_params=_cparams("parallel"),
        name="ada_mod",
    )(c_rows, ada_w, ada_b.reshape(1, n))


def _inproj_kernel(x_ref, mod_ref, g_ref, w_ref, cos_ref, sin_ref, *refs, on_grid):
    n_side = (len(refs) - len(IN_SIZES)) // 2
    side_in = refs[:n_side]
    q_ref, k_ref, v_ref, gsw_ref, up_ref, ga_ref, gb_ref = refs[n_side:n_side + len(IN_SIZES)]
    side_out = refs[n_side + len(IN_SIZES):]
    for src, dst in zip(side_in, side_out):
        dst[...] = src[...].astype(BF16)
    for s in range(x_ref.shape[0] // MXU_DIM):
        rows = slice(s * MXU_DIM, (s + 1) * MXU_DIM)
        h = _rms_mod(x_ref[rows, :], g_ref[...], mod_ref[0, 1:2, :], mod_ref[0, 0:1, :]).astype(BF16)

        def seg(i):
            return _dot(h, w_ref[:, IN_OFFS[i]:IN_OFFS[i + 1]])

        q = seg(0)
        k = seg(1)
        if on_grid:
            cos = jnp.concatenate([cos_ref[rows, :]] * RET_HEADS, axis=1)
            sin = jnp.concatenate([sin_ref[rows, :]] * RET_HEADS, axis=1)
            quarter = RET_DK // 4
            lane = lax.broadcasted_iota(jnp.int32, q.shape, 1)
            first = (lane & (2 * quarter - 1)) < quarter

            def rope(a):
                up = pltpu.roll(a, RET_QK_W - quarter, axis=1)
                dn = pltpu.roll(a, quarter, axis=1)
                return a * cos + jnp.where(first, up, dn) * sin

            q = rope(q)
            k = rope(k)
        q_ref[rows, :] = q.astype(BF16)
        k_ref[rows, :] = (k * (RET_DK ** -0.5)).astype(BF16)
        v_ref[rows, :] = seg(2).astype(BF16)
        gsw_ref[rows, :] = seg(3).astype(BF16)
        up_ref[rows, :] = seg(4).astype(BF16)
        ga_ref[rows, :] = seg(5).astype(BF16)
        gb_ref[rows, :] = seg(6).astype(BF16)


def _inproj(x, mods, norm_g, w_in, cos_t, sin_t, side_cast=(), *, tokens_per_mod, seq_len, on_grid, tm):
    t = x.shape[0]
    steps = t // tm
    tiles_per_mod = tokens_per_mod // tm
    tiles_per_seq = max(seq_len // tm, 1)
    widths = IN_SIZES
    out_shape = [jax.ShapeDtypeStruct((t, w), BF16) for w in widths]
    out_specs = [pl.BlockSpec((tm, w), lambda i: (i, 0)) for w in widths]
    side_specs = [pl.BlockSpec((a.shape[0] // steps,) + a.shape[1:], lambda i: (i, 0, 0)) for a in side_cast]
    outs = pl.pallas_call(
        functools.partial(_inproj_kernel, on_grid=on_grid),
        grid=(steps,),
        in_specs=[pl.BlockSpec((tm, D_MODEL), lambda i: (i, 0)),
                  pl.BlockSpec((1, 6, D_MODEL), lambda i: (i // tiles_per_mod, 0, 0)),
                  pl.BlockSpec((1, D_MODEL), lambda i: (0, 0)),
                  pl.BlockSpec((D_MODEL, IN_W), lambda i: (0, 0), pipeline_mode=pl.Buffered(1)),
                  pl.BlockSpec((tm, RET_DK), lambda i: (i % tiles_per_seq, 0)),
                  pl.BlockSpec((tm, RET_DK), lambda i: (i % tiles_per_seq, 0))] + side_specs,
        out_specs=out_specs + side_specs,
        out_shape=out_shape + [jax.ShapeDtypeStruct(a.shape, BF16) for a in side_cast],
        compiler_params=_cparams("parallel"),
        name="inproj_grid" if on_grid else "inproj_seq",
    )(x, mods, norm_g, w_in, cos_t, sin_t, *side_cast)
    return outs[:len(widths)], tuple(outs[len(widths):])


def _rope_tables(seq_len):
    t = np.arange(seq_len)
    row = (t // GRID_W).astype(np.float32)
    col = (t % GRID_W).astype(np.float32)
    m = RET_DK // 4
    inv = (np.float32(ROPE_BASE) ** (-np.arange(m, dtype=np.float32) / np.float32(m))).astype(np.float32)
    ar = row[:, None] * inv
    ac = col[:, None] * inv
    cos = np.concatenate([np.cos(ar), np.cos(ar), np.cos(ac), np.cos(ac)], axis=1)
    sin = np.concatenate([-np.sin(ar), np.sin(ar), -np.sin(ac), np.sin(ac)], axis=1)
    return jnp.asarray(cos, F32), jnp.asarray(sin, F32)


def _ret_heads_per_step(seq_len):
    per_head = seq_len * (2 * 2 * (2 * RET_DK + 3 * RET_DV) + 4 * RET_DV + 2 * RET_DK)
    heads = RET_HEADS
    while heads > 1 and heads * per_head > VMEM_LIMIT * 3 // 4:
        heads //= 2
    return heads


def _ret_kernel(dec_ref, q_ref, k_ref, v_ref, g_ref, *refs, n_chunks, heads, zero_init):
    s0_refs = () if zero_init else refs[:2]
    z_ref, sf_ref, sb_ref, oacc_ref, kt_ref = refs[len(s0_refs):]
    c = RET_CHUNK
    half = n_chunks // 2
    ii = lax.broadcasted_iota(I32, (c, c), 0)
    jj = lax.broadcasted_iota(I32, (c, c), 1)
    ik = lax.broadcasted_iota(I32, (c, RET_DK), 0).astype(F32)
    jk = lax.broadcasted_iota(I32, (RET_DK, c), 1).astype(F32)

    def log_gamma(d, shape):
        return jnp.log1p(-jnp.exp2(-jnp.full(shape, d, F32)))

    consts = {}
    for hh in range(heads):
        h = pl.program_id(1) * heads + hh
        dec_f = dec_ref[0, h]
        dec_b = dec_ref[1, h]
        rel = (ii - jj).astype(F32)
        consts[hh, "f"] = (
            jnp.where(rel >= 0, jnp.exp(log_gamma(dec_f, (c, c)) * jnp.maximum(rel, 0.0)), 0.0),
            jnp.exp(log_gamma(dec_f, (c, RET_DK)) * (ik + 1.0)),
            jnp.exp(log_gamma(dec_f, (RET_DK, c)) * (c - 1.0 - jk)),
            jnp.exp(log_gamma(dec_f, (RET_DK, RET_DV)) * c))
        consts[hh, "b"] = (
            jnp.where(rel <= 0, jnp.exp(log_gamma(dec_b, (c, c)) * jnp.maximum(-rel, 0.0)), 0.0),
            jnp.exp(log_gamma(dec_b, (c, RET_DK)) * (c - ik)),
            jnp.exp(log_gamma(dec_b, (RET_DK, c)) * jk),
            jnp.exp(log_gamma(dec_b, (RET_DK, RET_DV)) * c))

    for s_ref, s0_ref in zip((sf_ref, sb_ref), s0_refs or (None, None)):
        s_ref[...] = jnp.zeros(s_ref.shape, F32) if zero_init else s0_ref[...]

    def scores(ci, hh, direction, second):
        r = pl.ds(pl.multiple_of(ci * c, c), c)
        kcols = slice(hh * RET_DK, (hh + 1) * RET_DK)
        qc = q_ref[r, kcols]
        kc = k_ref[r, kcols]
        if not second:
            kt_ref[hh, ci] = kc.T
        sc = lax.dot_general(qc, kc, (((1,), (1,)), ((), ())), preferred_element_type=F32)
        return ci, hh, direction, r, qc, sc

    def advance(job):
        ci, hh, direction, r, qc, sc = job
        dmask, qdec, kdec, cdec = consts[hh, direction]
        s_ref = sf_ref if direction == "f" else sb_ref
        vc = v_ref[r, hh * RET_DV:(hh + 1) * RET_DV]
        s = s_ref[hh]
        lhs = jnp.concatenate([(sc * dmask).astype(BF16), (qc.astype(F32) * qdec).astype(BF16)], axis=1)
        o = _dot(lhs, jnp.concatenate([vc, s.astype(BF16)], axis=0))
        kd_t = (kt_ref[hh, ci].astype(F32) * kdec).astype(BF16)
        s_ref[hh] = s * cdec + _dot(kd_t, vc)
        return o

    def emit(job, o, second):
        _, hh, _, r, _, _ = job
        vcols = slice(hh * RET_DV, (hh + 1) * RET_DV)
        if not second:
            oacc_ref[hh, r, :] = o
        else:
            o = o + oacc_ref[hh, r, :]
            o = o * lax.rsqrt(jnp.mean(o * o, axis=-1, keepdims=True) + EPS)
            g = g_ref[r, vcols].astype(F32)
            z_ref[r, vcols] = (_silu(g) * o).astype(BF16)

    def body(second):
        def run(t, carry):
            jobs = [scores(ci, hh, d, second) for hh in range(heads)
                    for ci, d in ((t, "f"), (n_chunks - 1 - t, "b"))]
            outs = [advance(job) for job in jobs]
            for job, o in zip(jobs, outs):
                emit(job, o, second)
            return carry
        return run

    lax.fori_loop(0, half, body(False), 0, unroll=8 if half % 8 == 0 else 1)
    lax.fori_loop(half, n_chunks, body(True), 0, unroll=4 if half % 4 == 0 else 1)


def _retention(q, k, v, gsw, dec, s0, *, batch, seq_len):
    n_chunks = seq_len // RET_CHUNK
    assert n_chunks % 2 == 0
    heads = _ret_heads_per_step(seq_len)
    t = batch * seq_len
    st_spec = pl.BlockSpec((None, heads, RET_DK, RET_DV), lambda b, h: (b, h, 0, 0))
    st_shape = jax.ShapeDtypeStruct((batch, RET_HEADS, RET_DK, RET_DV), F32)
    kspec = pl.BlockSpec((seq_len, heads * RET_DK), lambda b, h: (b, h))
    vspec = pl.BlockSpec((seq_len, heads * RET_DV), lambda b, h: (b, h))
    return pl.pallas_call(
        functools.partial(_ret_kernel, n_chunks=n_chunks, heads=heads, zero_init=s0 is None),
        grid=(batch, RET_HEADS // heads),
        in_specs=[pl.BlockSpec(memory_space=pltpu.SMEM), kspec, kspec, vspec, vspec]
        + ([] if s0 is None else [st_spec, st_spec]),
        out_specs=[vspec, st_spec, st_spec],
        out_shape=[jax.ShapeDtypeStruct((t, RET_V_W), BF16), st_shape, st_shape],
        scratch_shapes=[pltpu.VMEM((heads, seq_len, RET_DV), F32),
                        pltpu.VMEM((heads, n_chunks, RET_DK, RET_CHUNK), BF16)],
        compiler_params=_cparams("parallel", "parallel"),
        name=f"retention_l{seq_len}",
    )(dec, q, k, v, gsw, *(s0 or ()))


def _pool_kernel(u_ref, w_ref, sc_ref, o_ref, *, n_tok, width, two_d):
    n_rows = n_tok // width
    pos = lax.broadcasted_iota(I32, (width, POOL_CH), 0)

    def every_row(a):
        return jnp.concatenate([a] * n_rows, axis=0) if n_rows > 1 else a

    def shift_in_row(a, s):
        ok = (pos < width - s) if s > 0 else (pos >= -s)
        return pltpu.roll(a, (-s) % n_tok, axis=0) * every_row(jnp.where(ok, 1.0, 0.0))

    def shift_rows(a, m):
        k = abs(m) * width
        zeros = jnp.zeros((k, POOL_CH), F32)
        return (jnp.concatenate([a[k:], zeros], axis=0) if m > 0
                else jnp.concatenate([zeros, a[:n_tok - k]], axis=0))

    def box_sum(a, half, shift):
        fw = a
        bw = shift(a, -1)
        m = 1
        while m < half:
            fw = fw + shift(fw, m)
            bw = bw + shift(bw, -m)
            m *= 2
        return fw + bw

    def inv_count(p, half, extent):
        return 1.0 / (jnp.minimum(p + half, extent) - jnp.maximum(p - half, 0)).astype(F32)

    for g, window in enumerate(POOL_WINDOWS):
        half = window // 2
        cols = slice(g * POOL_CH, (g + 1) * POOL_CH)
        ug = u_ref[:, cols].astype(F32)
        total = box_sum(ug, half, shift_in_row)
        inv = every_row(inv_count(pos, half, width))
        if two_d:
            total = box_sum(total, half, shift_rows)
            row = lax.broadcasted_iota(I32, (n_rows, 1, POOL_CH), 0)
            inv_r = jnp.broadcast_to(inv_count(row, half, n_rows), (n_rows, width, POOL_CH))
            inv = inv * inv_r.reshape(n_tok, POOL_CH)
        d = (total * inv - ug).astype(BF16)
        o_ref[:, cols] = (_dot(d, w_ref[g]) * sc_ref[:, cols]).astype(BF16)


def _pool(u, pool_w, pool_scale, *, batch, seq_len, on_grid):
    t = batch * seq_len
    width = GRID_W if on_grid else seq_len
    n_tok = seq_len if on_grid else seq_len * math.gcd(batch, POOL_SEQS_PER_STEP)
    return pl.pallas_call(
        functools.partial(_pool_kernel, n_tok=n_tok, width=width, two_d=on_grid),
        grid=(t // n_tok,),
        in_specs=[pl.BlockSpec((n_tok, POOL_W), lambda b: (b, 0)),
                  pl.BlockSpec((POOL_GROUPS, POOL_CH, POOL_CH), lambda b: (0, 0, 0)),
                  pl.BlockSpec((1, POOL_W), lambda b: (0, 0))],
        out_specs=pl.BlockSpec((n_tok, POOL_W), lambda b: (b, 0)),
        out_shape=jax.ShapeDtypeStruct((t, POOL_W), BF16),
        compiler_params=_cparams("parallel"),
        name=f"pool_l{seq_len}",
    )(u, pool_w, pool_scale)


def _pack_rows(x):
    half = D_MODEL // 2
    lo = lax.bitcast_convert_type(x[:, :half].astype(BF16).astype(F32), U32) >> 16
    hi = lax.bitcast_convert_type(x[:, half:].astype(BF16).astype(F32), U32) & jnp.uint32(HIGH_HALF)
    word = lax.bitcast_convert_type(hi | lo, I32)
    return [word[:, c * LANES:(c + 1) * LANES] for c in range(N_PIECES)]


def _unpack_rows(pieces):
    words = [lax.bitcast_convert_type(p, U32) for p in pieces]
    lo = [lax.bitcast_convert_type(w << 16, F32) for w in words]
    hi = [lax.bitcast_convert_type(w & jnp.uint32(HIGH_HALF), F32) for w in words]
    return lo, hi


def _merge_kernel(x_ref, z_ref, p_ref, ga_ref, gb_ref, mod_ref, g2_ref, wr_ref, wp_ref, wo_ref, rw_ref, bias_ref,
                  x1_ref, *refs):
    piece_refs, route_refs = refs[:N_PIECES], refs[N_PIECES:]
    subs = [slice(s * MXU_DIM, (s + 1) * MXU_DIM) for s in range(x_ref.shape[0] // MXU_DIM)]
    branches = [(_dot(z_ref[r, :], wr_ref[...]), _dot(p_ref[r, :], wp_ref[...])) for r in subs]
    merged = [(jax.nn.sigmoid(ga_ref[r, :].astype(F32)) * y_ret
               + jax.nn.sigmoid(gb_ref[r, :].astype(F32)) * y_pool).astype(BF16)
              for r, (y_ret, y_pool) in zip(subs, branches)]
    outs = [_dot(m, wo_ref[...]) for m in merged]
    logits = []
    for r, o in zip(subs, outs):
        x1 = x_ref[r, :] + mod_ref[0, 2:3, :] * o
        x1_ref[r, :] = x1
        h2 = _rms_mod(x1, g2_ref[...], mod_ref[0, 4:5, :], mod_ref[0, 3:4, :])
        logits.append(lax.dot_general(rw_ref[...], h2.astype(BF16), (((1,), (1,)), ((), ())),
                                      preferred_element_type=F32)[:N_EXPERTS])
        for ref, piece in zip(piece_refs, _pack_rows(h2)):
            ref[r, :] = piece
    _route_tile(jnp.concatenate(logits, axis=1), bias_ref, *route_refs)


def _merge(x, z, p, ga, gb, mods, norm2_g, w_br_ret, w_br_pool, w_out, router_wt, bias_col, *, tokens_per_mod, tm):
    t = x.shape[0]
    tiles_per_mod = tokens_per_mod // tm
    row = lambda w: pl.BlockSpec((tm, w), lambda i: (i, 0))
    full = lambda a: pl.BlockSpec(a.shape, lambda i: (0,) * len(a.shape))
    krow = pl.BlockSpec((TOP_K, tm), lambda i: (0, i))
    counts = jax.ShapeDtypeStruct((LANES, LANES), F32)
    outs = pl.pallas_call(
        _merge_kernel,
        grid=(t // tm,),
        in_specs=[row(D_MODEL), row(RET_V_W), row(POOL_W), row(D_MODEL), row(D_MODEL),
                  pl.BlockSpec((1, 6, D_MODEL), lambda i: (i // tiles_per_mod, 0, 0)),
                  full(norm2_g), full(w_br_ret), full(w_br_pool), full(w_out), full(router_wt), full(bias_col)],
        out_specs=[row(D_MODEL)] + [row(LANES)] * N_PIECES + [krow, krow, krow, full(counts)],
        out_shape=[jax.ShapeDtypeStruct((t, D_MODEL), F32)] + [jax.ShapeDtypeStruct((t, LANES), I32)] * N_PIECES
        + [jax.ShapeDtypeStruct((TOP_K, t), I32), jax.ShapeDtypeStruct((TOP_K, t), I32),
           jax.ShapeDtypeStruct((TOP_K, t), F32), counts],
        scratch_shapes=[pltpu.VMEM(counts.shape, F32)],
        compiler_params=_cparams("arbitrary"),
        name="merge",
    )(x, z, p, ga, gb, mods, norm2_g, w_br_ret, w_br_pool, w_out, router_wt, bias_col)
    return outs[0], outs[1:1 + N_PIECES], outs[1 + N_PIECES:]


def _route_tile(logits, bias_ref, idx_ref, rank_ref, wk_ref, cnt_ref, carry_ref):
    e = N_EXPERTS
    tm = logits.shape[1]
    neg = -jnp.inf

    @pl.when(pl.program_id(0) == 0)
    def _():
        carry_ref[...] = jnp.zeros(carry_ref.shape, F32)

    scores = jax.nn.sigmoid(logits)
    sel = scores + bias_ref[:e, 0:1]
    e_idx = lax.broadcasted_iota(I32, (e, tm), 0)

    grp = sel.reshape(N_EXPERT_GROUPS, GROUP_SIZE, tm)
    m_idx = lax.broadcasted_iota(I32, grp.shape, 1)
    m1 = jnp.max(grp, axis=1, keepdims=True)
    first = jnp.min(jnp.where(grp == m1, m_idx, GROUP_SIZE), axis=1, keepdims=True)
    m2 = jnp.max(jnp.where(m_idx == first, neg, grp), axis=1, keepdims=True)
    gscore = (m1 + m2).reshape(N_EXPERT_GROUPS, tm)

    g_idx = lax.broadcasted_iota(I32, gscore.shape, 0)
    grank = jnp.zeros(gscore.shape, I32)
    for g in range(N_EXPERT_GROUPS):
        other = gscore[g:g + 1, :]
        beats = jnp.where(other > gscore, 1, jnp.where(other == gscore, (g_idx > g).astype(I32), 0))
        grank = grank + beats
    gkeep = (grank < TOPK_GROUPS).astype(F32)
    ekeep = jnp.broadcast_to(gkeep.reshape(N_EXPERT_GROUPS, 1, tm), grp.shape).reshape(e, tm)
    masked = jnp.where(ekeep > 0, sel, neg)

    chosen = jnp.zeros((e, tm), F32)
    picks, hits = [], []
    for _ in range(TOP_K):
        m = jnp.max(masked, axis=0, keepdims=True)
        pick = jnp.min(jnp.where(masked == m, e_idx, e), axis=0, keepdims=True)
        hit = e_idx == pick
        chosen = jnp.where(hit, 1.0, chosen)
        masked = jnp.where(hit, neg, masked)
        picks.append(pick)
        hits.append(hit)

    w = scores * chosen
    comb = w / jnp.sum(w, axis=0, keepdims=True) * ROUTED_SCALE

    t_row = lax.broadcasted_iota(I32, (tm, tm), 0)
    t_col = lax.broadcasted_iota(I32, (tm, tm), 1)
    before = (t_row < t_col).astype(BF16)
    rankmat = _dot(chosen.astype(BF16), before) + carry_ref[:e, 0:1]
    carry_ref[:e, :] = carry_ref[:e, :] + jnp.sum(chosen, axis=1, keepdims=True)
    cnt_ref[...] = carry_ref[...]

    idx_ref[...] = jnp.concatenate(picks, axis=0)
    rank_ref[...] = jnp.concatenate(
        [jnp.sum(jnp.where(h, rankmat, 0.0), axis=0, keepdims=True) for h in hits], axis=0).astype(I32)
    wk_ref[...] = jnp.concatenate(
        [jnp.sum(jnp.where(h, comb, 0.0), axis=0, keepdims=True) for h in hits], axis=0)


def _plan_kernel(idx_ref, rank_ref, cnt_ref, pos_ref, te_ref, nv_ref, nu_ref, *, group_tile):
    tf = idx_ref.shape[1]
    nt = te_ref.shape[1]
    cnt = cnt_ref[...].astype(I32)
    padded = (((cnt + (group_tile - 1)) // group_tile) * group_tile).astype(F32)
    e_sub = lax.broadcasted_iota(I32, (LANES, LANES), 0)
    e_lane = lax.broadcasted_iota(I32, (LANES, LANES), 1)
    base = jnp.sum(jnp.where(e_lane < e_sub, padded.T, 0.0), axis=1, keepdims=True)
    end = base + padded[:, 0:1]

    idx = idx_ref[...]
    start = jnp.zeros(idx.shape, F32)
    for e in range(N_EXPERTS):
        start = jnp.where(idx == e, base[e:e + 1, 0:1], start)
    pos = start.astype(I32) + rank_ref[...]
    for j in range(tf // SC_CHUNK):
        pos_ref[j] = pos[:, j * SC_CHUNK:(j + 1) * SC_CHUNK]

    tile_start = (lax.broadcasted_iota(I32, (N_EXPERTS, nt), 1) * group_tile).astype(F32)
    done = jnp.sum(jnp.where(end[:N_EXPERTS] <= tile_start, 1.0, 0.0), axis=0, keepdims=True)
    te_ref[...] = jnp.minimum(done, N_EXPERTS - 1.0).astype(I32)
    in_group = (base[:N_EXPERTS] <= tile_start) & (tile_start < end[:N_EXPERTS])
    real = jnp.clip(base[:N_EXPERTS] + cnt[:N_EXPERTS, 0:1].astype(F32) - tile_start, 0.0, float(group_tile))
    nv_ref[...] = jnp.sum(jnp.where(in_group, real, 0.0), axis=0, keepdims=True).astype(I32)
    total = jnp.sum(padded[:, 0:1], axis=0, keepdims=True)
    nu_ref[...] = jnp.broadcast_to(total * (1.0 / group_tile), nu_ref.shape).astype(I32)


def _plan(idx, rank, counts, *, n_tiles, tf, group_tile):
    t = idx.shape[1]
    nt_pad = -(-n_tiles // LANES) * LANES
    krow = pl.BlockSpec((TOP_K, tf), lambda i: (0, i))
    return pl.pallas_call(
        functools.partial(_plan_kernel, group_tile=group_tile),
        grid=(t // tf,),
        in_specs=[krow, krow, pl.BlockSpec((LANES, LANES), lambda i: (0, 0))],
        out_specs=[pl.BlockSpec((tf // SC_CHUNK, TOP_K, SC_CHUNK), lambda i: (i, 0, 0)),
                   pl.BlockSpec((1, nt_pad), lambda i: (0, 0)),
                   pl.BlockSpec((1, nt_pad), lambda i: (0, 0)),
                   pl.BlockSpec((1, LANES), lambda i: (0, 0))],
        out_shape=[jax.ShapeDtypeStruct((t // SC_CHUNK, TOP_K, SC_CHUNK), I32),
                   jax.ShapeDtypeStruct((1, nt_pad), I32), jax.ShapeDtypeStruct((1, nt_pad), I32),
                   jax.ShapeDtypeStruct((1, LANES), I32)],
        compiler_params=_cparams("arbitrary"),
        name="moe_plan",
    )(idx, rank, counts)


def _sc_mesh_info():
    info = plsc.get_sparse_core_info()
    mesh = plsc.VectorSubcoreMesh(core_axis_name="c", subcore_axis_name="s")
    return mesh, info.num_cores, info.num_cores * info.num_subcores


def _sc_dispatch(pieces, pos, *, n_rows):
    t = pieces[0].shape[0]
    mesh, n_cores, n_workers = _sc_mesh_info()
    per_w = t // SC_CHUNK // n_workers

    @functools.partial(
        pl.kernel, mesh=mesh,
        out_type=[jax.ShapeDtypeStruct((n_rows, LANES), I32)] * N_PIECES,
        scratch_types=[pltpu.VMEM((TOP_K, SC_CHUNK), I32),
                       pltpu.VMEM((N_PIECES, SC_CHUNK, LANES), I32),
                       pltpu.SemaphoreType.DMA((N_PIECES,)),
                       pltpu.SemaphoreType.DMA],
        name="sc_dispatch",
    )
    def run(*refs):
        src = refs[:N_PIECES]
        pos_hbm = refs[N_PIECES]
        dst = refs[N_PIECES + 1:2 * N_PIECES + 1]
        idx_v, rows_v, load_sem, put_sem = refs[2 * N_PIECES + 1:]
        wid = lax.axis_index("s") * n_cores + lax.axis_index("c")

        @pl.loop(0, per_w)
        def _(j):
            ch = wid * per_w + j
            t0 = pl.multiple_of(ch * SC_CHUNK, SC_CHUNK)
            loads = [pltpu.make_async_copy(src[c].at[pl.ds(t0, SC_CHUNK)], rows_v.at[c], load_sem.at[c])
                     for c in range(N_PIECES)]
            for ld in loads:
                ld.start()
            pltpu.sync_copy(pos_hbm.at[ch], idx_v)
            puts = []
            for c in range(N_PIECES):
                loads[c].wait()
                for k in range(TOP_K):
                    puts.append(pltpu.make_async_copy(rows_v.at[c], dst[c].at[idx_v.at[k]], put_sem))
                    puts[-1].start()
            for cp in puts:
                cp.wait()

    return run(*pieces, pos)


SC_GROUP = 32


def _sc_combine(pieces, pos, wts, *, n_tokens):
    mesh, n_cores, n_workers = _sc_mesh_info()
    lanes = plsc.get_sparse_core_info().num_lanes
    per_w = n_tokens // SC_GROUP // n_workers

    @functools.partial(
        pl.kernel, mesh=mesh,
        out_type=[jax.ShapeDtypeStruct((n_tokens, LANES), I32)] * N_PIECES,
        scratch_types=[pltpu.VMEM((TOP_K, SC_GROUP), I32),
                       pltpu.VMEM((TOP_K, SC_GROUP), F32),
                       pltpu.VMEM((2, TOP_K, SC_GROUP, LANES), I32),
                       pltpu.VMEM((2, SC_GROUP, LANES), I32),
                       pltpu.SemaphoreType.DMA((2,)),
                       pltpu.SemaphoreType.DMA((2,))],
        compiler_params=pltpu.CompilerParams(needs_layout_passes=False),
        name="sc_combine",
    )
    def run(*refs):
        src = refs[:N_PIECES]
        pos_hbm, wts_hbm = refs[N_PIECES:N_PIECES + 2]
        dst = refs[N_PIECES + 2:2 * N_PIECES + 2]
        idx_v, w_v, buf, acc, get_sem, put_sem = refs[2 * N_PIECES + 2:]
        wid = lax.axis_index("s") * n_cores + lax.axis_index("c")

        @pl.loop(0, per_w)
        def _(j):
            grp = wid * per_w + j
            t0 = pl.multiple_of(grp * SC_GROUP, SC_GROUP)
            pltpu.sync_copy(pos_hbm.at[grp], idx_v)
            pltpu.sync_copy(wts_hbm.at[grp], w_v)

            def gets(c, slot):
                return [pltpu.make_async_copy(src[c].at[idx_v.at[k]], buf.at[slot, k], get_sem.at[slot])
                        for k in range(TOP_K)]

            def puts(c, slot):
                return [pltpu.make_async_copy(acc.at[slot], dst[c].at[pl.ds(t0, SC_GROUP)], put_sem.at[slot])]

            for cp in gets(0, 0):
                cp.start()
            for c in range(N_PIECES):
                slot = c % 2
                if c + 1 < N_PIECES:
                    for cp in gets(c + 1, 1 - slot):
                        cp.start()
                for cp in gets(c, slot):
                    cp.wait()
                if c >= 2:
                    for cp in puts(c - 2, slot):
                        cp.wait()

                @pl.loop(0, SC_GROUP)
                def _(r):
                    row = jnp.full((lanes,), r, I32)
                    w = []
                    for k in range(TOP_K):
                        w_k = plsc.load_gather(w_v, [jnp.full((lanes,), k, I32), row])
                        w.append(plsc.pack(w_k, w_k, format=plsc.PackFormat.INTERLEAVED))
                    for q in range(LANES // lanes):
                        cols = pl.ds(q * lanes, lanes)
                        total = None
                        for k in range(TOP_K):
                            term = plsc.bitcast(buf[slot, k, r, cols], BF16) * w[k]
                            total = term if total is None else total + term
                        acc[slot, r, cols] = plsc.bitcast(total, I32)

                for cp in puts(c, slot):
                    cp.start()
            for c in range(N_PIECES - 2, N_PIECES):
                for cp in puts(c, c % 2):
                    cp.wait()

    return run(*pieces, pos, wts)


def _group_tile(n_tokens):
    per_expert = n_tokens * TOP_K // N_EXPERTS
    return max(MXU_DIM, min(4 * MXU_DIM, per_expert // MXU_DIM * MXU_DIM))


SECOND_DMA_QUEUE = 1
X_RING = 4


def _experts_kernel(te_ref, nv_ref, nu_ref, *refs, tile):
    x_hbm = refs[:N_PIECES]
    wg_ref, wu_ref, wd_ref = refs[N_PIECES:N_PIECES + 3]
    y_hbm = refs[N_PIECES + 3:2 * N_PIECES + 3]
    xbuf, xsem, ybuf, ysem = refs[2 * N_PIECES + 3:]
    i = pl.program_id(0)
    n_used = nu_ref[0]
    slot = lax.rem(i, X_RING)

    half = tile // 2

    def x_copy(step, into, h, c):
        r = pl.ds(pl.multiple_of(step * tile + h * half, half), half)
        return pltpu.make_async_copy(x_hbm[c].at[r], xbuf.at[into, c, pl.ds(h * half, half)], xsem.at[into, c])

    def y_copy(step, out_of, h, c):
        r = pl.ds(pl.multiple_of(step * tile + h * half, half), half)
        return pltpu.make_async_copy(ybuf.at[out_of, c, pl.ds(h * half, half)], y_hbm[c].at[r], ysem.at[out_of, c])

    def real_halves(copy, step, buf, act):
        for c in range(N_PIECES):
            act(copy(step, buf, 0, c))

        @pl.when(nv_ref[step] > half)
        def _():
            for c in range(N_PIECES):
                act(copy(step, buf, 1, c))

    start = lambda cp: cp.start(priority=SECOND_DMA_QUEUE)
    wait = lambda cp: cp.wait()

    @pl.when(i == 0)
    def _():
        real_halves(x_copy, 0, 0, start)
        for ahead in range(1, X_RING - 1):
            @pl.when(ahead < n_used)
            def _():
                real_halves(x_copy, ahead, ahead, start)

    @pl.when(i + X_RING - 1 < n_used)
    def _():
        real_halves(x_copy, i + X_RING - 1, lax.rem(i + X_RING - 1, X_RING), start)

    @pl.when(i < n_used)
    def _():
        real_halves(x_copy, i, slot, wait)
        subs = [slice(s * MXU_DIM, (s + 1) * MXU_DIM) for s in range(tile // MXU_DIM)]
        xs = []
        for rows in subs:
            lo, hi = _unpack_rows([xbuf[slot, c, rows, :] for c in range(N_PIECES)])
            xs.append(jnp.concatenate(lo + hi, axis=1).astype(BF16))
        gates = [(_dot(x, wg_ref[...]), _dot(x, wu_ref[...])) for x in xs]
        ys = [_dot((_silu(g) * u).astype(BF16), wd_ref[...]) for g, u in gates]
        for rows, y in zip(subs, ys):
            for c, piece in enumerate(_pack_rows(y)):
                ybuf[slot, c, rows, :] = piece

        behind = X_RING - 1

        @pl.when(i >= behind)
        def _():
            real_halves(y_copy, i - behind, lax.rem(i - behind, X_RING), wait)

        real_halves(y_copy, i, slot, start)

        @pl.when(i == n_used - 1)
        def _():
            for back in range(behind - 1, -1, -1):
                @pl.when(i >= back)
                def _():
                    real_halves(y_copy, i - back, lax.rem(i - back, X_RING), wait)


def _experts(x_pieces, tile_expert, tile_rows, n_used, wg, wu, wd, *, group_tile):
    n_rows = x_pieces[0].shape[0]
    n_tiles = n_rows // group_tile

    wspec = lambda a: pl.BlockSpec((None,) + a.shape[1:],
                                   lambda i, te, nv, nu: (te[jnp.minimum(i, nu[0] - 1)], 0, 0))
    return pl.pallas_call(
        functools.partial(_experts_kernel, tile=group_tile),
        grid_spec=pltpu.PrefetchScalarGridSpec(
            num_scalar_prefetch=3,
            grid=(n_tiles,),
            in_specs=[pl.BlockSpec(memory_space=pl.ANY)] * N_PIECES + [wspec(wg), wspec(wu), wspec(wd)],
            out_specs=[pl.BlockSpec(memory_space=pl.ANY)] * N_PIECES,
            scratch_shapes=[pltpu.VMEM((X_RING, N_PIECES, group_tile, LANES), I32),
                            pltpu.SemaphoreType.DMA((X_RING, N_PIECES))] * 2),
        out_shape=[jax.ShapeDtypeStruct((n_rows, LANES), I32)] * N_PIECES,
        compiler_params=_cparams("arbitrary"),
        name="experts",
    )(tile_expert, tile_rows, n_used, *x_pieces, wg, wu, wd)


def _moe_out_kernel(x1_ref, mod_ref, fg_ref, sg_ref, su_ref, sd_ref, *refs):
    h_refs = refs[:N_PIECES]
    routed_refs = refs[N_PIECES:2 * N_PIECES]
    out_ref = refs[2 * N_PIECES]
    h_lo, h_hi = _unpack_rows([r[...] for r in h_refs])
    h = jnp.concatenate(h_lo + h_hi, axis=1).astype(BF16)
    hid = _silu(_dot(h, sg_ref[...])) * _dot(h, su_ref[...])
    shared = _dot(hid.astype(BF16), sd_ref[...])
    r_lo, r_hi = _unpack_rows([r[...] for r in routed_refs])
    routed = jnp.concatenate(r_lo + r_hi, axis=1)
    x2 = x1_ref[...] + mod_ref[0, 5:6, :] * (shared + routed)
    out_ref[...] = x2 * lax.rsqrt(jnp.mean(x2 * x2, axis=-1, keepdims=True) + EPS) * fg_ref[...]


def _moe_out(h2_pieces, x1, mods, final_g, sg, su, sd, routed_pieces, *, tokens_per_mod, tm):
    t = x1.shape[0]
    tiles_per_mod = tokens_per_mod // tm
    row = lambda w: pl.BlockSpec((tm, w), lambda i: (i, 0))
    full = lambda a: pl.BlockSpec(a.shape, lambda i: (0,) * a.ndim)
    return pl.pallas_call(
        _moe_out_kernel,
        grid=(t // tm,),
        in_specs=[row(D_MODEL),
                  pl.BlockSpec((1, 6, D_MODEL), lambda i: (i // tiles_per_mod, 0, 0)),
                  full(final_g), full(sg), full(su), full(sd)]
        + [row(LANES)] * (2 * N_PIECES),
        out_specs=row(D_MODEL),
        out_shape=jax.ShapeDtypeStruct((t, D_MODEL), F32),
        compiler_params=_cparams("parallel"),
        name="moe_out",
    )(x1, mods, final_g, sg, su, sd, *h2_pieces, *routed_pieces)


def _trunk(x, mods, s0, w, expert_w, *, batch, seq_len, on_grid):
    t = batch * seq_len
    tokens_per_mod = t // mods.shape[0]
    cos_t, sin_t = _rope_tables(max(seq_len, PROJ_TILE))
    to_cast = expert_w if expert_w[0].dtype != BF16 else ()
    (q, k, v, gsw, up, ga, gb), casted = _inproj(x, mods, w["norm1_g"], w["w_in"], cos_t, sin_t, to_cast,
                                                 tokens_per_mod=tokens_per_mod, seq_len=seq_len,
                                                 on_grid=on_grid, tm=PROJ_TILE)
    expert_w = casted or expert_w
    z, s_f, s_b = _retention(q, k, v, gsw, w["dec"], s0, batch=batch, seq_len=seq_len)
    p = _pool(up, w["pool_w"], w["pool_scale"], batch=batch, seq_len=seq_len, on_grid=on_grid)
    x1, h2_pieces, (idx, rank, wts, counts) = _merge(
        x, z, p, ga, gb, mods, w["norm2_g"], w["w_br_ret"], w["w_br_pool"], w["w_out"],
        w["router_wt"], w["router_bias"], tokens_per_mod=tokens_per_mod, tm=PROJ_TILE)

    group_tile = _group_tile(t)
    n_rows = t * TOP_K + N_EXPERTS * group_tile
    pos, tile_expert, tile_rows, n_used = _plan(idx, rank, counts, n_tiles=n_rows // group_tile, tf=PLAN_TILE,
                                                group_tile=group_tile)
    x_sorted = _sc_dispatch(h2_pieces, pos, n_rows=n_rows)
    y_sorted = _experts(x_sorted, tile_expert.reshape(-1), tile_rows.reshape(-1), n_used.reshape(-1),
                        *expert_w, group_tile=group_tile)
    regroup = lambda a: a.reshape(TOP_K, t // SC_GROUP, SC_GROUP).transpose(1, 0, 2)
    pos_rows = pos.transpose(1, 0, 2).reshape(TOP_K, t)
    routed = _sc_combine(y_sorted, regroup(pos_rows), regroup(wts), n_tokens=t)
    y = _moe_out(h2_pieces, x1, mods, w["final_g"], w["sh_w_gate"], w["sh_w_up"], w["sh_w_down"], routed,
                 tokens_per_mod=tokens_per_mod, tm=OUT_TILE)
    return y, s_f, s_b, expert_w


def kernel(x_prompt, x_sample, state_ret_fwd, state_ret_bwd, c, c_ctx, ada_w, ada_b, norm1_g, norm2_g, w_in,
           ret_decay_fwd, ret_decay_bwd, w_br_ret, pool_w, pool_scale, w_br_pool, w_out, router_w, router_bias,
           exp_w_gate, exp_w_up, exp_w_down, sh_w_gate, sh_w_up, sh_w_down, final_norm_g):
    n_req, seq, d = x_prompt.shape
    n_dec, dec_seq, _ = x_sample.shape
    depth = ada_w.shape[0]
    assert depth == 1 and d == D_MODEL

    xc = x_prompt.reshape(n_req * seq, d)
    xs = x_sample.reshape(n_dec * dec_seq, d)
    new_f, new_b = [], []
    for l in range(depth):
        c_rows = jnp.concatenate([c_ctx[None, :], c, jnp.zeros((8 - 1 - n_dec, d), F32)], axis=0)
        mods = _ada(c_rows, ada_w[l], ada_b[l]).reshape(8, 6, d)
        pad_rows = LANES - N_EXPERTS
        w = dict(
            norm1_g=norm1_g[l].reshape(1, d), norm2_g=norm2_g[l].reshape(1, d),
            final_g=final_norm_g.reshape(1, d),
            w_in=w_in[l].astype(BF16),
            dec=jnp.stack([ret_decay_fwd[l], ret_decay_bwd[l]]).astype(F32),
            w_br_ret=w_br_ret[l].astype(BF16), pool_w=pool_w[l].astype(BF16),
            pool_scale=pool_scale[l].reshape(1, POOL_W), w_br_pool=w_br_pool[l].astype(BF16),
            w_out=w_out[l].astype(BF16),
            router_wt=jnp.pad(router_w[l].T, ((0, pad_rows), (0, 0))).astype(BF16),
            router_bias=jnp.pad(router_bias[l].astype(F32).reshape(N_EXPERTS, 1), ((0, pad_rows), (0, 0))),
            sh_w_gate=sh_w_gate[l].astype(BF16),
            sh_w_up=sh_w_up[l].astype(BF16), sh_w_down=sh_w_down[l].astype(BF16),
        )
        cached = (state_ret_fwd[:, l].astype(F32), state_ret_bwd[:, l].astype(F32))
        expert_w = (exp_w_gate[l], exp_w_up[l], exp_w_down[l])
        xs, _, _, expert_w = _trunk(xs, mods[1:1 + n_dec], cached, w, expert_w,
                                    batch=n_dec, seq_len=dec_seq, on_grid=True)
        xc, s_f, s_b, _ = _trunk(xc, mods[0:1], None, w, expert_w, batch=n_req, seq_len=seq, on_grid=False)
        new_f.append(s_f)
        new_b.append(s_b)
    y_prompt = xc.reshape(n_req, seq, d)
    y_sample = xs.reshape(n_dec, dec_seq, d)
    return (y_prompt, y_sample, jnp.stack(new_f, axis=1).astype(x_prompt.dtype),
            jnp.stack(new_b, axis=1).astype(x_prompt.dtype))
```
